```python
import jax, jax.numpy as jnp
from jax import lax
import numpy as np

D_MODEL = 1024
BATCH = 8
SEQ = 4096
DEPTH = 1

D_MIX = D_MODEL
D_LRU = D_MIX // 2
LRU_BLOCKS = 8
LRU_BD = D_LRU // LRU_BLOCKS
CONV_W = 4
LRU_C = 8.0
N_HEADS = 8
QK_NOPE = 64
QK_ROPE = 32
QK_HEAD = QK_NOPE + QK_ROPE
V_HEAD = (D_MIX - D_LRU) // N_HEADS
Q_LORA = 256
KV_LORA = 128
ROPE_THETA = 10000.0
Q_BLOCK = 128
D_IN = 2 * D_LRU + Q_LORA + KV_LORA + QK_ROPE
N_EXPERTS = 32
TOP_K = 4
D_FF = D_MODEL
SWIGLU_LIMIT = 7.0
SWIGLU_ALPHA = 1.702
MOE_BLOCK = 512
EPS = 1e-6

kernel_name = "hymba_lru_mla_moe_adaln"


def rms_norm(x, g):
    x32 = x.astype(jnp.float32)
    y = x32 * lax.rsqrt(jnp.mean(x32 * x32, axis=-1, keepdims=True) + EPS)
    return (y * g.astype(jnp.float32)).astype(x.dtype)


def modulate(h, shift, scale):
    return h * (1.0 + scale[:, None, :]) + shift[:, None, :]


def apply_rope(x, pos):
    half = x.shape[-1] // 2
    freqs = ROPE_THETA ** (-jnp.arange(half, dtype=jnp.float32) / half)
    ang = pos.astype(jnp.float32)[..., None] * freqs
    cos = jnp.cos(ang)[:, :, None, :]
    sin = jnp.sin(ang)[:, :, None, :]
    x32 = x.astype(jnp.float32)
    x1, x2 = x32[..., :half], x32[..., half:]
    return jnp.concatenate([x1 * cos - x2 * sin, x2 * cos + x1 * sin], axis=-1).astype(x.dtype)


def causal_depthwise_conv(x, w, b):
    S = x.shape[1]
    xp = jnp.pad(x, ((0, 0), (CONV_W - 1, 0), (0, 0)))
    y = xp[:, 0:S] * w[0]
    for j in range(1, CONV_W):
        y = y + xp[:, j:j + S] * w[j]
    return y + b


def _lin_comb(left, right):
    a1, b1 = left
    a2, b2 = right
    return a1 * a2, a2 * b1 + b2


def rg_lru(x, w_a, b_a, w_x, b_x, lam):
    B, S, C = x.shape
    xb = x.reshape(B, S, LRU_BLOCKS, LRU_BD)
    r = jax.nn.sigmoid(jnp.einsum('bsnc,ncd->bsnd', xb, w_a).reshape(B, S, C) + b_a)
    i = jax.nn.sigmoid(jnp.einsum('bsnc,ncd->bsnd', xb, w_x).reshape(B, S, C) + b_x)
    log_a = -LRU_C * r.astype(jnp.float32) * jax.nn.softplus(-lam.astype(jnp.float32))
    a = jnp.exp(log_a)
    mult = jnp.sqrt(-jnp.expm1(2.0 * log_a))
    u = mult * (i * x).astype(jnp.float32)
    _, h = lax.associative_scan(_lin_comb, (a, u), axis=1)
    return h.astype(x.dtype)


def causal_block_attention(q, k, v):
    B, S, H, Dk = q.shape
    nb = S // Q_BLOCK
    qb = q.reshape(B, nb, Q_BLOCK, H, Dk).transpose(1, 0, 2, 3, 4)
    kpos = jnp.arange(S, dtype=jnp.int32)
    scale = QK_HEAD ** -0.5

    def one_block(args):
        qblk, bi = args
        s = jnp.einsum('bqhd,bkhd->bhqk', qblk, k, preferred_element_type=jnp.float32) * scale
        qpos = bi * Q_BLOCK + jnp.arange(Q_BLOCK, dtype=jnp.int32)
        mask = kpos[None, :] <= qpos[:, None]
        s = jnp.where(mask[None, None], s, jnp.float32(-1e30))
        p = jax.nn.softmax(s, axis=-1).astype(v.dtype)
        return jnp.einsum('bhqk,bkhd->bqhd', p, v)

    o = lax.map(one_block, (qb, jnp.arange(nb, dtype=jnp.int32)))
    return o.transpose(1, 0, 2, 3, 4).reshape(B, S, H * v.shape[-1])


def moe_ffn(h, w_router, b_router, w1, b1, w2, b2):
    Bn, S, D = h.shape
    T = Bn * S
    A = T * TOP_K
    hf = h.reshape(T, D)
    logits = (hf @ w_router + b_router).astype(jnp.float32)
    top_vals, top_idx = lax.top_k(logits, TOP_K)
    gates = jax.nn.softmax(top_vals, axis=-1).astype(h.dtype)
    flat_e = top_idx.reshape(A).astype(jnp.int32)
    flat_g = gates.reshape(A)
    flat_tok = jnp.arange(A, dtype=jnp.int32) // TOP_K
    order = jnp.argsort(flat_e)
    se, stok, sg = flat_e[order], flat_tok[order], flat_g[order]
    counts = jnp.zeros((N_EXPERTS,), jnp.int32).at[flat_e].add(1)
    group_start = jnp.cumsum(counts) - counts
    padded = (counts + MOE_BLOCK - 1) // MOE_BLOCK * MOE_BLOCK
    pad_end = jnp.cumsum(padded)
    pad_start = pad_end - padded
    dest = pad_start[se] + jnp.arange(A, dtype=jnp.int32) - group_start[se]
    n_blocks = -(-A // MOE_BLOCK) + N_EXPERTS
    P = n_blocks * MOE_BLOCK
    buf_tok = jnp.zeros((P,), jnp.int32).at[dest].set(stok)
    buf_g = jnp.zeros((P,), h.dtype).at[dest].set(sg)
    block_start = jnp.arange(n_blocks, dtype=jnp.int32) * MOE_BLOCK
    block_e = jnp.clip(jnp.searchsorted(pad_end, block_start, side='right'), 0, N_EXPERTS - 1)

    def body(acc, blk):
        tok, g, e = blk
        xb = hf[tok]
        gu = xb @ w1[e] + b1[e]
        glu = jnp.minimum(gu[:, :D_FF], SWIGLU_LIMIT)
        lin = jnp.clip(gu[:, D_FF:], -SWIGLU_LIMIT, SWIGLU_LIMIT)
        y = ((lin + 1.0) * (glu * jax.nn.sigmoid(SWIGLU_ALPHA * glu))) @ w2[e] + b2[e]
        return acc.at[tok].add(y * g[:, None]), None

    out, _ = lax.scan(body, jnp.zeros_like(hf),
                      (buf_tok.reshape(n_blocks, MOE_BLOCK), buf_g.reshape(n_blocks, MOE_BLOCK), block_e))
    return out.reshape(Bn, S, D)


def setup_inputs(seed: int = 0) -> dict:
    key = jax.random.key(seed)
    ks = jax.random.split(key, 32)
    f32 = jnp.float32
    L = DEPTH

    def nrm(k, shape, scale):
        return jax.random.normal(k, shape, f32) * scale

    def gain(k, shape):
        return 1.0 + 0.01 * jax.random.normal(k, shape, f32)

    x = jax.random.normal(ks[0], (BATCH, SEQ, D_MODEL), f32)
    c = jax.random.normal(ks[1], (BATCH, D_MODEL), f32)
    offs = jax.random.randint(ks[2], (BATCH, 1), 0, 1024, dtype=jnp.int32)
    positions = offs + jnp.arange(SEQ, dtype=jnp.int32)[None, :]
    u = jax.random.uniform(ks[3], (L, D_LRU), f32, 0.9, 0.999)
    s = u ** (1.0 / LRU_C)
    lam = jnp.log(s) - jnp.log1p(-s)
    return {
        "x": x, "c": c, "positions": positions,
        "w_ada": nrm(ks[4], (L, D_MODEL, 6 * D_MODEL), D_MODEL ** -0.5),
        "b_ada": nrm(ks[5], (L, 6 * D_MODEL), 0.01),
        "g_mix": gain(ks[6], (L, D_MODEL)),
        "w_in": nrm(ks[7], (L, D_MODEL, D_IN), D_MODEL ** -0.5),
        "conv_w": nrm(ks[8], (L, CONV_W, D_LRU), CONV_W ** -0.5),
        "conv_b": nrm(ks[9], (L, D_LRU), 0.01),
        "w_a": nrm(ks[10], (L, LRU_BLOCKS, LRU_BD, LRU_BD), LRU_BD ** -0.5),
        "b_a": nrm(ks[11], (L, D_LRU), 0.01),
        "w_x": nrm(ks[12], (L, LRU_BLOCKS, LRU_BD, LRU_BD), LRU_BD ** -0.5),
        "b_x": nrm(ks[13], (L, D_LRU), 0.01),
        "lam": lam,
        "g_q_lat": gain(ks[14], (L, Q_LORA)),
        "w_uq": nrm(ks[15], (L, Q_LORA, N_HEADS * QK_HEAD), Q_LORA ** -0.5),
        "g_kv_lat": gain(ks[16], (L, KV_LORA)),
        "w_ukv": nrm(ks[17], (L, KV_LORA, N_HEADS * (QK_NOPE + V_HEAD)), KV_LORA ** -0.5),
        "g_qn": gain(ks[18], (L, QK_HEAD)),
        "g_kn": gain(ks[19], (L, QK_HEAD)),
        "w_out": nrm(ks[20], (L, D_MIX, D_MODEL), D_MIX ** -0.5),
        "g_ffn": gain(ks[21], (L, D_MODEL)),
        "w_router": nrm(ks[22], (L, D_MODEL, N_EXPERTS), D_MODEL ** -0.5),
        "b_router": nrm(ks[23], (L, N_EXPERTS), 0.01),
        "w1": nrm(ks[24], (L, N_EXPERTS, D_MODEL, 2 * D_FF), D_MODEL ** -0.5),
        "b1": nrm(ks[25], (L, N_EXPERTS, 2 * D_FF), 0.01),
        "w2": nrm(ks[26], (L, N_EXPERTS, D_FF, D_MODEL), D_FF ** -0.5),
        "b2": nrm(ks[27], (L, N_EXPERTS, D_MODEL), 0.01),
    }


def reference(x, c, positions, w_ada, b_ada, g_mix, w_in, conv_w, conv_b, w_a, b_a, w_x, b_x, lam,
              g_q_lat, w_uq, g_kv_lat, w_ukv, g_qn, g_kn, w_out, g_ffn, w_router, b_router,
              w1, b1, w2, b2):
    B, S, D = x.shape
    o1 = 2 * D_LRU
    o2 = o1 + Q_LORA
    o3 = o2 + KV_LORA
    for l in range(DEPTH):
        mod = jax.nn.silu(c) @ w_ada[l] + b_ada[l]
        shift1, scale1, gate1, shift2, scale2, gate2 = jnp.split(mod, 6, axis=-1)

        h = modulate(rms_norm(x, g_mix[l]), shift1, scale1)
        z = h @ w_in[l]
        x_lru, y_lru = z[..., :D_LRU], z[..., D_LRU:o1]
        q_lat, kv_lat, k_rope = z[..., o1:o2], z[..., o2:o3], z[..., o3:]

        xc = causal_depthwise_conv(x_lru, conv_w[l], conv_b[l])
        hr = rg_lru(xc, w_a[l], b_a[l], w_x[l], b_x[l], lam[l])
        lru_out = jax.nn.gelu(y_lru) * hr

        q = (rms_norm(q_lat, g_q_lat[l]) @ w_uq[l]).reshape(B, S, N_HEADS, QK_HEAD)
        kv = (rms_norm(kv_lat, g_kv_lat[l]) @ w_ukv[l]).reshape(B, S, N_HEADS, QK_NOPE + V_HEAD)
        k_nope, v = kv[..., :QK_NOPE], kv[..., QK_NOPE:]
        k_r = jnp.broadcast_to(k_rope[:, :, None, :], (B, S, N_HEADS, QK_ROPE))
        k = jnp.concatenate([k_nope, k_r], axis=-1)
        q = rms_norm(q, g_qn[l])
        k = rms_norm(k, g_kn[l])
        q = jnp.concatenate([q[..., :QK_NOPE], apply_rope(q[..., QK_NOPE:], positions)], axis=-1)
        k = jnp.concatenate([k[..., :QK_NOPE], apply_rope(k[..., QK_NOPE:], positions)], axis=-1)
        attn_out = causal_block_attention(q, k, v)

        mix = jnp.concatenate([lru_out, attn_out], axis=-1) @ w_out[l]
        x = x + gate1[:, None, :] * mix

        h2 = modulate(rms_norm(x, g_ffn[l]), shift2, scale2)
        x = x + gate2[:, None, :] * moe_ffn(h2, w_router[l], b_router[l], w1[l], b1[l], w2[l], b2[l])
    return x
```

```python
import functools

import jax
import jax.numpy as jnp
from jax import lax
from jax.experimental import pallas as pl
from jax.experimental.pallas import tpu as pltpu

D_MODEL = 1024
D_LRU = 512
LRU_BLOCKS = 8
LRU_BD = 64
CONV_W = 4
LRU_C = 8.0
N_HEADS = 8
QK_NOPE = 64
QK_ROPE = 32
QK_HEAD = 96
V_HEAD = 64
Q_LORA = 256
KV_LORA = 128
ROPE_THETA = 10000.0
N_EXPERTS = 32
TOP_K = 4
D_FF = 1024
SWIGLU_LIMIT = 7.0
SWIGLU_ALPHA = 1.702
MOE_BLOCK = 512
EPS = 1e-6

LANES = 128
HEAD_PAD = 128
ROPE_LO = QK_NOPE
ROPE_HALF = QK_ROPE // 2
D_IN_PAD = 2 * D_LRU + Q_LORA + KV_LORA + LANES

VMEM_LIMIT = 56 * 1024 * 1024

F32 = jnp.float32
BF16 = jnp.bfloat16


def _cparams(sem):
    return pltpu.CompilerParams(dimension_semantics=sem, vmem_limit_bytes=VMEM_LIMIT)


def _dot(a, b):
    return jnp.dot(a, b, preferred_element_type=F32)


def _dot_nt(a, b):
    return lax.dot_general(a, b, (((1,), (1,)), ((), ())), preferred_element_type=F32)


def _split_bf16(a):
    hi = a.astype(BF16)
    lo = (a - hi.astype(F32)).astype(BF16)
    return hi, lo


def _sigmoid(x):
    return 1.0 / (1.0 + jnp.exp(-x))


def _ada_kernel(c_ref, whi_ref, wlo_ref, b_ref, o_ref):
    c = c_ref[...]
    s = c * _sigmoid(c)
    shi, slo = _split_bf16(s)
    whi = whi_ref[...]
    o_ref[...] = _dot(shi, whi) + _dot(slo, whi) + _dot(shi, wlo_ref[...]) + b_ref[...]


def _ada(c, w_ada, b_ada):
    B, D = c.shape
    N = w_ada.shape[1]
    tn = 1024
    whi, wlo = _split_bf16(w_ada)
    return pl.pallas_call(
        _ada_kernel,
        grid=(N // tn,),
        in_specs=[
            pl.BlockSpec((B, D), lambda j: (0, 0)),
            pl.BlockSpec((D, tn), lambda j: (0, j)),
            pl.BlockSpec((D, tn), lambda j: (0, j)),
            pl.BlockSpec((1, tn), lambda j: (0, j)),
        ],
        out_specs=pl.BlockSpec((B, tn), lambda j: (0, j)),
        out_shape=jax.ShapeDtypeStruct((B, N), F32),
        compiler_params=_cparams(("arbitrary",)),
        name="ada",
    )(c, whi, wlo, b_ada.reshape(1, N))


def _rope(xh, cos_t, sin_t, lane):
    fwd = pltpu.roll(xh, LANES - ROPE_HALF, axis=1)
    bwd = pltpu.roll(xh, ROPE_HALF, axis=1)
    rot = jnp.where(lane < ROPE_LO + ROPE_HALF, fwd, bwd)
    return xh * cos_t + rot * sin_t


def _inproj_kernel(x_ref, pos_ref, shift_ref, scale_ref, gmix_ref, win_ref, gq_ref, wuq_ref,
                   gkv_ref, wuk_ref, wuv_ref, gqn_ref, gkn_ref, freq_ref,
                   xlru_ref, ylru_ref, q_ref, k_ref, v_ref):
    x = x_ref[...]
    ms = jnp.mean(x * x, axis=-1, keepdims=True)
    xn = x * lax.rsqrt(ms + EPS) * gmix_ref[...]
    h = xn * (1.0 + scale_ref[0]) + shift_ref[0]
    z = _dot(h.astype(BF16), win_ref[...])
    xlru_ref[...] = z[:, :D_LRU]
    ylru_ref[...] = z[:, D_LRU:2 * D_LRU]
    o1 = 2 * D_LRU
    o2 = o1 + Q_LORA
    o3 = o2 + KV_LORA
    ql = z[:, o1:o2]
    kvl = z[:, o2:o3]
    kr = z[:, o3:]

    qn = ql * lax.rsqrt(jnp.mean(ql * ql, axis=-1, keepdims=True) + EPS) * gq_ref[...]
    q = _dot(qn.astype(BF16), wuq_ref[...])
    kvn = kvl * lax.rsqrt(jnp.mean(kvl * kvl, axis=-1, keepdims=True) + EPS) * gkv_ref[...]
    kvb = kvn.astype(BF16)
    kn = _dot(kvb, wuk_ref[...])
    v_ref[...] = _dot(kvb, wuv_ref[...]).astype(BF16)

    tm = x.shape[0]
    lane = lax.broadcasted_iota(jnp.int32, (tm, LANES), 1)
    in_rope = (lane >= ROPE_LO) & (lane < ROPE_LO + QK_ROPE)
    ang = pos_ref[...].astype(F32) * freq_ref[...]
    cos_t = jnp.where(in_rope, jnp.cos(ang), 1.0)
    sin_v = jnp.sin(ang)
    sin_t = jnp.where(in_rope, jnp.where(lane < ROPE_LO + ROPE_HALF, -sin_v, sin_v), 0.0)

    gqn = gqn_ref[...]
    gkn = gkn_ref[...]
    inv_w = 1.0 / QK_HEAD
    qscale = QK_HEAD ** -0.5
    for hh in range(N_HEADS):
        sl = slice(hh * HEAD_PAD, (hh + 1) * HEAD_PAD)
        qh = q[:, sl]
        qh = qh * lax.rsqrt(jnp.sum(qh * qh, axis=-1, keepdims=True) * inv_w + EPS) * gqn
        q_ref[:, sl] = (_rope(qh, cos_t, sin_t, lane) * qscale).astype(BF16)
        kh = kn[:, sl] + kr
        kh = kh * lax.rsqrt(jnp.sum(kh * kh, axis=-1, keepdims=True) * inv_w + EPS) * gkn
        k_ref[:, sl] = _rope(kh, cos_t, sin_t, lane).astype(BF16)


def _inproj(x2, pos2, mod3, g_mix, w_in_p, g_q_lat, w_uq_p, g_kv_lat, w_uk_p, w_uv, gqn_p, gkn_p,
            freq_lane, B, S, tm):
    T, D = x2.shape
    ns = S // tm
    HP = N_HEADS * HEAD_PAD
    row = lambda b, s: (b * ns + s, 0)
    full = lambda b, s: (0, 0)
    return pl.pallas_call(
        _inproj_kernel,
        grid=(B, ns),
        in_specs=[
            pl.BlockSpec((tm, D), row),
            pl.BlockSpec((tm, 1), row),
            pl.BlockSpec((1, 1, D), lambda b, s: (b * 6 + 0, 0, 0)),
            pl.BlockSpec((1, 1, D), lambda b, s: (b * 6 + 1, 0, 0)),
            pl.BlockSpec((1, D), full),
            pl.BlockSpec((D, D_IN_PAD), full),
            pl.BlockSpec((1, Q_LORA), full),
            pl.BlockSpec((Q_LORA, HP), full),
            pl.BlockSpec((1, KV_LORA), full),
            pl.BlockSpec((KV_LORA, HP), full),
            pl.BlockSpec((KV_LORA, N_HEADS * V_HEAD), full),
            pl.BlockSpec((1, HEAD_PAD), full),
            pl.BlockSpec((1, HEAD_PAD), full),
            pl.BlockSpec((1, LANES), full),
        ],
        out_specs=[
            pl.BlockSpec((tm, D_LRU), row),
            pl.BlockSpec((tm, D_LRU), row),
            pl.BlockSpec((tm, HP), row),
            pl.BlockSpec((tm, HP), row),
            pl.BlockSpec((tm, N_HEADS * V_HEAD), row),
        ],
        out_shape=[
            jax.ShapeDtypeStruct((T, D_LRU), F32),
            jax.ShapeDtypeStruct((T, D_LRU), F32),
            jax.ShapeDtypeStruct((T, HP), BF16),
            jax.ShapeDtypeStruct((T, HP), BF16),
            jax.ShapeDtypeStruct((T, N_HEADS * V_HEAD), BF16),
        ],
        compiler_params=_cparams(("arbitrary", "arbitrary")),
        name="inproj",
    )(x2, pos2, mod3, mod3, g_mix, w_in_p, g_q_lat, w_uq_p, g_kv_lat, w_uk_p, w_uv, gqn_p, gkn_p,
      freq_lane)


def _gelu_tanh(x):
    return 0.5 * x * (1.0 + jnp.tanh(0.7978845608028654 * (x + 0.044715 * x * x * x)))


def _lru_kernel(x_ref, y_ref, cw_ref, cb_ref, wa_ref, ba_ref, wx_ref, bx_ref, lam_ref,
                o_ref, tail_ref, carry_ref):
    s = pl.program_id(1)

    @pl.when(s == 0)
    def _():
        tail_ref[...] = jnp.zeros_like(tail_ref)
        carry_ref[...] = jnp.zeros_like(carry_ref)

    x = x_ref[...]
    ts = x.shape[0]
    xext = jnp.concatenate([tail_ref[...], x], axis=0)
    cw = cw_ref[...]
    xc = x * cw[CONV_W - 1:CONV_W, :]
    for j in range(CONV_W - 1):
        sh = CONV_W - 1 - j
        xc = xc + xext[8 - sh:8 - sh + ts, :] * cw[j:j + 1, :]
    xc = xc + cb_ref[...]
    tail_ref[...] = x[ts - 8:, :]

    xb = xc.astype(BF16)
    r = _sigmoid(_dot(xb, wa_ref[...]) + ba_ref[...])
    i = _sigmoid(_dot(xb, wx_ref[...]) + bx_ref[...])
    lam = lam_ref[...]
    nl = -lam
    softplus = jnp.maximum(nl, 0.0) + jnp.log(1.0 + jnp.exp(-jnp.abs(nl)))
    log_a = (-LRU_C) * r * softplus
    a = jnp.exp(log_a)
    mult = jnp.sqrt(1.0 - jnp.exp(2.0 * log_a))
    u = mult * (i * xc)

    rowi = lax.broadcasted_iota(jnp.int32, (ts, 1), 0)
    sh = 1
    while sh < ts:
        a_prev = pltpu.roll(a, sh, axis=0)
        u_prev = pltpu.roll(u, sh, axis=0)
        m = rowi >= sh
        u = jnp.where(m, a * u_prev + u, u)
        a = jnp.where(m, a * a_prev, a)
        sh *= 2
    hcar = carry_ref[0:1, :]
    hs = u + a * hcar
    carry_ref[...] = jnp.broadcast_to(hs[ts - 1:ts, :], carry_ref.shape)
    o_ref[...] = (_gelu_tanh(y_ref[...]) * hs).astype(BF16)


def _lru(xlru, ylru, conv_w, conv_b, wa_d, b_a, wx_d, b_x, lam, B, S, ts):
    T, C = xlru.shape
    ns = S // ts
    row = lambda b, s: (b * ns + s, 0)
    full = lambda b, s: (0, 0)
    return pl.pallas_call(
        _lru_kernel,
        grid=(B, ns),
        in_specs=[
            pl.BlockSpec((ts, C), row),
            pl.BlockSpec((ts, C), row),
            pl.BlockSpec((CONV_W, C), full),
            pl.BlockSpec((1, C), full),
            pl.BlockSpec((C, C), full),
            pl.BlockSpec((1, C), full),
            pl.BlockSpec((C, C), full),
            pl.BlockSpec((1, C), full),
            pl.BlockSpec((1, C), full),
        ],
        out_specs=pl.BlockSpec((ts, C), row),
        out_shape=jax.ShapeDtypeStruct((T, C), BF16),
        scratch_shapes=[pltpu.VMEM((8, C), F32), pltpu.VMEM((8, C), F32)],
        compiler_params=_cparams(("arbitrary", "arbitrary")),
        name="lru",
    )(xlru, ylru, conv_w, conv_b, wa_d, b_a, wx_d, b_x, lam)


NEG_INF = -1e30


def _attn_kernel(q_ref, k_ref, v_ref, o_ref, *, tq):
    qi = pl.program_id(1)
    rowi = lax.broadcasted_iota(jnp.int32, (tq, tq), 0)
    coli = lax.broadcasted_iota(jnp.int32, (tq, tq), 1)
    diag_mask = coli <= rowi

    for hh in range(N_HEADS):
        qs = slice(hh * HEAD_PAD, (hh + 1) * HEAD_PAD)
        vs = slice(hh * V_HEAD, (hh + 1) * V_HEAD)
        qh = q_ref[:, qs]

        def step(j, carry, masked):
            m, l, acc = carry
            r0 = pl.multiple_of(j * tq, tq)
            kh = k_ref[pl.ds(r0, tq), qs]
            vh = v_ref[pl.ds(r0, tq), vs]
            sc = _dot_nt(qh, kh)
            if masked:
                sc = jnp.where(diag_mask, sc, NEG_INF)
            m_new = jnp.maximum(m, jnp.max(sc, axis=-1, keepdims=True))
            alpha = jnp.exp(m - m_new)
            p = jnp.exp(sc - m_new)
            l_new = alpha * l + jnp.sum(p, axis=-1, keepdims=True)
            acc_new = alpha * acc + _dot(p.astype(BF16), vh)
            return m_new, l_new, acc_new

        init = (jnp.full((tq, 1), NEG_INF, F32), jnp.zeros((tq, 1), F32),
                jnp.zeros((tq, V_HEAD), F32))
        carry = lax.fori_loop(0, qi, functools.partial(step, masked=False), init)
        m, l, acc = step(qi, carry, True)
        o_ref[:, vs] = (acc / l).astype(BF16)


def _attn(qp, kp, v, B, S, tq):
    T = qp.shape[0]
    nq = S // tq
    HP = N_HEADS * HEAD_PAD
    HV = N_HEADS * V_HEAD
    return pl.pallas_call(
        functools.partial(_attn_kernel, tq=tq),
        grid=(B, nq),
        in_specs=[
            pl.BlockSpec((tq, HP), lambda b, i: (b * nq + i, 0)),
            pl.BlockSpec((S, HP), lambda b, i: (b, 0)),
            pl.BlockSpec((S, HV), lambda b, i: (b, 0)),
        ],
        out_specs=pl.BlockSpec((tq, HV), lambda b, i: (b * nq + i, 0)),
        out_shape=jax.ShapeDtypeStruct((T, HV), BF16),
        compiler_params=_cparams(("arbitrary", "arbitrary")),
        name="attn",
    )(qp, kp, v)


def _outproj_kernel(lru_ref, att_ref, x_ref, gate_ref, shift_ref, scale_ref, gffn_ref,
                    wo1_ref, wo2_ref, wrh_ref, wrl_ref, br_ref, tri_ref,
                    x1_ref, h2p_ref, idx_ref, gat_ref, rank_ref, cnt_ref, run_ref):
    first = (pl.program_id(0) == 0) & (pl.program_id(1) == 0)

    @pl.when(first)
    def _():
        run_ref[...] = jnp.zeros_like(run_ref)

    mix = _dot(lru_ref[...], wo1_ref[...]) + _dot(att_ref[...], wo2_ref[...])
    x1 = x_ref[...] + gate_ref[0] * mix
    x1_ref[...] = x1
    ms = jnp.mean(x1 * x1, axis=-1, keepdims=True)
    h2 = x1 * lax.rsqrt(ms + EPS) * gffn_ref[...]
    h2 = h2 * (1.0 + scale_ref[0]) + shift_ref[0]

    hhi = h2.astype(BF16)
    hhi32 = hhi.astype(F32)
    hlo = (h2 - hhi32).astype(BF16)
    bits = lax.bitcast_convert_type(hhi32, jnp.uint32)
    half = h2.shape[1] // 2
    h2p_ref[...] = (bits[:, :half] >> 16) | (bits[:, half:] & jnp.uint32(0xFFFF0000))

    wrh = wrh_ref[...]
    logits = _dot_nt(wrh, hhi) + _dot_nt(wrh, hlo) + _dot_nt(wrl_ref[...], hhi) + br_ref[...]
    ne, tm = logits.shape
    eio = lax.broadcasted_iota(jnp.int32, (ne, tm), 0)
    vals, idxs, sels = [], [], []
    l = logits
    for _ in range(TOP_K):
        m = jnp.max(l, axis=0, keepdims=True)
        idx = jnp.min(jnp.where(l == m, eio, ne), axis=0, keepdims=True)
        sel = eio == idx
        l = jnp.where(sel, -jnp.inf, l)
        vals.append(m)
        idxs.append(idx)
        sels.append(sel)
    es = [jnp.exp(v - vals[0]) for v in vals]
    den = es[0] + es[1] + es[2] + es[3]
    inv = 1.0 / den
    sel_any = jnp.where(sels[0] | sels[1] | sels[2] | sels[3], 1.0, 0.0)
    excl = _dot(sel_any.astype(BF16), tri_ref[...]) + run_ref[...]
    for kk in range(TOP_K):
        idx_ref[kk:kk + 1, :] = idxs[kk]
        gat_ref[kk:kk + 1, :] = es[kk] * inv
        rk = jnp.sum(jnp.where(sels[kk], excl, 0.0), axis=0, keepdims=True)
        rank_ref[kk:kk + 1, :] = rk.astype(jnp.int32)
    run = run_ref[...] + jnp.sum(sel_any, axis=1, keepdims=True)
    run_ref[...] = run
    cnt_ref[...] = run.astype(jnp.int32)


def _outproj(lru_o, att_o, x2, mod3, g_ffn, wo1, wo2, wr_hi, wr_lo, b_r, tri, B, S, tm):
    T, D = x2.shape
    ns = S // tm
    C = lru_o.shape[1]
    row = lambda b, s: (b * ns + s, 0)
    col = lambda b, s: (0, b * ns + s)
    full = lambda b, s: (0, 0)
    return pl.pallas_call(
        _outproj_kernel,
        grid=(B, ns),
        in_specs=[
            pl.BlockSpec((tm, C), row),
            pl.BlockSpec((tm, C), row),
            pl.BlockSpec((tm, D), row),
            pl.BlockSpec((1, 1, D), lambda b, s: (b * 6 + 2, 0, 0)),
            pl.BlockSpec((1, 1, D), lambda b, s: (b * 6 + 3, 0, 0)),
            pl.BlockSpec((1, 1, D), lambda b, s: (b * 6 + 4, 0, 0)),
            pl.BlockSpec((1, D), full),
            pl.BlockSpec((C, D), full),
            pl.BlockSpec((C, D), full),
            pl.BlockSpec((N_EXPERTS, D), full),
            pl.BlockSpec((N_EXPERTS, D), full),
            pl.BlockSpec((N_EXPERTS, 1), full),
            pl.BlockSpec((tm, tm), full),
        ],
        out_specs=[
            pl.BlockSpec((tm, D), row),
            pl.BlockSpec((tm, D // 2), row),
            pl.BlockSpec((TOP_K, tm), col),
            pl.BlockSpec((TOP_K, tm), col),
            pl.BlockSpec((TOP_K, tm), col),
            pl.BlockSpec((N_EXPERTS, 1), full),
        ],
        out_shape=[
            jax.ShapeDtypeStruct((T, D), F32),
            jax.ShapeDtypeStruct((T, D // 2), jnp.uint32),
            jax.ShapeDtypeStruct((TOP_K, T), jnp.int32),
            jax.ShapeDtypeStruct((TOP_K, T), F32),
            jax.ShapeDtypeStruct((TOP_K, T), jnp.int32),
            jax.ShapeDtypeStruct((N_EXPERTS, 1), jnp.int32),
        ],
        scratch_shapes=[pltpu.VMEM((N_EXPERTS, 1), F32)],
        compiler_params=_cparams(("arbitrary", "arbitrary")),
        name="outproj",
    )(lru_o, att_o, x2, mod3, mod3, mod3, g_ffn, wo1, wo2, wr_hi, wr_lo, b_r, tri)


def _dispatch_kernel(dest_ref, h_ref, xs_in_ref, xs_ref, sem, *, tm):
    del xs_in_ref
    n = TOP_K * tm

    def row_copy(a):
        t = lax.rem(a, tm)
        d = dest_ref[0, 0, a]
        return pltpu.make_async_copy(h_ref.at[pl.ds(t, 1), :], xs_ref.at[pl.ds(d, 1), :], sem)

    def issue(a, c):
        row_copy(a).start()
        return c

    lax.fori_loop(0, n, issue, 0)

    def drain(a, c):
        row_copy(a).wait()
        return c

    lax.fori_loop(0, n, drain, 0)


def _dispatch(dest_tiles, h2p, xs0, tm):
    T, W = h2p.shape
    nt = T // tm
    return pl.pallas_call(
        functools.partial(_dispatch_kernel, tm=tm),
        grid=(nt,),
        in_specs=[
            pl.BlockSpec((1, 1, TOP_K * tm), lambda i: (i, 0, 0), memory_space=pltpu.SMEM),
            pl.BlockSpec((tm, W), lambda i: (i, 0)),
            pl.BlockSpec(memory_space=pl.ANY),
        ],
        out_specs=pl.BlockSpec(memory_space=pl.ANY),
        out_shape=jax.ShapeDtypeStruct(xs0.shape, xs0.dtype),
        scratch_shapes=[pltpu.SemaphoreType.DMA(())],
        input_output_aliases={2: 0},
        compiler_params=_cparams(("arbitrary",)),
        name="dispatch",
    )(dest_tiles, h2p, xs0)


def _experts_kernel(be_ref, bv_ref, xs_ref, w1_ref, b1_ref, w2_ref, b2_ref, ys_ref):
    i = pl.program_id(0)

    @pl.when(bv_ref[i] > 0)
    def _():
        xw = xs_ref[...]
        lo = lax.bitcast_convert_type(xw << 16, F32).astype(BF16)
        hi = lax.bitcast_convert_type(xw & jnp.uint32(0xFFFF0000), F32).astype(BF16)
        half = xw.shape[1]
        gu = _dot(lo, w1_ref[0, :half, :]) + _dot(hi, w1_ref[0, half:, :]) + b1_ref[0]
        glu = jnp.minimum(gu[:, :D_FF], SWIGLU_LIMIT)
        lin = jnp.clip(gu[:, D_FF:], -SWIGLU_LIMIT, SWIGLU_LIMIT)
        act = (lin + 1.0) * (glu * _sigmoid(SWIGLU_ALPHA * glu))
        ys_ref[...] = _dot(act.astype(BF16), w2_ref[0]) + b2_ref[0]


def _experts(blk_e, blk_v, blk_r, xs, w1b, b1, w2b, b2):
    P, W = xs.shape
    nb = P // MOE_BLOCK
    E, D, F2 = w1b.shape
    grid_spec = pltpu.PrefetchScalarGridSpec(
        num_scalar_prefetch=3,
        grid=(nb,),
        in_specs=[
            pl.BlockSpec((MOE_BLOCK, W), lambda i, be, bv, br: (br[i], 0)),
            pl.BlockSpec((1, D, F2), lambda i, be, bv, br: (be[i], 0, 0)),
            pl.BlockSpec((1, 1, F2), lambda i, be, bv, br: (be[i], 0, 0)),
            pl.BlockSpec((1, D_FF, D), lambda i, be, bv, br: (be[i], 0, 0)),
            pl.BlockSpec((1, 1, D), lambda i, be, bv, br: (be[i], 0, 0)),
        ],
        out_specs=pl.BlockSpec((MOE_BLOCK, D), lambda i, be, bv, br: (br[i], 0)),
    )

    def kern(be_ref, bv_ref, br_ref, *refs):
        del br_ref
        _experts_kernel(be_ref, bv_ref, *refs)

    return pl.pallas_call(
        kern,
        grid_spec=grid_spec,
        out_shape=jax.ShapeDtypeStruct((P, D), F32),
        compiler_params=_cparams(("arbitrary",)),
        name="experts",
    )(blk_e, blk_v, blk_r, xs, w1b, b1.reshape(E, 1, F2), w2b, b2.reshape(E, 1, D))


def _combine_kernel(dest_ref, x1_ref, g_ref, gate_ref, ys_ref, o_ref, buf_ref, sem, *, tm):
    n = TOP_K * tm

    def row_copy(a):
        kk = a // tm
        t = lax.rem(a, tm)
        d = dest_ref[0, 0, a]
        return pltpu.make_async_copy(ys_ref.at[pl.ds(d, 1), :], buf_ref.at[kk, pl.ds(t, 1), :], sem)

    def issue(a, c):
        row_copy(a).start()
        return c

    lax.fori_loop(0, n, issue, 0)

    def drain(a, c):
        row_copy(a).wait()
        return c

    lax.fori_loop(0, n, drain, 0)

    g = g_ref[...]
    acc = buf_ref[0] * g[:, 0:1]
    for kk in range(1, TOP_K):
        acc = acc + buf_ref[kk] * g[:, kk:kk + 1]
    o_ref[...] = x1_ref[...] + gate_ref[0] * acc


def _combine(dest_tiles, x1, gates_tk, mod3, ys, B, S, tm):
    T, D = x1.shape
    ns = S // tm
    return pl.pallas_call(
        functools.partial(_combine_kernel, tm=tm),
        grid=(B, ns),
        in_specs=[
            pl.BlockSpec((1, 1, TOP_K * tm), lambda b, s: (b * ns + s, 0, 0), memory_space=pltpu.SMEM),
            pl.BlockSpec((tm, D), lambda b, s: (b * ns + s, 0)),
            pl.BlockSpec((tm, TOP_K), lambda b, s: (b * ns + s, 0)),
            pl.BlockSpec((1, 1, D), lambda b, s: (b * 6 + 5, 0, 0)),
            pl.BlockSpec(memory_space=pl.ANY),
        ],
        out_specs=pl.BlockSpec((tm, D), lambda b, s: (b * ns + s, 0)),
        out_shape=jax.ShapeDtypeStruct((T, D), F32),
        scratch_shapes=[pltpu.VMEM((TOP_K, tm, D), F32), pltpu.SemaphoreType.DMA(())],
        compiler_params=_cparams(("arbitrary", "arbitrary")),
        name="combine",
    )(dest_tiles, x1, gates_tk, mod3, ys)


def _block_diag(w):
    n, c, d = w.shape
    eye = jnp.eye(n, dtype=w.dtype)
    return jnp.einsum("ncd,nm->ncmd", w, eye).reshape(n * c, n * d)


def _pad_heads(w, width):
    k = w.shape[0]
    w = w.reshape(k, N_HEADS, width)
    return jnp.pad(w, ((0, 0), (0, 0), (0, HEAD_PAD - width))).reshape(k, N_HEADS * HEAD_PAD)


def _tile_major(a, tm):
    kk, T = a.shape
    return a.reshape(kk, T // tm, tm).transpose(1, 0, 2).reshape(T // tm, 1, kk * tm)


def kernel(x, c, positions, w_ada, b_ada, g_mix, w_in, conv_w, conv_b, w_a, b_a, w_x, b_x, lam,
           g_q_lat, w_uq, g_kv_lat, w_ukv, g_qn, g_kn, w_out, g_ffn, w_router, b_router,
           w1, b1, w2, b2):
    B, S, D = x.shape
    T = B * S
    depth = w_ada.shape[0]
    tm_in = min(512, S)
    ts_lru = min(256, S)
    tq = min(512, S)
    tm_out = min(512, S)
    tm_disp = min(512, S)
    tm_comb = min(256, S)

    o1 = 2 * D_LRU
    o2 = o1 + Q_LORA
    o3 = o2 + KV_LORA
    j = jnp.arange(LANES, dtype=jnp.int32)
    freqs = ROPE_THETA ** (-jnp.arange(ROPE_HALF, dtype=F32) / ROPE_HALF)
    freq_lane = jnp.where((j >= ROPE_LO) & (j < ROPE_LO + QK_ROPE),
                          freqs[(j - ROPE_LO) % ROPE_HALF], 0.0).reshape(1, LANES)
    tri = (jnp.arange(tm_out)[:, None] < jnp.arange(tm_out)[None, :]).astype(BF16)
    pos2 = positions.reshape(T, 1).astype(jnp.int32)

    x2 = x.reshape(T, D)
    for l in range(depth):
        mod3 = _ada(c, w_ada[l], b_ada[l]).reshape(B * 6, 1, D)

        w_in_l = w_in[l]
        kr_cols = jnp.pad(w_in_l[:, o3:], ((0, 0), (ROPE_LO, LANES - ROPE_LO - QK_ROPE)))
        w_in_p = jnp.concatenate([w_in_l[:, :o3], kr_cols], axis=1).astype(BF16)
        w_uq_p = _pad_heads(w_uq[l], QK_HEAD).astype(BF16)
        w_ukv_l = w_ukv[l].reshape(KV_LORA, N_HEADS, QK_NOPE + V_HEAD)
        w_uk_p = _pad_heads(w_ukv_l[:, :, :QK_NOPE].reshape(KV_LORA, N_HEADS * QK_NOPE),
                            QK_NOPE).astype(BF16)
        w_uv = w_ukv_l[:, :, QK_NOPE:].reshape(KV_LORA, N_HEADS * V_HEAD).astype(BF16)
        gqn_p = jnp.pad(g_qn[l], (0, HEAD_PAD - QK_HEAD)).reshape(1, HEAD_PAD)
        gkn_p = jnp.pad(g_kn[l], (0, HEAD_PAD - QK_HEAD)).reshape(1, HEAD_PAD)

        xlru, ylru, qp, kp, v = _inproj(
            x2, pos2, mod3, g_mix[l].reshape(1, D), w_in_p, g_q_lat[l].reshape(1, Q_LORA), w_uq_p,
            g_kv_lat[l].reshape(1, KV_LORA), w_uk_p, w_uv, gqn_p, gkn_p, freq_lane, B, S, tm_in)

        lru_o = _lru(xlru, ylru, conv_w[l], conv_b[l].reshape(1, D_LRU),
                     _block_diag(w_a[l]).astype(BF16), b_a[l].reshape(1, D_LRU),
                     _block_diag(w_x[l]).astype(BF16), b_x[l].reshape(1, D_LRU),
                     lam[l].reshape(1, D_LRU), B, S, ts_lru)

        att_o = _attn(qp, kp, v, B, S, tq)

        w_out_b = w_out[l].astype(BF16)
        wr_hi, wr_lo = _split_bf16(w_router[l].T)
        x1, h2p, idx_t, gat_t, rank_t, counts = _outproj(
            lru_o, att_o, x2, mod3, g_ffn[l].reshape(1, D), w_out_b[:D_LRU], w_out_b[D_LRU:],
            wr_hi, wr_lo, b_router[l].reshape(N_EXPERTS, 1), tri, B, S, tm_out)

        counts = counts.reshape(N_EXPERTS)
        nblk_e = (counts + MOE_BLOCK - 1) // MOE_BLOCK
        blk_end = jnp.cumsum(nblk_e)
        pad_start = (blk_end - nblk_e) * MOE_BLOCK
        n_blocks = -(-(T * TOP_K) // MOE_BLOCK) + N_EXPERTS
        bi = jnp.arange(n_blocks, dtype=jnp.int32)
        total = blk_end[-1]
        blk_v = (bi < total).astype(jnp.int32)
        blk_r = jnp.minimum(bi, total - 1).astype(jnp.int32)
        blk_e = jnp.clip(jnp.searchsorted(blk_end, blk_r, side="right"), 0,
                         N_EXPERTS - 1).astype(jnp.int32)
        dest = pad_start[idx_t].astype(jnp.int32) + rank_t

        xs0 = jnp.zeros((n_blocks * MOE_BLOCK, D // 2), jnp.uint32)
        xs = _dispatch(_tile_major(dest, tm_disp), h2p, xs0, tm_disp)
        ys = _experts(blk_e, blk_v, blk_r, xs, w1[l].astype(BF16), b1[l], w2[l].astype(BF16), b2[l])
        x2 = _combine(_tile_major(dest, tm_comb), x1, gat_t.T, mod3, ys, B, S, tm_comb)
    return x2.reshape(B, S, D)
```

```python
import functools

import jax
import jax.numpy as jnp
from jax import lax
from jax.experimental import pallas as pl
from jax.experimental.pallas import tpu as pltpu
from jax.experimental.pallas import tpu_sc as plsc

D_MODEL = 1024
D_LRU = 512
LRU_BLOCKS = 8
LRU_BD = 64
CONV_W = 4
LRU_C = 8.0
N_HEADS = 8
QK_NOPE = 64
QK_ROPE = 32
QK_HEAD = 96
V_HEAD = 64
Q_LORA = 256
KV_LORA = 128
ROPE_THETA = 10000.0
N_EXPERTS = 32
TOP_K = 4
D_FF = 1024
SWIGLU_LIMIT = 7.0
SWIGLU_ALPHA = 1.702
MOE_BLOCK = 512
EPS = 1e-6

LANES = 128
HEAD_PAD = 128
ROPE_LO = QK_NOPE
ROPE_HALF = QK_ROPE // 2
D_IN_PAD = 2 * D_LRU + Q_LORA + KV_LORA + LANES

VMEM_LIMIT = 56 * 1024 * 1024

F32 = jnp.float32
BF16 = jnp.bfloat16


def _cparams(sem):
    return pltpu.CompilerParams(dimension_semantics=sem, vmem_limit_bytes=VMEM_LIMIT)


def _dot(a, b):
    return jnp.dot(a, b, preferred_element_type=F32)


def _dot_nt(a, b):
    return lax.dot_general(a, b, (((1,), (1,)), ((), ())), preferred_element_type=F32)


def _split_bf16(a):
    hi = a.astype(BF16)
    lo = (a - hi.astype(F32)).astype(BF16)
    return hi, lo


def _sigmoid(x):
    return 1.0 / (1.0 + jnp.exp(-x))


def _ada_kernel(c_ref, whi_ref, wlo_ref, b_ref, o_ref):
    c = c_ref[...]
    s = c * _sigmoid(c)
    shi, slo = _split_bf16(s)
    whi = whi_ref[...]
    o_ref[...] = _dot(shi, whi) + _dot(slo, whi) + _dot(shi, wlo_ref[...]) + b_ref[...]


def _ada(c, w_ada, b_ada):
    B, D = c.shape
    N = w_ada.shape[1]
    tn = 1024
    whi, wlo = _split_bf16(w_ada)
    return pl.pallas_call(
        _ada_kernel,
        grid=(N // tn,),
        in_specs=[
            pl.BlockSpec((B, D), lambda j: (0, 0)),
            pl.BlockSpec((D, tn), lambda j: (0, j)),
            pl.BlockSpec((D, tn), lambda j: (0, j)),
            pl.BlockSpec((1, tn), lambda j: (0, j)),
        ],
        out_specs=pl.BlockSpec((B, tn), lambda j: (0, j)),
        out_shape=jax.ShapeDtypeStruct((B, N), F32),
        compiler_params=_cparams(("arbitrary",)),
        name="ada",
    )(c, whi, wlo, b_ada.reshape(1, N))


def _rope(xh, cos_t, sin_t, lane):
    fwd = pltpu.roll(xh, LANES - ROPE_HALF, axis=1)
    bwd = pltpu.roll(xh, ROPE_HALF, axis=1)
    rot = jnp.where(lane < ROPE_LO + ROPE_HALF, fwd, bwd)
    return xh * cos_t + rot * sin_t


def _inproj_kernel(x_ref, pos_ref, shift_ref, scale_ref, gmix_ref, win_ref, gq_ref, wuq_ref,
                   gkv_ref, wuk_ref, wuv_ref, gqn_ref, gkn_ref, freq_ref,
                   xlru_ref, ylru_ref, q_ref, k_ref, v_ref):
    x = x_ref[...]
    ms = jnp.mean(x * x, axis=-1, keepdims=True)
    xn = x * lax.rsqrt(ms + EPS) * gmix_ref[...]
    h = xn * (1.0 + scale_ref[0]) + shift_ref[0]
    z = _dot(h.astype(BF16), win_ref[...])
    xlru_ref[...] = z[:, :D_LRU]
    ylru_ref[...] = z[:, D_LRU:2 * D_LRU]
    o1 = 2 * D_LRU
    o2 = o1 + Q_LORA
    o3 = o2 + KV_LORA
    ql = z[:, o1:o2]
    kvl = z[:, o2:o3]
    kr = z[:, o3:]

    qn = ql * lax.rsqrt(jnp.mean(ql * ql, axis=-1, keepdims=True) + EPS) * gq_ref[...]
    q = _dot(qn.astype(BF16), wuq_ref[...])
    kvn = kvl * lax.rsqrt(jnp.mean(kvl * kvl, axis=-1, keepdims=True) + EPS) * gkv_ref[...]
    kvb = kvn.astype(BF16)
    kn = _dot(kvb, wuk_ref[...])
    v_ref[...] = _dot(kvb, wuv_ref[...]).astype(BF16)

    tm = x.shape[0]
    lane = lax.broadcasted_iota(jnp.int32, (tm, LANES), 1)
    in_rope = (lane >= ROPE_LO) & (lane < ROPE_LO + QK_ROPE)
    ang = pos_ref[...].astype(F32) * freq_ref[...]
    cos_t = jnp.where(in_rope, jnp.cos(ang), 1.0)
    sin_v = jnp.sin(ang)
    sin_t = jnp.where(in_rope, jnp.where(lane < ROPE_LO + ROPE_HALF, -sin_v, sin_v), 0.0)

    gqn = gqn_ref[...]
    gkn = gkn_ref[...]
    inv_w = 1.0 / QK_HEAD
    qscale = QK_HEAD ** -0.5
    for hh in range(N_HEADS):
        sl = slice(hh * HEAD_PAD, (hh + 1) * HEAD_PAD)
        qh = q[:, sl]
        qh = qh * lax.rsqrt(jnp.sum(qh * qh, axis=-1, keepdims=True) * inv_w + EPS) * gqn
        q_ref[:, sl] = (_rope(qh, cos_t, sin_t, lane) * qscale).astype(BF16)
        kh = kn[:, sl] + kr
        kh = kh * lax.rsqrt(jnp.sum(kh * kh, axis=-1, keepdims=True) * inv_w + EPS) * gkn
        k_ref[:, sl] = _rope(kh, cos_t, sin_t, lane).astype(BF16)


def _inproj(x2, pos2, mod3, g_mix, w_in_p, g_q_lat, w_uq_p, g_kv_lat, w_uk_p, w_uv, gqn_p, gkn_p,
            freq_lane, B, S, tm):
    T, D = x2.shape
    ns = S // tm
    HP = N_HEADS * HEAD_PAD
    row = lambda b, s: (b * ns + s, 0)
    full = lambda b, s: (0, 0)
    return pl.pallas_call(
        _inproj_kernel,
        grid=(B, ns),
        in_specs=[
            pl.BlockSpec((tm, D), row),
            pl.BlockSpec((tm, 1), row),
            pl.BlockSpec((1, 1, D), lambda b, s: (b * 6 + 0, 0, 0)),
            pl.BlockSpec((1, 1, D), lambda b, s: (b * 6 + 1, 0, 0)),
            pl.BlockSpec((1, D), full),
            pl.BlockSpec((D, D_IN_PAD), full),
            pl.BlockSpec((1, Q_LORA), full),
            pl.BlockSpec((Q_LORA, HP), full),
            pl.BlockSpec((1, KV_LORA), full),
            pl.BlockSpec((KV_LORA, HP), full),
            pl.BlockSpec((KV_LORA, N_HEADS * V_HEAD), full),
            pl.BlockSpec((1, HEAD_PAD), full),
            pl.BlockSpec((1, HEAD_PAD), full),
            pl.BlockSpec((1, LANES), full),
        ],
        out_specs=[
            pl.BlockSpec((tm, D_LRU), row),
            pl.BlockSpec((tm, D_LRU), row),
            pl.BlockSpec((tm, HP), row),
            pl.BlockSpec((tm, HP), row),
            pl.BlockSpec((tm, N_HEADS * V_HEAD), row),
        ],
        out_shape=[
            jax.ShapeDtypeStruct((T, D_LRU), F32),
            jax.ShapeDtypeStruct((T, D_LRU), F32),
            jax.ShapeDtypeStruct((T, HP), BF16),
            jax.ShapeDtypeStruct((T, HP), BF16),
            jax.ShapeDtypeStruct((T, N_HEADS * V_HEAD), BF16),
        ],
        compiler_params=_cparams(("arbitrary", "arbitrary")),
        name="inproj",
    )(x2, pos2, mod3, mod3, g_mix, w_in_p, g_q_lat, w_uq_p, g_kv_lat, w_uk_p, w_uv, gqn_p, gkn_p,
      freq_lane)


def _gelu_tanh(x):
    return 0.5 * x * (1.0 + jnp.tanh(0.7978845608028654 * (x + 0.044715 * x * x * x)))


def _lru_kernel(x_ref, y_ref, cw_ref, cb_ref, wa_ref, ba_ref, wx_ref, bx_ref, lam_ref,
                o_ref, tail_ref, carry_ref):
    s = pl.program_id(1)

    @pl.when(s == 0)
    def _():
        tail_ref[...] = jnp.zeros_like(tail_ref)
        carry_ref[...] = jnp.zeros_like(carry_ref)

    x = x_ref[...]
    ts = x.shape[0]
    xext = jnp.concatenate([tail_ref[...], x], axis=0)
    cw = cw_ref[...]
    xc = x * cw[CONV_W - 1:CONV_W, :]
    for j in range(CONV_W - 1):
        sh = CONV_W - 1 - j
        xc = xc + xext[8 - sh:8 - sh + ts, :] * cw[j:j + 1, :]
    xc = xc + cb_ref[...]
    tail_ref[...] = x[ts - 8:, :]

    xb = xc.astype(BF16)
    r = _sigmoid(_dot(xb, wa_ref[...]) + ba_ref[...])
    i = _sigmoid(_dot(xb, wx_ref[...]) + bx_ref[...])
    lam = lam_ref[...]
    nl = -lam
    softplus = jnp.maximum(nl, 0.0) + jnp.log(1.0 + jnp.exp(-jnp.abs(nl)))
    log_a = (-LRU_C) * r * softplus
    a = jnp.exp(log_a)
    mult = jnp.sqrt(1.0 - jnp.exp(2.0 * log_a))
    u = mult * (i * xc)

    rowi = lax.broadcasted_iota(jnp.int32, (ts, 1), 0)
    sh = 1
    while sh < ts:
        a_prev = pltpu.roll(a, sh, axis=0)
        u_prev = pltpu.roll(u, sh, axis=0)
        m = rowi >= sh
        u = jnp.where(m, a * u_prev + u, u)
        a = jnp.where(m, a * a_prev, a)
        sh *= 2
    hcar = carry_ref[0:1, :]
    hs = u + a * hcar
    carry_ref[...] = jnp.broadcast_to(hs[ts - 1:ts, :], carry_ref.shape)
    o_ref[...] = (_gelu_tanh(y_ref[...]) * hs).astype(BF16)


def _lru(xlru, ylru, conv_w, conv_b, wa_d, b_a, wx_d, b_x, lam, B, S, ts):
    T, C = xlru.shape
    ns = S // ts
    row = lambda b, s: (b * ns + s, 0)
    full = lambda b, s: (0, 0)
    return pl.pallas_call(
        _lru_kernel,
        grid=(B, ns),
        in_specs=[
            pl.BlockSpec((ts, C), row),
            pl.BlockSpec((ts, C), row),
            pl.BlockSpec((CONV_W, C), full),
            pl.BlockSpec((1, C), full),
            pl.BlockSpec((C, C), full),
            pl.BlockSpec((1, C), full),
            pl.BlockSpec((C, C), full),
            pl.BlockSpec((1, C), full),
            pl.BlockSpec((1, C), full),
        ],
        out_specs=pl.BlockSpec((ts, C), row),
        out_shape=jax.ShapeDtypeStruct((T, C), BF16),
        scratch_shapes=[pltpu.VMEM((8, C), F32), pltpu.VMEM((8, C), F32)],
        compiler_params=_cparams(("arbitrary", "arbitrary")),
        name="lru",
    )(xlru, ylru, conv_w, conv_b, wa_d, b_a, wx_d, b_x, lam)


NEG_INF = -1e30


def _attn_kernel(q_ref, k_ref, v_ref, o_ref, *, tq):
    qi = pl.program_id(1)
    rowi = lax.broadcasted_iota(jnp.int32, (tq, tq), 0)
    coli = lax.broadcasted_iota(jnp.int32, (tq, tq), 1)
    diag_mask = coli <= rowi

    for hh in range(N_HEADS):
        qs = slice(hh * HEAD_PAD, (hh + 1) * HEAD_PAD)
        vs = slice(hh * V_HEAD, (hh + 1) * V_HEAD)
        qh = q_ref[:, qs]

        def step(j, carry, masked):
            m, l, acc = carry
            r0 = pl.multiple_of(j * tq, tq)
            kh = k_ref[pl.ds(r0, tq), qs]
            vh = v_ref[pl.ds(r0, tq), vs]
            sc = _dot_nt(qh, kh)
            if masked:
                sc = jnp.where(diag_mask, sc, NEG_INF)
            m_new = jnp.maximum(m, jnp.max(sc, axis=-1, keepdims=True))
            alpha = jnp.exp(m - m_new)
            p = jnp.exp(sc - m_new)
            l_new = alpha * l + jnp.sum(p, axis=-1, keepdims=True)
            acc_new = alpha * acc + _dot(p.astype(BF16), vh)
            return m_new, l_new, acc_new

        init = (jnp.full((tq, 1), NEG_INF, F32), jnp.zeros((tq, 1), F32),
                jnp.zeros((tq, V_HEAD), F32))
        carry = lax.fori_loop(0, qi, functools.partial(step, masked=False), init)
        m, l, acc = step(qi, carry, True)
        o_ref[:, vs] = (acc / l).astype(BF16)


def _attn(qp, kp, v, B, S, tq):
    T = qp.shape[0]
    nq = S // tq
    HP = N_HEADS * HEAD_PAD
    HV = N_HEADS * V_HEAD
    return pl.pallas_call(
        functools.partial(_attn_kernel, tq=tq),
        grid=(B, nq),
        in_specs=[
            pl.BlockSpec((tq, HP), lambda b, i: (b * nq + i, 0)),
            pl.BlockSpec((S, HP), lambda b, i: (b, 0)),
            pl.BlockSpec((S, HV), lambda b, i: (b, 0)),
        ],
        out_specs=pl.BlockSpec((tq, HV), lambda b, i: (b * nq + i, 0)),
        out_shape=jax.ShapeDtypeStruct((T, HV), BF16),
        compiler_params=_cparams(("arbitrary", "arbitrary")),
        name="attn",
    )(qp, kp, v)


def _outproj_kernel(lru_ref, att_ref, x_ref, gate_ref, shift_ref, scale_ref, gffn_ref,
                    wo1_ref, wo2_ref, wrh_ref, wrl_ref, br_ref, tri_ref,
                    x1_ref, h2p_ref, idx_ref, gat_ref, rank_ref, cnt_ref, run_ref):
    first = (pl.program_id(0) == 0) & (pl.program_id(1) == 0)

    @pl.when(first)
    def _():
        run_ref[...] = jnp.zeros_like(run_ref)

    mix = _dot(lru_ref[...], wo1_ref[...]) + _dot(att_ref[...], wo2_ref[...])
    x1 = x_ref[...] + gate_ref[0] * mix
    x1_ref[...] = x1
    ms = jnp.mean(x1 * x1, axis=-1, keepdims=True)
    h2 = x1 * lax.rsqrt(ms + EPS) * gffn_ref[...]
    h2 = h2 * (1.0 + scale_ref[0]) + shift_ref[0]

    hhi = h2.astype(BF16)
    hhi32 = hhi.astype(F32)
    hlo = (h2 - hhi32).astype(BF16)
    bits = lax.bitcast_convert_type(hhi32, jnp.uint32)
    half = h2.shape[1] // 2
    words = (bits[:, :half] >> 16) | (bits[:, half:] & jnp.uint32(0xFFFF0000))
    h2p_ref[...] = lax.bitcast_convert_type(words, jnp.int32)

    wrh = wrh_ref[...]
    logits = _dot_nt(wrh, hhi) + _dot_nt(wrh, hlo) + _dot_nt(wrl_ref[...], hhi) + br_ref[...]
    ne, tm = logits.shape
    eio = lax.broadcasted_iota(jnp.int32, (ne, tm), 0)
    vals, idxs, sels = [], [], []
    l = logits
    for _ in range(TOP_K):
        m = jnp.max(l, axis=0, keepdims=True)
        idx = jnp.min(jnp.where(l == m, eio, ne), axis=0, keepdims=True)
        sel = eio == idx
        l = jnp.where(sel, -jnp.inf, l)
        vals.append(m)
        idxs.append(idx)
        sels.append(sel)
    es = [jnp.exp(v - vals[0]) for v in vals]
    den = es[0] + es[1] + es[2] + es[3]
    inv = 1.0 / den
    sel_any = jnp.where(sels[0] | sels[1] | sels[2] | sels[3], 1.0, 0.0)
    excl = _dot(sel_any.astype(BF16), tri_ref[...]) + run_ref[...]
    for kk in range(TOP_K):
        idx_ref[kk:kk + 1, :] = idxs[kk]
        gat_ref[kk:kk + 1, :] = es[kk] * inv
        rk = jnp.sum(jnp.where(sels[kk], excl, 0.0), axis=0, keepdims=True)
        rank_ref[kk:kk + 1, :] = rk.astype(jnp.int32)
    run = run_ref[...] + jnp.sum(sel_any, axis=1, keepdims=True)
    run_ref[...] = run
    cnt_ref[...] = run.astype(jnp.int32)


def _outproj(lru_o, att_o, x2, mod3, g_ffn, wo1, wo2, wr_hi, wr_lo, b_r, tri, B, S, tm):
    T, D = x2.shape
    ns = S // tm
    C = lru_o.shape[1]
    row = lambda b, s: (b * ns + s, 0)
    col = lambda b, s: (0, b * ns + s)
    full = lambda b, s: (0, 0)
    return pl.pallas_call(
        _outproj_kernel,
        grid=(B, ns),
        in_specs=[
            pl.BlockSpec((tm, C), row),
            pl.BlockSpec((tm, C), row),
            pl.BlockSpec((tm, D), row),
            pl.BlockSpec((1, 1, D), lambda b, s: (b * 6 + 2, 0, 0)),
            pl.BlockSpec((1, 1, D), lambda b, s: (b * 6 + 3, 0, 0)),
            pl.BlockSpec((1, 1, D), lambda b, s: (b * 6 + 4, 0, 0)),
            pl.BlockSpec((1, D), full),
            pl.BlockSpec((C, D), full),
            pl.BlockSpec((C, D), full),
            pl.BlockSpec((N_EXPERTS, D), full),
            pl.BlockSpec((N_EXPERTS, D), full),
            pl.BlockSpec((N_EXPERTS, 1), full),
            pl.BlockSpec((tm, tm), full),
        ],
        out_specs=[
            pl.BlockSpec((tm, D), row),
            pl.BlockSpec((tm, D // 2), row),
            pl.BlockSpec((TOP_K, tm), col),
            pl.BlockSpec((TOP_K, tm), col),
            pl.BlockSpec((TOP_K, tm), col),
            pl.BlockSpec((N_EXPERTS, 1), full),
        ],
        out_shape=[
            jax.ShapeDtypeStruct((T, D), F32),
            jax.ShapeDtypeStruct((T, D // 2), jnp.int32),
            jax.ShapeDtypeStruct((TOP_K, T), jnp.int32),
            jax.ShapeDtypeStruct((TOP_K, T), F32),
            jax.ShapeDtypeStruct((TOP_K, T), jnp.int32),
            jax.ShapeDtypeStruct((N_EXPERTS, 1), jnp.int32),
        ],
        scratch_shapes=[pltpu.VMEM((N_EXPERTS, 1), F32)],
        compiler_params=_cparams(("arbitrary", "arbitrary")),
        name="outproj",
    )(lru_o, att_o, x2, mod3, mod3, mod3, g_ffn, wo1, wo2, wr_hi, wr_lo, b_r, tri)


SC_CORES = 2
SC_SUBCORES = 16
SC_WORKERS = SC_CORES * SC_SUBCORES


def _sc_mesh():
    return plsc.VectorSubcoreMesh(core_axis_name="c", subcore_axis_name="s",
                                  num_cores=SC_CORES, num_subcores=SC_SUBCORES)


def _sc_worker_id():
    return lax.axis_index("s") * SC_CORES + lax.axis_index("c")


def _sc_scatter_rows(rows, idx, n_out, g):
    T, W = rows.shape
    K = idx.shape[0]
    per_w = T // SC_WORKERS
    nch = per_w // g
    assert per_w * SC_WORKERS == T and nch * g == per_w and nch % 2 == 0
    idx_w = idx.reshape(K, SC_WORKERS, nch, g).transpose(1, 2, 0, 3).reshape(SC_WORKERS, nch * K, g)

    def body(rows_hbm, idx_hbm, out_hbm, idx_v, buf0, buf1, semr0, semr1, semw):
        wid = _sc_worker_id()
        base = wid * per_w
        pltpu.sync_copy(idx_hbm.at[wid], idx_v)

        def read(j, buf, sem):
            return pltpu.make_async_copy(rows_hbm.at[pl.ds(base + j * g, g)], buf, sem)

        def scatter(j, buf):
            copies = [pltpu.async_copy(buf, out_hbm.at[idx_v.at[j * K + kk]], semw)
                      for kk in range(K)]
            for cp in copies:
                cp.wait()

        read(0, buf0, semr0).start()

        @pl.loop(0, nch // 2)
        def _(jj):
            j0 = 2 * jj
            read(j0 + 1, buf1, semr1).start()
            read(j0, buf0, semr0).wait()
            scatter(j0, buf0)

            @pl.when(j0 + 2 < nch)
            def _():
                read(j0 + 2, buf0, semr0).start()

            read(j0 + 1, buf1, semr1).wait()
            scatter(j0 + 1, buf1)

    return pl.kernel(
        body,
        out_type=jax.ShapeDtypeStruct((n_out, W), rows.dtype),
        mesh=_sc_mesh(),
        scratch_types=[
            pltpu.VMEM((nch * K, g), jnp.int32),
            pltpu.VMEM((g, W), rows.dtype),
            pltpu.VMEM((g, W), rows.dtype),
            pltpu.SemaphoreType.DMA,
            pltpu.SemaphoreType.DMA,
            pltpu.SemaphoreType.DMA,
        ],
        name="sc_scatter_rows",
    )(rows, idx_w)


def _sc_gather_rows(table, idx, g):
    W = table.shape[1]
    N = idx.shape[0]
    per_w = N // SC_WORKERS
    nch = per_w // g
    assert per_w * SC_WORKERS == N and nch * g == per_w and nch % 2 == 0
    idx_w = idx.reshape(SC_WORKERS, nch, g)

    def body(table_hbm, idx_hbm, out_hbm, idx_v, buf0, buf1, sem0, sem1):
        wid = _sc_worker_id()
        base = wid * per_w
        pltpu.sync_copy(idx_hbm.at[wid], idx_v)

        def gather(j, buf, sem):
            return pltpu.make_async_copy(table_hbm.at[idx_v.at[j]], buf, sem)

        def put(j, buf):
            pltpu.sync_copy(buf, out_hbm.at[pl.ds(base + j * g, g)])

        gather(0, buf0, sem0).start()

        @pl.loop(0, nch // 2)
        def _(jj):
            j0 = 2 * jj
            gather(j0 + 1, buf1, sem1).start()
            gather(j0, buf0, sem0).wait()
            put(j0, buf0)

            @pl.when(j0 + 2 < nch)
            def _():
                gather(j0 + 2, buf0, sem0).start()

            gather(j0 + 1, buf1, sem1).wait()
            put(j0 + 1, buf1)

    return pl.kernel(
        body,
        out_type=jax.ShapeDtypeStruct((N, W), table.dtype),
        mesh=_sc_mesh(),
        scratch_types=[
            pltpu.VMEM((nch, g), jnp.int32),
            pltpu.VMEM((g, W), table.dtype),
            pltpu.VMEM((g, W), table.dtype),
            pltpu.SemaphoreType.DMA,
            pltpu.SemaphoreType.DMA,
        ],
        name="sc_gather_rows",
    )(table, idx_w)


def _experts_kernel(be_ref, bv_ref, xs_ref, w1_ref, b1_ref, w2_ref, b2_ref, ys_ref):
    i = pl.program_id(0)

    nvalid = bv_ref[i]

    @pl.when(nvalid > 0)
    def _():
        xw = lax.bitcast_convert_type(xs_ref[...], jnp.uint32)
        rowi = lax.broadcasted_iota(jnp.int32, (xw.shape[0], 1), 0)
        xw = jnp.where(rowi < nvalid, xw, jnp.uint32(0))
        lo = lax.bitcast_convert_type(xw << 16, F32).astype(BF16)
        hi = lax.bitcast_convert_type(xw & jnp.uint32(0xFFFF0000), F32).astype(BF16)
        half = xw.shape[1]
        gu = _dot(lo, w1_ref[0, :half, :]) + _dot(hi, w1_ref[0, half:, :]) + b1_ref[0]
        glu = jnp.minimum(gu[:, :D_FF], SWIGLU_LIMIT)
        lin = jnp.clip(gu[:, D_FF:], -SWIGLU_LIMIT, SWIGLU_LIMIT)
        act = (lin + 1.0) * (glu * _sigmoid(SWIGLU_ALPHA * glu))
        ys_ref[...] = _dot(act.astype(BF16), w2_ref[0]) + b2_ref[0]


def _experts(blk_e, blk_v, blk_r, xs, w1b, b1, w2b, b2):
    P, W = xs.shape
    nb = P // MOE_BLOCK
    E, D, F2 = w1b.shape
    grid_spec = pltpu.PrefetchScalarGridSpec(
        num_scalar_prefetch=3,
        grid=(nb,),
        in_specs=[
            pl.BlockSpec((MOE_BLOCK, W), lambda i, be, bv, br: (br[i], 0)),
            pl.BlockSpec((1, D, F2), lambda i, be, bv, br: (be[i], 0, 0)),
            pl.BlockSpec((1, 1, F2), lambda i, be, bv, br: (be[i], 0, 0)),
            pl.BlockSpec((1, D_FF, D), lambda i, be, bv, br: (be[i], 0, 0)),
            pl.BlockSpec((1, 1, D), lambda i, be, bv, br: (be[i], 0, 0)),
        ],
        out_specs=pl.BlockSpec((MOE_BLOCK, D), lambda i, be, bv, br: (br[i], 0)),
    )

    def kern(be_ref, bv_ref, br_ref, *refs):
        del br_ref
        _experts_kernel(be_ref, bv_ref, *refs)

    return pl.pallas_call(
        kern,
        grid_spec=grid_spec,
        out_shape=jax.ShapeDtypeStruct((P, D), F32),
        compiler_params=_cparams(("arbitrary",)),
        name="experts",
    )(blk_e, blk_v, blk_r, xs, w1b, b1.reshape(E, 1, F2), w2b, b2.reshape(E, 1, D))


def _combine_kernel(x1_ref, g_ref, gate_ref, y0_ref, y1_ref, y2_ref, y3_ref, o_ref):
    g = g_ref[...]
    acc = y0_ref[0] * g[:, 0:1]
    for kk, y_ref in enumerate((y1_ref, y2_ref, y3_ref), start=1):
        acc = acc + y_ref[0] * g[:, kk:kk + 1]
    o_ref[...] = x1_ref[...] + gate_ref[0] * acc


def _combine(x1, gates_tk, mod3, ysg, B, S, tm):
    T, D = x1.shape
    ns = S // tm
    row = lambda b, s: (b * ns + s, 0)

    def yspec(kk):
        return pl.BlockSpec((1, tm, D), lambda b, s: (kk, b * ns + s, 0))

    return pl.pallas_call(
        _combine_kernel,
        grid=(B, ns),
        in_specs=[
            pl.BlockSpec((tm, D), row),
            pl.BlockSpec((tm, TOP_K), row),
            pl.BlockSpec((1, 1, D), lambda b, s: (b * 6 + 5, 0, 0)),
            yspec(0), yspec(1), yspec(2), yspec(3),
        ],
        out_specs=pl.BlockSpec((tm, D), row),
        out_shape=jax.ShapeDtypeStruct((T, D), F32),
        compiler_params=_cparams(("arbitrary", "arbitrary")),
        name="combine",
    )(x1, gates_tk, mod3, ysg, ysg, ysg, ysg)


def _block_diag(w):
    n, c, d = w.shape
    eye = jnp.eye(n, dtype=w.dtype)
    return jnp.einsum("ncd,nm->ncmd", w, eye).reshape(n * c, n * d)


def _pad_heads(w, width):
    k = w.shape[0]
    w = w.reshape(k, N_HEADS, width)
    return jnp.pad(w, ((0, 0), (0, 0), (0, HEAD_PAD - width))).reshape(k, N_HEADS * HEAD_PAD)


def kernel(x, c, positions, w_ada, b_ada, g_mix, w_in, conv_w, conv_b, w_a, b_a, w_x, b_x, lam,
           g_q_lat, w_uq, g_kv_lat, w_ukv, g_qn, g_kn, w_out, g_ffn, w_router, b_router,
           w1, b1, w2, b2):
    B, S, D = x.shape
    T = B * S
    depth = w_ada.shape[0]
    tm_in = min(512, S)
    ts_lru = min(256, S)
    tq = min(512, S)
    tm_out = min(512, S)
    tm_comb = min(512, S)
    g_disp = min(64, T // SC_WORKERS // 2)
    g_comb = min(32, T * TOP_K // SC_WORKERS // 2)

    o1 = 2 * D_LRU
    o2 = o1 + Q_LORA
    o3 = o2 + KV_LORA
    j = jnp.arange(LANES, dtype=jnp.int32)
    freqs = ROPE_THETA ** (-jnp.arange(ROPE_HALF, dtype=F32) / ROPE_HALF)
    freq_lane = jnp.where((j >= ROPE_LO) & (j < ROPE_LO + QK_ROPE),
                          freqs[(j - ROPE_LO) % ROPE_HALF], 0.0).reshape(1, LANES)
    tri = (jnp.arange(tm_out)[:, None] < jnp.arange(tm_out)[None, :]).astype(BF16)
    pos2 = positions.reshape(T, 1).astype(jnp.int32)

    x2 = x.reshape(T, D)
    for l in range(depth):
        mod3 = _ada(c, w_ada[l], b_ada[l]).reshape(B * 6, 1, D)

        w_in_l = w_in[l]
        kr_cols = jnp.pad(w_in_l[:, o3:], ((0, 0), (ROPE_LO, LANES - ROPE_LO - QK_ROPE)))
        w_in_p = jnp.concatenate([w_in_l[:, :o3], kr_cols], axis=1).astype(BF16)
        w_uq_p = _pad_heads(w_uq[l], QK_HEAD).astype(BF16)
        w_ukv_l = w_ukv[l].reshape(KV_LORA, N_HEADS, QK_NOPE + V_HEAD)
        w_uk_p = _pad_heads(w_ukv_l[:, :, :QK_NOPE].reshape(KV_LORA, N_HEADS * QK_NOPE),
                            QK_NOPE).astype(BF16)
        w_uv = w_ukv_l[:, :, QK_NOPE:].reshape(KV_LORA, N_HEADS * V_HEAD).astype(BF16)
        gqn_p = jnp.pad(g_qn[l], (0, HEAD_PAD - QK_HEAD)).reshape(1, HEAD_PAD)
        gkn_p = jnp.pad(g_kn[l], (0, HEAD_PAD - QK_HEAD)).reshape(1, HEAD_PAD)

        xlru, ylru, qp, kp, v = _inproj(
            x2, pos2, mod3, g_mix[l].reshape(1, D), w_in_p, g_q_lat[l].reshape(1, Q_LORA), w_uq_p,
            g_kv_lat[l].reshape(1, KV_LORA), w_uk_p, w_uv, gqn_p, gkn_p, freq_lane, B, S, tm_in)

        lru_o = _lru(xlru, ylru, conv_w[l], conv_b[l].reshape(1, D_LRU),
                     _block_diag(w_a[l]).astype(BF16), b_a[l].reshape(1, D_LRU),
                     _block_diag(w_x[l]).astype(BF16), b_x[l].reshape(1, D_LRU),
                     lam[l].reshape(1, D_LRU), B, S, ts_lru)

        att_o = _attn(qp, kp, v, B, S, tq)

        w_out_b = w_out[l].astype(BF16)
        wr_hi, wr_lo = _split_bf16(w_router[l].T)
        x1, h2p, idx_t, gat_t, rank_t, counts = _outproj(
            lru_o, att_o, x2, mod3, g_ffn[l].reshape(1, D), w_out_b[:D_LRU], w_out_b[D_LRU:],
            wr_hi, wr_lo, b_router[l].reshape(N_EXPERTS, 1), tri, B, S, tm_out)

        counts = counts.reshape(N_EXPERTS)
        nblk_e = (counts + MOE_BLOCK - 1) // MOE_BLOCK
        blk_end = jnp.cumsum(nblk_e)
        pad_start = (blk_end - nblk_e) * MOE_BLOCK
        n_blocks = -(-(T * TOP_K) // MOE_BLOCK) + N_EXPERTS
        bi = jnp.arange(n_blocks, dtype=jnp.int32)
        total = blk_end[-1]
        blk_r = jnp.minimum(bi, total - 1).astype(jnp.int32)
        blk_e = jnp.minimum(jnp.sum(blk_end[None, :] <= blk_r[:, None], axis=1),
                            N_EXPERTS - 1).astype(jnp.int32)
        eio = jnp.arange(N_EXPERTS, dtype=jnp.int32)
        blk_onehot = blk_e[:, None] == eio[None, :]
        blk_first = jnp.sum(jnp.where(blk_onehot, (blk_end - nblk_e)[None, :], 0), axis=1)
        blk_cnt = jnp.sum(jnp.where(blk_onehot, counts[None, :], 0), axis=1)
        blk_n = jnp.where(bi < total, jnp.clip(blk_cnt - (bi - blk_first) * MOE_BLOCK, 0, MOE_BLOCK),
                          0).astype(jnp.int32)
        slot0 = jnp.sum(jnp.where(idx_t[None] == eio[:, None, None], pad_start[:, None, None], 0),
                        axis=0)
        dest = slot0.astype(jnp.int32) + rank_t

        xs = _sc_scatter_rows(h2p, dest, n_blocks * MOE_BLOCK, g_disp)
        ys = _experts(blk_e, blk_n, blk_r, xs, w1[l].astype(BF16), b1[l], w2[l].astype(BF16), b2[l])
        ysg = _sc_gather_rows(ys, dest.reshape(TOP_K * T), g_comb).reshape(TOP_K, T, D)
        x2 = _combine(x1, gat_t.T, mod3, ysg, B, S, tm_comb)
    return x2.reshape(B, S, D)
```

```python
import functools

import jax
import jax.numpy as jnp
from jax import lax
from jax.experimental import pallas as pl
from jax.experimental.pallas import tpu as pltpu
from jax.experimental.pallas import tpu_sc as plsc

D_MODEL = 1024
D_LRU = 512
LRU_BLOCKS = 8
LRU_BD = 64
CONV_W = 4
LRU_C = 8.0
N_HEADS = 8
QK_NOPE = 64
QK_ROPE = 32
QK_HEAD = 96
V_HEAD = 64
Q_LORA = 256
KV_LORA = 128
ROPE_THETA = 10000.0
N_EXPERTS = 32
TOP_K = 4
D_FF = 1024
SWIGLU_LIMIT = 7.0
SWIGLU_ALPHA = 1.702
MOE_BLOCK = 512
EPS = 1e-6

LANES = 128
HEAD_PAD = 128
ROPE_LO = QK_NOPE
ROPE_HALF = QK_ROPE // 2
D_IN_PAD = 2 * D_LRU + Q_LORA + KV_LORA + 2 * LANES
LOG2_E = 1.4426950408889634
HEAD_GROUP = 4

VMEM_LIMIT = 56 * 1024 * 1024

F32 = jnp.float32
BF16 = jnp.bfloat16


def _cparams(sem):
    return pltpu.CompilerParams(dimension_semantics=sem, vmem_limit_bytes=VMEM_LIMIT)


def _dot(a, b):
    return jnp.dot(a, b, preferred_element_type=F32)


def _dot_nt(a, b):
    return lax.dot_general(a, b, (((1,), (1,)), ((), ())), preferred_element_type=F32)


def _split_bf16(a):
    hi = a.astype(BF16)
    lo = (a - hi.astype(F32)).astype(BF16)
    return hi, lo


def _sigmoid(x):
    return 1.0 / (1.0 + jnp.exp(-x))


def _ada_kernel(c_ref, whi_ref, wlo_ref, b_ref, o_ref):
    c = c_ref[...]
    s = c * _sigmoid(c)
    shi, slo = _split_bf16(s)
    whi = whi_ref[...]
    o_ref[...] = _dot(shi, whi) + _dot(slo, whi) + _dot(shi, wlo_ref[...]) + b_ref[...]


def _ada(c, w_ada, b_ada):
    B, D = c.shape
    N = w_ada.shape[1]
    tn = 1024
    whi, wlo = _split_bf16(w_ada)
    return pl.pallas_call(
        _ada_kernel,
        grid=(N // tn,),
        in_specs=[
            pl.BlockSpec((B, D), lambda j: (0, 0)),
            pl.BlockSpec((D, tn), lambda j: (0, j)),
            pl.BlockSpec((D, tn), lambda j: (0, j)),
            pl.BlockSpec((1, tn), lambda j: (0, j)),
        ],
        out_specs=pl.BlockSpec((B, tn), lambda j: (0, j)),
        out_shape=jax.ShapeDtypeStruct((B, N), F32),
        compiler_params=_cparams(("arbitrary",)),
        name="ada",
    )(c, whi, wlo, b_ada.reshape(1, N))


def _trig_kernel(pos_ref, freq_ref, cos_ref, sin_ref):
    ang = pos_ref[...].astype(F32) * freq_ref[...]
    cos_ref[...] = jnp.cos(ang)
    sin_ref[...] = jnp.sin(ang)


def _rope_tables(positions):
    T = positions.size
    per_row = LANES // ROPE_HALF
    freqs = ROPE_THETA ** (-jnp.arange(ROPE_HALF, dtype=F32) / ROPE_HALF)
    pos_c = jnp.repeat(positions.reshape(T).astype(jnp.int32), ROPE_HALF).reshape(T // per_row, LANES)
    freq_c = jnp.tile(freqs, per_row).reshape(1, LANES)
    rows = T // per_row
    tr = min(512, rows)
    cos_c, sin_c = pl.pallas_call(
        _trig_kernel,
        grid=(rows // tr,),
        in_specs=[pl.BlockSpec((tr, LANES), lambda i: (i, 0)), pl.BlockSpec((1, LANES), lambda i: (0, 0))],
        out_specs=[pl.BlockSpec((tr, LANES), lambda i: (i, 0))] * 2,
        out_shape=[jax.ShapeDtypeStruct((rows, LANES), F32)] * 2,
        compiler_params=_cparams(("arbitrary",)),
        name="rope_trig",
    )(pos_c, freq_c)
    cos16 = cos_c.reshape(T, ROPE_HALF)
    sin16 = sin_c.reshape(T, ROPE_HALF)
    tail = LANES - ROPE_LO - QK_ROPE
    cos_t = jnp.concatenate([jnp.ones((T, ROPE_LO), F32), cos16, cos16, jnp.ones((T, tail), F32)], axis=1)
    sin_t = jnp.concatenate([jnp.zeros((T, ROPE_LO), F32), -sin16, sin16, jnp.zeros((T, tail), F32)], axis=1)
    return cos_t, sin_t


def _inproj_kernel(x_ref, cos_ref, sin_ref, shift_ref, scale_ref, gmix_ref, win_ref, gq_ref, wuq_ref,
                   gkv_ref, wukv_ref, gqn_ref, gqr_ref, gkn_ref, gkr_ref,
                   xlru_ref, ylru_ref, q_ref, k_ref, v_ref):
    HP = N_HEADS * HEAD_PAD
    x = x_ref[...]
    ms = jnp.mean(x * x, axis=-1, keepdims=True)
    xn = x * lax.rsqrt(ms + EPS) * gmix_ref[...]
    h = xn * (1.0 + scale_ref[0]) + shift_ref[0]
    z = _dot(h.astype(BF16), win_ref[...])
    xlru_ref[...] = z[:, :D_LRU]
    ylru_ref[...] = z[:, D_LRU:2 * D_LRU]
    o1 = 2 * D_LRU
    o2 = o1 + Q_LORA
    o3 = o2 + KV_LORA
    ql = z[:, o1:o2]
    kvl = z[:, o2:o3]
    kr = z[:, o3:o3 + LANES]
    kr_rot = z[:, o3 + LANES:]

    qn = ql * lax.rsqrt(jnp.mean(ql * ql, axis=-1, keepdims=True) + EPS) * gq_ref[...]
    qq = _dot(qn.astype(BF16), wuq_ref[...])
    kvn = kvl * lax.rsqrt(jnp.mean(kvl * kvl, axis=-1, keepdims=True) + EPS) * gkv_ref[...]
    kv = _dot(kvn.astype(BF16), wukv_ref[...])

    tm = x.shape[0]
    lane = lax.broadcasted_iota(jnp.int32, (tm, HP), 1)
    v_ref[...] = jnp.where((lane & (HEAD_PAD - 1)) == V_HEAD, 1.0, kv[:, HP:]).astype(BF16)

    cos_t = cos_ref[...]
    sin_t = sin_ref[...]
    gqn = gqn_ref[...]
    gkn = gkn_ref[...]
    cq = gqn * cos_t
    sq = gqr_ref[...] * sin_t
    kb = kr * (gkn * cos_t) + kr_rot * (gkr_ref[...] * sin_t)
    inv_w = 1.0 / QK_HEAD
    qscale = QK_HEAD ** -0.5 * LOG2_E
    for hh in range(N_HEADS):
        sl = slice(hh * HEAD_PAD, (hh + 1) * HEAD_PAD)
        qh = qq[:, sl]
        rq = lax.rsqrt(jnp.sum(qh * qh, axis=-1, keepdims=True) * inv_w + EPS) * qscale
        q_ref[:, sl] = ((qh * cq + qq[:, HP + hh * HEAD_PAD:HP + (hh + 1) * HEAD_PAD] * sq) * rq).astype(BF16)
        kraw = kv[:, sl] + kr
        rk = lax.rsqrt(jnp.sum(kraw * kraw, axis=-1, keepdims=True) * inv_w + EPS)
        k_ref[:, sl] = ((kv[:, sl] * gkn + kb) * rk).astype(BF16)


def _inproj(x2, cos_t, sin_t, mod3, g_mix, w_in_p, g_q_lat, w_uq_p, g_kv_lat, w_ukv_p,
            gqn_p, gqr_p, gkn_p, gkr_p, B, S, tm):
    T, D = x2.shape
    ns = S // tm
    HP = N_HEADS * HEAD_PAD
    row = lambda b, s: (b * ns + s, 0)
    full = lambda b, s: (0, 0)
    return pl.pallas_call(
        _inproj_kernel,
        grid=(B, ns),
        in_specs=[
            pl.BlockSpec((tm, D), row),
            pl.BlockSpec((tm, LANES), row),
            pl.BlockSpec((tm, LANES), row),
            pl.BlockSpec((1, 1, D), lambda b, s: (b * 6 + 0, 0, 0)),
            pl.BlockSpec((1, 1, D), lambda b, s: (b * 6 + 1, 0, 0)),
            pl.BlockSpec((1, D), full),
            pl.BlockSpec((D, D_IN_PAD), full),
            pl.BlockSpec((1, Q_LORA), full),
            pl.BlockSpec((Q_LORA, 2 * HP), full),
            pl.BlockSpec((1, KV_LORA), full),
            pl.BlockSpec((KV_LORA, 2 * HP), full),
            pl.BlockSpec((1, HEAD_PAD), full),
            pl.BlockSpec((1, HEAD_PAD), full),
            pl.BlockSpec((1, HEAD_PAD), full),
            pl.BlockSpec((1, HEAD_PAD), full),
        ],
        out_specs=[
            pl.BlockSpec((tm, D_LRU), row),
            pl.BlockSpec((tm, D_LRU), row),
            pl.BlockSpec((tm, HP), row),
            pl.BlockSpec((tm, HP), row),
            pl.BlockSpec((tm, HP), row),
        ],
        out_shape=[
            jax.ShapeDtypeStruct((T, D_LRU), F32),
            jax.ShapeDtypeStruct((T, D_LRU), F32),
            jax.ShapeDtypeStruct((T, HP), BF16),
            jax.ShapeDtypeStruct((T, HP), BF16),
            jax.ShapeDtypeStruct((T, HP), BF16),
        ],
        compiler_params=_cparams(("arbitrary", "arbitrary")),
        name="inproj",
    )(x2, cos_t, sin_t, mod3, mod3, g_mix, w_in_p, g_q_lat, w_uq_p, g_kv_lat, w_ukv_p,
      gqn_p, gqr_p, gkn_p, gkr_p)


def _gelu_tanh(x):
    return 0.5 * x * (1.0 + jnp.tanh(0.7978845608028654 * (x + 0.044715 * x * x * x)))


def _lru_kernel(x_ref, y_ref, cw_ref, cb_ref, wa_ref, ba_ref, wx_ref, bx_ref, lam_ref,
                o_ref, tail_ref, carry_ref):
    s = pl.program_id(1)

    @pl.when(s == 0)
    def _():
        tail_ref[...] = jnp.zeros_like(tail_ref)
        carry_ref[...] = jnp.zeros_like(carry_ref)

    x = x_ref[...]
    ts = x.shape[0]
    xext = jnp.concatenate([tail_ref[...], x], axis=0)
    cw = cw_ref[...]
    xc = x * cw[CONV_W - 1:CONV_W, :]
    for j in range(CONV_W - 1):
        sh = CONV_W - 1 - j
        xc = xc + xext[8 - sh:8 - sh + ts, :] * cw[j:j + 1, :]
    xc = xc + cb_ref[...]
    tail_ref[...] = x[ts - 8:, :]

    xb = xc.astype(BF16)
    r = _sigmoid(_dot(xb, wa_ref[...]) + ba_ref[...])
    i = _sigmoid(_dot(xb, wx_ref[...]) + bx_ref[...])
    lam = lam_ref[...]
    nl = -lam
    softplus = jnp.maximum(nl, 0.0) + jnp.log(1.0 + jnp.exp(-jnp.abs(nl)))
    log_a = (-LRU_C) * r * softplus
    a = jnp.exp(log_a)
    mult = jnp.sqrt(1.0 - jnp.exp(2.0 * log_a))
    u = mult * (i * xc)

    rowi = lax.broadcasted_iota(jnp.int32, (ts, 1), 0)
    sh = 1
    while sh < ts:
        a_prev = pltpu.roll(a, sh, axis=0)
        u_prev = pltpu.roll(u, sh, axis=0)
        m = rowi >= sh
        u = jnp.where(m, a * u_prev + u, u)
        a = jnp.where(m, a * a_prev, a)
        sh *= 2
    hcar = carry_ref[0:1, :]
    hs = u + a * hcar
    carry_ref[...] = jnp.broadcast_to(hs[ts - 1:ts, :], carry_ref.shape)
    o_ref[...] = (_gelu_tanh(y_ref[...]) * hs).astype(BF16)


def _lru(xlru, ylru, conv_w, conv_b, wa_d, b_a, wx_d, b_x, lam, B, S, ts):
    T, C = xlru.shape
    ns = S // ts
    row = lambda b, s: (b * ns + s, 0)
    full = lambda b, s: (0, 0)
    return pl.pallas_call(
        _lru_kernel,
        grid=(B, ns),
        in_specs=[
            pl.BlockSpec((ts, C), row),
            pl.BlockSpec((ts, C), row),
            pl.BlockSpec((CONV_W, C), full),
            pl.BlockSpec((1, C), full),
            pl.BlockSpec((C, C), full),
            pl.BlockSpec((1, C), full),
            pl.BlockSpec((C, C), full),
            pl.BlockSpec((1, C), full),
            pl.BlockSpec((1, C), full),
        ],
        out_specs=pl.BlockSpec((ts, C), row),
        out_shape=jax.ShapeDtypeStruct((T, C), BF16),
        scratch_shapes=[pltpu.VMEM((8, C), F32), pltpu.VMEM((8, C), F32)],
        compiler_params=_cparams(("arbitrary", "arbitrary")),
        name="lru",
    )(xlru, ylru, conv_w, conv_b, wa_d, b_a, wx_d, b_x, lam)


NEG_INF = -1e30


def _attn_kernel(q_ref, k_ref, v_ref, o_ref, *, tq):
    qi = pl.program_id(1)
    rowi = lax.broadcasted_iota(jnp.int32, (tq, tq), 0)
    coli = lax.broadcasted_iota(jnp.int32, (tq, tq), 1)
    diag_mask = coli <= rowi

    for h0 in range(0, N_HEADS, HEAD_GROUP):
        heads = range(h0, h0 + HEAD_GROUP)
        qhs = [q_ref[:, hh * HEAD_PAD:(hh + 1) * HEAD_PAD] for hh in heads]

        def step(j, carry, masked):
            r0 = pl.multiple_of(j * tq, tq)
            out = []
            for gi, hh in enumerate(heads):
                m, acc = carry[gi]
                hs = slice(hh * HEAD_PAD, (hh + 1) * HEAD_PAD)
                kh = k_ref[pl.ds(r0, tq), hs]
                vh = v_ref[pl.ds(r0, tq), hs]
                sc = _dot_nt(qhs[gi], kh)
                if masked:
                    sc = jnp.where(diag_mask, sc, NEG_INF)
                m_new = jnp.maximum(m, jnp.max(sc, axis=-1, keepdims=True))
                alpha = jnp.exp2(m - m_new)
                p = jnp.exp2(sc - m_new)
                out.append((m_new, alpha * acc + _dot(p.astype(BF16), vh)))
            return tuple(out)

        init = tuple((jnp.full((tq, 1), NEG_INF, F32), jnp.zeros((tq, HEAD_PAD), F32))
                     for _ in heads)
        carry = lax.fori_loop(0, qi, functools.partial(step, masked=False), init)
        carry = step(qi, carry, True)
        for gi, hh in enumerate(heads):
            acc = carry[gi][1]
            o = acc[:, :V_HEAD] / acc[:, V_HEAD:V_HEAD + 1]
            o_ref[:, hh * V_HEAD:(hh + 1) * V_HEAD] = o.astype(BF16)


def _attn(qp, kp, v, B, S, tq):
    T = qp.shape[0]
    nq = S // tq
    HP = N_HEADS * HEAD_PAD
    HV = N_HEADS * V_HEAD
    return pl.pallas_call(
        functools.partial(_attn_kernel, tq=tq),
        grid=(B, nq),
        in_specs=[
            pl.BlockSpec((tq, HP), lambda b, i: (b * nq + i, 0)),
            pl.BlockSpec((S, HP), lambda b, i: (b, 0)),
            pl.BlockSpec((S, HP), lambda b, i: (b, 0)),
        ],
        out_specs=pl.BlockSpec((tq, HV), lambda b, i: (b * nq + i, 0)),
        out_shape=jax.ShapeDtypeStruct((T, HV), BF16),
        compiler_params=_cparams(("arbitrary", "arbitrary")),
        name="attn",
    )(qp, kp, v)


def _outproj_kernel(lru_ref, att_ref, x_ref, gate_ref, shift_ref, scale_ref, gffn_ref,
                    wo1_ref, wo2_ref, wrh_ref, wrl_ref, br_ref, tri_ref,
                    x1_ref, h2p_ref, idx_ref, gat_ref, rank_ref, cnt_ref, run_ref):
    first = (pl.program_id(0) == 0) & (pl.program_id(1) == 0)

    @pl.when(first)
    def _():
        run_ref[...] = jnp.zeros_like(run_ref)

    mix = _dot(lru_ref[...], wo1_ref[...]) + _dot(att_ref[...], wo2_ref[...])
    x1 = x_ref[...] + gate_ref[0] * mix
    x1_ref[...] = x1
    ms = jnp.mean(x1 * x1, axis=-1, keepdims=True)
    h2 = x1 * lax.rsqrt(ms + EPS) * gffn_ref[...]
    h2 = h2 * (1.0 + scale_ref[0]) + shift_ref[0]

    hhi = h2.astype(BF16)
    hhi32 = hhi.astype(F32)
    hlo = (h2 - hhi32).astype(BF16)
    bits = lax.bitcast_convert_type(hhi32, jnp.uint32)
    half = h2.shape[1] // 2
    words = (bits[:, :half] >> 16) | (bits[:, half:] & jnp.uint32(0xFFFF0000))
    h2p_ref[...] = lax.bitcast_convert_type(words, jnp.int32)

    wrh = wrh_ref[...]
    logits = _dot_nt(wrh, hhi) + _dot_nt(wrh, hlo) + _dot_nt(wrl_ref[...], hhi) + br_ref[...]
    ne, tm = logits.shape
    eio = lax.broadcasted_iota(jnp.int32, (ne, tm), 0)
    vals, idxs, sels = [], [], []
    l = logits
    for _ in range(TOP_K):
        m = jnp.max(l, axis=0, keepdims=True)
        idx = jnp.min(jnp.where(l == m, eio, ne), axis=0, keepdims=True)
        sel = eio == idx
        l = jnp.where(sel, -jnp.inf, l)
        vals.append(m)
        idxs.append(idx)
        sels.append(sel)
    es = [jnp.exp(v - vals[0]) for v in vals]
    den = es[0] + es[1] + es[2] + es[3]
    inv = 1.0 / den
    sel_any = jnp.where(sels[0] | sels[1] | sels[2] | sels[3], 1.0, 0.0)
    excl = _dot(sel_any.astype(BF16), tri_ref[...]) + run_ref[...]
    for kk in range(TOP_K):
        idx_ref[kk:kk + 1, :] = idxs[kk]
        gat_ref[kk:kk + 1, :] = es[kk] * inv
        rk = jnp.sum(jnp.where(sels[kk], excl, 0.0), axis=0, keepdims=True)
        rank_ref[kk:kk + 1, :] = rk.astype(jnp.int32)
    run = run_ref[...] + jnp.sum(sel_any, axis=1, keepdims=True)
    run_ref[...] = run
    cnt_ref[...] = run.astype(jnp.int32)


def _outproj(lru_o, att_o, x2, mod3, g_ffn, wo1, wo2, wr_hi, wr_lo, b_r, tri, B, S, tm):
    T, D = x2.shape
    ns = S // tm
    C = lru_o.shape[1]
    row = lambda b, s: (b * ns + s, 0)
    col = lambda b, s: (0, b * ns + s)
    full = lambda b, s: (0, 0)
    return pl.pallas_call(
        _outproj_kernel,
        grid=(B, ns),
        in_specs=[
            pl.BlockSpec((tm, C), row),
            pl.BlockSpec((tm, C), row),
            pl.BlockSpec((tm, D), row),
            pl.BlockSpec((1, 1, D), lambda b, s: (b * 6 + 2, 0, 0)),
            pl.BlockSpec((1, 1, D), lambda b, s: (b * 6 + 3, 0, 0)),
            pl.BlockSpec((1, 1, D), lambda b, s: (b * 6 + 4, 0, 0)),
            pl.BlockSpec((1, D), full),
            pl.BlockSpec((C, D), full),
            pl.BlockSpec((C, D), full),
            pl.BlockSpec((N_EXPERTS, D), full),
            pl.BlockSpec((N_EXPERTS, D), full),
            pl.BlockSpec((N_EXPERTS, 1), full),
            pl.BlockSpec((tm, tm), full),
        ],
        out_specs=[
            pl.BlockSpec((tm, D), row),
            pl.BlockSpec((tm, D // 2), row),
            pl.BlockSpec((TOP_K, tm), col),
            pl.BlockSpec((TOP_K, tm), col),
            pl.BlockSpec((TOP_K, tm), col),
            pl.BlockSpec((N_EXPERTS, 1), full),
        ],
        out_shape=[
            jax.ShapeDtypeStruct((T, D), F32),
            jax.ShapeDtypeStruct((T, D // 2), jnp.int32),
            jax.ShapeDtypeStruct((TOP_K, T), jnp.int32),
            jax.ShapeDtypeStruct((TOP_K, T), F32),
            jax.ShapeDtypeStruct((TOP_K, T), jnp.int32),
            jax.ShapeDtypeStruct((N_EXPERTS, 1), jnp.int32),
        ],
        scratch_shapes=[pltpu.VMEM((N_EXPERTS, 1), F32)],
        compiler_params=_cparams(("arbitrary", "arbitrary")),
        name="outproj",
    )(lru_o, att_o, x2, mod3, mod3, mod3, g_ffn, wo1, wo2, wr_hi, wr_lo, b_r, tri)


SC_CORES = 2
SC_SUBCORES = 16
SC_WORKERS = SC_CORES * SC_SUBCORES


def _sc_mesh():
    return plsc.VectorSubcoreMesh(core_axis_name="c", subcore_axis_name="s",
                                  num_cores=SC_CORES, num_subcores=SC_SUBCORES)


def _sc_worker_id():
    return lax.axis_index("s") * SC_CORES + lax.axis_index("c")


def _sc_scatter_rows(rows, idx, n_out, g):
    T, W = rows.shape
    K = idx.shape[0]
    per_w = T // SC_WORKERS
    nch = per_w // g
    assert per_w * SC_WORKERS == T and nch * g == per_w and nch % 2 == 0
    idx_w = idx.reshape(K, SC_WORKERS, nch, g).transpose(1, 2, 0, 3).reshape(SC_WORKERS, nch * K, g)

    def body(rows_hbm, idx_hbm, out_hbm, idx_v, buf0, buf1, semr0, semr1, semw):
        wid = _sc_worker_id()
        base = wid * per_w
        pltpu.sync_copy(idx_hbm.at[wid], idx_v)

        def read(j, buf, sem):
            return pltpu.make_async_copy(rows_hbm.at[pl.ds(base + j * g, g)], buf, sem)

        def scatter(j, buf):
            copies = [pltpu.async_copy(buf, out_hbm.at[idx_v.at[j * K + kk]], semw)
                      for kk in range(K)]
            for cp in copies:
                cp.wait()

        read(0, buf0, semr0).start()

        @pl.loop(0, nch // 2)
        def _(jj):
            j0 = 2 * jj
            read(j0 + 1, buf1, semr1).start()
            read(j0, buf0, semr0).wait()
            scatter(j0, buf0)

            @pl.when(j0 + 2 < nch)
            def _():
                read(j0 + 2, buf0, semr0).start()

            read(j0 + 1, buf1, semr1).wait()
            scatter(j0 + 1, buf1)

    return pl.kernel(
        body,
        out_type=jax.ShapeDtypeStruct((n_out, W), rows.dtype),
        mesh=_sc_mesh(),
        scratch_types=[
            pltpu.VMEM((nch * K, g), jnp.int32),
            pltpu.VMEM((g, W), rows.dtype),
            pltpu.VMEM((g, W), rows.dtype),
            pltpu.SemaphoreType.DMA,
            pltpu.SemaphoreType.DMA,
            pltpu.SemaphoreType.DMA,
        ],
        name="sc_scatter_rows",
    )(rows, idx_w)


def _sc_gather_rows(table, idx, g):
    W = table.shape[1]
    N = idx.shape[0]
    per_w = N // SC_WORKERS
    nch = per_w // g
    assert per_w * SC_WORKERS == N and nch * g == per_w and nch % 2 == 0
    idx_w = idx.reshape(SC_WORKERS, nch, g)

    def body(table_hbm, idx_hbm, out_hbm, idx_v, buf0, buf1, sem0, sem1):
        wid = _sc_worker_id()
        base = wid * per_w
        pltpu.sync_copy(idx_hbm.at[wid], idx_v)

        def gather(j, buf, sem):
            return pltpu.make_async_copy(table_hbm.at[idx_v.at[j]], buf, sem)

        def put(j, buf):
            pltpu.sync_copy(buf, out_hbm.at[pl.ds(base + j * g, g)])

        gather(0, buf0, sem0).start()

        @pl.loop(0, nch // 2)
        def _(jj):
            j0 = 2 * jj
            gather(j0 + 1, buf1, sem1).start()
            gather(j0, buf0, sem0).wait()
            put(j0, buf0)

            @pl.when(j0 + 2 < nch)
            def _():
                gather(j0 + 2, buf0, sem0).start()

            gather(j0 + 1, buf1, sem1).wait()
            put(j0 + 1, buf1)

    return pl.kernel(
        body,
        out_type=jax.ShapeDtypeStruct((N, W), table.dtype),
        mesh=_sc_mesh(),
        scratch_types=[
            pltpu.VMEM((nch, g), jnp.int32),
            pltpu.VMEM((g, W), table.dtype),
            pltpu.VMEM((g, W), table.dtype),
            pltpu.SemaphoreType.DMA,
            pltpu.SemaphoreType.DMA,
        ],
        name="sc_gather_rows",
    )(table, idx_w)


def _experts_kernel(be_ref, bv_ref, xs_ref, w1_ref, b1_ref, w2_ref, b2_ref, ys_ref):
    i = pl.program_id(0)

    nvalid = bv_ref[i]

    @pl.when(nvalid > 0)
    def _():
        xw = lax.bitcast_convert_type(xs_ref[...], jnp.uint32)
        rowi = lax.broadcasted_iota(jnp.int32, (xw.shape[0], 1), 0)
        xw = jnp.where(rowi < nvalid, xw, jnp.uint32(0))
        lo = lax.bitcast_convert_type(xw << 16, F32).astype(BF16)
        hi = lax.bitcast_convert_type(xw & jnp.uint32(0xFFFF0000), F32).astype(BF16)
        half = xw.shape[1]
        gu = _dot(lo, w1_ref[0, :half, :]) + _dot(hi, w1_ref[0, half:, :]) + b1_ref[0]
        glu = jnp.minimum(gu[:, :D_FF], SWIGLU_LIMIT)
        lin = jnp.clip(gu[:, D_FF:], -SWIGLU_LIMIT, SWIGLU_LIMIT)
        act = (lin + 1.0) * (glu * _sigmoid(SWIGLU_ALPHA * glu))
        ys_ref[...] = _dot(act.astype(BF16), w2_ref[0]) + b2_ref[0]


def _experts(blk_e, blk_v, blk_r, xs, w1b, b1, w2b, b2):
    P, W = xs.shape
    nb = P // MOE_BLOCK
    E, D, F2 = w1b.shape
    grid_spec = pltpu.PrefetchScalarGridSpec(
        num_scalar_prefetch=3,
        grid=(nb,),
        in_specs=[
            pl.BlockSpec((MOE_BLOCK, W), lambda i, be, bv, br: (br[i], 0)),
            pl.BlockSpec((1, D, F2), lambda i, be, bv, br: (be[i], 0, 0)),
            pl.BlockSpec((1, 1, F2), lambda i, be, bv, br: (be[i], 0, 0)),
            pl.BlockSpec((1, D_FF, D), lambda i, be, bv, br: (be[i], 0, 0)),
            pl.BlockSpec((1, 1, D), lambda i, be, bv, br: (be[i], 0, 0)),
        ],
        out_specs=pl.BlockSpec((MOE_BLOCK, D), lambda i, be, bv, br: (br[i], 0)),
    )

    def kern(be_ref, bv_ref, br_ref, *refs):
        del br_ref
        _experts_kernel(be_ref, bv_ref, *refs)

    return pl.pallas_call(
        kern,
        grid_spec=grid_spec,
        out_shape=jax.ShapeDtypeStruct((P, D), F32),
        compiler_params=_cparams(("arbitrary",)),
        name="experts",
    )(blk_e, blk_v, blk_r, xs, w1b, b1.reshape(E, 1, F2), w2b, b2.reshape(E, 1, D))


def _combine_kernel(x1_ref, g_ref, gate_ref, y0_ref, y1_ref, y2_ref, y3_ref, o_ref):
    g = g_ref[...]
    acc = y0_ref[0] * g[:, 0:1]
    for kk, y_ref in enumerate((y1_ref, y2_ref, y3_ref), start=1):
        acc = acc + y_ref[0] * g[:, kk:kk + 1]
    o_ref[...] = x1_ref[...] + gate_ref[0] * acc


def _combine(x1, gates_tk, mod3, ysg, B, S, tm):
    T, D = x1.shape
    ns = S // tm
    row = lambda b, s: (b * ns + s, 0)

    def yspec(kk):
        return pl.BlockSpec((1, tm, D), lambda b, s: (kk, b * ns + s, 0))

    return pl.pallas_call(
        _combine_kernel,
        grid=(B, ns),
        in_specs=[
            pl.BlockSpec((tm, D), row),
            pl.BlockSpec((tm, TOP_K), row),
            pl.BlockSpec((1, 1, D), lambda b, s: (b * 6 + 5, 0, 0)),
            yspec(0), yspec(1), yspec(2), yspec(3),
        ],
        out_specs=pl.BlockSpec((tm, D), row),
        out_shape=jax.ShapeDtypeStruct((T, D), F32),
        compiler_params=_cparams(("arbitrary", "arbitrary")),
        name="combine",
    )(x1, gates_tk, mod3, ysg, ysg, ysg, ysg)


def _block_diag(w):
    n, c, d = w.shape
    eye = jnp.eye(n, dtype=w.dtype)
    return jnp.einsum("ncd,nm->ncmd", w, eye).reshape(n * c, n * d)


def _pad_heads(w, width):
    k = w.shape[0]
    w = w.reshape(k, N_HEADS, width)
    return jnp.pad(w, ((0, 0), (0, 0), (0, HEAD_PAD - width))).reshape(k, N_HEADS * HEAD_PAD)


def kernel(x, c, positions, w_ada, b_ada, g_mix, w_in, conv_w, conv_b, w_a, b_a, w_x, b_x, lam,
           g_q_lat, w_uq, g_kv_lat, w_ukv, g_qn, g_kn, w_out, g_ffn, w_router, b_router,
           w1, b1, w2, b2):
    B, S, D = x.shape
    T = B * S
    depth = w_ada.shape[0]
    tm_in = min(512, S)
    ts_lru = min(256, S)
    tq = min(512, S)
    tm_out = min(512, S)
    tm_comb = min(512, S)
    g_disp = min(64, T // SC_WORKERS // 2)
    g_comb = min(32, T * TOP_K // SC_WORKERS // 2)

    o1 = 2 * D_LRU
    o2 = o1 + Q_LORA
    o3 = o2 + KV_LORA
    tri = (jnp.arange(tm_out)[:, None] < jnp.arange(tm_out)[None, :]).astype(BF16)
    cos_t, sin_t = _rope_tables(positions)
    lane = jnp.arange(HEAD_PAD, dtype=jnp.int32)
    first = (lane >= ROPE_LO) & (lane < ROPE_LO + ROPE_HALF)
    second = (lane >= ROPE_LO + ROPE_HALF) & (lane < ROPE_LO + QK_ROPE)
    partner = jnp.where(first, lane + ROPE_HALF, jnp.where(second, lane - ROPE_HALF, 0))
    is_rot = first | second

    def rot_cols(w):
        k = w.shape[0]
        w3 = w.reshape(k, -1, HEAD_PAD)
        return jnp.where(is_rot[None, None, :], w3[:, :, partner], 0.0).reshape(w.shape)

    x2 = x.reshape(T, D)
    for l in range(depth):
        mod3 = _ada(c, w_ada[l], b_ada[l]).reshape(B * 6, 1, D)

        w_in_l = w_in[l]
        kr_cols = jnp.pad(w_in_l[:, o3:], ((0, 0), (ROPE_LO, LANES - ROPE_LO - QK_ROPE)))
        w_in_p = jnp.concatenate([w_in_l[:, :o3], kr_cols, rot_cols(kr_cols)], axis=1).astype(BF16)
        w_uq_h = _pad_heads(w_uq[l], QK_HEAD)
        w_uq_p = jnp.concatenate([w_uq_h, rot_cols(w_uq_h)], axis=1).astype(BF16)
        w_ukv_l = w_ukv[l].reshape(KV_LORA, N_HEADS, QK_NOPE + V_HEAD)
        w_uk_h = _pad_heads(w_ukv_l[:, :, :QK_NOPE].reshape(KV_LORA, N_HEADS * QK_NOPE), QK_NOPE)
        w_uv_h = _pad_heads(w_ukv_l[:, :, QK_NOPE:].reshape(KV_LORA, N_HEADS * V_HEAD), V_HEAD)
        w_ukv_p = jnp.concatenate([w_uk_h, w_uv_h], axis=1).astype(BF16)
        gqn_p = jnp.pad(g_qn[l], (0, HEAD_PAD - QK_HEAD)).reshape(1, HEAD_PAD)
        gkn_p = jnp.pad(g_kn[l], (0, HEAD_PAD - QK_HEAD)).reshape(1, HEAD_PAD)

        xlru, ylru, qp, kp, v = _inproj(
            x2, cos_t, sin_t, mod3, g_mix[l].reshape(1, D), w_in_p, g_q_lat[l].reshape(1, Q_LORA),
            w_uq_p, g_kv_lat[l].reshape(1, KV_LORA), w_ukv_p, gqn_p, rot_cols(gqn_p), gkn_p,
            rot_cols(gkn_p), B, S, tm_in)

        lru_o = _lru(xlru, ylru, conv_w[l], conv_b[l].reshape(1, D_LRU),
                     _block_diag(w_a[l]).astype(BF16), b_a[l].reshape(1, D_LRU),
                     _block_diag(w_x[l]).astype(BF16), b_x[l].reshape(1, D_LRU),
                     lam[l].reshape(1, D_LRU), B, S, ts_lru)

        att_o = _attn(qp, kp, v, B, S, tq)

        w_out_b = w_out[l].astype(BF16)
        wr_hi, wr_lo = _split_bf16(w_router[l].T)
        x1, h2p, idx_t, gat_t, rank_t, counts = _outproj(
            lru_o, att_o, x2, mod3, g_ffn[l].reshape(1, D), w_out_b[:D_LRU], w_out_b[D_LRU:],
            wr_hi, wr_lo, b_router[l].reshape(N_EXPERTS, 1), tri, B, S, tm_out)

        counts = counts.reshape(N_EXPERTS)
        nblk_e = (counts + MOE_BLOCK - 1) // MOE_BLOCK
        blk_end = jnp.cumsum(nblk_e)
        pad_start = (blk_end - nblk_e) * MOE_BLOCK
        n_blocks = -(-(T * TOP_K) // MOE_BLOCK) + N_EXPERTS
        bi = jnp.arange(n_blocks, dtype=jnp.int32)
        total = blk_end[-1]
        blk_r = jnp.minimum(bi, total - 1).astype(jnp.int32)
        blk_e = jnp.minimum(jnp.sum(blk_end[None, :] <= blk_r[:, None], axis=1),
                            N_EXPERTS - 1).astype(jnp.int32)
        eio = jnp.arange(N_EXPERTS, dtype=jnp.int32)
        blk_onehot = blk_e[:, None] == eio[None, :]
        blk_first = jnp.sum(jnp.where(blk_onehot, (blk_end - nblk_e)[None, :], 0), axis=1)
        blk_cnt = jnp.sum(jnp.where(blk_onehot, counts[None, :], 0), axis=1)
        blk_n = jnp.where(bi < total, jnp.clip(blk_cnt - (bi - blk_first) * MOE_BLOCK, 0, MOE_BLOCK),
                          0).astype(jnp.int32)
        slot0 = jnp.sum(jnp.where(idx_t[None] == eio[:, None, None], pad_start[:, None, None], 0),
                        axis=0)
        dest = slot0.astype(jnp.int32) + rank_t

        xs = _sc_scatter_rows(h2p, dest, n_blocks * MOE_BLOCK, g_disp)
        ys = _experts(blk_e, blk_n, blk_r, xs, w1[l].astype(BF16), b1[l], w2[l].astype(BF16), b2[l])
        ysg = _sc_gather_rows(ys, dest.reshape(TOP_K * T), g_comb).reshape(TOP_K, T, D)
        x2 = _combine(x1, gat_t.T, mod3, ysg, B, S, tm_comb)
    return x2.reshape(B, S, D)
```

```python
import functools

import jax
import jax.numpy as jnp
from jax import lax
from jax.experimental import pallas as pl
from jax.experimental.pallas import tpu as pltpu
from jax.experimental.pallas import tpu_sc as plsc

D_MODEL = 1024
D_LRU = 512
LRU_BLOCKS = 8
LRU_BD = 64
CONV_W = 4
LRU_C = 8.0
N_HEADS = 8
QK_NOPE = 64
QK_ROPE = 32
QK_HEAD = 96
V_HEAD = 64
Q_LORA = 256
KV_LORA = 128
ROPE_THETA = 10000.0
N_EXPERTS = 32
TOP_K = 4
D_FF = 1024
SWIGLU_LIMIT = 7.0
SWIGLU_ALPHA = 1.702
MOE_BLOCK = 512
EPS = 1e-6

LANES = 128
HEAD_PAD = 128
ROPE_LO = QK_NOPE
ROPE_HALF = QK_ROPE // 2
D_IN_PAD = 2 * D_LRU + Q_LORA + KV_LORA + 2 * LANES
LOG2_E = 1.4426950408889634
HEAD_GROUP = 4

VMEM_LIMIT = 56 * 1024 * 1024

F32 = jnp.float32
BF16 = jnp.bfloat16


def _cparams(sem):
    return pltpu.CompilerParams(dimension_semantics=sem, vmem_limit_bytes=VMEM_LIMIT)


def _dot(a, b):
    return jnp.dot(a, b, preferred_element_type=F32)


def _dot_nt(a, b):
    return lax.dot_general(a, b, (((1,), (1,)), ((), ())), preferred_element_type=F32)


def _split_bf16(a):
    hi = a.astype(BF16)
    lo = (a - hi.astype(F32)).astype(BF16)
    return hi, lo


def _sigmoid(x):
    return 1.0 / (1.0 + jnp.exp(-x))


def _pack_halves(x):
    bits = lax.bitcast_convert_type(x.astype(BF16).astype(F32), jnp.uint32)
    half = x.shape[1] // 2
    words = (bits[:, :half] >> 16) | (bits[:, half:] & jnp.uint32(0xFFFF0000))
    return lax.bitcast_convert_type(words, jnp.int32)


def _unpack_halves(words):
    w = lax.bitcast_convert_type(words, jnp.uint32)
    lo = lax.bitcast_convert_type(w << 16, F32)
    hi = lax.bitcast_convert_type(w & jnp.uint32(0xFFFF0000), F32)
    return lo, hi


def _ada_kernel(c_ref, whi_ref, wlo_ref, b_ref, o_ref):
    c = c_ref[...]
    s = c * _sigmoid(c)
    shi, slo = _split_bf16(s)
    whi = whi_ref[...]
    o_ref[...] = _dot(shi, whi) + _dot(slo, whi) + _dot(shi, wlo_ref[...]) + b_ref[...]


def _ada(c, w_ada, b_ada):
    B, D = c.shape
    N = w_ada.shape[1]
    tn = 1024
    whi, wlo = _split_bf16(w_ada)
    return pl.pallas_call(
        _ada_kernel,
        grid=(N // tn,),
        in_specs=[
            pl.BlockSpec((B, D), lambda j: (0, 0)),
            pl.BlockSpec((D, tn), lambda j: (0, j)),
            pl.BlockSpec((D, tn), lambda j: (0, j)),
            pl.BlockSpec((1, tn), lambda j: (0, j)),
        ],
        out_specs=pl.BlockSpec((B, tn), lambda j: (0, j)),
        out_shape=jax.ShapeDtypeStruct((B, N), F32),
        compiler_params=_cparams(("arbitrary",)),
        name="ada",
    )(c, whi, wlo, b_ada.reshape(1, N))


def _trig_kernel(pos_ref, freq_ref, cos_ref, sin_ref):
    ang = pos_ref[...].astype(F32) * freq_ref[...]
    cos_ref[...] = jnp.cos(ang)
    sin_ref[...] = jnp.sin(ang)


def _rope_tables(positions):
    T = positions.size
    per_row = LANES // ROPE_HALF
    freqs = ROPE_THETA ** (-jnp.arange(ROPE_HALF, dtype=F32) / ROPE_HALF)
    pos_c = jnp.repeat(positions.reshape(T).astype(jnp.int32), ROPE_HALF).reshape(T // per_row, LANES)
    freq_c = jnp.tile(freqs, per_row).reshape(1, LANES)
    rows = T // per_row
    tr = min(512, rows)
    cos_c, sin_c = pl.pallas_call(
        _trig_kernel,
        grid=(rows // tr,),
        in_specs=[pl.BlockSpec((tr, LANES), lambda i: (i, 0)), pl.BlockSpec((1, LANES), lambda i: (0, 0))],
        out_specs=[pl.BlockSpec((tr, LANES), lambda i: (i, 0))] * 2,
        out_shape=[jax.ShapeDtypeStruct((rows, LANES), F32)] * 2,
        compiler_params=_cparams(("arbitrary",)),
        name="rope_trig",
    )(pos_c, freq_c)
    cos16 = cos_c.reshape(T, ROPE_HALF)
    sin16 = sin_c.reshape(T, ROPE_HALF)
    tail = LANES - ROPE_LO - QK_ROPE
    cos_t = jnp.concatenate([jnp.ones((T, ROPE_LO), F32), cos16, cos16, jnp.ones((T, tail), F32)], axis=1)
    sin_t = jnp.concatenate([jnp.zeros((T, ROPE_LO), F32), -sin16, sin16, jnp.zeros((T, tail), F32)], axis=1)
    return cos_t, sin_t


def _inproj_kernel(x_ref, cos_ref, sin_ref, shift_ref, scale_ref, gmix_ref, win_ref, gq_ref, wuq_ref,
                   gkv_ref, wukv_ref, gqn_ref, gqr_ref, gkn_ref, gkr_ref,
                   xlru_ref, ylru_ref, q_ref, k_ref, v_ref):
    HP = N_HEADS * HEAD_PAD
    x = x_ref[...]
    ms = jnp.mean(x * x, axis=-1, keepdims=True)
    xn = x * lax.rsqrt(ms + EPS) * gmix_ref[...]
    h = xn * (1.0 + scale_ref[0]) + shift_ref[0]
    z = _dot(h.astype(BF16), win_ref[...])
    xlru_ref[...] = z[:, :D_LRU]
    ylru_ref[...] = z[:, D_LRU:2 * D_LRU]
    o1 = 2 * D_LRU
    o2 = o1 + Q_LORA
    o3 = o2 + KV_LORA
    ql = z[:, o1:o2]
    kvl = z[:, o2:o3]
    kr = z[:, o3:o3 + LANES]
    kr_rot = z[:, o3 + LANES:]

    qn = ql * lax.rsqrt(jnp.mean(ql * ql, axis=-1, keepdims=True) + EPS) * gq_ref[...]
    qq = _dot(qn.astype(BF16), wuq_ref[...])
    kvn = kvl * lax.rsqrt(jnp.mean(kvl * kvl, axis=-1, keepdims=True) + EPS) * gkv_ref[...]
    kv = _dot(kvn.astype(BF16), wukv_ref[...])

    tm = x.shape[0]
    lane = lax.broadcasted_iota(jnp.int32, (tm, HP), 1)
    v_ref[...] = jnp.where((lane & (HEAD_PAD - 1)) == V_HEAD, 1.0, kv[:, HP:]).astype(BF16)

    cos_t = cos_ref[...]
    sin_t = sin_ref[...]
    gqn = gqn_ref[...]
    gkn = gkn_ref[...]
    cq = gqn * cos_t
    sq = gqr_ref[...] * sin_t
    kb = kr * (gkn * cos_t) + kr_rot * (gkr_ref[...] * sin_t)
    inv_w = 1.0 / QK_HEAD
    qscale = QK_HEAD ** -0.5 * LOG2_E
    for hh in range(N_HEADS):
        sl = slice(hh * HEAD_PAD, (hh + 1) * HEAD_PAD)
        qh = qq[:, sl]
        rq = lax.rsqrt(jnp.sum(qh * qh, axis=-1, keepdims=True) * inv_w + EPS) * qscale
        q_ref[:, sl] = ((qh * cq + qq[:, HP + hh * HEAD_PAD:HP + (hh + 1) * HEAD_PAD] * sq) * rq).astype(BF16)
        kraw = kv[:, sl] + kr
        rk = lax.rsqrt(jnp.sum(kraw * kraw, axis=-1, keepdims=True) * inv_w + EPS)
        k_ref[:, sl] = ((kv[:, sl] * gkn + kb) * rk).astype(BF16)


def _inproj(x2, cos_t, sin_t, mod3, g_mix, w_in_p, g_q_lat, w_uq_p, g_kv_lat, w_ukv_p,
            gqn_p, gqr_p, gkn_p, gkr_p, B, S, tm):
    T, D = x2.shape
    ns = S // tm
    HP = N_HEADS * HEAD_PAD
    row = lambda b, s: (b * ns + s, 0)
    full = lambda b, s: (0, 0)
    return pl.pallas_call(
        _inproj_kernel,
        grid=(B, ns),
        in_specs=[
            pl.BlockSpec((tm, D), row),
            pl.BlockSpec((tm, LANES), row),
            pl.BlockSpec((tm, LANES), row),
            pl.BlockSpec((1, 1, D), lambda b, s: (b * 6 + 0, 0, 0)),
            pl.BlockSpec((1, 1, D), lambda b, s: (b * 6 + 1, 0, 0)),
            pl.BlockSpec((1, D), full),
            pl.BlockSpec((D, D_IN_PAD), full),
            pl.BlockSpec((1, Q_LORA), full),
            pl.BlockSpec((Q_LORA, 2 * HP), full),
            pl.BlockSpec((1, KV_LORA), full),
            pl.BlockSpec((KV_LORA, 2 * HP), full),
            pl.BlockSpec((1, HEAD_PAD), full),
            pl.BlockSpec((1, HEAD_PAD), full),
            pl.BlockSpec((1, HEAD_PAD), full),
            pl.BlockSpec((1, HEAD_PAD), full),
        ],
        out_specs=[
            pl.BlockSpec((tm, D_LRU), row),
            pl.BlockSpec((tm, D_LRU), row),
            pl.BlockSpec((tm, HP), row),
            pl.BlockSpec((tm, HP), row),
            pl.BlockSpec((tm, HP), row),
        ],
        out_shape=[
            jax.ShapeDtypeStruct((T, D_LRU), F32),
            jax.ShapeDtypeStruct((T, D_LRU), F32),
            jax.ShapeDtypeStruct((T, HP), BF16),
            jax.ShapeDtypeStruct((T, HP), BF16),
            jax.ShapeDtypeStruct((T, HP), BF16),
        ],
        compiler_params=_cparams(("arbitrary", "arbitrary")),
        name="inproj",
    )(x2, cos_t, sin_t, mod3, mod3, g_mix, w_in_p, g_q_lat, w_uq_p, g_kv_lat, w_ukv_p,
      gqn_p, gqr_p, gkn_p, gkr_p)


def _gelu_tanh(x):
    return 0.5 * x * (1.0 + jnp.tanh(0.7978845608028654 * (x + 0.044715 * x * x * x)))


def _lru_kernel(x_ref, y_ref, cw_ref, cb_ref, wa_ref, ba_ref, wx_ref, bx_ref, lam_ref,
                o_ref, tail_ref, carry_ref):
    s = pl.program_id(1)

    @pl.when(s == 0)
    def _():
        tail_ref[...] = jnp.zeros_like(tail_ref)
        carry_ref[...] = jnp.zeros_like(carry_ref)

    x = x_ref[...]
    ts = x.shape[0]
    xext = jnp.concatenate([tail_ref[...], x], axis=0)
    cw = cw_ref[...]
    xc = x * cw[CONV_W - 1:CONV_W, :]
    for j in range(CONV_W - 1):
        sh = CONV_W - 1 - j
        xc = xc + xext[8 - sh:8 - sh + ts, :] * cw[j:j + 1, :]
    xc = xc + cb_ref[...]
    tail_ref[...] = x[ts - 8:, :]

    xb = xc.astype(BF16)
    r = _sigmoid(_dot(xb, wa_ref[...]) + ba_ref[...])
    i = _sigmoid(_dot(xb, wx_ref[...]) + bx_ref[...])
    lam = lam_ref[...]
    nl = -lam
    softplus = jnp.maximum(nl, 0.0) + jnp.log(1.0 + jnp.exp(-jnp.abs(nl)))
    log_a = (-LRU_C) * r * softplus
    a = jnp.exp(log_a)
    mult = jnp.sqrt(1.0 - jnp.exp(2.0 * log_a))
    u = mult * (i * xc)

    rowi = lax.broadcasted_iota(jnp.int32, (ts, 1), 0)
    sh = 1
    while sh < ts:
        a_prev = pltpu.roll(a, sh, axis=0)
        u_prev = pltpu.roll(u, sh, axis=0)
        m = rowi >= sh
        u = jnp.where(m, a * u_prev + u, u)
        a = jnp.where(m, a * a_prev, a)
        sh *= 2
    hcar = carry_ref[0:1, :]
    hs = u + a * hcar
    carry_ref[...] = jnp.broadcast_to(hs[ts - 1:ts, :], carry_ref.shape)
    o_ref[...] = (_gelu_tanh(y_ref[...]) * hs).astype(BF16)


def _lru(xlru, ylru, conv_w, conv_b, wa_d, b_a, wx_d, b_x, lam, B, S, ts):
    T, C = xlru.shape
    ns = S // ts
    row = lambda b, s: (b * ns + s, 0)
    full = lambda b, s: (0, 0)
    return pl.pallas_call(
        _lru_kernel,
        grid=(B, ns),
        in_specs=[
            pl.BlockSpec((ts, C), row),
            pl.BlockSpec((ts, C), row),
            pl.BlockSpec((CONV_W, C), full),
            pl.BlockSpec((1, C), full),
            pl.BlockSpec((C, C), full),
            pl.BlockSpec((1, C), full),
            pl.BlockSpec((C, C), full),
            pl.BlockSpec((1, C), full),
            pl.BlockSpec((1, C), full),
        ],
        out_specs=pl.BlockSpec((ts, C), row),
        out_shape=jax.ShapeDtypeStruct((T, C), BF16),
        scratch_shapes=[pltpu.VMEM((8, C), F32), pltpu.VMEM((8, C), F32)],
        compiler_params=_cparams(("arbitrary", "arbitrary")),
        name="lru",
    )(xlru, ylru, conv_w, conv_b, wa_d, b_a, wx_d, b_x, lam)


NEG_INF = -1e30


def _attn_kernel(q_ref, k_ref, v_ref, o_ref, *, tq):
    qi = pl.program_id(1)
    rowi = lax.broadcasted_iota(jnp.int32, (tq, tq), 0)
    coli = lax.broadcasted_iota(jnp.int32, (tq, tq), 1)
    diag_mask = coli <= rowi

    for h0 in range(0, N_HEADS, HEAD_GROUP):
        heads = range(h0, h0 + HEAD_GROUP)
        qhs = [q_ref[:, hh * HEAD_PAD:(hh + 1) * HEAD_PAD] for hh in heads]

        def step(j, carry, masked):
            r0 = pl.multiple_of(j * tq, tq)
            out = []
            for gi, hh in enumerate(heads):
                m, acc = carry[gi]
                hs = slice(hh * HEAD_PAD, (hh + 1) * HEAD_PAD)
                kh = k_ref[pl.ds(r0, tq), hs]
                vh = v_ref[pl.ds(r0, tq), hs]
                sc = _dot_nt(qhs[gi], kh)
                if masked:
                    sc = jnp.where(diag_mask, sc, NEG_INF)
                m_new = jnp.maximum(m, jnp.max(sc, axis=-1, keepdims=True))
                alpha = jnp.exp2(m - m_new)
                p = jnp.exp2(sc - m_new)
                out.append((m_new, alpha * acc + _dot(p.astype(BF16), vh)))
            return tuple(out)

        init = tuple((jnp.full((tq, 1), NEG_INF, F32), jnp.zeros((tq, HEAD_PAD), F32))
                     for _ in heads)
        carry = lax.fori_loop(0, qi, functools.partial(step, masked=False), init)
        carry = step(qi, carry, True)
        for gi, hh in enumerate(heads):
            acc = carry[gi][1]
            o = acc[:, :V_HEAD] / acc[:, V_HEAD:V_HEAD + 1]
            o_ref[:, hh * V_HEAD:(hh + 1) * V_HEAD] = o.astype(BF16)


def _attn(qp, kp, v, B, S, tq):
    T = qp.shape[0]
    nq = S // tq
    HP = N_HEADS * HEAD_PAD
    HV = N_HEADS * V_HEAD
    return pl.pallas_call(
        functools.partial(_attn_kernel, tq=tq),
        grid=(B, nq),
        in_specs=[
            pl.BlockSpec((tq, HP), lambda b, i: (b * nq + i, 0)),
            pl.BlockSpec((S, HP), lambda b, i: (b, 0)),
            pl.BlockSpec((S, HP), lambda b, i: (b, 0)),
        ],
        out_specs=pl.BlockSpec((tq, HV), lambda b, i: (b * nq + i, 0)),
        out_shape=jax.ShapeDtypeStruct((T, HV), BF16),
        compiler_params=_cparams(("arbitrary", "arbitrary")),
        name="attn",
    )(qp, kp, v)


def _outproj_kernel(lru_ref, att_ref, x_ref, gate_ref, shift_ref, scale_ref, gffn_ref,
                    wo1_ref, wo2_ref, wrh_ref, wrl_ref, br_ref, tri_ref,
                    x1_ref, h2p_ref, idx_ref, gat_ref, rank_ref, cnt_ref, run_ref):
    first = (pl.program_id(0) == 0) & (pl.program_id(1) == 0)

    @pl.when(first)
    def _():
        run_ref[...] = jnp.zeros_like(run_ref)

    mix = _dot(lru_ref[...], wo1_ref[...]) + _dot(att_ref[...], wo2_ref[...])
    x1 = x_ref[...] + gate_ref[0] * mix
    x1_ref[...] = x1
    ms = jnp.mean(x1 * x1, axis=-1, keepdims=True)
    h2 = x1 * lax.rsqrt(ms + EPS) * gffn_ref[...]
    h2 = h2 * (1.0 + scale_ref[0]) + shift_ref[0]

    hhi = h2.astype(BF16)
    hhi32 = hhi.astype(F32)
    hlo = (h2 - hhi32).astype(BF16)
    h2p_ref[...] = _pack_halves(h2)

    wrh = wrh_ref[...]
    logits = _dot_nt(wrh, hhi) + _dot_nt(wrh, hlo) + _dot_nt(wrl_ref[...], hhi) + br_ref[...]
    ne, tm = logits.shape
    eio = lax.broadcasted_iota(jnp.int32, (ne, tm), 0)
    vals, idxs, sels = [], [], []
    l = logits
    for _ in range(TOP_K):
        m = jnp.max(l, axis=0, keepdims=True)
        idx = jnp.min(jnp.where(l == m, eio, ne), axis=0, keepdims=True)
        sel = eio == idx
        l = jnp.where(sel, -jnp.inf, l)
        vals.append(m)
        idxs.append(idx)
        sels.append(sel)
    es = [jnp.exp(v - vals[0]) for v in vals]
    den = es[0] + es[1] + es[2] + es[3]
    inv = 1.0 / den
    sel_any = jnp.where(sels[0] | sels[1] | sels[2] | sels[3], 1.0, 0.0)
    excl = _dot(sel_any.astype(BF16), tri_ref[...]) + run_ref[...]
    for kk in range(TOP_K):
        idx_ref[kk:kk + 1, :] = idxs[kk]
        gat_ref[kk:kk + 1, :] = es[kk] * inv
        rk = jnp.sum(jnp.where(sels[kk], excl, 0.0), axis=0, keepdims=True)
        rank_ref[kk:kk + 1, :] = rk.astype(jnp.int32)
    run = run_ref[...] + jnp.sum(sel_any, axis=1, keepdims=True)
    run_ref[...] = run
    cnt_ref[...] = run.astype(jnp.int32)


def _outproj(lru_o, att_o, x2, mod3, g_ffn, wo1, wo2, wr_hi, wr_lo, b_r, tri, b0, B, S, tm):
    D = x2.shape[1]
    T = B * S
    ns = S // tm
    C = lru_o.shape[1]
    row_in = lambda b, s: ((b0 + b) * ns + s, 0)
    row = lambda b, s: (b * ns + s, 0)
    col = lambda b, s: (0, b * ns + s)
    full = lambda b, s: (0, 0)
    return pl.pallas_call(
        _outproj_kernel,
        grid=(B, ns),
        in_specs=[
            pl.BlockSpec((tm, C), row_in),
            pl.BlockSpec((tm, C), row_in),
            pl.BlockSpec((tm, D), row_in),
            pl.BlockSpec((1, 1, D), lambda b, s: ((b0 + b) * 6 + 2, 0, 0)),
            pl.BlockSpec((1, 1, D), lambda b, s: ((b0 + b) * 6 + 3, 0, 0)),
            pl.BlockSpec((1, 1, D), lambda b, s: ((b0 + b) * 6 + 4, 0, 0)),
            pl.BlockSpec((1, D), full),
            pl.BlockSpec((C, D), full),
            pl.BlockSpec((C, D), full),
            pl.BlockSpec((N_EXPERTS, D), full),
            pl.BlockSpec((N_EXPERTS, D), full),
            pl.BlockSpec((N_EXPERTS, 1), full),
            pl.BlockSpec((tm, tm), full),
        ],
        out_specs=[
            pl.BlockSpec((tm, D), row),
            pl.BlockSpec((tm, D // 2), row),
            pl.BlockSpec((TOP_K, tm), col),
            pl.BlockSpec((TOP_K, tm), col),
            pl.BlockSpec((TOP_K, tm), col),
            pl.BlockSpec((N_EXPERTS, 1), full),
        ],
        out_shape=[
            jax.ShapeDtypeStruct((T, D), F32),
            jax.ShapeDtypeStruct((T, D // 2), jnp.int32),
            jax.ShapeDtypeStruct((TOP_K, T), jnp.int32),
            jax.ShapeDtypeStruct((TOP_K, T), F32),
            jax.ShapeDtypeStruct((TOP_K, T), jnp.int32),
            jax.ShapeDtypeStruct((N_EXPERTS, 1), jnp.int32),
        ],
        scratch_shapes=[pltpu.VMEM((N_EXPERTS, 1), F32)],
        compiler_params=_cparams(("arbitrary", "arbitrary")),
        name="outproj",
    )(lru_o, att_o, x2, mod3, mod3, mod3, g_ffn, wo1, wo2, wr_hi, wr_lo, b_r, tri)


SC_CORES = 2
SC_SUBCORES = 16
SC_WORKERS = SC_CORES * SC_SUBCORES


def _sc_mesh():
    return plsc.VectorSubcoreMesh(core_axis_name="c", subcore_axis_name="s",
                                  num_cores=SC_CORES, num_subcores=SC_SUBCORES)


def _sc_worker_id():
    return lax.axis_index("s") * SC_CORES + lax.axis_index("c")


def _sc_scatter_rows(rows, idx, n_out, g):
    T, W = rows.shape
    K = idx.shape[0]
    per_w = T // SC_WORKERS
    nch = per_w // g
    assert per_w * SC_WORKERS == T and nch * g == per_w and nch % 2 == 0
    idx_w = idx.reshape(K, SC_WORKERS, nch, g).transpose(1, 2, 0, 3).reshape(SC_WORKERS, nch * K, g)

    def body(rows_hbm, idx_hbm, out_hbm, idx_v, buf0, buf1, semr0, semr1, semw):
        wid = _sc_worker_id()
        base = wid * per_w
        pltpu.sync_copy(idx_hbm.at[wid], idx_v)

        def read(j, buf, sem):
            return pltpu.make_async_copy(rows_hbm.at[pl.ds(base + j * g, g)], buf, sem)

        def scatter(j, buf):
            copies = [pltpu.async_copy(buf, out_hbm.at[idx_v.at[j * K + kk]], semw)
                      for kk in range(K)]
            for cp in copies:
                cp.wait()

        read(0, buf0, semr0).start()

        @pl.loop(0, nch // 2)
        def _(jj):
            j0 = 2 * jj
            read(j0 + 1, buf1, semr1).start()
            read(j0, buf0, semr0).wait()
            scatter(j0, buf0)

            @pl.when(j0 + 2 < nch)
            def _():
                read(j0 + 2, buf0, semr0).start()

            read(j0 + 1, buf1, semr1).wait()
            scatter(j0 + 1, buf1)

    return pl.kernel(
        body,
        out_type=jax.ShapeDtypeStruct((n_out, W), rows.dtype),
        mesh=_sc_mesh(),
        scratch_types=[
            pltpu.VMEM((nch * K, g), jnp.int32),
            pltpu.VMEM((g, W), rows.dtype),
            pltpu.VMEM((g, W), rows.dtype),
            pltpu.SemaphoreType.DMA,
            pltpu.SemaphoreType.DMA,
            pltpu.SemaphoreType.DMA,
        ],
        name="sc_scatter_rows",
    )(rows, idx_w)


def _sc_gather_rows(table, idx, g):
    W = table.shape[1]
    N = idx.shape[0]
    per_w = N // SC_WORKERS
    nch = per_w // g
    assert per_w * SC_WORKERS == N and nch * g == per_w and nch % 2 == 0
    idx_w = idx.reshape(SC_WORKERS, nch, g)

    def body(table_hbm, idx_hbm, out_hbm, idx_v, buf0, buf1, sem0, sem1):
        wid = _sc_worker_id()
        base = wid * per_w
        pltpu.sync_copy(idx_hbm.at[wid], idx_v)

        def gather(j, buf, sem):
            return pltpu.make_async_copy(table_hbm.at[idx_v.at[j]], buf, sem)

        def put(j, buf):
            pltpu.sync_copy(buf, out_hbm.at[pl.ds(base + j * g, g)])

        gather(0, buf0, sem0).start()

        @pl.loop(0, nch // 2)
        def _(jj):
            j0 = 2 * jj
            gather(j0 + 1, buf1, sem1).start()
            gather(j0, buf0, sem0).wait()
            put(j0, buf0)

            @pl.when(j0 + 2 < nch)
            def _():
                gather(j0 + 2, buf0, sem0).start()

            gather(j0 + 1, buf1, sem1).wait()
            put(j0 + 1, buf1)

    return pl.kernel(
        body,
        out_type=jax.ShapeDtypeStruct((N, W), table.dtype),
        mesh=_sc_mesh(),
        scratch_types=[
            pltpu.VMEM((nch, g), jnp.int32),
            pltpu.VMEM((g, W), table.dtype),
            pltpu.VMEM((g, W), table.dtype),
            pltpu.SemaphoreType.DMA,
            pltpu.SemaphoreType.DMA,
        ],
        name="sc_gather_rows",
    )(table, idx_w)


def _experts_kernel(be_ref, bv_ref, xs_ref, w1_ref, b1_ref, w2_ref, b2_ref, ys_ref):
    i = pl.program_id(0)

    nvalid = bv_ref[i]

    @pl.when(nvalid > 0)
    def _():
        xw = xs_ref[...]
        rowi = lax.broadcasted_iota(jnp.int32, (xw.shape[0], 1), 0)
        lo, hi = _unpack_halves(jnp.where(rowi < nvalid, xw, 0))
        half = xw.shape[1]
        gu = (_dot(lo.astype(BF16), w1_ref[0, :half, :]) + _dot(hi.astype(BF16), w1_ref[0, half:, :])
              + b1_ref[0])
        glu = jnp.minimum(gu[:, :D_FF], SWIGLU_LIMIT)
        lin = jnp.clip(gu[:, D_FF:], -SWIGLU_LIMIT, SWIGLU_LIMIT)
        act = (lin + 1.0) * (glu * _sigmoid(SWIGLU_ALPHA * glu))
        ys_ref[...] = _pack_halves(_dot(act.astype(BF16), w2_ref[0]) + b2_ref[0])


def _experts(blk_e, blk_v, blk_r, xs, w1b, b1, w2b, b2):
    P, W = xs.shape
    nb = P // MOE_BLOCK
    E, D, F2 = w1b.shape
    grid_spec = pltpu.PrefetchScalarGridSpec(
        num_scalar_prefetch=3,
        grid=(nb,),
        in_specs=[
            pl.BlockSpec((MOE_BLOCK, W), lambda i, be, bv, br: (br[i], 0)),
            pl.BlockSpec((1, D, F2), lambda i, be, bv, br: (be[i], 0, 0)),
            pl.BlockSpec((1, 1, F2), lambda i, be, bv, br: (be[i], 0, 0)),
            pl.BlockSpec((1, D_FF, D), lambda i, be, bv, br: (be[i], 0, 0)),
            pl.BlockSpec((1, 1, D), lambda i, be, bv, br: (be[i], 0, 0)),
        ],
        out_specs=pl.BlockSpec((MOE_BLOCK, D // 2), lambda i, be, bv, br: (br[i], 0)),
    )

    def kern(be_ref, bv_ref, br_ref, *refs):
        del br_ref
        _experts_kernel(be_ref, bv_ref, *refs)

    return pl.pallas_call(
        kern,
        grid_spec=grid_spec,
        out_shape=jax.ShapeDtypeStruct((P, D // 2), jnp.int32),
        compiler_params=_cparams(("arbitrary",)),
        name="experts",
    )(blk_e, blk_v, blk_r, xs, w1b, b1.reshape(E, 1, F2), w2b, b2.reshape(E, 1, D))


def _combine_kernel(x1_ref, g_ref, gate_ref, y0_ref, y1_ref, y2_ref, y3_ref, *rest):
    o_ref = rest[-1]
    g = g_ref[...]
    half = y0_ref.shape[2]
    acc_lo = acc_hi = None
    for kk, y_ref in enumerate((y0_ref, y1_ref, y2_ref, y3_ref)):
        lo, hi = _unpack_halves(y_ref[0])
        gk = g[:, kk:kk + 1]
        acc_lo = lo * gk if acc_lo is None else acc_lo + lo * gk
        acc_hi = hi * gk if acc_hi is None else acc_hi + hi * gk
    gate = gate_ref[0]
    o_ref[:, :half] = x1_ref[:, :half] + gate[:, :half] * acc_lo
    o_ref[:, half:] = x1_ref[:, half:] + gate[:, half:] * acc_hi


def _combine(x1, gates_tk, mod3, ysg, out_prev, b0, B, nb_total, S, tm):
    D = x1.shape[1]
    ns = S // tm
    row = lambda b, s: (b * ns + s, 0)
    row_out = lambda b, s: ((b0 + b) * ns + s, 0)

    def yspec(kk):
        return pl.BlockSpec((1, tm, D // 2), lambda b, s: (kk, b * ns + s, 0))

    in_specs = [
        pl.BlockSpec((tm, D), row),
        pl.BlockSpec((tm, TOP_K), row),
        pl.BlockSpec((1, 1, D), lambda b, s: ((b0 + b) * 6 + 5, 0, 0)),
        yspec(0), yspec(1), yspec(2), yspec(3),
    ]
    args = [x1, gates_tk, mod3, ysg, ysg, ysg, ysg]
    aliases = {}
    if out_prev is not None:
        in_specs.append(pl.BlockSpec(memory_space=pl.ANY))
        args.append(out_prev)
        aliases = {len(args) - 1: 0}
    return pl.pallas_call(
        _combine_kernel,
        grid=(B, ns),
        in_specs=in_specs,
        out_specs=pl.BlockSpec((tm, D), row_out),
        out_shape=jax.ShapeDtypeStruct((nb_total * S, D), F32),
        input_output_aliases=aliases,
        compiler_params=_cparams(("arbitrary", "arbitrary")),
        name="combine",
    )(*args)


def _block_diag(w):
    n, c, d = w.shape
    eye = jnp.eye(n, dtype=w.dtype)
    return jnp.einsum("ncd,nm->ncmd", w, eye).reshape(n * c, n * d)


def _pad_heads(w, width):
    k = w.shape[0]
    w = w.reshape(k, N_HEADS, width)
    return jnp.pad(w, ((0, 0), (0, 0), (0, HEAD_PAD - width))).reshape(k, N_HEADS * HEAD_PAD)


def kernel(x, c, positions, w_ada, b_ada, g_mix, w_in, conv_w, conv_b, w_a, b_a, w_x, b_x, lam,
           g_q_lat, w_uq, g_kv_lat, w_ukv, g_qn, g_kn, w_out, g_ffn, w_router, b_router,
           w1, b1, w2, b2):
    B, S, D = x.shape
    T = B * S
    depth = w_ada.shape[0]
    tm_in = min(512, S)
    ts_lru = min(256, S)
    tq = min(512, S)
    tm_out = min(512, S)
    tm_comb = min(512, S)
    n_groups = 2 if B % 2 == 0 else 1
    Bg = B // n_groups
    Tg = Bg * S
    g_disp = min(64, Tg // SC_WORKERS // 2)
    g_comb = min(64, Tg * TOP_K // SC_WORKERS // 2)

    o1 = 2 * D_LRU
    o2 = o1 + Q_LORA
    o3 = o2 + KV_LORA
    tri = (jnp.arange(tm_out)[:, None] < jnp.arange(tm_out)[None, :]).astype(BF16)
    cos_t, sin_t = _rope_tables(positions)
    lane = jnp.arange(HEAD_PAD, dtype=jnp.int32)
    first = (lane >= ROPE_LO) & (lane < ROPE_LO + ROPE_HALF)
    second = (lane >= ROPE_LO + ROPE_HALF) & (lane < ROPE_LO + QK_ROPE)
    partner = jnp.where(first, lane + ROPE_HALF, jnp.where(second, lane - ROPE_HALF, 0))
    is_rot = first | second

    def rot_cols(w):
        k = w.shape[0]
        w3 = w.reshape(k, -1, HEAD_PAD)
        return jnp.where(is_rot[None, None, :], w3[:, :, partner], 0.0).reshape(w.shape)

    x2 = x.reshape(T, D)
    for l in range(depth):
        mod3 = _ada(c, w_ada[l], b_ada[l]).reshape(B * 6, 1, D)

        w_in_l = w_in[l]
        kr_cols = jnp.pad(w_in_l[:, o3:], ((0, 0), (ROPE_LO, LANES - ROPE_LO - QK_ROPE)))
        w_in_p = jnp.concatenate([w_in_l[:, :o3], kr_cols, rot_cols(kr_cols)], axis=1).astype(BF16)
        w_uq_h = _pad_heads(w_uq[l], QK_HEAD)
        w_uq_p = jnp.concatenate([w_uq_h, rot_cols(w_uq_h)], axis=1).astype(BF16)
        w_ukv_l = w_ukv[l].reshape(KV_LORA, N_HEADS, QK_NOPE + V_HEAD)
        w_uk_h = _pad_heads(w_ukv_l[:, :, :QK_NOPE].reshape(KV_LORA, N_HEADS * QK_NOPE), QK_NOPE)
        w_uv_h = _pad_heads(w_ukv_l[:, :, QK_NOPE:].reshape(KV_LORA, N_HEADS * V_HEAD), V_HEAD)
        w_ukv_p = jnp.concatenate([w_uk_h, w_uv_h], axis=1).astype(BF16)
        gqn_p = jnp.pad(g_qn[l], (0, HEAD_PAD - QK_HEAD)).reshape(1, HEAD_PAD)
        gkn_p = jnp.pad(g_kn[l], (0, HEAD_PAD - QK_HEAD)).reshape(1, HEAD_PAD)

        xlru, ylru, qp, kp, v = _inproj(
            x2, cos_t, sin_t, mod3, g_mix[l].reshape(1, D), w_in_p, g_q_lat[l].reshape(1, Q_LORA),
            w_uq_p, g_kv_lat[l].reshape(1, KV_LORA), w_ukv_p, gqn_p, rot_cols(gqn_p), gkn_p,
            rot_cols(gkn_p), B, S, tm_in)

        lru_o = _lru(xlru, ylru, conv_w[l], conv_b[l].reshape(1, D_LRU),
                     _block_diag(w_a[l]).astype(BF16), b_a[l].reshape(1, D_LRU),
                     _block_diag(w_x[l]).astype(BF16), b_x[l].reshape(1, D_LRU),
                     lam[l].reshape(1, D_LRU), B, S, ts_lru)

        att_o = _attn(qp, kp, v, B, S, tq)

        w_out_b = w_out[l].astype(BF16)
        wr_hi, wr_lo = _split_bf16(w_router[l].T)
        w1b = w1[l].astype(BF16)
        w2b = w2[l].astype(BF16)
        g_ffn_l = g_ffn[l].reshape(1, D)
        b_r = b_router[l].reshape(N_EXPERTS, 1)
        eio = jnp.arange(N_EXPERTS, dtype=jnp.int32)
        n_blocks = -(-(Tg * TOP_K) // MOE_BLOCK) + N_EXPERTS
        bi = jnp.arange(n_blocks, dtype=jnp.int32)

        x_next = None
        for gi in range(n_groups):
            b0 = gi * Bg
            x1, h2p, idx_t, gat_t, rank_t, counts = _outproj(
                lru_o, att_o, x2, mod3, g_ffn_l, w_out_b[:D_LRU], w_out_b[D_LRU:],
                wr_hi, wr_lo, b_r, tri, b0, Bg, S, tm_out)

            counts = counts.reshape(N_EXPERTS)
            nblk_e = (counts + MOE_BLOCK - 1) // MOE_BLOCK
            blk_end = jnp.cumsum(nblk_e)
            pad_start = (blk_end - nblk_e) * MOE_BLOCK
            total = blk_end[-1]
            blk_r = jnp.minimum(bi, total - 1).astype(jnp.int32)
            blk_e = jnp.minimum(jnp.sum(blk_end[None, :] <= blk_r[:, None], axis=1),
                                N_EXPERTS - 1).astype(jnp.int32)
            blk_onehot = blk_e[:, None] == eio[None, :]
            blk_first = jnp.sum(jnp.where(blk_onehot, (blk_end - nblk_e)[None, :], 0), axis=1)
            blk_cnt = jnp.sum(jnp.where(blk_onehot, counts[None, :], 0), axis=1)
            blk_n = jnp.where(bi < total,
                              jnp.clip(blk_cnt - (bi - blk_first) * MOE_BLOCK, 0, MOE_BLOCK),
                              0).astype(jnp.int32)
            slot0 = jnp.sum(jnp.where(idx_t[None] == eio[:, None, None],
                                      pad_start[:, None, None], 0), axis=0)
            dest = slot0.astype(jnp.int32) + rank_t

            xs = _sc_scatter_rows(h2p, dest, n_blocks * MOE_BLOCK, g_disp)
            ys = _experts(blk_e, blk_n, blk_r, xs, w1b, b1[l], w2b, b2[l])
            ysg = _sc_gather_rows(ys, dest.reshape(TOP_K * Tg), g_comb).reshape(TOP_K, Tg, D // 2)
            x_next = _combine(x1, gat_t.T, mod3, ysg, x_next, b0, Bg, B, S, tm_comb)
        x2 = x_next
    return x2.reshape(B, S, D)
```

```python
import functools

import jax
import jax.numpy as jnp
from jax import lax
from jax.experimental import pallas as pl
from jax.experimental.pallas import tpu as pltpu
from jax.experimental.pallas import tpu_sc as plsc

D_MODEL = 1024
D_LRU = 512
LRU_BLOCKS = 8
LRU_BD = 64
CONV_W = 4
LRU_C = 8.0
N_HEADS = 8
QK_NOPE = 64
QK_ROPE = 32
QK_HEAD = 96
V_HEAD = 64
Q_LORA = 256
KV_LORA = 128
ROPE_THETA = 10000.0
N_EXPERTS = 32
TOP_K = 4
D_FF = 1024
SWIGLU_LIMIT = 7.0
SWIGLU_ALPHA = 1.702
MOE_BLOCK = 512
EPS = 1e-6

LANES = 128
HEAD_PAD = 128
ROPE_LO = QK_NOPE
ROPE_HALF = QK_ROPE // 2
D_IN_PAD = 2 * D_LRU + Q_LORA + KV_LORA + 2 * LANES
LOG2_E = 1.4426950408889634
SUM_LANE = V_HEAD
MAX_LANE = V_HEAD + 1

VMEM_LIMIT = 56 * 1024 * 1024

F32 = jnp.float32
BF16 = jnp.bfloat16


def _cparams(sem):
    return pltpu.CompilerParams(dimension_semantics=sem, vmem_limit_bytes=VMEM_LIMIT)


def _dot(a, b):
    return jnp.dot(a, b, preferred_element_type=F32)


def _dot_nt(a, b):
    return lax.dot_general(a, b, (((1,), (1,)), ((), ())), preferred_element_type=F32)


def _split_bf16(a):
    hi = a.astype(BF16)
    lo = (a - hi.astype(F32)).astype(BF16)
    return hi, lo


def _sigmoid(x):
    return 1.0 / (1.0 + jnp.exp(-x))


def _pack_halves(x):
    bits = lax.bitcast_convert_type(x.astype(BF16).astype(F32), jnp.uint32)
    half = x.shape[1] // 2
    words = (bits[:, :half] >> 16) | (bits[:, half:] & jnp.uint32(0xFFFF0000))
    return lax.bitcast_convert_type(words, jnp.int32)


def _unpack_halves(words):
    w = lax.bitcast_convert_type(words, jnp.uint32)
    lo = lax.bitcast_convert_type(w << 16, F32)
    hi = lax.bitcast_convert_type(w & jnp.uint32(0xFFFF0000), F32)
    return lo, hi


def _ada_kernel(c_ref, whi_ref, wlo_ref, b_ref, o_ref):
    c = c_ref[...]
    s = c * _sigmoid(c)
    shi, slo = _split_bf16(s)
    whi = whi_ref[...]
    o_ref[...] = _dot(shi, whi) + _dot(slo, whi) + _dot(shi, wlo_ref[...]) + b_ref[...]


def _ada(c, w_ada, b_ada):
    B, D = c.shape
    N = w_ada.shape[1]
    tn = 1024
    whi, wlo = _split_bf16(w_ada)
    return pl.pallas_call(
        _ada_kernel,
        grid=(N // tn,),
        in_specs=[
            pl.BlockSpec((B, D), lambda j: (0, 0)),
            pl.BlockSpec((D, tn), lambda j: (0, j)),
            pl.BlockSpec((D, tn), lambda j: (0, j)),
            pl.BlockSpec((1, tn), lambda j: (0, j)),
        ],
        out_specs=pl.BlockSpec((B, tn), lambda j: (0, j)),
        out_shape=jax.ShapeDtypeStruct((B, N), F32),
        compiler_params=_cparams(("arbitrary",)),
        name="ada",
    )(c, whi, wlo, b_ada.reshape(1, N))


def _trig_kernel(pos_ref, freq_ref, cos_ref, sin_ref):
    ang = pos_ref[...].astype(F32) * freq_ref[...]
    cos_ref[...] = jnp.cos(ang)
    sin_ref[...] = jnp.sin(ang)


def _rope_tables(positions):
    T = positions.size
    per_row = LANES // ROPE_HALF
    freqs = ROPE_THETA ** (-jnp.arange(ROPE_HALF, dtype=F32) / ROPE_HALF)
    pos_c = jnp.repeat(positions.reshape(T).astype(jnp.int32), ROPE_HALF).reshape(T // per_row, LANES)
    freq_c = jnp.tile(freqs, per_row).reshape(1, LANES)
    rows = T // per_row
    tr = min(512, rows)
    cos_c, sin_c = pl.pallas_call(
        _trig_kernel,
        grid=(rows // tr,),
        in_specs=[pl.BlockSpec((tr, LANES), lambda i: (i, 0)), pl.BlockSpec((1, LANES), lambda i: (0, 0))],
        out_specs=[pl.BlockSpec((tr, LANES), lambda i: (i, 0))] * 2,
        out_shape=[jax.ShapeDtypeStruct((rows, LANES), F32)] * 2,
        compiler_params=_cparams(("arbitrary",)),
        name="rope_trig",
    )(pos_c, freq_c)
    cos16 = cos_c.reshape(T, ROPE_HALF)
    sin16 = sin_c.reshape(T, ROPE_HALF)
    tail = LANES - ROPE_LO - QK_ROPE
    cos_t = jnp.concatenate([jnp.ones((T, ROPE_LO), F32), cos16, cos16, jnp.ones((T, tail), F32)], axis=1)
    sin_t = jnp.concatenate([jnp.zeros((T, ROPE_LO), F32), -sin16, sin16, jnp.zeros((T, tail), F32)], axis=1)
    return cos_t, sin_t


def _inproj_kernel(x_ref, cos_ref, sin_ref, shift_ref, scale_ref, gmix_ref, win_ref, gq_ref, wuq_ref,
                   gkv_ref, wukv_ref, gqn_ref, gqr_ref, gkn_ref, gkr_ref,
                   xlru_ref, ylru_ref, q_ref, k_ref, v_ref):
    HP = N_HEADS * HEAD_PAD
    x = x_ref[...]
    ms = jnp.mean(x * x, axis=-1, keepdims=True)
    xn = x * lax.rsqrt(ms + EPS) * gmix_ref[...]
    h = xn * (1.0 + scale_ref[0]) + shift_ref[0]
    z = _dot(h.astype(BF16), win_ref[...])
    xlru_ref[...] = z[:, :D_LRU]
    ylru_ref[...] = z[:, D_LRU:2 * D_LRU]
    o1 = 2 * D_LRU
    o2 = o1 + Q_LORA
    o3 = o2 + KV_LORA
    ql = z[:, o1:o2]
    kvl = z[:, o2:o3]
    kr = z[:, o3:o3 + LANES]
    kr_rot = z[:, o3 + LANES:]

    qn = ql * lax.rsqrt(jnp.mean(ql * ql, axis=-1, keepdims=True) + EPS) * gq_ref[...]
    qq = _dot(qn.astype(BF16), wuq_ref[...])
    kvn = kvl * lax.rsqrt(jnp.mean(kvl * kvl, axis=-1, keepdims=True) + EPS) * gkv_ref[...]
    kv = _dot(kvn.astype(BF16), wukv_ref[...])

    tm = x.shape[0]
    lane = lax.broadcasted_iota(jnp.int32, (tm, HP), 1)
    v_ref[...] = jnp.where((lane & (HEAD_PAD - 1)) == V_HEAD, 1.0, kv[:, HP:]).astype(BF16)

    cos_t = cos_ref[...]
    sin_t = sin_ref[...]
    gqn = gqn_ref[...]
    gkn = gkn_ref[...]
    cq = gqn * cos_t
    sq = gqr_ref[...] * sin_t
    kb = kr * (gkn * cos_t) + kr_rot * (gkr_ref[...] * sin_t)
    inv_w = 1.0 / QK_HEAD
    qscale = QK_HEAD ** -0.5 * LOG2_E
    for hh in range(N_HEADS):
        sl = slice(hh * HEAD_PAD, (hh + 1) * HEAD_PAD)
        qh = qq[:, sl]
        rq = lax.rsqrt(jnp.sum(qh * qh, axis=-1, keepdims=True) * inv_w + EPS) * qscale
        q_ref[:, sl] = ((qh * cq + qq[:, HP + hh * HEAD_PAD:HP + (hh + 1) * HEAD_PAD] * sq) * rq).astype(BF16)
        kraw = kv[:, sl] + kr
        rk = lax.rsqrt(jnp.sum(kraw * kraw, axis=-1, keepdims=True) * inv_w + EPS)
        k_ref[:, sl] = ((kv[:, sl] * gkn + kb) * rk).astype(BF16)


def _inproj(x2, cos_t, sin_t, mod3, g_mix, w_in_p, g_q_lat, w_uq_p, g_kv_lat, w_ukv_p,
            gqn_p, gqr_p, gkn_p, gkr_p, B, S, tm):
    T, D = x2.shape
    ns = S // tm
    HP = N_HEADS * HEAD_PAD
    row = lambda b, s: (b * ns + s, 0)
    full = lambda b, s: (0, 0)
    return pl.pallas_call(
        _inproj_kernel,
        grid=(B, ns),
        in_specs=[
            pl.BlockSpec((tm, D), row),
            pl.BlockSpec((tm, LANES), row),
            pl.BlockSpec((tm, LANES), row),
            pl.BlockSpec((1, 1, D), lambda b, s: (b * 6 + 0, 0, 0)),
            pl.BlockSpec((1, 1, D), lambda b, s: (b * 6 + 1, 0, 0)),
            pl.BlockSpec((1, D), full),
            pl.BlockSpec((D, D_IN_PAD), full),
            pl.BlockSpec((1, Q_LORA), full),
            pl.BlockSpec((Q_LORA, 2 * HP), full),
            pl.BlockSpec((1, KV_LORA), full),
            pl.BlockSpec((KV_LORA, 2 * HP), full),
            pl.BlockSpec((1, HEAD_PAD), full),
            pl.BlockSpec((1, HEAD_PAD), full),
            pl.BlockSpec((1, HEAD_PAD), full),
            pl.BlockSpec((1, HEAD_PAD), full),
        ],
        out_specs=[
            pl.BlockSpec((tm, D_LRU), row),
            pl.BlockSpec((tm, D_LRU), row),
            pl.BlockSpec((tm, HP), row),
            pl.BlockSpec((tm, HP), row),
            pl.BlockSpec((tm, HP), row),
        ],
        out_shape=[
            jax.ShapeDtypeStruct((T, D_LRU), F32),
            jax.ShapeDtypeStruct((T, D_LRU), F32),
            jax.ShapeDtypeStruct((T, HP), BF16),
            jax.ShapeDtypeStruct((T, HP), BF16),
            jax.ShapeDtypeStruct((T, HP), BF16),
        ],
        compiler_params=_cparams(("arbitrary", "arbitrary")),
        name="inproj",
    )(x2, cos_t, sin_t, mod3, mod3, g_mix, w_in_p, g_q_lat, w_uq_p, g_kv_lat, w_ukv_p,
      gqn_p, gqr_p, gkn_p, gkr_p)


def _gelu_tanh(x):
    return 0.5 * x * (1.0 + jnp.tanh(0.7978845608028654 * (x + 0.044715 * x * x * x)))


def _lru_kernel(x_ref, y_ref, cw_ref, cb_ref, wa_ref, ba_ref, wx_ref, bx_ref, lam_ref,
                o_ref, tail_ref, carry_ref):
    s = pl.program_id(1)

    @pl.when(s == 0)
    def _():
        tail_ref[...] = jnp.zeros_like(tail_ref)
        carry_ref[...] = jnp.zeros_like(carry_ref)

    x = x_ref[...]
    ts = x.shape[0]
    xext = jnp.concatenate([tail_ref[...], x], axis=0)
    cw = cw_ref[...]
    xc = x * cw[CONV_W - 1:CONV_W, :]
    for j in range(CONV_W - 1):
        sh = CONV_W - 1 - j
        xc = xc + xext[8 - sh:8 - sh + ts, :] * cw[j:j + 1, :]
    xc = xc + cb_ref[...]
    tail_ref[...] = x[ts - 8:, :]

    xb = xc.astype(BF16)
    r = _sigmoid(_dot(xb, wa_ref[...]) + ba_ref[...])
    i = _sigmoid(_dot(xb, wx_ref[...]) + bx_ref[...])
    lam = lam_ref[...]
    nl = -lam
    softplus = jnp.maximum(nl, 0.0) + jnp.log(1.0 + jnp.exp(-jnp.abs(nl)))
    log_a = (-LRU_C) * r * softplus
    a = jnp.exp(log_a)
    mult = jnp.sqrt(1.0 - jnp.exp(2.0 * log_a))
    u = mult * (i * xc)

    rowi = lax.broadcasted_iota(jnp.int32, (ts, 1), 0)
    sh = 1
    while sh < ts:
        a_prev = pltpu.roll(a, sh, axis=0)
        u_prev = pltpu.roll(u, sh, axis=0)
        m = rowi >= sh
        u = jnp.where(m, a * u_prev + u, u)
        a = jnp.where(m, a * a_prev, a)
        sh *= 2
    hcar = carry_ref[0:1, :]
    hs = u + a * hcar
    carry_ref[...] = jnp.broadcast_to(hs[ts - 1:ts, :], carry_ref.shape)
    o_ref[...] = (_gelu_tanh(y_ref[...]) * hs).astype(BF16)


def _lru(xlru, ylru, conv_w, conv_b, wa_d, b_a, wx_d, b_x, lam, B, S, ts):
    T, C = xlru.shape
    ns = S // ts
    row = lambda b, s: (b * ns + s, 0)
    full = lambda b, s: (0, 0)
    return pl.pallas_call(
        _lru_kernel,
        grid=(B, ns),
        in_specs=[
            pl.BlockSpec((ts, C), row),
            pl.BlockSpec((ts, C), row),
            pl.BlockSpec((CONV_W, C), full),
            pl.BlockSpec((1, C), full),
            pl.BlockSpec((C, C), full),
            pl.BlockSpec((1, C), full),
            pl.BlockSpec((C, C), full),
            pl.BlockSpec((1, C), full),
            pl.BlockSpec((1, C), full),
        ],
        out_specs=pl.BlockSpec((ts, C), row),
        out_shape=jax.ShapeDtypeStruct((T, C), BF16),
        scratch_shapes=[pltpu.VMEM((8, C), F32), pltpu.VMEM((8, C), F32)],
        compiler_params=_cparams(("arbitrary", "arbitrary")),
        name="lru",
    )(xlru, ylru, conv_w, conv_b, wa_d, b_a, wx_d, b_x, lam)


NEG_INF = -1e30


def _attn_kernel(q_ref, k_ref, v_ref, o_ref, *state, tq):
    m_refs = state[:N_HEADS]
    acc_refs = state[N_HEADS:]
    qi = pl.program_id(1)
    rowi = lax.broadcasted_iota(jnp.int32, (tq, tq), 0)
    coli = lax.broadcasted_iota(jnp.int32, (tq, tq), 1)
    diag_mask = coli <= rowi

    def head_slice(hh):
        return slice(hh * HEAD_PAD, (hh + 1) * HEAD_PAD)

    def weights(sc, m_b):
        cols = [jnp.exp2(sc[:, c0:c0 + LANES] - m_b) for c0 in range(0, tq, LANES)]
        return jnp.concatenate(cols, axis=1).astype(BF16)

    def scores(hh, r0):
        hs = head_slice(hh)
        return _dot_nt(q_ref[:, hs], k_ref[pl.ds(r0, tq), hs])

    r_diag = pl.multiple_of(qi * tq, tq)
    sc_next = scores(0, r_diag)
    for hh in range(N_HEADS):
        hs = head_slice(hh)
        sc = jnp.where(diag_mask, sc_next, NEG_INF)
        if hh + 1 < N_HEADS:
            sc_next = scores(hh + 1, r_diag)
        m_b = jnp.broadcast_to(jnp.max(sc, axis=-1, keepdims=True), (tq, LANES))
        m_refs[hh][...] = m_b
        acc_refs[hh][...] = _dot(weights(sc, m_b), v_ref[pl.ds(r_diag, tq), hs])

    @pl.loop(0, qi)
    def _(j):
        r0 = pl.multiple_of(j * tq, tq)
        sc_next = scores(0, r0)
        for hh in range(N_HEADS):
            hs = head_slice(hh)
            sc = sc_next
            if hh + 1 < N_HEADS:
                sc_next = scores(hh + 1, r0)
            m_b = m_refs[hh][...]
            m_new = jnp.maximum(m_b, jnp.max(sc, axis=-1, keepdims=True))
            alpha = jnp.exp2(m_b - m_new)
            m_refs[hh][...] = m_new
            acc_refs[hh][...] = (alpha * acc_refs[hh][...]
                                 + _dot(weights(sc, m_new), v_ref[pl.ds(r0, tq), hs]))

    for hh in range(N_HEADS):
        acc = acc_refs[hh][...]
        o = acc[:, :V_HEAD] / acc[:, SUM_LANE:SUM_LANE + 1]
        o_ref[:, hh * V_HEAD:(hh + 1) * V_HEAD] = o.astype(BF16)


def _attn(qp, kp, v, B, S, tq):
    T = qp.shape[0]
    nq = S // tq
    HP = N_HEADS * HEAD_PAD
    HV = N_HEADS * V_HEAD
    return pl.pallas_call(
        functools.partial(_attn_kernel, tq=tq),
        grid=(B, nq),
        in_specs=[
            pl.BlockSpec((tq, HP), lambda b, i: (b * nq + i, 0)),
            pl.BlockSpec((S, HP), lambda b, i: (b, 0)),
            pl.BlockSpec((S, HP), lambda b, i: (b, 0)),
        ],
        out_specs=pl.BlockSpec((tq, HV), lambda b, i: (b * nq + i, 0)),
        out_shape=jax.ShapeDtypeStruct((T, HV), BF16),
        scratch_shapes=([pltpu.VMEM((tq, LANES), F32)] * N_HEADS
                        + [pltpu.VMEM((tq, HEAD_PAD), F32)] * N_HEADS),
        compiler_params=_cparams(("arbitrary", "arbitrary")),
        name="attn",
    )(qp, kp, v)


def _outproj_kernel(lru_ref, att_ref, x_ref, gate_ref, shift_ref, scale_ref, gffn_ref,
                    wo1_ref, wo2_ref, wrh_ref, wrl_ref, br_ref, tri_ref,
                    x1_ref, h2p_ref, idx_ref, gat_ref, rank_ref, cnt_ref, run_ref):
    first = (pl.program_id(0) == 0) & (pl.program_id(1) == 0)

    @pl.when(first)
    def _():
        run_ref[...] = jnp.zeros_like(run_ref)

    mix = _dot(lru_ref[...], wo1_ref[...]) + _dot(att_ref[...], wo2_ref[...])
    x1 = x_ref[...] + gate_ref[0] * mix
    x1_ref[...] = x1
    ms = jnp.mean(x1 * x1, axis=-1, keepdims=True)
    h2 = x1 * lax.rsqrt(ms + EPS) * gffn_ref[...]
    h2 = h2 * (1.0 + scale_ref[0]) + shift_ref[0]

    hhi = h2.astype(BF16)
    hhi32 = hhi.astype(F32)
    hlo = (h2 - hhi32).astype(BF16)
    h2p_ref[...] = _pack_halves(h2)

    wrh = wrh_ref[...]
    logits = _dot_nt(wrh, hhi) + _dot_nt(wrh, hlo) + _dot_nt(wrl_ref[...], hhi) + br_ref[...]
    ne, tm = logits.shape
    eio = lax.broadcasted_iota(jnp.int32, (ne, tm), 0)
    vals, idxs, sels = [], [], []
    l = logits
    for _ in range(TOP_K):
        m = jnp.max(l, axis=0, keepdims=True)
        idx = jnp.min(jnp.where(l == m, eio, ne), axis=0, keepdims=True)
        sel = eio == idx
        l = jnp.where(sel, -jnp.inf, l)
        vals.append(m)
        idxs.append(idx)
        sels.append(sel)
    es = [jnp.exp(v - vals[0]) for v in vals]
    den = es[0] + es[1] + es[2] + es[3]
    inv = 1.0 / den
    sel_any = jnp.where(sels[0] | sels[1] | sels[2] | sels[3], 1.0, 0.0)
    excl = _dot(sel_any.astype(BF16), tri_ref[...]) + run_ref[...]
    for kk in range(TOP_K):
        idx_ref[kk:kk + 1, :] = idxs[kk]
        gat_ref[kk:kk + 1, :] = es[kk] * inv
        rk = jnp.sum(jnp.where(sels[kk], excl, 0.0), axis=0, keepdims=True)
        rank_ref[kk:kk + 1, :] = rk.astype(jnp.int32)
    run = run_ref[...] + jnp.sum(sel_any, axis=1, keepdims=True)
    run_ref[...] = run
    cnt_ref[...] = run.astype(jnp.int32)


def _outproj(lru_o, att_o, x2, mod3, g_ffn, wo1, wo2, wr_hi, wr_lo, b_r, tri, b0, B, S, tm):
    D = x2.shape[1]
    T = B * S
    ns = S // tm
    C = lru_o.shape[1]
    row_in = lambda b, s: ((b0 + b) * ns + s, 0)
    row = lambda b, s: (b * ns + s, 0)
    col = lambda b, s: (0, b * ns + s)
    full = lambda b, s: (0, 0)
    return pl.pallas_call(
        _outproj_kernel,
        grid=(B, ns),
        in_specs=[
            pl.BlockSpec((tm, C), row_in),
            pl.BlockSpec((tm, C), row_in),
            pl.BlockSpec((tm, D), row_in),
            pl.BlockSpec((1, 1, D), lambda b, s: ((b0 + b) * 6 + 2, 0, 0)),
            pl.BlockSpec((1, 1, D), lambda b, s: ((b0 + b) * 6 + 3, 0, 0)),
            pl.BlockSpec((1, 1, D), lambda b, s: ((b0 + b) * 6 + 4, 0, 0)),
            pl.BlockSpec((1, D), full),
            pl.BlockSpec((C, D), full),
            pl.BlockSpec((C, D), full),
            pl.BlockSpec((N_EXPERTS, D), full),
            pl.BlockSpec((N_EXPERTS, D), full),
            pl.BlockSpec((N_EXPERTS, 1), full),
            pl.BlockSpec((tm, tm), full),
        ],
        out_specs=[
            pl.BlockSpec((tm, D), row),
            pl.BlockSpec((tm, D // 2), row),
            pl.BlockSpec((TOP_K, tm), col),
            pl.BlockSpec((TOP_K, tm), col),
            pl.BlockSpec((TOP_K, tm), col),
            pl.BlockSpec((N_EXPERTS, 1), full),
        ],
        out_shape=[
            jax.ShapeDtypeStruct((T, D), F32),
            jax.ShapeDtypeStruct((T, D // 2), jnp.int32),
            jax.ShapeDtypeStruct((TOP_K, T), jnp.int32),
            jax.ShapeDtypeStruct((TOP_K, T), F32),
            jax.ShapeDtypeStruct((TOP_K, T), jnp.int32),
            jax.ShapeDtypeStruct((N_EXPERTS, 1), jnp.int32),
        ],
        scratch_shapes=[pltpu.VMEM((N_EXPERTS, 1), F32)],
        compiler_params=_cparams(("arbitrary", "arbitrary")),
        name="outproj",
    )(lru_o, att_o, x2, mod3, mod3, mod3, g_ffn, wo1, wo2, wr_hi, wr_lo, b_r, tri)


SC_CORES = 2
SC_SUBCORES = 16
SC_WORKERS = SC_CORES * SC_SUBCORES


def _sc_mesh():
    return plsc.VectorSubcoreMesh(core_axis_name="c", subcore_axis_name="s",
                                  num_cores=SC_CORES, num_subcores=SC_SUBCORES)


def _sc_worker_id():
    return lax.axis_index("s") * SC_CORES + lax.axis_index("c")


def _sc_scatter_rows(rows, idx, n_out, g):
    T, W = rows.shape
    K = idx.shape[0]
    per_w = T // SC_WORKERS
    nch = per_w // g
    assert per_w * SC_WORKERS == T and nch * g == per_w and nch % 2 == 0
    idx_w = idx.reshape(K, SC_WORKERS, nch, g).transpose(1, 2, 0, 3).reshape(SC_WORKERS, nch * K, g)

    def body(rows_hbm, idx_hbm, out_hbm, idx_v, buf0, buf1, semr0, semr1, semw):
        wid = _sc_worker_id()
        base = wid * per_w
        pltpu.sync_copy(idx_hbm.at[wid], idx_v)

        def read(j, buf, sem):
            return pltpu.make_async_copy(rows_hbm.at[pl.ds(base + j * g, g)], buf, sem)

        def scatter(j, buf):
            copies = [pltpu.async_copy(buf, out_hbm.at[idx_v.at[j * K + kk]], semw)
                      for kk in range(K)]
            for cp in copies:
                cp.wait()

        read(0, buf0, semr0).start()

        @pl.loop(0, nch // 2)
        def _(jj):
            j0 = 2 * jj
            read(j0 + 1, buf1, semr1).start()
            read(j0, buf0, semr0).wait()
            scatter(j0, buf0)

            @pl.when(j0 + 2 < nch)
            def _():
                read(j0 + 2, buf0, semr0).start()

            read(j0 + 1, buf1, semr1).wait()
            scatter(j0 + 1, buf1)

    return pl.kernel(
        body,
        out_type=jax.ShapeDtypeStruct((n_out, W), rows.dtype),
        mesh=_sc_mesh(),
        scratch_types=[
            pltpu.VMEM((nch * K, g), jnp.int32),
            pltpu.VMEM((g, W), rows.dtype),
            pltpu.VMEM((g, W), rows.dtype),
            pltpu.SemaphoreType.DMA,
            pltpu.SemaphoreType.DMA,
            pltpu.SemaphoreType.DMA,
        ],
        name="sc_scatter_rows",
    )(rows, idx_w)


def _sc_gather_rows(table, idx, g):
    W = table.shape[1]
    N = idx.shape[0]
    per_w = N // SC_WORKERS
    nch = per_w // g
    assert per_w * SC_WORKERS == N and nch * g == per_w and nch % 2 == 0
    idx_w = idx.reshape(SC_WORKERS, nch, g)

    def body(table_hbm, idx_hbm, out_hbm, idx_v, buf0, buf1, sem0, sem1):
        wid = _sc_worker_id()
        base = wid * per_w
        pltpu.sync_copy(idx_hbm.at[wid], idx_v)

        def gather(j, buf, sem):
            return pltpu.make_async_copy(table_hbm.at[idx_v.at[j]], buf, sem)

        def put(j, buf):
            pltpu.sync_copy(buf, out_hbm.at[pl.ds(base + j * g, g)])

        gather(0, buf0, sem0).start()

        @pl.loop(0, nch // 2)
        def _(jj):
            j0 = 2 * jj
            gather(j0 + 1, buf1, sem1).start()
            gather(j0, buf0, sem0).wait()
            put(j0, buf0)

            @pl.when(j0 + 2 < nch)
            def _():
                gather(j0 + 2, buf0, sem0).start()

            gather(j0 + 1, buf1, sem1).wait()
            put(j0 + 1, buf1)

    return pl.kernel(
        body,
        out_type=jax.ShapeDtypeStruct((N, W), table.dtype),
        mesh=_sc_mesh(),
        scratch_types=[
            pltpu.VMEM((nch, g), jnp.int32),
            pltpu.VMEM((g, W), table.dtype),
            pltpu.VMEM((g, W), table.dtype),
            pltpu.SemaphoreType.DMA,
            pltpu.SemaphoreType.DMA,
        ],
        name="sc_gather_rows",
    )(table, idx_w)


def _experts_kernel(be_ref, bv_ref, xs_ref, w1_ref, b1_ref, w2_ref, b2_ref, ys_ref):
    i = pl.program_id(0)

    nvalid = bv_ref[i]

    @pl.when(nvalid > 0)
    def _():
        xw = xs_ref[...]
        rowi = lax.broadcasted_iota(jnp.int32, (xw.shape[0], 1), 0)
        lo, hi = _unpack_halves(jnp.where(rowi < nvalid, xw, 0))
        half = xw.shape[1]
        gu = (_dot(lo.astype(BF16), w1_ref[0, :half, :]) + _dot(hi.astype(BF16), w1_ref[0, half:, :])
              + b1_ref[0])
        glu = jnp.minimum(gu[:, :D_FF], SWIGLU_LIMIT)
        lin = jnp.clip(gu[:, D_FF:], -SWIGLU_LIMIT, SWIGLU_LIMIT)
        act = (lin + 1.0) * (glu * _sigmoid(SWIGLU_ALPHA * glu))
        ys_ref[...] = _pack_halves(_dot(act.astype(BF16), w2_ref[0]) + b2_ref[0])


def _experts(blk_e, blk_v, blk_r, xs, w1b, b1, w2b, b2):
    P, W = xs.shape
    nb = P // MOE_BLOCK
    E, D, F2 = w1b.shape
    grid_spec = pltpu.PrefetchScalarGridSpec(
        num_scalar_prefetch=3,
        grid=(nb,),
        in_specs=[
            pl.BlockSpec((MOE_BLOCK, W), lambda i, be, bv, br: (br[i], 0)),
            pl.BlockSpec((1, D, F2), lambda i, be, bv, br: (be[i], 0, 0)),
            pl.BlockSpec((1, 1, F2), lambda i, be, bv, br: (be[i], 0, 0)),
            pl.BlockSpec((1, D_FF, D), lambda i, be, bv, br: (be[i], 0, 0)),
            pl.BlockSpec((1, 1, D), lambda i, be, bv, br: (be[i], 0, 0)),
        ],
        out_specs=pl.BlockSpec((MOE_BLOCK, D // 2), lambda i, be, bv, br: (br[i], 0)),
    )

    def kern(be_ref, bv_ref, br_ref, *refs):
        del br_ref
        _experts_kernel(be_ref, bv_ref, *refs)

    return pl.pallas_call(
        kern,
        grid_spec=grid_spec,
        out_shape=jax.ShapeDtypeStruct((P, D // 2), jnp.int32),
        compiler_params=_cparams(("arbitrary",)),
        name="experts",
    )(blk_e, blk_v, blk_r, xs, w1b, b1.reshape(E, 1, F2), w2b, b2.reshape(E, 1, D))


def _combine_kernel(x1_ref, g_ref, gate_ref, y0_ref, y1_ref, y2_ref, y3_ref, *rest):
    o_ref = rest[-1]
    g = g_ref[...]
    half = y0_ref.shape[2]
    acc_lo = acc_hi = None
    for kk, y_ref in enumerate((y0_ref, y1_ref, y2_ref, y3_ref)):
        lo, hi = _unpack_halves(y_ref[0])
        gk = g[:, kk:kk + 1]
        acc_lo = lo * gk if acc_lo is None else acc_lo + lo * gk
        acc_hi = hi * gk if acc_hi is None else acc_hi + hi * gk
    gate = gate_ref[0]
    o_ref[:, :half] = x1_ref[:, :half] + gate[:, :half] * acc_lo
    o_ref[:, half:] = x1_ref[:, half:] + gate[:, half:] * acc_hi


def _combine(x1, gates_tk, mod3, ysg, out_prev, b0, B, nb_total, S, tm):
    D = x1.shape[1]
    ns = S // tm
    row = lambda b, s: (b * ns + s, 0)
    row_out = lambda b, s: ((b0 + b) * ns + s, 0)

    def yspec(kk):
        return pl.BlockSpec((1, tm, D // 2), lambda b, s: (kk, b * ns + s, 0))

    in_specs = [
        pl.BlockSpec((tm, D), row),
        pl.BlockSpec((tm, TOP_K), row),
        pl.BlockSpec((1, 1, D), lambda b, s: ((b0 + b) * 6 + 5, 0, 0)),
        yspec(0), yspec(1), yspec(2), yspec(3),
    ]
    args = [x1, gates_tk, mod3, ysg, ysg, ysg, ysg]
    aliases = {}
    if out_prev is not None:
        in_specs.append(pl.BlockSpec(memory_space=pl.ANY))
        args.append(out_prev)
        aliases = {len(args) - 1: 0}
    return pl.pallas_call(
        _combine_kernel,
        grid=(B, ns),
        in_specs=in_specs,
        out_specs=pl.BlockSpec((tm, D), row_out),
        out_shape=jax.ShapeDtypeStruct((nb_total * S, D), F32),
        input_output_aliases=aliases,
        compiler_params=_cparams(("arbitrary", "arbitrary")),
        name="combine",
    )(*args)


def _block_diag(w):
    n, c, d = w.shape
    eye = jnp.eye(n, dtype=w.dtype)
    return jnp.einsum("ncd,nm->ncmd", w, eye).reshape(n * c, n * d)


def _pad_heads(w, width):
    k = w.shape[0]
    w = w.reshape(k, N_HEADS, width)
    return jnp.pad(w, ((0, 0), (0, 0), (0, HEAD_PAD - width))).reshape(k, N_HEADS * HEAD_PAD)


def kernel(x, c, positions, w_ada, b_ada, g_mix, w_in, conv_w, conv_b, w_a, b_a, w_x, b_x, lam,
           g_q_lat, w_uq, g_kv_lat, w_ukv, g_qn, g_kn, w_out, g_ffn, w_router, b_router,
           w1, b1, w2, b2):
    B, S, D = x.shape
    T = B * S
    depth = w_ada.shape[0]
    tm_in = min(512, S)
    ts_lru = min(256, S)
    tq = min(512, S)
    tm_out = min(512, S)
    tm_comb = min(512, S)
    n_groups = 2 if B % 2 == 0 else 1
    Bg = B // n_groups
    Tg = Bg * S
    g_disp = min(64, Tg // SC_WORKERS // 2)
    g_comb = min(64, Tg * TOP_K // SC_WORKERS // 2)

    o1 = 2 * D_LRU
    o2 = o1 + Q_LORA
    o3 = o2 + KV_LORA
    tri = (jnp.arange(tm_out)[:, None] < jnp.arange(tm_out)[None, :]).astype(BF16)
    cos_t, sin_t = _rope_tables(positions)
    lane = jnp.arange(HEAD_PAD, dtype=jnp.int32)
    first = (lane >= ROPE_LO) & (lane < ROPE_LO + ROPE_HALF)
    second = (lane >= ROPE_LO + ROPE_HALF) & (lane < ROPE_LO + QK_ROPE)
    partner = jnp.where(first, lane + ROPE_HALF, jnp.where(second, lane - ROPE_HALF, 0))
    is_rot = first | second

    def rot_cols(w):
        k = w.shape[0]
        w3 = w.reshape(k, -1, HEAD_PAD)
        return jnp.where(is_rot[None, None, :], w3[:, :, partner], 0.0).reshape(w.shape)

    x2 = x.reshape(T, D)
    for l in range(depth):
        mod3 = _ada(c, w_ada[l], b_ada[l]).reshape(B * 6, 1, D)

        w_in_l = w_in[l]
        kr_cols = jnp.pad(w_in_l[:, o3:], ((0, 0), (ROPE_LO, LANES - ROPE_LO - QK_ROPE)))
        w_in_p = jnp.concatenate([w_in_l[:, :o3], kr_cols, rot_cols(kr_cols)], axis=1).astype(BF16)
        w_uq_h = _pad_heads(w_uq[l], QK_HEAD)
        w_uq_p = jnp.concatenate([w_uq_h, rot_cols(w_uq_h)], axis=1).astype(BF16)
        w_ukv_l = w_ukv[l].reshape(KV_LORA, N_HEADS, QK_NOPE + V_HEAD)
        w_uk_h = _pad_heads(w_ukv_l[:, :, :QK_NOPE].reshape(KV_LORA, N_HEADS * QK_NOPE), QK_NOPE)
        w_uv_h = _pad_heads(w_ukv_l[:, :, QK_NOPE:].reshape(KV_LORA, N_HEADS * V_HEAD), V_HEAD)
        w_ukv_p = jnp.concatenate([w_uk_h, w_uv_h], axis=1).astype(BF16)
        gqn_p = jnp.pad(g_qn[l], (0, HEAD_PAD - QK_HEAD)).reshape(1, HEAD_PAD)
        gkn_p = jnp.pad(g_kn[l], (0, HEAD_PAD - QK_HEAD)).reshape(1, HEAD_PAD)

        xlru, ylru, qp, kp, v = _inproj(
            x2, cos_t, sin_t, mod3, g_mix[l].reshape(1, D), w_in_p, g_q_lat[l].reshape(1, Q_LORA),
            w_uq_p, g_kv_lat[l].reshape(1, KV_LORA), w_ukv_p, gqn_p, rot_cols(gqn_p), gkn_p,
            rot_cols(gkn_p), B, S, tm_in)

        lru_o = _lru(xlru, ylru, conv_w[l], conv_b[l].reshape(1, D_LRU),
                     _block_diag(w_a[l]).astype(BF16), b_a[l].reshape(1, D_LRU),
                     _block_diag(w_x[l]).astype(BF16), b_x[l].reshape(1, D_LRU),
                     lam[l].reshape(1, D_LRU), B, S, ts_lru)

        att_o = _attn(qp, kp, v, B, S, tq)

        w_out_b = w_out[l].astype(BF16)
        wr_hi, wr_lo = _split_bf16(w_router[l].T)
        w1b = w1[l].astype(BF16)
        w2b = w2[l].astype(BF16)
        g_ffn_l = g_ffn[l].reshape(1, D)
        b_r = b_router[l].reshape(N_EXPERTS, 1)
        eio = jnp.arange(N_EXPERTS, dtype=jnp.int32)
        n_blocks = -(-(Tg * TOP_K) // MOE_BLOCK) + N_EXPERTS
        bi = jnp.arange(n_blocks, dtype=jnp.int32)

        x_next = None
        for gi in range(n_groups):
            b0 = gi * Bg
            x1, h2p, idx_t, gat_t, rank_t, counts = _outproj(
                lru_o, att_o, x2, mod3, g_ffn_l, w_out_b[:D_LRU], w_out_b[D_LRU:],
                wr_hi, wr_lo, b_r, tri, b0, Bg, S, tm_out)

            counts = counts.reshape(N_EXPERTS)
            nblk_e = (counts + MOE_BLOCK - 1) // MOE_BLOCK
            blk_end = jnp.cumsum(nblk_e)
            pad_start = (blk_end - nblk_e) * MOE_BLOCK
            total = blk_end[-1]
            blk_r = jnp.minimum(bi, total - 1).astype(jnp.int32)
            blk_e = jnp.minimum(jnp.sum(blk_end[None, :] <= blk_r[:, None], axis=1),
                                N_EXPERTS - 1).astype(jnp.int32)
            blk_onehot = blk_e[:, None] == eio[None, :]
            blk_first = jnp.sum(jnp.where(blk_onehot, (blk_end - nblk_e)[None, :], 0), axis=1)
            blk_cnt = jnp.sum(jnp.where(blk_onehot, counts[None, :], 0), axis=1)
            blk_n = jnp.where(bi < total,
                              jnp.clip(blk_cnt - (bi - blk_first) * MOE_BLOCK, 0, MOE_BLOCK),
                              0).astype(jnp.int32)
            slot0 = jnp.sum(jnp.where(idx_t[None] == eio[:, None, None],
                                      pad_start[:, None, None], 0), axis=0)
            dest = slot0.astype(jnp.int32) + rank_t

            xs = _sc_scatter_rows(h2p, dest, n_blocks * MOE_BLOCK, g_disp)
            ys = _experts(blk_e, blk_n, blk_r, xs, w1b, b1[l], w2b, b2[l])
            ysg = _sc_gather_rows(ys, dest.reshape(TOP_K * Tg), g_comb).reshape(TOP_K, Tg, D // 2)
            x_next = _combine(x1, gat_t.T, mod3, ysg, x_next, b0, Bg, B, S, tm_comb)
        x2 = x_next
    return x2.reshape(B, S, D)
```

```python
import functools

import jax
import jax.numpy as jnp
from jax import lax
from jax.experimental import pallas as pl
from jax.experimental.pallas import tpu as pltpu
from jax.experimental.pallas import tpu_sc as plsc

D_MODEL = 1024
D_LRU = 512
LRU_BLOCKS = 8
LRU_BD = 64
CONV_W = 4
LRU_C = 8.0
N_HEADS = 8
QK_NOPE = 64
QK_ROPE = 32
QK_HEAD = 96
V_HEAD = 64
Q_LORA = 256
KV_LORA = 128
ROPE_THETA = 10000.0
N_EXPERTS = 32
TOP_K = 4
D_FF = 1024
SWIGLU_LIMIT = 7.0
SWIGLU_ALPHA = 1.702
MOE_BLOCK = 512
EPS = 1e-6

LANES = 128
SUBLANES = 8
HEAD_PAD = 128
ROPE_LO = QK_NOPE
ROPE_HALF = QK_ROPE // 2
D_IN_PAD = 2 * D_LRU + Q_LORA + KV_LORA + 2 * LANES
LOG2_E = 1.4426950408889634
SUM_LANE = V_HEAD
MAX_LANE = V_HEAD + 1

VMEM_LIMIT = 56 * 1024 * 1024

F32 = jnp.float32
BF16 = jnp.bfloat16


def _cparams(sem):
    return pltpu.CompilerParams(dimension_semantics=sem, vmem_limit_bytes=VMEM_LIMIT)


def _dot(a, b):
    return jnp.dot(a, b, preferred_element_type=F32)


def _dot_nt(a, b):
    return lax.dot_general(a, b, (((1,), (1,)), ((), ())), preferred_element_type=F32)


def _split_bf16(a):
    hi = a.astype(BF16)
    lo = (a - hi.astype(F32)).astype(BF16)
    return hi, lo


def _sigmoid(x):
    return 1.0 / (1.0 + jnp.exp(-x))


def _pack_halves(x):
    bits = lax.bitcast_convert_type(x.astype(BF16).astype(F32), jnp.uint32)
    half = x.shape[1] // 2
    words = (bits[:, :half] >> 16) | (bits[:, half:] & jnp.uint32(0xFFFF0000))
    return lax.bitcast_convert_type(words, jnp.int32)


def _unpack_halves(words):
    w = lax.bitcast_convert_type(words, jnp.uint32)
    lo = lax.bitcast_convert_type(w << 16, F32)
    hi = lax.bitcast_convert_type(w & jnp.uint32(0xFFFF0000), F32)
    return lo, hi


def _ada_kernel(c_ref, whi_ref, wlo_ref, b_ref, o_ref):
    c = c_ref[...]
    s = c * _sigmoid(c)
    shi, slo = _split_bf16(s)
    whi = whi_ref[...]
    o_ref[...] = _dot(shi, whi) + _dot(slo, whi) + _dot(shi, wlo_ref[...]) + b_ref[...]


def _ada(c, w_ada, b_ada):
    B, D = c.shape
    N = w_ada.shape[1]
    tn = 1024
    whi, wlo = _split_bf16(w_ada)
    return pl.pallas_call(
        _ada_kernel,
        grid=(N // tn,),
        in_specs=[
            pl.BlockSpec((B, D), lambda j: (0, 0)),
            pl.BlockSpec((D, tn), lambda j: (0, j)),
            pl.BlockSpec((D, tn), lambda j: (0, j)),
            pl.BlockSpec((1, tn), lambda j: (0, j)),
        ],
        out_specs=pl.BlockSpec((B, tn), lambda j: (0, j)),
        out_shape=jax.ShapeDtypeStruct((B, N), F32),
        compiler_params=_cparams(("arbitrary",)),
        name="ada",
    )(c, whi, wlo, b_ada.reshape(1, N))


def _trig_kernel(pos_ref, freq_ref, cos_ref, sin_ref):
    ang = pos_ref[...].astype(F32) * freq_ref[...]
    cos_ref[...] = jnp.cos(ang)
    sin_ref[...] = jnp.sin(ang)


def _rope_tables(positions):
    T = positions.size
    per_row = LANES // ROPE_HALF
    freqs = ROPE_THETA ** (-jnp.arange(ROPE_HALF, dtype=F32) / ROPE_HALF)
    pos_c = jnp.repeat(positions.reshape(T).astype(jnp.int32), ROPE_HALF).reshape(T // per_row, LANES)
    freq_c = jnp.tile(freqs, per_row).reshape(1, LANES)
    rows = T // per_row
    tr = min(512, rows)
    cos_c, sin_c = pl.pallas_call(
        _trig_kernel,
        grid=(rows // tr,),
        in_specs=[pl.BlockSpec((tr, LANES), lambda i: (i, 0)), pl.BlockSpec((1, LANES), lambda i: (0, 0))],
        out_specs=[pl.BlockSpec((tr, LANES), lambda i: (i, 0))] * 2,
        out_shape=[jax.ShapeDtypeStruct((rows, LANES), F32)] * 2,
        compiler_params=_cparams(("arbitrary",)),
        name="rope_trig",
    )(pos_c, freq_c)
    cos16 = cos_c.reshape(T, ROPE_HALF)
    sin16 = sin_c.reshape(T, ROPE_HALF)
    tail = LANES - ROPE_LO - QK_ROPE
    cos_t = jnp.concatenate([jnp.ones((T, ROPE_LO), F32), cos16, cos16, jnp.ones((T, tail), F32)], axis=1)
    sin_t = jnp.concatenate([jnp.zeros((T, ROPE_LO), F32), -sin16, sin16, jnp.zeros((T, tail), F32)], axis=1)
    return cos_t, sin_t


def _inproj_kernel(x_ref, cos_ref, sin_ref, shift_ref, scale_ref, gmix_ref, win_ref, gq_ref, wuq_ref,
                   gkv_ref, wukv_ref, gqn_ref, gqr_ref, gkn_ref, gkr_ref,
                   xlru_ref, ylru_ref, q_ref, k_ref, v_ref):
    HP = N_HEADS * HEAD_PAD
    x = x_ref[...]
    ms = jnp.mean(x * x, axis=-1, keepdims=True)
    xn = x * lax.rsqrt(ms + EPS) * gmix_ref[...]
    h = xn * (1.0 + scale_ref[0]) + shift_ref[0]
    z = _dot(h.astype(BF16), win_ref[...])
    xlru_ref[...] = z[:, :D_LRU]
    ylru_ref[...] = z[:, D_LRU:2 * D_LRU]
    o1 = 2 * D_LRU
    o2 = o1 + Q_LORA
    o3 = o2 + KV_LORA
    ql = z[:, o1:o2]
    kvl = z[:, o2:o3]
    kr = z[:, o3:o3 + LANES]
    kr_rot = z[:, o3 + LANES:]

    qn = ql * lax.rsqrt(jnp.mean(ql * ql, axis=-1, keepdims=True) + EPS) * gq_ref[...]
    qq = _dot(qn.astype(BF16), wuq_ref[...])
    kvn = kvl * lax.rsqrt(jnp.mean(kvl * kvl, axis=-1, keepdims=True) + EPS) * gkv_ref[...]
    kv = _dot(kvn.astype(BF16), wukv_ref[...])

    tm = x.shape[0]
    lane = lax.broadcasted_iota(jnp.int32, (tm, HP), 1)
    v_ref[...] = jnp.where((lane & (HEAD_PAD - 1)) == V_HEAD, 1.0, kv[:, HP:]).astype(BF16)

    cos_t = cos_ref[...]
    sin_t = sin_ref[...]
    gqn = gqn_ref[...]
    gkn = gkn_ref[...]
    cq = gqn * cos_t
    sq = gqr_ref[...] * sin_t
    kb = kr * (gkn * cos_t) + kr_rot * (gkr_ref[...] * sin_t)
    inv_w = 1.0 / QK_HEAD
    qscale = QK_HEAD ** -0.5 * LOG2_E
    for hh in range(N_HEADS):
        sl = slice(hh * HEAD_PAD, (hh + 1) * HEAD_PAD)
        qh = qq[:, sl]
        rq = lax.rsqrt(jnp.sum(qh * qh, axis=-1, keepdims=True) * inv_w + EPS) * qscale
        q_ref[:, sl] = ((qh * cq + qq[:, HP + hh * HEAD_PAD:HP + (hh + 1) * HEAD_PAD] * sq) * rq).astype(BF16)
        kraw = kv[:, sl] + kr
        rk = lax.rsqrt(jnp.sum(kraw * kraw, axis=-1, keepdims=True) * inv_w + EPS)
        k_ref[:, sl] = ((kv[:, sl] * gkn + kb) * rk).astype(BF16)


def _inproj(x2, cos_t, sin_t, mod3, g_mix, w_in_p, g_q_lat, w_uq_p, g_kv_lat, w_ukv_p,
            gqn_p, gqr_p, gkn_p, gkr_p, B, S, tm):
    T, D = x2.shape
    ns = S // tm
    HP = N_HEADS * HEAD_PAD
    row = lambda b, s: (b * ns + s, 0)
    full = lambda b, s: (0, 0)
    return pl.pallas_call(
        _inproj_kernel,
        grid=(B, ns),
        in_specs=[
            pl.BlockSpec((tm, D), row),
            pl.BlockSpec((tm, LANES), row),
            pl.BlockSpec((tm, LANES), row),
            pl.BlockSpec((1, 1, D), lambda b, s: (b * 6 + 0, 0, 0)),
            pl.BlockSpec((1, 1, D), lambda b, s: (b * 6 + 1, 0, 0)),
            pl.BlockSpec((1, D), full),
            pl.BlockSpec((D, D_IN_PAD), full),
            pl.BlockSpec((1, Q_LORA), full),
            pl.BlockSpec((Q_LORA, 2 * HP), full),
            pl.BlockSpec((1, KV_LORA), full),
            pl.BlockSpec((KV_LORA, 2 * HP), full),
            pl.BlockSpec((1, HEAD_PAD), full),
            pl.BlockSpec((1, HEAD_PAD), full),
            pl.BlockSpec((1, HEAD_PAD), full),
            pl.BlockSpec((1, HEAD_PAD), full),
        ],
        out_specs=[
            pl.BlockSpec((tm, D_LRU), row),
            pl.BlockSpec((tm, D_LRU), row),
            pl.BlockSpec((tm, HP), row),
            pl.BlockSpec((tm, HP), row),
            pl.BlockSpec((tm, HP), row),
        ],
        out_shape=[
            jax.ShapeDtypeStruct((T, D_LRU), F32),
            jax.ShapeDtypeStruct((T, D_LRU), F32),
            jax.ShapeDtypeStruct((T, HP), BF16),
            jax.ShapeDtypeStruct((T, HP), BF16),
            jax.ShapeDtypeStruct((T, HP), BF16),
        ],
        compiler_params=_cparams(("arbitrary", "arbitrary")),
        name="inproj",
    )(x2, cos_t, sin_t, mod3, mod3, g_mix, w_in_p, g_q_lat, w_uq_p, g_kv_lat, w_ukv_p,
      gqn_p, gqr_p, gkn_p, gkr_p)


def _gelu_tanh(x):
    return 0.5 * x * (1.0 + jnp.tanh(0.7978845608028654 * (x + 0.044715 * x * x * x)))


def _lru_kernel(x_ref, y_ref, cw_ref, cb_ref, wa_ref, ba_ref, wx_ref, bx_ref, lam_ref,
                o_ref, tail_ref, carry_ref):
    s = pl.program_id(1)

    @pl.when(s == 0)
    def _():
        tail_ref[...] = jnp.zeros_like(tail_ref)
        carry_ref[...] = jnp.zeros_like(carry_ref)

    x = x_ref[...]
    ts = x.shape[0]
    xext = jnp.concatenate([tail_ref[...], x], axis=0)
    cw = cw_ref[...]
    xc = x * cw[CONV_W - 1:CONV_W, :]
    for j in range(CONV_W - 1):
        sh = CONV_W - 1 - j
        xc = xc + xext[8 - sh:8 - sh + ts, :] * cw[j:j + 1, :]
    xc = xc + cb_ref[...]
    tail_ref[...] = x[ts - 8:, :]

    xb = xc.astype(BF16)
    r = _sigmoid(_dot(xb, wa_ref[...]) + ba_ref[...])
    i = _sigmoid(_dot(xb, wx_ref[...]) + bx_ref[...])
    lam = lam_ref[...]
    nl = -lam
    softplus = jnp.maximum(nl, 0.0) + jnp.log(1.0 + jnp.exp(-jnp.abs(nl)))
    log_a = (-LRU_C) * r * softplus
    a = jnp.exp(log_a)
    mult = jnp.sqrt(1.0 - jnp.exp(2.0 * log_a))
    u = mult * (i * xc)

    sub = lax.broadcasted_iota(jnp.int32, (ts, 1), 0) & (SUBLANES - 1)
    sh = 1
    while sh < SUBLANES:
        a_prev = pltpu.roll(a, sh, axis=0)
        u_prev = pltpu.roll(u, sh, axis=0)
        m = sub >= sh
        u = jnp.where(m, a * u_prev + u, u)
        a = jnp.where(m, a * a_prev, a)
        sh *= 2
    h = carry_ref[0:1, :]
    groups = []
    for g0 in range(0, ts, SUBLANES):
        hg = u[g0:g0 + SUBLANES, :] + a[g0:g0 + SUBLANES, :] * h
        groups.append(hg)
        h = hg[SUBLANES - 1:SUBLANES, :]
    carry_ref[...] = jnp.broadcast_to(h, carry_ref.shape)
    hs = jnp.concatenate(groups, axis=0)
    o_ref[...] = (_gelu_tanh(y_ref[...]) * hs).astype(BF16)


def _lru(xlru, ylru, conv_w, conv_b, wa_d, b_a, wx_d, b_x, lam, B, S, ts):
    T, C = xlru.shape
    ns = S // ts
    row = lambda b, s: (b * ns + s, 0)
    full = lambda b, s: (0, 0)
    return pl.pallas_call(
        _lru_kernel,
        grid=(B, ns),
        in_specs=[
            pl.BlockSpec((ts, C), row),
            pl.BlockSpec((ts, C), row),
            pl.BlockSpec((CONV_W, C), full),
            pl.BlockSpec((1, C), full),
            pl.BlockSpec((C, C), full),
            pl.BlockSpec((1, C), full),
            pl.BlockSpec((C, C), full),
            pl.BlockSpec((1, C), full),
            pl.BlockSpec((1, C), full),
        ],
        out_specs=pl.BlockSpec((ts, C), row),
        out_shape=jax.ShapeDtypeStruct((T, C), BF16),
        scratch_shapes=[pltpu.VMEM((8, C), F32), pltpu.VMEM((8, C), F32)],
        compiler_params=_cparams(("arbitrary", "arbitrary")),
        name="lru",
    )(xlru, ylru, conv_w, conv_b, wa_d, b_a, wx_d, b_x, lam)


NEG_INF = -1e30


def _attn_kernel(q_ref, k_ref, v_ref, o_ref, *state, tq):
    m_refs = state[:N_HEADS]
    acc_refs = state[N_HEADS:]
    qi = pl.program_id(1)
    rowi = lax.broadcasted_iota(jnp.int32, (tq, tq), 0)
    coli = lax.broadcasted_iota(jnp.int32, (tq, tq), 1)
    diag_mask = coli <= rowi

    def head_slice(hh):
        return slice(hh * HEAD_PAD, (hh + 1) * HEAD_PAD)

    def weights(sc, m_b):
        cols = [jnp.exp2(sc[:, c0:c0 + LANES] - m_b) for c0 in range(0, tq, LANES)]
        return jnp.concatenate(cols, axis=1).astype(BF16)

    def scores(hh, r0):
        hs = head_slice(hh)
        return _dot_nt(q_ref[:, hs], k_ref[pl.ds(r0, tq), hs])

    r_diag = pl.multiple_of(qi * tq, tq)
    sc_next = scores(0, r_diag)
    for hh in range(N_HEADS):
        hs = head_slice(hh)
        sc = jnp.where(diag_mask, sc_next, NEG_INF)
        if hh + 1 < N_HEADS:
            sc_next = scores(hh + 1, r_diag)
        m_b = jnp.broadcast_to(jnp.max(sc, axis=-1, keepdims=True), (tq, LANES))
        m_refs[hh][...] = m_b
        acc_refs[hh][...] = _dot(weights(sc, m_b), v_ref[pl.ds(r_diag, tq), hs])

    @pl.loop(0, qi)
    def _(j):
        r0 = pl.multiple_of(j * tq, tq)
        sc_next = scores(0, r0)
        for hh in range(N_HEADS):
            hs = head_slice(hh)
            sc = sc_next
            if hh + 1 < N_HEADS:
                sc_next = scores(hh + 1, r0)
            m_b = m_refs[hh][...]
            m_new = jnp.maximum(m_b, jnp.max(sc, axis=-1, keepdims=True))
            alpha = jnp.exp2(m_b - m_new)
            m_refs[hh][...] = m_new
            acc_refs[hh][...] = (alpha * acc_refs[hh][...]
                                 + _dot(weights(sc, m_new), v_ref[pl.ds(r0, tq), hs]))

    for hh in range(N_HEADS):
        acc = acc_refs[hh][...]
        o = acc[:, :V_HEAD] / acc[:, SUM_LANE:SUM_LANE + 1]
        o_ref[:, hh * V_HEAD:(hh + 1) * V_HEAD] = o.astype(BF16)


def _attn(qp, kp, v, B, S, tq):
    T = qp.shape[0]
    nq = S // tq
    HP = N_HEADS * HEAD_PAD
    HV = N_HEADS * V_HEAD
    return pl.pallas_call(
        functools.partial(_attn_kernel, tq=tq),
        grid=(B, nq),
        in_specs=[
            pl.BlockSpec((tq, HP), lambda b, i: (b * nq + i, 0)),
            pl.BlockSpec((S, HP), lambda b, i: (b, 0)),
            pl.BlockSpec((S, HP), lambda b, i: (b, 0)),
        ],
        out_specs=pl.BlockSpec((tq, HV), lambda b, i: (b * nq + i, 0)),
        out_shape=jax.ShapeDtypeStruct((T, HV), BF16),
        scratch_shapes=([pltpu.VMEM((tq, LANES), F32)] * N_HEADS
                        + [pltpu.VMEM((tq, HEAD_PAD), F32)] * N_HEADS),
        compiler_params=_cparams(("arbitrary", "arbitrary")),
        name="attn",
    )(qp, kp, v)


def _outproj_kernel(lru_ref, att_ref, x_ref, gate_ref, shift_ref, scale_ref, gffn_ref,
                    wo1_ref, wo2_ref, wrh_ref, wrl_ref, br_ref, tri_ref,
                    x1_ref, h2p_ref, idx_ref, gat_ref, rank_ref, cnt_ref, run_ref):
    first = (pl.program_id(0) == 0) & (pl.program_id(1) == 0)

    @pl.when(first)
    def _():
        run_ref[...] = jnp.zeros_like(run_ref)

    mix = _dot(lru_ref[...], wo1_ref[...]) + _dot(att_ref[...], wo2_ref[...])
    x1 = x_ref[...] + gate_ref[0] * mix
    x1_ref[...] = x1
    ms = jnp.mean(x1 * x1, axis=-1, keepdims=True)
    h2 = x1 * lax.rsqrt(ms + EPS) * gffn_ref[...]
    h2 = h2 * (1.0 + scale_ref[0]) + shift_ref[0]

    hhi = h2.astype(BF16)
    hhi32 = hhi.astype(F32)
    hlo = (h2 - hhi32).astype(BF16)
    h2p_ref[...] = _pack_halves(h2)

    wrh = wrh_ref[...]
    logits = _dot_nt(wrh, hhi) + _dot_nt(wrh, hlo) + _dot_nt(wrl_ref[...], hhi) + br_ref[...]
    ne, tm = logits.shape
    eio = lax.broadcasted_iota(jnp.int32, (ne, tm), 0)
    vals, idxs, sels = [], [], []
    l = logits
    for _ in range(TOP_K):
        m = jnp.max(l, axis=0, keepdims=True)
        idx = jnp.min(jnp.where(l == m, eio, ne), axis=0, keepdims=True)
        sel = eio == idx
        l = jnp.where(sel, -jnp.inf, l)
        vals.append(m)
        idxs.append(idx)
        sels.append(sel)
    es = [jnp.exp(v - vals[0]) for v in vals]
    den = es[0] + es[1] + es[2] + es[3]
    inv = 1.0 / den
    sel_any = jnp.where(sels[0] | sels[1] | sels[2] | sels[3], 1.0, 0.0)
    excl = _dot(sel_any.astype(BF16), tri_ref[...]) + run_ref[...]
    for kk in range(TOP_K):
        idx_ref[kk:kk + 1, :] = idxs[kk]
        gat_ref[kk:kk + 1, :] = es[kk] * inv
        rk = jnp.sum(jnp.where(sels[kk], excl, 0.0), axis=0, keepdims=True)
        rank_ref[kk:kk + 1, :] = rk.astype(jnp.int32)
    run = run_ref[...] + jnp.sum(sel_any, axis=1, keepdims=True)
    run_ref[...] = run
    cnt_ref[...] = run.astype(jnp.int32)


def _outproj(lru_o, att_o, x2, mod3, g_ffn, wo1, wo2, wr_hi, wr_lo, b_r, tri, b0, B, S, tm):
    D = x2.shape[1]
    T = B * S
    ns = S // tm
    C = lru_o.shape[1]
    row_in = lambda b, s: ((b0 + b) * ns + s, 0)
    row = lambda b, s: (b * ns + s, 0)
    col = lambda b, s: (0, b * ns + s)
    full = lambda b, s: (0, 0)
    return pl.pallas_call(
        _outproj_kernel,
        grid=(B, ns),
        in_specs=[
            pl.BlockSpec((tm, C), row_in),
            pl.BlockSpec((tm, C), row_in),
            pl.BlockSpec((tm, D), row_in),
            pl.BlockSpec((1, 1, D), lambda b, s: ((b0 + b) * 6 + 2, 0, 0)),
            pl.BlockSpec((1, 1, D), lambda b, s: ((b0 + b) * 6 + 3, 0, 0)),
            pl.BlockSpec((1, 1, D), lambda b, s: ((b0 + b) * 6 + 4, 0, 0)),
            pl.BlockSpec((1, D), full),
            pl.BlockSpec((C, D), full),
            pl.BlockSpec((C, D), full),
            pl.BlockSpec((N_EXPERTS, D), full),
            pl.BlockSpec((N_EXPERTS, D), full),
            pl.BlockSpec((N_EXPERTS, 1), full),
            pl.BlockSpec((tm, tm), full),
        ],
        out_specs=[
            pl.BlockSpec((tm, D), row),
            pl.BlockSpec((tm, D // 2), row),
            pl.BlockSpec((TOP_K, tm), col),
            pl.BlockSpec((TOP_K, tm), col),
            pl.BlockSpec((TOP_K, tm), col),
            pl.BlockSpec((N_EXPERTS, 1), full),
        ],
        out_shape=[
            jax.ShapeDtypeStruct((T, D), F32),
            jax.ShapeDtypeStruct((T, D // 2), jnp.int32),
            jax.ShapeDtypeStruct((TOP_K, T), jnp.int32),
            jax.ShapeDtypeStruct((TOP_K, T), F32),
            jax.ShapeDtypeStruct((TOP_K, T), jnp.int32),
            jax.ShapeDtypeStruct((N_EXPERTS, 1), jnp.int32),
        ],
        scratch_shapes=[pltpu.VMEM((N_EXPERTS, 1), F32)],
        compiler_params=_cparams(("arbitrary", "arbitrary")),
        name="outproj",
    )(lru_o, att_o, x2, mod3, mod3, mod3, g_ffn, wo1, wo2, wr_hi, wr_lo, b_r, tri)


SC_CORES = 2
SC_SUBCORES = 16
SC_WORKERS = SC_CORES * SC_SUBCORES


def _sc_mesh():
    return plsc.VectorSubcoreMesh(core_axis_name="c", subcore_axis_name="s",
                                  num_cores=SC_CORES, num_subcores=SC_SUBCORES)


def _sc_worker_id():
    return lax.axis_index("s") * SC_CORES + lax.axis_index("c")


def _sc_scatter_rows(rows, idx, n_out, g):
    T, W = rows.shape
    K = idx.shape[0]
    per_w = T // SC_WORKERS
    nch = per_w // g
    assert per_w * SC_WORKERS == T and nch * g == per_w and nch % 2 == 0
    idx_w = idx.reshape(K, SC_WORKERS, nch, g).transpose(1, 2, 0, 3).reshape(SC_WORKERS, nch * K, g)

    def body(rows_hbm, idx_hbm, out_hbm, idx_v, buf0, buf1, semr0, semr1, semw):
        wid = _sc_worker_id()
        base = wid * per_w
        pltpu.sync_copy(idx_hbm.at[wid], idx_v)

        def read(j, buf, sem):
            return pltpu.make_async_copy(rows_hbm.at[pl.ds(base + j * g, g)], buf, sem)

        def scatter(j, buf):
            copies = [pltpu.async_copy(buf, out_hbm.at[idx_v.at[j * K + kk]], semw)
                      for kk in range(K)]
            for cp in copies:
                cp.wait()

        read(0, buf0, semr0).start()

        @pl.loop(0, nch // 2)
        def _(jj):
            j0 = 2 * jj
            read(j0 + 1, buf1, semr1).start()
            read(j0, buf0, semr0).wait()
            scatter(j0, buf0)

            @pl.when(j0 + 2 < nch)
            def _():
                read(j0 + 2, buf0, semr0).start()

            read(j0 + 1, buf1, semr1).wait()
            scatter(j0 + 1, buf1)

    return pl.kernel(
        body,
        out_type=jax.ShapeDtypeStruct((n_out, W), rows.dtype),
        mesh=_sc_mesh(),
        scratch_types=[
            pltpu.VMEM((nch * K, g), jnp.int32),
            pltpu.VMEM((g, W), rows.dtype),
            pltpu.VMEM((g, W), rows.dtype),
            pltpu.SemaphoreType.DMA,
            pltpu.SemaphoreType.DMA,
            pltpu.SemaphoreType.DMA,
        ],
        name="sc_scatter_rows",
    )(rows, idx_w)


def _sc_gather_rows(table, idx, g):
    W = table.shape[1]
    N = idx.shape[0]
    per_w = N // SC_WORKERS
    nch = per_w // g
    assert per_w * SC_WORKERS == N and nch * g == per_w and nch % 2 == 0
    idx_w = idx.reshape(SC_WORKERS, nch, g)

    def body(table_hbm, idx_hbm, out_hbm, idx_v, buf0, buf1, sem0, sem1):
        wid = _sc_worker_id()
        base = wid * per_w
        pltpu.sync_copy(idx_hbm.at[wid], idx_v)

        def gather(j, buf, sem):
            return pltpu.make_async_copy(table_hbm.at[idx_v.at[j]], buf, sem)

        def put(j, buf):
            pltpu.sync_copy(buf, out_hbm.at[pl.ds(base + j * g, g)])

        gather(0, buf0, sem0).start()

        @pl.loop(0, nch // 2)
        def _(jj):
            j0 = 2 * jj
            gather(j0 + 1, buf1, sem1).start()
            gather(j0, buf0, sem0).wait()
            put(j0, buf0)

            @pl.when(j0 + 2 < nch)
            def _():
                gather(j0 + 2, buf0, sem0).start()

            gather(j0 + 1, buf1, sem1).wait()
            put(j0 + 1, buf1)

    return pl.kernel(
        body,
        out_type=jax.ShapeDtypeStruct((N, W), table.dtype),
        mesh=_sc_mesh(),
        scratch_types=[
            pltpu.VMEM((nch, g), jnp.int32),
            pltpu.VMEM((g, W), table.dtype),
            pltpu.VMEM((g, W), table.dtype),
            pltpu.SemaphoreType.DMA,
            pltpu.SemaphoreType.DMA,
        ],
        name="sc_gather_rows",
    )(table, idx_w)


def _experts_kernel(be_ref, bv_ref, bf_ref, xs_ref, w1_ref, b1_ref, w2_ref, b2_ref, ys_ref,
                    w1b_ref, w2b_ref):
    i = pl.program_id(0)
    nvalid = bv_ref[i]

    @pl.when(bf_ref[i] > 0)
    def _():
        w1b_ref[...] = w1_ref[0].astype(BF16)
        w2b_ref[...] = w2_ref[0].astype(BF16)

    @pl.when(nvalid > 0)
    def _():
        xw = xs_ref[...]
        rowi = lax.broadcasted_iota(jnp.int32, (xw.shape[0], 1), 0)
        lo, hi = _unpack_halves(jnp.where(rowi < nvalid, xw, 0))
        half = xw.shape[1]
        gu = (_dot(lo.astype(BF16), w1b_ref[:half, :]) + _dot(hi.astype(BF16), w1b_ref[half:, :])
              + b1_ref[0])
        glu = jnp.minimum(gu[:, :D_FF], SWIGLU_LIMIT)
        lin = jnp.clip(gu[:, D_FF:], -SWIGLU_LIMIT, SWIGLU_LIMIT)
        act = (lin + 1.0) * (glu * _sigmoid(SWIGLU_ALPHA * glu))
        ys_ref[...] = _pack_halves(_dot(act.astype(BF16), w2b_ref[...]) + b2_ref[0])


def _experts(blk_e, blk_v, blk_f, blk_r, xs, w1, b1, w2, b2):
    P, W = xs.shape
    nb = P // MOE_BLOCK
    E, D, F2 = w1.shape
    grid_spec = pltpu.PrefetchScalarGridSpec(
        num_scalar_prefetch=4,
        grid=(nb,),
        in_specs=[
            pl.BlockSpec((MOE_BLOCK, W), lambda i, be, bv, bf, br: (br[i], 0)),
            pl.BlockSpec((1, D, F2), lambda i, be, bv, bf, br: (be[i], 0, 0)),
            pl.BlockSpec((1, 1, F2), lambda i, be, bv, bf, br: (be[i], 0, 0)),
            pl.BlockSpec((1, D_FF, D), lambda i, be, bv, bf, br: (be[i], 0, 0)),
            pl.BlockSpec((1, 1, D), lambda i, be, bv, bf, br: (be[i], 0, 0)),
        ],
        out_specs=pl.BlockSpec((MOE_BLOCK, D // 2), lambda i, be, bv, bf, br: (br[i], 0)),
        scratch_shapes=[pltpu.VMEM((D, F2), BF16), pltpu.VMEM((D_FF, D), BF16)],
    )

    def kern(be_ref, bv_ref, bf_ref, br_ref, *refs):
        del br_ref
        _experts_kernel(be_ref, bv_ref, bf_ref, *refs)

    return pl.pallas_call(
        kern,
        grid_spec=grid_spec,
        out_shape=jax.ShapeDtypeStruct((P, D // 2), jnp.int32),
        compiler_params=_cparams(("arbitrary",)),
        name="experts",
    )(blk_e, blk_v, blk_f, blk_r, xs, w1, b1.reshape(E, 1, F2), w2, b2.reshape(E, 1, D))


def _combine_kernel(x1_ref, g_ref, gate_ref, y0_ref, y1_ref, y2_ref, y3_ref, *rest):
    o_ref = rest[-1]
    g = g_ref[...]
    half = y0_ref.shape[2]
    acc_lo = acc_hi = None
    for kk, y_ref in enumerate((y0_ref, y1_ref, y2_ref, y3_ref)):
        lo, hi = _unpack_halves(y_ref[0])
        gk = g[:, kk:kk + 1]
        acc_lo = lo * gk if acc_lo is None else acc_lo + lo * gk
        acc_hi = hi * gk if acc_hi is None else acc_hi + hi * gk
    gate = gate_ref[0]
    o_ref[:, :half] = x1_ref[:, :half] + gate[:, :half] * acc_lo
    o_ref[:, half:] = x1_ref[:, half:] + gate[:, half:] * acc_hi


def _combine(x1, gates_tk, mod3, ysg, out_prev, b0, B, nb_total, S, tm):
    D = x1.shape[1]
    ns = S // tm
    row = lambda b, s: (b * ns + s, 0)
    row_out = lambda b, s: ((b0 + b) * ns + s, 0)

    def yspec(kk):
        return pl.BlockSpec((1, tm, D // 2), lambda b, s: (kk, b * ns + s, 0))

    in_specs = [
        pl.BlockSpec((tm, D), row),
        pl.BlockSpec((tm, TOP_K), row),
        pl.BlockSpec((1, 1, D), lambda b, s: ((b0 + b) * 6 + 5, 0, 0)),
        yspec(0), yspec(1), yspec(2), yspec(3),
    ]
    args = [x1, gates_tk, mod3, ysg, ysg, ysg, ysg]
    aliases = {}
    if out_prev is not None:
        in_specs.append(pl.BlockSpec(memory_space=pl.ANY))
        args.append(out_prev)
        aliases = {len(args) - 1: 0}
    return pl.pallas_call(
        _combine_kernel,
        grid=(B, ns),
        in_specs=in_specs,
        out_specs=pl.BlockSpec((tm, D), row_out),
        out_shape=jax.ShapeDtypeStruct((nb_total * S, D), F32),
        input_output_aliases=aliases,
        compiler_params=_cparams(("arbitrary", "arbitrary")),
        name="combine",
    )(*args)


def _block_diag(w):
    n, c, d = w.shape
    eye = jnp.eye(n, dtype=w.dtype)
    return jnp.einsum("ncd,nm->ncmd", w, eye).reshape(n * c, n * d)


def _pad_heads(w, width):
    k = w.shape[0]
    w = w.reshape(k, N_HEADS, width)
    return jnp.pad(w, ((0, 0), (0, 0), (0, HEAD_PAD - width))).reshape(k, N_HEADS * HEAD_PAD)


def kernel(x, c, positions, w_ada, b_ada, g_mix, w_in, conv_w, conv_b, w_a, b_a, w_x, b_x, lam,
           g_q_lat, w_uq, g_kv_lat, w_ukv, g_qn, g_kn, w_out, g_ffn, w_router, b_router,
           w1, b1, w2, b2):
    B, S, D = x.shape
    T = B * S
    depth = w_ada.shape[0]
    tm_in = min(512, S)
    ts_lru = min(512, S)
    tq = min(512, S)
    tm_out = min(512, S)
    tm_comb = min(512, S)
    n_groups = 2 if B % 2 == 0 else 1
    Bg = B // n_groups
    Tg = Bg * S
    g_disp = min(64, Tg // SC_WORKERS // 2)
    g_comb = min(64, Tg * TOP_K // SC_WORKERS // 2)

    o1 = 2 * D_LRU
    o2 = o1 + Q_LORA
    o3 = o2 + KV_LORA
    tri = (jnp.arange(tm_out)[:, None] < jnp.arange(tm_out)[None, :]).astype(BF16)
    cos_t, sin_t = _rope_tables(positions)
    lane = jnp.arange(HEAD_PAD, dtype=jnp.int32)
    first = (lane >= ROPE_LO) & (lane < ROPE_LO + ROPE_HALF)
    second = (lane >= ROPE_LO + ROPE_HALF) & (lane < ROPE_LO + QK_ROPE)
    partner = jnp.where(first, lane + ROPE_HALF, jnp.where(second, lane - ROPE_HALF, 0))
    is_rot = first | second

    def rot_cols(w):
        k = w.shape[0]
        w3 = w.reshape(k, -1, HEAD_PAD)
        return jnp.where(is_rot[None, None, :], w3[:, :, partner], 0.0).reshape(w.shape)

    x2 = x.reshape(T, D)
    for l in range(depth):
        mod3 = _ada(c, w_ada[l], b_ada[l]).reshape(B * 6, 1, D)

        w_in_l = w_in[l]
        kr_cols = jnp.pad(w_in_l[:, o3:], ((0, 0), (ROPE_LO, LANES - ROPE_LO - QK_ROPE)))
        w_in_p = jnp.concatenate([w_in_l[:, :o3], kr_cols, rot_cols(kr_cols)], axis=1).astype(BF16)
        w_uq_h = _pad_heads(w_uq[l], QK_HEAD)
        w_uq_p = jnp.concatenate([w_uq_h, rot_cols(w_uq_h)], axis=1).astype(BF16)
        w_ukv_l = w_ukv[l].reshape(KV_LORA, N_HEADS, QK_NOPE + V_HEAD)
        w_uk_h = _pad_heads(w_ukv_l[:, :, :QK_NOPE].reshape(KV_LORA, N_HEADS * QK_NOPE), QK_NOPE)
        w_uv_h = _pad_heads(w_ukv_l[:, :, QK_NOPE:].reshape(KV_LORA, N_HEADS * V_HEAD), V_HEAD)
        w_ukv_p = jnp.concatenate([w_uk_h, w_uv_h], axis=1).astype(BF16)
        gqn_p = jnp.pad(g_qn[l], (0, HEAD_PAD - QK_HEAD)).reshape(1, HEAD_PAD)
        gkn_p = jnp.pad(g_kn[l], (0, HEAD_PAD - QK_HEAD)).reshape(1, HEAD_PAD)

        xlru, ylru, qp, kp, v = _inproj(
            x2, cos_t, sin_t, mod3, g_mix[l].reshape(1, D), w_in_p, g_q_lat[l].reshape(1, Q_LORA),
            w_uq_p, g_kv_lat[l].reshape(1, KV_LORA), w_ukv_p, gqn_p, rot_cols(gqn_p), gkn_p,
            rot_cols(gkn_p), B, S, tm_in)

        lru_o = _lru(xlru, ylru, conv_w[l], conv_b[l].reshape(1, D_LRU),
                     _block_diag(w_a[l]).astype(BF16), b_a[l].reshape(1, D_LRU),
                     _block_diag(w_x[l]).astype(BF16), b_x[l].reshape(1, D_LRU),
                     lam[l].reshape(1, D_LRU), B, S, ts_lru)

        att_o = _attn(qp, kp, v, B, S, tq)

        w_out_b = w_out[l].astype(BF16)
        wr_hi, wr_lo = _split_bf16(w_router[l].T)
        g_ffn_l = g_ffn[l].reshape(1, D)
        b_r = b_router[l].reshape(N_EXPERTS, 1)
        eio = jnp.arange(N_EXPERTS, dtype=jnp.int32)
        n_blocks = -(-(Tg * TOP_K) // MOE_BLOCK) + N_EXPERTS
        bi = jnp.arange(n_blocks, dtype=jnp.int32)

        x_next = None
        for gi in range(n_groups):
            b0 = gi * Bg
            x1, h2p, idx_t, gat_t, rank_t, counts = _outproj(
                lru_o, att_o, x2, mod3, g_ffn_l, w_out_b[:D_LRU], w_out_b[D_LRU:],
                wr_hi, wr_lo, b_r, tri, b0, Bg, S, tm_out)

            counts = counts.reshape(N_EXPERTS)
            nblk_e = (counts + MOE_BLOCK - 1) // MOE_BLOCK
            blk_end = jnp.cumsum(nblk_e)
            pad_start = (blk_end - nblk_e) * MOE_BLOCK
            total = blk_end[-1]
            blk_r = jnp.minimum(bi, total - 1).astype(jnp.int32)
            blk_e = jnp.minimum(jnp.sum(blk_end[None, :] <= blk_r[:, None], axis=1),
                                N_EXPERTS - 1).astype(jnp.int32)
            blk_onehot = blk_e[:, None] == eio[None, :]
            blk_first = jnp.sum(jnp.where(blk_onehot, (blk_end - nblk_e)[None, :], 0), axis=1)
            blk_cnt = jnp.sum(jnp.where(blk_onehot, counts[None, :], 0), axis=1)
            blk_n = jnp.where(bi < total,
                              jnp.clip(blk_cnt - (bi - blk_first) * MOE_BLOCK, 0, MOE_BLOCK),
                              0).astype(jnp.int32)
            slot0 = jnp.sum(jnp.where(idx_t[None] == eio[:, None, None],
                                      pad_start[:, None, None], 0), axis=0)
            dest = slot0.astype(jnp.int32) + rank_t

            xs = _sc_scatter_rows(h2p, dest, n_blocks * MOE_BLOCK, g_disp)
            blk_f = ((bi == blk_first) & (bi < total)).astype(jnp.int32)
            ys = _experts(blk_e, blk_n, blk_f, blk_r, xs, w1[l], b1[l], w2[l], b2[l])
            ysg = _sc_gather_rows(ys, dest.reshape(TOP_K * Tg), g_comb).reshape(TOP_K, Tg, D // 2)
            x_next = _combine(x1, gat_t.T, mod3, ysg, x_next, b0, Bg, B, S, tm_comb)
        x2 = x_next
    return x2.reshape(B, S, D)
```

```python
import functools

import jax
import jax.numpy as jnp
from jax import lax
from jax.experimental import pallas as pl
from jax.experimental.pallas import tpu as pltpu
from jax.experimental.pallas import tpu_sc as plsc

D_MODEL = 1024
D_LRU = 512
LRU_BLOCKS = 8
LRU_BD = 64
CONV_W = 4
LRU_C = 8.0
N_HEADS = 8
QK_NOPE = 64
QK_ROPE = 32
QK_HEAD = 96
V_HEAD = 64
Q_LORA = 256
KV_LORA = 128
ROPE_THETA = 10000.0
N_EXPERTS = 32
TOP_K = 4
D_FF = 1024
SWIGLU_LIMIT = 7.0
SWIGLU_ALPHA = 1.702
MOE_BLOCK = 512
EPS = 1e-6

LANES = 128
SUBLANES = 8
HEAD_PAD = 128
ROPE_LO = QK_NOPE
ROPE_HALF = QK_ROPE // 2
TOK_PER_ROW = LANES // ROPE_HALF
D_IN_PAD = 2 * D_LRU + Q_LORA + KV_LORA + 2 * LANES
LOG2_E = 1.4426950408889634
FF_CHUNK = 256
OUT_SUB = 1
SUM_LANE = V_HEAD
MAX_LANE = V_HEAD + 1

VMEM_LIMIT = 56 * 1024 * 1024

F32 = jnp.float32
BF16 = jnp.bfloat16


def _cparams(sem):
    return pltpu.CompilerParams(dimension_semantics=sem, vmem_limit_bytes=VMEM_LIMIT)


def _dot(a, b):
    return jnp.dot(a, b, preferred_element_type=F32)


def _dot_nt(a, b):
    return lax.dot_general(a, b, (((1,), (1,)), ((), ())), preferred_element_type=F32)


def _split_bf16(a):
    hi = a.astype(BF16)
    lo = (a - hi.astype(F32)).astype(BF16)
    return hi, lo


def _sigmoid(x):
    return 1.0 / (1.0 + jnp.exp(-x))


def _pack_halves(x):
    bits = lax.bitcast_convert_type(x.astype(BF16).astype(F32), jnp.uint32)
    half = x.shape[1] // 2
    words = (bits[:, :half] >> 16) | (bits[:, half:] & jnp.uint32(0xFFFF0000))
    return lax.bitcast_convert_type(words, jnp.int32)


def _unpack_halves(words):
    w = lax.bitcast_convert_type(words, jnp.uint32)
    lo = lax.bitcast_convert_type(w << 16, F32)
    hi = lax.bitcast_convert_type(w & jnp.uint32(0xFFFF0000), F32)
    return lo, hi


def _ada_kernel(c_ref, w_ref, b_ref, o_ref):
    c = c_ref[...]
    s = c * _sigmoid(c)
    shi, slo = _split_bf16(s)
    whi, wlo = _split_bf16(w_ref[...])
    o_ref[...] = _dot(shi, whi) + _dot(slo, whi) + _dot(shi, wlo) + b_ref[...]


def _ada(c, w_ada, b_ada):
    B, D = c.shape
    N = w_ada.shape[1]
    tn = 1024
    return pl.pallas_call(
        _ada_kernel,
        grid=(N // tn,),
        in_specs=[
            pl.BlockSpec((B, D), lambda j: (0, 0)),
            pl.BlockSpec((D, tn), lambda j: (0, j)),
            pl.BlockSpec((1, tn), lambda j: (0, j)),
        ],
        out_specs=pl.BlockSpec((B, tn), lambda j: (0, j)),
        out_shape=jax.ShapeDtypeStruct((B, N), F32),
        compiler_params=_cparams(("arbitrary",)),
        name="ada",
    )(c, w_ada, b_ada.reshape(1, N))


def _trig_kernel(pos_ref, freq_ref, rsel_ref, fold_ref, cbase_ref, cos_ref, sin_ref):
    ang = pos_ref[...].astype(F32) * freq_ref[...]
    cs = jnp.concatenate([jnp.cos(ang), jnp.sin(ang)], axis=1)
    tm = cos_ref.shape[0]
    row = lax.broadcasted_iota(jnp.int32, (tm, 2 * LANES), 0)
    lane = lax.broadcasted_iota(jnp.int32, (tm, 2 * LANES), 1)
    own = ((lane % LANES) // ROPE_HALF) == (row % TOK_PER_ROW)
    rsel = rsel_ref[...]
    fold = fold_ref[...]
    by_row = sum(_dot(rsel, part) for part in _split_bf16(cs))
    mine = jnp.where(own, by_row, 0.0)
    out = sum(_dot(part, fold) for part in _split_bf16(mine))
    cos_ref[...] = out[:, :LANES] + cbase_ref[...]
    sin_ref[...] = out[:, LANES:]


def _rope_tables(positions):
    T = positions.size
    freqs = ROPE_THETA ** (-jnp.arange(ROPE_HALF, dtype=F32) / ROPE_HALF)
    rows = T // TOK_PER_ROW
    pos_c = jnp.repeat(positions.reshape(T).astype(jnp.int32), ROPE_HALF).reshape(rows, LANES)
    freq_c = jnp.tile(freqs, TOK_PER_ROW).reshape(1, LANES)
    tm = min(2048, T)
    tr = tm // TOK_PER_ROW
    rsel = (jnp.arange(tm)[:, None] // TOK_PER_ROW == jnp.arange(tr)[None, :]).astype(BF16)
    src = jnp.arange(LANES)[:, None] % ROPE_HALF
    dst = jnp.arange(LANES)[None, :]
    first = dst == ROPE_LO + src
    second = dst == ROPE_LO + ROPE_HALF + src
    fcos = (first | second).astype(F32)
    fsin = second.astype(F32) - first.astype(F32)
    zero = jnp.zeros((LANES, LANES), F32)
    fold = jnp.block([[fcos, zero], [zero, fsin]]).astype(BF16)
    lane = jnp.arange(LANES)
    cbase = ((lane < ROPE_LO) | (lane >= ROPE_LO + QK_ROPE)).astype(F32).reshape(1, LANES)
    full = lambda i: (0, 0)
    return pl.pallas_call(
        _trig_kernel,
        grid=(T // tm,),
        in_specs=[
            pl.BlockSpec((tr, LANES), lambda i: (i, 0)),
            pl.BlockSpec((1, LANES), full),
            pl.BlockSpec((tm, tr), full),
            pl.BlockSpec((2 * LANES, 2 * LANES), full),
            pl.BlockSpec((1, LANES), full),
        ],
        out_specs=[pl.BlockSpec((tm, LANES), lambda i: (i, 0))] * 2,
        out_shape=[jax.ShapeDtypeStruct((T, LANES), F32)] * 2,
        compiler_params=_cparams(("arbitrary",)),
        name="rope_trig",
    )(pos_c, freq_c, rsel, fold, cbase)


def _inproj_kernel(x_ref, cos_ref, sin_ref, shift_ref, scale_ref, gmix_ref, win_ref, gq_ref, wuq_ref,
                   gkv_ref, wukv_ref, gqn_ref, gqr_ref, gkn_ref, gkr_ref,
                   xlru_ref, ylru_ref, q_ref, k_ref, v_ref):
    HP = N_HEADS * HEAD_PAD
    x = x_ref[...]
    ms = jnp.mean(x * x, axis=-1, keepdims=True)
    xn = x * lax.rsqrt(ms + EPS) * gmix_ref[...]
    h = xn * (1.0 + scale_ref[0]) + shift_ref[0]
    z = _dot(h.astype(BF16), win_ref[...])
    xlru_ref[...] = z[:, :D_LRU]
    ylru_ref[...] = z[:, D_LRU:2 * D_LRU]
    o1 = 2 * D_LRU
    o2 = o1 + Q_LORA
    o3 = o2 + KV_LORA
    ql = z[:, o1:o2]
    kvl = z[:, o2:o3]
    kr = z[:, o3:o3 + LANES]
    kr_rot = z[:, o3 + LANES:]

    qn = ql * lax.rsqrt(jnp.mean(ql * ql, axis=-1, keepdims=True) + EPS) * gq_ref[...]
    qq = _dot(qn.astype(BF16), wuq_ref[...])
    kvn = kvl * lax.rsqrt(jnp.mean(kvl * kvl, axis=-1, keepdims=True) + EPS) * gkv_ref[...]
    kv = _dot(kvn.astype(BF16), wukv_ref[...])

    tm = x.shape[0]
    lane = lax.broadcasted_iota(jnp.int32, (tm, HP), 1)
    v_ref[...] = jnp.where((lane & (HEAD_PAD - 1)) == V_HEAD, 1.0, kv[:, HP:]).astype(BF16)

    cos_t = cos_ref[...]
    sin_t = sin_ref[...]
    gqn = gqn_ref[...]
    gkn = gkn_ref[...]
    cq = gqn * cos_t
    sq = gqr_ref[...] * sin_t
    kb = kr * (gkn * cos_t) + kr_rot * (gkr_ref[...] * sin_t)
    inv_w = 1.0 / QK_HEAD
    qscale = QK_HEAD ** -0.5 * LOG2_E
    for hh in range(N_HEADS):
        sl = slice(hh * HEAD_PAD, (hh + 1) * HEAD_PAD)
        qh = qq[:, sl]
        rq = lax.rsqrt(jnp.sum(qh * qh, axis=-1, keepdims=True) * inv_w + EPS) * qscale
        q_ref[:, sl] = ((qh * cq + qq[:, HP + hh * HEAD_PAD:HP + (hh + 1) * HEAD_PAD] * sq) * rq).astype(BF16)
        kraw = kv[:, sl] + kr
        rk = lax.rsqrt(jnp.sum(kraw * kraw, axis=-1, keepdims=True) * inv_w + EPS)
        k_ref[:, sl] = ((kv[:, sl] * gkn + kb) * rk).astype(BF16)


def _inproj(x2, cos_t, sin_t, mod3, g_mix, w_in_p, g_q_lat, w_uq_p, g_kv_lat, w_ukv_p,
            gqn_p, gqr_p, gkn_p, gkr_p, B, S, tm):
    T, D = x2.shape
    ns = S // tm
    HP = N_HEADS * HEAD_PAD
    row = lambda b, s: (b * ns + s, 0)
    full = lambda b, s: (0, 0)
    return pl.pallas_call(
        _inproj_kernel,
        grid=(B, ns),
        in_specs=[
            pl.BlockSpec((tm, D), row),
            pl.BlockSpec((tm, LANES), row),
            pl.BlockSpec((tm, LANES), row),
            pl.BlockSpec((1, 1, D), lambda b, s: (b * 6 + 0, 0, 0)),
            pl.BlockSpec((1, 1, D), lambda b, s: (b * 6 + 1, 0, 0)),
            pl.BlockSpec((1, D), full),
            pl.BlockSpec((D, D_IN_PAD), full),
            pl.BlockSpec((1, Q_LORA), full),
            pl.BlockSpec((Q_LORA, 2 * HP), full),
            pl.BlockSpec((1, KV_LORA), full),
            pl.BlockSpec((KV_LORA, 2 * HP), full),
            pl.BlockSpec((1, HEAD_PAD), full),
            pl.BlockSpec((1, HEAD_PAD), full),
            pl.BlockSpec((1, HEAD_PAD), full),
            pl.BlockSpec((1, HEAD_PAD), full),
        ],
        out_specs=[
            pl.BlockSpec((tm, D_LRU), row),
            pl.BlockSpec((tm, D_LRU), row),
            pl.BlockSpec((tm, HP), row),
            pl.BlockSpec((tm, HP), row),
            pl.BlockSpec((tm, HP), row),
        ],
        out_shape=[
            jax.ShapeDtypeStruct((T, D_LRU), F32),
            jax.ShapeDtypeStruct((T, D_LRU), F32),
            jax.ShapeDtypeStruct((T, HP), BF16),
            jax.ShapeDtypeStruct((T, HP), BF16),
            jax.ShapeDtypeStruct((T, HP), BF16),
        ],
        compiler_params=_cparams(("arbitrary", "arbitrary")),
        name="inproj",
    )(x2, cos_t, sin_t, mod3, mod3, g_mix, w_in_p, g_q_lat, w_uq_p, g_kv_lat, w_ukv_p,
      gqn_p, gqr_p, gkn_p, gkr_p)


def _gelu_tanh(x):
    return 0.5 * x * (1.0 + jnp.tanh(0.7978845608028654 * (x + 0.044715 * x * x * x)))


def _lru_kernel(x_ref, y_ref, cw_ref, cb_ref, wa_ref, ba_ref, wx_ref, bx_ref, lam_ref,
                o_ref, tail_ref, carry_ref):
    s = pl.program_id(1)

    @pl.when(s == 0)
    def _():
        tail_ref[...] = jnp.zeros_like(tail_ref)
        carry_ref[...] = jnp.zeros_like(carry_ref)

    x = x_ref[...]
    ts = x.shape[0]
    xext = jnp.concatenate([tail_ref[...], x], axis=0)
    cw = cw_ref[...]
    xc = x * cw[CONV_W - 1:CONV_W, :]
    for j in range(CONV_W - 1):
        sh = CONV_W - 1 - j
        xc = xc + xext[8 - sh:8 - sh + ts, :] * cw[j:j + 1, :]
    xc = xc + cb_ref[...]
    tail_ref[...] = x[ts - 8:, :]

    xb = xc.astype(BF16)
    r = _sigmoid(_dot(xb, wa_ref[...]) + ba_ref[...])
    i = _sigmoid(_dot(xb, wx_ref[...]) + bx_ref[...])
    lam = lam_ref[...]
    nl = -lam
    softplus = jnp.maximum(nl, 0.0) + jnp.log(1.0 + jnp.exp(-jnp.abs(nl)))
    log_a = (-LRU_C) * r * softplus
    a = jnp.exp(log_a)
    mult = jnp.sqrt(1.0 - jnp.exp(2.0 * log_a))
    u = mult * (i * xc)

    sub = lax.broadcasted_iota(jnp.int32, (ts, 1), 0) & (SUBLANES - 1)
    sh = 1
    while sh < SUBLANES:
        a_prev = pltpu.roll(a, sh, axis=0)
        u_prev = pltpu.roll(u, sh, axis=0)
        m = sub >= sh
        u = jnp.where(m, a * u_prev + u, u)
        a = jnp.where(m, a * a_prev, a)
        sh *= 2
    h = carry_ref[0:1, :]
    groups = []
    for g0 in range(0, ts, SUBLANES):
        hg = u[g0:g0 + SUBLANES, :] + a[g0:g0 + SUBLANES, :] * h
        groups.append(hg)
        h = hg[SUBLANES - 1:SUBLANES, :]
    carry_ref[...] = jnp.broadcast_to(h, carry_ref.shape)
    hs = jnp.concatenate(groups, axis=0)
    o_ref[...] = (_gelu_tanh(y_ref[...]) * hs).astype(BF16)


def _lru(xlru, ylru, conv_w, conv_b, wa_d, b_a, wx_d, b_x, lam, B, S, ts):
    T, C = xlru.shape
    ns = S // ts
    row = lambda b, s: (b * ns + s, 0)
    full = lambda b, s: (0, 0)
    return pl.pallas_call(
        _lru_kernel,
        grid=(B, ns),
        in_specs=[
            pl.BlockSpec((ts, C), row),
            pl.BlockSpec((ts, C), row),
            pl.BlockSpec((CONV_W, C), full),
            pl.BlockSpec((1, C), full),
            pl.BlockSpec((C, C), full),
            pl.BlockSpec((1, C), full),
            pl.BlockSpec((C, C), full),
            pl.BlockSpec((1, C), full),
            pl.BlockSpec((1, C), full),
        ],
        out_specs=pl.BlockSpec((ts, C), row),
        out_shape=jax.ShapeDtypeStruct((T, C), BF16),
        scratch_shapes=[pltpu.VMEM((8, C), F32), pltpu.VMEM((8, C), F32)],
        compiler_params=_cparams(("arbitrary", "arbitrary")),
        name="lru",
    )(xlru, ylru, conv_w, conv_b, wa_d, b_a, wx_d, b_x, lam)


NEG_INF = -1e30


def _attn_kernel(q_ref, k_ref, v_ref, o_ref, *state, tq):
    m_refs = state[:N_HEADS]
    acc_refs = state[N_HEADS:]
    qi = pl.program_id(1)
    rowi = lax.broadcasted_iota(jnp.int32, (tq, tq), 0)
    coli = lax.broadcasted_iota(jnp.int32, (tq, tq), 1)
    diag_mask = coli <= rowi

    def head_slice(hh):
        return slice(hh * HEAD_PAD, (hh + 1) * HEAD_PAD)

    def weights(sc, m_b):
        cols = [jnp.exp2(sc[:, c0:c0 + LANES] - m_b) for c0 in range(0, tq, LANES)]
        return jnp.concatenate(cols, axis=1).astype(BF16)

    def scores(hh, r0):
        hs = head_slice(hh)
        return _dot_nt(q_ref[:, hs], k_ref[pl.ds(r0, tq), hs])

    r_diag = pl.multiple_of(qi * tq, tq)
    sc_next = scores(0, r_diag)
    for hh in range(N_HEADS):
        hs = head_slice(hh)
        sc = jnp.where(diag_mask, sc_next, NEG_INF)
        if hh + 1 < N_HEADS:
            sc_next = scores(hh + 1, r_diag)
        m_b = jnp.broadcast_to(jnp.max(sc, axis=-1, keepdims=True), (tq, LANES))
        m_refs[hh][...] = m_b
        acc_refs[hh][...] = _dot(weights(sc, m_b), v_ref[pl.ds(r_diag, tq), hs])

    @pl.loop(0, qi)
    def _(j):
        r0 = pl.multiple_of(j * tq, tq)
        sc_next = scores(0, r0)
        for hh in range(N_HEADS):
            hs = head_slice(hh)
            sc = sc_next
            if hh + 1 < N_HEADS:
                sc_next = scores(hh + 1, r0)
            m_b = m_refs[hh][...]
            m_new = jnp.maximum(m_b, jnp.max(sc, axis=-1, keepdims=True))
            alpha = jnp.exp2(m_b - m_new)
            m_refs[hh][...] = m_new
            acc_refs[hh][...] = (alpha * acc_refs[hh][...]
                                 + _dot(weights(sc, m_new), v_ref[pl.ds(r0, tq), hs]))

    for hh in range(N_HEADS):
        acc = acc_refs[hh][...]
        o = acc[:, :V_HEAD] / acc[:, SUM_LANE:SUM_LANE + 1]
        o_ref[:, hh * V_HEAD:(hh + 1) * V_HEAD] = o.astype(BF16)


def _attn(qp, kp, v, B, S, tq):
    T = qp.shape[0]
    nq = S // tq
    HP = N_HEADS * HEAD_PAD
    HV = N_HEADS * V_HEAD
    return pl.pallas_call(
        functools.partial(_attn_kernel, tq=tq),
        grid=(B, nq),
        in_specs=[
            pl.BlockSpec((tq, HP), lambda b, i: (b * nq + i, 0)),
            pl.BlockSpec((S, HP), lambda b, i: (b, 0)),
            pl.BlockSpec((S, HP), lambda b, i: (b, 0)),
        ],
        out_specs=pl.BlockSpec((tq, HV), lambda b, i: (b * nq + i, 0)),
        out_shape=jax.ShapeDtypeStruct((T, HV), BF16),
        scratch_shapes=([pltpu.VMEM((tq, LANES), F32)] * N_HEADS
                        + [pltpu.VMEM((tq, HEAD_PAD), F32)] * N_HEADS),
        compiler_params=_cparams(("arbitrary", "arbitrary")),
        name="attn",
    )(qp, kp, v)


def _outproj_kernel(lru_ref, att_ref, x_ref, gate_ref, shift_ref, scale_ref, gffn_ref,
                    wo1_ref, wo2_ref, wrh_ref, wrl_ref, br_ref, tri_ref,
                    x1_ref, h2p_ref, idx_ref, gat_ref, rank_ref, cnt_ref, run_ref):
    first = (pl.program_id(0) == 0) & (pl.program_id(1) == 0)

    @pl.when(first)
    def _():
        run_ref[...] = jnp.zeros_like(run_ref)

    tm = x_ref.shape[0]
    hm = tm // OUT_SUB
    ne = wrh_ref.shape[0]

    def project(r):
        rows = slice(r * hm, (r + 1) * hm)
        return _dot(lru_ref[rows, :], wo1_ref[...]) + _dot(att_ref[rows, :], wo2_ref[...])

    def route(r, mix):
        rows = slice(r * hm, (r + 1) * hm)
        x1 = x_ref[rows, :] + gate_ref[0] * mix
        x1_ref[rows, :] = x1
        ms = jnp.mean(x1 * x1, axis=-1, keepdims=True)
        h2 = x1 * lax.rsqrt(ms + EPS) * gffn_ref[...]
        h2 = h2 * (1.0 + scale_ref[0]) + shift_ref[0]

        hhi = h2.astype(BF16)
        hlo = (h2 - hhi.astype(F32)).astype(BF16)
        h2p_ref[rows, :] = _pack_halves(h2)

        wrh = wrh_ref[...]
        logits = _dot_nt(wrh, hhi) + _dot_nt(wrh, hlo) + _dot_nt(wrl_ref[...], hhi) + br_ref[...]
        eio = lax.broadcasted_iota(jnp.int32, (ne, hm), 0)
        vals, idxs, sels = [], [], []
        l = logits
        for _ in range(TOP_K):
            m = jnp.max(l, axis=0, keepdims=True)
            idx = jnp.min(jnp.where(l == m, eio, ne), axis=0, keepdims=True)
            sel = eio == idx
            l = jnp.where(sel, -jnp.inf, l)
            vals.append(m)
            idxs.append(idx)
            sels.append(sel)
        es = [jnp.exp(v - vals[0]) for v in vals]
        inv = 1.0 / (es[0] + es[1] + es[2] + es[3])
        sel_any = jnp.where(sels[0] | sels[1] | sels[2] | sels[3], 1.0, 0.0)
        run = run_ref[...]
        excl = _dot(sel_any.astype(BF16), tri_ref[...]) + run
        cols = slice(r * hm, (r + 1) * hm)
        for kk in range(TOP_K):
            idx_ref[kk:kk + 1, cols] = idxs[kk]
            gat_ref[kk:kk + 1, cols] = es[kk] * inv
            rk = jnp.sum(jnp.where(sels[kk], excl, 0.0), axis=0, keepdims=True)
            rank_ref[kk:kk + 1, cols] = rk.astype(jnp.int32)
        run_ref[...] = run + jnp.sum(sel_any, axis=1, keepdims=True)

    mix_next = project(0)
    for r in range(OUT_SUB):
        mix = mix_next
        if r + 1 < OUT_SUB:
            mix_next = project(r + 1)
        route(r, mix)
    cnt_ref[...] = run_ref[...].astype(jnp.int32)


def _outproj(lru_o, att_o, x2, mod3, g_ffn, wo1, wo2, wr_hi, wr_lo, b_r, tri, b0, B, S, tm):
    D = x2.shape[1]
    T = B * S
    ns = S // tm
    C = lru_o.shape[1]
    row_in = lambda b, s: ((b0 + b) * ns + s, 0)
    row = lambda b, s: (b * ns + s, 0)
    col = lambda b, s: (0, b * ns + s)
    full = lambda b, s: (0, 0)
    return pl.pallas_call(
        _outproj_kernel,
        grid=(B, ns),
        in_specs=[
            pl.BlockSpec((tm, C), row_in),
            pl.BlockSpec((tm, C), row_in),
            pl.BlockSpec((tm, D), row_in),
            pl.BlockSpec((1, 1, D), lambda b, s: ((b0 + b) * 6 + 2, 0, 0)),
            pl.BlockSpec((1, 1, D), lambda b, s: ((b0 + b) * 6 + 3, 0, 0)),
            pl.BlockSpec((1, 1, D), lambda b, s: ((b0 + b) * 6 + 4, 0, 0)),
            pl.BlockSpec((1, D), full),
            pl.BlockSpec((C, D), full),
            pl.BlockSpec((C, D), full),
            pl.BlockSpec((N_EXPERTS, D), full),
            pl.BlockSpec((N_EXPERTS, D), full),
            pl.BlockSpec((N_EXPERTS, 1), full),
            pl.BlockSpec(tri.shape, full),
        ],
        out_specs=[
            pl.BlockSpec((tm, D), row),
            pl.BlockSpec((tm, D // 2), row),
            pl.BlockSpec((TOP_K, tm), col),
            pl.BlockSpec((TOP_K, tm), col),
            pl.BlockSpec((TOP_K, tm), col),
            pl.BlockSpec((N_EXPERTS, 1), full),
        ],
        out_shape=[
            jax.ShapeDtypeStruct((T, D), F32),
            jax.ShapeDtypeStruct((T, D // 2), jnp.int32),
            jax.ShapeDtypeStruct((TOP_K, T), jnp.int32),
            jax.ShapeDtypeStruct((TOP_K, T), F32),
            jax.ShapeDtypeStruct((TOP_K, T), jnp.int32),
            jax.ShapeDtypeStruct((N_EXPERTS, 1), jnp.int32),
        ],
        scratch_shapes=[pltpu.VMEM((N_EXPERTS, 1), F32)],
        compiler_params=_cparams(("arbitrary", "arbitrary")),
        name="outproj",
    )(lru_o, att_o, x2, mod3, mod3, mod3, g_ffn, wo1, wo2, wr_hi, wr_lo, b_r, tri)


SC_CORES = 2
SC_SUBCORES = 16
SC_WORKERS = SC_CORES * SC_SUBCORES


def _sc_mesh():
    return plsc.VectorSubcoreMesh(core_axis_name="c", subcore_axis_name="s",
                                  num_cores=SC_CORES, num_subcores=SC_SUBCORES)


def _sc_worker_id():
    return lax.axis_index("s") * SC_CORES + lax.axis_index("c")


def _sc_scatter_rows(rows, idx, n_out, g):
    T, W = rows.shape
    K = idx.shape[0]
    per_w = T // SC_WORKERS
    nch = per_w // g
    assert per_w * SC_WORKERS == T and nch * g == per_w and nch % 2 == 0
    idx_w = idx.reshape(K, SC_WORKERS, nch, g).transpose(1, 2, 0, 3).reshape(SC_WORKERS, nch * K, g)

    def body(rows_hbm, idx_hbm, out_hbm, idx_v, buf0, buf1, semr0, semr1, semw):
        wid = _sc_worker_id()
        base = wid * per_w
        pltpu.sync_copy(idx_hbm.at[wid], idx_v)

        def read(j, buf, sem):
            return pltpu.make_async_copy(rows_hbm.at[pl.ds(base + j * g, g)], buf, sem)

        def scatter(j, buf):
            copies = [pltpu.async_copy(buf, out_hbm.at[idx_v.at[j * K + kk]], semw)
                      for kk in range(K)]
            for cp in copies:
                cp.wait()

        read(0, buf0, semr0).start()

        @pl.loop(0, nch // 2)
        def _(jj):
            j0 = 2 * jj
            read(j0 + 1, buf1, semr1).start()
            read(j0, buf0, semr0).wait()
            scatter(j0, buf0)

            @pl.when(j0 + 2 < nch)
            def _():
                read(j0 + 2, buf0, semr0).start()

            read(j0 + 1, buf1, semr1).wait()
            scatter(j0 + 1, buf1)

    return pl.kernel(
        body,
        out_type=jax.ShapeDtypeStruct((n_out, W), rows.dtype),
        mesh=_sc_mesh(),
        scratch_types=[
            pltpu.VMEM((nch * K, g), jnp.int32),
            pltpu.VMEM((g, W), rows.dtype),
            pltpu.VMEM((g, W), rows.dtype),
            pltpu.SemaphoreType.DMA,
            pltpu.SemaphoreType.DMA,
            pltpu.SemaphoreType.DMA,
        ],
        name="sc_scatter_rows",
    )(rows, idx_w)


def _sc_gather_rows(table, idx, g):
    W = table.shape[1]
    N = idx.shape[0]
    per_w = N // SC_WORKERS
    nch = per_w // g
    assert per_w * SC_WORKERS == N and nch * g == per_w and nch % 2 == 0
    idx_w = idx.reshape(SC_WORKERS, nch, g)

    def body(table_hbm, idx_hbm, out_hbm, idx_v, buf0, buf1, sem0, sem1):
        wid = _sc_worker_id()
        base = wid * per_w
        pltpu.sync_copy(idx_hbm.at[wid], idx_v)

        def gather(j, buf, sem):
            return pltpu.make_async_copy(table_hbm.at[idx_v.at[j]], buf, sem)

        def put(j, buf):
            pltpu.sync_copy(buf, out_hbm.at[pl.ds(base + j * g, g)])

        gather(0, buf0, sem0).start()

        @pl.loop(0, nch // 2)
        def _(jj):
            j0 = 2 * jj
            gather(j0 + 1, buf1, sem1).start()
            gather(j0, buf0, sem0).wait()
            put(j0, buf0)

            @pl.when(j0 + 2 < nch)
            def _():
                gather(j0 + 2, buf0, sem0).start()

            gather(j0 + 1, buf1, sem1).wait()
            put(j0 + 1, buf1)

    return pl.kernel(
        body,
        out_type=jax.ShapeDtypeStruct((N, W), table.dtype),
        mesh=_sc_mesh(),
        scratch_types=[
            pltpu.VMEM((nch, g), jnp.int32),
            pltpu.VMEM((g, W), table.dtype),
            pltpu.VMEM((g, W), table.dtype),
            pltpu.SemaphoreType.DMA,
            pltpu.SemaphoreType.DMA,
        ],
        name="sc_gather_rows",
    )(table, idx_w)


def _experts_kernel(be_ref, bv_ref, bf_ref, xs_ref, w1_ref, b1_ref, w2_ref, b2_ref, ys_ref,
                    w1b_ref, w2b_ref):
    i = pl.program_id(0)
    nvalid = bv_ref[i]

    @pl.when(bf_ref[i] > 0)
    def _():
        w1b_ref[...] = w1_ref[0].astype(BF16)
        w2b_ref[...] = w2_ref[0].astype(BF16)

    @pl.when(nvalid > 0)
    def _():
        xw = xs_ref[...]
        rowi = lax.broadcasted_iota(jnp.int32, (xw.shape[0], 1), 0)
        lo, hi = _unpack_halves(jnp.where(rowi < nvalid, xw, 0))
        half = xw.shape[1]
        lo = lo.astype(BF16)
        hi = hi.astype(BF16)

        def gate_up(c0):
            gs = slice(c0, c0 + FF_CHUNK)
            us = slice(D_FF + c0, D_FF + c0 + FF_CHUNK)
            g = _dot(lo, w1b_ref[:half, gs]) + _dot(hi, w1b_ref[half:, gs]) + b1_ref[0, :, gs]
            u = _dot(lo, w1b_ref[:half, us]) + _dot(hi, w1b_ref[half:, us]) + b1_ref[0, :, us]
            return g, u

        y = b2_ref[0]
        nxt = gate_up(0)
        for c0 in range(0, D_FF, FF_CHUNK):
            g, u = nxt
            if c0 + FF_CHUNK < D_FF:
                nxt = gate_up(c0 + FF_CHUNK)
            glu = jnp.minimum(g, SWIGLU_LIMIT)
            lin = jnp.clip(u, -SWIGLU_LIMIT, SWIGLU_LIMIT)
            act = (lin + 1.0) * (glu * _sigmoid(SWIGLU_ALPHA * glu))
            y = y + _dot(act.astype(BF16), w2b_ref[c0:c0 + FF_CHUNK, :])
        ys_ref[...] = _pack_halves(y)


def _experts(blk_e, blk_v, blk_f, blk_r, xs, w1, b1, w2, b2):
    P, W = xs.shape
    nb = P // MOE_BLOCK
    E, D, F2 = w1.shape
    grid_spec = pltpu.PrefetchScalarGridSpec(
        num_scalar_prefetch=4,
        grid=(nb,),
        in_specs=[
            pl.BlockSpec((MOE_BLOCK, W), lambda i, be, bv, bf, br: (br[i], 0)),
            pl.BlockSpec((1, D, F2), lambda i, be, bv, bf, br: (be[i], 0, 0)),
            pl.BlockSpec((1, 1, F2), lambda i, be, bv, bf, br: (be[i], 0, 0)),
            pl.BlockSpec((1, D_FF, D), lambda i, be, bv, bf, br: (be[i], 0, 0)),
            pl.BlockSpec((1, 1, D), lambda i, be, bv, bf, br: (be[i], 0, 0)),
        ],
        out_specs=pl.BlockSpec((MOE_BLOCK, D // 2), lambda i, be, bv, bf, br: (br[i], 0)),
        scratch_shapes=[pltpu.VMEM((D, F2), BF16), pltpu.VMEM((D_FF, D), BF16)],
    )

    def kern(be_ref, bv_ref, bf_ref, br_ref, *refs):
        del br_ref
        _experts_kernel(be_ref, bv_ref, bf_ref, *refs)

    return pl.pallas_call(
        kern,
        grid_spec=grid_spec,
        out_shape=jax.ShapeDtypeStruct((P, D // 2), jnp.int32),
        compiler_params=_cparams(("arbitrary",)),
        name="experts",
    )(blk_e, blk_v, blk_f, blk_r, xs, w1, b1.reshape(E, 1, F2), w2, b2.reshape(E, 1, D))


def _combine_kernel(x1_ref, g_ref, gate_ref, y0_ref, y1_ref, y2_ref, y3_ref, *rest):
    o_ref = rest[-1]
    g = g_ref[...]
    half = y0_ref.shape[2]
    acc_lo = acc_hi = None
    for kk, y_ref in enumerate((y0_ref, y1_ref, y2_ref, y3_ref)):
        lo, hi = _unpack_halves(y_ref[0])
        gk = g[:, kk:kk + 1]
        acc_lo = lo * gk if acc_lo is None else acc_lo + lo * gk
        acc_hi = hi * gk if acc_hi is None else acc_hi + hi * gk
    gate = gate_ref[0]
    o_ref[:, :half] = x1_ref[:, :half] + gate[:, :half] * acc_lo
    o_ref[:, half:] = x1_ref[:, half:] + gate[:, half:] * acc_hi


def _combine(x1, gates_tk, mod3, ysg, out_prev, b0, B, nb_total, S, tm):
    D = x1.shape[1]
    ns = S // tm
    row = lambda b, s: (b * ns + s, 0)
    row_out = lambda b, s: ((b0 + b) * ns + s, 0)

    def yspec(kk):
        return pl.BlockSpec((1, tm, D // 2), lambda b, s: (kk, b * ns + s, 0))

    in_specs = [
        pl.BlockSpec((tm, D), row),
        pl.BlockSpec((tm, TOP_K), row),
        pl.BlockSpec((1, 1, D), lambda b, s: ((b0 + b) * 6 + 5, 0, 0)),
        yspec(0), yspec(1), yspec(2), yspec(3),
    ]
    args = [x1, gates_tk, mod3, ysg, ysg, ysg, ysg]
    aliases = {}
    if out_prev is not None:
        in_specs.append(pl.BlockSpec(memory_space=pl.ANY))
        args.append(out_prev)
        aliases = {len(args) - 1: 0}
    return pl.pallas_call(
        _combine_kernel,
        grid=(B, ns),
        in_specs=in_specs,
        out_specs=pl.BlockSpec((tm, D), row_out),
        out_shape=jax.ShapeDtypeStruct((nb_total * S, D), F32),
        input_output_aliases=aliases,
        compiler_params=_cparams(("arbitrary", "arbitrary")),
        name="combine",
    )(*args)


def _block_diag(w):
    n, c, d = w.shape
    eye = jnp.eye(n, dtype=w.dtype)
    return jnp.einsum("ncd,nm->ncmd", w, eye).reshape(n * c, n * d)


def _pad_heads(w, width):
    k = w.shape[0]
    w = w.reshape(k, N_HEADS, width)
    return jnp.pad(w, ((0, 0), (0, 0), (0, HEAD_PAD - width))).reshape(k, N_HEADS * HEAD_PAD)


def kernel(x, c, positions, w_ada, b_ada, g_mix, w_in, conv_w, conv_b, w_a, b_a, w_x, b_x, lam,
           g_q_lat, w_uq, g_kv_lat, w_ukv, g_qn, g_kn, w_out, g_ffn, w_router, b_router,
           w1, b1, w2, b2):
    B, S, D = x.shape
    T = B * S
    depth = w_ada.shape[0]
    tm_in = min(512, S)
    ts_lru = min(512, S)
    tq = min(512, S)
    tm_out = min(512, S)
    tm_comb = min(512, S)
    n_groups = 2 if B % 2 == 0 else 1
    Bg = B // n_groups
    Tg = Bg * S
    g_disp = min(64, Tg // SC_WORKERS // 2)
    g_comb = min(64, Tg * TOP_K // SC_WORKERS // 2)

    o1 = 2 * D_LRU
    o2 = o1 + Q_LORA
    o3 = o2 + KV_LORA
    n_tri = tm_out // OUT_SUB
    tri = (jnp.arange(n_tri)[:, None] < jnp.arange(n_tri)[None, :]).astype(BF16)
    cos_t, sin_t = _rope_tables(positions)
    lane = jnp.arange(HEAD_PAD, dtype=jnp.int32)
    first = (lane >= ROPE_LO) & (lane < ROPE_LO + ROPE_HALF)
    second = (lane >= ROPE_LO + ROPE_HALF) & (lane < ROPE_LO + QK_ROPE)
    partner = jnp.where(first, lane + ROPE_HALF, jnp.where(second, lane - ROPE_HALF, 0))
    is_rot = first | second

    def rot_cols(w):
        k = w.shape[0]
        w3 = w.reshape(k, -1, HEAD_PAD)
        return jnp.where(is_rot[None, None, :], w3[:, :, partner], 0.0).reshape(w.shape)

    x2 = x.reshape(T, D)
    for l in range(depth):
        mod3 = _ada(c, w_ada[l], b_ada[l]).reshape(B * 6, 1, D)

        w_in_l = w_in[l]
        kr_cols = jnp.pad(w_in_l[:, o3:], ((0, 0), (ROPE_LO, LANES - ROPE_LO - QK_ROPE)))
        w_in_p = jnp.concatenate([w_in_l[:, :o3], kr_cols, rot_cols(kr_cols)], axis=1).astype(BF16)
        w_uq_h = _pad_heads(w_uq[l], QK_HEAD)
        w_uq_p = jnp.concatenate([w_uq_h, rot_cols(w_uq_h)], axis=1).astype(BF16)
        w_ukv_l = w_ukv[l].reshape(KV_LORA, N_HEADS, QK_NOPE + V_HEAD)
        w_uk_h = _pad_heads(w_ukv_l[:, :, :QK_NOPE].reshape(KV_LORA, N_HEADS * QK_NOPE), QK_NOPE)
        w_uv_h = _pad_heads(w_ukv_l[:, :, QK_NOPE:].reshape(KV_LORA, N_HEADS * V_HEAD), V_HEAD)
        w_ukv_p = jnp.concatenate([w_uk_h, w_uv_h], axis=1).astype(BF16)
        gqn_p = jnp.pad(g_qn[l], (0, HEAD_PAD - QK_HEAD)).reshape(1, HEAD_PAD)
        gkn_p = jnp.pad(g_kn[l], (0, HEAD_PAD - QK_HEAD)).reshape(1, HEAD_PAD)

        xlru, ylru, qp, kp, v = _inproj(
            x2, cos_t, sin_t, mod3, g_mix[l].reshape(1, D), w_in_p, g_q_lat[l].reshape(1, Q_LORA),
            w_uq_p, g_kv_lat[l].reshape(1, KV_LORA), w_ukv_p, gqn_p, rot_cols(gqn_p), gkn_p,
            rot_cols(gkn_p), B, S, tm_in)

        lru_o = _lru(xlru, ylru, conv_w[l], conv_b[l].reshape(1, D_LRU),
                     _block_diag(w_a[l]).astype(BF16), b_a[l].reshape(1, D_LRU),
                     _block_diag(w_x[l]).astype(BF16), b_x[l].reshape(1, D_LRU),
                     lam[l].reshape(1, D_LRU), B, S, ts_lru)

        att_o = _attn(qp, kp, v, B, S, tq)

        w_out_b = w_out[l].astype(BF16)
        wr_hi, wr_lo = _split_bf16(w_router[l].T)
        g_ffn_l = g_ffn[l].reshape(1, D)
        b_r = b_router[l].reshape(N_EXPERTS, 1)
        eio = jnp.arange(N_EXPERTS, dtype=jnp.int32)
        n_blocks = -(-(Tg * TOP_K) // MOE_BLOCK) + N_EXPERTS
        bi = jnp.arange(n_blocks, dtype=jnp.int32)

        x_next = None
        for gi in range(n_groups):
            b0 = gi * Bg
            x1, h2p, idx_t, gat_t, rank_t, counts = _outproj(
                lru_o, att_o, x2, mod3, g_ffn_l, w_out_b[:D_LRU], w_out_b[D_LRU:],
                wr_hi, wr_lo, b_r, tri, b0, Bg, S, tm_out)

            counts = counts.reshape(N_EXPERTS)
            nblk_e = (counts + MOE_BLOCK - 1) // MOE_BLOCK
            blk_end = jnp.cumsum(nblk_e)
            pad_start = (blk_end - nblk_e) * MOE_BLOCK
            total = blk_end[-1]
            blk_r = jnp.minimum(bi, total - 1).astype(jnp.int32)
            blk_e = jnp.minimum(jnp.sum(blk_end[None, :] <= blk_r[:, None], axis=1),
                                N_EXPERTS - 1).astype(jnp.int32)
            blk_onehot = blk_e[:, None] == eio[None, :]
            blk_first = jnp.sum(jnp.where(blk_onehot, (blk_end - nblk_e)[None, :], 0), axis=1)
            blk_cnt = jnp.sum(jnp.where(blk_onehot, counts[None, :], 0), axis=1)
            blk_n = jnp.where(bi < total,
                              jnp.clip(blk_cnt - (bi - blk_first) * MOE_BLOCK, 0, MOE_BLOCK),
                              0).astype(jnp.int32)
            slot0 = jnp.sum(jnp.where(idx_t[None] == eio[:, None, None],
                                      pad_start[:, None, None], 0), axis=0)
            dest = slot0.astype(jnp.int32) + rank_t

            xs = _sc_scatter_rows(h2p, dest, n_blocks * MOE_BLOCK, g_disp)
            blk_f = ((bi == blk_first) & (bi < total)).astype(jnp.int32)
            ys = _experts(blk_e, blk_n, blk_f, blk_r, xs, w1[l], b1[l], w2[l], b2[l])
            ysg = _sc_gather_rows(ys, dest.reshape(TOP_K * Tg), g_comb).reshape(TOP_K, Tg, D // 2)
            x_next = _combine(x1, gat_t.T, mod3, ysg, x_next, b0, Bg, B, S, tm_comb)
        x2 = x_next
    return x2.reshape(B, S, D)
```

```python
import functools

import jax
import jax.numpy as jnp
from jax import lax
from jax.experimental import pallas as pl
from jax.experimental.pallas import tpu as pltpu
from jax.experimental.pallas import tpu_sc as plsc

D_MODEL = 1024
D_LRU = 512
LRU_BLOCKS = 8
LRU_BD = 64
CONV_W = 4
LRU_C = 8.0
N_HEADS = 8
QK_NOPE = 64
QK_ROPE = 32
QK_HEAD = 96
V_HEAD = 64
Q_LORA = 256
KV_LORA = 128
ROPE_THETA = 10000.0
N_EXPERTS = 32
TOP_K = 4
D_FF = 1024
SWIGLU_LIMIT = 7.0
SWIGLU_ALPHA = 1.702
MOE_BLOCK = 512
EPS = 1e-6

LANES = 128
SUBLANES = 8
HEAD_PAD = 128
ROPE_LO = QK_NOPE
ROPE_HALF = QK_ROPE // 2
TOK_PER_ROW = LANES // ROPE_HALF
D_IN_PAD = 2 * D_LRU + Q_LORA + KV_LORA + 2 * LANES
LOG2_E = 1.4426950408889634
SUM_LANE = V_HEAD
MAX_LANE = V_HEAD + 1

VMEM_LIMIT = 56 * 1024 * 1024

F32 = jnp.float32
BF16 = jnp.bfloat16


def _cparams(sem):
    return pltpu.CompilerParams(dimension_semantics=sem, vmem_limit_bytes=VMEM_LIMIT)


def _dot(a, b):
    return jnp.dot(a, b, preferred_element_type=F32)


def _dot_nt(a, b):
    return lax.dot_general(a, b, (((1,), (1,)), ((), ())), preferred_element_type=F32)


def _split_bf16(a):
    hi = a.astype(BF16)
    lo = (a - hi.astype(F32)).astype(BF16)
    return hi, lo


def _sigmoid(x):
    return 1.0 / (1.0 + jnp.exp(-x))


def _pack_halves(x):
    bits = lax.bitcast_convert_type(x.astype(BF16).astype(F32), jnp.uint32)
    half = x.shape[1] // 2
    words = (bits[:, :half] >> 16) | (bits[:, half:] & jnp.uint32(0xFFFF0000))
    return lax.bitcast_convert_type(words, jnp.int32)


def _unpack_halves(words):
    w = lax.bitcast_convert_type(words, jnp.uint32)
    lo = lax.bitcast_convert_type(w << 16, F32)
    hi = lax.bitcast_convert_type(w & jnp.uint32(0xFFFF0000), F32)
    return lo, hi


def _ada_kernel(c_ref, w_ref, b_ref, o_ref):
    c = c_ref[...]
    s = c * _sigmoid(c)
    shi, slo = _split_bf16(s)
    whi, wlo = _split_bf16(w_ref[...])
    o_ref[...] = _dot(shi, whi) + _dot(slo, whi) + _dot(shi, wlo) + b_ref[...]


def _ada(c, w_ada, b_ada):
    B, D = c.shape
    N = w_ada.shape[1]
    tn = 1024
    return pl.pallas_call(
        _ada_kernel,
        grid=(N // tn,),
        in_specs=[
            pl.BlockSpec((B, D), lambda j: (0, 0)),
            pl.BlockSpec((D, tn), lambda j: (0, j)),
            pl.BlockSpec((1, tn), lambda j: (0, j)),
        ],
        out_specs=pl.BlockSpec((B, tn), lambda j: (0, j)),
        out_shape=jax.ShapeDtypeStruct((B, N), F32),
        compiler_params=_cparams(("arbitrary",)),
        name="ada",
    )(c, w_ada, b_ada.reshape(1, N))


def _trig_kernel(pos_ref, freq_ref, rsel_ref, fold_ref, cbase_ref, cos_ref, sin_ref):
    ang = pos_ref[...].astype(F32) * freq_ref[...]
    cs = jnp.concatenate([jnp.cos(ang), jnp.sin(ang)], axis=1)
    tm = cos_ref.shape[0]
    row = lax.broadcasted_iota(jnp.int32, (tm, 2 * LANES), 0)
    lane = lax.broadcasted_iota(jnp.int32, (tm, 2 * LANES), 1)
    own = ((lane % LANES) // ROPE_HALF) == (row % TOK_PER_ROW)
    rsel = rsel_ref[...]
    fold = fold_ref[...]
    by_row = sum(_dot(rsel, part) for part in _split_bf16(cs))
    mine = jnp.where(own, by_row, 0.0)
    out = sum(_dot(part, fold) for part in _split_bf16(mine))
    cos_ref[...] = out[:, :LANES] + cbase_ref[...]
    sin_ref[...] = out[:, LANES:]


def _rope_tables(positions):
    T = positions.size
    freqs = ROPE_THETA ** (-jnp.arange(ROPE_HALF, dtype=F32) / ROPE_HALF)
    rows = T // TOK_PER_ROW
    pos_c = jnp.repeat(positions.reshape(T).astype(jnp.int32), ROPE_HALF).reshape(rows, LANES)
    freq_c = jnp.tile(freqs, TOK_PER_ROW).reshape(1, LANES)
    tm = min(2048, T)
    tr = tm // TOK_PER_ROW
    rsel = (jnp.arange(tm)[:, None] // TOK_PER_ROW == jnp.arange(tr)[None, :]).astype(BF16)
    src = jnp.arange(LANES)[:, None] % ROPE_HALF
    dst = jnp.arange(LANES)[None, :]
    first = dst == ROPE_LO + src
    second = dst == ROPE_LO + ROPE_HALF + src
    fcos = (first | second).astype(F32)
    fsin = second.astype(F32) - first.astype(F32)
    zero = jnp.zeros((LANES, LANES), F32)
    fold = jnp.block([[fcos, zero], [zero, fsin]]).astype(BF16)
    lane = jnp.arange(LANES)
    cbase = ((lane < ROPE_LO) | (lane >= ROPE_LO + QK_ROPE)).astype(F32).reshape(1, LANES)
    full = lambda i: (0, 0)
    return pl.pallas_call(
        _trig_kernel,
        grid=(T // tm,),
        in_specs=[
            pl.BlockSpec((tr, LANES), lambda i: (i, 0)),
            pl.BlockSpec((1, LANES), full),
            pl.BlockSpec((tm, tr), full),
            pl.BlockSpec((2 * LANES, 2 * LANES), full),
            pl.BlockSpec((1, LANES), full),
        ],
        out_specs=[pl.BlockSpec((tm, LANES), lambda i: (i, 0))] * 2,
        out_shape=[jax.ShapeDtypeStruct((T, LANES), F32)] * 2,
        compiler_params=_cparams(("arbitrary",)),
        name="rope_trig",
    )(pos_c, freq_c, rsel, fold, cbase)


def _inproj_kernel(x_ref, cos_ref, sin_ref, shift_ref, scale_ref, gmix_ref, win_ref, gq_ref, wuq_ref,
                   gkv_ref, wukv_ref, gqn_ref, gqr_ref, gkn_ref, gkr_ref,
                   xlru_ref, ylru_ref, q_ref, k_ref, v_ref):
    HP = N_HEADS * HEAD_PAD
    x = x_ref[...]
    ms = jnp.mean(x * x, axis=-1, keepdims=True)
    xn = x * lax.rsqrt(ms + EPS) * gmix_ref[...]
    h = xn * (1.0 + scale_ref[0]) + shift_ref[0]
    z = _dot(h.astype(BF16), win_ref[...])
    xlru_ref[...] = z[:, :D_LRU]
    ylru_ref[...] = z[:, D_LRU:2 * D_LRU]
    o1 = 2 * D_LRU
    o2 = o1 + Q_LORA
    o3 = o2 + KV_LORA
    ql = z[:, o1:o2]
    kvl = z[:, o2:o3]
    kr = z[:, o3:o3 + LANES]
    kr_rot = z[:, o3 + LANES:]

    qn = ql * lax.rsqrt(jnp.mean(ql * ql, axis=-1, keepdims=True) + EPS) * gq_ref[...]
    qq = _dot(qn.astype(BF16), wuq_ref[...])
    kvn = kvl * lax.rsqrt(jnp.mean(kvl * kvl, axis=-1, keepdims=True) + EPS) * gkv_ref[...]
    kv = _dot(kvn.astype(BF16), wukv_ref[...])

    tm = x.shape[0]
    lane = lax.broadcasted_iota(jnp.int32, (tm, HP), 1)
    v_ref[...] = jnp.where((lane & (HEAD_PAD - 1)) == V_HEAD, 1.0, kv[:, HP:]).astype(BF16)

    cos_t = cos_ref[...]
    sin_t = sin_ref[...]
    gqn = gqn_ref[...]
    gkn = gkn_ref[...]
    cq = gqn * cos_t
    sq = gqr_ref[...] * sin_t
    kb = kr * (gkn * cos_t) + kr_rot * (gkr_ref[...] * sin_t)
    inv_w = 1.0 / QK_HEAD
    qscale = QK_HEAD ** -0.5 * LOG2_E
    for hh in range(N_HEADS):
        sl = slice(hh * HEAD_PAD, (hh + 1) * HEAD_PAD)
        qh = qq[:, sl]
        rq = lax.rsqrt(jnp.sum(qh * qh, axis=-1, keepdims=True) * inv_w + EPS) * qscale
        q_ref[:, sl] = ((qh * cq + qq[:, HP + hh * HEAD_PAD:HP + (hh + 1) * HEAD_PAD] * sq) * rq).astype(BF16)
        kraw = kv[:, sl] + kr
        rk = lax.rsqrt(jnp.sum(kraw * kraw, axis=-1, keepdims=True) * inv_w + EPS)
        k_ref[:, sl] = ((kv[:, sl] * gkn + kb) * rk).astype(BF16)


def _inproj(x2, cos_t, sin_t, mod3, g_mix, w_in_p, g_q_lat, w_uq_p, g_kv_lat, w_ukv_p,
            gqn_p, gqr_p, gkn_p, gkr_p, B, S, tm):
    T, D = x2.shape
    ns = S // tm
    HP = N_HEADS * HEAD_PAD
    row = lambda b, s: (b * ns + s, 0)
    full = lambda b, s: (0, 0)
    return pl.pallas_call(
        _inproj_kernel,
        grid=(B, ns),
        in_specs=[
            pl.BlockSpec((tm, D), row),
            pl.BlockSpec((tm, LANES), row),
            pl.BlockSpec((tm, LANES), row),
            pl.BlockSpec((1, 1, D), lambda b, s: (b * 6 + 0, 0, 0)),
            pl.BlockSpec((1, 1, D), lambda b, s: (b * 6 + 1, 0, 0)),
            pl.BlockSpec((1, D), full),
            pl.BlockSpec((D, D_IN_PAD), full),
            pl.BlockSpec((1, Q_LORA), full),
            pl.BlockSpec((Q_LORA, 2 * HP), full),
            pl.BlockSpec((1, KV_LORA), full),
            pl.BlockSpec((KV_LORA, 2 * HP), full),
            pl.BlockSpec((1, HEAD_PAD), full),
            pl.BlockSpec((1, HEAD_PAD), full),
            pl.BlockSpec((1, HEAD_PAD), full),
            pl.BlockSpec((1, HEAD_PAD), full),
        ],
        out_specs=[
            pl.BlockSpec((tm, D_LRU), row),
            pl.BlockSpec((tm, D_LRU), row),
            pl.BlockSpec((tm, HP), row),
            pl.BlockSpec((tm, HP), row),
            pl.BlockSpec((tm, HP), row),
        ],
        out_shape=[
            jax.ShapeDtypeStruct((T, D_LRU), F32),
            jax.ShapeDtypeStruct((T, D_LRU), F32),
            jax.ShapeDtypeStruct((T, HP), BF16),
            jax.ShapeDtypeStruct((T, HP), BF16),
            jax.ShapeDtypeStruct((T, HP), BF16),
        ],
        compiler_params=_cparams(("arbitrary", "arbitrary")),
        name="inproj",
    )(x2, cos_t, sin_t, mod3, mod3, g_mix, w_in_p, g_q_lat, w_uq_p, g_kv_lat, w_ukv_p,
      gqn_p, gqr_p, gkn_p, gkr_p)


def _gelu_tanh(x):
    return 0.5 * x * (1.0 + jnp.tanh(0.7978845608028654 * (x + 0.044715 * x * x * x)))


def _lru_kernel(x_ref, y_ref, cw_ref, cb_ref, wa_ref, ba_ref, wx_ref, bx_ref, lam_ref,
                o_ref, tail_ref, carry_ref):
    s = pl.program_id(1)

    @pl.when(s == 0)
    def _():
        tail_ref[...] = jnp.zeros_like(tail_ref)
        carry_ref[...] = jnp.zeros_like(carry_ref)

    x = x_ref[...]
    ts = x.shape[0]
    xext = jnp.concatenate([tail_ref[...], x], axis=0)
    cw = cw_ref[...]
    xc = x * cw[CONV_W - 1:CONV_W, :]
    for j in range(CONV_W - 1):
        sh = CONV_W - 1 - j
        xc = xc + xext[8 - sh:8 - sh + ts, :] * cw[j:j + 1, :]
    xc = xc + cb_ref[...]
    tail_ref[...] = x[ts - 8:, :]

    xb = xc.astype(BF16)
    r = _sigmoid(_dot(xb, wa_ref[...]) + ba_ref[...])
    i = _sigmoid(_dot(xb, wx_ref[...]) + bx_ref[...])
    lam = lam_ref[...]
    nl = -lam
    softplus = jnp.maximum(nl, 0.0) + jnp.log(1.0 + jnp.exp(-jnp.abs(nl)))
    log_a = (-LRU_C) * r * softplus
    a = jnp.exp(log_a)
    mult = jnp.sqrt(1.0 - jnp.exp(2.0 * log_a))
    u = mult * (i * xc)

    C = a.shape[1]
    a = a.reshape(ts // SUBLANES, SUBLANES, C)
    u = u.reshape(ts // SUBLANES, SUBLANES, C)
    sub = lax.broadcasted_iota(jnp.int32, (1, SUBLANES, 1), 1)
    sh = 1
    while sh < SUBLANES:
        a_prev = pltpu.roll(a, sh, axis=1)
        u_prev = pltpu.roll(u, sh, axis=1)
        m = sub >= sh
        u = jnp.where(m, a * u_prev + u, u)
        a = jnp.where(m, a * a_prev, a)
        sh *= 2
    a = a.reshape(ts, C)
    u = u.reshape(ts, C)
    h = carry_ref[0:1, :]
    groups = []
    for g0 in range(0, ts, SUBLANES):
        hg = u[g0:g0 + SUBLANES, :] + a[g0:g0 + SUBLANES, :] * h
        groups.append(hg)
        h = hg[SUBLANES - 1:SUBLANES, :]
    carry_ref[...] = jnp.broadcast_to(h, carry_ref.shape)
    hs = jnp.concatenate(groups, axis=0)
    o_ref[...] = (_gelu_tanh(y_ref[...]) * hs).astype(BF16)


def _lru(xlru, ylru, conv_w, conv_b, wa_d, b_a, wx_d, b_x, lam, B, S, ts):
    T, C = xlru.shape
    ns = S // ts
    row = lambda b, s: (b * ns + s, 0)
    full = lambda b, s: (0, 0)
    return pl.pallas_call(
        _lru_kernel,
        grid=(B, ns),
        in_specs=[
            pl.BlockSpec((ts, C), row),
            pl.BlockSpec((ts, C), row),
            pl.BlockSpec((CONV_W, C), full),
            pl.BlockSpec((1, C), full),
            pl.BlockSpec((C, C), full),
            pl.BlockSpec((1, C), full),
            pl.BlockSpec((C, C), full),
            pl.BlockSpec((1, C), full),
            pl.BlockSpec((1, C), full),
        ],
        out_specs=pl.BlockSpec((ts, C), row),
        out_shape=jax.ShapeDtypeStruct((T, C), BF16),
        scratch_shapes=[pltpu.VMEM((8, C), F32), pltpu.VMEM((8, C), F32)],
        compiler_params=_cparams(("arbitrary", "arbitrary")),
        name="lru",
    )(xlru, ylru, conv_w, conv_b, wa_d, b_a, wx_d, b_x, lam)


NEG_INF = -1e30


def _attn_kernel(q_ref, k_ref, v_ref, o_ref, *state, tq):
    m_refs = state[:N_HEADS]
    acc_refs = state[N_HEADS:]
    qi = pl.program_id(1)
    rowi = lax.broadcasted_iota(jnp.int32, (tq, tq), 0)
    coli = lax.broadcasted_iota(jnp.int32, (tq, tq), 1)
    diag_mask = coli <= rowi

    def head_slice(hh):
        return slice(hh * HEAD_PAD, (hh + 1) * HEAD_PAD)

    def weights(sc, m_b):
        cols = [jnp.exp2(sc[:, c0:c0 + LANES] - m_b) for c0 in range(0, tq, LANES)]
        return jnp.concatenate(cols, axis=1).astype(BF16)

    def scores(hh, r0):
        hs = head_slice(hh)
        return _dot_nt(q_ref[:, hs], k_ref[pl.ds(r0, tq), hs])

    r_diag = pl.multiple_of(qi * tq, tq)
    sc_next = scores(0, r_diag)
    for hh in range(N_HEADS):
        hs = head_slice(hh)
        sc = jnp.where(diag_mask, sc_next, NEG_INF)
        if hh + 1 < N_HEADS:
            sc_next = scores(hh + 1, r_diag)
        m_b = jnp.broadcast_to(jnp.max(sc, axis=-1, keepdims=True), (tq, LANES))
        m_refs[hh][...] = m_b
        acc_refs[hh][...] = _dot(weights(sc, m_b), v_ref[pl.ds(r_diag, tq), hs])

    @pl.loop(0, qi)
    def _(j):
        r0 = pl.multiple_of(j * tq, tq)
        sc_next = scores(0, r0)
        for hh in range(N_HEADS):
            hs = head_slice(hh)
            sc = sc_next
            if hh + 1 < N_HEADS:
                sc_next = scores(hh + 1, r0)
            m_b = m_refs[hh][...]
            m_new = jnp.maximum(m_b, jnp.max(sc, axis=-1, keepdims=True))
            alpha = jnp.exp2(m_b - m_new)
            m_refs[hh][...] = m_new
            acc_refs[hh][...] = (alpha * acc_refs[hh][...]
                                 + _dot(weights(sc, m_new), v_ref[pl.ds(r0, tq), hs]))

    for hh in range(N_HEADS):
        acc = acc_refs[hh][...]
        o = acc[:, :V_HEAD] / acc[:, SUM_LANE:SUM_LANE + 1]
        o_ref[:, hh * V_HEAD:(hh + 1) * V_HEAD] = o.astype(BF16)


def _attn(qp, kp, v, B, S, tq):
    T = qp.shape[0]
    nq = S // tq
    HP = N_HEADS * HEAD_PAD
    HV = N_HEADS * V_HEAD
    return pl.pallas_call(
        functools.partial(_attn_kernel, tq=tq),
        grid=(B, nq),
        in_specs=[
            pl.BlockSpec((tq, HP), lambda b, i: (b * nq + i, 0)),
            pl.BlockSpec((S, HP), lambda b, i: (b, 0)),
            pl.BlockSpec((S, HP), lambda b, i: (b, 0)),
        ],
        out_specs=pl.BlockSpec((tq, HV), lambda b, i: (b * nq + i, 0)),
        out_shape=jax.ShapeDtypeStruct((T, HV), BF16),
        scratch_shapes=([pltpu.VMEM((tq, LANES), F32)] * N_HEADS
                        + [pltpu.VMEM((tq, HEAD_PAD), F32)] * N_HEADS),
        compiler_params=_cparams(("arbitrary", "arbitrary")),
        name="attn",
    )(qp, kp, v)


def _outproj_kernel(lru_ref, att_ref, x_ref, gate_ref, shift_ref, scale_ref, gffn_ref,
                    wo1_ref, wo2_ref, wrh_ref, wrl_ref, br_ref, tri_ref,
                    x1_ref, h2p_ref, idx_ref, gat_ref, rank_ref, cnt_ref, run_ref):
    first = (pl.program_id(0) == 0) & (pl.program_id(1) == 0)

    @pl.when(first)
    def _():
        run_ref[...] = jnp.zeros_like(run_ref)

    mix = _dot(lru_ref[...], wo1_ref[...]) + _dot(att_ref[...], wo2_ref[...])
    x1 = x_ref[...] + gate_ref[0] * mix
    x1_ref[...] = x1
    ms = jnp.mean(x1 * x1, axis=-1, keepdims=True)
    h2 = x1 * lax.rsqrt(ms + EPS) * gffn_ref[...]
    h2 = h2 * (1.0 + scale_ref[0]) + shift_ref[0]

    hhi = h2.astype(BF16)
    hlo = (h2 - hhi.astype(F32)).astype(BF16)
    h2p_ref[...] = _pack_halves(h2)

    wrh = wrh_ref[...]
    logits = _dot_nt(wrh, hhi) + _dot_nt(wrh, hlo) + _dot_nt(wrl_ref[...], hhi) + br_ref[...]
    ne, tm = logits.shape
    eio = lax.broadcasted_iota(jnp.int32, (ne, tm), 0)
    vals, idxs, sels = [], [], []
    l = logits
    for _ in range(TOP_K):
        m = jnp.max(l, axis=0, keepdims=True)
        idx = jnp.min(jnp.where(l == m, eio, ne), axis=0, keepdims=True)
        sel = eio == idx
        l = jnp.where(sel, -jnp.inf, l)
        vals.append(m)
        idxs.append(idx)
        sels.append(sel)
    es = [jnp.exp(v - vals[0]) for v in vals]
    inv = 1.0 / (es[0] + es[1] + es[2] + es[3])
    sel_any = jnp.where(sels[0] | sels[1] | sels[2] | sels[3], 1.0, 0.0)
    run = run_ref[...]
    excl = _dot(sel_any.astype(BF16), tri_ref[...]) + run
    for kk in range(TOP_K):
        idx_ref[kk:kk + 1, :] = idxs[kk]
        gat_ref[kk:kk + 1, :] = es[kk] * inv
        rk = jnp.sum(jnp.where(sels[kk], excl, 0.0), axis=0, keepdims=True)
        rank_ref[kk:kk + 1, :] = rk.astype(jnp.int32)
    run = run + jnp.sum(sel_any, axis=1, keepdims=True)
    run_ref[...] = run
    cnt_ref[...] = run.astype(jnp.int32)


def _outproj(lru_o, att_o, x2, mod3, g_ffn, wo1, wo2, wr_hi, wr_lo, b_r, tri, b0, B, S, tm):
    D = x2.shape[1]
    T = B * S
    ns = S // tm
    C = lru_o.shape[1]
    row_in = lambda b, s: ((b0 + b) * ns + s, 0)
    row = lambda b, s: (b * ns + s, 0)
    col = lambda b, s: (0, b * ns + s)
    full = lambda b, s: (0, 0)
    return pl.pallas_call(
        _outproj_kernel,
        grid=(B, ns),
        in_specs=[
            pl.BlockSpec((tm, C), row_in),
            pl.BlockSpec((tm, C), row_in),
            pl.BlockSpec((tm, D), row_in),
            pl.BlockSpec((1, 1, D), lambda b, s: ((b0 + b) * 6 + 2, 0, 0)),
            pl.BlockSpec((1, 1, D), lambda b, s: ((b0 + b) * 6 + 3, 0, 0)),
            pl.BlockSpec((1, 1, D), lambda b, s: ((b0 + b) * 6 + 4, 0, 0)),
            pl.BlockSpec((1, D), full),
            pl.BlockSpec((C, D), full),
            pl.BlockSpec((C, D), full),
            pl.BlockSpec((N_EXPERTS, D), full),
            pl.BlockSpec((N_EXPERTS, D), full),
            pl.BlockSpec((N_EXPERTS, 1), full),
            pl.BlockSpec(tri.shape, full),
        ],
        out_specs=[
            pl.BlockSpec((tm, D), row),
            pl.BlockSpec((tm, D // 2), row),
            pl.BlockSpec((TOP_K, tm), col),
            pl.BlockSpec((TOP_K, tm), col),
            pl.BlockSpec((TOP_K, tm), col),
            pl.BlockSpec((N_EXPERTS, 1), full),
        ],
        out_shape=[
            jax.ShapeDtypeStruct((T, D), F32),
            jax.ShapeDtypeStruct((T, D // 2), jnp.int32),
            jax.ShapeDtypeStruct((TOP_K, T), jnp.int32),
            jax.ShapeDtypeStruct((TOP_K, T), F32),
            jax.ShapeDtypeStruct((TOP_K, T), jnp.int32),
            jax.ShapeDtypeStruct((N_EXPERTS, 1), jnp.int32),
        ],
        scratch_shapes=[pltpu.VMEM((N_EXPERTS, 1), F32)],
        compiler_params=_cparams(("arbitrary", "arbitrary")),
        name="outproj",
    )(lru_o, att_o, x2, mod3, mod3, mod3, g_ffn, wo1, wo2, wr_hi, wr_lo, b_r, tri)


SC_CORES = 2
SC_SUBCORES = 16
SC_WORKERS = SC_CORES * SC_SUBCORES


def _sc_mesh():
    return plsc.VectorSubcoreMesh(core_axis_name="c", subcore_axis_name="s",
                                  num_cores=SC_CORES, num_subcores=SC_SUBCORES)


def _sc_worker_id():
    return lax.axis_index("s") * SC_CORES + lax.axis_index("c")


def _sc_scatter_rows(rows, idx, n_out, g):
    T, W = rows.shape
    K = idx.shape[0]
    per_w = T // SC_WORKERS
    nch = per_w // g
    assert per_w * SC_WORKERS == T and nch * g == per_w and nch % 2 == 0
    idx_w = idx.reshape(K, SC_WORKERS, nch, g).transpose(1, 2, 0, 3).reshape(SC_WORKERS, nch * K, g)

    def body(rows_hbm, idx_hbm, out_hbm, idx_v, buf0, buf1, semr0, semr1, semw):
        wid = _sc_worker_id()
        base = wid * per_w
        pltpu.sync_copy(idx_hbm.at[wid], idx_v)

        def read(j, buf, sem):
            return pltpu.make_async_copy(rows_hbm.at[pl.ds(base + j * g, g)], buf, sem)

        def scatter(j, buf):
            copies = [pltpu.async_copy(buf, out_hbm.at[idx_v.at[j * K + kk]], semw)
                      for kk in range(K)]
            for cp in copies:
                cp.wait()

        read(0, buf0, semr0).start()

        @pl.loop(0, nch // 2)
        def _(jj):
            j0 = 2 * jj
            read(j0 + 1, buf1, semr1).start()
            read(j0, buf0, semr0).wait()
            scatter(j0, buf0)

            @pl.when(j0 + 2 < nch)
            def _():
                read(j0 + 2, buf0, semr0).start()

            read(j0 + 1, buf1, semr1).wait()
            scatter(j0 + 1, buf1)

    return pl.kernel(
        body,
        out_type=jax.ShapeDtypeStruct((n_out, W), rows.dtype),
        mesh=_sc_mesh(),
        scratch_types=[
            pltpu.VMEM((nch * K, g), jnp.int32),
            pltpu.VMEM((g, W), rows.dtype),
            pltpu.VMEM((g, W), rows.dtype),
            pltpu.SemaphoreType.DMA,
            pltpu.SemaphoreType.DMA,
            pltpu.SemaphoreType.DMA,
        ],
        name="sc_scatter_rows",
    )(rows, idx_w)


def _sc_gather_rows(table, idx, g):
    W = table.shape[1]
    N = idx.shape[0]
    per_w = N // SC_WORKERS
    nch = per_w // g
    assert per_w * SC_WORKERS == N and nch * g == per_w and nch % 2 == 0
    idx_w = idx.reshape(SC_WORKERS, nch, g)

    def body(table_hbm, idx_hbm, out_hbm, idx_v, buf0, buf1, sem0, sem1):
        wid = _sc_worker_id()
        base = wid * per_w
        pltpu.sync_copy(idx_hbm.at[wid], idx_v)

        def gather(j, buf, sem):
            return pltpu.make_async_copy(table_hbm.at[idx_v.at[j]], buf, sem)

        def put(j, buf):
            pltpu.sync_copy(buf, out_hbm.at[pl.ds(base + j * g, g)])

        gather(0, buf0, sem0).start()

        @pl.loop(0, nch // 2)
        def _(jj):
            j0 = 2 * jj
            gather(j0 + 1, buf1, sem1).start()
            gather(j0, buf0, sem0).wait()
            put(j0, buf0)

            @pl.when(j0 + 2 < nch)
            def _():
                gather(j0 + 2, buf0, sem0).start()

            gather(j0 + 1, buf1, sem1).wait()
            put(j0 + 1, buf1)

    return pl.kernel(
        body,
        out_type=jax.ShapeDtypeStruct((N, W), table.dtype),
        mesh=_sc_mesh(),
        scratch_types=[
            pltpu.VMEM((nch, g), jnp.int32),
            pltpu.VMEM((g, W), table.dtype),
            pltpu.VMEM((g, W), table.dtype),
            pltpu.SemaphoreType.DMA,
            pltpu.SemaphoreType.DMA,
        ],
        name="sc_gather_rows",
    )(table, idx_w)


def _experts_kernel(be_ref, bv_ref, bf_ref, xs_ref, w1_ref, b1_ref, w2_ref, b2_ref, ys_ref,
                    w1b_ref, w2b_ref):
    i = pl.program_id(0)
    nvalid = bv_ref[i]

    @pl.when(bf_ref[i] > 0)
    def _():
        w1b_ref[...] = w1_ref[0].astype(BF16)
        w2b_ref[...] = w2_ref[0].astype(BF16)

    @pl.when(nvalid > 0)
    def _():
        xw = xs_ref[...]
        rowi = lax.broadcasted_iota(jnp.int32, (xw.shape[0], 1), 0)
        lo, hi = _unpack_halves(jnp.where(rowi < nvalid, xw, 0))
        half = xw.shape[1]
        gu = (_dot(lo.astype(BF16), w1b_ref[:half, :]) + _dot(hi.astype(BF16), w1b_ref[half:, :])
              + b1_ref[0])
        glu = jnp.minimum(gu[:, :D_FF], SWIGLU_LIMIT)
        lin = jnp.clip(gu[:, D_FF:], -SWIGLU_LIMIT, SWIGLU_LIMIT)
        act = (lin + 1.0) * (glu * _sigmoid(SWIGLU_ALPHA * glu))
        ys_ref[...] = _pack_halves(_dot(act.astype(BF16), w2b_ref[...]) + b2_ref[0])


def _experts(blk_e, blk_v, blk_f, blk_r, xs, w1, b1, w2, b2):
    P, W = xs.shape
    nb = P // MOE_BLOCK
    E, D, F2 = w1.shape
    grid_spec = pltpu.PrefetchScalarGridSpec(
        num_scalar_prefetch=4,
        grid=(nb,),
        in_specs=[
            pl.BlockSpec((MOE_BLOCK, W), lambda i, be, bv, bf, br: (br[i], 0)),
            pl.BlockSpec((1, D, F2), lambda i, be, bv, bf, br: (be[i], 0, 0)),
            pl.BlockSpec((1, 1, F2), lambda i, be, bv, bf, br: (be[i], 0, 0)),
            pl.BlockSpec((1, D_FF, D), lambda i, be, bv, bf, br: (be[i], 0, 0)),
            pl.BlockSpec((1, 1, D), lambda i, be, bv, bf, br: (be[i], 0, 0)),
        ],
        out_specs=pl.BlockSpec((MOE_BLOCK, D // 2), lambda i, be, bv, bf, br: (br[i], 0)),
        scratch_shapes=[pltpu.VMEM((D, F2), BF16), pltpu.VMEM((D_FF, D), BF16)],
    )

    def kern(be_ref, bv_ref, bf_ref, br_ref, *refs):
        del br_ref
        _experts_kernel(be_ref, bv_ref, bf_ref, *refs)

    return pl.pallas_call(
        kern,
        grid_spec=grid_spec,
        out_shape=jax.ShapeDtypeStruct((P, D // 2), jnp.int32),
        compiler_params=_cparams(("arbitrary",)),
        name="experts",
    )(blk_e, blk_v, blk_f, blk_r, xs, w1, b1.reshape(E, 1, F2), w2, b2.reshape(E, 1, D))


def _combine_kernel(x1_ref, g_ref, gate_ref, y0_ref, y1_ref, y2_ref, y3_ref, *rest):
    o_ref = rest[-1]
    g = g_ref[...]
    half = y0_ref.shape[2]
    acc_lo = acc_hi = None
    for kk, y_ref in enumerate((y0_ref, y1_ref, y2_ref, y3_ref)):
        lo, hi = _unpack_halves(y_ref[0])
        gk = g[:, kk:kk + 1]
        acc_lo = lo * gk if acc_lo is None else acc_lo + lo * gk
        acc_hi = hi * gk if acc_hi is None else acc_hi + hi * gk
    gate = gate_ref[0]
    o_ref[:, :half] = x1_ref[:, :half] + gate[:, :half] * acc_lo
    o_ref[:, half:] = x1_ref[:, half:] + gate[:, half:] * acc_hi


def _combine(x1, gates_tk, mod3, ysg, out_prev, b0, B, nb_total, S, tm):
    D = x1.shape[1]
    ns = S // tm
    row = lambda b, s: (b * ns + s, 0)
    row_out = lambda b, s: ((b0 + b) * ns + s, 0)

    def yspec(kk):
        return pl.BlockSpec((1, tm, D // 2), lambda b, s: (kk, b * ns + s, 0))

    in_specs = [
        pl.BlockSpec((tm, D), row),
        pl.BlockSpec((tm, TOP_K), row),
        pl.BlockSpec((1, 1, D), lambda b, s: ((b0 + b) * 6 + 5, 0, 0)),
        yspec(0), yspec(1), yspec(2), yspec(3),
    ]
    args = [x1, gates_tk, mod3, ysg, ysg, ysg, ysg]
    aliases = {}
    if out_prev is not None:
        in_specs.append(pl.BlockSpec(memory_space=pl.ANY))
        args.append(out_prev)
        aliases = {len(args) - 1: 0}
    return pl.pallas_call(
        _combine_kernel,
        grid=(B, ns),
        in_specs=in_specs,
        out_specs=pl.BlockSpec((tm, D), row_out),
        out_shape=jax.ShapeDtypeStruct((nb_total * S, D), F32),
        input_output_aliases=aliases,
        compiler_params=_cparams(("arbitrary", "arbitrary")),
        name="combine",
    )(*args)


def _block_diag(w):
    n, c, d = w.shape
    eye = jnp.eye(n, dtype=w.dtype)
    return jnp.einsum("ncd,nm->ncmd", w, eye).reshape(n * c, n * d)


def _pad_heads(w, width):
    k = w.shape[0]
    w = w.reshape(k, N_HEADS, width)
    return jnp.pad(w, ((0, 0), (0, 0), (0, HEAD_PAD - width))).reshape(k, N_HEADS * HEAD_PAD)


def kernel(x, c, positions, w_ada, b_ada, g_mix, w_in, conv_w, conv_b, w_a, b_a, w_x, b_x, lam,
           g_q_lat, w_uq, g_kv_lat, w_ukv, g_qn, g_kn, w_out, g_ffn, w_router, b_router,
           w1, b1, w2, b2):
    B, S, D = x.shape
    T = B * S
    depth = w_ada.shape[0]
    tm_in = min(512, S)
    ts_lru = min(512, S)
    tq = min(512, S)
    tm_out = min(512, S)
    tm_comb = min(512, S)
    n_groups = 2 if B % 2 == 0 else 1
    Bg = B // n_groups
    Tg = Bg * S
    g_disp = min(64, Tg // SC_WORKERS // 2)
    g_comb = min(64, Tg * TOP_K // SC_WORKERS // 2)

    o1 = 2 * D_LRU
    o2 = o1 + Q_LORA
    o3 = o2 + KV_LORA
    tri = (jnp.arange(tm_out)[:, None] < jnp.arange(tm_out)[None, :]).astype(BF16)
    cos_t, sin_t = _rope_tables(positions)
    lane = jnp.arange(HEAD_PAD, dtype=jnp.int32)
    first = (lane >= ROPE_LO) & (lane < ROPE_LO + ROPE_HALF)
    second = (lane >= ROPE_LO + ROPE_HALF) & (lane < ROPE_LO + QK_ROPE)
    partner = jnp.where(first, lane + ROPE_HALF, jnp.where(second, lane - ROPE_HALF, 0))
    is_rot = first | second

    def rot_cols(w):
        k = w.shape[0]
        w3 = w.reshape(k, -1, HEAD_PAD)
        return jnp.where(is_rot[None, None, :], w3[:, :, partner], 0.0).reshape(w.shape)

    x2 = x.reshape(T, D)
    for l in range(depth):
        mod3 = _ada(c, w_ada[l], b_ada[l]).reshape(B * 6, 1, D)

        w_in_l = w_in[l]
        kr_cols = jnp.pad(w_in_l[:, o3:], ((0, 0), (ROPE_LO, LANES - ROPE_LO - QK_ROPE)))
        w_in_p = jnp.concatenate([w_in_l[:, :o3], kr_cols, rot_cols(kr_cols)], axis=1).astype(BF16)
        w_uq_h = _pad_heads(w_uq[l], QK_HEAD)
        w_uq_p = jnp.concatenate([w_uq_h, rot_cols(w_uq_h)], axis=1).astype(BF16)
        w_ukv_l = w_ukv[l].reshape(KV_LORA, N_HEADS, QK_NOPE + V_HEAD)
        w_uk_h = _pad_heads(w_ukv_l[:, :, :QK_NOPE].reshape(KV_LORA, N_HEADS * QK_NOPE), QK_NOPE)
        w_uv_h = _pad_heads(w_ukv_l[:, :, QK_NOPE:].reshape(KV_LORA, N_HEADS * V_HEAD), V_HEAD)
        w_ukv_p = jnp.concatenate([w_uk_h, w_uv_h], axis=1).astype(BF16)
        gqn_p = jnp.pad(g_qn[l], (0, HEAD_PAD - QK_HEAD)).reshape(1, HEAD_PAD)
        gkn_p = jnp.pad(g_kn[l], (0, HEAD_PAD - QK_HEAD)).reshape(1, HEAD_PAD)

        xlru, ylru, qp, kp, v = _inproj(
            x2, cos_t, sin_t, mod3, g_mix[l].reshape(1, D), w_in_p, g_q_lat[l].reshape(1, Q_LORA),
            w_uq_p, g_kv_lat[l].reshape(1, KV_LORA), w_ukv_p, gqn_p, rot_cols(gqn_p), gkn_p,
            rot_cols(gkn_p), B, S, tm_in)

        lru_o = _lru(xlru, ylru, conv_w[l], conv_b[l].reshape(1, D_LRU),
                     _block_diag(w_a[l]).astype(BF16), b_a[l].reshape(1, D_LRU),
                     _block_diag(w_x[l]).astype(BF16), b_x[l].reshape(1, D_LRU),
                     lam[l].reshape(1, D_LRU), B, S, ts_lru)

        att_o = _attn(qp, kp, v, B, S, tq)

        w_out_b = w_out[l].astype(BF16)
        wr_hi, wr_lo = _split_bf16(w_router[l].T)
        g_ffn_l = g_ffn[l].reshape(1, D)
        b_r = b_router[l].reshape(N_EXPERTS, 1)
        eio = jnp.arange(N_EXPERTS, dtype=jnp.int32)
        n_blocks = -(-(Tg * TOP_K) // MOE_BLOCK) + N_EXPERTS
        bi = jnp.arange(n_blocks, dtype=jnp.int32)

        x_next = None
        for gi in range(n_groups):
            b0 = gi * Bg
            x1, h2p, idx_t, gat_t, rank_t, counts = _outproj(
                lru_o, att_o, x2, mod3, g_ffn_l, w_out_b[:D_LRU], w_out_b[D_LRU:],
                wr_hi, wr_lo, b_r, tri, b0, Bg, S, tm_out)

            counts = counts.reshape(N_EXPERTS)
            nblk_e = (counts + MOE_BLOCK - 1) // MOE_BLOCK
            blk_end = jnp.cumsum(nblk_e)
            pad_start = (blk_end - nblk_e) * MOE_BLOCK
            total = blk_end[-1]
            blk_r = jnp.minimum(bi, total - 1).astype(jnp.int32)
            blk_e = jnp.minimum(jnp.sum(blk_end[None, :] <= blk_r[:, None], axis=1),
                                N_EXPERTS - 1).astype(jnp.int32)
            blk_onehot = blk_e[:, None] == eio[None, :]
            blk_first = jnp.sum(jnp.where(blk_onehot, (blk_end - nblk_e)[None, :], 0), axis=1)
            blk_cnt = jnp.sum(jnp.where(blk_onehot, counts[None, :], 0), axis=1)
            blk_n = jnp.where(bi < total,
                              jnp.clip(blk_cnt - (bi - blk_first) * MOE_BLOCK, 0, MOE_BLOCK),
                              0).astype(jnp.int32)
            slot0 = jnp.sum(jnp.where(idx_t[None] == eio[:, None, None],
                                      pad_start[:, None, None], 0), axis=0)
            dest = slot0.astype(jnp.int32) + rank_t

            xs = _sc_scatter_rows(h2p, dest, n_blocks * MOE_BLOCK, g_disp)
            blk_f = ((bi == blk_first) & (bi < total)).astype(jnp.int32)
            ys = _experts(blk_e, blk_n, blk_f, blk_r, xs, w1[l], b1[l], w2[l], b2[l])
            ysg = _sc_gather_rows(ys, dest.reshape(TOP_K * Tg), g_comb).reshape(TOP_K, Tg, D // 2)
            x_next = _combine(x1, gat_t.T, mod3, ysg, x_next, b0, Bg, B, S, tm_comb)
        x2 = x_next
    return x2.reshape(B, S, D)
```

```python
import functools

import jax
import jax.numpy as jnp
from jax import lax
from jax.experimental import pallas as pl
from jax.experimental.pallas import tpu as pltpu
from jax.experimental.pallas import tpu_sc as plsc

D_MODEL = 1024
D_LRU = 512
LRU_BLOCKS = 8
LRU_BD = 64
CONV_W = 4
LRU_C = 8.0
N_HEADS = 8
QK_NOPE = 64
QK_ROPE = 32
QK_HEAD = 96
V_HEAD = 64
Q_LORA = 256
KV_LORA = 128
ROPE_THETA = 10000.0
N_EXPERTS = 32
TOP_K = 4
D_FF = 1024
SWIGLU_LIMIT = 7.0
SWIGLU_ALPHA = 1.702
MOE_BLOCK = 512
EPS = 1e-6

LANES = 128
SUBLANES = 8
HEAD_PAD = 128
ROPE_LO = QK_NOPE
ROPE_HALF = QK_ROPE // 2
TOK_PER_ROW = LANES // ROPE_HALF
D_IN_PAD = 2 * D_LRU + Q_LORA + KV_LORA + 2 * LANES
LOG2_E = 1.4426950408889634
SUM_LANE = V_HEAD
MAX_LANE = V_HEAD + 1

VMEM_LIMIT = 56 * 1024 * 1024

F32 = jnp.float32
BF16 = jnp.bfloat16


def _cparams(sem):
    return pltpu.CompilerParams(dimension_semantics=sem, vmem_limit_bytes=VMEM_LIMIT)


def _dot(a, b):
    return jnp.dot(a, b, preferred_element_type=F32)


def _dot_nt(a, b):
    return lax.dot_general(a, b, (((1,), (1,)), ((), ())), preferred_element_type=F32)


def _split_bf16(a):
    hi = a.astype(BF16)
    lo = (a - hi.astype(F32)).astype(BF16)
    return hi, lo


def _sigmoid(x):
    return 1.0 / (1.0 + jnp.exp(-x))


def _pack_halves(x):
    bits = lax.bitcast_convert_type(x.astype(BF16).astype(F32), jnp.uint32)
    half = x.shape[1] // 2
    words = (bits[:, :half] >> 16) | (bits[:, half:] & jnp.uint32(0xFFFF0000))
    return lax.bitcast_convert_type(words, jnp.int32)


def _unpack_halves(words):
    w = lax.bitcast_convert_type(words, jnp.uint32)
    lo = lax.bitcast_convert_type(w << 16, F32)
    hi = lax.bitcast_convert_type(w & jnp.uint32(0xFFFF0000), F32)
    return lo, hi


def _ada_kernel(c_ref, w_ref, b_ref, o_ref):
    c = c_ref[...]
    s = c * _sigmoid(c)
    shi, slo = _split_bf16(s)
    whi, wlo = _split_bf16(w_ref[...])
    o_ref[...] = _dot(shi, whi) + _dot(slo, whi) + _dot(shi, wlo) + b_ref[...]


def _ada(c, w_ada, b_ada):
    B, D = c.shape
    N = w_ada.shape[1]
    tn = 1024
    return pl.pallas_call(
        _ada_kernel,
        grid=(N // tn,),
        in_specs=[
            pl.BlockSpec((B, D), lambda j: (0, 0)),
            pl.BlockSpec((D, tn), lambda j: (0, j)),
            pl.BlockSpec((1, tn), lambda j: (0, j)),
        ],
        out_specs=pl.BlockSpec((B, tn), lambda j: (0, j)),
        out_shape=jax.ShapeDtypeStruct((B, N), F32),
        compiler_params=_cparams(("arbitrary",)),
        name="ada",
    )(c, w_ada, b_ada.reshape(1, N))


def _trig_kernel(pos_ref, freq_ref, rsel_ref, fold_ref, cbase_ref, cos_ref, sin_ref):
    ang = pos_ref[...].astype(F32) * freq_ref[...]
    cs = jnp.concatenate([jnp.cos(ang), jnp.sin(ang)], axis=1)
    tm = cos_ref.shape[0]
    row = lax.broadcasted_iota(jnp.int32, (tm, 2 * LANES), 0)
    lane = lax.broadcasted_iota(jnp.int32, (tm, 2 * LANES), 1)
    own = ((lane % LANES) // ROPE_HALF) == (row % TOK_PER_ROW)
    rsel = rsel_ref[...]
    fold = fold_ref[...]
    by_row = sum(_dot(rsel, part) for part in _split_bf16(cs))
    mine = jnp.where(own, by_row, 0.0)
    out = sum(_dot(part, fold) for part in _split_bf16(mine))
    cos_ref[...] = out[:, :LANES] + cbase_ref[...]
    sin_ref[...] = out[:, LANES:]


def _rope_tables(positions):
    T = positions.size
    freqs = ROPE_THETA ** (-jnp.arange(ROPE_HALF, dtype=F32) / ROPE_HALF)
    rows = T // TOK_PER_ROW
    pos_c = jnp.repeat(positions.reshape(T).astype(jnp.int32), ROPE_HALF).reshape(rows, LANES)
    freq_c = jnp.tile(freqs, TOK_PER_ROW).reshape(1, LANES)
    tm = min(2048, T)
    tr = tm // TOK_PER_ROW
    rsel = (jnp.arange(tm)[:, None] // TOK_PER_ROW == jnp.arange(tr)[None, :]).astype(BF16)
    src = jnp.arange(LANES)[:, None] % ROPE_HALF
    dst = jnp.arange(LANES)[None, :]
    first = dst == ROPE_LO + src
    second = dst == ROPE_LO + ROPE_HALF + src
    fcos = (first | second).astype(F32)
    fsin = second.astype(F32) - first.astype(F32)
    zero = jnp.zeros((LANES, LANES), F32)
    fold = jnp.block([[fcos, zero], [zero, fsin]]).astype(BF16)
    lane = jnp.arange(LANES)
    cbase = ((lane < ROPE_LO) | (lane >= ROPE_LO + QK_ROPE)).astype(F32).reshape(1, LANES)
    full = lambda i: (0, 0)
    return pl.pallas_call(
        _trig_kernel,
        grid=(T // tm,),
        in_specs=[
            pl.BlockSpec((tr, LANES), lambda i: (i, 0)),
            pl.BlockSpec((1, LANES), full),
            pl.BlockSpec((tm, tr), full),
            pl.BlockSpec((2 * LANES, 2 * LANES), full),
            pl.BlockSpec((1, LANES), full),
        ],
        out_specs=[pl.BlockSpec((tm, LANES), lambda i: (i, 0))] * 2,
        out_shape=[jax.ShapeDtypeStruct((T, LANES), F32)] * 2,
        compiler_params=_cparams(("arbitrary",)),
        name="rope_trig",
    )(pos_c, freq_c, rsel, fold, cbase)


def _inproj_kernel(x_ref, cos_ref, sin_ref, shift_ref, scale_ref, gmix_ref, win_ref, gq_ref, wuq_ref,
                   gkv_ref, wukv_ref, gqn_ref, gqr_ref, gkn_ref, gkr_ref,
                   xlru_ref, ylru_ref, q_ref, k_ref, v_ref):
    HP = N_HEADS * HEAD_PAD
    x = x_ref[...]
    ms = jnp.mean(x * x, axis=-1, keepdims=True)
    xn = x * lax.rsqrt(ms + EPS) * gmix_ref[...]
    h = xn * (1.0 + scale_ref[0]) + shift_ref[0]
    z = _dot(h.astype(BF16), win_ref[...])
    xlru_ref[...] = z[:, :D_LRU]
    ylru_ref[...] = z[:, D_LRU:2 * D_LRU]
    o1 = 2 * D_LRU
    o2 = o1 + Q_LORA
    o3 = o2 + KV_LORA
    ql = z[:, o1:o2]
    kvl = z[:, o2:o3]
    kr = z[:, o3:o3 + LANES]
    kr_rot = z[:, o3 + LANES:]

    qn = ql * lax.rsqrt(jnp.mean(ql * ql, axis=-1, keepdims=True) + EPS) * gq_ref[...]
    qq = _dot(qn.astype(BF16), wuq_ref[...])
    kvn = kvl * lax.rsqrt(jnp.mean(kvl * kvl, axis=-1, keepdims=True) + EPS) * gkv_ref[...]
    kv = _dot(kvn.astype(BF16), wukv_ref[...])

    tm = x.shape[0]
    lane = lax.broadcasted_iota(jnp.int32, (tm, HP), 1)
    v_ref[...] = jnp.where((lane & (HEAD_PAD - 1)) == V_HEAD, 1.0, kv[:, HP:]).astype(BF16)

    cos_t = cos_ref[...]
    sin_t = sin_ref[...]
    gqn = gqn_ref[...]
    gkn = gkn_ref[...]
    cq = gqn * cos_t
    sq = gqr_ref[...] * sin_t
    kb = kr * (gkn * cos_t) + kr_rot * (gkr_ref[...] * sin_t)
    inv_w = 1.0 / QK_HEAD
    qscale = QK_HEAD ** -0.5 * LOG2_E
    for hh in range(N_HEADS):
        sl = slice(hh * HEAD_PAD, (hh + 1) * HEAD_PAD)
        qh = qq[:, sl]
        rq = lax.rsqrt(jnp.sum(qh * qh, axis=-1, keepdims=True) * inv_w + EPS) * qscale
        q_ref[:, sl] = ((qh * cq + qq[:, HP + hh * HEAD_PAD:HP + (hh + 1) * HEAD_PAD] * sq) * rq).astype(BF16)
        kraw = kv[:, sl] + kr
        rk = lax.rsqrt(jnp.sum(kraw * kraw, axis=-1, keepdims=True) * inv_w + EPS)
        k_ref[:, sl] = ((kv[:, sl] * gkn + kb) * rk).astype(BF16)


def _inproj(x2, cos_t, sin_t, mod3, g_mix, w_in_p, g_q_lat, w_uq_p, g_kv_lat, w_ukv_p,
            gqn_p, gqr_p, gkn_p, gkr_p, B, S, tm):
    T, D = x2.shape
    ns = S // tm
    HP = N_HEADS * HEAD_PAD
    row = lambda b, s: (b * ns + s, 0)
    full = lambda b, s: (0, 0)
    return pl.pallas_call(
        _inproj_kernel,
        grid=(B, ns),
        in_specs=[
            pl.BlockSpec((tm, D), row),
            pl.BlockSpec((tm, LANES), row),
            pl.BlockSpec((tm, LANES), row),
            pl.BlockSpec((1, 1, D), lambda b, s: (b * 6 + 0, 0, 0)),
            pl.BlockSpec((1, 1, D), lambda b, s: (b * 6 + 1, 0, 0)),
            pl.BlockSpec((1, D), full),
            pl.BlockSpec((D, D_IN_PAD), full),
            pl.BlockSpec((1, Q_LORA), full),
            pl.BlockSpec((Q_LORA, 2 * HP), full),
            pl.BlockSpec((1, KV_LORA), full),
            pl.BlockSpec((KV_LORA, 2 * HP), full),
            pl.BlockSpec((1, HEAD_PAD), full),
            pl.BlockSpec((1, HEAD_PAD), full),
            pl.BlockSpec((1, HEAD_PAD), full),
            pl.BlockSpec((1, HEAD_PAD), full),
        ],
        out_specs=[
            pl.BlockSpec((tm, D_LRU), row),
            pl.BlockSpec((tm, D_LRU), row),
            pl.BlockSpec((tm, HP), row),
            pl.BlockSpec((tm, HP), row),
            pl.BlockSpec((tm, HP), row),
        ],
        out_shape=[
            jax.ShapeDtypeStruct((T, D_LRU), F32),
            jax.ShapeDtypeStruct((T, D_LRU), F32),
            jax.ShapeDtypeStruct((T, HP), BF16),
            jax.ShapeDtypeStruct((T, HP), BF16),
            jax.ShapeDtypeStruct((T, HP), BF16),
        ],
        compiler_params=_cparams(("arbitrary", "arbitrary")),
        name="inproj",
    )(x2, cos_t, sin_t, mod3, mod3, g_mix, w_in_p, g_q_lat, w_uq_p, g_kv_lat, w_ukv_p,
      gqn_p, gqr_p, gkn_p, gkr_p)


def _gelu_tanh(x):
    return 0.5 * x * (1.0 + jnp.tanh(0.7978845608028654 * (x + 0.044715 * x * x * x)))


def _lru_kernel(x_ref, y_ref, cw_ref, cb_ref, wa_ref, ba_ref, wx_ref, bx_ref, lam_ref,
                o_ref, tail_ref, carry_ref):
    s = pl.program_id(1)

    @pl.when(s == 0)
    def _():
        tail_ref[...] = jnp.zeros_like(tail_ref)
        carry_ref[...] = jnp.zeros_like(carry_ref)

    x = x_ref[...]
    ts = x.shape[0]
    xext = jnp.concatenate([tail_ref[...], x], axis=0)
    cw = cw_ref[...]
    xc = x * cw[CONV_W - 1:CONV_W, :]
    for j in range(CONV_W - 1):
        sh = CONV_W - 1 - j
        xc = xc + xext[8 - sh:8 - sh + ts, :] * cw[j:j + 1, :]
    xc = xc + cb_ref[...]
    tail_ref[...] = x[ts - 8:, :]

    xb = xc.astype(BF16)
    r = _sigmoid(_dot(xb, wa_ref[...]) + ba_ref[...])
    i = _sigmoid(_dot(xb, wx_ref[...]) + bx_ref[...])
    lam = lam_ref[...]
    nl = -lam
    softplus = jnp.maximum(nl, 0.0) + jnp.log(1.0 + jnp.exp(-jnp.abs(nl)))
    log_a = (-LRU_C) * r * softplus
    a = jnp.exp(log_a)
    mult = jnp.sqrt(1.0 - jnp.exp(2.0 * log_a))
    u = mult * (i * xc)

    C = a.shape[1]
    a = a.reshape(ts // SUBLANES, SUBLANES, C)
    u = u.reshape(ts // SUBLANES, SUBLANES, C)
    sub = lax.broadcasted_iota(jnp.int32, (1, SUBLANES, 1), 1)
    sh = 1
    while sh < SUBLANES:
        a_prev = pltpu.roll(a, sh, axis=1)
        u_prev = pltpu.roll(u, sh, axis=1)
        m = sub >= sh
        u = jnp.where(m, a * u_prev + u, u)
        a = jnp.where(m, a * a_prev, a)
        sh *= 2
    a = a.reshape(ts, C)
    u = u.reshape(ts, C)
    h = carry_ref[0:1, :]
    groups = []
    for g0 in range(0, ts, SUBLANES):
        hg = u[g0:g0 + SUBLANES, :] + a[g0:g0 + SUBLANES, :] * h
        groups.append(hg)
        h = hg[SUBLANES - 1:SUBLANES, :]
    carry_ref[...] = jnp.broadcast_to(h, carry_ref.shape)
    hs = jnp.concatenate(groups, axis=0)
    o_ref[...] = (_gelu_tanh(y_ref[...]) * hs).astype(BF16)


def _lru(xlru, ylru, conv_w, conv_b, wa_d, b_a, wx_d, b_x, lam, B, S, ts):
    T, C = xlru.shape
    ns = S // ts
    row = lambda b, s: (b * ns + s, 0)
    full = lambda b, s: (0, 0)
    return pl.pallas_call(
        _lru_kernel,
        grid=(B, ns),
        in_specs=[
            pl.BlockSpec((ts, C), row),
            pl.BlockSpec((ts, C), row),
            pl.BlockSpec((CONV_W, C), full),
            pl.BlockSpec((1, C), full),
            pl.BlockSpec((C, C), full),
            pl.BlockSpec((1, C), full),
            pl.BlockSpec((C, C), full),
            pl.BlockSpec((1, C), full),
            pl.BlockSpec((1, C), full),
        ],
        out_specs=pl.BlockSpec((ts, C), row),
        out_shape=jax.ShapeDtypeStruct((T, C), BF16),
        scratch_shapes=[pltpu.VMEM((8, C), F32), pltpu.VMEM((8, C), F32)],
        compiler_params=_cparams(("arbitrary", "arbitrary")),
        name="lru",
    )(xlru, ylru, conv_w, conv_b, wa_d, b_a, wx_d, b_x, lam)


NEG_INF = -1e30


def _attn_kernel(q_ref, k_ref, v_ref, o_ref, *state, tq):
    m_refs = state[:N_HEADS]
    acc_refs = state[N_HEADS:]
    qi = pl.program_id(1)
    rowi = lax.broadcasted_iota(jnp.int32, (tq, tq), 0)
    coli = lax.broadcasted_iota(jnp.int32, (tq, tq), 1)
    diag_mask = coli <= rowi

    def head_slice(hh):
        return slice(hh * HEAD_PAD, (hh + 1) * HEAD_PAD)

    def weights(sc, m_b):
        cols = [jnp.exp2(sc[:, c0:c0 + LANES] - m_b) for c0 in range(0, tq, LANES)]
        return jnp.concatenate(cols, axis=1).astype(BF16)

    def scores(hh, r0):
        hs = head_slice(hh)
        return _dot_nt(q_ref[:, hs], k_ref[pl.ds(r0, tq), hs])

    r_diag = pl.multiple_of(qi * tq, tq)
    sc_next = scores(0, r_diag)
    for hh in range(N_HEADS):
        hs = head_slice(hh)
        sc = jnp.where(diag_mask, sc_next, NEG_INF)
        if hh + 1 < N_HEADS:
            sc_next = scores(hh + 1, r_diag)
        m_b = jnp.broadcast_to(jnp.max(sc, axis=-1, keepdims=True), (tq, LANES))
        m_refs[hh][...] = m_b
        acc_refs[hh][...] = _dot(weights(sc, m_b), v_ref[pl.ds(r_diag, tq), hs])

    @pl.loop(0, qi)
    def _(j):
        r0 = pl.multiple_of(j * tq, tq)
        sc_next = scores(0, r0)
        for hh in range(N_HEADS):
            hs = head_slice(hh)
            sc = sc_next
            if hh + 1 < N_HEADS:
                sc_next = scores(hh + 1, r0)
            m_b = m_refs[hh][...]
            m_new = jnp.maximum(m_b, jnp.max(sc, axis=-1, keepdims=True))
            alpha = jnp.exp2(m_b - m_new)
            m_refs[hh][...] = m_new
            acc_refs[hh][...] = (alpha * acc_refs[hh][...]
                                 + _dot(weights(sc, m_new), v_ref[pl.ds(r0, tq), hs]))

    for hh in range(N_HEADS):
        acc = acc_refs[hh][...]
        o = acc[:, :V_HEAD] / acc[:, SUM_LANE:SUM_LANE + 1]
        o_ref[:, hh * V_HEAD:(hh + 1) * V_HEAD] = o.astype(BF16)


def _attn(qp, kp, v, B, S, tq):
    T = qp.shape[0]
    nq = S // tq
    HP = N_HEADS * HEAD_PAD
    HV = N_HEADS * V_HEAD
    return pl.pallas_call(
        functools.partial(_attn_kernel, tq=tq),
        grid=(B, nq),
        in_specs=[
            pl.BlockSpec((tq, HP), lambda b, i: (b * nq + i, 0)),
            pl.BlockSpec((S, HP), lambda b, i: (b, 0)),
            pl.BlockSpec((S, HP), lambda b, i: (b, 0)),
        ],
        out_specs=pl.BlockSpec((tq, HV), lambda b, i: (b * nq + i, 0)),
        out_shape=jax.ShapeDtypeStruct((T, HV), BF16),
        scratch_shapes=([pltpu.VMEM((tq, LANES), F32)] * N_HEADS
                        + [pltpu.VMEM((tq, HEAD_PAD), F32)] * N_HEADS),
        compiler_params=_cparams(("arbitrary", "arbitrary")),
        name="attn",
    )(qp, kp, v)


def _outproj_kernel(lru_ref, att_ref, x_ref, gate_ref, shift_ref, scale_ref, gffn_ref,
                    wo1_ref, wo2_ref, wrh_ref, wrl_ref, br_ref, tri_ref,
                    x1_ref, h2p_ref, idx_ref, gat_ref, rank_ref, cnt_ref, run_ref):
    first = (pl.program_id(0) == 0) & (pl.program_id(1) == 0)

    @pl.when(first)
    def _():
        run_ref[...] = jnp.zeros_like(run_ref)

    mix = _dot(lru_ref[...], wo1_ref[...]) + _dot(att_ref[...], wo2_ref[...])
    x1 = x_ref[...] + gate_ref[0] * mix
    x1_ref[...] = x1
    ms = jnp.mean(x1 * x1, axis=-1, keepdims=True)
    h2 = x1 * lax.rsqrt(ms + EPS) * gffn_ref[...]
    h2 = h2 * (1.0 + scale_ref[0]) + shift_ref[0]

    hhi = h2.astype(BF16)
    hlo = (h2 - hhi.astype(F32)).astype(BF16)
    h2p_ref[...] = _pack_halves(h2)

    wrh = wrh_ref[...]
    logits = _dot_nt(wrh, hhi) + _dot_nt(wrh, hlo) + _dot_nt(wrl_ref[...], hhi) + br_ref[...]
    ne, tm = logits.shape
    eio = lax.broadcasted_iota(jnp.int32, (ne, tm), 0)
    vals, idxs, sels = [], [], []
    l = logits
    for _ in range(TOP_K):
        m = jnp.max(l, axis=0, keepdims=True)
        idx = jnp.min(jnp.where(l == m, eio, ne), axis=0, keepdims=True)
        sel = eio == idx
        l = jnp.where(sel, -jnp.inf, l)
        vals.append(m)
        idxs.append(idx)
        sels.append(sel)
    es = [jnp.exp(v - vals[0]) for v in vals]
    inv = 1.0 / (es[0] + es[1] + es[2] + es[3])
    sel_any = jnp.where(sels[0] | sels[1] | sels[2] | sels[3], 1.0, 0.0)
    run = run_ref[...]
    excl = _dot(sel_any.astype(BF16), tri_ref[...]) + run
    for kk in range(TOP_K):
        idx_ref[kk:kk + 1, :] = idxs[kk]
        gat_ref[kk:kk + 1, :] = es[kk] * inv
        rk = jnp.sum(jnp.where(sels[kk], excl, 0.0), axis=0, keepdims=True)
        rank_ref[kk:kk + 1, :] = rk.astype(jnp.int32)
    run = run + jnp.sum(sel_any, axis=1, keepdims=True)
    run_ref[...] = run
    cnt_ref[...] = run.astype(jnp.int32)


def _outproj(lru_o, att_o, x2, mod3, g_ffn, wo1, wo2, wr_hi, wr_lo, b_r, tri, b0, B, S, tm):
    D = x2.shape[1]
    T = B * S
    ns = S // tm
    C = lru_o.shape[1]
    row_in = lambda b, s: ((b0 + b) * ns + s, 0)
    row = lambda b, s: (b * ns + s, 0)
    col = lambda b, s: (0, b * ns + s)
    full = lambda b, s: (0, 0)
    return pl.pallas_call(
        _outproj_kernel,
        grid=(B, ns),
        in_specs=[
            pl.BlockSpec((tm, C), row_in),
            pl.BlockSpec((tm, C), row_in),
            pl.BlockSpec((tm, D), row_in),
            pl.BlockSpec((1, 1, D), lambda b, s: ((b0 + b) * 6 + 2, 0, 0)),
            pl.BlockSpec((1, 1, D), lambda b, s: ((b0 + b) * 6 + 3, 0, 0)),
            pl.BlockSpec((1, 1, D), lambda b, s: ((b0 + b) * 6 + 4, 0, 0)),
            pl.BlockSpec((1, D), full),
            pl.BlockSpec((C, D), full),
            pl.BlockSpec((C, D), full),
            pl.BlockSpec((N_EXPERTS, D), full),
            pl.BlockSpec((N_EXPERTS, D), full),
            pl.BlockSpec((N_EXPERTS, 1), full),
            pl.BlockSpec(tri.shape, full),
        ],
        out_specs=[
            pl.BlockSpec((tm, D), row),
            pl.BlockSpec((tm, D // 2), row),
            pl.BlockSpec((TOP_K, tm), col),
            pl.BlockSpec((TOP_K, tm), col),
            pl.BlockSpec((TOP_K, tm), col),
            pl.BlockSpec((N_EXPERTS, 1), full),
        ],
        out_shape=[
            jax.ShapeDtypeStruct((T, D), F32),
            jax.ShapeDtypeStruct((T, D // 2), jnp.int32),
            jax.ShapeDtypeStruct((TOP_K, T), jnp.int32),
            jax.ShapeDtypeStruct((TOP_K, T), F32),
            jax.ShapeDtypeStruct((TOP_K, T), jnp.int32),
            jax.ShapeDtypeStruct((N_EXPERTS, 1), jnp.int32),
        ],
        scratch_shapes=[pltpu.VMEM((N_EXPERTS, 1), F32)],
        compiler_params=_cparams(("arbitrary", "arbitrary")),
        name="outproj",
    )(lru_o, att_o, x2, mod3, mod3, mod3, g_ffn, wo1, wo2, wr_hi, wr_lo, b_r, tri)


SC_CORES = 2
SC_SUBCORES = 16
SC_WORKERS = SC_CORES * SC_SUBCORES


def _sc_mesh():
    return plsc.VectorSubcoreMesh(core_axis_name="c", subcore_axis_name="s",
                                  num_cores=SC_CORES, num_subcores=SC_SUBCORES)


def _sc_worker_id():
    return lax.axis_index("s") * SC_CORES + lax.axis_index("c")


def _sc_scatter_rows(rows, idx, n_out, g):
    T, W = rows.shape
    K = idx.shape[0]
    per_w = T // SC_WORKERS
    nch = per_w // g
    assert per_w * SC_WORKERS == T and nch * g == per_w and nch % 2 == 0
    idx_w = idx.reshape(K, SC_WORKERS, nch, g).transpose(1, 2, 0, 3).reshape(SC_WORKERS, nch * K, g)

    def body(rows_hbm, idx_hbm, out_hbm, idx_v, buf0, buf1, semr0, semr1, semw):
        wid = _sc_worker_id()
        base = wid * per_w
        pltpu.sync_copy(idx_hbm.at[wid], idx_v)

        def read(j, buf, sem):
            return pltpu.make_async_copy(rows_hbm.at[pl.ds(base + j * g, g)], buf, sem)

        def scatter(j, buf):
            copies = [pltpu.async_copy(buf, out_hbm.at[idx_v.at[j * K + kk]], semw)
                      for kk in range(K)]
            for cp in copies:
                cp.wait()

        read(0, buf0, semr0).start()

        @pl.loop(0, nch // 2)
        def _(jj):
            j0 = 2 * jj
            read(j0 + 1, buf1, semr1).start()
            read(j0, buf0, semr0).wait()
            scatter(j0, buf0)

            @pl.when(j0 + 2 < nch)
            def _():
                read(j0 + 2, buf0, semr0).start()

            read(j0 + 1, buf1, semr1).wait()
            scatter(j0 + 1, buf1)

    return pl.kernel(
        body,
        out_type=jax.ShapeDtypeStruct((n_out, W), rows.dtype),
        mesh=_sc_mesh(),
        scratch_types=[
            pltpu.VMEM((nch * K, g), jnp.int32),
            pltpu.VMEM((g, W), rows.dtype),
            pltpu.VMEM((g, W), rows.dtype),
            pltpu.SemaphoreType.DMA,
            pltpu.SemaphoreType.DMA,
            pltpu.SemaphoreType.DMA,
        ],
        name="sc_scatter_rows",
    )(rows, idx_w)


def _sc_gather_rows(table, idx, g):
    W = table.shape[1]
    N = idx.shape[0]
    per_w = N // SC_WORKERS
    nch = per_w // g
    assert per_w * SC_WORKERS == N and nch * g == per_w and nch % 2 == 0
    idx_w = idx.reshape(SC_WORKERS, nch, g)

    def body(table_hbm, idx_hbm, out_hbm, idx_v, buf0, buf1, sem0, sem1):
        wid = _sc_worker_id()
        base = wid * per_w
        pltpu.sync_copy(idx_hbm.at[wid], idx_v)

        def gather(j, buf, sem):
            return pltpu.make_async_copy(table_hbm.at[idx_v.at[j]], buf, sem)

        def put(j, buf):
            pltpu.sync_copy(buf, out_hbm.at[pl.ds(base + j * g, g)])

        gather(0, buf0, sem0).start()

        @pl.loop(0, nch // 2)
        def _(jj):
            j0 = 2 * jj
            gather(j0 + 1, buf1, sem1).start()
            gather(j0, buf0, sem0).wait()
            put(j0, buf0)

            @pl.when(j0 + 2 < nch)
            def _():
                gather(j0 + 2, buf0, sem0).start()

            gather(j0 + 1, buf1, sem1).wait()
            put(j0 + 1, buf1)

    return pl.kernel(
        body,
        out_type=jax.ShapeDtypeStruct((N, W), table.dtype),
        mesh=_sc_mesh(),
        scratch_types=[
            pltpu.VMEM((nch, g), jnp.int32),
            pltpu.VMEM((g, W), table.dtype),
            pltpu.VMEM((g, W), table.dtype),
            pltpu.SemaphoreType.DMA,
            pltpu.SemaphoreType.DMA,
        ],
        name="sc_gather_rows",
    )(table, idx_w)


def _experts_kernel(be_ref, bv_ref, bf_ref, bn_ref, bs_ref, xs_ref, w1_hbm, b1_ref, w2_hbm, b2_ref,
                    ys_ref, w1s_ref, w2s_ref, w1b_ref, w2b_ref, sem):
    i = pl.program_id(0)
    nvalid = bv_ref[i]

    def weight_copies(e, slot):
        return (pltpu.make_async_copy(w1_hbm.at[e], w1s_ref.at[slot], sem.at[0, slot]),
                pltpu.make_async_copy(w2_hbm.at[e], w2s_ref.at[slot], sem.at[1, slot]))

    @pl.when(bf_ref[i] > 0)
    def _():
        slot = bs_ref[i]

        @pl.when(i == 0)
        def _():
            for cp in weight_copies(be_ref[0], slot):
                cp.start()

        for cp in weight_copies(be_ref[i], slot):
            cp.wait()
        w1b_ref[...] = w1s_ref[slot].astype(BF16)
        w2b_ref[...] = w2s_ref[slot].astype(BF16)

        @pl.when(bn_ref[i] >= 0)
        def _():
            for cp in weight_copies(bn_ref[i], 1 - slot):
                cp.start()

    @pl.when(nvalid > 0)
    def _():
        xw = xs_ref[...]
        rowi = lax.broadcasted_iota(jnp.int32, (xw.shape[0], 1), 0)
        lo, hi = _unpack_halves(jnp.where(rowi < nvalid, xw, 0))
        half = xw.shape[1]
        gu = (_dot(lo.astype(BF16), w1b_ref[:half, :]) + _dot(hi.astype(BF16), w1b_ref[half:, :])
              + b1_ref[0])
        glu = jnp.minimum(gu[:, :D_FF], SWIGLU_LIMIT)
        lin = jnp.clip(gu[:, D_FF:], -SWIGLU_LIMIT, SWIGLU_LIMIT)
        act = (lin + 1.0) * (glu * _sigmoid(SWIGLU_ALPHA * glu))
        ys_ref[...] = _pack_halves(_dot(act.astype(BF16), w2b_ref[...]) + b2_ref[0])


def _experts(blk_e, blk_v, blk_f, blk_n, blk_s, blk_r, xs, w1, b1, w2, b2):
    P, W = xs.shape
    nb = P // MOE_BLOCK
    E, D, F2 = w1.shape
    grid_spec = pltpu.PrefetchScalarGridSpec(
        num_scalar_prefetch=6,
        grid=(nb,),
        in_specs=[
            pl.BlockSpec((MOE_BLOCK, W), lambda i, be, bv, bf, bn, bs, br: (br[i], 0)),
            pl.BlockSpec(memory_space=pl.ANY),
            pl.BlockSpec((1, 1, F2), lambda i, be, bv, bf, bn, bs, br: (be[i], 0, 0)),
            pl.BlockSpec(memory_space=pl.ANY),
            pl.BlockSpec((1, 1, D), lambda i, be, bv, bf, bn, bs, br: (be[i], 0, 0)),
        ],
        out_specs=pl.BlockSpec((MOE_BLOCK, D // 2), lambda i, be, bv, bf, bn, bs, br: (br[i], 0)),
        scratch_shapes=[
            pltpu.VMEM((2, D, F2), F32),
            pltpu.VMEM((2, D_FF, D), F32),
            pltpu.VMEM((D, F2), BF16),
            pltpu.VMEM((D_FF, D), BF16),
            pltpu.SemaphoreType.DMA((2, 2)),
        ],
    )

    def kern(be_ref, bv_ref, bf_ref, bn_ref, bs_ref, br_ref, *refs):
        del br_ref
        _experts_kernel(be_ref, bv_ref, bf_ref, bn_ref, bs_ref, *refs)

    return pl.pallas_call(
        kern,
        grid_spec=grid_spec,
        out_shape=jax.ShapeDtypeStruct((P, D // 2), jnp.int32),
        compiler_params=_cparams(("arbitrary",)),
        name="experts",
    )(blk_e, blk_v, blk_f, blk_n, blk_s, blk_r, xs, w1, b1.reshape(E, 1, F2), w2,
      b2.reshape(E, 1, D))


def _combine_kernel(x1_ref, g_ref, gate_ref, y0_ref, y1_ref, y2_ref, y3_ref, *rest):
    o_ref = rest[-1]
    g = g_ref[...]
    half = y0_ref.shape[2]
    acc_lo = acc_hi = None
    for kk, y_ref in enumerate((y0_ref, y1_ref, y2_ref, y3_ref)):
        lo, hi = _unpack_halves(y_ref[0])
        gk = g[:, kk:kk + 1]
        acc_lo = lo * gk if acc_lo is None else acc_lo + lo * gk
        acc_hi = hi * gk if acc_hi is None else acc_hi + hi * gk
    gate = gate_ref[0]
    o_ref[:, :half] = x1_ref[:, :half] + gate[:, :half] * acc_lo
    o_ref[:, half:] = x1_ref[:, half:] + gate[:, half:] * acc_hi


def _combine(x1, gates_tk, mod3, ysg, out_prev, b0, B, nb_total, S, tm):
    D = x1.shape[1]
    ns = S // tm
    row = lambda b, s: (b * ns + s, 0)
    row_out = lambda b, s: ((b0 + b) * ns + s, 0)

    def yspec(kk):
        return pl.BlockSpec((1, tm, D // 2), lambda b, s: (kk, b * ns + s, 0))

    in_specs = [
        pl.BlockSpec((tm, D), row),
        pl.BlockSpec((tm, TOP_K), row),
        pl.BlockSpec((1, 1, D), lambda b, s: ((b0 + b) * 6 + 5, 0, 0)),
        yspec(0), yspec(1), yspec(2), yspec(3),
    ]
    args = [x1, gates_tk, mod3, ysg, ysg, ysg, ysg]
    aliases = {}
    if out_prev is not None:
        in_specs.append(pl.BlockSpec(memory_space=pl.ANY))
        args.append(out_prev)
        aliases = {len(args) - 1: 0}
    return pl.pallas_call(
        _combine_kernel,
        grid=(B, ns),
        in_specs=in_specs,
        out_specs=pl.BlockSpec((tm, D), row_out),
        out_shape=jax.ShapeDtypeStruct((nb_total * S, D), F32),
        input_output_aliases=aliases,
        compiler_params=_cparams(("arbitrary", "arbitrary")),
        name="combine",
    )(*args)


def _block_diag(w):
    n, c, d = w.shape
    eye = jnp.eye(n, dtype=w.dtype)
    return jnp.einsum("ncd,nm->ncmd", w, eye).reshape(n * c, n * d)


def _pad_heads(w, width):
    k = w.shape[0]
    w = w.reshape(k, N_HEADS, width)
    return jnp.pad(w, ((0, 0), (0, 0), (0, HEAD_PAD - width))).reshape(k, N_HEADS * HEAD_PAD)


def kernel(x, c, positions, w_ada, b_ada, g_mix, w_in, conv_w, conv_b, w_a, b_a, w_x, b_x, lam,
           g_q_lat, w_uq, g_kv_lat, w_ukv, g_qn, g_kn, w_out, g_ffn, w_router, b_router,
           w1, b1, w2, b2):
    B, S, D = x.shape
    T = B * S
    depth = w_ada.shape[0]
    tm_in = min(512, S)
    ts_lru = min(512, S)
    tq = min(512, S)
    tm_out = min(512, S)
    tm_comb = min(512, S)
    n_groups = 2 if B % 2 == 0 else 1
    Bg = B // n_groups
    Tg = Bg * S
    g_disp = min(64, Tg // SC_WORKERS // 2)
    g_comb = min(64, Tg * TOP_K // SC_WORKERS // 2)

    o1 = 2 * D_LRU
    o2 = o1 + Q_LORA
    o3 = o2 + KV_LORA
    tri = (jnp.arange(tm_out)[:, None] < jnp.arange(tm_out)[None, :]).astype(BF16)
    cos_t, sin_t = _rope_tables(positions)
    lane = jnp.arange(HEAD_PAD, dtype=jnp.int32)
    first = (lane >= ROPE_LO) & (lane < ROPE_LO + ROPE_HALF)
    second = (lane >= ROPE_LO + ROPE_HALF) & (lane < ROPE_LO + QK_ROPE)
    partner = jnp.where(first, lane + ROPE_HALF, jnp.where(second, lane - ROPE_HALF, 0))
    is_rot = first | second

    def rot_cols(w):
        k = w.shape[0]
        w3 = w.reshape(k, -1, HEAD_PAD)
        return jnp.where(is_rot[None, None, :], w3[:, :, partner], 0.0).reshape(w.shape)

    x2 = x.reshape(T, D)
    for l in range(depth):
        mod3 = _ada(c, w_ada[l], b_ada[l]).reshape(B * 6, 1, D)

        w_in_l = w_in[l]
        kr_cols = jnp.pad(w_in_l[:, o3:], ((0, 0), (ROPE_LO, LANES - ROPE_LO - QK_ROPE)))
        w_in_p = jnp.concatenate([w_in_l[:, :o3], kr_cols, rot_cols(kr_cols)], axis=1).astype(BF16)
        w_uq_h = _pad_heads(w_uq[l], QK_HEAD)
        w_uq_p = jnp.concatenate([w_uq_h, rot_cols(w_uq_h)], axis=1).astype(BF16)
        w_ukv_l = w_ukv[l].reshape(KV_LORA, N_HEADS, QK_NOPE + V_HEAD)
        w_uk_h = _pad_heads(w_ukv_l[:, :, :QK_NOPE].reshape(KV_LORA, N_HEADS * QK_NOPE), QK_NOPE)
        w_uv_h = _pad_heads(w_ukv_l[:, :, QK_NOPE:].reshape(KV_LORA, N_HEADS * V_HEAD), V_HEAD)
        w_ukv_p = jnp.concatenate([w_uk_h, w_uv_h], axis=1).astype(BF16)
        gqn_p = jnp.pad(g_qn[l], (0, HEAD_PAD - QK_HEAD)).reshape(1, HEAD_PAD)
        gkn_p = jnp.pad(g_kn[l], (0, HEAD_PAD - QK_HEAD)).reshape(1, HEAD_PAD)

        xlru, ylru, qp, kp, v = _inproj(
            x2, cos_t, sin_t, mod3, g_mix[l].reshape(1, D), w_in_p, g_q_lat[l].reshape(1, Q_LORA),
            w_uq_p, g_kv_lat[l].reshape(1, KV_LORA), w_ukv_p, gqn_p, rot_cols(gqn_p), gkn_p,
            rot_cols(gkn_p), B, S, tm_in)

        lru_o = _lru(xlru, ylru, conv_w[l], conv_b[l].reshape(1, D_LRU),
                     _block_diag(w_a[l]).astype(BF16), b_a[l].reshape(1, D_LRU),
                     _block_diag(w_x[l]).astype(BF16), b_x[l].reshape(1, D_LRU),
                     lam[l].reshape(1, D_LRU), B, S, ts_lru)

        att_o = _attn(qp, kp, v, B, S, tq)

        w_out_b = w_out[l].astype(BF16)
        wr_hi, wr_lo = _split_bf16(w_router[l].T)
        g_ffn_l = g_ffn[l].reshape(1, D)
        b_r = b_router[l].reshape(N_EXPERTS, 1)
        eio = jnp.arange(N_EXPERTS, dtype=jnp.int32)
        n_blocks = -(-(Tg * TOP_K) // MOE_BLOCK) + N_EXPERTS
        bi = jnp.arange(n_blocks, dtype=jnp.int32)

        x_next = None
        for gi in range(n_groups):
            b0 = gi * Bg
            x1, h2p, idx_t, gat_t, rank_t, counts = _outproj(
                lru_o, att_o, x2, mod3, g_ffn_l, w_out_b[:D_LRU], w_out_b[D_LRU:],
                wr_hi, wr_lo, b_r, tri, b0, Bg, S, tm_out)

            counts = counts.reshape(N_EXPERTS)
            nblk_e = (counts + MOE_BLOCK - 1) // MOE_BLOCK
            blk_end = jnp.cumsum(nblk_e)
            pad_start = (blk_end - nblk_e) * MOE_BLOCK
            total = blk_end[-1]
            blk_r = jnp.minimum(bi, total - 1).astype(jnp.int32)
            blk_e = jnp.minimum(jnp.sum(blk_end[None, :] <= blk_r[:, None], axis=1),
                                N_EXPERTS - 1).astype(jnp.int32)
            blk_onehot = blk_e[:, None] == eio[None, :]
            blk_first = jnp.sum(jnp.where(blk_onehot, (blk_end - nblk_e)[None, :], 0), axis=1)
            blk_cnt = jnp.sum(jnp.where(blk_onehot, counts[None, :], 0), axis=1)
            blk_v = jnp.where(bi < total,
                              jnp.clip(blk_cnt - (bi - blk_first) * MOE_BLOCK, 0, MOE_BLOCK),
                              0).astype(jnp.int32)
            blk_f = ((bi == blk_first) & (bi < total)).astype(jnp.int32)
            nxt_first = jnp.sum(jnp.where(blk_onehot, blk_end[None, :], 0), axis=1)
            nxt_e = jnp.minimum(jnp.sum(blk_end[None, :] <= nxt_first[:, None], axis=1),
                                N_EXPERTS - 1)
            blk_n = jnp.where(nxt_first < total, nxt_e, -1).astype(jnp.int32)
            ordinal = jnp.cumsum((nblk_e > 0).astype(jnp.int32)) - 1
            blk_s = (jnp.sum(jnp.where(blk_onehot, ordinal[None, :], 0), axis=1) % 2).astype(jnp.int32)
            slot0 = jnp.sum(jnp.where(idx_t[None] == eio[:, None, None],
                                      pad_start[:, None, None], 0), axis=0)
            dest = slot0.astype(jnp.int32) + rank_t

            xs = _sc_scatter_rows(h2p, dest, n_blocks * MOE_BLOCK, g_disp)
            ys = _experts(blk_e, blk_v, blk_f, blk_n, blk_s, blk_r, xs, w1[l], b1[l], w2[l], b2[l])
            ysg = _sc_gather_rows(ys, dest.reshape(TOP_K * Tg), g_comb).reshape(TOP_K, Tg, D // 2)
            x_next = _combine(x1, gat_t.T, mod3, ysg, x_next, b0, Bg, B, S, tm_comb)
        x2 = x_next
    return x2.reshape(B, S, D)
```

```python
import functools

import jax
import jax.numpy as jnp
from jax import lax
from jax.experimental import pallas as pl
from jax.experimental.pallas import tpu as pltpu
from jax.experimental.pallas import tpu_sc as plsc

D_MODEL = 1024
D_LRU = 512
LRU_BLOCKS = 8
LRU_BD = 64
CONV_W = 4
LRU_C = 8.0
N_HEADS = 8
QK_NOPE = 64
QK_ROPE = 32
QK_HEAD = 96
V_HEAD = 64
Q_LORA = 256
KV_LORA = 128
ROPE_THETA = 10000.0
N_EXPERTS = 32
TOP_K = 4
D_FF = 1024
SWIGLU_LIMIT = 7.0
SWIGLU_ALPHA = 1.702
MOE_BLOCK = 512
EPS = 1e-6

LANES = 128
SUBLANES = 8
HEAD_PAD = 128
ROPE_LO = QK_NOPE
ROPE_HALF = QK_ROPE // 2
TOK_PER_ROW = LANES // ROPE_HALF
D_IN_PAD = 2 * D_LRU + Q_LORA + KV_LORA + 2 * LANES
LOG2_E = 1.4426950408889634
SUM_LANE = V_HEAD
MAX_LANE = V_HEAD + 1

VMEM_LIMIT = 56 * 1024 * 1024

F32 = jnp.float32
BF16 = jnp.bfloat16


def _cparams(sem):
    return pltpu.CompilerParams(dimension_semantics=sem, vmem_limit_bytes=VMEM_LIMIT)


def _dot(a, b):
    return jnp.dot(a, b, preferred_element_type=F32)


def _dot_nt(a, b):
    return lax.dot_general(a, b, (((1,), (1,)), ((), ())), preferred_element_type=F32)


def _split_bf16(a):
    hi = a.astype(BF16)
    lo = (a - hi.astype(F32)).astype(BF16)
    return hi, lo


def _sigmoid(x):
    return 1.0 / (1.0 + jnp.exp(-x))


def _pack_halves(x):
    bits = lax.bitcast_convert_type(x.astype(BF16).astype(F32), jnp.uint32)
    half = x.shape[1] // 2
    words = (bits[:, :half] >> 16) | (bits[:, half:] & jnp.uint32(0xFFFF0000))
    return lax.bitcast_convert_type(words, jnp.int32)


def _unpack_halves(words):
    w = lax.bitcast_convert_type(words, jnp.uint32)
    lo = lax.bitcast_convert_type(w << 16, F32)
    hi = lax.bitcast_convert_type(w & jnp.uint32(0xFFFF0000), F32)
    return lo, hi


def _ada_kernel(c_ref, w_ref, b_ref, o_ref):
    c = c_ref[...]
    s = c * _sigmoid(c)
    shi, slo = _split_bf16(s)
    whi, wlo = _split_bf16(w_ref[...])
    o_ref[...] = _dot(shi, whi) + _dot(slo, whi) + _dot(shi, wlo) + b_ref[...]


def _ada(c, w_ada, b_ada):
    B, D = c.shape
    N = w_ada.shape[1]
    tn = 1024
    return pl.pallas_call(
        _ada_kernel,
        grid=(N // tn,),
        in_specs=[
            pl.BlockSpec((B, D), lambda j: (0, 0)),
            pl.BlockSpec((D, tn), lambda j: (0, j)),
            pl.BlockSpec((1, tn), lambda j: (0, j)),
        ],
        out_specs=pl.BlockSpec((B, tn), lambda j: (0, j)),
        out_shape=jax.ShapeDtypeStruct((B, N), F32),
        compiler_params=_cparams(("arbitrary",)),
        name="ada",
    )(c, w_ada, b_ada.reshape(1, N))


def _trig_kernel(pos_ref, freq_ref, rsel_ref, fold_ref, cbase_ref, cos_ref, sin_ref):
    ang = pos_ref[...].astype(F32) * freq_ref[...]
    cs = jnp.concatenate([jnp.cos(ang), jnp.sin(ang)], axis=1)
    tm = cos_ref.shape[0]
    row = lax.broadcasted_iota(jnp.int32, (tm, 2 * LANES), 0)
    lane = lax.broadcasted_iota(jnp.int32, (tm, 2 * LANES), 1)
    own = ((lane % LANES) // ROPE_HALF) == (row % TOK_PER_ROW)
    rsel = rsel_ref[...]
    fold = fold_ref[...]
    by_row = sum(_dot(rsel, part) for part in _split_bf16(cs))
    mine = jnp.where(own, by_row, 0.0)
    out = sum(_dot(part, fold) for part in _split_bf16(mine))
    cos_ref[...] = out[:, :LANES] + cbase_ref[...]
    sin_ref[...] = out[:, LANES:]


def _rope_tables(positions):
    T = positions.size
    freqs = ROPE_THETA ** (-jnp.arange(ROPE_HALF, dtype=F32) / ROPE_HALF)
    rows = T // TOK_PER_ROW
    pos_c = jnp.repeat(positions.reshape(T).astype(jnp.int32), ROPE_HALF).reshape(rows, LANES)
    freq_c = jnp.tile(freqs, TOK_PER_ROW).reshape(1, LANES)
    tm = min(2048, T)
    tr = tm // TOK_PER_ROW
    rsel = (jnp.arange(tm)[:, None] // TOK_PER_ROW == jnp.arange(tr)[None, :]).astype(BF16)
    src = jnp.arange(LANES)[:, None] % ROPE_HALF
    dst = jnp.arange(LANES)[None, :]
    first = dst == ROPE_LO + src
    second = dst == ROPE_LO + ROPE_HALF + src
    fcos = (first | second).astype(F32)
    fsin = second.astype(F32) - first.astype(F32)
    zero = jnp.zeros((LANES, LANES), F32)
    fold = jnp.block([[fcos, zero], [zero, fsin]]).astype(BF16)
    lane = jnp.arange(LANES)
    cbase = ((lane < ROPE_LO) | (lane >= ROPE_LO + QK_ROPE)).astype(F32).reshape(1, LANES)
    full = lambda i: (0, 0)
    return pl.pallas_call(
        _trig_kernel,
        grid=(T // tm,),
        in_specs=[
            pl.BlockSpec((tr, LANES), lambda i: (i, 0)),
            pl.BlockSpec((1, LANES), full),
            pl.BlockSpec((tm, tr), full),
            pl.BlockSpec((2 * LANES, 2 * LANES), full),
            pl.BlockSpec((1, LANES), full),
        ],
        out_specs=[pl.BlockSpec((tm, LANES), lambda i: (i, 0))] * 2,
        out_shape=[jax.ShapeDtypeStruct((T, LANES), F32)] * 2,
        compiler_params=_cparams(("arbitrary",)),
        name="rope_trig",
    )(pos_c, freq_c, rsel, fold, cbase)


def _inproj_kernel(x_ref, cos_ref, sin_ref, shift_ref, scale_ref, gmix_ref, win_ref, gq_ref, wuq_ref,
                   gkv_ref, wukv_ref, gqn_ref, gqr_ref, gkn_ref, gkr_ref,
                   cw_ref, cb_ref, wa_ref, ba_ref, wx_ref, bx_ref, lam_ref,
                   lru_ref, q_ref, k_ref, v_ref, tail_ref, carry_ref):
    HP = N_HEADS * HEAD_PAD

    @pl.when(pl.program_id(1) == 0)
    def _():
        tail_ref[...] = jnp.zeros_like(tail_ref)
        carry_ref[...] = jnp.zeros_like(carry_ref)

    x = x_ref[...]
    ms = jnp.mean(x * x, axis=-1, keepdims=True)
    xn = x * lax.rsqrt(ms + EPS) * gmix_ref[...]
    h = xn * (1.0 + scale_ref[0]) + shift_ref[0]
    hb = h.astype(BF16)
    o1 = 2 * D_LRU
    o2 = Q_LORA
    o3 = o2 + KV_LORA
    z_lru = _dot(hb, win_ref[:, :o1])
    z = _dot(hb, win_ref[:, o1:])
    lru_ref[...] = _lru_tile(z_lru[:, :D_LRU], z_lru[:, D_LRU:], cw_ref, cb_ref, wa_ref, ba_ref,
                             wx_ref, bx_ref, lam_ref, tail_ref, carry_ref)
    ql = z[:, :o2]
    kvl = z[:, o2:o3]
    kr = z[:, o3:o3 + LANES]
    kr_rot = z[:, o3 + LANES:]

    qn = ql * lax.rsqrt(jnp.mean(ql * ql, axis=-1, keepdims=True) + EPS) * gq_ref[...]
    qq = _dot(qn.astype(BF16), wuq_ref[...])
    kvn = kvl * lax.rsqrt(jnp.mean(kvl * kvl, axis=-1, keepdims=True) + EPS) * gkv_ref[...]
    kv = _dot(kvn.astype(BF16), wukv_ref[...])

    tm = x.shape[0]
    lane = lax.broadcasted_iota(jnp.int32, (tm, HP), 1)
    v_ref[...] = jnp.where((lane & (HEAD_PAD - 1)) == V_HEAD, 1.0, kv[:, HP:]).astype(BF16)

    cos_t = cos_ref[...]
    sin_t = sin_ref[...]
    gqn = gqn_ref[...]
    gkn = gkn_ref[...]
    cq = gqn * cos_t
    sq = gqr_ref[...] * sin_t
    kb = kr * (gkn * cos_t) + kr_rot * (gkr_ref[...] * sin_t)
    inv_w = 1.0 / QK_HEAD
    qscale = QK_HEAD ** -0.5 * LOG2_E
    for hh in range(N_HEADS):
        sl = slice(hh * HEAD_PAD, (hh + 1) * HEAD_PAD)
        qh = qq[:, sl]
        rq = lax.rsqrt(jnp.sum(qh * qh, axis=-1, keepdims=True) * inv_w + EPS) * qscale
        q_ref[:, sl] = ((qh * cq + qq[:, HP + hh * HEAD_PAD:HP + (hh + 1) * HEAD_PAD] * sq) * rq).astype(BF16)
        kraw = kv[:, sl] + kr
        rk = lax.rsqrt(jnp.sum(kraw * kraw, axis=-1, keepdims=True) * inv_w + EPS)
        k_ref[:, sl] = ((kv[:, sl] * gkn + kb) * rk).astype(BF16)


def _inproj(x2, cos_t, sin_t, mod3, g_mix, w_in_p, g_q_lat, w_uq_p, g_kv_lat, w_ukv_p,
            gqn_p, gqr_p, gkn_p, gkr_p, conv_w, conv_b, wa_d, b_a, wx_d, b_x, lam, B, S, tm):
    T, D = x2.shape
    ns = S // tm
    HP = N_HEADS * HEAD_PAD
    C = D_LRU
    row = lambda b, s: (b * ns + s, 0)
    full = lambda b, s: (0, 0)
    return pl.pallas_call(
        _inproj_kernel,
        grid=(B, ns),
        in_specs=[
            pl.BlockSpec((tm, D), row),
            pl.BlockSpec((tm, LANES), row),
            pl.BlockSpec((tm, LANES), row),
            pl.BlockSpec((1, 1, D), lambda b, s: (b * 6 + 0, 0, 0)),
            pl.BlockSpec((1, 1, D), lambda b, s: (b * 6 + 1, 0, 0)),
            pl.BlockSpec((1, D), full),
            pl.BlockSpec((D, D_IN_PAD), full),
            pl.BlockSpec((1, Q_LORA), full),
            pl.BlockSpec((Q_LORA, 2 * HP), full),
            pl.BlockSpec((1, KV_LORA), full),
            pl.BlockSpec((KV_LORA, 2 * HP), full),
            pl.BlockSpec((1, HEAD_PAD), full),
            pl.BlockSpec((1, HEAD_PAD), full),
            pl.BlockSpec((1, HEAD_PAD), full),
            pl.BlockSpec((1, HEAD_PAD), full),
            pl.BlockSpec((CONV_W, C), full),
            pl.BlockSpec((1, C), full),
            pl.BlockSpec((C, C), full),
            pl.BlockSpec((1, C), full),
            pl.BlockSpec((C, C), full),
            pl.BlockSpec((1, C), full),
            pl.BlockSpec((1, C), full),
        ],
        out_specs=[
            pl.BlockSpec((tm, C), row),
            pl.BlockSpec((tm, HP), row),
            pl.BlockSpec((tm, HP), row),
            pl.BlockSpec((tm, HP), row),
        ],
        out_shape=[
            jax.ShapeDtypeStruct((T, C), BF16),
            jax.ShapeDtypeStruct((T, HP), BF16),
            jax.ShapeDtypeStruct((T, HP), BF16),
            jax.ShapeDtypeStruct((T, HP), BF16),
        ],
        scratch_shapes=[pltpu.VMEM((SUBLANES, C), F32), pltpu.VMEM((SUBLANES, C), F32)],
        compiler_params=_cparams(("arbitrary", "arbitrary")),
        name="inproj",
    )(x2, cos_t, sin_t, mod3, mod3, g_mix, w_in_p, g_q_lat, w_uq_p, g_kv_lat, w_ukv_p,
      gqn_p, gqr_p, gkn_p, gkr_p, conv_w, conv_b, wa_d, b_a, wx_d, b_x, lam)


def _gelu_tanh(x):
    return 0.5 * x * (1.0 + jnp.tanh(0.7978845608028654 * (x + 0.044715 * x * x * x)))


def _lru_tile(x, y, cw_ref, cb_ref, wa_ref, ba_ref, wx_ref, bx_ref, lam_ref, tail_ref, carry_ref):
    ts = x.shape[0]
    xext = jnp.concatenate([tail_ref[...], x], axis=0)
    cw = cw_ref[...]
    xc = x * cw[CONV_W - 1:CONV_W, :]
    for j in range(CONV_W - 1):
        sh = CONV_W - 1 - j
        xc = xc + xext[8 - sh:8 - sh + ts, :] * cw[j:j + 1, :]
    xc = xc + cb_ref[...]
    tail_ref[...] = x[ts - 8:, :]

    xb = xc.astype(BF16)
    r = _sigmoid(_dot(xb, wa_ref[...]) + ba_ref[...])
    i = _sigmoid(_dot(xb, wx_ref[...]) + bx_ref[...])
    lam = lam_ref[...]
    nl = -lam
    softplus = jnp.maximum(nl, 0.0) + jnp.log(1.0 + jnp.exp(-jnp.abs(nl)))
    log_a = (-LRU_C) * r * softplus
    a = jnp.exp(log_a)
    mult = jnp.sqrt(1.0 - jnp.exp(2.0 * log_a))
    u = mult * (i * xc)

    C = a.shape[1]
    a = a.reshape(ts // SUBLANES, SUBLANES, C)
    u = u.reshape(ts // SUBLANES, SUBLANES, C)
    sub = lax.broadcasted_iota(jnp.int32, (1, SUBLANES, 1), 1)
    sh = 1
    while sh < SUBLANES:
        a_prev = pltpu.roll(a, sh, axis=1)
        u_prev = pltpu.roll(u, sh, axis=1)
        m = sub >= sh
        u = jnp.where(m, a * u_prev + u, u)
        a = jnp.where(m, a * a_prev, a)
        sh *= 2
    a = a.reshape(ts, C)
    u = u.reshape(ts, C)
    h = carry_ref[0:1, :]
    groups = []
    for g0 in range(0, ts, SUBLANES):
        hg = u[g0:g0 + SUBLANES, :] + a[g0:g0 + SUBLANES, :] * h
        groups.append(hg)
        h = hg[SUBLANES - 1:SUBLANES, :]
    carry_ref[...] = jnp.broadcast_to(h, carry_ref.shape)
    hs = jnp.concatenate(groups, axis=0)
    return (_gelu_tanh(y) * hs).astype(BF16)


NEG_INF = -1e30


def _attn_kernel(q_ref, k_ref, v_ref, o_ref, *state, tq):
    m_refs = state[:N_HEADS]
    acc_refs = state[N_HEADS:]
    qi = pl.program_id(1)
    rowi = lax.broadcasted_iota(jnp.int32, (tq, tq), 0)
    coli = lax.broadcasted_iota(jnp.int32, (tq, tq), 1)
    diag_mask = coli <= rowi

    def head_slice(hh):
        return slice(hh * HEAD_PAD, (hh + 1) * HEAD_PAD)

    def weights(sc, m_b):
        cols = [jnp.exp2(sc[:, c0:c0 + LANES] - m_b) for c0 in range(0, tq, LANES)]
        return jnp.concatenate(cols, axis=1).astype(BF16)

    def scores(hh, r0):
        hs = head_slice(hh)
        return _dot_nt(q_ref[:, hs], k_ref[pl.ds(r0, tq), hs])

    r_diag = pl.multiple_of(qi * tq, tq)
    sc_next = scores(0, r_diag)
    for hh in range(N_HEADS):
        hs = head_slice(hh)
        sc = jnp.where(diag_mask, sc_next, NEG_INF)
        if hh + 1 < N_HEADS:
            sc_next = scores(hh + 1, r_diag)
        m_b = jnp.broadcast_to(jnp.max(sc, axis=-1, keepdims=True), (tq, LANES))
        m_refs[hh][...] = m_b
        acc_refs[hh][...] = _dot(weights(sc, m_b), v_ref[pl.ds(r_diag, tq), hs])

    @pl.loop(0, qi)
    def _(j):
        r0 = pl.multiple_of(j * tq, tq)
        sc_next = scores(0, r0)
        for hh in range(N_HEADS):
            hs = head_slice(hh)
            sc = sc_next
            if hh + 1 < N_HEADS:
                sc_next = scores(hh + 1, r0)
            m_b = m_refs[hh][...]
            m_new = jnp.maximum(m_b, jnp.max(sc, axis=-1, keepdims=True))
            alpha = jnp.exp2(m_b - m_new)
            m_refs[hh][...] = m_new
            acc_refs[hh][...] = (alpha * acc_refs[hh][...]
                                 + _dot(weights(sc, m_new), v_ref[pl.ds(r0, tq), hs]))

    for hh in range(N_HEADS):
        acc = acc_refs[hh][...]
        o = acc[:, :V_HEAD] / acc[:, SUM_LANE:SUM_LANE + 1]
        o_ref[:, hh * V_HEAD:(hh + 1) * V_HEAD] = o.astype(BF16)


def _attn(qp, kp, v, B, S, tq):
    T = qp.shape[0]
    nq = S // tq
    HP = N_HEADS * HEAD_PAD
    HV = N_HEADS * V_HEAD
    return pl.pallas_call(
        functools.partial(_attn_kernel, tq=tq),
        grid=(B, nq),
        in_specs=[
            pl.BlockSpec((tq, HP), lambda b, i: (b * nq + i, 0)),
            pl.BlockSpec((S, HP), lambda b, i: (b, 0)),
            pl.BlockSpec((S, HP), lambda b, i: (b, 0)),
        ],
        out_specs=pl.BlockSpec((tq, HV), lambda b, i: (b * nq + i, 0)),
        out_shape=jax.ShapeDtypeStruct((T, HV), BF16),
        scratch_shapes=([pltpu.VMEM((tq, LANES), F32)] * N_HEADS
                        + [pltpu.VMEM((tq, HEAD_PAD), F32)] * N_HEADS),
        compiler_params=_cparams(("arbitrary", "arbitrary")),
        name="attn",
    )(qp, kp, v)


def _outproj_kernel(lru_ref, att_ref, x_ref, gate_ref, shift_ref, scale_ref, gffn_ref,
                    wo1_ref, wo2_ref, wrh_ref, wrl_ref, br_ref, tri_ref,
                    x1_ref, h2p_ref, idx_ref, gat_ref, rank_ref, cnt_ref, run_ref):
    first = (pl.program_id(0) == 0) & (pl.program_id(1) == 0)

    @pl.when(first)
    def _():
        run_ref[...] = jnp.zeros_like(run_ref)

    mix = _dot(lru_ref[...], wo1_ref[...]) + _dot(att_ref[...], wo2_ref[...])
    x1 = x_ref[...] + gate_ref[0] * mix
    x1_ref[...] = x1
    ms = jnp.mean(x1 * x1, axis=-1, keepdims=True)
    h2 = x1 * lax.rsqrt(ms + EPS) * gffn_ref[...]
    h2 = h2 * (1.0 + scale_ref[0]) + shift_ref[0]

    hhi = h2.astype(BF16)
    hlo = (h2 - hhi.astype(F32)).astype(BF16)
    h2p_ref[...] = _pack_halves(h2)

    wrh = wrh_ref[...]
    logits = _dot_nt(wrh, hhi) + _dot_nt(wrh, hlo) + _dot_nt(wrl_ref[...], hhi) + br_ref[...]
    ne, tm = logits.shape
    eio = lax.broadcasted_iota(jnp.int32, (ne, tm), 0)
    vals, idxs, sels = [], [], []
    l = logits
    for _ in range(TOP_K):
        m = jnp.max(l, axis=0, keepdims=True)
        idx = jnp.min(jnp.where(l == m, eio, ne), axis=0, keepdims=True)
        sel = eio == idx
        l = jnp.where(sel, -jnp.inf, l)
        vals.append(m)
        idxs.append(idx)
        sels.append(sel)
    es = [jnp.exp(v - vals[0]) for v in vals]
    inv = 1.0 / (es[0] + es[1] + es[2] + es[3])
    sel_any = jnp.where(sels[0] | sels[1] | sels[2] | sels[3], 1.0, 0.0)
    run = run_ref[...]
    excl = _dot(sel_any.astype(BF16), tri_ref[...]) + run
    for kk in range(TOP_K):
        idx_ref[kk:kk + 1, :] = idxs[kk]
        gat_ref[kk:kk + 1, :] = es[kk] * inv
        rk = jnp.sum(jnp.where(sels[kk], excl, 0.0), axis=0, keepdims=True)
        rank_ref[kk:kk + 1, :] = rk.astype(jnp.int32)
    run = run + jnp.sum(sel_any, axis=1, keepdims=True)
    run_ref[...] = run
    cnt_ref[...] = run.astype(jnp.int32)


def _outproj(lru_o, att_o, x2, mod3, g_ffn, wo1, wo2, wr_hi, wr_lo, b_r, tri, b0, B, S, tm):
    D = x2.shape[1]
    T = B * S
    ns = S // tm
    C = lru_o.shape[1]
    row_in = lambda b, s: ((b0 + b) * ns + s, 0)
    row = lambda b, s: (b * ns + s, 0)
    col = lambda b, s: (0, b * ns + s)
    full = lambda b, s: (0, 0)
    return pl.pallas_call(
        _outproj_kernel,
        grid=(B, ns),
        in_specs=[
            pl.BlockSpec((tm, C), row_in),
            pl.BlockSpec((tm, C), row_in),
            pl.BlockSpec((tm, D), row_in),
            pl.BlockSpec((1, 1, D), lambda b, s: ((b0 + b) * 6 + 2, 0, 0)),
            pl.BlockSpec((1, 1, D), lambda b, s: ((b0 + b) * 6 + 3, 0, 0)),
            pl.BlockSpec((1, 1, D), lambda b, s: ((b0 + b) * 6 + 4, 0, 0)),
            pl.BlockSpec((1, D), full),
            pl.BlockSpec((C, D), full),
            pl.BlockSpec((C, D), full),
            pl.BlockSpec((N_EXPERTS, D), full),
            pl.BlockSpec((N_EXPERTS, D), full),
            pl.BlockSpec((N_EXPERTS, 1), full),
            pl.BlockSpec(tri.shape, full),
        ],
        out_specs=[
            pl.BlockSpec((tm, D), row),
            pl.BlockSpec((tm, D // 2), row),
            pl.BlockSpec((TOP_K, tm), col),
            pl.BlockSpec((TOP_K, tm), col),
            pl.BlockSpec((TOP_K, tm), col),
            pl.BlockSpec((N_EXPERTS, 1), full),
        ],
        out_shape=[
            jax.ShapeDtypeStruct((T, D), F32),
            jax.ShapeDtypeStruct((T, D // 2), jnp.int32),
            jax.ShapeDtypeStruct((TOP_K, T), jnp.int32),
            jax.ShapeDtypeStruct((TOP_K, T), F32),
            jax.ShapeDtypeStruct((TOP_K, T), jnp.int32),
            jax.ShapeDtypeStruct((N_EXPERTS, 1), jnp.int32),
        ],
        scratch_shapes=[pltpu.VMEM((N_EXPERTS, 1), F32)],
        compiler_params=_cparams(("arbitrary", "arbitrary")),
        name="outproj",
    )(lru_o, att_o, x2, mod3, mod3, mod3, g_ffn, wo1, wo2, wr_hi, wr_lo, b_r, tri)


SC_CORES = 2
SC_SUBCORES = 16
SC_WORKERS = SC_CORES * SC_SUBCORES


def _sc_mesh():
    return plsc.VectorSubcoreMesh(core_axis_name="c", subcore_axis_name="s",
                                  num_cores=SC_CORES, num_subcores=SC_SUBCORES)


def _sc_worker_id():
    return lax.axis_index("s") * SC_CORES + lax.axis_index("c")


def _sc_scatter_rows(rows, idx, n_out, g):
    T, W = rows.shape
    K = idx.shape[0]
    per_w = T // SC_WORKERS
    nch = per_w // g
    assert per_w * SC_WORKERS == T and nch * g == per_w and nch % 2 == 0
    idx_w = idx.reshape(K, SC_WORKERS, nch, g).transpose(1, 2, 0, 3).reshape(SC_WORKERS, nch * K, g)

    def body(rows_hbm, idx_hbm, out_hbm, idx_v, buf0, buf1, semr0, semr1, semw):
        wid = _sc_worker_id()
        base = wid * per_w
        pltpu.sync_copy(idx_hbm.at[wid], idx_v)

        def read(j, buf, sem):
            return pltpu.make_async_copy(rows_hbm.at[pl.ds(base + j * g, g)], buf, sem)

        def scatter(j, buf):
            copies = [pltpu.async_copy(buf, out_hbm.at[idx_v.at[j * K + kk]], semw)
                      for kk in range(K)]
            for cp in copies:
                cp.wait()

        read(0, buf0, semr0).start()

        @pl.loop(0, nch // 2)
        def _(jj):
            j0 = 2 * jj
            read(j0 + 1, buf1, semr1).start()
            read(j0, buf0, semr0).wait()
            scatter(j0, buf0)

            @pl.when(j0 + 2 < nch)
            def _():
                read(j0 + 2, buf0, semr0).start()

            read(j0 + 1, buf1, semr1).wait()
            scatter(j0 + 1, buf1)

    return pl.kernel(
        body,
        out_type=jax.ShapeDtypeStruct((n_out, W), rows.dtype),
        mesh=_sc_mesh(),
        scratch_types=[
            pltpu.VMEM((nch * K, g), jnp.int32),
            pltpu.VMEM((g, W), rows.dtype),
            pltpu.VMEM((g, W), rows.dtype),
            pltpu.SemaphoreType.DMA,
            pltpu.SemaphoreType.DMA,
            pltpu.SemaphoreType.DMA,
        ],
        name="sc_scatter_rows",
    )(rows, idx_w)


def _sc_gather_rows(table, idx, g):
    W = table.shape[1]
    N = idx.shape[0]
    per_w = N // SC_WORKERS
    nch = per_w // g
    assert per_w * SC_WORKERS == N and nch * g == per_w and nch % 2 == 0
    idx_w = idx.reshape(SC_WORKERS, nch, g)

    def body(table_hbm, idx_hbm, out_hbm, idx_v, buf0, buf1, sem0, sem1):
        wid = _sc_worker_id()
        base = wid * per_w
        pltpu.sync_copy(idx_hbm.at[wid], idx_v)

        def gather(j, buf, sem):
            return pltpu.make_async_copy(table_hbm.at[idx_v.at[j]], buf, sem)

        def put(j, buf):
            pltpu.sync_copy(buf, out_hbm.at[pl.ds(base + j * g, g)])

        gather(0, buf0, sem0).start()

        @pl.loop(0, nch // 2)
        def _(jj):
            j0 = 2 * jj
            gather(j0 + 1, buf1, sem1).start()
            gather(j0, buf0, sem0).wait()
            put(j0, buf0)

            @pl.when(j0 + 2 < nch)
            def _():
                gather(j0 + 2, buf0, sem0).start()

            gather(j0 + 1, buf1, sem1).wait()
            put(j0 + 1, buf1)

    return pl.kernel(
        body,
        out_type=jax.ShapeDtypeStruct((N, W), table.dtype),
        mesh=_sc_mesh(),
        scratch_types=[
            pltpu.VMEM((nch, g), jnp.int32),
            pltpu.VMEM((g, W), table.dtype),
            pltpu.VMEM((g, W), table.dtype),
            pltpu.SemaphoreType.DMA,
            pltpu.SemaphoreType.DMA,
        ],
        name="sc_gather_rows",
    )(table, idx_w)


def _experts_kernel(be_ref, bv_ref, bf_ref, bn_ref, bs_ref, xs_ref, w1_hbm, b1_ref, w2_hbm, b2_ref,
                    ys_ref, w1s_ref, w2s_ref, w1b_ref, w2b_ref, sem):
    i = pl.program_id(0)
    nvalid = bv_ref[i]

    def weight_copies(e, slot):
        return (pltpu.make_async_copy(w1_hbm.at[e], w1s_ref.at[slot], sem.at[0, slot]),
                pltpu.make_async_copy(w2_hbm.at[e], w2s_ref.at[slot], sem.at[1, slot]))

    @pl.when(bf_ref[i] > 0)
    def _():
        slot = bs_ref[i]

        @pl.when(i == 0)
        def _():
            for cp in weight_copies(be_ref[0], slot):
                cp.start()

        for cp in weight_copies(be_ref[i], slot):
            cp.wait()
        w1b_ref[...] = w1s_ref[slot].astype(BF16)
        w2b_ref[...] = w2s_ref[slot].astype(BF16)

        @pl.when(bn_ref[i] >= 0)
        def _():
            for cp in weight_copies(bn_ref[i], 1 - slot):
                cp.start()

    @pl.when(nvalid > 0)
    def _():
        xw = xs_ref[...]
        rowi = lax.broadcasted_iota(jnp.int32, (xw.shape[0], 1), 0)
        lo, hi = _unpack_halves(jnp.where(rowi < nvalid, xw, 0))
        half = xw.shape[1]
        gu = (_dot(lo.astype(BF16), w1b_ref[:half, :]) + _dot(hi.astype(BF16), w1b_ref[half:, :])
              + b1_ref[0])
        glu = jnp.minimum(gu[:, :D_FF], SWIGLU_LIMIT)
        lin = jnp.clip(gu[:, D_FF:], -SWIGLU_LIMIT, SWIGLU_LIMIT)
        act = (lin + 1.0) * (glu * _sigmoid(SWIGLU_ALPHA * glu))
        ys_ref[...] = _pack_halves(_dot(act.astype(BF16), w2b_ref[...]) + b2_ref[0])


def _experts(blk_e, blk_v, blk_f, blk_n, blk_s, blk_r, xs, w1, b1, w2, b2):
    P, W = xs.shape
    nb = P // MOE_BLOCK
    E, D, F2 = w1.shape
    grid_spec = pltpu.PrefetchScalarGridSpec(
        num_scalar_prefetch=6,
        grid=(nb,),
        in_specs=[
            pl.BlockSpec((MOE_BLOCK, W), lambda i, be, bv, bf, bn, bs, br: (br[i], 0)),
            pl.BlockSpec(memory_space=pl.ANY),
            pl.BlockSpec((1, 1, F2), lambda i, be, bv, bf, bn, bs, br: (be[i], 0, 0)),
            pl.BlockSpec(memory_space=pl.ANY),
            pl.BlockSpec((1, 1, D), lambda i, be, bv, bf, bn, bs, br: (be[i], 0, 0)),
        ],
        out_specs=pl.BlockSpec((MOE_BLOCK, D // 2), lambda i, be, bv, bf, bn, bs, br: (br[i], 0)),
        scratch_shapes=[
            pltpu.VMEM((2, D, F2), F32),
            pltpu.VMEM((2, D_FF, D), F32),
            pltpu.VMEM((D, F2), BF16),
            pltpu.VMEM((D_FF, D), BF16),
            pltpu.SemaphoreType.DMA((2, 2)),
        ],
    )

    def kern(be_ref, bv_ref, bf_ref, bn_ref, bs_ref, br_ref, *refs):
        del br_ref
        _experts_kernel(be_ref, bv_ref, bf_ref, bn_ref, bs_ref, *refs)

    return pl.pallas_call(
        kern,
        grid_spec=grid_spec,
        out_shape=jax.ShapeDtypeStruct((P, D // 2), jnp.int32),
        compiler_params=_cparams(("arbitrary",)),
        name="experts",
    )(blk_e, blk_v, blk_f, blk_n, blk_s, blk_r, xs, w1, b1.reshape(E, 1, F2), w2,
      b2.reshape(E, 1, D))


def _combine_kernel(x1_ref, g_ref, gate_ref, y0_ref, y1_ref, y2_ref, y3_ref, *rest):
    o_ref = rest[-1]
    g = g_ref[...]
    half = y0_ref.shape[2]
    acc_lo = acc_hi = None
    for kk, y_ref in enumerate((y0_ref, y1_ref, y2_ref, y3_ref)):
        lo, hi = _unpack_halves(y_ref[0])
        gk = g[:, kk:kk + 1]
        acc_lo = lo * gk if acc_lo is None else acc_lo + lo * gk
        acc_hi = hi * gk if acc_hi is None else acc_hi + hi * gk
    gate = gate_ref[0]
    o_ref[:, :half] = x1_ref[:, :half] + gate[:, :half] * acc_lo
    o_ref[:, half:] = x1_ref[:, half:] + gate[:, half:] * acc_hi


def _combine(x1, gates_tk, mod3, ysg, out_prev, b0, B, nb_total, S, tm):
    D = x1.shape[1]
    ns = S // tm
    row = lambda b, s: (b * ns + s, 0)
    row_out = lambda b, s: ((b0 + b) * ns + s, 0)

    def yspec(kk):
        return pl.BlockSpec((1, tm, D // 2), lambda b, s: (kk, b * ns + s, 0))

    in_specs = [
        pl.BlockSpec((tm, D), row),
        pl.BlockSpec((tm, TOP_K), row),
        pl.BlockSpec((1, 1, D), lambda b, s: ((b0 + b) * 6 + 5, 0, 0)),
        yspec(0), yspec(1), yspec(2), yspec(3),
    ]
    args = [x1, gates_tk, mod3, ysg, ysg, ysg, ysg]
    aliases = {}
    if out_prev is not None:
        in_specs.append(pl.BlockSpec(memory_space=pl.ANY))
        args.append(out_prev)
        aliases = {len(args) - 1: 0}
    return pl.pallas_call(
        _combine_kernel,
        grid=(B, ns),
        in_specs=in_specs,
        out_specs=pl.BlockSpec((tm, D), row_out),
        out_shape=jax.ShapeDtypeStruct((nb_total * S, D), F32),
        input_output_aliases=aliases,
        compiler_params=_cparams(("arbitrary", "arbitrary")),
        name="combine",
    )(*args)


def _block_diag(w):
    n, c, d = w.shape
    eye = jnp.eye(n, dtype=w.dtype)
    return jnp.einsum("ncd,nm->ncmd", w, eye).reshape(n * c, n * d)


def _pad_heads(w, width):
    k = w.shape[0]
    w = w.reshape(k, N_HEADS, width)
    return jnp.pad(w, ((0, 0), (0, 0), (0, HEAD_PAD - width))).reshape(k, N_HEADS * HEAD_PAD)


def kernel(x, c, positions, w_ada, b_ada, g_mix, w_in, conv_w, conv_b, w_a, b_a, w_x, b_x, lam,
           g_q_lat, w_uq, g_kv_lat, w_ukv, g_qn, g_kn, w_out, g_ffn, w_router, b_router,
           w1, b1, w2, b2):
    B, S, D = x.shape
    T = B * S
    depth = w_ada.shape[0]
    tm_in = min(512, S)
    tq = min(512, S)
    tm_out = min(512, S)
    tm_comb = min(512, S)
    n_groups = 2 if B % 2 == 0 else 1
    Bg = B // n_groups
    Tg = Bg * S
    g_disp = min(64, Tg // SC_WORKERS // 2)
    g_comb = min(64, Tg * TOP_K // SC_WORKERS // 2)

    o1 = 2 * D_LRU
    o2 = o1 + Q_LORA
    o3 = o2 + KV_LORA
    tri = (jnp.arange(tm_out)[:, None] < jnp.arange(tm_out)[None, :]).astype(BF16)
    cos_t, sin_t = _rope_tables(positions)
    lane = jnp.arange(HEAD_PAD, dtype=jnp.int32)
    first = (lane >= ROPE_LO) & (lane < ROPE_LO + ROPE_HALF)
    second = (lane >= ROPE_LO + ROPE_HALF) & (lane < ROPE_LO + QK_ROPE)
    partner = jnp.where(first, lane + ROPE_HALF, jnp.where(second, lane - ROPE_HALF, 0))
    is_rot = first | second

    def rot_cols(w):
        k = w.shape[0]
        w3 = w.reshape(k, -1, HEAD_PAD)
        return jnp.where(is_rot[None, None, :], w3[:, :, partner], 0.0).reshape(w.shape)

    x2 = x.reshape(T, D)
    for l in range(depth):
        mod3 = _ada(c, w_ada[l], b_ada[l]).reshape(B * 6, 1, D)

        w_in_l = w_in[l]
        kr_cols = jnp.pad(w_in_l[:, o3:], ((0, 0), (ROPE_LO, LANES - ROPE_LO - QK_ROPE)))
        w_in_p = jnp.concatenate([w_in_l[:, :o3], kr_cols, rot_cols(kr_cols)], axis=1).astype(BF16)
        w_uq_h = _pad_heads(w_uq[l], QK_HEAD)
        w_uq_p = jnp.concatenate([w_uq_h, rot_cols(w_uq_h)], axis=1).astype(BF16)
        w_ukv_l = w_ukv[l].reshape(KV_LORA, N_HEADS, QK_NOPE + V_HEAD)
        w_uk_h = _pad_heads(w_ukv_l[:, :, :QK_NOPE].reshape(KV_LORA, N_HEADS * QK_NOPE), QK_NOPE)
        w_uv_h = _pad_heads(w_ukv_l[:, :, QK_NOPE:].reshape(KV_LORA, N_HEADS * V_HEAD), V_HEAD)
        w_ukv_p = jnp.concatenate([w_uk_h, w_uv_h], axis=1).astype(BF16)
        gqn_p = jnp.pad(g_qn[l], (0, HEAD_PAD - QK_HEAD)).reshape(1, HEAD_PAD)
        gkn_p = jnp.pad(g_kn[l], (0, HEAD_PAD - QK_HEAD)).reshape(1, HEAD_PAD)

        lru_o, qp, kp, v = _inproj(
            x2, cos_t, sin_t, mod3, g_mix[l].reshape(1, D), w_in_p, g_q_lat[l].reshape(1, Q_LORA),
            w_uq_p, g_kv_lat[l].reshape(1, KV_LORA), w_ukv_p, gqn_p, rot_cols(gqn_p), gkn_p,
            rot_cols(gkn_p), conv_w[l], conv_b[l].reshape(1, D_LRU),
            _block_diag(w_a[l]).astype(BF16), b_a[l].reshape(1, D_LRU),
            _block_diag(w_x[l]).astype(BF16), b_x[l].reshape(1, D_LRU),
            lam[l].reshape(1, D_LRU), B, S, tm_in)

        att_o = _attn(qp, kp, v, B, S, tq)

        w_out_b = w_out[l].astype(BF16)
        wr_hi, wr_lo = _split_bf16(w_router[l].T)
        g_ffn_l = g_ffn[l].reshape(1, D)
        b_r = b_router[l].reshape(N_EXPERTS, 1)
        eio = jnp.arange(N_EXPERTS, dtype=jnp.int32)
        n_blocks = -(-(Tg * TOP_K) // MOE_BLOCK) + N_EXPERTS
        bi = jnp.arange(n_blocks, dtype=jnp.int32)

        x_next = None
        for gi in range(n_groups):
            b0 = gi * Bg
            x1, h2p, idx_t, gat_t, rank_t, counts = _outproj(
                lru_o, att_o, x2, mod3, g_ffn_l, w_out_b[:D_LRU], w_out_b[D_LRU:],
                wr_hi, wr_lo, b_r, tri, b0, Bg, S, tm_out)

            counts = counts.reshape(N_EXPERTS)
            nblk_e = (counts + MOE_BLOCK - 1) // MOE_BLOCK
            blk_end = jnp.cumsum(nblk_e)
            pad_start = (blk_end - nblk_e) * MOE_BLOCK
            total = blk_end[-1]
            blk_r = jnp.minimum(bi, total - 1).astype(jnp.int32)
            blk_e = jnp.minimum(jnp.sum(blk_end[None, :] <= blk_r[:, None], axis=1),
                                N_EXPERTS - 1).astype(jnp.int32)
            blk_onehot = blk_e[:, None] == eio[None, :]
            blk_first = jnp.sum(jnp.where(blk_onehot, (blk_end - nblk_e)[None, :], 0), axis=1)
            blk_cnt = jnp.sum(jnp.where(blk_onehot, counts[None, :], 0), axis=1)
            blk_v = jnp.where(bi < total,
                              jnp.clip(blk_cnt - (bi - blk_first) * MOE_BLOCK, 0, MOE_BLOCK),
                              0).astype(jnp.int32)
            blk_f = ((bi == blk_first) & (bi < total)).astype(jnp.int32)
            nxt_first = jnp.sum(jnp.where(blk_onehot, blk_end[None, :], 0), axis=1)
            nxt_e = jnp.minimum(jnp.sum(blk_end[None, :] <= nxt_first[:, None], axis=1),
                                N_EXPERTS - 1)
            blk_n = jnp.where(nxt_first < total, nxt_e, -1).astype(jnp.int32)
            ordinal = jnp.cumsum((nblk_e > 0).astype(jnp.int32)) - 1
            blk_s = (jnp.sum(jnp.where(blk_onehot, ordinal[None, :], 0), axis=1) % 2).astype(jnp.int32)
            slot0 = jnp.sum(jnp.where(idx_t[None] == eio[:, None, None],
                                      pad_start[:, None, None], 0), axis=0)
            dest = slot0.astype(jnp.int32) + rank_t

            xs = _sc_scatter_rows(h2p, dest, n_blocks * MOE_BLOCK, g_disp)
            ys = _experts(blk_e, blk_v, blk_f, blk_n, blk_s, blk_r, xs, w1[l], b1[l], w2[l], b2[l])
            ysg = _sc_gather_rows(ys, dest.reshape(TOP_K * Tg), g_comb).reshape(TOP_K, Tg, D // 2)
            x_next = _combine(x1, gat_t.T, mod3, ysg, x_next, b0, Bg, B, S, tm_comb)
        x2 = x_next
    return x2.reshape(B, S, D)
```

```python
import functools

import jax
import jax.numpy as jnp
from jax import lax
from jax.experimental import pallas as pl
from jax.experimental.pallas import tpu as pltpu
from jax.experimental.pallas import tpu_sc as plsc

D_MODEL = 1024
D_LRU = 512
LRU_BLOCKS = 8
LRU_BD = 64
CONV_W = 4
LRU_C = 8.0
N_HEADS = 8
QK_NOPE = 64
QK_ROPE = 32
QK_HEAD = 96
V_HEAD = 64
Q_LORA = 256
KV_LORA = 128
ROPE_THETA = 10000.0
N_EXPERTS = 32
TOP_K = 4
D_FF = 1024
SWIGLU_LIMIT = 7.0
SWIGLU_ALPHA = 1.702
MOE_BLOCK = 512
EPS = 1e-6

LANES = 128
SUBLANES = 8
HEAD_PAD = 128
ROPE_LO = QK_NOPE
ROPE_HALF = QK_ROPE // 2
TOK_PER_ROW = LANES // ROPE_HALF
D_IN_PAD = 2 * D_LRU + Q_LORA + KV_LORA + 2 * LANES
LOG2_E = 1.4426950408889634
ROW_QUARTERS = 4
SUM_LANE = V_HEAD
MAX_LANE = V_HEAD + 1

VMEM_LIMIT = 56 * 1024 * 1024

F32 = jnp.float32
BF16 = jnp.bfloat16


def _cparams(sem):
    return pltpu.CompilerParams(dimension_semantics=sem, vmem_limit_bytes=VMEM_LIMIT)


def _dot(a, b):
    return jnp.dot(a, b, preferred_element_type=F32)


def _dot_nt(a, b):
    return lax.dot_general(a, b, (((1,), (1,)), ((), ())), preferred_element_type=F32)


def _split_bf16(a):
    hi = a.astype(BF16)
    lo = (a - hi.astype(F32)).astype(BF16)
    return hi, lo


def _sigmoid(x, scale=1.0):
    return 1.0 / (1.0 + jnp.exp2(x * (-scale * LOG2_E)))


def _pack_halves(x):
    bits = lax.bitcast_convert_type(x.astype(BF16).astype(F32), jnp.uint32)
    half = x.shape[1] // 2
    words = (bits[:, :half] >> 16) | (bits[:, half:] & jnp.uint32(0xFFFF0000))
    return lax.bitcast_convert_type(words, jnp.int32)


def _unpack_halves(words):
    w = lax.bitcast_convert_type(words, jnp.uint32)
    lo = lax.bitcast_convert_type(w << 16, F32)
    hi = lax.bitcast_convert_type(w & jnp.uint32(0xFFFF0000), F32)
    return lo, hi


def _ada_kernel(c_ref, w_ref, b_ref, o_ref):
    c = c_ref[...]
    s = c * _sigmoid(c)
    shi, slo = _split_bf16(s)
    whi, wlo = _split_bf16(w_ref[...])
    o_ref[...] = _dot(shi, whi) + _dot(slo, whi) + _dot(shi, wlo) + b_ref[...]


def _ada(c, w_ada, b_ada):
    B, D = c.shape
    N = w_ada.shape[1]
    tn = 1024
    return pl.pallas_call(
        _ada_kernel,
        grid=(N // tn,),
        in_specs=[
            pl.BlockSpec((B, D), lambda j: (0, 0)),
            pl.BlockSpec((D, tn), lambda j: (0, j)),
            pl.BlockSpec((1, tn), lambda j: (0, j)),
        ],
        out_specs=pl.BlockSpec((B, tn), lambda j: (0, j)),
        out_shape=jax.ShapeDtypeStruct((B, N), F32),
        compiler_params=_cparams(("arbitrary",)),
        name="ada",
    )(c, w_ada, b_ada.reshape(1, N))


def _trig_kernel(pos_ref, freq_ref, rsel_ref, fold_ref, cbase_ref, cos_ref, sin_ref):
    ang = pos_ref[...].astype(F32) * freq_ref[...]
    cs = jnp.concatenate([jnp.cos(ang), jnp.sin(ang)], axis=1)
    tm = cos_ref.shape[0]
    row = lax.broadcasted_iota(jnp.int32, (tm, 2 * LANES), 0)
    lane = lax.broadcasted_iota(jnp.int32, (tm, 2 * LANES), 1)
    own = ((lane % LANES) // ROPE_HALF) == (row % TOK_PER_ROW)
    rsel = rsel_ref[...]
    fold = fold_ref[...]
    by_row = sum(_dot(rsel, part) for part in _split_bf16(cs))
    mine = jnp.where(own, by_row, 0.0)
    out = sum(_dot(part, fold) for part in _split_bf16(mine))
    cos_ref[...] = out[:, :LANES] + cbase_ref[...]
    sin_ref[...] = out[:, LANES:]


def _rope_tables(positions):
    T = positions.size
    freqs = ROPE_THETA ** (-jnp.arange(ROPE_HALF, dtype=F32) / ROPE_HALF)
    rows = T // TOK_PER_ROW
    pos_c = jnp.repeat(positions.reshape(T).astype(jnp.int32), ROPE_HALF).reshape(rows, LANES)
    freq_c = jnp.tile(freqs, TOK_PER_ROW).reshape(1, LANES)
    tm = min(2048, T)
    tr = tm // TOK_PER_ROW
    rsel = (jnp.arange(tm)[:, None] // TOK_PER_ROW == jnp.arange(tr)[None, :]).astype(BF16)
    src = jnp.arange(LANES)[:, None] % ROPE_HALF
    dst = jnp.arange(LANES)[None, :]
    first = dst == ROPE_LO + src
    second = dst == ROPE_LO + ROPE_HALF + src
    fcos = (first | second).astype(F32)
    fsin = second.astype(F32) - first.astype(F32)
    zero = jnp.zeros((LANES, LANES), F32)
    fold = jnp.block([[fcos, zero], [zero, fsin]]).astype(BF16)
    lane = jnp.arange(LANES)
    cbase = ((lane < ROPE_LO) | (lane >= ROPE_LO + QK_ROPE)).astype(F32).reshape(1, LANES)
    full = lambda i: (0, 0)
    return pl.pallas_call(
        _trig_kernel,
        grid=(T // tm,),
        in_specs=[
            pl.BlockSpec((tr, LANES), lambda i: (i, 0)),
            pl.BlockSpec((1, LANES), full),
            pl.BlockSpec((tm, tr), full),
            pl.BlockSpec((2 * LANES, 2 * LANES), full),
            pl.BlockSpec((1, LANES), full),
        ],
        out_specs=[pl.BlockSpec((tm, LANES), lambda i: (i, 0))] * 2,
        out_shape=[jax.ShapeDtypeStruct((T, LANES), F32)] * 2,
        compiler_params=_cparams(("arbitrary",)),
        name="rope_trig",
    )(pos_c, freq_c, rsel, fold, cbase)


def _inproj_kernel(x_ref, cos_ref, sin_ref, shift_ref, scale_ref, gmix_ref, win_ref, gq_ref, wuq_ref,
                   gkv_ref, wukv_ref, gqn_ref, gqr_ref, gkn_ref, gkr_ref,
                   cw_ref, cb_ref, wa_ref, ba_ref, wx_ref, bx_ref, lam_ref,
                   lru_ref, q_ref, k_ref, v_ref, tail_ref, carry_ref):
    HP = N_HEADS * HEAD_PAD

    @pl.when(pl.program_id(1) == 0)
    def _():
        tail_ref[...] = jnp.zeros_like(tail_ref)
        carry_ref[...] = jnp.zeros_like(carry_ref)

    x = x_ref[...]
    ms = jnp.mean(x * x, axis=-1, keepdims=True)
    gain = gmix_ref[...] * (1.0 + scale_ref[0])
    h = x * lax.rsqrt(ms + EPS) * gain + shift_ref[0]
    hb = h.astype(BF16)
    o1 = 2 * D_LRU
    o2 = Q_LORA
    o3 = o2 + KV_LORA
    z_lru = _dot(hb, win_ref[:, :o1])
    z = _dot(hb, win_ref[:, o1:])
    lru_ref[...] = _lru_tile(z_lru[:, :D_LRU], z_lru[:, D_LRU:], cw_ref, cb_ref, wa_ref, ba_ref,
                             wx_ref, bx_ref, lam_ref, tail_ref, carry_ref)
    ql = z[:, :o2]
    kvl = z[:, o2:o3]
    kr = z[:, o3:o3 + LANES]
    kr_rot = z[:, o3 + LANES:]

    qn = ql * lax.rsqrt(jnp.mean(ql * ql, axis=-1, keepdims=True) + EPS) * gq_ref[...]
    qq = _dot(qn.astype(BF16), wuq_ref[...])
    kvn = kvl * lax.rsqrt(jnp.mean(kvl * kvl, axis=-1, keepdims=True) + EPS) * gkv_ref[...]
    kv = _dot(kvn.astype(BF16), wukv_ref[...])

    tm = x.shape[0]
    lane = lax.broadcasted_iota(jnp.int32, (tm, HP), 1)
    v_ref[...] = jnp.where((lane & (HEAD_PAD - 1)) == V_HEAD, 1.0, kv[:, HP:]).astype(BF16)

    cos_t = cos_ref[...]
    sin_t = sin_ref[...]
    gqn = gqn_ref[...]
    gkn = gkn_ref[...]
    cq = gqn * cos_t
    sq = gqr_ref[...] * sin_t
    kb = kr * (gkn * cos_t) + kr_rot * (gkr_ref[...] * sin_t)
    inv_w = 1.0 / QK_HEAD
    qscale = QK_HEAD ** -0.5 * LOG2_E
    for hh in range(N_HEADS):
        sl = slice(hh * HEAD_PAD, (hh + 1) * HEAD_PAD)
        qh = qq[:, sl]
        rq = lax.rsqrt(jnp.sum(qh * qh, axis=-1, keepdims=True) * inv_w + EPS) * qscale
        q_ref[:, sl] = ((qh * cq + qq[:, HP + hh * HEAD_PAD:HP + (hh + 1) * HEAD_PAD] * sq) * rq).astype(BF16)
        kraw = kv[:, sl] + kr
        rk = lax.rsqrt(jnp.sum(kraw * kraw, axis=-1, keepdims=True) * inv_w + EPS)
        k_ref[:, sl] = ((kv[:, sl] * gkn + kb) * rk).astype(BF16)


def _inproj(x2, cos_t, sin_t, mod3, g_mix, w_in_p, g_q_lat, w_uq_p, g_kv_lat, w_ukv_p,
            gqn_p, gqr_p, gkn_p, gkr_p, conv_w, conv_b, wa_d, b_a, wx_d, b_x, lam, B, S, tm):
    T, D = x2.shape
    ns = S // tm
    HP = N_HEADS * HEAD_PAD
    C = D_LRU
    row = lambda b, s: (b * ns + s, 0)
    full = lambda b, s: (0, 0)
    return pl.pallas_call(
        _inproj_kernel,
        grid=(B, ns),
        in_specs=[
            pl.BlockSpec((tm, D), row),
            pl.BlockSpec((tm, LANES), row),
            pl.BlockSpec((tm, LANES), row),
            pl.BlockSpec((1, 1, D), lambda b, s: (b * 6 + 0, 0, 0)),
            pl.BlockSpec((1, 1, D), lambda b, s: (b * 6 + 1, 0, 0)),
            pl.BlockSpec((1, D), full),
            pl.BlockSpec((D, D_IN_PAD), full),
            pl.BlockSpec((1, Q_LORA), full),
            pl.BlockSpec((Q_LORA, 2 * HP), full),
            pl.BlockSpec((1, KV_LORA), full),
            pl.BlockSpec((KV_LORA, 2 * HP), full),
            pl.BlockSpec((1, HEAD_PAD), full),
            pl.BlockSpec((1, HEAD_PAD), full),
            pl.BlockSpec((1, HEAD_PAD), full),
            pl.BlockSpec((1, HEAD_PAD), full),
            pl.BlockSpec((CONV_W, C), full),
            pl.BlockSpec((1, C), full),
            pl.BlockSpec((C, C), full),
            pl.BlockSpec((1, C), full),
            pl.BlockSpec((C, C), full),
            pl.BlockSpec((1, C), full),
            pl.BlockSpec((1, C), full),
        ],
        out_specs=[
            pl.BlockSpec((tm, C), row),
            pl.BlockSpec((tm, HP), row),
            pl.BlockSpec((tm, HP), row),
            pl.BlockSpec((tm, HP), row),
        ],
        out_shape=[
            jax.ShapeDtypeStruct((T, C), BF16),
            jax.ShapeDtypeStruct((T, HP), BF16),
            jax.ShapeDtypeStruct((T, HP), BF16),
            jax.ShapeDtypeStruct((T, HP), BF16),
        ],
        scratch_shapes=[pltpu.VMEM((SUBLANES, C), F32), pltpu.VMEM((SUBLANES, C), F32)],
        compiler_params=_cparams(("arbitrary", "arbitrary")),
        name="inproj",
    )(x2, cos_t, sin_t, mod3, mod3, g_mix, w_in_p, g_q_lat, w_uq_p, g_kv_lat, w_ukv_p,
      gqn_p, gqr_p, gkn_p, gkr_p, conv_w, conv_b, wa_d, b_a, wx_d, b_x, lam)


def _gelu_tanh(x):
    c = 0.7978845608028654
    hx = 0.5 * x
    return hx + hx * jnp.tanh(x * (c + (c * 0.044715) * (x * x)))


def _lru_tile(x, y, cw_ref, cb_ref, wa_ref, ba_ref, wx_ref, bx_ref, lam_ref, tail_ref, carry_ref):
    ts = x.shape[0]
    xext = jnp.concatenate([tail_ref[...], x], axis=0)
    cw = cw_ref[...]
    xc = x * cw[CONV_W - 1:CONV_W, :]
    for j in range(CONV_W - 1):
        sh = CONV_W - 1 - j
        xc = xc + xext[8 - sh:8 - sh + ts, :] * cw[j:j + 1, :]
    xc = xc + cb_ref[...]
    tail_ref[...] = x[ts - 8:, :]

    xb = xc.astype(BF16)
    r = _sigmoid(_dot(xb, wa_ref[...]) + ba_ref[...])
    i = _sigmoid(_dot(xb, wx_ref[...]) + bx_ref[...])
    lam = lam_ref[...]
    nl = -lam
    softplus = jnp.maximum(nl, 0.0) + jnp.log(1.0 + jnp.exp(-jnp.abs(nl)))
    log_a = (-LRU_C) * r * softplus
    a = jnp.exp(log_a)
    mult = jnp.sqrt(1.0 - a * a)
    u = mult * (i * xc)

    C = a.shape[1]
    a = a.reshape(ts // SUBLANES, SUBLANES, C)
    u = u.reshape(ts // SUBLANES, SUBLANES, C)
    sub = lax.broadcasted_iota(jnp.int32, (1, SUBLANES, 1), 1)
    sh = 1
    while sh < SUBLANES:
        a_prev = pltpu.roll(a, sh, axis=1)
        u_prev = pltpu.roll(u, sh, axis=1)
        m = sub >= sh
        u = jnp.where(m, a * u_prev + u, u)
        a = jnp.where(m, a * a_prev, a)
        sh *= 2
    a = a.reshape(ts, C)
    u = u.reshape(ts, C)
    h = carry_ref[0:1, :]
    groups = []
    for g0 in range(0, ts, SUBLANES):
        hg = u[g0:g0 + SUBLANES, :] + a[g0:g0 + SUBLANES, :] * h
        groups.append(hg)
        h = hg[SUBLANES - 1:SUBLANES, :]
    carry_ref[...] = jnp.broadcast_to(h, carry_ref.shape)
    hs = jnp.concatenate(groups, axis=0)
    return (_gelu_tanh(y) * hs).astype(BF16)


NEG_INF = -1e30


def _attn_kernel(q_ref, k_ref, v_ref, o_ref, *state, tq):
    m_refs = state[:N_HEADS]
    acc_refs = state[N_HEADS:]
    qi = pl.program_id(1)
    rowi = lax.broadcasted_iota(jnp.int32, (tq, tq), 0)
    coli = lax.broadcasted_iota(jnp.int32, (tq, tq), 1)
    diag_mask = coli <= rowi

    def head_slice(hh):
        return slice(hh * HEAD_PAD, (hh + 1) * HEAD_PAD)

    def weights(sc, m_b):
        cols = [jnp.exp2(sc[:, c0:c0 + LANES] - m_b) for c0 in range(0, tq, LANES)]
        return jnp.concatenate(cols, axis=1).astype(BF16)

    def scores(hh, r0):
        hs = head_slice(hh)
        return _dot_nt(q_ref[:, hs], k_ref[pl.ds(r0, tq), hs])

    r_diag = pl.multiple_of(qi * tq, tq)
    sc_next = scores(0, r_diag)
    for hh in range(N_HEADS):
        hs = head_slice(hh)
        sc = jnp.where(diag_mask, sc_next, NEG_INF)
        if hh + 1 < N_HEADS:
            sc_next = scores(hh + 1, r_diag)
        m_b = jnp.broadcast_to(jnp.max(sc, axis=-1, keepdims=True), (tq, LANES))
        m_refs[hh][...] = m_b
        acc_refs[hh][...] = _dot(weights(sc, m_b), v_ref[pl.ds(r_diag, tq), hs])

    @pl.loop(0, qi)
    def _(j):
        r0 = pl.multiple_of(j * tq, tq)
        sc_next = scores(0, r0)
        for hh in range(N_HEADS):
            hs = head_slice(hh)
            sc = sc_next
            if hh + 1 < N_HEADS:
                sc_next = scores(hh + 1, r0)
            m_b = m_refs[hh][...]
            m_new = jnp.maximum(m_b, jnp.max(sc, axis=-1, keepdims=True))
            alpha = jnp.exp2(m_b - m_new)
            m_refs[hh][...] = m_new
            acc_refs[hh][...] = (alpha * acc_refs[hh][...]
                                 + _dot(weights(sc, m_new), v_ref[pl.ds(r0, tq), hs]))

    for hh in range(N_HEADS):
        acc = acc_refs[hh][...]
        o = acc[:, :V_HEAD] / acc[:, SUM_LANE:SUM_LANE + 1]
        o_ref[:, hh * V_HEAD:(hh + 1) * V_HEAD] = o.astype(BF16)


def _attn(qp, kp, v, B, S, tq):
    T = qp.shape[0]
    nq = S // tq
    HP = N_HEADS * HEAD_PAD
    HV = N_HEADS * V_HEAD
    return pl.pallas_call(
        functools.partial(_attn_kernel, tq=tq),
        grid=(B, nq),
        in_specs=[
            pl.BlockSpec((tq, HP), lambda b, i: (b * nq + i, 0)),
            pl.BlockSpec((S, HP), lambda b, i: (b, 0)),
            pl.BlockSpec((S, HP), lambda b, i: (b, 0)),
        ],
        out_specs=pl.BlockSpec((tq, HV), lambda b, i: (b * nq + i, 0)),
        out_shape=jax.ShapeDtypeStruct((T, HV), BF16),
        scratch_shapes=([pltpu.VMEM((tq, LANES), F32)] * N_HEADS
                        + [pltpu.VMEM((tq, HEAD_PAD), F32)] * N_HEADS),
        compiler_params=_cparams(("arbitrary", "arbitrary")),
        name="attn",
    )(qp, kp, v)


def _outproj_kernel(lru_ref, att_ref, x_ref, gate_ref, shift_ref, scale_ref, gffn_ref,
                    wo1_ref, wo2_ref, wrh_ref, wrl_ref, br_ref, tri_ref,
                    x1_ref, h2p_ref, idx_ref, gat_ref, rank_ref, cnt_ref, run_ref):
    first = (pl.program_id(0) == 0) & (pl.program_id(1) == 0)

    @pl.when(first)
    def _():
        run_ref[...] = jnp.zeros_like(run_ref)

    mix = _dot(lru_ref[...], wo1_ref[...]) + _dot(att_ref[...], wo2_ref[...])
    x1 = x_ref[...] + gate_ref[0] * mix
    x1_ref[...] = x1
    ms = jnp.mean(x1 * x1, axis=-1, keepdims=True)
    gain = gffn_ref[...] * (1.0 + scale_ref[0])
    h2 = x1 * lax.rsqrt(ms + EPS) * gain + shift_ref[0]

    hhi = h2.astype(BF16)
    hlo = (h2 - hhi.astype(F32)).astype(BF16)
    h2p_ref[...] = _pack_halves(h2)

    wrh = wrh_ref[...]
    logits = _dot_nt(wrh, hhi) + _dot_nt(wrh, hlo) + _dot_nt(wrl_ref[...], hhi) + br_ref[...]
    ne, tm = logits.shape
    eio = lax.broadcasted_iota(jnp.int32, (ne, tm), 0)
    vals, idxs, sels = [], [], []
    l = logits
    for _ in range(TOP_K):
        m = jnp.max(l, axis=0, keepdims=True)
        idx = jnp.min(jnp.where(l == m, eio, ne), axis=0, keepdims=True)
        sel = eio == idx
        l = jnp.where(sel, -jnp.inf, l)
        vals.append(m)
        idxs.append(idx)
        sels.append(sel)
    es = [jnp.exp(v - vals[0]) for v in vals]
    inv = 1.0 / (es[0] + es[1] + es[2] + es[3])
    sel_any = jnp.where(sels[0] | sels[1] | sels[2] | sels[3], 1.0, 0.0)
    run = run_ref[...]
    excl = _dot(sel_any.astype(BF16), tri_ref[...]) + run
    for kk in range(TOP_K):
        idx_ref[kk:kk + 1, :] = idxs[kk]
        gat_ref[kk:kk + 1, :] = es[kk] * inv
        rk = jnp.sum(jnp.where(sels[kk], excl, 0.0), axis=0, keepdims=True)
        rank_ref[kk:kk + 1, :] = rk.astype(jnp.int32)
    run = run + jnp.sum(sel_any, axis=1, keepdims=True)
    run_ref[...] = run
    cnt_ref[...] = run.astype(jnp.int32)


def _outproj(lru_o, att_o, x2, mod3, g_ffn, wo1, wo2, wr_hi, wr_lo, b_r, tri, b0, B, S, tm):
    D = x2.shape[1]
    T = B * S
    ns = S // tm
    C = lru_o.shape[1]
    row_in = lambda b, s: ((b0 + b) * ns + s, 0)
    row = lambda b, s: (b * ns + s, 0)
    col = lambda b, s: (0, b * ns + s)
    full = lambda b, s: (0, 0)
    return pl.pallas_call(
        _outproj_kernel,
        grid=(B, ns),
        in_specs=[
            pl.BlockSpec((tm, C), row_in),
            pl.BlockSpec((tm, C), row_in),
            pl.BlockSpec((tm, D), row_in),
            pl.BlockSpec((1, 1, D), lambda b, s: ((b0 + b) * 6 + 2, 0, 0)),
            pl.BlockSpec((1, 1, D), lambda b, s: ((b0 + b) * 6 + 3, 0, 0)),
            pl.BlockSpec((1, 1, D), lambda b, s: ((b0 + b) * 6 + 4, 0, 0)),
            pl.BlockSpec((1, D), full),
            pl.BlockSpec((C, D), full),
            pl.BlockSpec((C, D), full),
            pl.BlockSpec((N_EXPERTS, D), full),
            pl.BlockSpec((N_EXPERTS, D), full),
            pl.BlockSpec((N_EXPERTS, 1), full),
            pl.BlockSpec(tri.shape, full),
        ],
        out_specs=[
            pl.BlockSpec((tm, D), row),
            pl.BlockSpec((tm, D // 2), row),
            pl.BlockSpec((TOP_K, tm), col),
            pl.BlockSpec((TOP_K, tm), col),
            pl.BlockSpec((TOP_K, tm), col),
            pl.BlockSpec((N_EXPERTS, 1), full),
        ],
        out_shape=[
            jax.ShapeDtypeStruct((T, D), F32),
            jax.ShapeDtypeStruct((T, D // 2), jnp.int32),
            jax.ShapeDtypeStruct((TOP_K, T), jnp.int32),
            jax.ShapeDtypeStruct((TOP_K, T), F32),
            jax.ShapeDtypeStruct((TOP_K, T), jnp.int32),
            jax.ShapeDtypeStruct((N_EXPERTS, 1), jnp.int32),
        ],
        scratch_shapes=[pltpu.VMEM((N_EXPERTS, 1), F32)],
        compiler_params=_cparams(("arbitrary", "arbitrary")),
        name="outproj",
    )(lru_o, att_o, x2, mod3, mod3, mod3, g_ffn, wo1, wo2, wr_hi, wr_lo, b_r, tri)


SC_CORES = 2
SC_SUBCORES = 16
SC_WORKERS = SC_CORES * SC_SUBCORES


def _sc_mesh():
    return plsc.VectorSubcoreMesh(core_axis_name="c", subcore_axis_name="s",
                                  num_cores=SC_CORES, num_subcores=SC_SUBCORES)


def _sc_worker_id():
    return lax.axis_index("s") * SC_CORES + lax.axis_index("c")


def _sc_scatter_rows(rows, idx, n_out, g):
    T, W = rows.shape
    K = idx.shape[0]
    per_w = T // SC_WORKERS
    nch = per_w // g
    assert per_w * SC_WORKERS == T and nch * g == per_w and nch % 2 == 0
    idx_w = idx.reshape(K, SC_WORKERS, nch, g).transpose(1, 2, 0, 3).reshape(SC_WORKERS, nch * K, g)

    def body(rows_hbm, idx_hbm, out_hbm, idx_v, buf0, buf1, semr0, semr1, semw):
        wid = _sc_worker_id()
        base = wid * per_w
        pltpu.sync_copy(idx_hbm.at[wid], idx_v)

        def read(j, buf, sem):
            return pltpu.make_async_copy(rows_hbm.at[pl.ds(base + j * g, g)], buf, sem)

        def scatter(j, buf):
            copies = [pltpu.async_copy(buf, out_hbm.at[idx_v.at[j * K + kk]], semw)
                      for kk in range(K)]
            for cp in copies:
                cp.wait()

        read(0, buf0, semr0).start()

        @pl.loop(0, nch // 2)
        def _(jj):
            j0 = 2 * jj
            read(j0 + 1, buf1, semr1).start()
            read(j0, buf0, semr0).wait()
            scatter(j0, buf0)

            @pl.when(j0 + 2 < nch)
            def _():
                read(j0 + 2, buf0, semr0).start()

            read(j0 + 1, buf1, semr1).wait()
            scatter(j0 + 1, buf1)

    return pl.kernel(
        body,
        out_type=jax.ShapeDtypeStruct((n_out, W), rows.dtype),
        mesh=_sc_mesh(),
        scratch_types=[
            pltpu.VMEM((nch * K, g), jnp.int32),
            pltpu.VMEM((g, W), rows.dtype),
            pltpu.VMEM((g, W), rows.dtype),
            pltpu.SemaphoreType.DMA,
            pltpu.SemaphoreType.DMA,
            pltpu.SemaphoreType.DMA,
        ],
        name="sc_scatter_rows",
    )(rows, idx_w)


def _sc_gather_rows(table, idx, g):
    W = table.shape[1]
    N = idx.shape[0]
    per_w = N // SC_WORKERS
    nch = per_w // g
    assert per_w * SC_WORKERS == N and nch * g == per_w and nch % 2 == 0
    idx_w = idx.reshape(SC_WORKERS, nch, g)

    def body(table_hbm, idx_hbm, out_hbm, idx_v, buf0, buf1, sem0, sem1):
        wid = _sc_worker_id()
        base = wid * per_w
        pltpu.sync_copy(idx_hbm.at[wid], idx_v)

        def gather(j, buf, sem):
            return pltpu.make_async_copy(table_hbm.at[idx_v.at[j]], buf, sem)

        def put(j, buf):
            pltpu.sync_copy(buf, out_hbm.at[pl.ds(base + j * g, g)])

        gather(0, buf0, sem0).start()

        @pl.loop(0, nch // 2)
        def _(jj):
            j0 = 2 * jj
            gather(j0 + 1, buf1, sem1).start()
            gather(j0, buf0, sem0).wait()
            put(j0, buf0)

            @pl.when(j0 + 2 < nch)
            def _():
                gather(j0 + 2, buf0, sem0).start()

            gather(j0 + 1, buf1, sem1).wait()
            put(j0 + 1, buf1)

    return pl.kernel(
        body,
        out_type=jax.ShapeDtypeStruct((N, W), table.dtype),
        mesh=_sc_mesh(),
        scratch_types=[
            pltpu.VMEM((nch, g), jnp.int32),
            pltpu.VMEM((g, W), table.dtype),
            pltpu.VMEM((g, W), table.dtype),
            pltpu.SemaphoreType.DMA,
            pltpu.SemaphoreType.DMA,
        ],
        name="sc_gather_rows",
    )(table, idx_w)


def _experts_kernel(be_ref, bv_ref, bf_ref, bn_ref, bs_ref, xs_ref, w1_hbm, b1_ref, w2_hbm, b2_ref,
                    ys_ref, w1s_ref, w2s_ref, w1b_ref, w2b_ref, sem):
    i = pl.program_id(0)
    nvalid = bv_ref[i]

    def weight_copies(e, slot):
        return (pltpu.make_async_copy(w1_hbm.at[e], w1s_ref.at[slot], sem.at[0, slot]),
                pltpu.make_async_copy(w2_hbm.at[e], w2s_ref.at[slot], sem.at[1, slot]))

    @pl.when(bf_ref[i] > 0)
    def _():
        slot = bs_ref[i]

        @pl.when(i == 0)
        def _():
            for cp in weight_copies(be_ref[0], slot):
                cp.start()

        for cp in weight_copies(be_ref[i], slot):
            cp.wait()
        w1b_ref[...] = w1s_ref[slot].astype(BF16)
        w2b_ref[...] = w2s_ref[slot].astype(BF16)

        @pl.when(bn_ref[i] >= 0)
        def _():
            for cp in weight_copies(bn_ref[i], 1 - slot):
                cp.start()

    def ffn(rows):
        xw = xs_ref[:rows, :]
        rowi = lax.broadcasted_iota(jnp.int32, (rows, 1), 0)
        lo, hi = _unpack_halves(jnp.where(rowi < nvalid, xw, 0))
        xb = jnp.concatenate([lo.astype(BF16), hi.astype(BF16)], axis=1)
        gu = _dot(xb, w1b_ref[...]) + b1_ref[0]
        glu = jnp.minimum(gu[:, :D_FF], SWIGLU_LIMIT)
        lin = jnp.clip(gu[:, D_FF:], -SWIGLU_LIMIT, SWIGLU_LIMIT)
        act = (lin + 1.0) * (glu * _sigmoid(glu, SWIGLU_ALPHA))
        ys_ref[:rows, :] = _pack_halves(_dot(act.astype(BF16), w2b_ref[...]) + b2_ref[0])

    step = xs_ref.shape[0] // ROW_QUARTERS
    for nq in range(1, ROW_QUARTERS + 1):
        pl.when((nvalid > (nq - 1) * step) & (nvalid <= nq * step))(functools.partial(ffn, nq * step))


def _experts(blk_e, blk_v, blk_f, blk_n, blk_s, blk_r, xs, w1, b1, w2, b2):
    P, W = xs.shape
    nb = P // MOE_BLOCK
    E, D, F2 = w1.shape
    grid_spec = pltpu.PrefetchScalarGridSpec(
        num_scalar_prefetch=6,
        grid=(nb,),
        in_specs=[
            pl.BlockSpec((MOE_BLOCK, W), lambda i, be, bv, bf, bn, bs, br: (br[i], 0)),
            pl.BlockSpec(memory_space=pl.ANY),
            pl.BlockSpec((1, 1, F2), lambda i, be, bv, bf, bn, bs, br: (be[i], 0, 0)),
            pl.BlockSpec(memory_space=pl.ANY),
            pl.BlockSpec((1, 1, D), lambda i, be, bv, bf, bn, bs, br: (be[i], 0, 0)),
        ],
        out_specs=pl.BlockSpec((MOE_BLOCK, D // 2), lambda i, be, bv, bf, bn, bs, br: (br[i], 0)),
        scratch_shapes=[
            pltpu.VMEM((2, D, F2), F32),
            pltpu.VMEM((2, D_FF, D), F32),
            pltpu.VMEM((D, F2), BF16),
            pltpu.VMEM((D_FF, D), BF16),
            pltpu.SemaphoreType.DMA((2, 2)),
        ],
    )

    def kern(be_ref, bv_ref, bf_ref, bn_ref, bs_ref, br_ref, *refs):
        del br_ref
        _experts_kernel(be_ref, bv_ref, bf_ref, bn_ref, bs_ref, *refs)

    return pl.pallas_call(
        kern,
        grid_spec=grid_spec,
        out_shape=jax.ShapeDtypeStruct((P, D // 2), jnp.int32),
        compiler_params=_cparams(("arbitrary",)),
        name="experts",
    )(blk_e, blk_v, blk_f, blk_n, blk_s, blk_r, xs, w1, b1.reshape(E, 1, F2), w2,
      b2.reshape(E, 1, D))


def _combine_kernel(x1_ref, g_ref, gate_ref, y0_ref, y1_ref, y2_ref, y3_ref, *rest):
    o_ref = rest[-1]
    g = g_ref[...]
    half = y0_ref.shape[2]
    acc_lo = acc_hi = None
    for kk, y_ref in enumerate((y0_ref, y1_ref, y2_ref, y3_ref)):
        lo, hi = _unpack_halves(y_ref[0])
        gk = g[:, kk:kk + 1]
        acc_lo = lo * gk if acc_lo is None else acc_lo + lo * gk
        acc_hi = hi * gk if acc_hi is None else acc_hi + hi * gk
    gate = gate_ref[0]
    o_ref[:, :half] = x1_ref[:, :half] + gate[:, :half] * acc_lo
    o_ref[:, half:] = x1_ref[:, half:] + gate[:, half:] * acc_hi


def _combine(x1, gates_tk, mod3, ysg, out_prev, b0, B, nb_total, S, tm):
    D = x1.shape[1]
    ns = S // tm
    row = lambda b, s: (b * ns + s, 0)
    row_out = lambda b, s: ((b0 + b) * ns + s, 0)

    def yspec(kk):
        return pl.BlockSpec((1, tm, D // 2), lambda b, s: (kk, b * ns + s, 0))

    in_specs = [
        pl.BlockSpec((tm, D), row),
        pl.BlockSpec((tm, TOP_K), row),
        pl.BlockSpec((1, 1, D), lambda b, s: ((b0 + b) * 6 + 5, 0, 0)),
        yspec(0), yspec(1), yspec(2), yspec(3),
    ]
    args = [x1, gates_tk, mod3, ysg, ysg, ysg, ysg]
    aliases = {}
    if out_prev is not None:
        in_specs.append(pl.BlockSpec(memory_space=pl.ANY))
        args.append(out_prev)
        aliases = {len(args) - 1: 0}
    return pl.pallas_call(
        _combine_kernel,
        grid=(B, ns),
        in_specs=in_specs,
        out_specs=pl.BlockSpec((tm, D), row_out),
        out_shape=jax.ShapeDtypeStruct((nb_total * S, D), F32),
        input_output_aliases=aliases,
        compiler_params=_cparams(("arbitrary", "arbitrary")),
        name="combine",
    )(*args)


def _block_diag(w):
    n, c, d = w.shape
    eye = jnp.eye(n, dtype=w.dtype)
    return jnp.einsum("ncd,nm->ncmd", w, eye).reshape(n * c, n * d)


def _pad_heads(w, width):
    k = w.shape[0]
    w = w.reshape(k, N_HEADS, width)
    return jnp.pad(w, ((0, 0), (0, 0), (0, HEAD_PAD - width))).reshape(k, N_HEADS * HEAD_PAD)


def kernel(x, c, positions, w_ada, b_ada, g_mix, w_in, conv_w, conv_b, w_a, b_a, w_x, b_x, lam,
           g_q_lat, w_uq, g_kv_lat, w_ukv, g_qn, g_kn, w_out, g_ffn, w_router, b_router,
           w1, b1, w2, b2):
    B, S, D = x.shape
    T = B * S
    depth = w_ada.shape[0]
    tm_in = min(512, S)
    tq = min(512, S)
    tm_out = min(512, S)
    tm_comb = min(512, S)
    n_groups = 2 if B % 2 == 0 else 1
    Bg = B // n_groups
    Tg = Bg * S
    g_disp = min(64, Tg // SC_WORKERS // 2)
    g_comb = min(64, Tg * TOP_K // SC_WORKERS // 2)

    o1 = 2 * D_LRU
    o2 = o1 + Q_LORA
    o3 = o2 + KV_LORA
    tri = (jnp.arange(tm_out)[:, None] < jnp.arange(tm_out)[None, :]).astype(BF16)
    cos_t, sin_t = _rope_tables(positions)
    lane = jnp.arange(HEAD_PAD, dtype=jnp.int32)
    first = (lane >= ROPE_LO) & (lane < ROPE_LO + ROPE_HALF)
    second = (lane >= ROPE_LO + ROPE_HALF) & (lane < ROPE_LO + QK_ROPE)
    partner = jnp.where(first, lane + ROPE_HALF, jnp.where(second, lane - ROPE_HALF, 0))
    is_rot = first | second

    def rot_cols(w):
        k = w.shape[0]
        w3 = w.reshape(k, -1, HEAD_PAD)
        return jnp.where(is_rot[None, None, :], w3[:, :, partner], 0.0).reshape(w.shape)

    x2 = x.reshape(T, D)
    for l in range(depth):
        mod3 = _ada(c, w_ada[l], b_ada[l]).reshape(B * 6, 1, D)

        w_in_l = w_in[l]
        kr_cols = jnp.pad(w_in_l[:, o3:], ((0, 0), (ROPE_LO, LANES - ROPE_LO - QK_ROPE)))
        w_in_p = jnp.concatenate([w_in_l[:, :o3], kr_cols, rot_cols(kr_cols)], axis=1).astype(BF16)
        w_uq_h = _pad_heads(w_uq[l], QK_HEAD)
        w_uq_p = jnp.concatenate([w_uq_h, rot_cols(w_uq_h)], axis=1).astype(BF16)
        w_ukv_l = w_ukv[l].reshape(KV_LORA, N_HEADS, QK_NOPE + V_HEAD)
        w_uk_h = _pad_heads(w_ukv_l[:, :, :QK_NOPE].reshape(KV_LORA, N_HEADS * QK_NOPE), QK_NOPE)
        w_uv_h = _pad_heads(w_ukv_l[:, :, QK_NOPE:].reshape(KV_LORA, N_HEADS * V_HEAD), V_HEAD)
        w_ukv_p = jnp.concatenate([w_uk_h, w_uv_h], axis=1).astype(BF16)
        gqn_p = jnp.pad(g_qn[l], (0, HEAD_PAD - QK_HEAD)).reshape(1, HEAD_PAD)
        gkn_p = jnp.pad(g_kn[l], (0, HEAD_PAD - QK_HEAD)).reshape(1, HEAD_PAD)

        lru_o, qp, kp, v = _inproj(
            x2, cos_t, sin_t, mod3, g_mix[l].reshape(1, D), w_in_p, g_q_lat[l].reshape(1, Q_LORA),
            w_uq_p, g_kv_lat[l].reshape(1, KV_LORA), w_ukv_p, gqn_p, rot_cols(gqn_p), gkn_p,
            rot_cols(gkn_p), conv_w[l], conv_b[l].reshape(1, D_LRU),
            _block_diag(w_a[l]).astype(BF16), b_a[l].reshape(1, D_LRU),
            _block_diag(w_x[l]).astype(BF16), b_x[l].reshape(1, D_LRU),
            lam[l].reshape(1, D_LRU), B, S, tm_in)

        att_o = _attn(qp, kp, v, B, S, tq)

        w_out_b = w_out[l].astype(BF16)
        wr_hi, wr_lo = _split_bf16(w_router[l].T)
        g_ffn_l = g_ffn[l].reshape(1, D)
        b_r = b_router[l].reshape(N_EXPERTS, 1)
        eio = jnp.arange(N_EXPERTS, dtype=jnp.int32)
        n_blocks = -(-(Tg * TOP_K) // MOE_BLOCK) + N_EXPERTS
        bi = jnp.arange(n_blocks, dtype=jnp.int32)

        x_next = None
        for gi in range(n_groups):
            b0 = gi * Bg
            x1, h2p, idx_t, gat_t, rank_t, counts = _outproj(
                lru_o, att_o, x2, mod3, g_ffn_l, w_out_b[:D_LRU], w_out_b[D_LRU:],
                wr_hi, wr_lo, b_r, tri, b0, Bg, S, tm_out)

            counts = counts.reshape(N_EXPERTS)
            nblk_e = (counts + MOE_BLOCK - 1) // MOE_BLOCK
            blk_end = jnp.cumsum(nblk_e)
            pad_start = (blk_end - nblk_e) * MOE_BLOCK
            total = blk_end[-1]
            blk_r = jnp.minimum(bi, total - 1).astype(jnp.int32)
            blk_e = jnp.minimum(jnp.sum(blk_end[None, :] <= blk_r[:, None], axis=1),
                                N_EXPERTS - 1).astype(jnp.int32)
            blk_onehot = blk_e[:, None] == eio[None, :]
            blk_first = jnp.sum(jnp.where(blk_onehot, (blk_end - nblk_e)[None, :], 0), axis=1)
            blk_cnt = jnp.sum(jnp.where(blk_onehot, counts[None, :], 0), axis=1)
            blk_v = jnp.where(bi < total,
                              jnp.clip(blk_cnt - (bi - blk_first) * MOE_BLOCK, 0, MOE_BLOCK),
                              0).astype(jnp.int32)
            blk_f = ((bi == blk_first) & (bi < total)).astype(jnp.int32)
            nxt_first = jnp.sum(jnp.where(blk_onehot, blk_end[None, :], 0), axis=1)
            nxt_e = jnp.minimum(jnp.sum(blk_end[None, :] <= nxt_first[:, None], axis=1),
                                N_EXPERTS - 1)
            blk_n = jnp.where(nxt_first < total, nxt_e, -1).astype(jnp.int32)
            ordinal = jnp.cumsum((nblk_e > 0).astype(jnp.int32)) - 1
            blk_s = (jnp.sum(jnp.where(blk_onehot, ordinal[None, :], 0), axis=1) % 2).astype(jnp.int32)
            slot0 = jnp.sum(jnp.where(idx_t[None] == eio[:, None, None],
                                      pad_start[:, None, None], 0), axis=0)
            dest = slot0.astype(jnp.int32) + rank_t

            xs = _sc_scatter_rows(h2p, dest, n_blocks * MOE_BLOCK, g_disp)
            ys = _experts(blk_e, blk_v, blk_f, blk_n, blk_s, blk_r, xs, w1[l], b1[l], w2[l], b2[l])
            ysg = _sc_gather_rows(ys, dest.reshape(TOP_K * Tg), g_comb).reshape(TOP_K, Tg, D // 2)
            x_next = _combine(x1, gat_t.T, mod3, ysg, x_next, b0, Bg, B, S, tm_comb)
        x2 = x_next
    return x2.reshape(B, S, D)
```

```python
import functools

import jax
import jax.numpy as jnp
from jax import lax
from jax.experimental import pallas as pl
from jax.experimental.pallas import tpu as pltpu
from jax.experimental.pallas import tpu_sc as plsc

D_MODEL = 1024
D_LRU = 512
LRU_BLOCKS = 8
LRU_BD = 64
CONV_W = 4
LRU_C = 8.0
N_HEADS = 8
QK_NOPE = 64
QK_ROPE = 32
QK_HEAD = 96
V_HEAD = 64
Q_LORA = 256
KV_LORA = 128
ROPE_THETA = 10000.0
N_EXPERTS = 32
TOP_K = 4
D_FF = 1024
SWIGLU_LIMIT = 7.0
SWIGLU_ALPHA = 1.702
MOE_BLOCK = 512
EPS = 1e-6

LANES = 128
SUBLANES = 8
HEAD_PAD = 128
ROPE_LO = QK_NOPE
ROPE_HALF = QK_ROPE // 2
TOK_PER_ROW = LANES // ROPE_HALF
D_IN_PAD = 2 * D_LRU + Q_LORA + KV_LORA + 2 * LANES
LOG2_E = 1.4426950408889634
ROW_QUARTERS = 4
SUM_LANE = V_HEAD
MAX_LANE = V_HEAD + 1

VMEM_LIMIT = 56 * 1024 * 1024

F32 = jnp.float32
BF16 = jnp.bfloat16


def _cparams(sem):
    return pltpu.CompilerParams(dimension_semantics=sem, vmem_limit_bytes=VMEM_LIMIT)


def _dot(a, b):
    return jnp.dot(a, b, preferred_element_type=F32)


def _dot_nt(a, b):
    return lax.dot_general(a, b, (((1,), (1,)), ((), ())), preferred_element_type=F32)


def _split_bf16(a):
    hi = a.astype(BF16)
    lo = (a - hi.astype(F32)).astype(BF16)
    return hi, lo


def _sigmoid(x, scale=1.0):
    return 1.0 / (1.0 + jnp.exp2(x * (-scale * LOG2_E)))


def _pack_halves(x):
    bits = lax.bitcast_convert_type(x.astype(BF16).astype(F32), jnp.uint32)
    half = x.shape[1] // 2
    words = (bits[:, :half] >> 16) | (bits[:, half:] & jnp.uint32(0xFFFF0000))
    return lax.bitcast_convert_type(words, jnp.int32)


def _unpack_halves(words):
    w = lax.bitcast_convert_type(words, jnp.uint32)
    lo = lax.bitcast_convert_type(w << 16, F32)
    hi = lax.bitcast_convert_type(w & jnp.uint32(0xFFFF0000), F32)
    return lo, hi


def _ada_kernel(c_ref, w_ref, b_ref, o_ref):
    c = c_ref[...]
    s = c * _sigmoid(c)
    shi, slo = _split_bf16(s)
    whi, wlo = _split_bf16(w_ref[...])
    o_ref[...] = _dot(shi, whi) + _dot(slo, whi) + _dot(shi, wlo) + b_ref[...]


def _ada(c, w_ada, b_ada):
    B, D = c.shape
    N = w_ada.shape[1]
    tn = 1024
    return pl.pallas_call(
        _ada_kernel,
        grid=(N // tn,),
        in_specs=[
            pl.BlockSpec((B, D), lambda j: (0, 0)),
            pl.BlockSpec((D, tn), lambda j: (0, j)),
            pl.BlockSpec((1, tn), lambda j: (0, j)),
        ],
        out_specs=pl.BlockSpec((B, tn), lambda j: (0, j)),
        out_shape=jax.ShapeDtypeStruct((B, N), F32),
        compiler_params=_cparams(("arbitrary",)),
        name="ada",
    )(c, w_ada, b_ada.reshape(1, N))


def _trig_kernel(pos_ref, freq_ref, rsel_ref, fold_ref, cbase_ref, cos_ref, sin_ref):
    ang = pos_ref[...].astype(F32) * freq_ref[...]
    cs = jnp.concatenate([jnp.cos(ang), jnp.sin(ang)], axis=1)
    tm = cos_ref.shape[0]
    row = lax.broadcasted_iota(jnp.int32, (tm, 2 * LANES), 0)
    lane = lax.broadcasted_iota(jnp.int32, (tm, 2 * LANES), 1)
    own = ((lane % LANES) // ROPE_HALF) == (row % TOK_PER_ROW)
    rsel = rsel_ref[...]
    fold = fold_ref[...]
    by_row = sum(_dot(rsel, part) for part in _split_bf16(cs))
    mine = jnp.where(own, by_row, 0.0)
    out = sum(_dot(part, fold) for part in _split_bf16(mine))
    cos_ref[...] = out[:, :LANES] + cbase_ref[...]
    sin_ref[...] = out[:, LANES:]


def _rope_tables(positions):
    T = positions.size
    freqs = ROPE_THETA ** (-jnp.arange(ROPE_HALF, dtype=F32) / ROPE_HALF)
    rows = T // TOK_PER_ROW
    pos_c = jnp.repeat(positions.reshape(T).astype(jnp.int32), ROPE_HALF).reshape(rows, LANES)
    freq_c = jnp.tile(freqs, TOK_PER_ROW).reshape(1, LANES)
    tm = min(2048, T)
    tr = tm // TOK_PER_ROW
    rsel = (jnp.arange(tm)[:, None] // TOK_PER_ROW == jnp.arange(tr)[None, :]).astype(BF16)
    src = jnp.arange(LANES)[:, None] % ROPE_HALF
    dst = jnp.arange(LANES)[None, :]
    first = dst == ROPE_LO + src
    second = dst == ROPE_LO + ROPE_HALF + src
    fcos = (first | second).astype(F32)
    fsin = second.astype(F32) - first.astype(F32)
    zero = jnp.zeros((LANES, LANES), F32)
    fold = jnp.block([[fcos, zero], [zero, fsin]]).astype(BF16)
    lane = jnp.arange(LANES)
    cbase = ((lane < ROPE_LO) | (lane >= ROPE_LO + QK_ROPE)).astype(F32).reshape(1, LANES)
    full = lambda i: (0, 0)
    return pl.pallas_call(
        _trig_kernel,
        grid=(T // tm,),
        in_specs=[
            pl.BlockSpec((tr, LANES), lambda i: (i, 0)),
            pl.BlockSpec((1, LANES), full),
            pl.BlockSpec((tm, tr), full),
            pl.BlockSpec((2 * LANES, 2 * LANES), full),
            pl.BlockSpec((1, LANES), full),
        ],
        out_specs=[pl.BlockSpec((tm, LANES), lambda i: (i, 0))] * 2,
        out_shape=[jax.ShapeDtypeStruct((T, LANES), F32)] * 2,
        compiler_params=_cparams(("arbitrary",)),
        name="rope_trig",
    )(pos_c, freq_c, rsel, fold, cbase)


def _inproj_kernel(x_ref, cos_ref, sin_ref, shift_ref, scale_ref, gmix_ref, win_ref, gq_ref, wuq_ref,
                   gkv_ref, wukv_ref, gqn_ref, gqr_ref, gkn_ref, gkr_ref,
                   cw_ref, cb_ref, wa_ref, ba_ref, wx_ref, bx_ref, lam_ref,
                   lru_ref, q_ref, k_ref, v_ref, tail_ref, carry_ref):
    HP = N_HEADS * HEAD_PAD

    @pl.when(pl.program_id(1) == 0)
    def _():
        tail_ref[...] = jnp.zeros_like(tail_ref)
        carry_ref[...] = jnp.zeros_like(carry_ref)

    x = x_ref[...]
    ms = jnp.mean(x * x, axis=-1, keepdims=True)
    gain = gmix_ref[...] * (1.0 + scale_ref[0])
    h = x * lax.rsqrt(ms + EPS) * gain + shift_ref[0]
    hb = h.astype(BF16)
    o1 = 2 * D_LRU
    o2 = Q_LORA
    o3 = o2 + KV_LORA
    z_lru = _dot(hb, win_ref[:, :o1])
    z = _dot(hb, win_ref[:, o1:])
    lru_ref[...] = _lru_tile(z_lru[:, :D_LRU], z_lru[:, D_LRU:], cw_ref, cb_ref, wa_ref, ba_ref,
                             wx_ref, bx_ref, lam_ref, tail_ref, carry_ref)
    ql = z[:, :o2]
    kvl = z[:, o2:o3]
    kr = z[:, o3:o3 + LANES]
    kr_rot = z[:, o3 + LANES:]

    qn = ql * lax.rsqrt(jnp.mean(ql * ql, axis=-1, keepdims=True) + EPS) * gq_ref[...]
    qq = _dot(qn.astype(BF16), wuq_ref[...])
    kvn = kvl * lax.rsqrt(jnp.mean(kvl * kvl, axis=-1, keepdims=True) + EPS) * gkv_ref[...]
    kv = _dot(kvn.astype(BF16), wukv_ref[...])

    tm = x.shape[0]
    lane = lax.broadcasted_iota(jnp.int32, (tm, HP), 1)
    v_ref[...] = jnp.where((lane & (HEAD_PAD - 1)) == V_HEAD, 1.0, kv[:, HP:]).astype(BF16)

    cos_t = cos_ref[...]
    sin_t = sin_ref[...]
    gqn = gqn_ref[...]
    gkn = gkn_ref[...]
    cq = gqn * cos_t
    sq = gqr_ref[...] * sin_t
    kb = kr * (gkn * cos_t) + kr_rot * (gkr_ref[...] * sin_t)
    inv_w = 1.0 / QK_HEAD
    qscale = QK_HEAD ** -0.5 * LOG2_E
    for hh in range(N_HEADS):
        sl = slice(hh * HEAD_PAD, (hh + 1) * HEAD_PAD)
        qh = qq[:, sl]
        rq = lax.rsqrt(jnp.sum(qh * qh, axis=-1, keepdims=True) * inv_w + EPS) * qscale
        q_ref[:, sl] = ((qh * cq + qq[:, HP + hh * HEAD_PAD:HP + (hh + 1) * HEAD_PAD] * sq) * rq).astype(BF16)
        kraw = kv[:, sl] + kr
        rk = lax.rsqrt(jnp.sum(kraw * kraw, axis=-1, keepdims=True) * inv_w + EPS)
        k_ref[:, sl] = ((kv[:, sl] * gkn + kb) * rk).astype(BF16)


def _inproj(x2, cos_t, sin_t, mod3, g_mix, w_in_p, g_q_lat, w_uq_p, g_kv_lat, w_ukv_p,
            gqn_p, gqr_p, gkn_p, gkr_p, conv_w, conv_b, wa_d, b_a, wx_d, b_x, lam, B, S, tm):
    T, D = x2.shape
    ns = S // tm
    HP = N_HEADS * HEAD_PAD
    C = D_LRU
    row = lambda b, s: (b * ns + s, 0)
    full = lambda b, s: (0, 0)
    return pl.pallas_call(
        _inproj_kernel,
        grid=(B, ns),
        in_specs=[
            pl.BlockSpec((tm, D), row),
            pl.BlockSpec((tm, LANES), row),
            pl.BlockSpec((tm, LANES), row),
            pl.BlockSpec((1, 1, D), lambda b, s: (b * 6 + 0, 0, 0)),
            pl.BlockSpec((1, 1, D), lambda b, s: (b * 6 + 1, 0, 0)),
            pl.BlockSpec((1, D), full),
            pl.BlockSpec((D, D_IN_PAD), full),
            pl.BlockSpec((1, Q_LORA), full),
            pl.BlockSpec((Q_LORA, 2 * HP), full),
            pl.BlockSpec((1, KV_LORA), full),
            pl.BlockSpec((KV_LORA, 2 * HP), full),
            pl.BlockSpec((1, HEAD_PAD), full),
            pl.BlockSpec((1, HEAD_PAD), full),
            pl.BlockSpec((1, HEAD_PAD), full),
            pl.BlockSpec((1, HEAD_PAD), full),
            pl.BlockSpec((CONV_W, C), full),
            pl.BlockSpec((1, C), full),
            pl.BlockSpec((C, C), full),
            pl.BlockSpec((1, C), full),
            pl.BlockSpec((C, C), full),
            pl.BlockSpec((1, C), full),
            pl.BlockSpec((1, C), full),
        ],
        out_specs=[
            pl.BlockSpec((tm, C), row),
            pl.BlockSpec((tm, HP), row),
            pl.BlockSpec((tm, HP), row),
            pl.BlockSpec((tm, HP), row),
        ],
        out_shape=[
            jax.ShapeDtypeStruct((T, C), BF16),
            jax.ShapeDtypeStruct((T, HP), BF16),
            jax.ShapeDtypeStruct((T, HP), BF16),
            jax.ShapeDtypeStruct((T, HP), BF16),
        ],
        scratch_shapes=[pltpu.VMEM((SUBLANES, C), F32), pltpu.VMEM((SUBLANES, C), F32)],
        compiler_params=_cparams(("arbitrary", "arbitrary")),
        name="inproj",
    )(x2, cos_t, sin_t, mod3, mod3, g_mix, w_in_p, g_q_lat, w_uq_p, g_kv_lat, w_ukv_p,
      gqn_p, gqr_p, gkn_p, gkr_p, conv_w, conv_b, wa_d, b_a, wx_d, b_x, lam)


def _gelu_tanh(x):
    c = 0.7978845608028654
    hx = 0.5 * x
    return hx + hx * jnp.tanh(x * (c + (c * 0.044715) * (x * x)))


def _lru_tile(x, y, cw_ref, cb_ref, wa_ref, ba_ref, wx_ref, bx_ref, lam_ref, tail_ref, carry_ref):
    ts = x.shape[0]
    xext = jnp.concatenate([tail_ref[...], x], axis=0)
    cw = cw_ref[...]
    xc = x * cw[CONV_W - 1:CONV_W, :]
    for j in range(CONV_W - 1):
        sh = CONV_W - 1 - j
        xc = xc + xext[8 - sh:8 - sh + ts, :] * cw[j:j + 1, :]
    xc = xc + cb_ref[...]
    tail_ref[...] = x[ts - 8:, :]

    xb = xc.astype(BF16)
    r = _sigmoid(_dot(xb, wa_ref[...]) + ba_ref[...])
    i = _sigmoid(_dot(xb, wx_ref[...]) + bx_ref[...])
    lam = lam_ref[...]
    nl = -lam
    softplus = jnp.maximum(nl, 0.0) + jnp.log(1.0 + jnp.exp(-jnp.abs(nl)))
    log_a = (-LRU_C) * r * softplus
    a = jnp.exp(log_a)
    mult = jnp.sqrt(1.0 - a * a)
    u = mult * (i * xc)

    C = a.shape[1]
    a = a.reshape(ts // SUBLANES, SUBLANES, C)
    u = u.reshape(ts // SUBLANES, SUBLANES, C)
    sub = lax.broadcasted_iota(jnp.int32, (1, SUBLANES, 1), 1)
    sh = 1
    while sh < SUBLANES:
        a_prev = pltpu.roll(a, sh, axis=1)
        u_prev = pltpu.roll(u, sh, axis=1)
        m = sub >= sh
        u = jnp.where(m, a * u_prev + u, u)
        a = jnp.where(m, a * a_prev, a)
        sh *= 2
    a = a.reshape(ts, C)
    u = u.reshape(ts, C)
    h = carry_ref[0:1, :]
    groups = []
    for g0 in range(0, ts, SUBLANES):
        hg = u[g0:g0 + SUBLANES, :] + a[g0:g0 + SUBLANES, :] * h
        groups.append(hg)
        h = hg[SUBLANES - 1:SUBLANES, :]
    carry_ref[...] = jnp.broadcast_to(h, carry_ref.shape)
    hs = jnp.concatenate(groups, axis=0)
    return (_gelu_tanh(y) * hs).astype(BF16)


NEG_INF = -1e30


def _attn_kernel(q_ref, k_ref, v_ref, o_ref, *state, tq):
    m_refs = state[:N_HEADS]
    acc_refs = state[N_HEADS:]
    qi = pl.program_id(1)
    rowi = lax.broadcasted_iota(jnp.int32, (tq, tq), 0)
    coli = lax.broadcasted_iota(jnp.int32, (tq, tq), 1)
    diag_mask = coli <= rowi

    def head_slice(hh):
        return slice(hh * HEAD_PAD, (hh + 1) * HEAD_PAD)

    def weights(sc, m_b):
        cols = [jnp.exp2(sc[:, c0:c0 + LANES] - m_b) for c0 in range(0, tq, LANES)]
        return jnp.concatenate(cols, axis=1).astype(BF16)

    def scores(hh, r0):
        hs = head_slice(hh)
        return _dot_nt(q_ref[:, hs], k_ref[pl.ds(r0, tq), hs])

    r_diag = pl.multiple_of(qi * tq, tq)
    sc_next = scores(0, r_diag)
    for hh in range(N_HEADS):
        hs = head_slice(hh)
        sc = jnp.where(diag_mask, sc_next, NEG_INF)
        if hh + 1 < N_HEADS:
            sc_next = scores(hh + 1, r_diag)
        m_b = jnp.broadcast_to(jnp.max(sc, axis=-1, keepdims=True), (tq, LANES))
        m_refs[hh][...] = m_b
        acc_refs[hh][...] = _dot(weights(sc, m_b), v_ref[pl.ds(r_diag, tq), hs])

    @pl.loop(0, qi)
    def _(j):
        r0 = pl.multiple_of(j * tq, tq)
        sc_next = scores(0, r0)
        for hh in range(N_HEADS):
            hs = head_slice(hh)
            sc = sc_next
            if hh + 1 < N_HEADS:
                sc_next = scores(hh + 1, r0)
            m_b = m_refs[hh][...]
            m_new = jnp.maximum(m_b, jnp.max(sc, axis=-1, keepdims=True))
            alpha = jnp.exp2(m_b - m_new)
            m_refs[hh][...] = m_new
            acc_refs[hh][...] = (alpha * acc_refs[hh][...]
                                 + _dot(weights(sc, m_new), v_ref[pl.ds(r0, tq), hs]))

    for hh in range(N_HEADS):
        acc = acc_refs[hh][...]
        o = acc[:, :V_HEAD] / acc[:, SUM_LANE:SUM_LANE + 1]
        o_ref[:, hh * V_HEAD:(hh + 1) * V_HEAD] = o.astype(BF16)


def _attn(qp, kp, v, B, S, tq):
    T = qp.shape[0]
    nq = S // tq
    HP = N_HEADS * HEAD_PAD
    HV = N_HEADS * V_HEAD
    return pl.pallas_call(
        functools.partial(_attn_kernel, tq=tq),
        grid=(B, nq),
        in_specs=[
            pl.BlockSpec((tq, HP), lambda b, i: (b * nq + i, 0)),
            pl.BlockSpec((S, HP), lambda b, i: (b, 0)),
            pl.BlockSpec((S, HP), lambda b, i: (b, 0)),
        ],
        out_specs=pl.BlockSpec((tq, HV), lambda b, i: (b * nq + i, 0)),
        out_shape=jax.ShapeDtypeStruct((T, HV), BF16),
        scratch_shapes=([pltpu.VMEM((tq, LANES), F32)] * N_HEADS
                        + [pltpu.VMEM((tq, HEAD_PAD), F32)] * N_HEADS),
        compiler_params=_cparams(("arbitrary", "arbitrary")),
        name="attn",
    )(qp, kp, v)


def _outproj_kernel(lru_ref, att_ref, x_ref, gate_ref, shift_ref, scale_ref, gffn_ref,
                    wo1_ref, wo2_ref, wr_ref, br_ref, tri_ref,
                    x1_ref, h2p_ref, idx_ref, gat_ref, rank_ref, cnt_ref, run_ref):
    first = (pl.program_id(0) == 0) & (pl.program_id(1) == 0)

    @pl.when(first)
    def _():
        run_ref[...] = jnp.zeros_like(run_ref)

    mix = _dot(lru_ref[...], wo1_ref[...]) + _dot(att_ref[...], wo2_ref[...])
    x1 = x_ref[...] + gate_ref[0] * mix
    x1_ref[...] = x1
    ms = jnp.mean(x1 * x1, axis=-1, keepdims=True)
    gain = gffn_ref[...] * (1.0 + scale_ref[0])
    h2 = x1 * lax.rsqrt(ms + EPS) * gain + shift_ref[0]

    hhi = h2.astype(BF16)
    hlo = (h2 - hhi.astype(F32)).astype(BF16)
    h2p_ref[...] = _pack_halves(h2)

    ne = br_ref.shape[0]
    stacked = _dot_nt(wr_ref[...], hhi)
    logits = stacked[:ne] + stacked[ne:] + _dot_nt(wr_ref[:ne, :], hlo) + br_ref[...]

    tm = logits.shape[1]
    eio = lax.broadcasted_iota(jnp.int32, (ne, tm), 0)
    vals, idxs, sels = [], [], []
    l = logits
    for _ in range(TOP_K):
        m = jnp.max(l, axis=0, keepdims=True)
        idx = jnp.min(jnp.where(l == m, eio, ne), axis=0, keepdims=True)
        sel = eio == idx
        l = jnp.where(sel, -jnp.inf, l)
        vals.append(m)
        idxs.append(idx)
        sels.append(sel)
    es = [jnp.exp(v - vals[0]) for v in vals]
    inv = 1.0 / (es[0] + es[1] + es[2] + es[3])
    sel_any = jnp.where(sels[0] | sels[1] | sels[2] | sels[3], 1.0, 0.0)
    run = run_ref[...]
    excl = _dot(sel_any.astype(BF16), tri_ref[...]) + run
    for kk in range(TOP_K):
        idx_ref[kk:kk + 1, :] = idxs[kk]
        gat_ref[kk:kk + 1, :] = es[kk] * inv
        rk = jnp.sum(jnp.where(sels[kk], excl, 0.0), axis=0, keepdims=True)
        rank_ref[kk:kk + 1, :] = rk.astype(jnp.int32)
    run = run + jnp.sum(sel_any, axis=1, keepdims=True)
    run_ref[...] = run
    cnt_ref[...] = run.astype(jnp.int32)


def _outproj(lru_o, att_o, x2, mod3, g_ffn, wo1, wo2, wr_stack, b_r, tri, b0, B, S, tm):
    D = x2.shape[1]
    T = B * S
    ns = S // tm
    C = lru_o.shape[1]
    row_in = lambda b, s: ((b0 + b) * ns + s, 0)
    row = lambda b, s: (b * ns + s, 0)
    col = lambda b, s: (0, b * ns + s)
    full = lambda b, s: (0, 0)
    return pl.pallas_call(
        _outproj_kernel,
        grid=(B, ns),
        in_specs=[
            pl.BlockSpec((tm, C), row_in),
            pl.BlockSpec((tm, C), row_in),
            pl.BlockSpec((tm, D), row_in),
            pl.BlockSpec((1, 1, D), lambda b, s: ((b0 + b) * 6 + 2, 0, 0)),
            pl.BlockSpec((1, 1, D), lambda b, s: ((b0 + b) * 6 + 3, 0, 0)),
            pl.BlockSpec((1, 1, D), lambda b, s: ((b0 + b) * 6 + 4, 0, 0)),
            pl.BlockSpec((1, D), full),
            pl.BlockSpec((C, D), full),
            pl.BlockSpec((C, D), full),
            pl.BlockSpec((2 * N_EXPERTS, D), full),
            pl.BlockSpec((N_EXPERTS, 1), full),
            pl.BlockSpec(tri.shape, full),
        ],
        out_specs=[
            pl.BlockSpec((tm, D), row),
            pl.BlockSpec((tm, D // 2), row),
            pl.BlockSpec((TOP_K, tm), col),
            pl.BlockSpec((TOP_K, tm), col),
            pl.BlockSpec((TOP_K, tm), col),
            pl.BlockSpec((N_EXPERTS, 1), full),
        ],
        out_shape=[
            jax.ShapeDtypeStruct((T, D), F32),
            jax.ShapeDtypeStruct((T, D // 2), jnp.int32),
            jax.ShapeDtypeStruct((TOP_K, T), jnp.int32),
            jax.ShapeDtypeStruct((TOP_K, T), F32),
            jax.ShapeDtypeStruct((TOP_K, T), jnp.int32),
            jax.ShapeDtypeStruct((N_EXPERTS, 1), jnp.int32),
        ],
        scratch_shapes=[pltpu.VMEM((N_EXPERTS, 1), F32)],
        compiler_params=_cparams(("arbitrary", "arbitrary")),
        name="outproj",
    )(lru_o, att_o, x2, mod3, mod3, mod3, g_ffn, wo1, wo2, wr_stack, b_r, tri)


SC_CORES = 2
SC_SUBCORES = 16
SC_WORKERS = SC_CORES * SC_SUBCORES


def _sc_mesh():
    return plsc.VectorSubcoreMesh(core_axis_name="c", subcore_axis_name="s",
                                  num_cores=SC_CORES, num_subcores=SC_SUBCORES)


def _sc_worker_id():
    return lax.axis_index("s") * SC_CORES + lax.axis_index("c")


def _sc_scatter_rows(rows, idx, n_out, g):
    T, W = rows.shape
    K = idx.shape[0]
    per_w = T // SC_WORKERS
    nch = per_w // g
    assert per_w * SC_WORKERS == T and nch * g == per_w and nch % 2 == 0
    idx_w = idx.reshape(K, SC_WORKERS, nch, g).transpose(1, 2, 0, 3).reshape(SC_WORKERS, nch * K, g)

    def body(rows_hbm, idx_hbm, out_hbm, idx_v, buf0, buf1, semr0, semr1, semw):
        wid = _sc_worker_id()
        base = wid * per_w
        pltpu.sync_copy(idx_hbm.at[wid], idx_v)

        def read(j, buf, sem):
            return pltpu.make_async_copy(rows_hbm.at[pl.ds(base + j * g, g)], buf, sem)

        def scatter(j, buf):
            copies = [pltpu.async_copy(buf, out_hbm.at[idx_v.at[j * K + kk]], semw)
                      for kk in range(K)]
            for cp in copies:
                cp.wait()

        read(0, buf0, semr0).start()

        @pl.loop(0, nch // 2)
        def _(jj):
            j0 = 2 * jj
            read(j0 + 1, buf1, semr1).start()
            read(j0, buf0, semr0).wait()
            scatter(j0, buf0)

            @pl.when(j0 + 2 < nch)
            def _():
                read(j0 + 2, buf0, semr0).start()

            read(j0 + 1, buf1, semr1).wait()
            scatter(j0 + 1, buf1)

    return pl.kernel(
        body,
        out_type=jax.ShapeDtypeStruct((n_out, W), rows.dtype),
        mesh=_sc_mesh(),
        scratch_types=[
            pltpu.VMEM((nch * K, g), jnp.int32),
            pltpu.VMEM((g, W), rows.dtype),
            pltpu.VMEM((g, W), rows.dtype),
            pltpu.SemaphoreType.DMA,
            pltpu.SemaphoreType.DMA,
            pltpu.SemaphoreType.DMA,
        ],
        name="sc_scatter_rows",
    )(rows, idx_w)


def _sc_gather_rows(table, idx, g):
    W = table.shape[1]
    N = idx.shape[0]
    per_w = N // SC_WORKERS
    nch = per_w // g
    assert per_w * SC_WORKERS == N and nch * g == per_w and nch % 2 == 0
    idx_w = idx.reshape(SC_WORKERS, nch, g)

    def body(table_hbm, idx_hbm, out_hbm, idx_v, buf0, buf1, sem0, sem1):
        wid = _sc_worker_id()
        base = wid * per_w
        pltpu.sync_copy(idx_hbm.at[wid], idx_v)

        def gather(j, buf, sem):
            return pltpu.make_async_copy(table_hbm.at[idx_v.at[j]], buf, sem)

        def put(j, buf):
            pltpu.sync_copy(buf, out_hbm.at[pl.ds(base + j * g, g)])

        gather(0, buf0, sem0).start()

        @pl.loop(0, nch // 2)
        def _(jj):
            j0 = 2 * jj
            gather(j0 + 1, buf1, sem1).start()
            gather(j0, buf0, sem0).wait()
            put(j0, buf0)

            @pl.when(j0 + 2 < nch)
            def _():
                gather(j0 + 2, buf0, sem0).start()

            gather(j0 + 1, buf1, sem1).wait()
            put(j0 + 1, buf1)

    return pl.kernel(
        body,
        out_type=jax.ShapeDtypeStruct((N, W), table.dtype),
        mesh=_sc_mesh(),
        scratch_types=[
            pltpu.VMEM((nch, g), jnp.int32),
            pltpu.VMEM((g, W), table.dtype),
            pltpu.VMEM((g, W), table.dtype),
            pltpu.SemaphoreType.DMA,
            pltpu.SemaphoreType.DMA,
        ],
        name="sc_gather_rows",
    )(table, idx_w)


def _experts_kernel(be_ref, bv_ref, bf_ref, bn_ref, bs_ref, xs_ref, w1_hbm, b1_ref, w2_hbm, b2_ref,
                    ys_ref, w1s_ref, w2s_ref, w1b_ref, w2b_ref, sem):
    i = pl.program_id(0)
    nvalid = bv_ref[i]

    def weight_copies(e, slot):
        return (pltpu.make_async_copy(w1_hbm.at[e], w1s_ref.at[slot], sem.at[0, slot]),
                pltpu.make_async_copy(w2_hbm.at[e], w2s_ref.at[slot], sem.at[1, slot]))

    @pl.when(bf_ref[i] > 0)
    def _():
        slot = bs_ref[i]

        @pl.when(i == 0)
        def _():
            for cp in weight_copies(be_ref[0], slot):
                cp.start()

        for cp in weight_copies(be_ref[i], slot):
            cp.wait()
        w1b_ref[...] = w1s_ref[slot].astype(BF16)
        w2b_ref[...] = w2s_ref[slot].astype(BF16)

        @pl.when(bn_ref[i] >= 0)
        def _():
            for cp in weight_copies(bn_ref[i], 1 - slot):
                cp.start()

    def ffn(rows):
        xw = xs_ref[:rows, :]
        rowi = lax.broadcasted_iota(jnp.int32, (rows, 1), 0)
        lo, hi = _unpack_halves(jnp.where(rowi < nvalid, xw, 0))
        xb = jnp.concatenate([lo.astype(BF16), hi.astype(BF16)], axis=1)
        gu = _dot(xb, w1b_ref[...]) + b1_ref[0]
        glu = jnp.minimum(gu[:, :D_FF], SWIGLU_LIMIT)
        lin = jnp.clip(gu[:, D_FF:], -SWIGLU_LIMIT, SWIGLU_LIMIT)
        act = (lin + 1.0) * (glu * _sigmoid(glu, SWIGLU_ALPHA))
        ys_ref[:rows, :] = _pack_halves(_dot(act.astype(BF16), w2b_ref[...]) + b2_ref[0])

    step = xs_ref.shape[0] // ROW_QUARTERS
    for nq in range(1, ROW_QUARTERS + 1):
        pl.when((nvalid > (nq - 1) * step) & (nvalid <= nq * step))(functools.partial(ffn, nq * step))


def _experts(blk_e, blk_v, blk_f, blk_n, blk_s, blk_r, xs, w1, b1, w2, b2):
    P, W = xs.shape
    nb = P // MOE_BLOCK
    E, D, F2 = w1.shape
    grid_spec = pltpu.PrefetchScalarGridSpec(
        num_scalar_prefetch=6,
        grid=(nb,),
        in_specs=[
            pl.BlockSpec((MOE_BLOCK, W), lambda i, be, bv, bf, bn, bs, br: (br[i], 0)),
            pl.BlockSpec(memory_space=pl.ANY),
            pl.BlockSpec((1, 1, F2), lambda i, be, bv, bf, bn, bs, br: (be[i], 0, 0)),
            pl.BlockSpec(memory_space=pl.ANY),
            pl.BlockSpec((1, 1, D), lambda i, be, bv, bf, bn, bs, br: (be[i], 0, 0)),
        ],
        out_specs=pl.BlockSpec((MOE_BLOCK, D // 2), lambda i, be, bv, bf, bn, bs, br: (br[i], 0)),
        scratch_shapes=[
            pltpu.VMEM((2, D, F2), F32),
            pltpu.VMEM((2, D_FF, D), F32),
            pltpu.VMEM((D, F2), BF16),
            pltpu.VMEM((D_FF, D), BF16),
            pltpu.SemaphoreType.DMA((2, 2)),
        ],
    )

    def kern(be_ref, bv_ref, bf_ref, bn_ref, bs_ref, br_ref, *refs):
        del br_ref
        _experts_kernel(be_ref, bv_ref, bf_ref, bn_ref, bs_ref, *refs)

    return pl.pallas_call(
        kern,
        grid_spec=grid_spec,
        out_shape=jax.ShapeDtypeStruct((P, D // 2), jnp.int32),
        compiler_params=_cparams(("arbitrary",)),
        name="experts",
    )(blk_e, blk_v, blk_f, blk_n, blk_s, blk_r, xs, w1, b1.reshape(E, 1, F2), w2,
      b2.reshape(E, 1, D))


def _combine_kernel(x1_ref, g_ref, gate_ref, y0_ref, y1_ref, y2_ref, y3_ref, *rest):
    o_ref = rest[-1]
    g = g_ref[...]
    half = y0_ref.shape[2]
    acc_lo = acc_hi = None
    for kk, y_ref in enumerate((y0_ref, y1_ref, y2_ref, y3_ref)):
        lo, hi = _unpack_halves(y_ref[0])
        gk = g[:, kk:kk + 1]
        acc_lo = lo * gk if acc_lo is None else acc_lo + lo * gk
        acc_hi = hi * gk if acc_hi is None else acc_hi + hi * gk
    gate = gate_ref[0]
    o_ref[:, :half] = x1_ref[:, :half] + gate[:, :half] * acc_lo
    o_ref[:, half:] = x1_ref[:, half:] + gate[:, half:] * acc_hi


def _combine(x1, gates_tk, mod3, ysg, out_prev, b0, B, nb_total, S, tm):
    D = x1.shape[1]
    ns = S // tm
    row = lambda b, s: (b * ns + s, 0)
    row_out = lambda b, s: ((b0 + b) * ns + s, 0)

    def yspec(kk):
        return pl.BlockSpec((1, tm, D // 2), lambda b, s: (kk, b * ns + s, 0))

    in_specs = [
        pl.BlockSpec((tm, D), row),
        pl.BlockSpec((tm, TOP_K), row),
        pl.BlockSpec((1, 1, D), lambda b, s: ((b0 + b) * 6 + 5, 0, 0)),
        yspec(0), yspec(1), yspec(2), yspec(3),
    ]
    args = [x1, gates_tk, mod3, ysg, ysg, ysg, ysg]
    aliases = {}
    if out_prev is not None:
        in_specs.append(pl.BlockSpec(memory_space=pl.ANY))
        args.append(out_prev)
        aliases = {len(args) - 1: 0}
    return pl.pallas_call(
        _combine_kernel,
        grid=(B, ns),
        in_specs=in_specs,
        out_specs=pl.BlockSpec((tm, D), row_out),
        out_shape=jax.ShapeDtypeStruct((nb_total * S, D), F32),
        input_output_aliases=aliases,
        compiler_params=_cparams(("arbitrary", "arbitrary")),
        name="combine",
    )(*args)


def _block_diag(w):
    n, c, d = w.shape
    eye = jnp.eye(n, dtype=w.dtype)
    return jnp.einsum("ncd,nm->ncmd", w, eye).reshape(n * c, n * d)


def _pad_heads(w, width):
    k = w.shape[0]
    w = w.reshape(k, N_HEADS, width)
    return jnp.pad(w, ((0, 0), (0, 0), (0, HEAD_PAD - width))).reshape(k, N_HEADS * HEAD_PAD)


def kernel(x, c, positions, w_ada, b_ada, g_mix, w_in, conv_w, conv_b, w_a, b_a, w_x, b_x, lam,
           g_q_lat, w_uq, g_kv_lat, w_ukv, g_qn, g_kn, w_out, g_ffn, w_router, b_router,
           w1, b1, w2, b2):
    B, S, D = x.shape
    T = B * S
    depth = w_ada.shape[0]
    tm_in = min(512, S)
    tq = min(512, S)
    tm_out = min(512, S)
    tm_comb = min(512, S)
    n_groups = 2 if B % 2 == 0 else 1
    Bg = B // n_groups
    Tg = Bg * S
    g_disp = min(64, Tg // SC_WORKERS // 2)
    g_comb = min(64, Tg * TOP_K // SC_WORKERS // 2)

    o1 = 2 * D_LRU
    o2 = o1 + Q_LORA
    o3 = o2 + KV_LORA
    tri = (jnp.arange(tm_out)[:, None] < jnp.arange(tm_out)[None, :]).astype(BF16)
    cos_t, sin_t = _rope_tables(positions)
    lane = jnp.arange(HEAD_PAD, dtype=jnp.int32)
    first = (lane >= ROPE_LO) & (lane < ROPE_LO + ROPE_HALF)
    second = (lane >= ROPE_LO + ROPE_HALF) & (lane < ROPE_LO + QK_ROPE)
    partner = jnp.where(first, lane + ROPE_HALF, jnp.where(second, lane - ROPE_HALF, 0))
    is_rot = first | second

    def rot_cols(w):
        k = w.shape[0]
        w3 = w.reshape(k, -1, HEAD_PAD)
        return jnp.where(is_rot[None, None, :], w3[:, :, partner], 0.0).reshape(w.shape)

    x2 = x.reshape(T, D)
    for l in range(depth):
        mod3 = _ada(c, w_ada[l], b_ada[l]).reshape(B * 6, 1, D)

        w_in_l = w_in[l]
        kr_cols = jnp.pad(w_in_l[:, o3:], ((0, 0), (ROPE_LO, LANES - ROPE_LO - QK_ROPE)))
        w_in_p = jnp.concatenate([w_in_l[:, :o3], kr_cols, rot_cols(kr_cols)], axis=1).astype(BF16)
        w_uq_h = _pad_heads(w_uq[l], QK_HEAD)
        w_uq_p = jnp.concatenate([w_uq_h, rot_cols(w_uq_h)], axis=1).astype(BF16)
        w_ukv_l = w_ukv[l].reshape(KV_LORA, N_HEADS, QK_NOPE + V_HEAD)
        w_uk_h = _pad_heads(w_ukv_l[:, :, :QK_NOPE].reshape(KV_LORA, N_HEADS * QK_NOPE), QK_NOPE)
        w_uv_h = _pad_heads(w_ukv_l[:, :, QK_NOPE:].reshape(KV_LORA, N_HEADS * V_HEAD), V_HEAD)
        w_ukv_p = jnp.concatenate([w_uk_h, w_uv_h], axis=1).astype(BF16)
        gqn_p = jnp.pad(g_qn[l], (0, HEAD_PAD - QK_HEAD)).reshape(1, HEAD_PAD)
        gkn_p = jnp.pad(g_kn[l], (0, HEAD_PAD - QK_HEAD)).reshape(1, HEAD_PAD)

        lru_o, qp, kp, v = _inproj(
            x2, cos_t, sin_t, mod3, g_mix[l].reshape(1, D), w_in_p, g_q_lat[l].reshape(1, Q_LORA),
            w_uq_p, g_kv_lat[l].reshape(1, KV_LORA), w_ukv_p, gqn_p, rot_cols(gqn_p), gkn_p,
            rot_cols(gkn_p), conv_w[l], conv_b[l].reshape(1, D_LRU),
            _block_diag(w_a[l]).astype(BF16), b_a[l].reshape(1, D_LRU),
            _block_diag(w_x[l]).astype(BF16), b_x[l].reshape(1, D_LRU),
            lam[l].reshape(1, D_LRU), B, S, tm_in)

        att_o = _attn(qp, kp, v, B, S, tq)

        w_out_b = w_out[l].astype(BF16)
        wr_stack = jnp.concatenate(_split_bf16(w_router[l].T), axis=0)
        g_ffn_l = g_ffn[l].reshape(1, D)
        b_r = b_router[l].reshape(N_EXPERTS, 1)
        eio = jnp.arange(N_EXPERTS, dtype=jnp.int32)
        n_blocks = -(-(Tg * TOP_K) // MOE_BLOCK) + N_EXPERTS
        bi = jnp.arange(n_blocks, dtype=jnp.int32)

        x_next = None
        for gi in range(n_groups):
            b0 = gi * Bg
            x1, h2p, idx_t, gat_t, rank_t, counts = _outproj(
                lru_o, att_o, x2, mod3, g_ffn_l, w_out_b[:D_LRU], w_out_b[D_LRU:],
                wr_stack, b_r, tri, b0, Bg, S, tm_out)

            counts = counts.reshape(N_EXPERTS)
            nblk_e = (counts + MOE_BLOCK - 1) // MOE_BLOCK
            blk_end = jnp.cumsum(nblk_e)
            pad_start = (blk_end - nblk_e) * MOE_BLOCK
            total = blk_end[-1]
            blk_r = jnp.minimum(bi, total - 1).astype(jnp.int32)
            blk_e = jnp.minimum(jnp.sum(blk_end[None, :] <= blk_r[:, None], axis=1),
                                N_EXPERTS - 1).astype(jnp.int32)
            blk_onehot = blk_e[:, None] == eio[None, :]
            blk_first = jnp.sum(jnp.where(blk_onehot, (blk_end - nblk_e)[None, :], 0), axis=1)
            blk_cnt = jnp.sum(jnp.where(blk_onehot, counts[None, :], 0), axis=1)
            blk_v = jnp.where(bi < total,
                              jnp.clip(blk_cnt - (bi - blk_first) * MOE_BLOCK, 0, MOE_BLOCK),
                              0).astype(jnp.int32)
            blk_f = ((bi == blk_first) & (bi < total)).astype(jnp.int32)
            nxt_first = jnp.sum(jnp.where(blk_onehot, blk_end[None, :], 0), axis=1)
            nxt_e = jnp.minimum(jnp.sum(blk_end[None, :] <= nxt_first[:, None], axis=1),
                                N_EXPERTS - 1)
            blk_n = jnp.where(nxt_first < total, nxt_e, -1).astype(jnp.int32)
            ordinal = jnp.cumsum((nblk_e > 0).astype(jnp.int32)) - 1
            blk_s = (jnp.sum(jnp.where(blk_onehot, ordinal[None, :], 0), axis=1) % 2).astype(jnp.int32)
            slot0 = jnp.sum(jnp.where(idx_t[None] == eio[:, None, None],
                                      pad_start[:, None, None], 0), axis=0)
            dest = slot0.astype(jnp.int32) + rank_t

            xs = _sc_scatter_rows(h2p, dest, n_blocks * MOE_BLOCK, g_disp)
            ys = _experts(blk_e, blk_v, blk_f, blk_n, blk_s, blk_r, xs, w1[l], b1[l], w2[l], b2[l])
            ysg = _sc_gather_rows(ys, dest.reshape(TOP_K * Tg), g_comb).reshape(TOP_K, Tg, D // 2)
            x_next = _combine(x1, gat_t.T, mod3, ysg, x_next, b0, Bg, B, S, tm_comb)
        x2 = x_next
    return x2.reshape(B, S, D)
```

```python
import functools

import jax
import jax.numpy as jnp
from jax import lax
from jax.experimental import pallas as pl
from jax.experimental.pallas import tpu as pltpu
from jax.experimental.pallas import tpu_sc as plsc

D_MODEL = 1024
D_LRU = 512
LRU_BLOCKS = 8
LRU_BD = 64
CONV_W = 4
LRU_C = 8.0
N_HEADS = 8
QK_NOPE = 64
QK_ROPE = 32
QK_HEAD = 96
V_HEAD = 64
Q_LORA = 256
KV_LORA = 128
ROPE_THETA = 10000.0
N_EXPERTS = 32
TOP_K = 4
D_FF = 1024
SWIGLU_LIMIT = 7.0
SWIGLU_ALPHA = 1.702
MOE_BLOCK = 512
EPS = 1e-6

LANES = 128
SUBLANES = 8
HEAD_PAD = 128
ROPE_LO = QK_NOPE
ROPE_HALF = QK_ROPE // 2
TOK_PER_ROW = LANES // ROPE_HALF
D_IN_PAD = 2 * D_LRU + Q_LORA + KV_LORA + 2 * LANES
LOG2_E = 1.4426950408889634
ROW_QUARTERS = 4
SUM_LANE = V_HEAD
MAX_LANE = V_HEAD + 1

VMEM_LIMIT = 56 * 1024 * 1024

F32 = jnp.float32
BF16 = jnp.bfloat16


def _cparams(sem):
    return pltpu.CompilerParams(dimension_semantics=sem, vmem_limit_bytes=VMEM_LIMIT)


def _dot(a, b):
    return jnp.dot(a, b, preferred_element_type=F32)


def _dot_nt(a, b):
    return lax.dot_general(a, b, (((1,), (1,)), ((), ())), preferred_element_type=F32)


def _split_bf16(a):
    hi = a.astype(BF16)
    lo = (a - hi.astype(F32)).astype(BF16)
    return hi, lo


def _sigmoid(x, scale=1.0):
    return 1.0 / (1.0 + jnp.exp2(x * (-scale * LOG2_E)))


def _pack_halves(x):
    bits = lax.bitcast_convert_type(x.astype(BF16).astype(F32), jnp.uint32)
    half = x.shape[1] // 2
    words = (bits[:, :half] >> 16) | (bits[:, half:] & jnp.uint32(0xFFFF0000))
    return lax.bitcast_convert_type(words, jnp.int32)


def _unpack_halves(words):
    w = lax.bitcast_convert_type(words, jnp.uint32)
    lo = lax.bitcast_convert_type(w << 16, F32)
    hi = lax.bitcast_convert_type(w & jnp.uint32(0xFFFF0000), F32)
    return lo, hi


def _ada_kernel(c_ref, w_ref, b_ref, o_ref):
    c = c_ref[...]
    s = c * _sigmoid(c)
    shi, slo = _split_bf16(s)
    whi, wlo = _split_bf16(w_ref[...])
    o_ref[...] = _dot(shi, whi) + _dot(slo, whi) + _dot(shi, wlo) + b_ref[...]


def _ada(c, w_ada, b_ada):
    B, D = c.shape
    N = w_ada.shape[1]
    tn = 1024
    return pl.pallas_call(
        _ada_kernel,
        grid=(N // tn,),
        in_specs=[
            pl.BlockSpec((B, D), lambda j: (0, 0)),
            pl.BlockSpec((D, tn), lambda j: (0, j)),
            pl.BlockSpec((1, tn), lambda j: (0, j)),
        ],
        out_specs=pl.BlockSpec((B, tn), lambda j: (0, j)),
        out_shape=jax.ShapeDtypeStruct((B, N), F32),
        compiler_params=_cparams(("arbitrary",)),
        name="ada",
    )(c, w_ada, b_ada.reshape(1, N))


def _trig_kernel(pos_ref, freq_ref, rsel_ref, fold_ref, cbase_ref, cos_ref, sin_ref):
    ang = pos_ref[...].astype(F32) * freq_ref[...]
    cs = jnp.concatenate([jnp.cos(ang), jnp.sin(ang)], axis=1)
    tm = cos_ref.shape[0]
    row = lax.broadcasted_iota(jnp.int32, (tm, 2 * LANES), 0)
    lane = lax.broadcasted_iota(jnp.int32, (tm, 2 * LANES), 1)
    own = ((lane % LANES) // ROPE_HALF) == (row % TOK_PER_ROW)
    rsel = rsel_ref[...]
    fold = fold_ref[...]
    by_row = sum(_dot(rsel, part) for part in _split_bf16(cs))
    mine = jnp.where(own, by_row, 0.0)
    out = sum(_dot(part, fold) for part in _split_bf16(mine))
    cos_ref[...] = out[:, :LANES] + cbase_ref[...]
    sin_ref[...] = out[:, LANES:]


def _rope_tables(positions):
    T = positions.size
    freqs = ROPE_THETA ** (-jnp.arange(ROPE_HALF, dtype=F32) / ROPE_HALF)
    rows = T // TOK_PER_ROW
    pos_c = jnp.repeat(positions.reshape(T).astype(jnp.int32), ROPE_HALF).reshape(rows, LANES)
    freq_c = jnp.tile(freqs, TOK_PER_ROW).reshape(1, LANES)
    tm = min(2048, T)
    tr = tm // TOK_PER_ROW
    rsel = (jnp.arange(tm)[:, None] // TOK_PER_ROW == jnp.arange(tr)[None, :]).astype(BF16)
    src = jnp.arange(LANES)[:, None] % ROPE_HALF
    dst = jnp.arange(LANES)[None, :]
    first = dst == ROPE_LO + src
    second = dst == ROPE_LO + ROPE_HALF + src
    fcos = (first | second).astype(F32)
    fsin = second.astype(F32) - first.astype(F32)
    zero = jnp.zeros((LANES, LANES), F32)
    fold = jnp.block([[fcos, zero], [zero, fsin]]).astype(BF16)
    lane = jnp.arange(LANES)
    cbase = ((lane < ROPE_LO) | (lane >= ROPE_LO + QK_ROPE)).astype(F32).reshape(1, LANES)
    full = lambda i: (0, 0)
    return pl.pallas_call(
        _trig_kernel,
        grid=(T // tm,),
        in_specs=[
            pl.BlockSpec((tr, LANES), lambda i: (i, 0)),
            pl.BlockSpec((1, LANES), full),
            pl.BlockSpec((tm, tr), full),
            pl.BlockSpec((2 * LANES, 2 * LANES), full),
            pl.BlockSpec((1, LANES), full),
        ],
        out_specs=[pl.BlockSpec((tm, LANES), lambda i: (i, 0))] * 2,
        out_shape=[jax.ShapeDtypeStruct((T, LANES), F32)] * 2,
        compiler_params=_cparams(("arbitrary",)),
        name="rope_trig",
    )(pos_c, freq_c, rsel, fold, cbase)


def _inproj_kernel(x_ref, cos_ref, sin_ref, shift_ref, scale_ref, gmix_ref, win_ref, gq_ref, wuq_ref,
                   gkv_ref, wukv_ref, gqn_ref, gqr_ref, gkn_ref, gkr_ref,
                   cw_ref, cb_ref, wa_ref, ba_ref, wx_ref, bx_ref, lam_ref,
                   lru_ref, q_ref, k_ref, v_ref, tail_ref, carry_ref):
    HP = N_HEADS * HEAD_PAD

    @pl.when(pl.program_id(1) == 0)
    def _():
        tail_ref[...] = jnp.zeros_like(tail_ref)
        carry_ref[...] = jnp.zeros_like(carry_ref)

    x = x_ref[...]
    ms = jnp.mean(x * x, axis=-1, keepdims=True)
    gain = gmix_ref[...] * (1.0 + scale_ref[0])
    h = x * lax.rsqrt(ms + EPS) * gain + shift_ref[0]
    hb = h.astype(BF16)
    o1 = 2 * D_LRU
    o2 = Q_LORA
    o3 = o2 + KV_LORA
    z_lru = _dot(hb, win_ref[:, :o1])
    z = _dot(hb, win_ref[:, o1:])
    lru_ref[...] = _lru_tile(z_lru[:, :D_LRU], z_lru[:, D_LRU:], cw_ref, cb_ref, wa_ref, ba_ref,
                             wx_ref, bx_ref, lam_ref, tail_ref, carry_ref)
    ql = z[:, :o2]
    kvl = z[:, o2:o3]
    kr = z[:, o3:o3 + LANES]
    kr_rot = z[:, o3 + LANES:]

    qn = ql * lax.rsqrt(jnp.mean(ql * ql, axis=-1, keepdims=True) + EPS) * gq_ref[...]
    qq = _dot(qn.astype(BF16), wuq_ref[...])
    kvn = kvl * lax.rsqrt(jnp.mean(kvl * kvl, axis=-1, keepdims=True) + EPS) * gkv_ref[...]
    kv = _dot(kvn.astype(BF16), wukv_ref[...])

    tm = x.shape[0]
    lane = lax.broadcasted_iota(jnp.int32, (tm, HP), 1)
    v_ref[...] = jnp.where((lane & (HEAD_PAD - 1)) == V_HEAD, 1.0, kv[:, HP:]).astype(BF16)

    cos_t = cos_ref[...]
    sin_t = sin_ref[...]
    gqn = gqn_ref[...]
    gkn = gkn_ref[...]
    cq = gqn * cos_t
    sq = gqr_ref[...] * sin_t
    kb = kr * (gkn * cos_t) + kr_rot * (gkr_ref[...] * sin_t)
    inv_w = 1.0 / QK_HEAD
    qscale = QK_HEAD ** -0.5 * LOG2_E
    for hh in range(N_HEADS):
        sl = slice(hh * HEAD_PAD, (hh + 1) * HEAD_PAD)
        qh = qq[:, sl]
        rq = lax.rsqrt(jnp.sum(qh * qh, axis=-1, keepdims=True) * inv_w + EPS) * qscale
        q_ref[:, sl] = ((qh * cq + qq[:, HP + hh * HEAD_PAD:HP + (hh + 1) * HEAD_PAD] * sq) * rq).astype(BF16)
        kraw = kv[:, sl] + kr
        rk = lax.rsqrt(jnp.sum(kraw * kraw, axis=-1, keepdims=True) * inv_w + EPS)
        k_ref[:, sl] = ((kv[:, sl] * gkn + kb) * rk).astype(BF16)


def _inproj(x2, cos_t, sin_t, mod3, g_mix, w_in_p, g_q_lat, w_uq_p, g_kv_lat, w_ukv_p,
            gqn_p, gqr_p, gkn_p, gkr_p, conv_w, conv_b, wa_d, b_a, wx_d, b_x, lam, B, S, tm):
    T, D = x2.shape
    ns = S // tm
    HP = N_HEADS * HEAD_PAD
    C = D_LRU
    row = lambda b, s: (b * ns + s, 0)
    full = lambda b, s: (0, 0)
    return pl.pallas_call(
        _inproj_kernel,
        grid=(B, ns),
        in_specs=[
            pl.BlockSpec((tm, D), row),
            pl.BlockSpec((tm, LANES), row),
            pl.BlockSpec((tm, LANES), row),
            pl.BlockSpec((1, 1, D), lambda b, s: (b * 6 + 0, 0, 0)),
            pl.BlockSpec((1, 1, D), lambda b, s: (b * 6 + 1, 0, 0)),
            pl.BlockSpec((1, D), full),
            pl.BlockSpec((D, D_IN_PAD), full),
            pl.BlockSpec((1, Q_LORA), full),
            pl.BlockSpec((Q_LORA, 2 * HP), full),
            pl.BlockSpec((1, KV_LORA), full),
            pl.BlockSpec((KV_LORA, 2 * HP), full),
            pl.BlockSpec((1, HEAD_PAD), full),
            pl.BlockSpec((1, HEAD_PAD), full),
            pl.BlockSpec((1, HEAD_PAD), full),
            pl.BlockSpec((1, HEAD_PAD), full),
            pl.BlockSpec((CONV_W, C), full),
            pl.BlockSpec((1, C), full),
            pl.BlockSpec((C, C), full),
            pl.BlockSpec((1, C), full),
            pl.BlockSpec((C, C), full),
            pl.BlockSpec((1, C), full),
            pl.BlockSpec((1, C), full),
        ],
        out_specs=[
            pl.BlockSpec((tm, C), row),
            pl.BlockSpec((tm, HP), row),
            pl.BlockSpec((tm, HP), row),
            pl.BlockSpec((tm, HP), row),
        ],
        out_shape=[
            jax.ShapeDtypeStruct((T, C), BF16),
            jax.ShapeDtypeStruct((T, HP), BF16),
            jax.ShapeDtypeStruct((T, HP), BF16),
            jax.ShapeDtypeStruct((T, HP), BF16),
        ],
        scratch_shapes=[pltpu.VMEM((SUBLANES, C), F32), pltpu.VMEM((SUBLANES, C), F32)],
        compiler_params=_cparams(("arbitrary", "arbitrary")),
        name="inproj",
    )(x2, cos_t, sin_t, mod3, mod3, g_mix, w_in_p, g_q_lat, w_uq_p, g_kv_lat, w_ukv_p,
      gqn_p, gqr_p, gkn_p, gkr_p, conv_w, conv_b, wa_d, b_a, wx_d, b_x, lam)


def _gelu_tanh(x):
    c = 0.7978845608028654
    hx = 0.5 * x
    return hx + hx * jnp.tanh(x * (c + (c * 0.044715) * (x * x)))


def _lru_tile(x, y, cw_ref, cb_ref, wa_ref, ba_ref, wx_ref, bx_ref, lam_ref, tail_ref, carry_ref):
    ts = x.shape[0]
    xext = jnp.concatenate([tail_ref[...], x], axis=0)
    cw = cw_ref[...]
    xc = x * cw[CONV_W - 1:CONV_W, :]
    for j in range(CONV_W - 1):
        sh = CONV_W - 1 - j
        xc = xc + xext[8 - sh:8 - sh + ts, :] * cw[j:j + 1, :]
    xc = xc + cb_ref[...]
    tail_ref[...] = x[ts - 8:, :]

    xb = xc.astype(BF16)
    r = _sigmoid(_dot(xb, wa_ref[...]) + ba_ref[...])
    i = _sigmoid(_dot(xb, wx_ref[...]) + bx_ref[...])
    lam = lam_ref[...]
    nl = -lam
    softplus = jnp.maximum(nl, 0.0) + jnp.log(1.0 + jnp.exp(-jnp.abs(nl)))
    log_a = (-LRU_C) * r * softplus
    a = jnp.exp(log_a)
    mult = jnp.sqrt(1.0 - a * a)
    u = mult * (i * xc)

    C = a.shape[1]
    a = a.reshape(ts // SUBLANES, SUBLANES, C)
    u = u.reshape(ts // SUBLANES, SUBLANES, C)
    sub = lax.broadcasted_iota(jnp.int32, (1, SUBLANES, 1), 1)
    sh = 1
    while sh < SUBLANES:
        a_prev = pltpu.roll(a, sh, axis=1)
        u_prev = pltpu.roll(u, sh, axis=1)
        m = sub >= sh
        u = jnp.where(m, a * u_prev + u, u)
        a = jnp.where(m, a * a_prev, a)
        sh *= 2
    a = a.reshape(ts, C)
    u = u.reshape(ts, C)
    h = carry_ref[0:1, :]
    groups = []
    for g0 in range(0, ts, SUBLANES):
        hg = u[g0:g0 + SUBLANES, :] + a[g0:g0 + SUBLANES, :] * h
        groups.append(hg)
        h = hg[SUBLANES - 1:SUBLANES, :]
    carry_ref[...] = jnp.broadcast_to(h, carry_ref.shape)
    hs = jnp.concatenate(groups, axis=0)
    return (_gelu_tanh(y) * hs).astype(BF16)


NEG_INF = -1e30


def _attn_kernel(q_ref, k_ref, v_ref, o_ref, *state, tq):
    m_refs = state[:N_HEADS]
    acc_refs = state[N_HEADS:]
    qi = pl.program_id(1)
    rowi = lax.broadcasted_iota(jnp.int32, (tq, tq), 0)
    coli = lax.broadcasted_iota(jnp.int32, (tq, tq), 1)
    diag_mask = coli <= rowi

    def head_slice(hh):
        return slice(hh * HEAD_PAD, (hh + 1) * HEAD_PAD)

    def weights(sc, m_b):
        cols = [jnp.exp2(sc[:, c0:c0 + LANES] - m_b) for c0 in range(0, tq, LANES)]
        return jnp.concatenate(cols, axis=1).astype(BF16)

    def scores(hh, r0):
        hs = head_slice(hh)
        return _dot_nt(q_ref[:, hs], k_ref[pl.ds(r0, tq), hs])

    r_diag = pl.multiple_of(qi * tq, tq)
    sc_next = scores(0, r_diag)
    for hh in range(N_HEADS):
        hs = head_slice(hh)
        sc = jnp.where(diag_mask, sc_next, NEG_INF)
        if hh + 1 < N_HEADS:
            sc_next = scores(hh + 1, r_diag)
        m_b = jnp.broadcast_to(jnp.max(sc, axis=-1, keepdims=True), (tq, LANES))
        m_refs[hh][...] = m_b
        acc_refs[hh][...] = _dot(weights(sc, m_b), v_ref[pl.ds(r_diag, tq), hs])

    @pl.loop(0, qi)
    def _(j):
        r0 = pl.multiple_of(j * tq, tq)
        sc_next = scores(0, r0)
        for hh in range(N_HEADS):
            hs = head_slice(hh)
            sc = sc_next
            if hh + 1 < N_HEADS:
                sc_next = scores(hh + 1, r0)
            m_b = m_refs[hh][...]
            m_new = jnp.maximum(m_b, jnp.max(sc, axis=-1, keepdims=True))
            alpha = jnp.exp2(m_b - m_new)
            m_refs[hh][...] = m_new
            acc_refs[hh][...] = (alpha * acc_refs[hh][...]
                                 + _dot(weights(sc, m_new), v_ref[pl.ds(r0, tq), hs]))

    for hh in range(N_HEADS):
        acc = acc_refs[hh][...]
        o = acc[:, :V_HEAD] / acc[:, SUM_LANE:SUM_LANE + 1]
        o_ref[:, hh * V_HEAD:(hh + 1) * V_HEAD] = o.astype(BF16)


def _attn(qp, kp, v, B, S, tq):
    T = qp.shape[0]
    nq = S // tq
    HP = N_HEADS * HEAD_PAD
    HV = N_HEADS * V_HEAD
    return pl.pallas_call(
        functools.partial(_attn_kernel, tq=tq),
        grid=(B, nq),
        in_specs=[
            pl.BlockSpec((tq, HP), lambda b, i: (b * nq + i, 0)),
            pl.BlockSpec((S, HP), lambda b, i: (b, 0)),
            pl.BlockSpec((S, HP), lambda b, i: (b, 0)),
        ],
        out_specs=pl.BlockSpec((tq, HV), lambda b, i: (b * nq + i, 0)),
        out_shape=jax.ShapeDtypeStruct((T, HV), BF16),
        scratch_shapes=([pltpu.VMEM((tq, LANES), F32)] * N_HEADS
                        + [pltpu.VMEM((tq, HEAD_PAD), F32)] * N_HEADS),
        compiler_params=_cparams(("arbitrary", "arbitrary")),
        name="attn",
    )(qp, kp, v)


def _outproj_kernel(lru_ref, att_ref, x_ref, gate_ref, shift_ref, scale_ref, gffn_ref,
                    wo1_ref, wo2_ref, wr_ref, br_ref, tri_ref,
                    x1_ref, h2p_ref, idx_ref, gat_ref, rank_ref, cnt_ref, run_ref):
    first = (pl.program_id(0) == 0) & (pl.program_id(1) == 0)

    @pl.when(first)
    def _():
        run_ref[...] = jnp.zeros_like(run_ref)

    mix = _dot(lru_ref[...], wo1_ref[...]) + _dot(att_ref[...], wo2_ref[...])
    x1 = x_ref[...] + gate_ref[0] * mix
    x1_ref[...] = x1
    ms = jnp.mean(x1 * x1, axis=-1, keepdims=True)
    gain = gffn_ref[...] * (1.0 + scale_ref[0])
    h2 = x1 * lax.rsqrt(ms + EPS) * gain + shift_ref[0]

    hhi = h2.astype(BF16)
    hlo = (h2 - hhi.astype(F32)).astype(BF16)
    h2p_ref[...] = _pack_halves(h2)

    ne = br_ref.shape[0]
    stacked = _dot_nt(wr_ref[...], hhi)
    logits = stacked[:ne] + stacked[ne:] + _dot_nt(wr_ref[:ne, :], hlo) + br_ref[...]

    tm = logits.shape[1]
    eio = lax.broadcasted_iota(jnp.int32, (ne, tm), 0)
    vals, idxs, sels = [], [], []
    l = logits
    for _ in range(TOP_K):
        m = jnp.max(l, axis=0, keepdims=True)
        idx = jnp.min(jnp.where(l == m, eio, ne), axis=0, keepdims=True)
        sel = eio == idx
        l = jnp.where(sel, -jnp.inf, l)
        vals.append(m)
        idxs.append(idx)
        sels.append(sel)
    es = [jnp.exp(v - vals[0]) for v in vals]
    inv = 1.0 / (es[0] + es[1] + es[2] + es[3])
    sel_any = jnp.where(sels[0] | sels[1] | sels[2] | sels[3], 1.0, 0.0)
    run = run_ref[...]
    excl = _dot(sel_any.astype(BF16), tri_ref[...]) + run
    for kk in range(TOP_K):
        idx_ref[kk:kk + 1, :] = idxs[kk]
        gat_ref[kk:kk + 1, :] = es[kk] * inv
        rk = jnp.sum(jnp.where(sels[kk], excl, 0.0), axis=0, keepdims=True)
        rank_ref[kk:kk + 1, :] = rk.astype(jnp.int32)
    run = run + jnp.sum(sel_any, axis=1, keepdims=True)
    run_ref[...] = run
    cnt_ref[...] = run.astype(jnp.int32)


def _outproj(lru_o, att_o, x2, mod3, g_ffn, wo1, wo2, wr_stack, b_r, tri, b0, B, S, tm):
    D = x2.shape[1]
    T = B * S
    ns = S // tm
    C = lru_o.shape[1]
    row_in = lambda b, s: ((b0 + b) * ns + s, 0)
    row = lambda b, s: (b * ns + s, 0)
    col = lambda b, s: (0, b * ns + s)
    full = lambda b, s: (0, 0)
    return pl.pallas_call(
        _outproj_kernel,
        grid=(B, ns),
        in_specs=[
            pl.BlockSpec((tm, C), row_in),
            pl.BlockSpec((tm, C), row_in),
            pl.BlockSpec((tm, D), row_in),
            pl.BlockSpec((1, 1, D), lambda b, s: ((b0 + b) * 6 + 2, 0, 0)),
            pl.BlockSpec((1, 1, D), lambda b, s: ((b0 + b) * 6 + 3, 0, 0)),
            pl.BlockSpec((1, 1, D), lambda b, s: ((b0 + b) * 6 + 4, 0, 0)),
            pl.BlockSpec((1, D), full),
            pl.BlockSpec((C, D), full),
            pl.BlockSpec((C, D), full),
            pl.BlockSpec((2 * N_EXPERTS, D), full),
            pl.BlockSpec((N_EXPERTS, 1), full),
            pl.BlockSpec(tri.shape, full),
        ],
        out_specs=[
            pl.BlockSpec((tm, D), row),
            pl.BlockSpec((tm, D // 2), row),
            pl.BlockSpec((TOP_K, tm), col),
            pl.BlockSpec((TOP_K, tm), col),
            pl.BlockSpec((TOP_K, tm), col),
            pl.BlockSpec((N_EXPERTS, 1), full),
        ],
        out_shape=[
            jax.ShapeDtypeStruct((T, D), F32),
            jax.ShapeDtypeStruct((T, D // 2), jnp.int32),
            jax.ShapeDtypeStruct((TOP_K, T), jnp.int32),
            jax.ShapeDtypeStruct((TOP_K, T), F32),
            jax.ShapeDtypeStruct((TOP_K, T), jnp.int32),
            jax.ShapeDtypeStruct((N_EXPERTS, 1), jnp.int32),
        ],
        scratch_shapes=[pltpu.VMEM((N_EXPERTS, 1), F32)],
        compiler_params=_cparams(("arbitrary", "arbitrary")),
        name="outproj",
    )(lru_o, att_o, x2, mod3, mod3, mod3, g_ffn, wo1, wo2, wr_stack, b_r, tri)


SC_CORES = 2
SC_SUBCORES = 16
SC_WORKERS = SC_CORES * SC_SUBCORES
SC_LANES = 16
GATES_PER_ROW = LANES // SC_LANES


def _sc_mesh():
    return plsc.VectorSubcoreMesh(core_axis_name="c", subcore_axis_name="s",
                                  num_cores=SC_CORES, num_subcores=SC_SUBCORES)


def _sc_worker_id():
    return lax.axis_index("s") * SC_CORES + lax.axis_index("c")


def _sc_scatter_rows(rows, idx, n_out, g):
    T, W = rows.shape
    K = idx.shape[0]
    per_w = T // SC_WORKERS
    nch = per_w // g
    assert per_w * SC_WORKERS == T and nch * g == per_w and nch % 2 == 0
    idx_w = idx.reshape(K, SC_WORKERS, nch, g).transpose(1, 2, 0, 3).reshape(SC_WORKERS, nch * K, g)

    def body(rows_hbm, idx_hbm, out_hbm, idx_v, buf0, buf1, semr0, semr1, semw):
        wid = _sc_worker_id()
        base = wid * per_w
        pltpu.sync_copy(idx_hbm.at[wid], idx_v)

        def read(j, buf, sem):
            return pltpu.make_async_copy(rows_hbm.at[pl.ds(base + j * g, g)], buf, sem)

        def scatter(j, buf):
            copies = [pltpu.async_copy(buf, out_hbm.at[idx_v.at[j * K + kk]], semw)
                      for kk in range(K)]
            for cp in copies:
                cp.wait()

        read(0, buf0, semr0).start()

        @pl.loop(0, nch // 2)
        def _(jj):
            j0 = 2 * jj
            read(j0 + 1, buf1, semr1).start()
            read(j0, buf0, semr0).wait()
            scatter(j0, buf0)

            @pl.when(j0 + 2 < nch)
            def _():
                read(j0 + 2, buf0, semr0).start()

            read(j0 + 1, buf1, semr1).wait()
            scatter(j0 + 1, buf1)

    return pl.kernel(
        body,
        out_type=jax.ShapeDtypeStruct((n_out, W), rows.dtype),
        mesh=_sc_mesh(),
        scratch_types=[
            pltpu.VMEM((nch * K, g), jnp.int32),
            pltpu.VMEM((g, W), rows.dtype),
            pltpu.VMEM((g, W), rows.dtype),
            pltpu.SemaphoreType.DMA,
            pltpu.SemaphoreType.DMA,
            pltpu.SemaphoreType.DMA,
        ],
        name="sc_scatter_rows",
    )(rows, idx_w)


def _sc_gather_gated_sum(table, idx, gates, g):
    W = table.shape[1]
    K, T = idx.shape
    per_w = T // SC_WORKERS
    nch = per_w // g
    assert per_w * SC_WORKERS == T and nch * g == per_w and nch % 2 == 0 and W % SC_LANES == 0
    idx_w = idx.reshape(K, SC_WORKERS, nch, g).transpose(1, 2, 0, 3).reshape(SC_WORKERS, nch * K, g)
    gates_w = jnp.broadcast_to(
        gates.reshape(K, SC_WORKERS, nch, g).transpose(1, 2, 0, 3)[..., None],
        (SC_WORKERS, nch, K, g, SC_LANES)).reshape(SC_WORKERS, nch * K, g // GATES_PER_ROW, LANES)
    hi_mask = jnp.int32(-65536)

    def body(table_hbm, idx_hbm, gates_hbm, out_hbm, idx_v, rows_v, gts_v, out_v, sem0, sem1):
        wid = _sc_worker_id()
        base = wid * per_w
        pltpu.sync_copy(idx_hbm.at[wid], idx_v)

        def fetch(j, slot, sem):
            cps = []
            for kk in range(K):
                cps.append(pltpu.make_async_copy(table_hbm.at[idx_v.at[j * K + kk]],
                                                 rows_v.at[slot, kk], sem))
                cps.append(pltpu.make_async_copy(gates_hbm.at[wid, j * K + kk],
                                                 gts_v.at[slot, kk], sem))
            return cps

        def start(j, slot, sem):
            for cp in fetch(j, slot, sem):
                cp.start()

        def finish(j, slot, sem):
            for cp in fetch(j, slot, sem):
                cp.wait()

            @pl.loop(0, g)
            def _(t):
                g_off = pl.multiple_of((t % GATES_PER_ROW) * SC_LANES, SC_LANES)
                gk = [gts_v[slot, kk, t // GATES_PER_ROW, pl.ds(g_off, SC_LANES)] for kk in range(K)]

                @pl.loop(0, W // SC_LANES)
                def _(c):
                    off = pl.multiple_of(c * SC_LANES, SC_LANES)
                    acc_lo = jnp.zeros((SC_LANES,), F32)
                    acc_hi = jnp.zeros((SC_LANES,), F32)
                    for kk in range(K):
                        w = rows_v[slot, kk, t, pl.ds(off, SC_LANES)]
                        lo = lax.bitcast_convert_type(w << 16, F32)
                        hi = lax.bitcast_convert_type(w & hi_mask, F32)
                        acc_lo = acc_lo + gk[kk] * lo
                        acc_hi = acc_hi + gk[kk] * hi
                    out_v[slot, t, pl.ds(off, SC_LANES)] = acc_lo
                    out_v[slot, t, pl.ds(W + off, SC_LANES)] = acc_hi

            pltpu.sync_copy(out_v.at[slot], out_hbm.at[pl.ds(base + j * g, g)])

        start(0, 0, sem0)

        @pl.loop(0, nch // 2)
        def _(jj):
            j0 = 2 * jj
            start(j0 + 1, 1, sem1)
            finish(j0, 0, sem0)

            @pl.when(j0 + 2 < nch)
            def _():
                start(j0 + 2, 0, sem0)

            finish(j0 + 1, 1, sem1)

    return pl.kernel(
        body,
        out_type=jax.ShapeDtypeStruct((T, 2 * W), F32),
        mesh=_sc_mesh(),
        scratch_types=[
            pltpu.VMEM((nch * K, g), jnp.int32),
            pltpu.VMEM((2, K, g, W), jnp.int32),
            pltpu.VMEM((2, K, g // GATES_PER_ROW, LANES), F32),
            pltpu.VMEM((2, g, 2 * W), F32),
            pltpu.SemaphoreType.DMA,
            pltpu.SemaphoreType.DMA,
        ],
        compiler_params=pltpu.CompilerParams(needs_layout_passes=False),
        name="sc_gather_gated_sum",
    )(table, idx_w, gates_w)


def _experts_kernel(be_ref, bv_ref, bf_ref, bn_ref, bs_ref, xs_ref, w1_hbm, b1_ref, w2_hbm, b2_ref,
                    ys_ref, w1s_ref, w2s_ref, w1b_ref, w2b_ref, sem):
    i = pl.program_id(0)
    nvalid = bv_ref[i]

    def weight_copies(e, slot):
        return (pltpu.make_async_copy(w1_hbm.at[e], w1s_ref.at[slot], sem.at[0, slot]),
                pltpu.make_async_copy(w2_hbm.at[e], w2s_ref.at[slot], sem.at[1, slot]))

    @pl.when(bf_ref[i] > 0)
    def _():
        slot = bs_ref[i]

        @pl.when(i == 0)
        def _():
            for cp in weight_copies(be_ref[0], slot):
                cp.start()

        for cp in weight_copies(be_ref[i], slot):
            cp.wait()
        w1b_ref[...] = w1s_ref[slot].astype(BF16)
        w2b_ref[...] = w2s_ref[slot].astype(BF16)

        @pl.when(bn_ref[i] >= 0)
        def _():
            for cp in weight_copies(bn_ref[i], 1 - slot):
                cp.start()

    def ffn(rows):
        xw = xs_ref[:rows, :]
        rowi = lax.broadcasted_iota(jnp.int32, (rows, 1), 0)
        lo, hi = _unpack_halves(jnp.where(rowi < nvalid, xw, 0))
        xb = jnp.concatenate([lo.astype(BF16), hi.astype(BF16)], axis=1)
        gu = _dot(xb, w1b_ref[...]) + b1_ref[0]
        glu = jnp.minimum(gu[:, :D_FF], SWIGLU_LIMIT)
        lin = jnp.clip(gu[:, D_FF:], -SWIGLU_LIMIT, SWIGLU_LIMIT)
        act = (lin + 1.0) * (glu * _sigmoid(glu, SWIGLU_ALPHA))
        ys_ref[:rows, :] = _pack_halves(_dot(act.astype(BF16), w2b_ref[...]) + b2_ref[0])

    step = xs_ref.shape[0] // ROW_QUARTERS
    for nq in range(1, ROW_QUARTERS + 1):
        pl.when((nvalid > (nq - 1) * step) & (nvalid <= nq * step))(functools.partial(ffn, nq * step))


def _experts(blk_e, blk_v, blk_f, blk_n, blk_s, blk_r, xs, w1, b1, w2, b2):
    P, W = xs.shape
    nb = P // MOE_BLOCK
    E, D, F2 = w1.shape
    grid_spec = pltpu.PrefetchScalarGridSpec(
        num_scalar_prefetch=6,
        grid=(nb,),
        in_specs=[
            pl.BlockSpec((MOE_BLOCK, W), lambda i, be, bv, bf, bn, bs, br: (br[i], 0)),
            pl.BlockSpec(memory_space=pl.ANY),
            pl.BlockSpec((1, 1, F2), lambda i, be, bv, bf, bn, bs, br: (be[i], 0, 0)),
            pl.BlockSpec(memory_space=pl.ANY),
            pl.BlockSpec((1, 1, D), lambda i, be, bv, bf, bn, bs, br: (be[i], 0, 0)),
        ],
        out_specs=pl.BlockSpec((MOE_BLOCK, D // 2), lambda i, be, bv, bf, bn, bs, br: (br[i], 0)),
        scratch_shapes=[
            pltpu.VMEM((2, D, F2), F32),
            pltpu.VMEM((2, D_FF, D), F32),
            pltpu.VMEM((D, F2), BF16),
            pltpu.VMEM((D_FF, D), BF16),
            pltpu.SemaphoreType.DMA((2, 2)),
        ],
    )

    def kern(be_ref, bv_ref, bf_ref, bn_ref, bs_ref, br_ref, *refs):
        del br_ref
        _experts_kernel(be_ref, bv_ref, bf_ref, bn_ref, bs_ref, *refs)

    return pl.pallas_call(
        kern,
        grid_spec=grid_spec,
        out_shape=jax.ShapeDtypeStruct((P, D // 2), jnp.int32),
        compiler_params=_cparams(("arbitrary",)),
        name="experts",
    )(blk_e, blk_v, blk_f, blk_n, blk_s, blk_r, xs, w1, b1.reshape(E, 1, F2), w2,
      b2.reshape(E, 1, D))


def _combine_kernel(x1_ref, gate_ref, ysum_ref, *rest):
    o_ref = rest[-1]
    o_ref[...] = x1_ref[...] + gate_ref[0] * ysum_ref[...]


def _combine(x1, mod3, ysum, out_prev, b0, B, nb_total, S, tm):
    D = x1.shape[1]
    ns = S // tm
    row = lambda b, s: (b * ns + s, 0)
    row_out = lambda b, s: ((b0 + b) * ns + s, 0)
    in_specs = [
        pl.BlockSpec((tm, D), row),
        pl.BlockSpec((1, 1, D), lambda b, s: ((b0 + b) * 6 + 5, 0, 0)),
        pl.BlockSpec((tm, D), row),
    ]
    args = [x1, mod3, ysum]
    aliases = {}
    if out_prev is not None:
        in_specs.append(pl.BlockSpec(memory_space=pl.ANY))
        args.append(out_prev)
        aliases = {len(args) - 1: 0}
    return pl.pallas_call(
        _combine_kernel,
        grid=(B, ns),
        in_specs=in_specs,
        out_specs=pl.BlockSpec((tm, D), row_out),
        out_shape=jax.ShapeDtypeStruct((nb_total * S, D), F32),
        input_output_aliases=aliases,
        compiler_params=_cparams(("arbitrary", "arbitrary")),
        name="combine",
    )(*args)


def _block_diag(w):
    n, c, d = w.shape
    eye = jnp.eye(n, dtype=w.dtype)
    return jnp.einsum("ncd,nm->ncmd", w, eye).reshape(n * c, n * d)


def _pad_heads(w, width):
    k = w.shape[0]
    w = w.reshape(k, N_HEADS, width)
    return jnp.pad(w, ((0, 0), (0, 0), (0, HEAD_PAD - width))).reshape(k, N_HEADS * HEAD_PAD)


def kernel(x, c, positions, w_ada, b_ada, g_mix, w_in, conv_w, conv_b, w_a, b_a, w_x, b_x, lam,
           g_q_lat, w_uq, g_kv_lat, w_ukv, g_qn, g_kn, w_out, g_ffn, w_router, b_router,
           w1, b1, w2, b2):
    B, S, D = x.shape
    T = B * S
    depth = w_ada.shape[0]
    tm_in = min(512, S)
    tq = min(512, S)
    tm_out = min(512, S)
    tm_comb = min(512, S)
    n_groups = 2 if B % 2 == 0 else 1
    Bg = B // n_groups
    Tg = Bg * S
    g_disp = min(64, Tg // SC_WORKERS // 2)
    g_comb = min(16, Tg // SC_WORKERS // 2)

    o1 = 2 * D_LRU
    o2 = o1 + Q_LORA
    o3 = o2 + KV_LORA
    tri = (jnp.arange(tm_out)[:, None] < jnp.arange(tm_out)[None, :]).astype(BF16)
    cos_t, sin_t = _rope_tables(positions)
    lane = jnp.arange(HEAD_PAD, dtype=jnp.int32)
    first = (lane >= ROPE_LO) & (lane < ROPE_LO + ROPE_HALF)
    second = (lane >= ROPE_LO + ROPE_HALF) & (lane < ROPE_LO + QK_ROPE)
    partner = jnp.where(first, lane + ROPE_HALF, jnp.where(second, lane - ROPE_HALF, 0))
    is_rot = first | second

    def rot_cols(w):
        k = w.shape[0]
        w3 = w.reshape(k, -1, HEAD_PAD)
        return jnp.where(is_rot[None, None, :], w3[:, :, partner], 0.0).reshape(w.shape)

    x2 = x.reshape(T, D)
    for l in range(depth):
        mod3 = _ada(c, w_ada[l], b_ada[l]).reshape(B * 6, 1, D)

        w_in_l = w_in[l]
        kr_cols = jnp.pad(w_in_l[:, o3:], ((0, 0), (ROPE_LO, LANES - ROPE_LO - QK_ROPE)))
        w_in_p = jnp.concatenate([w_in_l[:, :o3], kr_cols, rot_cols(kr_cols)], axis=1).astype(BF16)
        w_uq_h = _pad_heads(w_uq[l], QK_HEAD)
        w_uq_p = jnp.concatenate([w_uq_h, rot_cols(w_uq_h)], axis=1).astype(BF16)
        w_ukv_l = w_ukv[l].reshape(KV_LORA, N_HEADS, QK_NOPE + V_HEAD)
        w_uk_h = _pad_heads(w_ukv_l[:, :, :QK_NOPE].reshape(KV_LORA, N_HEADS * QK_NOPE), QK_NOPE)
        w_uv_h = _pad_heads(w_ukv_l[:, :, QK_NOPE:].reshape(KV_LORA, N_HEADS * V_HEAD), V_HEAD)
        w_ukv_p = jnp.concatenate([w_uk_h, w_uv_h], axis=1).astype(BF16)
        gqn_p = jnp.pad(g_qn[l], (0, HEAD_PAD - QK_HEAD)).reshape(1, HEAD_PAD)
        gkn_p = jnp.pad(g_kn[l], (0, HEAD_PAD - QK_HEAD)).reshape(1, HEAD_PAD)

        lru_o, qp, kp, v = _inproj(
            x2, cos_t, sin_t, mod3, g_mix[l].reshape(1, D), w_in_p, g_q_lat[l].reshape(1, Q_LORA),
            w_uq_p, g_kv_lat[l].reshape(1, KV_LORA), w_ukv_p, gqn_p, rot_cols(gqn_p), gkn_p,
            rot_cols(gkn_p), conv_w[l], conv_b[l].reshape(1, D_LRU),
            _block_diag(w_a[l]).astype(BF16), b_a[l].reshape(1, D_LRU),
            _block_diag(w_x[l]).astype(BF16), b_x[l].reshape(1, D_LRU),
            lam[l].reshape(1, D_LRU), B, S, tm_in)

        att_o = _attn(qp, kp, v, B, S, tq)

        w_out_b = w_out[l].astype(BF16)
        wr_stack = jnp.concatenate(_split_bf16(w_router[l].T), axis=0)
        g_ffn_l = g_ffn[l].reshape(1, D)
        b_r = b_router[l].reshape(N_EXPERTS, 1)
        eio = jnp.arange(N_EXPERTS, dtype=jnp.int32)
        n_blocks = -(-(Tg * TOP_K) // MOE_BLOCK) + N_EXPERTS
        bi = jnp.arange(n_blocks, dtype=jnp.int32)

        x_next = None
        for gi in range(n_groups):
            b0 = gi * Bg
            x1, h2p, idx_t, gat_t, rank_t, counts = _outproj(
                lru_o, att_o, x2, mod3, g_ffn_l, w_out_b[:D_LRU], w_out_b[D_LRU:],
                wr_stack, b_r, tri, b0, Bg, S, tm_out)

            counts = counts.reshape(N_EXPERTS)
            nblk_e = (counts + MOE_BLOCK - 1) // MOE_BLOCK
            blk_end = jnp.cumsum(nblk_e)
            pad_start = (blk_end - nblk_e) * MOE_BLOCK
            total = blk_end[-1]
            blk_r = jnp.minimum(bi, total - 1).astype(jnp.int32)
            blk_e = jnp.minimum(jnp.sum(blk_end[None, :] <= blk_r[:, None], axis=1),
                                N_EXPERTS - 1).astype(jnp.int32)
            blk_onehot = blk_e[:, None] == eio[None, :]
            blk_first = jnp.sum(jnp.where(blk_onehot, (blk_end - nblk_e)[None, :], 0), axis=1)
            blk_cnt = jnp.sum(jnp.where(blk_onehot, counts[None, :], 0), axis=1)
            blk_v = jnp.where(bi < total,
                              jnp.clip(blk_cnt - (bi - blk_first) * MOE_BLOCK, 0, MOE_BLOCK),
                              0).astype(jnp.int32)
            blk_f = ((bi == blk_first) & (bi < total)).astype(jnp.int32)
            nxt_first = jnp.sum(jnp.where(blk_onehot, blk_end[None, :], 0), axis=1)
            nxt_e = jnp.minimum(jnp.sum(blk_end[None, :] <= nxt_first[:, None], axis=1),
                                N_EXPERTS - 1)
            blk_n = jnp.where(nxt_first < total, nxt_e, -1).astype(jnp.int32)
            ordinal = jnp.cumsum((nblk_e > 0).astype(jnp.int32)) - 1
            blk_s = (jnp.sum(jnp.where(blk_onehot, ordinal[None, :], 0), axis=1) % 2).astype(jnp.int32)
            slot0 = jnp.sum(jnp.where(idx_t[None] == eio[:, None, None],
                                      pad_start[:, None, None], 0), axis=0)
            dest = slot0.astype(jnp.int32) + rank_t

            xs = _sc_scatter_rows(h2p, dest, n_blocks * MOE_BLOCK, g_disp)
            ys = _experts(blk_e, blk_v, blk_f, blk_n, blk_s, blk_r, xs, w1[l], b1[l], w2[l], b2[l])
            ysum = _sc_gather_gated_sum(ys, dest, gat_t, g_comb)
            x_next = _combine(x1, mod3, ysum, x_next, b0, Bg, B, S, tm_comb)
        x2 = x_next
    return x2.reshape(B, S, D)
```

```python
import functools

import jax
import jax.numpy as jnp
from jax import lax
from jax.experimental import pallas as pl
from jax.experimental.pallas import tpu as pltpu
from jax.experimental.pallas import tpu_sc as plsc

D_MODEL = 1024
D_LRU = 512
LRU_BLOCKS = 8
LRU_BD = 64
CONV_W = 4
LRU_C = 8.0
N_HEADS = 8
QK_NOPE = 64
QK_ROPE = 32
QK_HEAD = 96
V_HEAD = 64
Q_LORA = 256
KV_LORA = 128
ROPE_THETA = 10000.0
N_EXPERTS = 32
TOP_K = 4
D_FF = 1024
SWIGLU_LIMIT = 7.0
SWIGLU_ALPHA = 1.702
MOE_BLOCK = 512
EPS = 1e-6

LANES = 128
SUBLANES = 8
HEAD_PAD = 128
ROPE_LO = QK_NOPE
ROPE_HALF = QK_ROPE // 2
TOK_PER_ROW = LANES // ROPE_HALF
D_IN_PAD = 2 * D_LRU + Q_LORA + KV_LORA + 2 * LANES
LOG2_E = 1.4426950408889634
ROW_QUARTERS = 4

VMEM_LIMIT = 56 * 1024 * 1024

F32 = jnp.float32
BF16 = jnp.bfloat16


def _cparams(sem):
    return pltpu.CompilerParams(dimension_semantics=sem, vmem_limit_bytes=VMEM_LIMIT)


def _dot(a, b):
    return jnp.dot(a, b, preferred_element_type=F32)


def _dot_nt(a, b):
    return lax.dot_general(a, b, (((1,), (1,)), ((), ())), preferred_element_type=F32)


def _split_bf16(a):
    hi = a.astype(BF16)
    lo = (a - hi.astype(F32)).astype(BF16)
    return hi, lo


def _sigmoid(x, scale=1.0):
    return 1.0 / (1.0 + jnp.exp2(x * (-scale * LOG2_E)))


def _pack_halves(x):
    bits = lax.bitcast_convert_type(x.astype(BF16).astype(F32), jnp.uint32)
    half = x.shape[1] // 2
    words = (bits[:, :half] >> 16) | (bits[:, half:] & jnp.uint32(0xFFFF0000))
    return lax.bitcast_convert_type(words, jnp.int32)


def _unpack_halves(words):
    w = lax.bitcast_convert_type(words, jnp.uint32)
    lo = lax.bitcast_convert_type(w << 16, F32)
    hi = lax.bitcast_convert_type(w & jnp.uint32(0xFFFF0000), F32)
    return lo, hi


def _ada_kernel(c_ref, w_ref, b_ref, o_ref):
    c = c_ref[...]
    s = c * _sigmoid(c)
    shi, slo = _split_bf16(s)
    whi, wlo = _split_bf16(w_ref[...])
    o_ref[...] = _dot(shi, whi) + _dot(slo, whi) + _dot(shi, wlo) + b_ref[...]


def _ada(c, w_ada, b_ada):
    B, D = c.shape
    N = w_ada.shape[1]
    tn = 1024
    return pl.pallas_call(
        _ada_kernel,
        grid=(N // tn,),
        in_specs=[
            pl.BlockSpec((B, D), lambda j: (0, 0)),
            pl.BlockSpec((D, tn), lambda j: (0, j)),
            pl.BlockSpec((1, tn), lambda j: (0, j)),
        ],
        out_specs=pl.BlockSpec((B, tn), lambda j: (0, j)),
        out_shape=jax.ShapeDtypeStruct((B, N), F32),
        compiler_params=_cparams(("arbitrary",)),
        name="ada",
    )(c, w_ada, b_ada.reshape(1, N))


def _trig_kernel(pos_ref, freq_ref, rsel_ref, fold_ref, cbase_ref, cos_ref, sin_ref):
    ang = pos_ref[...].astype(F32) * freq_ref[...]
    cs = jnp.concatenate([jnp.cos(ang), jnp.sin(ang)], axis=1)
    tm = cos_ref.shape[0]
    row = lax.broadcasted_iota(jnp.int32, (tm, 2 * LANES), 0)
    lane = lax.broadcasted_iota(jnp.int32, (tm, 2 * LANES), 1)
    own = ((lane % LANES) // ROPE_HALF) == (row % TOK_PER_ROW)
    rsel = rsel_ref[...]
    fold = fold_ref[...]
    by_row = sum(_dot(rsel, part) for part in _split_bf16(cs))
    mine = jnp.where(own, by_row, 0.0)
    out = sum(_dot(part, fold) for part in _split_bf16(mine))
    cos_ref[...] = out[:, :LANES] + cbase_ref[...]
    sin_ref[...] = out[:, LANES:]


def _rope_tables(positions):
    T = positions.size
    freqs = ROPE_THETA ** (-jnp.arange(ROPE_HALF, dtype=F32) / ROPE_HALF)
    rows = T // TOK_PER_ROW
    pos_c = jnp.repeat(positions.reshape(T).astype(jnp.int32), ROPE_HALF).reshape(rows, LANES)
    freq_c = jnp.tile(freqs, TOK_PER_ROW).reshape(1, LANES)
    tm = min(2048, T)
    tr = tm // TOK_PER_ROW
    rsel = (jnp.arange(tm)[:, None] // TOK_PER_ROW == jnp.arange(tr)[None, :]).astype(BF16)
    src = jnp.arange(LANES)[:, None] % ROPE_HALF
    dst = jnp.arange(LANES)[None, :]
    first = dst == ROPE_LO + src
    second = dst == ROPE_LO + ROPE_HALF + src
    fcos = (first | second).astype(F32)
    fsin = second.astype(F32) - first.astype(F32)
    zero = jnp.zeros((LANES, LANES), F32)
    fold = jnp.block([[fcos, zero], [zero, fsin]]).astype(BF16)
    lane = jnp.arange(LANES)
    cbase = ((lane < ROPE_LO) | (lane >= ROPE_LO + QK_ROPE)).astype(F32).reshape(1, LANES)
    full = lambda i: (0, 0)
    return pl.pallas_call(
        _trig_kernel,
        grid=(T // tm,),
        in_specs=[
            pl.BlockSpec((tr, LANES), lambda i: (i, 0)),
            pl.BlockSpec((1, LANES), full),
            pl.BlockSpec((tm, tr), full),
            pl.BlockSpec((2 * LANES, 2 * LANES), full),
            pl.BlockSpec((1, LANES), full),
        ],
        out_specs=[pl.BlockSpec((tm, LANES), lambda i: (i, 0))] * 2,
        out_shape=[jax.ShapeDtypeStruct((T, LANES), F32)] * 2,
        compiler_params=_cparams(("arbitrary",)),
        name="rope_trig",
    )(pos_c, freq_c, rsel, fold, cbase)


def _inproj_kernel(x_ref, cos_ref, sin_ref, shift_ref, scale_ref, gmix_ref, win_ref, gq_ref, wuq_ref,
                   gkv_ref, wukv_ref, gqn_ref, gqr_ref, gkn_ref, gkr_ref,
                   cw_ref, cb_ref, wa_ref, ba_ref, wx_ref, bx_ref, lam_ref,
                   lru_ref, q_ref, k_ref, v_ref, tail_ref, carry_ref):
    HP = N_HEADS * HEAD_PAD

    @pl.when(pl.program_id(1) == 0)
    def _():
        tail_ref[...] = jnp.zeros_like(tail_ref)
        carry_ref[...] = jnp.zeros_like(carry_ref)

    x = x_ref[...]
    ms = jnp.mean(x * x, axis=-1, keepdims=True)
    gain = gmix_ref[...] * (1.0 + scale_ref[0])
    h = x * lax.rsqrt(ms + EPS) * gain + shift_ref[0]
    hb = h.astype(BF16)
    o1 = 2 * D_LRU
    o2 = Q_LORA
    o3 = o2 + KV_LORA
    z_lru = _dot(hb, win_ref[:, :o1])
    z = _dot(hb, win_ref[:, o1:])
    lru_ref[...] = _lru_tile(z_lru[:, :D_LRU], z_lru[:, D_LRU:], cw_ref, cb_ref, wa_ref, ba_ref,
                             wx_ref, bx_ref, lam_ref, tail_ref, carry_ref)
    ql = z[:, :o2]
    kvl = z[:, o2:o3]
    kr = z[:, o3:o3 + LANES]
    kr_rot = z[:, o3 + LANES:]

    qn = ql * lax.rsqrt(jnp.mean(ql * ql, axis=-1, keepdims=True) + EPS) * gq_ref[...]
    qq = _dot(qn.astype(BF16), wuq_ref[...])
    kvn = kvl * lax.rsqrt(jnp.mean(kvl * kvl, axis=-1, keepdims=True) + EPS) * gkv_ref[...]
    kv = _dot(kvn.astype(BF16), wukv_ref[...])

    tm = x.shape[0]
    lane = lax.broadcasted_iota(jnp.int32, (tm, HP), 1)
    pair_lane = lane & (2 * HEAD_PAD - 1)
    ones_cols = (pair_lane >= V_HEAD) & (pair_lane < 2 * HEAD_PAD - V_HEAD)
    v_ref[...] = jnp.where(ones_cols, 1.0, kv[:, HP:]).astype(BF16)

    cos_t = cos_ref[...]
    sin_t = sin_ref[...]
    gqn = gqn_ref[...]
    gkn = gkn_ref[...]
    cq = gqn * cos_t
    sq = gqr_ref[...] * sin_t
    kb = kr * (gkn * cos_t) + kr_rot * (gkr_ref[...] * sin_t)
    inv_w = 1.0 / QK_HEAD
    qscale = QK_HEAD ** -0.5 * LOG2_E
    for hh in range(N_HEADS):
        sl = slice(hh * HEAD_PAD, (hh + 1) * HEAD_PAD)
        qh = qq[:, sl]
        rq = lax.rsqrt(jnp.sum(qh * qh, axis=-1, keepdims=True) * inv_w + EPS) * qscale
        q_ref[:, sl] = ((qh * cq + qq[:, HP + hh * HEAD_PAD:HP + (hh + 1) * HEAD_PAD] * sq) * rq).astype(BF16)
        kraw = kv[:, sl] + kr
        rk = lax.rsqrt(jnp.sum(kraw * kraw, axis=-1, keepdims=True) * inv_w + EPS)
        k_ref[:, sl] = ((kv[:, sl] * gkn + kb) * rk).astype(BF16)


def _inproj(x2, cos_t, sin_t, mod3, g_mix, w_in_p, g_q_lat, w_uq_p, g_kv_lat, w_ukv_p,
            gqn_p, gqr_p, gkn_p, gkr_p, conv_w, conv_b, wa_d, b_a, wx_d, b_x, lam, B, S, tm):
    T, D = x2.shape
    ns = S // tm
    HP = N_HEADS * HEAD_PAD
    C = D_LRU
    row = lambda b, s: (b * ns + s, 0)
    full = lambda b, s: (0, 0)
    return pl.pallas_call(
        _inproj_kernel,
        grid=(B, ns),
        in_specs=[
            pl.BlockSpec((tm, D), row),
            pl.BlockSpec((tm, LANES), row),
            pl.BlockSpec((tm, LANES), row),
            pl.BlockSpec((1, 1, D), lambda b, s: (b * 6 + 0, 0, 0)),
            pl.BlockSpec((1, 1, D), lambda b, s: (b * 6 + 1, 0, 0)),
            pl.BlockSpec((1, D), full),
            pl.BlockSpec((D, D_IN_PAD), full),
            pl.BlockSpec((1, Q_LORA), full),
            pl.BlockSpec((Q_LORA, 2 * HP), full),
            pl.BlockSpec((1, KV_LORA), full),
            pl.BlockSpec((KV_LORA, 2 * HP), full),
            pl.BlockSpec((1, HEAD_PAD), full),
            pl.BlockSpec((1, HEAD_PAD), full),
            pl.BlockSpec((1, HEAD_PAD), full),
            pl.BlockSpec((1, HEAD_PAD), full),
            pl.BlockSpec((CONV_W, C), full),
            pl.BlockSpec((1, C), full),
            pl.BlockSpec((C, C), full),
            pl.BlockSpec((1, C), full),
            pl.BlockSpec((C, C), full),
            pl.BlockSpec((1, C), full),
            pl.BlockSpec((1, C), full),
        ],
        out_specs=[
            pl.BlockSpec((tm, C), row),
            pl.BlockSpec((tm, HP), row),
            pl.BlockSpec((tm, HP), row),
            pl.BlockSpec((tm, HP), row),
        ],
        out_shape=[
            jax.ShapeDtypeStruct((T, C), BF16),
            jax.ShapeDtypeStruct((T, HP), BF16),
            jax.ShapeDtypeStruct((T, HP), BF16),
            jax.ShapeDtypeStruct((T, HP), BF16),
        ],
        scratch_shapes=[pltpu.VMEM((SUBLANES, C), F32), pltpu.VMEM((SUBLANES, C), F32)],
        compiler_params=_cparams(("arbitrary", "arbitrary")),
        name="inproj",
    )(x2, cos_t, sin_t, mod3, mod3, g_mix, w_in_p, g_q_lat, w_uq_p, g_kv_lat, w_ukv_p,
      gqn_p, gqr_p, gkn_p, gkr_p, conv_w, conv_b, wa_d, b_a, wx_d, b_x, lam)


def _gelu_tanh(x):
    c = 0.7978845608028654
    hx = 0.5 * x
    return hx + hx * jnp.tanh(x * (c + (c * 0.044715) * (x * x)))


def _lru_tile(x, y, cw_ref, cb_ref, wa_ref, ba_ref, wx_ref, bx_ref, lam_ref, tail_ref, carry_ref):
    ts = x.shape[0]
    xext = jnp.concatenate([tail_ref[...], x], axis=0)
    cw = cw_ref[...]
    xc = x * cw[CONV_W - 1:CONV_W, :]
    for j in range(CONV_W - 1):
        sh = CONV_W - 1 - j
        xc = xc + xext[8 - sh:8 - sh + ts, :] * cw[j:j + 1, :]
    xc = xc + cb_ref[...]
    tail_ref[...] = x[ts - 8:, :]

    xb = xc.astype(BF16)
    r = _sigmoid(_dot(xb, wa_ref[...]) + ba_ref[...])
    i = _sigmoid(_dot(xb, wx_ref[...]) + bx_ref[...])
    lam = lam_ref[...]
    nl = -lam
    softplus = jnp.maximum(nl, 0.0) + jnp.log(1.0 + jnp.exp(-jnp.abs(nl)))
    log_a = (-LRU_C) * r * softplus
    a = jnp.exp(log_a)
    mult = jnp.sqrt(1.0 - a * a)
    u = mult * (i * xc)

    C = a.shape[1]
    a = a.reshape(ts // SUBLANES, SUBLANES, C)
    u = u.reshape(ts // SUBLANES, SUBLANES, C)
    sub = lax.broadcasted_iota(jnp.int32, (1, SUBLANES, 1), 1)
    sh = 1
    while sh < SUBLANES:
        a_prev = pltpu.roll(a, sh, axis=1)
        u_prev = pltpu.roll(u, sh, axis=1)
        m = sub >= sh
        u = jnp.where(m, a * u_prev + u, u)
        a = jnp.where(m, a * a_prev, a)
        sh *= 2
    a = a.reshape(ts, C)
    u = u.reshape(ts, C)
    h = carry_ref[0:1, :]
    groups = []
    for g0 in range(0, ts, SUBLANES):
        hg = u[g0:g0 + SUBLANES, :] + a[g0:g0 + SUBLANES, :] * h
        groups.append(hg)
        h = hg[SUBLANES - 1:SUBLANES, :]
    carry_ref[...] = jnp.broadcast_to(h, carry_ref.shape)
    hs = jnp.concatenate(groups, axis=0)
    return (_gelu_tanh(y) * hs).astype(BF16)


NEG_INF = -1e30


def _attn_kernel(q_ref, k_ref, v_ref, o_ref, *state, tq):
    m_refs = state[:N_HEADS]
    acc_refs = state[N_HEADS:]
    qi = pl.program_id(1)
    rowi = lax.broadcasted_iota(jnp.int32, (tq, tq), 0)
    coli = lax.broadcasted_iota(jnp.int32, (tq, tq), 1)
    diag_mask = coli <= rowi

    def head_slice(hh):
        return slice(hh * HEAD_PAD, (hh + 1) * HEAD_PAD)

    def weights(sc, m_b):
        cols = [jnp.exp2(sc[:, c0:c0 + LANES] - m_b) for c0 in range(0, tq, LANES)]
        return jnp.concatenate(cols, axis=1).astype(BF16)

    def scores(hh, r0):
        hs = head_slice(hh)
        return _dot_nt(q_ref[:, hs], k_ref[pl.ds(r0, tq), hs])

    r_diag = pl.multiple_of(qi * tq, tq)
    sc_next = scores(0, r_diag)
    for hh in range(N_HEADS):
        hs = head_slice(hh)
        sc = jnp.where(diag_mask, sc_next, NEG_INF)
        if hh + 1 < N_HEADS:
            sc_next = scores(hh + 1, r_diag)
        m_b = jnp.broadcast_to(jnp.max(sc, axis=-1, keepdims=True), (tq, LANES))
        m_refs[hh][...] = m_b
        acc_refs[hh][...] = _dot(weights(sc, m_b), v_ref[pl.ds(r_diag, tq), hs])

    @pl.loop(0, qi)
    def _(j):
        r0 = pl.multiple_of(j * tq, tq)
        sc_next = scores(0, r0)
        for hh in range(N_HEADS):
            hs = head_slice(hh)
            sc = sc_next
            if hh + 1 < N_HEADS:
                sc_next = scores(hh + 1, r0)
            m_b = m_refs[hh][...]
            m_new = jnp.maximum(m_b, jnp.max(sc, axis=-1, keepdims=True))
            alpha = jnp.exp2(m_b - m_new)
            m_refs[hh][...] = m_new
            acc_refs[hh][...] = (alpha * acc_refs[hh][...]
                                 + _dot(weights(sc, m_new), v_ref[pl.ds(r0, tq), hs]))

    low = lax.broadcasted_iota(jnp.int32, (tq, HEAD_PAD), 1) < V_HEAD
    for he in range(0, N_HEADS, 2):
        acc_e = acc_refs[he][...]
        acc_o = acc_refs[he + 1][...]
        num = jnp.where(low, acc_e, acc_o)
        den = pltpu.roll(jnp.where(low, acc_o, acc_e), V_HEAD, axis=1)
        o_ref[:, he * V_HEAD:(he + 2) * V_HEAD] = (num / den).astype(BF16)


def _attn(qp, kp, v, B, S, tq):
    T = qp.shape[0]
    nq = S // tq
    HP = N_HEADS * HEAD_PAD
    HV = N_HEADS * V_HEAD
    return pl.pallas_call(
        functools.partial(_attn_kernel, tq=tq),
        grid=(B, nq),
        in_specs=[
            pl.BlockSpec((tq, HP), lambda b, i: (b * nq + i, 0)),
            pl.BlockSpec((S, HP), lambda b, i: (b, 0)),
            pl.BlockSpec((S, HP), lambda b, i: (b, 0)),
        ],
        out_specs=pl.BlockSpec((tq, HV), lambda b, i: (b * nq + i, 0)),
        out_shape=jax.ShapeDtypeStruct((T, HV), BF16),
        scratch_shapes=([pltpu.VMEM((tq, LANES), F32)] * N_HEADS
                        + [pltpu.VMEM((tq, HEAD_PAD), F32)] * N_HEADS),
        compiler_params=_cparams(("arbitrary", "arbitrary")),
        name="attn",
    )(qp, kp, v)


def _outproj_kernel(lru_ref, att_ref, x_ref, gate_ref, shift_ref, scale_ref, gffn_ref,
                    wo1_ref, wo2_ref, wr_ref, br_ref, tri_ref,
                    x1_ref, h2p_ref, idx_ref, gat_ref, rank_ref, cnt_ref, run_ref):
    first = (pl.program_id(0) == 0) & (pl.program_id(1) == 0)

    @pl.when(first)
    def _():
        run_ref[...] = jnp.zeros_like(run_ref)

    mix = _dot(lru_ref[...], wo1_ref[...]) + _dot(att_ref[...], wo2_ref[...])
    x1 = x_ref[...] + gate_ref[0] * mix
    x1_ref[...] = x1
    ms = jnp.mean(x1 * x1, axis=-1, keepdims=True)
    gain = gffn_ref[...] * (1.0 + scale_ref[0])
    h2 = x1 * lax.rsqrt(ms + EPS) * gain + shift_ref[0]

    hhi = h2.astype(BF16)
    hlo = (h2 - hhi.astype(F32)).astype(BF16)
    h2p_ref[...] = _pack_halves(h2)

    ne = br_ref.shape[0]
    stacked = _dot_nt(wr_ref[...], hhi)
    logits = stacked[:ne] + stacked[ne:] + _dot_nt(wr_ref[:ne, :], hlo) + br_ref[...]

    tm = logits.shape[1]
    eio = lax.broadcasted_iota(jnp.int32, (ne, tm), 0)
    vals, idxs, sels = [], [], []
    l = logits
    for _ in range(TOP_K):
        m = jnp.max(l, axis=0, keepdims=True)
        idx = jnp.min(jnp.where(l == m, eio, ne), axis=0, keepdims=True)
        sel = eio == idx
        l = jnp.where(sel, -jnp.inf, l)
        vals.append(m)
        idxs.append(idx)
        sels.append(sel)
    es = [jnp.exp(v - vals[0]) for v in vals]
    inv = 1.0 / (es[0] + es[1] + es[2] + es[3])
    sel_any = jnp.where(sels[0] | sels[1] | sels[2] | sels[3], 1.0, 0.0)
    run = run_ref[...]
    excl = _dot(sel_any.astype(BF16), tri_ref[...]) + run
    for kk in range(TOP_K):
        idx_ref[kk:kk + 1, :] = idxs[kk]
        gat_ref[kk:kk + 1, :] = es[kk] * inv
        rk = jnp.sum(jnp.where(sels[kk], excl, 0.0), axis=0, keepdims=True)
        rank_ref[kk:kk + 1, :] = rk.astype(jnp.int32)
    run = run + jnp.sum(sel_any, axis=1, keepdims=True)
    run_ref[...] = run
    cnt_ref[...] = run.astype(jnp.int32)


def _outproj(lru_o, att_o, x2, mod3, g_ffn, wo1, wo2, wr_stack, b_r, tri, b0, B, S, tm):
    D = x2.shape[1]
    T = B * S
    ns = S // tm
    C = lru_o.shape[1]
    row_in = lambda b, s: ((b0 + b) * ns + s, 0)
    row = lambda b, s: (b * ns + s, 0)
    col = lambda b, s: (0, b * ns + s)
    full = lambda b, s: (0, 0)
    return pl.pallas_call(
        _outproj_kernel,
        grid=(B, ns),
        in_specs=[
            pl.BlockSpec((tm, C), row_in),
            pl.BlockSpec((tm, C), row_in),
            pl.BlockSpec((tm, D), row_in),
            pl.BlockSpec((1, 1, D), lambda b, s: ((b0 + b) * 6 + 2, 0, 0)),
            pl.BlockSpec((1, 1, D), lambda b, s: ((b0 + b) * 6 + 3, 0, 0)),
            pl.BlockSpec((1, 1, D), lambda b, s: ((b0 + b) * 6 + 4, 0, 0)),
            pl.BlockSpec((1, D), full),
            pl.BlockSpec((C, D), full),
            pl.BlockSpec((C, D), full),
            pl.BlockSpec((2 * N_EXPERTS, D), full),
            pl.BlockSpec((N_EXPERTS, 1), full),
            pl.BlockSpec(tri.shape, full),
        ],
        out_specs=[
            pl.BlockSpec((tm, D), row),
            pl.BlockSpec((tm, D // 2), row),
            pl.BlockSpec((TOP_K, tm), col),
            pl.BlockSpec((TOP_K, tm), col),
            pl.BlockSpec((TOP_K, tm), col),
            pl.BlockSpec((N_EXPERTS, 1), full),
        ],
        out_shape=[
            jax.ShapeDtypeStruct((T, D), F32),
            jax.ShapeDtypeStruct((T, D // 2), jnp.int32),
            jax.ShapeDtypeStruct((TOP_K, T), jnp.int32),
            jax.ShapeDtypeStruct((TOP_K, T), F32),
            jax.ShapeDtypeStruct((TOP_K, T), jnp.int32),
            jax.ShapeDtypeStruct((N_EXPERTS, 1), jnp.int32),
        ],
        scratch_shapes=[pltpu.VMEM((N_EXPERTS, 1), F32)],
        compiler_params=_cparams(("arbitrary", "arbitrary")),
        name="outproj",
    )(lru_o, att_o, x2, mod3, mod3, mod3, g_ffn, wo1, wo2, wr_stack, b_r, tri)


SC_CORES = 2
SC_SUBCORES = 16
SC_WORKERS = SC_CORES * SC_SUBCORES
SC_LANES = 16


def _sc_mesh():
    return plsc.VectorSubcoreMesh(core_axis_name="c", subcore_axis_name="s",
                                  num_cores=SC_CORES, num_subcores=SC_SUBCORES)


def _sc_worker_id():
    return lax.axis_index("s") * SC_CORES + lax.axis_index("c")


def _sc_scatter_rows(rows, idx, n_out, g):
    T, W = rows.shape
    K = idx.shape[0]
    per_w = T // SC_WORKERS
    nch = per_w // g
    assert per_w * SC_WORKERS == T and nch * g == per_w and nch % 2 == 0
    idx_w = idx.reshape(K, SC_WORKERS, nch, g).transpose(1, 2, 0, 3).reshape(SC_WORKERS, nch * K, g)

    def body(rows_hbm, idx_hbm, out_hbm, idx_v, buf0, buf1, semr0, semr1, semw):
        wid = _sc_worker_id()
        base = wid * per_w
        pltpu.sync_copy(idx_hbm.at[wid], idx_v)

        def read(j, buf, sem):
            return pltpu.make_async_copy(rows_hbm.at[pl.ds(base + j * g, g)], buf, sem)

        def scatter(j, buf):
            copies = [pltpu.async_copy(buf, out_hbm.at[idx_v.at[j * K + kk]], semw)
                      for kk in range(K)]
            for cp in copies:
                cp.wait()

        read(0, buf0, semr0).start()

        @pl.loop(0, nch // 2)
        def _(jj):
            j0 = 2 * jj
            read(j0 + 1, buf1, semr1).start()
            read(j0, buf0, semr0).wait()
            scatter(j0, buf0)

            @pl.when(j0 + 2 < nch)
            def _():
                read(j0 + 2, buf0, semr0).start()

            read(j0 + 1, buf1, semr1).wait()
            scatter(j0 + 1, buf1)

    return pl.kernel(
        body,
        out_type=jax.ShapeDtypeStruct((n_out, W), rows.dtype),
        mesh=_sc_mesh(),
        scratch_types=[
            pltpu.VMEM((nch * K, g), jnp.int32),
            pltpu.VMEM((g, W), rows.dtype),
            pltpu.VMEM((g, W), rows.dtype),
            pltpu.SemaphoreType.DMA,
            pltpu.SemaphoreType.DMA,
            pltpu.SemaphoreType.DMA,
        ],
        name="sc_scatter_rows",
    )(rows, idx_w)


def _sc_gather_gated_sum(table, idx, gates, g):
    W = table.shape[1]
    K, T = idx.shape
    per_w = T // SC_WORKERS
    nch = per_w // g
    assert per_w * SC_WORKERS == T and nch * g == per_w and nch % 2 == 0 and W % SC_LANES == 0
    idx_w = idx.reshape(K, SC_WORKERS, nch, g).transpose(1, 2, 0, 3).reshape(SC_WORKERS, nch * K, g)
    gates_w = gates.reshape(K, SC_WORKERS, nch, g).transpose(1, 2, 0, 3).reshape(SC_WORKERS, nch, K * g)
    def body(table_hbm, idx_hbm, gates_hbm, out_hbm, idx_v, rows_v, gts_v, out_v, sem0, sem1):
        wid = _sc_worker_id()
        base = wid * per_w
        pltpu.sync_copy(idx_hbm.at[wid], idx_v)

        def fetch(j, slot, sem):
            cps = []
            for kk in range(K):
                cps.append(pltpu.make_async_copy(table_hbm.at[idx_v.at[j * K + kk]],
                                                 rows_v.at[slot, kk], sem))
            cps.append(pltpu.make_async_copy(gates_hbm.at[wid, j], gts_v.at[slot], sem))
            return cps

        def start(j, slot, sem):
            for cp in fetch(j, slot, sem):
                cp.start()

        def finish(j, slot, sem):
            for cp in fetch(j, slot, sem):
                cp.wait()

            @pl.loop(0, g)
            def _(t):
                gk = [plsc.load_gather(gts_v.at[slot], [jnp.full((SC_LANES,), kk * g, jnp.int32) + t])
                      for kk in range(K)]

                @plsc.parallel_loop(0, W, SC_LANES, unroll=4)
                def _(off):
                    off = pl.multiple_of(off, SC_LANES)
                    acc_lo = jnp.zeros((SC_LANES,), F32)
                    acc_hi = jnp.zeros((SC_LANES,), F32)
                    for kk in range(K):
                        w = rows_v[slot, kk, t, pl.ds(off, SC_LANES)]
                        lo, hi = plsc.unpack(plsc.bitcast(w, BF16), format=plsc.PackFormat.INTERLEAVED)
                        acc_lo = acc_lo + gk[kk] * lo
                        acc_hi = acc_hi + gk[kk] * hi
                    out_v[slot, t, pl.ds(off, SC_LANES)] = acc_lo
                    out_v[slot, t, pl.ds(W + off, SC_LANES)] = acc_hi

            pltpu.sync_copy(out_v.at[slot], out_hbm.at[pl.ds(base + j * g, g)])

        start(0, 0, sem0)

        @pl.loop(0, nch // 2)
        def _(jj):
            j0 = 2 * jj
            start(j0 + 1, 1, sem1)
            finish(j0, 0, sem0)

            @pl.when(j0 + 2 < nch)
            def _():
                start(j0 + 2, 0, sem0)

            finish(j0 + 1, 1, sem1)

    return pl.kernel(
        body,
        out_type=jax.ShapeDtypeStruct((T, 2 * W), F32),
        mesh=_sc_mesh(),
        scratch_types=[
            pltpu.VMEM((nch * K, g), jnp.int32),
            pltpu.VMEM((2, K, g, W), jnp.int32),
            pltpu.VMEM((2, K * g), F32),
            pltpu.VMEM((2, g, 2 * W), F32),
            pltpu.SemaphoreType.DMA,
            pltpu.SemaphoreType.DMA,
        ],
        compiler_params=pltpu.CompilerParams(needs_layout_passes=False),
        name="sc_gather_gated_sum",
    )(table, idx_w, gates_w)


def _experts_kernel(be_ref, bv_ref, bf_ref, bn_ref, bs_ref, xs_ref, w1_hbm, b1_ref, w2_hbm, b2_ref,
                    ys_ref, w1s_ref, w2s_ref, w1b_ref, w2b_ref, sem):
    i = pl.program_id(0)
    nvalid = bv_ref[i]

    def weight_copies(e, slot):
        return (pltpu.make_async_copy(w1_hbm.at[e], w1s_ref.at[slot], sem.at[0, slot]),
                pltpu.make_async_copy(w2_hbm.at[e], w2s_ref.at[slot], sem.at[1, slot]))

    @pl.when(bf_ref[i] > 0)
    def _():
        slot = bs_ref[i]

        @pl.when(i == 0)
        def _():
            for cp in weight_copies(be_ref[0], slot):
                cp.start()

        for cp in weight_copies(be_ref[i], slot):
            cp.wait()
        w1b_ref[...] = w1s_ref[slot].astype(BF16)
        w2b_ref[...] = w2s_ref[slot].astype(BF16)

        @pl.when(bn_ref[i] >= 0)
        def _():
            for cp in weight_copies(bn_ref[i], 1 - slot):
                cp.start()

    def ffn(rows):
        xw = xs_ref[:rows, :]
        rowi = lax.broadcasted_iota(jnp.int32, (rows, 1), 0)
        lo, hi = _unpack_halves(jnp.where(rowi < nvalid, xw, 0))
        xb = jnp.concatenate([lo.astype(BF16), hi.astype(BF16)], axis=1)
        gu = _dot(xb, w1b_ref[...]) + b1_ref[0]
        glu = jnp.minimum(gu[:, :D_FF], SWIGLU_LIMIT)
        lin = jnp.clip(gu[:, D_FF:], -SWIGLU_LIMIT, SWIGLU_LIMIT)
        act = (lin + 1.0) * (glu * _sigmoid(glu, SWIGLU_ALPHA))
        ys_ref[:rows, :] = _pack_halves(_dot(act.astype(BF16), w2b_ref[...]) + b2_ref[0])

    step = xs_ref.shape[0] // ROW_QUARTERS
    for nq in range(1, ROW_QUARTERS + 1):
        pl.when((nvalid > (nq - 1) * step) & (nvalid <= nq * step))(functools.partial(ffn, nq * step))


def _experts(blk_e, blk_v, blk_f, blk_n, blk_s, blk_r, xs, w1, b1, w2, b2):
    P, W = xs.shape
    nb = P // MOE_BLOCK
    E, D, F2 = w1.shape
    grid_spec = pltpu.PrefetchScalarGridSpec(
        num_scalar_prefetch=6,
        grid=(nb,),
        in_specs=[
            pl.BlockSpec((MOE_BLOCK, W), lambda i, be, bv, bf, bn, bs, br: (br[i], 0)),
            pl.BlockSpec(memory_space=pl.ANY),
            pl.BlockSpec((1, 1, F2), lambda i, be, bv, bf, bn, bs, br: (be[i], 0, 0)),
            pl.BlockSpec(memory_space=pl.ANY),
            pl.BlockSpec((1, 1, D), lambda i, be, bv, bf, bn, bs, br: (be[i], 0, 0)),
        ],
        out_specs=pl.BlockSpec((MOE_BLOCK, D // 2), lambda i, be, bv, bf, bn, bs, br: (br[i], 0)),
        scratch_shapes=[
            pltpu.VMEM((2, D, F2), F32),
            pltpu.VMEM((2, D_FF, D), F32),
            pltpu.VMEM((D, F2), BF16),
            pltpu.VMEM((D_FF, D), BF16),
            pltpu.SemaphoreType.DMA((2, 2)),
        ],
    )

    def kern(be_ref, bv_ref, bf_ref, bn_ref, bs_ref, br_ref, *refs):
        del br_ref
        _experts_kernel(be_ref, bv_ref, bf_ref, bn_ref, bs_ref, *refs)

    return pl.pallas_call(
        kern,
        grid_spec=grid_spec,
        out_shape=jax.ShapeDtypeStruct((P, D // 2), jnp.int32),
        compiler_params=_cparams(("arbitrary",)),
        name="experts",
    )(blk_e, blk_v, blk_f, blk_n, blk_s, blk_r, xs, w1, b1.reshape(E, 1, F2), w2,
      b2.reshape(E, 1, D))


def _combine_kernel(x1_ref, gate_ref, ysum_ref, *rest):
    o_ref = rest[-1]
    o_ref[...] = x1_ref[...] + gate_ref[0] * ysum_ref[...]


def _combine(x1, mod3, ysum, out_prev, b0, B, nb_total, S, tm):
    D = x1.shape[1]
    ns = S // tm
    row = lambda b, s: (b * ns + s, 0)
    row_out = lambda b, s: ((b0 + b) * ns + s, 0)
    in_specs = [
        pl.BlockSpec((tm, D), row),
        pl.BlockSpec((1, 1, D), lambda b, s: ((b0 + b) * 6 + 5, 0, 0)),
        pl.BlockSpec((tm, D), row),
    ]
    args = [x1, mod3, ysum]
    aliases = {}
    if out_prev is not None:
        in_specs.append(pl.BlockSpec(memory_space=pl.ANY))
        args.append(out_prev)
        aliases = {len(args) - 1: 0}
    return pl.pallas_call(
        _combine_kernel,
        grid=(B, ns),
        in_specs=in_specs,
        out_specs=pl.BlockSpec((tm, D), row_out),
        out_shape=jax.ShapeDtypeStruct((nb_total * S, D), F32),
        input_output_aliases=aliases,
        compiler_params=_cparams(("arbitrary", "arbitrary")),
        name="combine",
    )(*args)


def _block_diag(w):
    n, c, d = w.shape
    eye = jnp.eye(n, dtype=w.dtype)
    return jnp.einsum("ncd,nm->ncmd", w, eye).reshape(n * c, n * d)


def _pad_heads(w, width):
    k = w.shape[0]
    w = w.reshape(k, N_HEADS, width)
    return jnp.pad(w, ((0, 0), (0, 0), (0, HEAD_PAD - width))).reshape(k, N_HEADS * HEAD_PAD)


def kernel(x, c, positions, w_ada, b_ada, g_mix, w_in, conv_w, conv_b, w_a, b_a, w_x, b_x, lam,
           g_q_lat, w_uq, g_kv_lat, w_ukv, g_qn, g_kn, w_out, g_ffn, w_router, b_router,
           w1, b1, w2, b2):
    B, S, D = x.shape
    T = B * S
    depth = w_ada.shape[0]
    tm_in = min(512, S)
    tq = min(512, S)
    tm_out = min(512, S)
    tm_comb = min(1024, S)
    n_groups = 2 if B % 2 == 0 else 1
    Bg = B // n_groups
    Tg = Bg * S
    g_disp = min(64, Tg // SC_WORKERS // 2)
    g_comb = min(16, Tg // SC_WORKERS // 2)

    o1 = 2 * D_LRU
    o2 = o1 + Q_LORA
    o3 = o2 + KV_LORA
    tri = (jnp.arange(tm_out)[:, None] < jnp.arange(tm_out)[None, :]).astype(BF16)
    cos_t, sin_t = _rope_tables(positions)
    lane = jnp.arange(HEAD_PAD, dtype=jnp.int32)
    first = (lane >= ROPE_LO) & (lane < ROPE_LO + ROPE_HALF)
    second = (lane >= ROPE_LO + ROPE_HALF) & (lane < ROPE_LO + QK_ROPE)
    partner = jnp.where(first, lane + ROPE_HALF, jnp.where(second, lane - ROPE_HALF, 0))
    is_rot = first | second

    def rot_cols(w):
        k = w.shape[0]
        w3 = w.reshape(k, -1, HEAD_PAD)
        return jnp.where(is_rot[None, None, :], w3[:, :, partner], 0.0).reshape(w.shape)

    x2 = x.reshape(T, D)
    for l in range(depth):
        mod3 = _ada(c, w_ada[l], b_ada[l]).reshape(B * 6, 1, D)

        w_in_l = w_in[l]
        kr_cols = jnp.pad(w_in_l[:, o3:], ((0, 0), (ROPE_LO, LANES - ROPE_LO - QK_ROPE)))
        w_in_p = jnp.concatenate([w_in_l[:, :o3], kr_cols, rot_cols(kr_cols)], axis=1).astype(BF16)
        w_uq_h = _pad_heads(w_uq[l], QK_HEAD)
        w_uq_p = jnp.concatenate([w_uq_h, rot_cols(w_uq_h)], axis=1).astype(BF16)
        w_ukv_l = w_ukv[l].reshape(KV_LORA, N_HEADS, QK_NOPE + V_HEAD)
        w_uk_h = _pad_heads(w_ukv_l[:, :, :QK_NOPE].reshape(KV_LORA, N_HEADS * QK_NOPE), QK_NOPE)
        w_uv_pairs = w_ukv_l[:, :, QK_NOPE:].reshape(KV_LORA, N_HEADS // 2, 2, V_HEAD)
        zero_v = jnp.zeros_like(w_uv_pairs[:, :, 0])
        w_uv_h = jnp.stack([w_uv_pairs[:, :, 0], zero_v, zero_v, w_uv_pairs[:, :, 1]],
                           axis=2).reshape(KV_LORA, N_HEADS * HEAD_PAD)
        w_ukv_p = jnp.concatenate([w_uk_h, w_uv_h], axis=1).astype(BF16)
        gqn_p = jnp.pad(g_qn[l], (0, HEAD_PAD - QK_HEAD)).reshape(1, HEAD_PAD)
        gkn_p = jnp.pad(g_kn[l], (0, HEAD_PAD - QK_HEAD)).reshape(1, HEAD_PAD)

        lru_o, qp, kp, v = _inproj(
            x2, cos_t, sin_t, mod3, g_mix[l].reshape(1, D), w_in_p, g_q_lat[l].reshape(1, Q_LORA),
            w_uq_p, g_kv_lat[l].reshape(1, KV_LORA), w_ukv_p, gqn_p, rot_cols(gqn_p), gkn_p,
            rot_cols(gkn_p), conv_w[l], conv_b[l].reshape(1, D_LRU),
            _block_diag(w_a[l]).astype(BF16), b_a[l].reshape(1, D_LRU),
            _block_diag(w_x[l]).astype(BF16), b_x[l].reshape(1, D_LRU),
            lam[l].reshape(1, D_LRU), B, S, tm_in)

        att_o = _attn(qp, kp, v, B, S, tq)

        w_out_b = w_out[l].astype(BF16)
        wr_stack = jnp.concatenate(_split_bf16(w_router[l].T), axis=0)
        g_ffn_l = g_ffn[l].reshape(1, D)
        b_r = b_router[l].reshape(N_EXPERTS, 1)
        eio = jnp.arange(N_EXPERTS, dtype=jnp.int32)
        n_blocks = -(-(Tg * TOP_K) // MOE_BLOCK) + N_EXPERTS
        bi = jnp.arange(n_blocks, dtype=jnp.int32)

        x_next = None
        for gi in range(n_groups):
            b0 = gi * Bg
            x1, h2p, idx_t, gat_t, rank_t, counts = _outproj(
                lru_o, att_o, x2, mod3, g_ffn_l, w_out_b[:D_LRU], w_out_b[D_LRU:],
                wr_stack, b_r, tri, b0, Bg, S, tm_out)

            counts = counts.reshape(N_EXPERTS)
            nblk_e = (counts + MOE_BLOCK - 1) // MOE_BLOCK
            blk_end = jnp.cumsum(nblk_e)
            pad_start = (blk_end - nblk_e) * MOE_BLOCK
            total = blk_end[-1]
            blk_r = jnp.minimum(bi, total - 1).astype(jnp.int32)
            blk_e = jnp.minimum(jnp.sum(blk_end[None, :] <= blk_r[:, None], axis=1),
                                N_EXPERTS - 1).astype(jnp.int32)
            blk_onehot = blk_e[:, None] == eio[None, :]
            blk_first = jnp.sum(jnp.where(blk_onehot, (blk_end - nblk_e)[None, :], 0), axis=1)
            blk_cnt = jnp.sum(jnp.where(blk_onehot, counts[None, :], 0), axis=1)
            blk_v = jnp.where(bi < total,
                              jnp.clip(blk_cnt - (bi - blk_first) * MOE_BLOCK, 0, MOE_BLOCK),
                              0).astype(jnp.int32)
            blk_f = ((bi == blk_first) & (bi < total)).astype(jnp.int32)
            nxt_first = jnp.sum(jnp.where(blk_onehot, blk_end[None, :], 0), axis=1)
            nxt_e = jnp.minimum(jnp.sum(blk_end[None, :] <= nxt_first[:, None], axis=1),
                                N_EXPERTS - 1)
            blk_n = jnp.where(nxt_first < total, nxt_e, -1).astype(jnp.int32)
            ordinal = jnp.cumsum((nblk_e > 0).astype(jnp.int32)) - 1
            blk_s = (jnp.sum(jnp.where(blk_onehot, ordinal[None, :], 0), axis=1) % 2).astype(jnp.int32)
            slot0 = jnp.sum(jnp.where(idx_t[None] == eio[:, None, None],
                                      pad_start[:, None, None], 0), axis=0)
            dest = slot0.astype(jnp.int32) + rank_t

            xs = _sc_scatter_rows(h2p, dest, n_blocks * MOE_BLOCK, g_disp)
            ys = _experts(blk_e, blk_v, blk_f, blk_n, blk_s, blk_r, xs, w1[l], b1[l], w2[l], b2[l])
            ysum = _sc_gather_gated_sum(ys, dest, gat_t, g_comb)
            x_next = _combine(x1, mod3, ysum, x_next, b0, Bg, B, S, tm_comb)
        x2 = x_next
    return x2.reshape(B, S, D)
```

```python
import functools

import jax
import jax.numpy as jnp
import numpy as np
from jax import lax
from jax.experimental import pallas as pl
from jax.experimental.pallas import tpu as pltpu
from jax.experimental.pallas import tpu_sc as plsc

D_MODEL = 1024
D_LRU = 512
LRU_BLOCKS = 8
LRU_BD = 64
CONV_W = 4
LRU_C = 8.0
N_HEADS = 8
QK_NOPE = 64
QK_ROPE = 32
QK_HEAD = 96
V_HEAD = 64
Q_LORA = 256
KV_LORA = 128
ROPE_THETA = 10000.0
N_EXPERTS = 32
TOP_K = 4
D_FF = 1024
SWIGLU_LIMIT = 7.0
SWIGLU_ALPHA = 1.702
MOE_BLOCK = 512
EPS = 1e-6

LANES = 128
SUBLANES = 8
HEAD_PAD = 128
ROPE_LO = QK_NOPE
ROPE_HALF = QK_ROPE // 2
TOK_PER_ROW = LANES // ROPE_HALF
D_IN_PAD = 2 * D_LRU + Q_LORA + KV_LORA + 2 * LANES
LOG2_E = 1.4426950408889634
ROW_QUARTERS = 4

VMEM_LIMIT = 56 * 1024 * 1024

F32 = jnp.float32
BF16 = jnp.bfloat16


def _cparams(sem):
    return pltpu.CompilerParams(dimension_semantics=sem, vmem_limit_bytes=VMEM_LIMIT)


def _dot(a, b):
    return jnp.dot(a, b, preferred_element_type=F32)


def _dot_nt(a, b):
    return lax.dot_general(a, b, (((1,), (1,)), ((), ())), preferred_element_type=F32)


def _split_bf16(a):
    hi = a.astype(BF16)
    lo = (a - hi.astype(F32)).astype(BF16)
    return hi, lo


def _sigmoid(x, scale=1.0):
    return 1.0 / (1.0 + jnp.exp2(x * (-scale * LOG2_E)))


def _pack_halves(x):
    bits = lax.bitcast_convert_type(x.astype(BF16).astype(F32), jnp.uint32)
    half = x.shape[1] // 2
    words = (bits[:, :half] >> 16) | (bits[:, half:] & jnp.uint32(0xFFFF0000))
    return lax.bitcast_convert_type(words, jnp.int32)


def _unpack_halves(words):
    w = lax.bitcast_convert_type(words, jnp.uint32)
    lo = lax.bitcast_convert_type(w << 16, F32)
    hi = lax.bitcast_convert_type(w & jnp.uint32(0xFFFF0000), F32)
    return lo, hi


def _ada_kernel(c_ref, w_ref, b_ref, o_ref):
    c = c_ref[...]
    s = c * _sigmoid(c)
    shi, slo = _split_bf16(s)
    whi, wlo = _split_bf16(w_ref[...])
    o_ref[...] = _dot(shi, whi) + _dot(slo, whi) + _dot(shi, wlo) + b_ref[...]


def _ada(c, w_ada, b_ada):
    B, D = c.shape
    N = w_ada.shape[1]
    tn = 1024
    return pl.pallas_call(
        _ada_kernel,
        grid=(N // tn,),
        in_specs=[
            pl.BlockSpec((B, D), lambda j: (0, 0)),
            pl.BlockSpec((D, tn), lambda j: (0, j)),
            pl.BlockSpec((1, tn), lambda j: (0, j)),
        ],
        out_specs=pl.BlockSpec((B, tn), lambda j: (0, j)),
        out_shape=jax.ShapeDtypeStruct((B, N), F32),
        compiler_params=_cparams(("arbitrary",)),
        name="ada",
    )(c, w_ada, b_ada.reshape(1, N))


def _trig_kernel(pos_ref, freq_ref, rsel_ref, fold_ref, cbase_ref, cos_ref, sin_ref):
    ang = pos_ref[...].astype(F32) * freq_ref[...]
    cs = jnp.concatenate([jnp.cos(ang), jnp.sin(ang)], axis=1)
    tm = cos_ref.shape[0]
    row = lax.broadcasted_iota(jnp.int32, (tm, 2 * LANES), 0)
    lane = lax.broadcasted_iota(jnp.int32, (tm, 2 * LANES), 1)
    own = ((lane % LANES) // ROPE_HALF) == (row % TOK_PER_ROW)
    rsel = rsel_ref[...]
    fold = fold_ref[...]
    by_row = sum(_dot(rsel, part) for part in _split_bf16(cs))
    mine = jnp.where(own, by_row, 0.0)
    out = sum(_dot(part, fold) for part in _split_bf16(mine))
    cos_ref[...] = out[:, :LANES] + cbase_ref[...]
    sin_ref[...] = out[:, LANES:]


def _rope_tables(positions):
    T = positions.size
    rows = T // TOK_PER_ROW
    pos_c = jnp.repeat(positions.reshape(T).astype(jnp.int32), ROPE_HALF).reshape(rows, LANES)
    tm = min(2048, T)
    tr = tm // TOK_PER_ROW
    freqs = np.float32(ROPE_THETA) ** (-np.arange(ROPE_HALF, dtype=np.float32) / np.float32(ROPE_HALF))
    freq_c = np.tile(freqs.astype(np.float32), TOK_PER_ROW).reshape(1, LANES)
    rsel = jnp.asarray(np.arange(tm)[:, None] // TOK_PER_ROW == np.arange(tr)[None, :], BF16)
    src = np.arange(LANES)[:, None] % ROPE_HALF
    dst = np.arange(LANES)[None, :]
    first = dst == ROPE_LO + src
    second = dst == ROPE_LO + ROPE_HALF + src
    fcos = (first | second).astype(np.float32)
    fsin = second.astype(np.float32) - first.astype(np.float32)
    zero = np.zeros((LANES, LANES), np.float32)
    fold = jnp.asarray(np.block([[fcos, zero], [zero, fsin]]), BF16)
    lane = np.arange(LANES)
    cbase = ((lane < ROPE_LO) | (lane >= ROPE_LO + QK_ROPE)).astype(np.float32).reshape(1, LANES)
    full = lambda i: (0, 0)
    return pl.pallas_call(
        _trig_kernel,
        grid=(T // tm,),
        in_specs=[
            pl.BlockSpec((tr, LANES), lambda i: (i, 0)),
            pl.BlockSpec((1, LANES), full),
            pl.BlockSpec((tm, tr), full),
            pl.BlockSpec((2 * LANES, 2 * LANES), full),
            pl.BlockSpec((1, LANES), full),
        ],
        out_specs=[pl.BlockSpec((tm, LANES), lambda i: (i, 0))] * 2,
        out_shape=[jax.ShapeDtypeStruct((T, LANES), F32)] * 2,
        compiler_params=_cparams(("arbitrary",)),
        name="rope_trig",
    )(pos_c, freq_c, rsel, fold, cbase)


def _inproj_kernel(x_ref, cos_ref, sin_ref, shift_ref, scale_ref, gmix_ref, win_ref, gq_ref, wuq_ref,
                   gkv_ref, wukv_ref, gqn_ref, gqr_ref, gkn_ref, gkr_ref,
                   cw_ref, cb_ref, wa_ref, ba_ref, wx_ref, bx_ref, lam_ref,
                   lru_ref, q_ref, k_ref, v_ref, tail_ref, carry_ref):
    HP = N_HEADS * HEAD_PAD

    @pl.when(pl.program_id(1) == 0)
    def _():
        tail_ref[...] = jnp.zeros_like(tail_ref)
        carry_ref[...] = jnp.zeros_like(carry_ref)

    x = x_ref[...]
    ms = jnp.mean(x * x, axis=-1, keepdims=True)
    gain = gmix_ref[...] * (1.0 + scale_ref[0])
    h = x * lax.rsqrt(ms + EPS) * gain + shift_ref[0]
    hb = h.astype(BF16)
    o1 = 2 * D_LRU
    o2 = Q_LORA
    o3 = o2 + KV_LORA
    z_lru = _dot(hb, win_ref[:, :o1])
    z = _dot(hb, win_ref[:, o1:])
    lru_ref[...] = _lru_tile(z_lru[:, :D_LRU], z_lru[:, D_LRU:], cw_ref, cb_ref, wa_ref, ba_ref,
                             wx_ref, bx_ref, lam_ref, tail_ref, carry_ref)
    ql = z[:, :o2]
    kvl = z[:, o2:o3]
    kr = z[:, o3:o3 + LANES]
    kr_rot = z[:, o3 + LANES:]

    qn = ql * lax.rsqrt(jnp.mean(ql * ql, axis=-1, keepdims=True) + EPS) * gq_ref[...]
    qq = _dot(qn.astype(BF16), wuq_ref[...])
    kvn = kvl * lax.rsqrt(jnp.mean(kvl * kvl, axis=-1, keepdims=True) + EPS) * gkv_ref[...]
    kv = _dot(kvn.astype(BF16), wukv_ref[...])

    tm = x.shape[0]
    lane = lax.broadcasted_iota(jnp.int32, (tm, HP), 1)
    pair_lane = lane & (2 * HEAD_PAD - 1)
    ones_cols = (pair_lane >= V_HEAD) & (pair_lane < 2 * HEAD_PAD - V_HEAD)
    v_ref[...] = jnp.where(ones_cols, 1.0, kv[:, HP:]).astype(BF16)

    cos_t = cos_ref[...]
    sin_t = sin_ref[...]
    gqn = gqn_ref[...]
    gkn = gkn_ref[...]
    cq = gqn * cos_t
    sq = gqr_ref[...] * sin_t
    kb = kr * (gkn * cos_t) + kr_rot * (gkr_ref[...] * sin_t)
    inv_w = 1.0 / QK_HEAD
    qscale = QK_HEAD ** -0.5 * LOG2_E
    for hh in range(N_HEADS):
        sl = slice(hh * HEAD_PAD, (hh + 1) * HEAD_PAD)
        qh = qq[:, sl]
        rq = lax.rsqrt(jnp.sum(qh * qh, axis=-1, keepdims=True) * inv_w + EPS) * qscale
        q_ref[:, sl] = ((qh * cq + qq[:, HP + hh * HEAD_PAD:HP + (hh + 1) * HEAD_PAD] * sq) * rq).astype(BF16)
        kraw = kv[:, sl] + kr
        rk = lax.rsqrt(jnp.sum(kraw * kraw, axis=-1, keepdims=True) * inv_w + EPS)
        k_ref[:, sl] = ((kv[:, sl] * gkn + kb) * rk).astype(BF16)


def _inproj(x2, cos_t, sin_t, mod3, g_mix, w_in_p, g_q_lat, w_uq_p, g_kv_lat, w_ukv_p,
            gqn_p, gqr_p, gkn_p, gkr_p, conv_w, conv_b, wa_d, b_a, wx_d, b_x, lam, B, S, tm):
    T, D = x2.shape
    ns = S // tm
    HP = N_HEADS * HEAD_PAD
    C = D_LRU
    row = lambda b, s: (b * ns + s, 0)
    full = lambda b, s: (0, 0)
    return pl.pallas_call(
        _inproj_kernel,
        grid=(B, ns),
        in_specs=[
            pl.BlockSpec((tm, D), row),
            pl.BlockSpec((tm, LANES), row),
            pl.BlockSpec((tm, LANES), row),
            pl.BlockSpec((1, 1, D), lambda b, s: (b * 6 + 0, 0, 0)),
            pl.BlockSpec((1, 1, D), lambda b, s: (b * 6 + 1, 0, 0)),
            pl.BlockSpec((1, D), full),
            pl.BlockSpec((D, D_IN_PAD), full),
            pl.BlockSpec((1, Q_LORA), full),
            pl.BlockSpec((Q_LORA, 2 * HP), full),
            pl.BlockSpec((1, KV_LORA), full),
            pl.BlockSpec((KV_LORA, 2 * HP), full),
            pl.BlockSpec((1, HEAD_PAD), full),
            pl.BlockSpec((1, HEAD_PAD), full),
            pl.BlockSpec((1, HEAD_PAD), full),
            pl.BlockSpec((1, HEAD_PAD), full),
            pl.BlockSpec((CONV_W, C), full),
            pl.BlockSpec((1, C), full),
            pl.BlockSpec((C, C), full),
            pl.BlockSpec((1, C), full),
            pl.BlockSpec((C, C), full),
            pl.BlockSpec((1, C), full),
            pl.BlockSpec((1, C), full),
        ],
        out_specs=[
            pl.BlockSpec((tm, C), row),
            pl.BlockSpec((tm, HP), row),
            pl.BlockSpec((tm, HP), row),
            pl.BlockSpec((tm, HP), row),
        ],
        out_shape=[
            jax.ShapeDtypeStruct((T, C), BF16),
            jax.ShapeDtypeStruct((T, HP), BF16),
            jax.ShapeDtypeStruct((T, HP), BF16),
            jax.ShapeDtypeStruct((T, HP), BF16),
        ],
        scratch_shapes=[pltpu.VMEM((SUBLANES, C), F32), pltpu.VMEM((SUBLANES, C), F32)],
        compiler_params=_cparams(("arbitrary", "arbitrary")),
        name="inproj",
    )(x2, cos_t, sin_t, mod3, mod3, g_mix, w_in_p, g_q_lat, w_uq_p, g_kv_lat, w_ukv_p,
      gqn_p, gqr_p, gkn_p, gkr_p, conv_w, conv_b, wa_d, b_a, wx_d, b_x, lam)


def _gelu_tanh(x):
    c = 0.7978845608028654
    hx = 0.5 * x
    return hx + hx * jnp.tanh(x * (c + (c * 0.044715) * (x * x)))


def _lru_tile(x, y, cw_ref, cb_ref, wa_ref, ba_ref, wx_ref, bx_ref, lam_ref, tail_ref, carry_ref):
    ts = x.shape[0]
    xext = jnp.concatenate([tail_ref[...], x], axis=0)
    cw = cw_ref[...]
    xc = x * cw[CONV_W - 1:CONV_W, :]
    for j in range(CONV_W - 1):
        sh = CONV_W - 1 - j
        xc = xc + xext[8 - sh:8 - sh + ts, :] * cw[j:j + 1, :]
    xc = xc + cb_ref[...]
    tail_ref[...] = x[ts - 8:, :]

    xb = xc.astype(BF16)
    r = _sigmoid(_dot(xb, wa_ref[...]) + ba_ref[...])
    i = _sigmoid(_dot(xb, wx_ref[...]) + bx_ref[...])
    lam = lam_ref[...]
    nl = -lam
    softplus = jnp.maximum(nl, 0.0) + jnp.log(1.0 + jnp.exp(-jnp.abs(nl)))
    log_a = (-LRU_C) * r * softplus
    a = jnp.exp(log_a)
    mult = jnp.sqrt(1.0 - a * a)
    u = mult * (i * xc)

    C = a.shape[1]
    a = a.reshape(ts // SUBLANES, SUBLANES, C)
    u = u.reshape(ts // SUBLANES, SUBLANES, C)
    sub = lax.broadcasted_iota(jnp.int32, (1, SUBLANES, 1), 1)
    sh = 1
    while sh < SUBLANES:
        a_prev = pltpu.roll(a, sh, axis=1)
        u_prev = pltpu.roll(u, sh, axis=1)
        m = sub >= sh
        u = jnp.where(m, a * u_prev + u, u)
        a = jnp.where(m, a * a_prev, a)
        sh *= 2
    a = a.reshape(ts, C)
    u = u.reshape(ts, C)
    h = carry_ref[0:1, :]
    groups = []
    for g0 in range(0, ts, SUBLANES):
        hg = u[g0:g0 + SUBLANES, :] + a[g0:g0 + SUBLANES, :] * h
        groups.append(hg)
        h = hg[SUBLANES - 1:SUBLANES, :]
    carry_ref[...] = jnp.broadcast_to(h, carry_ref.shape)
    hs = jnp.concatenate(groups, axis=0)
    return (_gelu_tanh(y) * hs).astype(BF16)


NEG_INF = -1e30


def _attn_kernel(q_ref, k_ref, v_ref, o_ref, *state, tq):
    m_refs = state[:N_HEADS]
    acc_refs = state[N_HEADS:]
    qi = pl.program_id(1)
    rowi = lax.broadcasted_iota(jnp.int32, (tq, tq), 0)
    coli = lax.broadcasted_iota(jnp.int32, (tq, tq), 1)
    diag_mask = coli <= rowi

    def head_slice(hh):
        return slice(hh * HEAD_PAD, (hh + 1) * HEAD_PAD)

    def weights(sc, m_b):
        cols = [jnp.exp2(sc[:, c0:c0 + LANES] - m_b) for c0 in range(0, tq, LANES)]
        return jnp.concatenate(cols, axis=1).astype(BF16)

    def scores(hh, r0):
        hs = head_slice(hh)
        return _dot_nt(q_ref[:, hs], k_ref[pl.ds(r0, tq), hs])

    r_diag = pl.multiple_of(qi * tq, tq)
    sc_next = scores(0, r_diag)
    for hh in range(N_HEADS):
        hs = head_slice(hh)
        sc = jnp.where(diag_mask, sc_next, NEG_INF)
        if hh + 1 < N_HEADS:
            sc_next = scores(hh + 1, r_diag)
        m_b = jnp.broadcast_to(jnp.max(sc, axis=-1, keepdims=True), (tq, LANES))
        m_refs[hh][...] = m_b
        acc_refs[hh][...] = _dot(weights(sc, m_b), v_ref[pl.ds(r_diag, tq), hs])

    @pl.loop(0, qi)
    def _(j):
        r0 = pl.multiple_of(j * tq, tq)
        sc_next = scores(0, r0)
        for hh in range(N_HEADS):
            hs = head_slice(hh)
            sc = sc_next
            if hh + 1 < N_HEADS:
                sc_next = scores(hh + 1, r0)
            m_b = m_refs[hh][...]
            m_new = jnp.maximum(m_b, jnp.max(sc, axis=-1, keepdims=True))
            alpha = jnp.exp2(m_b - m_new)
            m_refs[hh][...] = m_new
            acc_refs[hh][...] = (alpha * acc_refs[hh][...]
                                 + _dot(weights(sc, m_new), v_ref[pl.ds(r0, tq), hs]))

    low = lax.broadcasted_iota(jnp.int32, (tq, HEAD_PAD), 1) < V_HEAD
    for he in range(0, N_HEADS, 2):
        acc_e = acc_refs[he][...]
        acc_o = acc_refs[he + 1][...]
        num = jnp.where(low, acc_e, acc_o)
        den = pltpu.roll(jnp.where(low, acc_o, acc_e), V_HEAD, axis=1)
        o_ref[:, he * V_HEAD:(he + 2) * V_HEAD] = (num / den).astype(BF16)


def _attn(qp, kp, v, B, S, tq):
    T = qp.shape[0]
    nq = S // tq
    HP = N_HEADS * HEAD_PAD
    HV = N_HEADS * V_HEAD
    return pl.pallas_call(
        functools.partial(_attn_kernel, tq=tq),
        grid=(B, nq),
        in_specs=[
            pl.BlockSpec((tq, HP), lambda b, i: (b * nq + i, 0)),
            pl.BlockSpec((S, HP), lambda b, i: (b, 0)),
            pl.BlockSpec((S, HP), lambda b, i: (b, 0)),
        ],
        out_specs=pl.BlockSpec((tq, HV), lambda b, i: (b * nq + i, 0)),
        out_shape=jax.ShapeDtypeStruct((T, HV), BF16),
        scratch_shapes=([pltpu.VMEM((tq, LANES), F32)] * N_HEADS
                        + [pltpu.VMEM((tq, HEAD_PAD), F32)] * N_HEADS),
        compiler_params=_cparams(("arbitrary", "arbitrary")),
        name="attn",
    )(qp, kp, v)


def _outproj_kernel(lru_ref, att_ref, x_ref, gate_ref, shift_ref, scale_ref, gffn_ref,
                    wo1_ref, wo2_ref, wr_ref, br_ref, tri_ref,
                    x1_ref, h2p_ref, idx_ref, gat_ref, rank_ref, cnt_ref, run_ref):
    first = (pl.program_id(0) == 0) & (pl.program_id(1) == 0)

    @pl.when(first)
    def _():
        run_ref[...] = jnp.zeros_like(run_ref)

    mix = _dot(lru_ref[...], wo1_ref[...]) + _dot(att_ref[...], wo2_ref[...])
    x1 = x_ref[...] + gate_ref[0] * mix
    x1_ref[...] = x1
    ms = jnp.mean(x1 * x1, axis=-1, keepdims=True)
    gain = gffn_ref[...] * (1.0 + scale_ref[0])
    h2 = x1 * lax.rsqrt(ms + EPS) * gain + shift_ref[0]

    hhi = h2.astype(BF16)
    hlo = (h2 - hhi.astype(F32)).astype(BF16)
    h2p_ref[...] = _pack_halves(h2)

    ne = br_ref.shape[0]
    stacked = _dot_nt(wr_ref[...], hhi)
    logits = stacked[:ne] + stacked[ne:] + _dot_nt(wr_ref[:ne, :], hlo) + br_ref[...]

    tm = logits.shape[1]
    eio = lax.broadcasted_iota(jnp.int32, (ne, tm), 0)
    vals, idxs, sels = [], [], []
    l = logits
    for _ in range(TOP_K):
        m = jnp.max(l, axis=0, keepdims=True)
        idx = jnp.min(jnp.where(l == m, eio, ne), axis=0, keepdims=True)
        sel = eio == idx
        l = jnp.where(sel, -jnp.inf, l)
        vals.append(m)
        idxs.append(idx)
        sels.append(sel)
    es = [jnp.exp(v - vals[0]) for v in vals]
    inv = 1.0 / (es[0] + es[1] + es[2] + es[3])
    sel_any = jnp.where(sels[0] | sels[1] | sels[2] | sels[3], 1.0, 0.0)
    run = run_ref[...]
    excl = _dot(sel_any.astype(BF16), tri_ref[...]) + run
    for kk in range(TOP_K):
        idx_ref[kk:kk + 1, :] = idxs[kk]
        gat_ref[kk:kk + 1, :] = es[kk] * inv
        rk = jnp.sum(jnp.where(sels[kk], excl, 0.0), axis=0, keepdims=True)
        rank_ref[kk:kk + 1, :] = rk.astype(jnp.int32)
    run = run + jnp.sum(sel_any, axis=1, keepdims=True)
    run_ref[...] = run
    cnt_ref[...] = run.astype(jnp.int32)


def _outproj(lru_o, att_o, x2, mod3, g_ffn, wo1, wo2, wr_stack, b_r, tri, b0, B, S, tm):
    D = x2.shape[1]
    T = B * S
    ns = S // tm
    C = lru_o.shape[1]
    row_in = lambda b, s: ((b0 + b) * ns + s, 0)
    row = lambda b, s: (b * ns + s, 0)
    col = lambda b, s: (0, b * ns + s)
    full = lambda b, s: (0, 0)
    return pl.pallas_call(
        _outproj_kernel,
        grid=(B, ns),
        in_specs=[
            pl.BlockSpec((tm, C), row_in),
            pl.BlockSpec((tm, C), row_in),
            pl.BlockSpec((tm, D), row_in),
            pl.BlockSpec((1, 1, D), lambda b, s: ((b0 + b) * 6 + 2, 0, 0)),
            pl.BlockSpec((1, 1, D), lambda b, s: ((b0 + b) * 6 + 3, 0, 0)),
            pl.BlockSpec((1, 1, D), lambda b, s: ((b0 + b) * 6 + 4, 0, 0)),
            pl.BlockSpec((1, D), full),
            pl.BlockSpec((C, D), full),
            pl.BlockSpec((C, D), full),
            pl.BlockSpec((2 * N_EXPERTS, D), full),
            pl.BlockSpec((N_EXPERTS, 1), full),
            pl.BlockSpec(tri.shape, full),
        ],
        out_specs=[
            pl.BlockSpec((tm, D), row),
            pl.BlockSpec((tm, D // 2), row),
            pl.BlockSpec((TOP_K, tm), col),
            pl.BlockSpec((TOP_K, tm), col),
            pl.BlockSpec((TOP_K, tm), col),
            pl.BlockSpec((N_EXPERTS, 1), full),
        ],
        out_shape=[
            jax.ShapeDtypeStruct((T, D), F32),
            jax.ShapeDtypeStruct((T, D // 2), jnp.int32),
            jax.ShapeDtypeStruct((TOP_K, T), jnp.int32),
            jax.ShapeDtypeStruct((TOP_K, T), F32),
            jax.ShapeDtypeStruct((TOP_K, T), jnp.int32),
            jax.ShapeDtypeStruct((N_EXPERTS, 1), jnp.int32),
        ],
        scratch_shapes=[pltpu.VMEM((N_EXPERTS, 1), F32)],
        compiler_params=_cparams(("arbitrary", "arbitrary")),
        name="outproj",
    )(lru_o, att_o, x2, mod3, mod3, mod3, g_ffn, wo1, wo2, wr_stack, b_r, tri)


SC_CORES = 2
SC_SUBCORES = 16
SC_WORKERS = SC_CORES * SC_SUBCORES
SC_LANES = 16


def _sc_mesh():
    return plsc.VectorSubcoreMesh(core_axis_name="c", subcore_axis_name="s",
                                  num_cores=SC_CORES, num_subcores=SC_SUBCORES)


def _sc_worker_id():
    return lax.axis_index("s") * SC_CORES + lax.axis_index("c")


def _sc_scatter_rows(rows, idx, n_out, g):
    T, W = rows.shape
    K = idx.shape[0]
    per_w = T // SC_WORKERS
    nch = per_w // g
    assert per_w * SC_WORKERS == T and nch * g == per_w and nch % 2 == 0
    idx_w = idx.reshape(K, SC_WORKERS, nch, g).transpose(1, 2, 0, 3).reshape(SC_WORKERS, nch * K, g)

    def body(rows_hbm, idx_hbm, out_hbm, idx_v, buf0, buf1, semr0, semr1, semw):
        wid = _sc_worker_id()
        base = wid * per_w
        pltpu.sync_copy(idx_hbm.at[wid], idx_v)

        def read(j, buf, sem):
            return pltpu.make_async_copy(rows_hbm.at[pl.ds(base + j * g, g)], buf, sem)

        def scatter(j, buf):
            copies = [pltpu.async_copy(buf, out_hbm.at[idx_v.at[j * K + kk]], semw)
                      for kk in range(K)]
            for cp in copies:
                cp.wait()

        read(0, buf0, semr0).start()

        @pl.loop(0, nch // 2)
        def _(jj):
            j0 = 2 * jj
            read(j0 + 1, buf1, semr1).start()
            read(j0, buf0, semr0).wait()
            scatter(j0, buf0)

            @pl.when(j0 + 2 < nch)
            def _():
                read(j0 + 2, buf0, semr0).start()

            read(j0 + 1, buf1, semr1).wait()
            scatter(j0 + 1, buf1)

    return pl.kernel(
        body,
        out_type=jax.ShapeDtypeStruct((n_out, W), rows.dtype),
        mesh=_sc_mesh(),
        scratch_types=[
            pltpu.VMEM((nch * K, g), jnp.int32),
            pltpu.VMEM((g, W), rows.dtype),
            pltpu.VMEM((g, W), rows.dtype),
            pltpu.SemaphoreType.DMA,
            pltpu.SemaphoreType.DMA,
            pltpu.SemaphoreType.DMA,
        ],
        name="sc_scatter_rows",
    )(rows, idx_w)


def _sc_gather_gated_sum(table, idx, gates, g):
    W = table.shape[1]
    K, T = idx.shape
    per_w = T // SC_WORKERS
    nch = per_w // g
    assert per_w * SC_WORKERS == T and nch * g == per_w and nch % 2 == 0 and W % SC_LANES == 0
    idx_w = idx.reshape(K, SC_WORKERS, nch, g).transpose(1, 2, 0, 3).reshape(SC_WORKERS, nch * K, g)
    gates_w = gates.reshape(K, SC_WORKERS, nch, g).transpose(1, 2, 0, 3).reshape(SC_WORKERS, nch, K * g)
    def body(table_hbm, idx_hbm, gates_hbm, out_hbm, idx_v, rows_v, gts_v, out_v, sem0, sem1):
        wid = _sc_worker_id()
        base = wid * per_w
        pltpu.sync_copy(idx_hbm.at[wid], idx_v)

        def fetch(j, slot, sem):
            cps = []
            for kk in range(K):
                cps.append(pltpu.make_async_copy(table_hbm.at[idx_v.at[j * K + kk]],
                                                 rows_v.at[slot, kk], sem))
            cps.append(pltpu.make_async_copy(gates_hbm.at[wid, j], gts_v.at[slot], sem))
            return cps

        def start(j, slot, sem):
            for cp in fetch(j, slot, sem):
                cp.start()

        def finish(j, slot, sem):
            for cp in fetch(j, slot, sem):
                cp.wait()

            @pl.loop(0, g)
            def _(t):
                gk = [plsc.load_gather(gts_v.at[slot], [jnp.full((SC_LANES,), kk * g, jnp.int32) + t])
                      for kk in range(K)]

                @plsc.parallel_loop(0, W, SC_LANES, unroll=4)
                def _(off):
                    off = pl.multiple_of(off, SC_LANES)
                    acc_lo = jnp.zeros((SC_LANES,), F32)
                    acc_hi = jnp.zeros((SC_LANES,), F32)
                    for kk in range(K):
                        w = rows_v[slot, kk, t, pl.ds(off, SC_LANES)]
                        lo, hi = plsc.unpack(plsc.bitcast(w, BF16), format=plsc.PackFormat.INTERLEAVED)
                        acc_lo = acc_lo + gk[kk] * lo
                        acc_hi = acc_hi + gk[kk] * hi
                    out_v[slot, t, pl.ds(off, SC_LANES)] = acc_lo
                    out_v[slot, t, pl.ds(W + off, SC_LANES)] = acc_hi

            pltpu.sync_copy(out_v.at[slot], out_hbm.at[pl.ds(base + j * g, g)])

        start(0, 0, sem0)

        @pl.loop(0, nch // 2)
        def _(jj):
            j0 = 2 * jj
            start(j0 + 1, 1, sem1)
            finish(j0, 0, sem0)

            @pl.when(j0 + 2 < nch)
            def _():
                start(j0 + 2, 0, sem0)

            finish(j0 + 1, 1, sem1)

    return pl.kernel(
        body,
        out_type=jax.ShapeDtypeStruct((T, 2 * W), F32),
        mesh=_sc_mesh(),
        scratch_types=[
            pltpu.VMEM((nch * K, g), jnp.int32),
            pltpu.VMEM((2, K, g, W), jnp.int32),
            pltpu.VMEM((2, K * g), F32),
            pltpu.VMEM((2, g, 2 * W), F32),
            pltpu.SemaphoreType.DMA,
            pltpu.SemaphoreType.DMA,
        ],
        compiler_params=pltpu.CompilerParams(needs_layout_passes=False),
        name="sc_gather_gated_sum",
    )(table, idx_w, gates_w)


def _experts_kernel(be_ref, bv_ref, bf_ref, bn_ref, bs_ref, xs_ref, w1_hbm, b1_ref, w2_hbm, b2_ref,
                    ys_ref, w1s_ref, w2s_ref, w1b_ref, w2b_ref, sem):
    i = pl.program_id(0)
    nvalid = bv_ref[i]

    def weight_copies(e, slot):
        return (pltpu.make_async_copy(w1_hbm.at[e], w1s_ref.at[slot], sem.at[0, slot]),
                pltpu.make_async_copy(w2_hbm.at[e], w2s_ref.at[slot], sem.at[1, slot]))

    @pl.when(bf_ref[i] > 0)
    def _():
        slot = bs_ref[i]

        @pl.when(i == 0)
        def _():
            for cp in weight_copies(be_ref[0], slot):
                cp.start()

        for cp in weight_copies(be_ref[i], slot):
            cp.wait()
        w1b_ref[...] = w1s_ref[slot].astype(BF16)
        w2b_ref[...] = w2s_ref[slot].astype(BF16)

        @pl.when(bn_ref[i] >= 0)
        def _():
            for cp in weight_copies(bn_ref[i], 1 - slot):
                cp.start()

    def ffn(rows):
        xw = xs_ref[:rows, :]
        rowi = lax.broadcasted_iota(jnp.int32, (rows, 1), 0)
        lo, hi = _unpack_halves(jnp.where(rowi < nvalid, xw, 0))
        xb = jnp.concatenate([lo.astype(BF16), hi.astype(BF16)], axis=1)
        gu = _dot(xb, w1b_ref[...]) + b1_ref[0]
        glu = jnp.minimum(gu[:, :D_FF], SWIGLU_LIMIT)
        lin = jnp.clip(gu[:, D_FF:], -SWIGLU_LIMIT, SWIGLU_LIMIT)
        act = (lin + 1.0) * (glu * _sigmoid(glu, SWIGLU_ALPHA))
        ys_ref[:rows, :] = _pack_halves(_dot(act.astype(BF16), w2b_ref[...]) + b2_ref[0])

    step = xs_ref.shape[0] // ROW_QUARTERS
    for nq in range(1, ROW_QUARTERS + 1):
        pl.when((nvalid > (nq - 1) * step) & (nvalid <= nq * step))(functools.partial(ffn, nq * step))


def _experts(blk_e, blk_v, blk_f, blk_n, blk_s, blk_r, xs, w1, b1, w2, b2):
    P, W = xs.shape
    nb = P // MOE_BLOCK
    E, D, F2 = w1.shape
    grid_spec = pltpu.PrefetchScalarGridSpec(
        num_scalar_prefetch=6,
        grid=(nb,),
        in_specs=[
            pl.BlockSpec((MOE_BLOCK, W), lambda i, be, bv, bf, bn, bs, br: (br[i], 0)),
            pl.BlockSpec(memory_space=pl.ANY),
            pl.BlockSpec((1, 1, F2), lambda i, be, bv, bf, bn, bs, br: (be[i], 0, 0)),
            pl.BlockSpec(memory_space=pl.ANY),
            pl.BlockSpec((1, 1, D), lambda i, be, bv, bf, bn, bs, br: (be[i], 0, 0)),
        ],
        out_specs=pl.BlockSpec((MOE_BLOCK, D // 2), lambda i, be, bv, bf, bn, bs, br: (br[i], 0)),
        scratch_shapes=[
            pltpu.VMEM((2, D, F2), F32),
            pltpu.VMEM((2, D_FF, D), F32),
            pltpu.VMEM((D, F2), BF16),
            pltpu.VMEM((D_FF, D), BF16),
            pltpu.SemaphoreType.DMA((2, 2)),
        ],
    )

    def kern(be_ref, bv_ref, bf_ref, bn_ref, bs_ref, br_ref, *refs):
        del br_ref
        _experts_kernel(be_ref, bv_ref, bf_ref, bn_ref, bs_ref, *refs)

    return pl.pallas_call(
        kern,
        grid_spec=grid_spec,
        out_shape=jax.ShapeDtypeStruct((P, D // 2), jnp.int32),
        compiler_params=_cparams(("arbitrary",)),
        name="experts",
    )(blk_e, blk_v, blk_f, blk_n, blk_s, blk_r, xs, w1, b1.reshape(E, 1, F2), w2,
      b2.reshape(E, 1, D))


def _combine_kernel(x1_ref, gate_ref, ysum_ref, *rest):
    o_ref = rest[-1]
    o_ref[...] = x1_ref[...] + gate_ref[0] * ysum_ref[...]


def _combine(x1, mod3, ysum, out_prev, b0, B, nb_total, S, tm):
    D = x1.shape[1]
    ns = S // tm
    row = lambda b, s: (b * ns + s, 0)
    row_out = lambda b, s: ((b0 + b) * ns + s, 0)
    in_specs = [
        pl.BlockSpec((tm, D), row),
        pl.BlockSpec((1, 1, D), lambda b, s: ((b0 + b) * 6 + 5, 0, 0)),
        pl.BlockSpec((tm, D), row),
    ]
    args = [x1, mod3, ysum]
    aliases = {}
    if out_prev is not None:
        in_specs.append(pl.BlockSpec(memory_space=pl.ANY))
        args.append(out_prev)
        aliases = {len(args) - 1: 0}
    return pl.pallas_call(
        _combine_kernel,
        grid=(B, ns),
        in_specs=in_specs,
        out_specs=pl.BlockSpec((tm, D), row_out),
        out_shape=jax.ShapeDtypeStruct((nb_total * S, D), F32),
        input_output_aliases=aliases,
        compiler_params=_cparams(("arbitrary", "arbitrary")),
        name="combine",
    )(*args)


def _block_diag(w):
    n, c, d = w.shape
    eye = jnp.eye(n, dtype=w.dtype)
    return jnp.einsum("ncd,nm->ncmd", w, eye).reshape(n * c, n * d)


def _pad_heads(w, width):
    k = w.shape[0]
    w = w.reshape(k, N_HEADS, width)
    return jnp.pad(w, ((0, 0), (0, 0), (0, HEAD_PAD - width))).reshape(k, N_HEADS * HEAD_PAD)


def kernel(x, c, positions, w_ada, b_ada, g_mix, w_in, conv_w, conv_b, w_a, b_a, w_x, b_x, lam,
           g_q_lat, w_uq, g_kv_lat, w_ukv, g_qn, g_kn, w_out, g_ffn, w_router, b_router,
           w1, b1, w2, b2):
    B, S, D = x.shape
    T = B * S
    depth = w_ada.shape[0]
    tm_in = min(512, S)
    tq = min(512, S)
    tm_out = min(512, S)
    tm_comb = min(1024, S)
    n_groups = 2 if B % 2 == 0 else 1
    Bg = B // n_groups
    Tg = Bg * S
    g_disp = min(64, Tg // SC_WORKERS // 2)
    g_comb = min(16, Tg // SC_WORKERS // 2)

    o1 = 2 * D_LRU
    o2 = o1 + Q_LORA
    o3 = o2 + KV_LORA
    tri = jnp.asarray(np.arange(tm_out)[:, None] < np.arange(tm_out)[None, :], BF16)
    cos_t, sin_t = _rope_tables(positions)

    def rot_cols(w):
        k = w.shape[0]
        w3 = w.reshape(k, -1, HEAD_PAD)
        lo = w3[:, :, ROPE_LO:ROPE_LO + ROPE_HALF]
        hi = w3[:, :, ROPE_LO + ROPE_HALF:ROPE_LO + QK_ROPE]
        zl = jnp.zeros_like(w3[:, :, :ROPE_LO])
        zr = jnp.zeros_like(w3[:, :, ROPE_LO + QK_ROPE:])
        return jnp.concatenate([zl, hi, lo, zr], axis=2).reshape(w.shape)

    x2 = x.reshape(T, D)
    for l in range(depth):
        mod3 = _ada(c, w_ada[l], b_ada[l]).reshape(B * 6, 1, D)

        w_in_l = w_in[l]
        kr_cols = jnp.pad(w_in_l[:, o3:], ((0, 0), (ROPE_LO, LANES - ROPE_LO - QK_ROPE)))
        w_in_p = jnp.concatenate([w_in_l[:, :o3], kr_cols, rot_cols(kr_cols)], axis=1).astype(BF16)
        w_uq_h = _pad_heads(w_uq[l], QK_HEAD)
        w_uq_p = jnp.concatenate([w_uq_h, rot_cols(w_uq_h)], axis=1).astype(BF16)
        w_ukv_l = w_ukv[l].reshape(KV_LORA, N_HEADS, QK_NOPE + V_HEAD)
        w_uk_h = _pad_heads(w_ukv_l[:, :, :QK_NOPE].reshape(KV_LORA, N_HEADS * QK_NOPE), QK_NOPE)
        w_uv_pairs = w_ukv_l[:, :, QK_NOPE:].reshape(KV_LORA, N_HEADS // 2, 2, V_HEAD)
        zero_v = jnp.zeros_like(w_uv_pairs[:, :, 0])
        w_uv_h = jnp.stack([w_uv_pairs[:, :, 0], zero_v, zero_v, w_uv_pairs[:, :, 1]],
                           axis=2).reshape(KV_LORA, N_HEADS * HEAD_PAD)
        w_ukv_p = jnp.concatenate([w_uk_h, w_uv_h], axis=1).astype(BF16)
        gqn_p = jnp.pad(g_qn[l], (0, HEAD_PAD - QK_HEAD)).reshape(1, HEAD_PAD)
        gkn_p = jnp.pad(g_kn[l], (0, HEAD_PAD - QK_HEAD)).reshape(1, HEAD_PAD)

        lru_o, qp, kp, v = _inproj(
            x2, cos_t, sin_t, mod3, g_mix[l].reshape(1, D), w_in_p, g_q_lat[l].reshape(1, Q_LORA),
            w_uq_p, g_kv_lat[l].reshape(1, KV_LORA), w_ukv_p, gqn_p, rot_cols(gqn_p), gkn_p,
            rot_cols(gkn_p), conv_w[l], conv_b[l].reshape(1, D_LRU),
            _block_diag(w_a[l]).astype(BF16), b_a[l].reshape(1, D_LRU),
            _block_diag(w_x[l]).astype(BF16), b_x[l].reshape(1, D_LRU),
            lam[l].reshape(1, D_LRU), B, S, tm_in)

        att_o = _attn(qp, kp, v, B, S, tq)

        w_out_b = w_out[l].astype(BF16)
        wr_stack = jnp.concatenate(_split_bf16(w_router[l].T), axis=0)
        g_ffn_l = g_ffn[l].reshape(1, D)
        b_r = b_router[l].reshape(N_EXPERTS, 1)
        eio = np.arange(N_EXPERTS, dtype=np.int32)
        n_blocks = -(-(Tg * TOP_K) // MOE_BLOCK) + N_EXPERTS
        bi = np.arange(n_blocks, dtype=np.int32)

        x_next = None
        for gi in range(n_groups):
            b0 = gi * Bg
            x1, h2p, idx_t, gat_t, rank_t, counts = _outproj(
                lru_o, att_o, x2, mod3, g_ffn_l, w_out_b[:D_LRU], w_out_b[D_LRU:],
                wr_stack, b_r, tri, b0, Bg, S, tm_out)

            counts = counts.reshape(N_EXPERTS)
            nblk_e = (counts + MOE_BLOCK - 1) // MOE_BLOCK
            blk_end = jnp.cumsum(nblk_e)
            pad_start = (blk_end - nblk_e) * MOE_BLOCK
            total = blk_end[-1]
            blk_r = jnp.minimum(bi, total - 1).astype(jnp.int32)
            blk_e = jnp.minimum(jnp.sum(blk_end[None, :] <= blk_r[:, None], axis=1),
                                N_EXPERTS - 1).astype(jnp.int32)
            blk_onehot = blk_e[:, None] == eio[None, :]
            blk_first = jnp.sum(jnp.where(blk_onehot, (blk_end - nblk_e)[None, :], 0), axis=1)
            blk_cnt = jnp.sum(jnp.where(blk_onehot, counts[None, :], 0), axis=1)
            blk_v = jnp.where(bi < total,
                              jnp.clip(blk_cnt - (bi - blk_first) * MOE_BLOCK, 0, MOE_BLOCK),
                              0).astype(jnp.int32)
            blk_f = ((bi == blk_first) & (bi < total)).astype(jnp.int32)
            nxt_first = jnp.sum(jnp.where(blk_onehot, blk_end[None, :], 0), axis=1)
            nxt_e = jnp.minimum(jnp.sum(blk_end[None, :] <= nxt_first[:, None], axis=1),
                                N_EXPERTS - 1)
            blk_n = jnp.where(nxt_first < total, nxt_e, -1).astype(jnp.int32)
            ordinal = jnp.cumsum((nblk_e > 0).astype(jnp.int32)) - 1
            blk_s = (jnp.sum(jnp.where(blk_onehot, ordinal[None, :], 0), axis=1) % 2).astype(jnp.int32)
            slot0 = jnp.sum(jnp.where(idx_t[None] == eio[:, None, None],
                                      pad_start[:, None, None], 0), axis=0)
            dest = slot0.astype(jnp.int32) + rank_t

            xs = _sc_scatter_rows(h2p, dest, n_blocks * MOE_BLOCK, g_disp)
            ys = _experts(blk_e, blk_v, blk_f, blk_n, blk_s, blk_r, xs, w1[l], b1[l], w2[l], b2[l])
            ysum = _sc_gather_gated_sum(ys, dest, gat_t, g_comb)
            x_next = _combine(x1, mod3, ysum, x_next, b0, Bg, B, S, tm_comb)
        x2 = x_next
    return x2.reshape(B, S, D)
```

```python
import functools

import jax
import jax.numpy as jnp
import numpy as np
from jax import lax
from jax.experimental import pallas as pl
from jax.experimental.pallas import tpu as pltpu
from jax.experimental.pallas import tpu_sc as plsc

D_MODEL = 1024
D_LRU = 512
LRU_BLOCKS = 8
LRU_BD = 64
CONV_W = 4
LRU_C = 8.0
N_HEADS = 8
QK_NOPE = 64
QK_ROPE = 32
QK_HEAD = 96
V_HEAD = 64
Q_LORA = 256
KV_LORA = 128
ROPE_THETA = 10000.0
N_EXPERTS = 32
TOP_K = 4
D_FF = 1024
SWIGLU_LIMIT = 7.0
SWIGLU_ALPHA = 1.702
MOE_BLOCK = 512
EPS = 1e-6

LANES = 128
SUBLANES = 8
HEAD_PAD = 128
ROPE_LO = QK_NOPE
ROPE_HALF = QK_ROPE // 2
TOK_PER_ROW = LANES // ROPE_HALF
D_IN_PAD = 2 * D_LRU + Q_LORA + KV_LORA + 2 * LANES
LOG2_E = 1.4426950408889634
ROW_QUARTERS = 4

VMEM_LIMIT = 56 * 1024 * 1024

F32 = jnp.float32
BF16 = jnp.bfloat16


def _cparams(sem):
    return pltpu.CompilerParams(dimension_semantics=sem, vmem_limit_bytes=VMEM_LIMIT)


def _dot(a, b):
    return jnp.dot(a, b, preferred_element_type=F32)


def _dot_nt(a, b):
    return lax.dot_general(a, b, (((1,), (1,)), ((), ())), preferred_element_type=F32)


def _split_bf16(a):
    hi = a.astype(BF16)
    lo = (a - hi.astype(F32)).astype(BF16)
    return hi, lo


def _sigmoid(x, scale=1.0):
    return 1.0 / (1.0 + jnp.exp2(x * (-scale * LOG2_E)))


def _pack_halves(x):
    bits = lax.bitcast_convert_type(x.astype(BF16).astype(F32), jnp.uint32)
    half = x.shape[1] // 2
    words = (bits[:, :half] >> 16) | (bits[:, half:] & jnp.uint32(0xFFFF0000))
    return lax.bitcast_convert_type(words, jnp.int32)


def _unpack_halves(words):
    w = lax.bitcast_convert_type(words, jnp.uint32)
    lo = lax.bitcast_convert_type(w << 16, F32)
    hi = lax.bitcast_convert_type(w & jnp.uint32(0xFFFF0000), F32)
    return lo, hi


def _ada_kernel(c_ref, w_ref, b_ref, o_ref):
    c = c_ref[...]
    s = c * _sigmoid(c)
    shi, slo = _split_bf16(s)
    whi, wlo = _split_bf16(w_ref[...])
    o_ref[...] = _dot(shi, whi) + _dot(slo, whi) + _dot(shi, wlo) + b_ref[...]


def _ada(c, w_ada, b_ada):
    B, D = c.shape
    N = w_ada.shape[1]
    tn = 1024
    return pl.pallas_call(
        _ada_kernel,
        grid=(N // tn,),
        in_specs=[
            pl.BlockSpec((B, D), lambda j: (0, 0)),
            pl.BlockSpec((D, tn), lambda j: (0, j)),
            pl.BlockSpec((1, tn), lambda j: (0, j)),
        ],
        out_specs=pl.BlockSpec((B, tn), lambda j: (0, j)),
        out_shape=jax.ShapeDtypeStruct((B, N), F32),
        compiler_params=_cparams(("arbitrary",)),
        name="ada",
    )(c, w_ada, b_ada.reshape(1, N))


def _trig_kernel(pos_ref, freq_ref, rsel_ref, fold_ref, cbase_ref, cos_ref, sin_ref):
    ang = pos_ref[...].astype(F32) * freq_ref[...]
    cs = jnp.concatenate([jnp.cos(ang), jnp.sin(ang)], axis=1)
    tm = cos_ref.shape[0]
    row = lax.broadcasted_iota(jnp.int32, (tm, 2 * LANES), 0)
    lane = lax.broadcasted_iota(jnp.int32, (tm, 2 * LANES), 1)
    own = ((lane % LANES) // ROPE_HALF) == (row % TOK_PER_ROW)
    rsel = rsel_ref[...]
    fold = fold_ref[...]
    by_row = sum(_dot(rsel, part) for part in _split_bf16(cs))
    mine = jnp.where(own, by_row, 0.0)
    out = sum(_dot(part, fold) for part in _split_bf16(mine))
    cos_ref[...] = out[:, :LANES] + cbase_ref[...]
    sin_ref[...] = out[:, LANES:]


def _rope_tables(positions):
    T = positions.size
    rows = T // TOK_PER_ROW
    pos_c = jnp.repeat(positions.reshape(T).astype(jnp.int32), ROPE_HALF).reshape(rows, LANES)
    tm = min(2048, T)
    tr = tm // TOK_PER_ROW
    freqs = np.float32(ROPE_THETA) ** (-np.arange(ROPE_HALF, dtype=np.float32) / np.float32(ROPE_HALF))
    freq_c = np.tile(freqs.astype(np.float32), TOK_PER_ROW).reshape(1, LANES)
    rsel = jnp.asarray(np.arange(tm)[:, None] // TOK_PER_ROW == np.arange(tr)[None, :], BF16)
    src = np.arange(LANES)[:, None] % ROPE_HALF
    dst = np.arange(LANES)[None, :]
    first = dst == ROPE_LO + src
    second = dst == ROPE_LO + ROPE_HALF + src
    fcos = (first | second).astype(np.float32)
    fsin = second.astype(np.float32) - first.astype(np.float32)
    zero = np.zeros((LANES, LANES), np.float32)
    fold = jnp.asarray(np.block([[fcos, zero], [zero, fsin]]), BF16)
    lane = np.arange(LANES)
    cbase = ((lane < ROPE_LO) | (lane >= ROPE_LO + QK_ROPE)).astype(np.float32).reshape(1, LANES)
    full = lambda i: (0, 0)
    return pl.pallas_call(
        _trig_kernel,
        grid=(T // tm,),
        in_specs=[
            pl.BlockSpec((tr, LANES), lambda i: (i, 0)),
            pl.BlockSpec((1, LANES), full),
            pl.BlockSpec((tm, tr), full),
            pl.BlockSpec((2 * LANES, 2 * LANES), full),
            pl.BlockSpec((1, LANES), full),
        ],
        out_specs=[pl.BlockSpec((tm, LANES), lambda i: (i, 0))] * 2,
        out_shape=[jax.ShapeDtypeStruct((T, LANES), F32)] * 2,
        compiler_params=_cparams(("arbitrary",)),
        name="rope_trig",
    )(pos_c, freq_c, rsel, fold, cbase)


def _inproj_kernel(x_ref, cos_ref, sin_ref, shift_ref, scale_ref, gmix_ref, win_ref, gq_ref, wuq_ref,
                   gkv_ref, wukv_ref, gqn_ref, gqr_ref, gkn_ref, gkr_ref,
                   cw_ref, cb_ref, wa_ref, ba_ref, wx_ref, bx_ref, lam_ref,
                   lru_ref, q_ref, k_ref, v_ref, tail_ref, carry_ref):
    HP = N_HEADS * HEAD_PAD

    @pl.when(pl.program_id(1) == 0)
    def _():
        tail_ref[...] = jnp.zeros_like(tail_ref)
        carry_ref[...] = jnp.zeros_like(carry_ref)

    x = x_ref[...]
    ms = jnp.mean(x * x, axis=-1, keepdims=True)
    gain = gmix_ref[...] * (1.0 + scale_ref[0])
    h = x * lax.rsqrt(ms + EPS) * gain + shift_ref[0]
    hb = h.astype(BF16)
    o1 = 2 * D_LRU
    o2 = Q_LORA
    o3 = o2 + KV_LORA
    z_lru = _dot(hb, win_ref[:, :o1])
    z = _dot(hb, win_ref[:, o1:])
    lru_ref[...] = _lru_tile(z_lru[:, :D_LRU], z_lru[:, D_LRU:], cw_ref, cb_ref, wa_ref, ba_ref,
                             wx_ref, bx_ref, lam_ref, tail_ref, carry_ref)
    ql = z[:, :o2]
    kvl = z[:, o2:o3]
    kr = z[:, o3:o3 + LANES]
    kr_rot = z[:, o3 + LANES:]

    qn = ql * lax.rsqrt(jnp.mean(ql * ql, axis=-1, keepdims=True) + EPS) * gq_ref[...]
    qq = _dot(qn.astype(BF16), wuq_ref[...])
    kvn = kvl * lax.rsqrt(jnp.mean(kvl * kvl, axis=-1, keepdims=True) + EPS) * gkv_ref[...]
    kv = _dot(kvn.astype(BF16), wukv_ref[...])

    tm = x.shape[0]
    lane = lax.broadcasted_iota(jnp.int32, (tm, HP), 1)
    pair_lane = lane & (2 * HEAD_PAD - 1)
    ones_cols = (pair_lane >= V_HEAD) & (pair_lane < 2 * HEAD_PAD - V_HEAD)
    v_ref[...] = jnp.where(ones_cols, 1.0, kv[:, HP:]).astype(BF16)

    cos_t = cos_ref[...]
    sin_t = sin_ref[...]
    gqn = gqn_ref[...]
    gkn = gkn_ref[...]
    cq = gqn * cos_t
    sq = gqr_ref[...] * sin_t
    kb = kr * (gkn * cos_t) + kr_rot * (gkr_ref[...] * sin_t)
    inv_w = 1.0 / QK_HEAD
    qscale = QK_HEAD ** -0.5 * LOG2_E
    for hh in range(N_HEADS):
        sl = slice(hh * HEAD_PAD, (hh + 1) * HEAD_PAD)
        qh = qq[:, sl]
        rq = lax.rsqrt(jnp.sum(qh * qh, axis=-1, keepdims=True) * inv_w + EPS) * qscale
        q_ref[:, sl] = ((qh * cq + qq[:, HP + hh * HEAD_PAD:HP + (hh + 1) * HEAD_PAD] * sq) * rq).astype(BF16)
        kraw = kv[:, sl] + kr
        rk = lax.rsqrt(jnp.sum(kraw * kraw, axis=-1, keepdims=True) * inv_w + EPS)
        k_ref[:, sl] = ((kv[:, sl] * gkn + kb) * rk).astype(BF16)


def _inproj(x2, cos_t, sin_t, mod3, g_mix, w_in_p, g_q_lat, w_uq_p, g_kv_lat, w_ukv_p,
            gqn_p, gqr_p, gkn_p, gkr_p, conv_w, conv_b, wa_d, b_a, wx_d, b_x, lam, B, S, tm):
    T, D = x2.shape
    ns = S // tm
    HP = N_HEADS * HEAD_PAD
    C = D_LRU
    row = lambda b, s: (b * ns + s, 0)
    full = lambda b, s: (0, 0)
    return pl.pallas_call(
        _inproj_kernel,
        grid=(B, ns),
        in_specs=[
            pl.BlockSpec((tm, D), row),
            pl.BlockSpec((tm, LANES), row),
            pl.BlockSpec((tm, LANES), row),
            pl.BlockSpec((1, 1, D), lambda b, s: (b * 6 + 0, 0, 0)),
            pl.BlockSpec((1, 1, D), lambda b, s: (b * 6 + 1, 0, 0)),
            pl.BlockSpec((1, D), full),
            pl.BlockSpec((D, D_IN_PAD), full),
            pl.BlockSpec((1, Q_LORA), full),
            pl.BlockSpec((Q_LORA, 2 * HP), full),
            pl.BlockSpec((1, KV_LORA), full),
            pl.BlockSpec((KV_LORA, 2 * HP), full),
            pl.BlockSpec((1, HEAD_PAD), full),
            pl.BlockSpec((1, HEAD_PAD), full),
            pl.BlockSpec((1, HEAD_PAD), full),
            pl.BlockSpec((1, HEAD_PAD), full),
            pl.BlockSpec((CONV_W, C), full),
            pl.BlockSpec((1, C), full),
            pl.BlockSpec((C, C), full),
            pl.BlockSpec((1, C), full),
            pl.BlockSpec((C, C), full),
            pl.BlockSpec((1, C), full),
            pl.BlockSpec((1, C), full),
        ],
        out_specs=[
            pl.BlockSpec((tm, C), row),
            pl.BlockSpec((tm, HP), row),
            pl.BlockSpec((tm, HP), row),
            pl.BlockSpec((tm, HP), row),
        ],
        out_shape=[
            jax.ShapeDtypeStruct((T, C), BF16),
            jax.ShapeDtypeStruct((T, HP), BF16),
            jax.ShapeDtypeStruct((T, HP), BF16),
            jax.ShapeDtypeStruct((T, HP), BF16),
        ],
        scratch_shapes=[pltpu.VMEM((SUBLANES, C), F32), pltpu.VMEM((SUBLANES, C), F32)],
        compiler_params=_cparams(("arbitrary", "arbitrary")),
        name="inproj",
    )(x2, cos_t, sin_t, mod3, mod3, g_mix, w_in_p, g_q_lat, w_uq_p, g_kv_lat, w_ukv_p,
      gqn_p, gqr_p, gkn_p, gkr_p, conv_w, conv_b, wa_d, b_a, wx_d, b_x, lam)


def _gelu_tanh(x):
    c = 0.7978845608028654
    hx = 0.5 * x
    return hx + hx * jnp.tanh(x * (c + (c * 0.044715) * (x * x)))


def _lru_tile(x, y, cw_ref, cb_ref, wa_ref, ba_ref, wx_ref, bx_ref, lam_ref, tail_ref, carry_ref):
    ts = x.shape[0]
    xext = jnp.concatenate([tail_ref[...], x], axis=0)
    cw = cw_ref[...]
    xc = x * cw[CONV_W - 1:CONV_W, :]
    for j in range(CONV_W - 1):
        sh = CONV_W - 1 - j
        xc = xc + xext[8 - sh:8 - sh + ts, :] * cw[j:j + 1, :]
    xc = xc + cb_ref[...]
    tail_ref[...] = x[ts - 8:, :]

    xb = xc.astype(BF16)
    r = _sigmoid(_dot(xb, wa_ref[...]) + ba_ref[...])
    i = _sigmoid(_dot(xb, wx_ref[...]) + bx_ref[...])
    lam = lam_ref[...]
    nl = -lam
    softplus = jnp.maximum(nl, 0.0) + jnp.log(1.0 + jnp.exp(-jnp.abs(nl)))
    log_a = (-LRU_C) * r * softplus
    a = jnp.exp(log_a)
    mult = jnp.sqrt(1.0 - a * a)
    u = mult * (i * xc)

    C = a.shape[1]
    a = a.reshape(ts // SUBLANES, SUBLANES, C)
    u = u.reshape(ts // SUBLANES, SUBLANES, C)
    sub = lax.broadcasted_iota(jnp.int32, (1, SUBLANES, 1), 1)
    sh = 1
    while sh < SUBLANES:
        a_prev = pltpu.roll(a, sh, axis=1)
        u_prev = pltpu.roll(u, sh, axis=1)
        m = sub >= sh
        u = jnp.where(m, a * u_prev + u, u)
        a = jnp.where(m, a * a_prev, a)
        sh *= 2
    a = a.reshape(ts, C)
    u = u.reshape(ts, C)
    h = carry_ref[0:1, :]
    groups = []
    for g0 in range(0, ts, SUBLANES):
        hg = u[g0:g0 + SUBLANES, :] + a[g0:g0 + SUBLANES, :] * h
        groups.append(hg)
        h = hg[SUBLANES - 1:SUBLANES, :]
    carry_ref[...] = jnp.broadcast_to(h, carry_ref.shape)
    hs = jnp.concatenate(groups, axis=0)
    return (_gelu_tanh(y) * hs).astype(BF16)


NEG_INF = -1e30


def _attn_kernel(q_ref, k_ref, v_ref, o_ref, *state, tq, tk):
    m_refs = state[:N_HEADS]
    acc_refs = state[N_HEADS:]
    qi = pl.program_id(1)
    causal = (lax.broadcasted_iota(jnp.int32, (tk, tk), 1)
              <= lax.broadcasted_iota(jnp.int32, (tk, tk), 0))
    lower = slice(tk, tq)

    def head_slice(hh):
        return slice(hh * HEAD_PAD, (hh + 1) * HEAD_PAD)

    def weights(sc, m_b):
        cols = [jnp.exp2(sc[:, c0:c0 + LANES] - m_b) for c0 in range(0, tk, LANES)]
        return jnp.concatenate(cols, axis=1).astype(BF16)

    def scores(hh, r0, rows):
        hs = head_slice(hh)
        return _dot_nt(q_ref[rows, hs], k_ref[pl.ds(r0, tk), hs])

    def update(hh, rows, sc, r0):
        hs = head_slice(hh)
        m_b = m_refs[hh][rows, :]
        m_new = jnp.maximum(m_b, jnp.max(sc, axis=-1, keepdims=True))
        alpha = jnp.exp2(m_b - m_new)
        m_refs[hh][rows, :] = m_new
        acc_refs[hh][rows, :] = (alpha * acc_refs[hh][rows, :]
                                 + _dot(weights(sc, m_new), v_ref[pl.ds(r0, tk), hs]))


    every = slice(0, tq)
    r_d0 = pl.multiple_of(qi * tq, tk)
    sc_next = scores(0, r_d0, every)
    for hh in range(N_HEADS):
        hs = head_slice(hh)
        sc = jnp.concatenate([jnp.where(causal, sc_next[:tk], NEG_INF), sc_next[tk:]], axis=0)
        if hh + 1 < N_HEADS:
            sc_next = scores(hh + 1, r_d0, every)
        m_b = jnp.broadcast_to(jnp.max(sc, axis=-1, keepdims=True), (tq, LANES))
        m_refs[hh][...] = m_b
        acc_refs[hh][...] = _dot(weights(sc, m_b), v_ref[pl.ds(r_d0, tk), hs])

    r_d1 = pl.multiple_of(qi * tq + tk, tk)
    sc_next = scores(0, r_d1, lower)
    for hh in range(N_HEADS):
        sc = jnp.where(causal, sc_next, NEG_INF)
        if hh + 1 < N_HEADS:
            sc_next = scores(hh + 1, r_d1, lower)
        update(hh, lower, sc, r_d1)

    @pl.loop(0, qi * (tq // tk))
    def _(j):
        r0 = pl.multiple_of(j * tk, tk)
        sc_next = scores(0, r0, every)
        for hh in range(N_HEADS):
            sc = sc_next
            if hh + 1 < N_HEADS:
                sc_next = scores(hh + 1, r0, every)
            update(hh, every, sc, r0)

    low = lax.broadcasted_iota(jnp.int32, (tq, HEAD_PAD), 1) < V_HEAD
    for he in range(0, N_HEADS, 2):
        acc_e = acc_refs[he][...]
        acc_o = acc_refs[he + 1][...]
        num = jnp.where(low, acc_e, acc_o)
        den = pltpu.roll(jnp.where(low, acc_o, acc_e), V_HEAD, axis=1)
        o_ref[:, he * V_HEAD:(he + 2) * V_HEAD] = (num / den).astype(BF16)


def _attn(qp, kp, v, B, S, tk):
    T = qp.shape[0]
    tq = 2 * tk
    nq = S // tq
    HP = N_HEADS * HEAD_PAD
    HV = N_HEADS * V_HEAD
    return pl.pallas_call(
        functools.partial(_attn_kernel, tq=tq, tk=tk),
        grid=(B, nq),
        in_specs=[
            pl.BlockSpec((tq, HP), lambda b, i: (b * nq + i, 0)),
            pl.BlockSpec((S, HP), lambda b, i: (b, 0)),
            pl.BlockSpec((S, HP), lambda b, i: (b, 0)),
        ],
        out_specs=pl.BlockSpec((tq, HV), lambda b, i: (b * nq + i, 0)),
        out_shape=jax.ShapeDtypeStruct((T, HV), BF16),
        scratch_shapes=([pltpu.VMEM((tq, LANES), F32)] * N_HEADS
                        + [pltpu.VMEM((tq, HEAD_PAD), F32)] * N_HEADS),
        compiler_params=_cparams(("arbitrary", "arbitrary")),
        name="attn",
    )(qp, kp, v)


def _outproj_kernel(lru_ref, att_ref, x_ref, gate_ref, shift_ref, scale_ref, gffn_ref,
                    wo1_ref, wo2_ref, wr_ref, br_ref, tri_ref,
                    x1_ref, h2p_ref, idx_ref, gat_ref, rank_ref, cnt_ref, run_ref):
    first = (pl.program_id(0) == 0) & (pl.program_id(1) == 0)

    @pl.when(first)
    def _():
        run_ref[...] = jnp.zeros_like(run_ref)

    mix = _dot(lru_ref[...], wo1_ref[...]) + _dot(att_ref[...], wo2_ref[...])
    x1 = x_ref[...] + gate_ref[0] * mix
    x1_ref[...] = x1
    ms = jnp.mean(x1 * x1, axis=-1, keepdims=True)
    gain = gffn_ref[...] * (1.0 + scale_ref[0])
    h2 = x1 * lax.rsqrt(ms + EPS) * gain + shift_ref[0]

    hhi = h2.astype(BF16)
    hlo = (h2 - hhi.astype(F32)).astype(BF16)
    h2p_ref[...] = _pack_halves(h2)

    ne = br_ref.shape[0]
    stacked = _dot_nt(wr_ref[...], hhi)
    logits = stacked[:ne] + stacked[ne:] + _dot_nt(wr_ref[:ne, :], hlo) + br_ref[...]

    tm = logits.shape[1]
    eio = lax.broadcasted_iota(jnp.int32, (ne, tm), 0)
    vals, idxs, sels = [], [], []
    l = logits
    for _ in range(TOP_K):
        m = jnp.max(l, axis=0, keepdims=True)
        idx = jnp.min(jnp.where(l == m, eio, ne), axis=0, keepdims=True)
        sel = eio == idx
        l = jnp.where(sel, -jnp.inf, l)
        vals.append(m)
        idxs.append(idx)
        sels.append(sel)
    es = [jnp.exp(v - vals[0]) for v in vals]
    inv = 1.0 / (es[0] + es[1] + es[2] + es[3])
    sel_any = jnp.where(sels[0] | sels[1] | sels[2] | sels[3], 1.0, 0.0)
    run = run_ref[...]
    excl = _dot(sel_any.astype(BF16), tri_ref[...]) + run
    for kk in range(TOP_K):
        idx_ref[kk:kk + 1, :] = idxs[kk]
        gat_ref[kk:kk + 1, :] = es[kk] * inv
        rk = jnp.sum(jnp.where(sels[kk], excl, 0.0), axis=0, keepdims=True)
        rank_ref[kk:kk + 1, :] = rk.astype(jnp.int32)
    run = run + jnp.sum(sel_any, axis=1, keepdims=True)
    run_ref[...] = run
    cnt_ref[...] = run.astype(jnp.int32)


def _outproj(lru_o, att_o, x2, mod3, g_ffn, wo1, wo2, wr_stack, b_r, tri, b0, B, S, tm):
    D = x2.shape[1]
    T = B * S
    ns = S // tm
    C = lru_o.shape[1]
    row_in = lambda b, s: ((b0 + b) * ns + s, 0)
    row = lambda b, s: (b * ns + s, 0)
    col = lambda b, s: (0, b * ns + s)
    full = lambda b, s: (0, 0)
    return pl.pallas_call(
        _outproj_kernel,
        grid=(B, ns),
        in_specs=[
            pl.BlockSpec((tm, C), row_in),
            pl.BlockSpec((tm, C), row_in),
            pl.BlockSpec((tm, D), row_in),
            pl.BlockSpec((1, 1, D), lambda b, s: ((b0 + b) * 6 + 2, 0, 0)),
            pl.BlockSpec((1, 1, D), lambda b, s: ((b0 + b) * 6 + 3, 0, 0)),
            pl.BlockSpec((1, 1, D), lambda b, s: ((b0 + b) * 6 + 4, 0, 0)),
            pl.BlockSpec((1, D), full),
            pl.BlockSpec((C, D), full),
            pl.BlockSpec((C, D), full),
            pl.BlockSpec((2 * N_EXPERTS, D), full),
            pl.BlockSpec((N_EXPERTS, 1), full),
            pl.BlockSpec(tri.shape, full),
        ],
        out_specs=[
            pl.BlockSpec((tm, D), row),
            pl.BlockSpec((tm, D // 2), row),
            pl.BlockSpec((TOP_K, tm), col),
            pl.BlockSpec((TOP_K, tm), col),
            pl.BlockSpec((TOP_K, tm), col),
            pl.BlockSpec((N_EXPERTS, 1), full),
        ],
        out_shape=[
            jax.ShapeDtypeStruct((T, D), F32),
            jax.ShapeDtypeStruct((T, D // 2), jnp.int32),
            jax.ShapeDtypeStruct((TOP_K, T), jnp.int32),
            jax.ShapeDtypeStruct((TOP_K, T), F32),
            jax.ShapeDtypeStruct((TOP_K, T), jnp.int32),
            jax.ShapeDtypeStruct((N_EXPERTS, 1), jnp.int32),
        ],
        scratch_shapes=[pltpu.VMEM((N_EXPERTS, 1), F32)],
        compiler_params=_cparams(("arbitrary", "arbitrary")),
        name="outproj",
    )(lru_o, att_o, x2, mod3, mod3, mod3, g_ffn, wo1, wo2, wr_stack, b_r, tri)


SC_CORES = 2
SC_SUBCORES = 16
SC_WORKERS = SC_CORES * SC_SUBCORES
SC_LANES = 16


def _sc_mesh():
    return plsc.VectorSubcoreMesh(core_axis_name="c", subcore_axis_name="s",
                                  num_cores=SC_CORES, num_subcores=SC_SUBCORES)


def _sc_worker_id():
    return lax.axis_index("s") * SC_CORES + lax.axis_index("c")


def _sc_scatter_rows(rows, idx, n_out, g):
    T, W = rows.shape
    K = idx.shape[0]
    per_w = T // SC_WORKERS
    nch = per_w // g
    assert per_w * SC_WORKERS == T and nch * g == per_w and nch % 2 == 0
    idx_w = idx.reshape(K, SC_WORKERS, nch, g).transpose(1, 2, 0, 3).reshape(SC_WORKERS, nch * K, g)

    def body(rows_hbm, idx_hbm, out_hbm, idx_v, buf0, buf1, semr0, semr1, semw):
        wid = _sc_worker_id()
        base = wid * per_w
        pltpu.sync_copy(idx_hbm.at[wid], idx_v)

        def read(j, buf, sem):
            return pltpu.make_async_copy(rows_hbm.at[pl.ds(base + j * g, g)], buf, sem)

        def scatter(j, buf):
            copies = [pltpu.async_copy(buf, out_hbm.at[idx_v.at[j * K + kk]], semw)
                      for kk in range(K)]
            for cp in copies:
                cp.wait()

        read(0, buf0, semr0).start()

        @pl.loop(0, nch // 2)
        def _(jj):
            j0 = 2 * jj
            read(j0 + 1, buf1, semr1).start()
            read(j0, buf0, semr0).wait()
            scatter(j0, buf0)

            @pl.when(j0 + 2 < nch)
            def _():
                read(j0 + 2, buf0, semr0).start()

            read(j0 + 1, buf1, semr1).wait()
            scatter(j0 + 1, buf1)

    return pl.kernel(
        body,
        out_type=jax.ShapeDtypeStruct((n_out, W), rows.dtype),
        mesh=_sc_mesh(),
        scratch_types=[
            pltpu.VMEM((nch * K, g), jnp.int32),
            pltpu.VMEM((g, W), rows.dtype),
            pltpu.VMEM((g, W), rows.dtype),
            pltpu.SemaphoreType.DMA,
            pltpu.SemaphoreType.DMA,
            pltpu.SemaphoreType.DMA,
        ],
        name="sc_scatter_rows",
    )(rows, idx_w)


def _sc_gather_gated_sum(table, idx, gates, g):
    W = table.shape[1]
    K, T = idx.shape
    per_w = T // SC_WORKERS
    nch = per_w // g
    assert per_w * SC_WORKERS == T and nch * g == per_w and nch % 2 == 0 and W % SC_LANES == 0
    idx_w = idx.reshape(K, SC_WORKERS, nch, g).transpose(1, 2, 0, 3).reshape(SC_WORKERS, nch * K, g)
    gates_w = gates.reshape(K, SC_WORKERS, nch, g).transpose(1, 2, 0, 3).reshape(SC_WORKERS, nch, K * g)
    def body(table_hbm, idx_hbm, gates_hbm, out_hbm, idx_v, rows_v, gts_v, out_v, sem0, sem1):
        wid = _sc_worker_id()
        base = wid * per_w
        pltpu.sync_copy(idx_hbm.at[wid], idx_v)

        def fetch(j, slot, sem):
            cps = []
            for kk in range(K):
                cps.append(pltpu.make_async_copy(table_hbm.at[idx_v.at[j * K + kk]],
                                                 rows_v.at[slot, kk], sem))
            cps.append(pltpu.make_async_copy(gates_hbm.at[wid, j], gts_v.at[slot], sem))
            return cps

        def start(j, slot, sem):
            for cp in fetch(j, slot, sem):
                cp.start()

        def finish(j, slot, sem):
            for cp in fetch(j, slot, sem):
                cp.wait()

            @pl.loop(0, g)
            def _(t):
                gk = [plsc.load_gather(gts_v.at[slot], [jnp.full((SC_LANES,), kk * g, jnp.int32) + t])
                      for kk in range(K)]

                @plsc.parallel_loop(0, W, SC_LANES, unroll=4)
                def _(off):
                    off = pl.multiple_of(off, SC_LANES)
                    acc_lo = jnp.zeros((SC_LANES,), F32)
                    acc_hi = jnp.zeros((SC_LANES,), F32)
                    for kk in range(K):
                        w = rows_v[slot, kk, t, pl.ds(off, SC_LANES)]
                        lo, hi = plsc.unpack(plsc.bitcast(w, BF16), format=plsc.PackFormat.INTERLEAVED)
                        acc_lo = acc_lo + gk[kk] * lo
                        acc_hi = acc_hi + gk[kk] * hi
                    out_v[slot, t, pl.ds(off, SC_LANES)] = acc_lo
                    out_v[slot, t, pl.ds(W + off, SC_LANES)] = acc_hi

            pltpu.sync_copy(out_v.at[slot], out_hbm.at[pl.ds(base + j * g, g)])

        start(0, 0, sem0)

        @pl.loop(0, nch // 2)
        def _(jj):
            j0 = 2 * jj
            start(j0 + 1, 1, sem1)
            finish(j0, 0, sem0)

            @pl.when(j0 + 2 < nch)
            def _():
                start(j0 + 2, 0, sem0)

            finish(j0 + 1, 1, sem1)

    return pl.kernel(
        body,
        out_type=jax.ShapeDtypeStruct((T, 2 * W), F32),
        mesh=_sc_mesh(),
        scratch_types=[
            pltpu.VMEM((nch * K, g), jnp.int32),
            pltpu.VMEM((2, K, g, W), jnp.int32),
            pltpu.VMEM((2, K * g), F32),
            pltpu.VMEM((2, g, 2 * W), F32),
            pltpu.SemaphoreType.DMA,
            pltpu.SemaphoreType.DMA,
        ],
        compiler_params=pltpu.CompilerParams(needs_layout_passes=False),
        name="sc_gather_gated_sum",
    )(table, idx_w, gates_w)


def _experts_kernel(be_ref, bv_ref, bf_ref, bn_ref, bs_ref, xs_ref, w1_hbm, b1_ref, w2_hbm, b2_ref,
                    ys_ref, w1s_ref, w2s_ref, w1b_ref, w2b_ref, sem):
    i = pl.program_id(0)
    nvalid = bv_ref[i]

    def weight_copies(e, slot):
        return (pltpu.make_async_copy(w1_hbm.at[e], w1s_ref.at[slot], sem.at[0, slot]),
                pltpu.make_async_copy(w2_hbm.at[e], w2s_ref.at[slot], sem.at[1, slot]))

    @pl.when(bf_ref[i] > 0)
    def _():
        slot = bs_ref[i]

        @pl.when(i == 0)
        def _():
            for cp in weight_copies(be_ref[0], slot):
                cp.start()

        for cp in weight_copies(be_ref[i], slot):
            cp.wait()
        w1b_ref[...] = w1s_ref[slot].astype(BF16)
        w2b_ref[...] = w2s_ref[slot].astype(BF16)

        @pl.when(bn_ref[i] >= 0)
        def _():
            for cp in weight_copies(bn_ref[i], 1 - slot):
                cp.start()

    def ffn(rows):
        xw = xs_ref[:rows, :]
        rowi = lax.broadcasted_iota(jnp.int32, (rows, 1), 0)
        lo, hi = _unpack_halves(jnp.where(rowi < nvalid, xw, 0))
        xb = jnp.concatenate([lo.astype(BF16), hi.astype(BF16)], axis=1)
        gu = _dot(xb, w1b_ref[...]) + b1_ref[0]
        glu = jnp.minimum(gu[:, :D_FF], SWIGLU_LIMIT)
        lin = jnp.clip(gu[:, D_FF:], -SWIGLU_LIMIT, SWIGLU_LIMIT)
        act = (lin + 1.0) * (glu * _sigmoid(glu, SWIGLU_ALPHA))
        ys_ref[:rows, :] = _pack_halves(_dot(act.astype(BF16), w2b_ref[...]) + b2_ref[0])

    step = xs_ref.shape[0] // ROW_QUARTERS
    for nq in range(1, ROW_QUARTERS + 1):
        pl.when((nvalid > (nq - 1) * step) & (nvalid <= nq * step))(functools.partial(ffn, nq * step))


def _experts(blk_e, blk_v, blk_f, blk_n, blk_s, blk_r, xs, w1, b1, w2, b2):
    P, W = xs.shape
    nb = P // MOE_BLOCK
    E, D, F2 = w1.shape
    grid_spec = pltpu.PrefetchScalarGridSpec(
        num_scalar_prefetch=6,
        grid=(nb,),
        in_specs=[
            pl.BlockSpec((MOE_BLOCK, W), lambda i, be, bv, bf, bn, bs, br: (br[i], 0)),
            pl.BlockSpec(memory_space=pl.ANY),
            pl.BlockSpec((1, 1, F2), lambda i, be, bv, bf, bn, bs, br: (be[i], 0, 0)),
            pl.BlockSpec(memory_space=pl.ANY),
            pl.BlockSpec((1, 1, D), lambda i, be, bv, bf, bn, bs, br: (be[i], 0, 0)),
        ],
        out_specs=pl.BlockSpec((MOE_BLOCK, D // 2), lambda i, be, bv, bf, bn, bs, br: (br[i], 0)),
        scratch_shapes=[
            pltpu.VMEM((2, D, F2), F32),
            pltpu.VMEM((2, D_FF, D), F32),
            pltpu.VMEM((D, F2), BF16),
            pltpu.VMEM((D_FF, D), BF16),
            pltpu.SemaphoreType.DMA((2, 2)),
        ],
    )

    def kern(be_ref, bv_ref, bf_ref, bn_ref, bs_ref, br_ref, *refs):
        del br_ref
        _experts_kernel(be_ref, bv_ref, bf_ref, bn_ref, bs_ref, *refs)

    return pl.pallas_call(
        kern,
        grid_spec=grid_spec,
        out_shape=jax.ShapeDtypeStruct((P, D // 2), jnp.int32),
        compiler_params=_cparams(("arbitrary",)),
        name="experts",
    )(blk_e, blk_v, blk_f, blk_n, blk_s, blk_r, xs, w1, b1.reshape(E, 1, F2), w2,
      b2.reshape(E, 1, D))


def _combine_kernel(x1_ref, gate_ref, ysum_ref, *rest):
    o_ref = rest[-1]
    o_ref[...] = x1_ref[...] + gate_ref[0] * ysum_ref[...]


def _combine(x1, mod3, ysum, out_prev, b0, B, nb_total, S, tm):
    D = x1.shape[1]
    ns = S // tm
    row = lambda b, s: (b * ns + s, 0)
    row_out = lambda b, s: ((b0 + b) * ns + s, 0)
    in_specs = [
        pl.BlockSpec((tm, D), row),
        pl.BlockSpec((1, 1, D), lambda b, s: ((b0 + b) * 6 + 5, 0, 0)),
        pl.BlockSpec((tm, D), row),
    ]
    args = [x1, mod3, ysum]
    aliases = {}
    if out_prev is not None:
        in_specs.append(pl.BlockSpec(memory_space=pl.ANY))
        args.append(out_prev)
        aliases = {len(args) - 1: 0}
    return pl.pallas_call(
        _combine_kernel,
        grid=(B, ns),
        in_specs=in_specs,
        out_specs=pl.BlockSpec((tm, D), row_out),
        out_shape=jax.ShapeDtypeStruct((nb_total * S, D), F32),
        input_output_aliases=aliases,
        compiler_params=_cparams(("arbitrary", "arbitrary")),
        name="combine",
    )(*args)


def _block_diag(w):
    n, c, d = w.shape
    eye = jnp.eye(n, dtype=w.dtype)
    return jnp.einsum("ncd,nm->ncmd", w, eye).reshape(n * c, n * d)


def _pad_heads(w, width):
    k = w.shape[0]
    w = w.reshape(k, N_HEADS, width)
    return jnp.pad(w, ((0, 0), (0, 0), (0, HEAD_PAD - width))).reshape(k, N_HEADS * HEAD_PAD)


def kernel(x, c, positions, w_ada, b_ada, g_mix, w_in, conv_w, conv_b, w_a, b_a, w_x, b_x, lam,
           g_q_lat, w_uq, g_kv_lat, w_ukv, g_qn, g_kn, w_out, g_ffn, w_router, b_router,
           w1, b1, w2, b2):
    B, S, D = x.shape
    T = B * S
    depth = w_ada.shape[0]
    tm_in = min(512, S)
    tk_att = min(512, S // 2)
    tm_out = min(512, S)
    tm_comb = min(1024, S)
    n_groups = 2 if B % 2 == 0 else 1
    Bg = B // n_groups
    Tg = Bg * S
    g_disp = min(64, Tg // SC_WORKERS // 2)
    g_comb = min(16, Tg // SC_WORKERS // 2)

    o1 = 2 * D_LRU
    o2 = o1 + Q_LORA
    o3 = o2 + KV_LORA
    tri = jnp.asarray(np.arange(tm_out)[:, None] < np.arange(tm_out)[None, :], BF16)
    cos_t, sin_t = _rope_tables(positions)

    def rot_cols(w):
        k = w.shape[0]
        w3 = w.reshape(k, -1, HEAD_PAD)
        lo = w3[:, :, ROPE_LO:ROPE_LO + ROPE_HALF]
        hi = w3[:, :, ROPE_LO + ROPE_HALF:ROPE_LO + QK_ROPE]
        zl = jnp.zeros_like(w3[:, :, :ROPE_LO])
        zr = jnp.zeros_like(w3[:, :, ROPE_LO + QK_ROPE:])
        return jnp.concatenate([zl, hi, lo, zr], axis=2).reshape(w.shape)

    x2 = x.reshape(T, D)
    for l in range(depth):
        mod3 = _ada(c, w_ada[l], b_ada[l]).reshape(B * 6, 1, D)

        w_in_l = w_in[l]
        kr_cols = jnp.pad(w_in_l[:, o3:], ((0, 0), (ROPE_LO, LANES - ROPE_LO - QK_ROPE)))
        w_in_p = jnp.concatenate([w_in_l[:, :o3], kr_cols, rot_cols(kr_cols)], axis=1).astype(BF16)
        w_uq_h = _pad_heads(w_uq[l], QK_HEAD)
        w_uq_p = jnp.concatenate([w_uq_h, rot_cols(w_uq_h)], axis=1).astype(BF16)
        w_ukv_l = w_ukv[l].reshape(KV_LORA, N_HEADS, QK_NOPE + V_HEAD)
        w_uk_h = _pad_heads(w_ukv_l[:, :, :QK_NOPE].reshape(KV_LORA, N_HEADS * QK_NOPE), QK_NOPE)
        w_uv_pairs = w_ukv_l[:, :, QK_NOPE:].reshape(KV_LORA, N_HEADS // 2, 2, V_HEAD)
        zero_v = jnp.zeros_like(w_uv_pairs[:, :, 0])
        w_uv_h = jnp.stack([w_uv_pairs[:, :, 0], zero_v, zero_v, w_uv_pairs[:, :, 1]],
                           axis=2).reshape(KV_LORA, N_HEADS * HEAD_PAD)
        w_ukv_p = jnp.concatenate([w_uk_h, w_uv_h], axis=1).astype(BF16)
        gqn_p = jnp.pad(g_qn[l], (0, HEAD_PAD - QK_HEAD)).reshape(1, HEAD_PAD)
        gkn_p = jnp.pad(g_kn[l], (0, HEAD_PAD - QK_HEAD)).reshape(1, HEAD_PAD)

        lru_o, qp, kp, v = _inproj(
            x2, cos_t, sin_t, mod3, g_mix[l].reshape(1, D), w_in_p, g_q_lat[l].reshape(1, Q_LORA),
            w_uq_p, g_kv_lat[l].reshape(1, KV_LORA), w_ukv_p, gqn_p, rot_cols(gqn_p), gkn_p,
            rot_cols(gkn_p), conv_w[l], conv_b[l].reshape(1, D_LRU),
            _block_diag(w_a[l]).astype(BF16), b_a[l].reshape(1, D_LRU),
            _block_diag(w_x[l]).astype(BF16), b_x[l].reshape(1, D_LRU),
            lam[l].reshape(1, D_LRU), B, S, tm_in)

        att_o = _attn(qp, kp, v, B, S, tk_att)

        w_out_b = w_out[l].astype(BF16)
        wr_stack = jnp.concatenate(_split_bf16(w_router[l].T), axis=0)
        g_ffn_l = g_ffn[l].reshape(1, D)
        b_r = b_router[l].reshape(N_EXPERTS, 1)
        eio = np.arange(N_EXPERTS, dtype=np.int32)
        n_blocks = -(-(Tg * TOP_K) // MOE_BLOCK) + N_EXPERTS
        bi = np.arange(n_blocks, dtype=np.int32)

        x_next = None
        for gi in range(n_groups):
            b0 = gi * Bg
            x1, h2p, idx_t, gat_t, rank_t, counts = _outproj(
                lru_o, att_o, x2, mod3, g_ffn_l, w_out_b[:D_LRU], w_out_b[D_LRU:],
                wr_stack, b_r, tri, b0, Bg, S, tm_out)

            counts = counts.reshape(N_EXPERTS)
            nblk_e = (counts + MOE_BLOCK - 1) // MOE_BLOCK
            blk_end = jnp.cumsum(nblk_e)
            pad_start = (blk_end - nblk_e) * MOE_BLOCK
            total = blk_end[-1]
            blk_r = jnp.minimum(bi, total - 1).astype(jnp.int32)
            blk_e = jnp.minimum(jnp.sum(blk_end[None, :] <= blk_r[:, None], axis=1),
                                N_EXPERTS - 1).astype(jnp.int32)
            blk_onehot = blk_e[:, None] == eio[None, :]
            blk_first = jnp.sum(jnp.where(blk_onehot, (blk_end - nblk_e)[None, :], 0), axis=1)
            blk_cnt = jnp.sum(jnp.where(blk_onehot, counts[None, :], 0), axis=1)
            blk_v = jnp.where(bi < total,
                              jnp.clip(blk_cnt - (bi - blk_first) * MOE_BLOCK, 0, MOE_BLOCK),
                              0).astype(jnp.int32)
            blk_f = ((bi == blk_first) & (bi < total)).astype(jnp.int32)
            nxt_first = jnp.sum(jnp.where(blk_onehot, blk_end[None, :], 0), axis=1)
            nxt_e = jnp.minimum(jnp.sum(blk_end[None, :] <= nxt_first[:, None], axis=1),
                                N_EXPERTS - 1)
            blk_n = jnp.where(nxt_first < total, nxt_e, -1).astype(jnp.int32)
            ordinal = jnp.cumsum((nblk_e > 0).astype(jnp.int32)) - 1
            blk_s = (jnp.sum(jnp.where(blk_onehot, ordinal[None, :], 0), axis=1) % 2).astype(jnp.int32)
            slot0 = jnp.sum(jnp.where(idx_t[None] == eio[:, None, None],
                                      pad_start[:, None, None], 0), axis=0)
            dest = slot0.astype(jnp.int32) + rank_t

            xs = _sc_scatter_rows(h2p, dest, n_blocks * MOE_BLOCK, g_disp)
            ys = _experts(blk_e, blk_v, blk_f, blk_n, blk_s, blk_r, xs, w1[l], b1[l], w2[l], b2[l])
            ysum = _sc_gather_gated_sum(ys, dest, gat_t, g_comb)
            x_next = _combine(x1, mod3, ysum, x_next, b0, Bg, B, S, tm_comb)
        x2 = x_next
    return x2.reshape(B, S, D)
```

```python
import functools

import jax
import jax.numpy as jnp
import numpy as np
from jax import lax
from jax.experimental import pallas as pl
from jax.experimental.pallas import tpu as pltpu
from jax.experimental.pallas import tpu_sc as plsc

D_MODEL = 1024
D_LRU = 512
LRU_BLOCKS = 8
LRU_BD = 64
CONV_W = 4
LRU_C = 8.0
N_HEADS = 8
QK_NOPE = 64
QK_ROPE = 32
QK_HEAD = 96
V_HEAD = 64
Q_LORA = 256
KV_LORA = 128
ROPE_THETA = 10000.0
N_EXPERTS = 32
TOP_K = 4
D_FF = 1024
SWIGLU_LIMIT = 7.0
SWIGLU_ALPHA = 1.702
MOE_BLOCK = 1024
EPS = 1e-6

LANES = 128
SUBLANES = 8
HEAD_PAD = 128
ROPE_LO = QK_NOPE
ROPE_HALF = QK_ROPE // 2
TOK_PER_ROW = LANES // ROPE_HALF
D_IN_PAD = 2 * D_LRU + Q_LORA + KV_LORA + 2 * LANES
LOG2_E = 1.4426950408889634
ROW_QUARTERS = 4

VMEM_LIMIT = 56 * 1024 * 1024

F32 = jnp.float32
BF16 = jnp.bfloat16


def _cparams(sem):
    return pltpu.CompilerParams(dimension_semantics=sem, vmem_limit_bytes=VMEM_LIMIT)


def _dot(a, b):
    return jnp.dot(a, b, preferred_element_type=F32)


def _dot_nt(a, b):
    return lax.dot_general(a, b, (((1,), (1,)), ((), ())), preferred_element_type=F32)


def _split_bf16(a):
    hi = a.astype(BF16)
    lo = (a - hi.astype(F32)).astype(BF16)
    return hi, lo


def _sigmoid(x, scale=1.0):
    return 1.0 / (1.0 + jnp.exp2(x * (-scale * LOG2_E)))


def _pack_halves(x):
    bits = lax.bitcast_convert_type(x.astype(BF16).astype(F32), jnp.uint32)
    half = x.shape[1] // 2
    words = (bits[:, :half] >> 16) | (bits[:, half:] & jnp.uint32(0xFFFF0000))
    return lax.bitcast_convert_type(words, jnp.int32)


def _unpack_halves(words):
    w = lax.bitcast_convert_type(words, jnp.uint32)
    lo = lax.bitcast_convert_type(w << 16, F32)
    hi = lax.bitcast_convert_type(w & jnp.uint32(0xFFFF0000), F32)
    return lo, hi


def _ada_kernel(c_ref, w_ref, b_ref, o_ref):
    c = c_ref[...]
    s = c * _sigmoid(c)
    shi, slo = _split_bf16(s)
    whi, wlo = _split_bf16(w_ref[...])
    o_ref[...] = _dot(shi, whi) + _dot(slo, whi) + _dot(shi, wlo) + b_ref[...]


def _ada(c, w_ada, b_ada):
    B, D = c.shape
    N = w_ada.shape[1]
    tn = 1024
    return pl.pallas_call(
        _ada_kernel,
        grid=(N // tn,),
        in_specs=[
            pl.BlockSpec((B, D), lambda j: (0, 0)),
            pl.BlockSpec((D, tn), lambda j: (0, j)),
            pl.BlockSpec((1, tn), lambda j: (0, j)),
        ],
        out_specs=pl.BlockSpec((B, tn), lambda j: (0, j)),
        out_shape=jax.ShapeDtypeStruct((B, N), F32),
        compiler_params=_cparams(("arbitrary",)),
        name="ada",
    )(c, w_ada, b_ada.reshape(1, N))


def _trig_kernel(pos_ref, freq_ref, rsel_ref, fold_ref, cbase_ref, cos_ref, sin_ref):
    ang = pos_ref[...].astype(F32) * freq_ref[...]
    cs = jnp.concatenate([jnp.cos(ang), jnp.sin(ang)], axis=1)
    tm = cos_ref.shape[0]
    row = lax.broadcasted_iota(jnp.int32, (tm, 2 * LANES), 0)
    lane = lax.broadcasted_iota(jnp.int32, (tm, 2 * LANES), 1)
    own = ((lane % LANES) // ROPE_HALF) == (row % TOK_PER_ROW)
    rsel = rsel_ref[...]
    fold = fold_ref[...]
    by_row = sum(_dot(rsel, part) for part in _split_bf16(cs))
    mine = jnp.where(own, by_row, 0.0)
    out = sum(_dot(part, fold) for part in _split_bf16(mine))
    cos_ref[...] = out[:, :LANES] + cbase_ref[...]
    sin_ref[...] = out[:, LANES:]


def _rope_tables(positions):
    T = positions.size
    rows = T // TOK_PER_ROW
    pos_c = jnp.repeat(positions.reshape(T).astype(jnp.int32), ROPE_HALF).reshape(rows, LANES)
    tm = min(2048, T)
    tr = tm // TOK_PER_ROW
    freqs = np.float32(ROPE_THETA) ** (-np.arange(ROPE_HALF, dtype=np.float32) / np.float32(ROPE_HALF))
    freq_c = np.tile(freqs.astype(np.float32), TOK_PER_ROW).reshape(1, LANES)
    rsel = jnp.asarray(np.arange(tm)[:, None] // TOK_PER_ROW == np.arange(tr)[None, :], BF16)
    src = np.arange(LANES)[:, None] % ROPE_HALF
    dst = np.arange(LANES)[None, :]
    first = dst == ROPE_LO + src
    second = dst == ROPE_LO + ROPE_HALF + src
    fcos = (first | second).astype(np.float32)
    fsin = second.astype(np.float32) - first.astype(np.float32)
    zero = np.zeros((LANES, LANES), np.float32)
    fold = jnp.asarray(np.block([[fcos, zero], [zero, fsin]]), BF16)
    lane = np.arange(LANES)
    cbase = ((lane < ROPE_LO) | (lane >= ROPE_LO + QK_ROPE)).astype(np.float32).reshape(1, LANES)
    full = lambda i: (0, 0)
    return pl.pallas_call(
        _trig_kernel,
        grid=(T // tm,),
        in_specs=[
            pl.BlockSpec((tr, LANES), lambda i: (i, 0)),
            pl.BlockSpec((1, LANES), full),
            pl.BlockSpec((tm, tr), full),
            pl.BlockSpec((2 * LANES, 2 * LANES), full),
            pl.BlockSpec((1, LANES), full),
        ],
        out_specs=[pl.BlockSpec((tm, LANES), lambda i: (i, 0))] * 2,
        out_shape=[jax.ShapeDtypeStruct((T, LANES), F32)] * 2,
        compiler_params=_cparams(("arbitrary",)),
        name="rope_trig",
    )(pos_c, freq_c, rsel, fold, cbase)


def _inproj_kernel(x_ref, cos_ref, sin_ref, shift_ref, scale_ref, gmix_ref, win_ref, gq_ref, wuq_ref,
                   gkv_ref, wukv_ref, gqn_ref, gqr_ref, gkn_ref, gkr_ref,
                   cw_ref, cb_ref, wa_ref, ba_ref, wx_ref, bx_ref, lam_ref,
                   lru_ref, q_ref, k_ref, v_ref, tail_ref, carry_ref):
    HP = N_HEADS * HEAD_PAD

    @pl.when(pl.program_id(1) == 0)
    def _():
        tail_ref[...] = jnp.zeros_like(tail_ref)
        carry_ref[...] = jnp.zeros_like(carry_ref)

    x = x_ref[...]
    ms = jnp.mean(x * x, axis=-1, keepdims=True)
    gain = gmix_ref[...] * (1.0 + scale_ref[0])
    h = x * lax.rsqrt(ms + EPS) * gain + shift_ref[0]
    hb = h.astype(BF16)
    o1 = 2 * D_LRU
    o2 = Q_LORA
    o3 = o2 + KV_LORA
    z_lru = _dot(hb, win_ref[:, :o1])
    z = _dot(hb, win_ref[:, o1:])
    lru_ref[...] = _lru_tile(z_lru[:, :D_LRU], z_lru[:, D_LRU:], cw_ref, cb_ref, wa_ref, ba_ref,
                             wx_ref, bx_ref, lam_ref, tail_ref, carry_ref)
    ql = z[:, :o2]
    kvl = z[:, o2:o3]
    kr = z[:, o3:o3 + LANES]
    kr_rot = z[:, o3 + LANES:]

    qn = ql * lax.rsqrt(jnp.mean(ql * ql, axis=-1, keepdims=True) + EPS) * gq_ref[...]
    qq = _dot(qn.astype(BF16), wuq_ref[...])
    kvn = kvl * lax.rsqrt(jnp.mean(kvl * kvl, axis=-1, keepdims=True) + EPS) * gkv_ref[...]
    kv = _dot(kvn.astype(BF16), wukv_ref[...])

    tm = x.shape[0]
    lane = lax.broadcasted_iota(jnp.int32, (tm, HP), 1)
    pair_lane = lane & (2 * HEAD_PAD - 1)
    ones_cols = (pair_lane >= V_HEAD) & (pair_lane < 2 * HEAD_PAD - V_HEAD)
    v_ref[...] = jnp.where(ones_cols, 1.0, kv[:, HP:]).astype(BF16)

    cos_t = cos_ref[...]
    sin_t = sin_ref[...]
    gqn = gqn_ref[...]
    gkn = gkn_ref[...]
    cq = gqn * cos_t
    sq = gqr_ref[...] * sin_t
    kb = kr * (gkn * cos_t) + kr_rot * (gkr_ref[...] * sin_t)
    inv_w = 1.0 / QK_HEAD
    qscale = QK_HEAD ** -0.5 * LOG2_E
    for hh in range(N_HEADS):
        sl = slice(hh * HEAD_PAD, (hh + 1) * HEAD_PAD)
        qh = qq[:, sl]
        rq = lax.rsqrt(jnp.sum(qh * qh, axis=-1, keepdims=True) * inv_w + EPS) * qscale
        q_ref[:, sl] = ((qh * cq + qq[:, HP + hh * HEAD_PAD:HP + (hh + 1) * HEAD_PAD] * sq) * rq).astype(BF16)
        kraw = kv[:, sl] + kr
        rk = lax.rsqrt(jnp.sum(kraw * kraw, axis=-1, keepdims=True) * inv_w + EPS)
        k_ref[:, sl] = ((kv[:, sl] * gkn + kb) * rk).astype(BF16)


def _inproj(x2, cos_t, sin_t, mod3, g_mix, w_in_p, g_q_lat, w_uq_p, g_kv_lat, w_ukv_p,
            gqn_p, gqr_p, gkn_p, gkr_p, conv_w, conv_b, wa_d, b_a, wx_d, b_x, lam, B, S, tm):
    T, D = x2.shape
    ns = S // tm
    HP = N_HEADS * HEAD_PAD
    C = D_LRU
    row = lambda b, s: (b * ns + s, 0)
    full = lambda b, s: (0, 0)
    return pl.pallas_call(
        _inproj_kernel,
        grid=(B, ns),
        in_specs=[
            pl.BlockSpec((tm, D), row),
            pl.BlockSpec((tm, LANES), row),
            pl.BlockSpec((tm, LANES), row),
            pl.BlockSpec((1, 1, D), lambda b, s: (b * 6 + 0, 0, 0)),
            pl.BlockSpec((1, 1, D), lambda b, s: (b * 6 + 1, 0, 0)),
            pl.BlockSpec((1, D), full),
            pl.BlockSpec((D, D_IN_PAD), full),
            pl.BlockSpec((1, Q_LORA), full),
            pl.BlockSpec((Q_LORA, 2 * HP), full),
            pl.BlockSpec((1, KV_LORA), full),
            pl.BlockSpec((KV_LORA, 2 * HP), full),
            pl.BlockSpec((1, HEAD_PAD), full),
            pl.BlockSpec((1, HEAD_PAD), full),
            pl.BlockSpec((1, HEAD_PAD), full),
            pl.BlockSpec((1, HEAD_PAD), full),
            pl.BlockSpec((CONV_W, C), full),
            pl.BlockSpec((1, C), full),
            pl.BlockSpec((C, C), full),
            pl.BlockSpec((1, C), full),
            pl.BlockSpec((C, C), full),
            pl.BlockSpec((1, C), full),
            pl.BlockSpec((1, C), full),
        ],
        out_specs=[
            pl.BlockSpec((tm, C), row),
            pl.BlockSpec((tm, HP), row),
            pl.BlockSpec((tm, HP), row),
            pl.BlockSpec((tm, HP), row),
        ],
        out_shape=[
            jax.ShapeDtypeStruct((T, C), BF16),
            jax.ShapeDtypeStruct((T, HP), BF16),
            jax.ShapeDtypeStruct((T, HP), BF16),
            jax.ShapeDtypeStruct((T, HP), BF16),
        ],
        scratch_shapes=[pltpu.VMEM((SUBLANES, C), F32), pltpu.VMEM((SUBLANES, C), F32)],
        compiler_params=_cparams(("arbitrary", "arbitrary")),
        name="inproj",
    )(x2, cos_t, sin_t, mod3, mod3, g_mix, w_in_p, g_q_lat, w_uq_p, g_kv_lat, w_ukv_p,
      gqn_p, gqr_p, gkn_p, gkr_p, conv_w, conv_b, wa_d, b_a, wx_d, b_x, lam)


def _gelu_tanh(x):
    c = 0.7978845608028654
    hx = 0.5 * x
    return hx + hx * jnp.tanh(x * (c + (c * 0.044715) * (x * x)))


def _lru_tile(x, y, cw_ref, cb_ref, wa_ref, ba_ref, wx_ref, bx_ref, lam_ref, tail_ref, carry_ref):
    ts = x.shape[0]
    xext = jnp.concatenate([tail_ref[...], x], axis=0)
    cw = cw_ref[...]
    xc = x * cw[CONV_W - 1:CONV_W, :]
    for j in range(CONV_W - 1):
        sh = CONV_W - 1 - j
        xc = xc + xext[8 - sh:8 - sh + ts, :] * cw[j:j + 1, :]
    xc = xc + cb_ref[...]
    tail_ref[...] = x[ts - 8:, :]

    xb = xc.astype(BF16)
    r = _sigmoid(_dot(xb, wa_ref[...]) + ba_ref[...])
    i = _sigmoid(_dot(xb, wx_ref[...]) + bx_ref[...])
    lam = lam_ref[...]
    nl = -lam
    softplus = jnp.maximum(nl, 0.0) + jnp.log(1.0 + jnp.exp(-jnp.abs(nl)))
    log_a = (-LRU_C) * r * softplus
    a = jnp.exp(log_a)
    mult = jnp.sqrt(1.0 - a * a)
    u = mult * (i * xc)

    C = a.shape[1]
    a = a.reshape(ts // SUBLANES, SUBLANES, C)
    u = u.reshape(ts // SUBLANES, SUBLANES, C)
    sub = lax.broadcasted_iota(jnp.int32, (1, SUBLANES, 1), 1)
    sh = 1
    while sh < SUBLANES:
        a_prev = pltpu.roll(a, sh, axis=1)
        u_prev = pltpu.roll(u, sh, axis=1)
        m = sub >= sh
        u = jnp.where(m, a * u_prev + u, u)
        a = jnp.where(m, a * a_prev, a)
        sh *= 2
    a = a.reshape(ts, C)
    u = u.reshape(ts, C)
    h = carry_ref[0:1, :]
    groups = []
    for g0 in range(0, ts, SUBLANES):
        hg = u[g0:g0 + SUBLANES, :] + a[g0:g0 + SUBLANES, :] * h
        groups.append(hg)
        h = hg[SUBLANES - 1:SUBLANES, :]
    carry_ref[...] = jnp.broadcast_to(h, carry_ref.shape)
    hs = jnp.concatenate(groups, axis=0)
    return (_gelu_tanh(y) * hs).astype(BF16)


NEG_INF = -1e30


def _attn_kernel(q_ref, k_ref, v_ref, o_ref, *state, tq, tk):
    m_refs = state[:N_HEADS]
    acc_refs = state[N_HEADS:]
    qi = pl.program_id(1)
    causal = (lax.broadcasted_iota(jnp.int32, (tk, tk), 1)
              <= lax.broadcasted_iota(jnp.int32, (tk, tk), 0))
    lower = slice(tk, tq)

    def head_slice(hh):
        return slice(hh * HEAD_PAD, (hh + 1) * HEAD_PAD)

    def weights(sc, m_b):
        cols = [jnp.exp2(sc[:, c0:c0 + LANES] - m_b) for c0 in range(0, tk, LANES)]
        return jnp.concatenate(cols, axis=1).astype(BF16)

    def scores(hh, r0, rows):
        hs = head_slice(hh)
        return _dot_nt(q_ref[rows, hs], k_ref[pl.ds(r0, tk), hs])

    def update(hh, rows, sc, r0):
        hs = head_slice(hh)
        m_b = m_refs[hh][rows, :]
        m_new = jnp.maximum(m_b, jnp.max(sc, axis=-1, keepdims=True))
        alpha = jnp.exp2(m_b - m_new)
        m_refs[hh][rows, :] = m_new
        acc_refs[hh][rows, :] = (alpha * acc_refs[hh][rows, :]
                                 + _dot(weights(sc, m_new), v_ref[pl.ds(r0, tk), hs]))


    every = slice(0, tq)
    r_d0 = pl.multiple_of(qi * tq, tk)
    sc_next = scores(0, r_d0, every)
    for hh in range(N_HEADS):
        hs = head_slice(hh)
        sc = jnp.concatenate([jnp.where(causal, sc_next[:tk], NEG_INF), sc_next[tk:]], axis=0)
        if hh + 1 < N_HEADS:
            sc_next = scores(hh + 1, r_d0, every)
        m_b = jnp.broadcast_to(jnp.max(sc, axis=-1, keepdims=True), (tq, LANES))
        m_refs[hh][...] = m_b
        acc_refs[hh][...] = _dot(weights(sc, m_b), v_ref[pl.ds(r_d0, tk), hs])

    r_d1 = pl.multiple_of(qi * tq + tk, tk)
    sc_next = scores(0, r_d1, lower)
    for hh in range(N_HEADS):
        sc = jnp.where(causal, sc_next, NEG_INF)
        if hh + 1 < N_HEADS:
            sc_next = scores(hh + 1, r_d1, lower)
        update(hh, lower, sc, r_d1)

    @pl.loop(0, qi * (tq // tk))
    def _(j):
        r0 = pl.multiple_of(j * tk, tk)
        sc_next = scores(0, r0, every)
        for hh in range(N_HEADS):
            sc = sc_next
            if hh + 1 < N_HEADS:
                sc_next = scores(hh + 1, r0, every)
            update(hh, every, sc, r0)

    low = lax.broadcasted_iota(jnp.int32, (tq, HEAD_PAD), 1) < V_HEAD
    for he in range(0, N_HEADS, 2):
        acc_e = acc_refs[he][...]
        acc_o = acc_refs[he + 1][...]
        num = jnp.where(low, acc_e, acc_o)
        den = pltpu.roll(jnp.where(low, acc_o, acc_e), V_HEAD, axis=1)
        o_ref[:, he * V_HEAD:(he + 2) * V_HEAD] = (num / den).astype(BF16)


def _attn(qp, kp, v, B, S, tk):
    T = qp.shape[0]
    tq = 2 * tk
    nq = S // tq
    HP = N_HEADS * HEAD_PAD
    HV = N_HEADS * V_HEAD
    return pl.pallas_call(
        functools.partial(_attn_kernel, tq=tq, tk=tk),
        grid=(B, nq),
        in_specs=[
            pl.BlockSpec((tq, HP), lambda b, i: (b * nq + i, 0)),
            pl.BlockSpec((S, HP), lambda b, i: (b, 0)),
            pl.BlockSpec((S, HP), lambda b, i: (b, 0)),
        ],
        out_specs=pl.BlockSpec((tq, HV), lambda b, i: (b * nq + i, 0)),
        out_shape=jax.ShapeDtypeStruct((T, HV), BF16),
        scratch_shapes=([pltpu.VMEM((tq, LANES), F32)] * N_HEADS
                        + [pltpu.VMEM((tq, HEAD_PAD), F32)] * N_HEADS),
        compiler_params=_cparams(("arbitrary", "arbitrary")),
        name="attn",
    )(qp, kp, v)


def _outproj_kernel(lru_ref, att_ref, x_ref, gate_ref, shift_ref, scale_ref, gffn_ref,
                    wo1_ref, wo2_ref, wr_ref, br_ref, tri_ref,
                    x1_ref, h2p_ref, idx_ref, gat_ref, rank_ref, cnt_ref, run_ref):
    first = (pl.program_id(0) == 0) & (pl.program_id(1) == 0)

    @pl.when(first)
    def _():
        run_ref[...] = jnp.zeros_like(run_ref)

    mix = _dot(lru_ref[...], wo1_ref[...]) + _dot(att_ref[...], wo2_ref[...])
    x1 = x_ref[...] + gate_ref[0] * mix
    x1_ref[...] = x1
    ms = jnp.mean(x1 * x1, axis=-1, keepdims=True)
    gain = gffn_ref[...] * (1.0 + scale_ref[0])
    h2 = x1 * lax.rsqrt(ms + EPS) * gain + shift_ref[0]

    hhi = h2.astype(BF16)
    hlo = (h2 - hhi.astype(F32)).astype(BF16)
    h2p_ref[...] = _pack_halves(h2)

    ne = br_ref.shape[0]
    stacked = _dot_nt(wr_ref[...], hhi)
    logits = stacked[:ne] + stacked[ne:] + _dot_nt(wr_ref[:ne, :], hlo) + br_ref[...]

    tm = logits.shape[1]
    eio = lax.broadcasted_iota(jnp.int32, (ne, tm), 0)
    vals, idxs, sels = [], [], []
    l = logits
    for _ in range(TOP_K):
        m = jnp.max(l, axis=0, keepdims=True)
        idx = jnp.min(jnp.where(l == m, eio, ne), axis=0, keepdims=True)
        sel = eio == idx
        l = jnp.where(sel, -jnp.inf, l)
        vals.append(m)
        idxs.append(idx)
        sels.append(sel)
    es = [jnp.exp(v - vals[0]) for v in vals]
    inv = 1.0 / (es[0] + es[1] + es[2] + es[3])
    sel_any = jnp.where(sels[0] | sels[1] | sels[2] | sels[3], 1.0, 0.0)
    run = run_ref[...]
    excl = _dot(sel_any.astype(BF16), tri_ref[...]) + run
    for kk in range(TOP_K):
        idx_ref[kk:kk + 1, :] = idxs[kk]
        gat_ref[kk:kk + 1, :] = es[kk] * inv
        rk = jnp.sum(jnp.where(sels[kk], excl, 0.0), axis=0, keepdims=True)
        rank_ref[kk:kk + 1, :] = rk.astype(jnp.int32)
    run = run + jnp.sum(sel_any, axis=1, keepdims=True)
    run_ref[...] = run
    cnt_ref[...] = run.astype(jnp.int32)


def _outproj(lru_o, att_o, x2, mod3, g_ffn, wo1, wo2, wr_stack, b_r, tri, b0, B, S, tm):
    D = x2.shape[1]
    T = B * S
    ns = S // tm
    C = lru_o.shape[1]
    row_in = lambda b, s: ((b0 + b) * ns + s, 0)
    row = lambda b, s: (b * ns + s, 0)
    col = lambda b, s: (0, b * ns + s)
    full = lambda b, s: (0, 0)
    return pl.pallas_call(
        _outproj_kernel,
        grid=(B, ns),
        in_specs=[
            pl.BlockSpec((tm, C), row_in),
            pl.BlockSpec((tm, C), row_in),
            pl.BlockSpec((tm, D), row_in),
            pl.BlockSpec((1, 1, D), lambda b, s: ((b0 + b) * 6 + 2, 0, 0)),
            pl.BlockSpec((1, 1, D), lambda b, s: ((b0 + b) * 6 + 3, 0, 0)),
            pl.BlockSpec((1, 1, D), lambda b, s: ((b0 + b) * 6 + 4, 0, 0)),
            pl.BlockSpec((1, D), full),
            pl.BlockSpec((C, D), full),
            pl.BlockSpec((C, D), full),
            pl.BlockSpec((2 * N_EXPERTS, D), full),
            pl.BlockSpec((N_EXPERTS, 1), full),
            pl.BlockSpec(tri.shape, full),
        ],
        out_specs=[
            pl.BlockSpec((tm, D), row),
            pl.BlockSpec((tm, D // 2), row),
            pl.BlockSpec((TOP_K, tm), col),
            pl.BlockSpec((TOP_K, tm), col),
            pl.BlockSpec((TOP_K, tm), col),
            pl.BlockSpec((N_EXPERTS, 1), full),
        ],
        out_shape=[
            jax.ShapeDtypeStruct((T, D), F32),
            jax.ShapeDtypeStruct((T, D // 2), jnp.int32),
            jax.ShapeDtypeStruct((TOP_K, T), jnp.int32),
            jax.ShapeDtypeStruct((TOP_K, T), F32),
            jax.ShapeDtypeStruct((TOP_K, T), jnp.int32),
            jax.ShapeDtypeStruct((N_EXPERTS, 1), jnp.int32),
        ],
        scratch_shapes=[pltpu.VMEM((N_EXPERTS, 1), F32)],
        compiler_params=_cparams(("arbitrary", "arbitrary")),
        name="outproj",
    )(lru_o, att_o, x2, mod3, mod3, mod3, g_ffn, wo1, wo2, wr_stack, b_r, tri)


SC_CORES = 2
SC_SUBCORES = 16
SC_WORKERS = SC_CORES * SC_SUBCORES
SC_LANES = 16


def _sc_mesh():
    return plsc.VectorSubcoreMesh(core_axis_name="c", subcore_axis_name="s",
                                  num_cores=SC_CORES, num_subcores=SC_SUBCORES)


def _sc_worker_id():
    return lax.axis_index("s") * SC_CORES + lax.axis_index("c")


def _sc_scatter_rows(rows, idx, n_out, g):
    T, W = rows.shape
    K = idx.shape[0]
    per_w = T // SC_WORKERS
    nch = per_w // g
    assert per_w * SC_WORKERS == T and nch * g == per_w and nch % 2 == 0
    idx_w = idx.reshape(K, SC_WORKERS, nch, g).transpose(1, 2, 0, 3).reshape(SC_WORKERS, nch * K, g)

    def body(rows_hbm, idx_hbm, out_hbm, idx_v, buf0, buf1, semr0, semr1, semw):
        wid = _sc_worker_id()
        base = wid * per_w
        pltpu.sync_copy(idx_hbm.at[wid], idx_v)

        def read(j, buf, sem):
            return pltpu.make_async_copy(rows_hbm.at[pl.ds(base + j * g, g)], buf, sem)

        def scatter(j, buf):
            copies = [pltpu.async_copy(buf, out_hbm.at[idx_v.at[j * K + kk]], semw)
                      for kk in range(K)]
            for cp in copies:
                cp.wait()

        read(0, buf0, semr0).start()

        @pl.loop(0, nch // 2)
        def _(jj):
            j0 = 2 * jj
            read(j0 + 1, buf1, semr1).start()
            read(j0, buf0, semr0).wait()
            scatter(j0, buf0)

            @pl.when(j0 + 2 < nch)
            def _():
                read(j0 + 2, buf0, semr0).start()

            read(j0 + 1, buf1, semr1).wait()
            scatter(j0 + 1, buf1)

    return pl.kernel(
        body,
        out_type=jax.ShapeDtypeStruct((n_out, W), rows.dtype),
        mesh=_sc_mesh(),
        scratch_types=[
            pltpu.VMEM((nch * K, g), jnp.int32),
            pltpu.VMEM((g, W), rows.dtype),
            pltpu.VMEM((g, W), rows.dtype),
            pltpu.SemaphoreType.DMA,
            pltpu.SemaphoreType.DMA,
            pltpu.SemaphoreType.DMA,
        ],
        name="sc_scatter_rows",
    )(rows, idx_w)


def _sc_gather_gated_sum(table, idx, gates, g):
    W = table.shape[1]
    K, T = idx.shape
    per_w = T // SC_WORKERS
    nch = per_w // g
    assert per_w * SC_WORKERS == T and nch * g == per_w and nch % 2 == 0 and W % SC_LANES == 0
    idx_w = idx.reshape(K, SC_WORKERS, nch, g).transpose(1, 2, 0, 3).reshape(SC_WORKERS, nch * K, g)
    gates_w = gates.reshape(K, SC_WORKERS, nch, g).transpose(1, 2, 0, 3).reshape(SC_WORKERS, nch, K * g)
    def body(table_hbm, idx_hbm, gates_hbm, out_hbm, idx_v, rows_v, gts_v, out_v, sem0, sem1):
        wid = _sc_worker_id()
        base = wid * per_w
        pltpu.sync_copy(idx_hbm.at[wid], idx_v)

        def fetch(j, slot, sem):
            cps = []
            for kk in range(K):
                cps.append(pltpu.make_async_copy(table_hbm.at[idx_v.at[j * K + kk]],
                                                 rows_v.at[slot, kk], sem))
            cps.append(pltpu.make_async_copy(gates_hbm.at[wid, j], gts_v.at[slot], sem))
            return cps

        def start(j, slot, sem):
            for cp in fetch(j, slot, sem):
                cp.start()

        def finish(j, slot, sem):
            for cp in fetch(j, slot, sem):
                cp.wait()

            @pl.loop(0, g)
            def _(t):
                gk = [plsc.load_gather(gts_v.at[slot], [jnp.full((SC_LANES,), kk * g, jnp.int32) + t])
                      for kk in range(K)]

                @plsc.parallel_loop(0, W, SC_LANES, unroll=4)
                def _(off):
                    off = pl.multiple_of(off, SC_LANES)
                    acc_lo = jnp.zeros((SC_LANES,), F32)
                    acc_hi = jnp.zeros((SC_LANES,), F32)
                    for kk in range(K):
                        w = rows_v[slot, kk, t, pl.ds(off, SC_LANES)]
                        lo, hi = plsc.unpack(plsc.bitcast(w, BF16), format=plsc.PackFormat.INTERLEAVED)
                        acc_lo = acc_lo + gk[kk] * lo
                        acc_hi = acc_hi + gk[kk] * hi
                    out_v[slot, t, pl.ds(off, SC_LANES)] = acc_lo
                    out_v[slot, t, pl.ds(W + off, SC_LANES)] = acc_hi

            pltpu.sync_copy(out_v.at[slot], out_hbm.at[pl.ds(base + j * g, g)])

        start(0, 0, sem0)

        @pl.loop(0, nch // 2)
        def _(jj):
            j0 = 2 * jj
            start(j0 + 1, 1, sem1)
            finish(j0, 0, sem0)

            @pl.when(j0 + 2 < nch)
            def _():
                start(j0 + 2, 0, sem0)

            finish(j0 + 1, 1, sem1)

    return pl.kernel(
        body,
        out_type=jax.ShapeDtypeStruct((T, 2 * W), F32),
        mesh=_sc_mesh(),
        scratch_types=[
            pltpu.VMEM((nch * K, g), jnp.int32),
            pltpu.VMEM((2, K, g, W), jnp.int32),
            pltpu.VMEM((2, K * g), F32),
            pltpu.VMEM((2, g, 2 * W), F32),
            pltpu.SemaphoreType.DMA,
            pltpu.SemaphoreType.DMA,
        ],
        compiler_params=pltpu.CompilerParams(needs_layout_passes=False),
        name="sc_gather_gated_sum",
    )(table, idx_w, gates_w)


def _experts_kernel(be_ref, bv_ref, bf_ref, bn_ref, bs_ref, xs_ref, w1_hbm, b1_ref, w2_hbm, b2_ref,
                    ys_ref, w1s_ref, w2s_ref, w1b_ref, w2b_ref, sem):
    i = pl.program_id(0)
    nvalid = bv_ref[i]

    def weight_copies(e, slot):
        return (pltpu.make_async_copy(w1_hbm.at[e], w1s_ref.at[slot], sem.at[0, slot]),
                pltpu.make_async_copy(w2_hbm.at[e], w2s_ref.at[slot], sem.at[1, slot]))

    @pl.when(bf_ref[i] > 0)
    def _():
        slot = bs_ref[i]

        @pl.when(i == 0)
        def _():
            for cp in weight_copies(be_ref[0], slot):
                cp.start()

        for cp in weight_copies(be_ref[i], slot):
            cp.wait()
        w1b_ref[...] = w1s_ref[slot].astype(BF16)
        w2b_ref[...] = w2s_ref[slot].astype(BF16)

        @pl.when(bn_ref[i] >= 0)
        def _():
            for cp in weight_copies(bn_ref[i], 1 - slot):
                cp.start()

    def ffn(rows):
        xw = xs_ref[:rows, :]
        rowi = lax.broadcasted_iota(jnp.int32, (rows, 1), 0)
        lo, hi = _unpack_halves(jnp.where(rowi < nvalid, xw, 0))
        xb = jnp.concatenate([lo.astype(BF16), hi.astype(BF16)], axis=1)
        gu = _dot(xb, w1b_ref[...]) + b1_ref[0]
        glu = jnp.minimum(gu[:, :D_FF], SWIGLU_LIMIT)
        lin = jnp.clip(gu[:, D_FF:], -SWIGLU_LIMIT, SWIGLU_LIMIT)
        act = (lin + 1.0) * (glu * _sigmoid(glu, SWIGLU_ALPHA))
        ys_ref[:rows, :] = _pack_halves(_dot(act.astype(BF16), w2b_ref[...]) + b2_ref[0])

    step = xs_ref.shape[0] // ROW_QUARTERS
    for nq in range(1, ROW_QUARTERS + 1):
        pl.when((nvalid > (nq - 1) * step) & (nvalid <= nq * step))(functools.partial(ffn, nq * step))


def _experts(blk_e, blk_v, blk_f, blk_n, blk_s, blk_r, xs, w1, b1, w2, b2):
    P, W = xs.shape
    nb = P // MOE_BLOCK
    E, D, F2 = w1.shape
    grid_spec = pltpu.PrefetchScalarGridSpec(
        num_scalar_prefetch=6,
        grid=(nb,),
        in_specs=[
            pl.BlockSpec((MOE_BLOCK, W), lambda i, be, bv, bf, bn, bs, br: (br[i], 0)),
            pl.BlockSpec(memory_space=pl.ANY),
            pl.BlockSpec((1, 1, F2), lambda i, be, bv, bf, bn, bs, br: (be[i], 0, 0)),
            pl.BlockSpec(memory_space=pl.ANY),
            pl.BlockSpec((1, 1, D), lambda i, be, bv, bf, bn, bs, br: (be[i], 0, 0)),
        ],
        out_specs=pl.BlockSpec((MOE_BLOCK, D // 2), lambda i, be, bv, bf, bn, bs, br: (br[i], 0)),
        scratch_shapes=[
            pltpu.VMEM((2, D, F2), F32),
            pltpu.VMEM((2, D_FF, D), F32),
            pltpu.VMEM((D, F2), BF16),
            pltpu.VMEM((D_FF, D), BF16),
            pltpu.SemaphoreType.DMA((2, 2)),
        ],
    )

    def kern(be_ref, bv_ref, bf_ref, bn_ref, bs_ref, br_ref, *refs):
        del br_ref
        _experts_kernel(be_ref, bv_ref, bf_ref, bn_ref, bs_ref, *refs)

    return pl.pallas_call(
        kern,
        grid_spec=grid_spec,
        out_shape=jax.ShapeDtypeStruct((P, D // 2), jnp.int32),
        compiler_params=_cparams(("arbitrary",)),
        name="experts",
    )(blk_e, blk_v, blk_f, blk_n, blk_s, blk_r, xs, w1, b1.reshape(E, 1, F2), w2,
      b2.reshape(E, 1, D))


def _combine_kernel(x1_ref, gate_ref, ysum_ref, *rest):
    o_ref = rest[-1]
    o_ref[...] = x1_ref[...] + gate_ref[0] * ysum_ref[...]


def _combine(x1, mod3, ysum, out_prev, b0, B, nb_total, S, tm):
    D = x1.shape[1]
    ns = S // tm
    row = lambda b, s: (b * ns + s, 0)
    row_out = lambda b, s: ((b0 + b) * ns + s, 0)
    in_specs = [
        pl.BlockSpec((tm, D), row),
        pl.BlockSpec((1, 1, D), lambda b, s: ((b0 + b) * 6 + 5, 0, 0)),
        pl.BlockSpec((tm, D), row),
    ]
    args = [x1, mod3, ysum]
    aliases = {}
    if out_prev is not None:
        in_specs.append(pl.BlockSpec(memory_space=pl.ANY))
        args.append(out_prev)
        aliases = {len(args) - 1: 0}
    return pl.pallas_call(
        _combine_kernel,
        grid=(B, ns),
        in_specs=in_specs,
        out_specs=pl.BlockSpec((tm, D), row_out),
        out_shape=jax.ShapeDtypeStruct((nb_total * S, D), F32),
        input_output_aliases=aliases,
        compiler_params=_cparams(("arbitrary", "arbitrary")),
        name="combine",
    )(*args)


def _block_diag(w):
    n, c, d = w.shape
    eye = jnp.eye(n, dtype=w.dtype)
    return jnp.einsum("ncd,nm->ncmd", w, eye).reshape(n * c, n * d)


def _pad_heads(w, width):
    k = w.shape[0]
    w = w.reshape(k, N_HEADS, width)
    return jnp.pad(w, ((0, 0), (0, 0), (0, HEAD_PAD - width))).reshape(k, N_HEADS * HEAD_PAD)


def kernel(x, c, positions, w_ada, b_ada, g_mix, w_in, conv_w, conv_b, w_a, b_a, w_x, b_x, lam,
           g_q_lat, w_uq, g_kv_lat, w_ukv, g_qn, g_kn, w_out, g_ffn, w_router, b_router,
           w1, b1, w2, b2):
    B, S, D = x.shape
    T = B * S
    depth = w_ada.shape[0]
    tm_in = min(512, S)
    tk_att = min(512, S // 2)
    tm_out = min(512, S)
    tm_comb = min(1024, S)
    n_groups = 2 if B % 2 == 0 else 1
    Bg = B // n_groups
    Tg = Bg * S
    g_disp = min(64, Tg // SC_WORKERS // 2)
    g_comb = min(16, Tg // SC_WORKERS // 2)

    o1 = 2 * D_LRU
    o2 = o1 + Q_LORA
    o3 = o2 + KV_LORA
    tri = jnp.asarray(np.arange(tm_out)[:, None] < np.arange(tm_out)[None, :], BF16)
    cos_t, sin_t = _rope_tables(positions)

    def rot_cols(w):
        k = w.shape[0]
        w3 = w.reshape(k, -1, HEAD_PAD)
        lo = w3[:, :, ROPE_LO:ROPE_LO + ROPE_HALF]
        hi = w3[:, :, ROPE_LO + ROPE_HALF:ROPE_LO + QK_ROPE]
        zl = jnp.zeros_like(w3[:, :, :ROPE_LO])
        zr = jnp.zeros_like(w3[:, :, ROPE_LO + QK_ROPE:])
        return jnp.concatenate([zl, hi, lo, zr], axis=2).reshape(w.shape)

    x2 = x.reshape(T, D)
    for l in range(depth):
        mod3 = _ada(c, w_ada[l], b_ada[l]).reshape(B * 6, 1, D)

        w_in_l = w_in[l]
        kr_cols = jnp.pad(w_in_l[:, o3:], ((0, 0), (ROPE_LO, LANES - ROPE_LO - QK_ROPE)))
        w_in_p = jnp.concatenate([w_in_l[:, :o3], kr_cols, rot_cols(kr_cols)], axis=1).astype(BF16)
        w_uq_h = _pad_heads(w_uq[l], QK_HEAD)
        w_uq_p = jnp.concatenate([w_uq_h, rot_cols(w_uq_h)], axis=1).astype(BF16)
        w_ukv_l = w_ukv[l].reshape(KV_LORA, N_HEADS, QK_NOPE + V_HEAD)
        w_uk_h = _pad_heads(w_ukv_l[:, :, :QK_NOPE].reshape(KV_LORA, N_HEADS * QK_NOPE), QK_NOPE)
        w_uv_pairs = w_ukv_l[:, :, QK_NOPE:].reshape(KV_LORA, N_HEADS // 2, 2, V_HEAD)
        zero_v = jnp.zeros_like(w_uv_pairs[:, :, 0])
        w_uv_h = jnp.stack([w_uv_pairs[:, :, 0], zero_v, zero_v, w_uv_pairs[:, :, 1]],
                           axis=2).reshape(KV_LORA, N_HEADS * HEAD_PAD)
        w_ukv_p = jnp.concatenate([w_uk_h, w_uv_h], axis=1).astype(BF16)
        gqn_p = jnp.pad(g_qn[l], (0, HEAD_PAD - QK_HEAD)).reshape(1, HEAD_PAD)
        gkn_p = jnp.pad(g_kn[l], (0, HEAD_PAD - QK_HEAD)).reshape(1, HEAD_PAD)

        lru_o, qp, kp, v = _inproj(
            x2, cos_t, sin_t, mod3, g_mix[l].reshape(1, D), w_in_p, g_q_lat[l].reshape(1, Q_LORA),
            w_uq_p, g_kv_lat[l].reshape(1, KV_LORA), w_ukv_p, gqn_p, rot_cols(gqn_p), gkn_p,
            rot_cols(gkn_p), conv_w[l], conv_b[l].reshape(1, D_LRU),
            _block_diag(w_a[l]).astype(BF16), b_a[l].reshape(1, D_LRU),
            _block_diag(w_x[l]).astype(BF16), b_x[l].reshape(1, D_LRU),
            lam[l].reshape(1, D_LRU), B, S, tm_in)

        att_o = _attn(qp, kp, v, B, S, tk_att)

        w_out_b = w_out[l].astype(BF16)
        wr_stack = jnp.concatenate(_split_bf16(w_router[l].T), axis=0)
        g_ffn_l = g_ffn[l].reshape(1, D)
        b_r = b_router[l].reshape(N_EXPERTS, 1)
        eio = np.arange(N_EXPERTS, dtype=np.int32)
        n_blocks = -(-(Tg * TOP_K) // MOE_BLOCK) + N_EXPERTS
        bi = np.arange(n_blocks, dtype=np.int32)

        x_next = None
        for gi in range(n_groups):
            b0 = gi * Bg
            x1, h2p, idx_t, gat_t, rank_t, counts = _outproj(
                lru_o, att_o, x2, mod3, g_ffn_l, w_out_b[:D_LRU], w_out_b[D_LRU:],
                wr_stack, b_r, tri, b0, Bg, S, tm_out)

            counts = counts.reshape(N_EXPERTS)
            nblk_e = (counts + MOE_BLOCK - 1) // MOE_BLOCK
            blk_end = jnp.cumsum(nblk_e)
            pad_start = (blk_end - nblk_e) * MOE_BLOCK
            total = blk_end[-1]
            blk_r = jnp.minimum(bi, total - 1).astype(jnp.int32)
            blk_e = jnp.minimum(jnp.sum(blk_end[None, :] <= blk_r[:, None], axis=1),
                                N_EXPERTS - 1).astype(jnp.int32)
            blk_onehot = blk_e[:, None] == eio[None, :]
            blk_first = jnp.sum(jnp.where(blk_onehot, (blk_end - nblk_e)[None, :], 0), axis=1)
            blk_cnt = jnp.sum(jnp.where(blk_onehot, counts[None, :], 0), axis=1)
            blk_v = jnp.where(bi < total,
                              jnp.clip(blk_cnt - (bi - blk_first) * MOE_BLOCK, 0, MOE_BLOCK),
                              0).astype(jnp.int32)
            blk_f = ((bi == blk_first) & (bi < total)).astype(jnp.int32)
            nxt_first = jnp.sum(jnp.where(blk_onehot, blk_end[None, :], 0), axis=1)
            nxt_e = jnp.minimum(jnp.sum(blk_end[None, :] <= nxt_first[:, None], axis=1),
                                N_EXPERTS - 1)
            blk_n = jnp.where(nxt_first < total, nxt_e, -1).astype(jnp.int32)
            ordinal = jnp.cumsum((nblk_e > 0).astype(jnp.int32)) - 1
            blk_s = (jnp.sum(jnp.where(blk_onehot, ordinal[None, :], 0), axis=1) % 2).astype(jnp.int32)
            slot0 = jnp.sum(jnp.where(idx_t[None] == eio[:, None, None],
                                      pad_start[:, None, None], 0), axis=0)
            dest = slot0.astype(jnp.int32) + rank_t

            xs = _sc_scatter_rows(h2p, dest, n_blocks * MOE_BLOCK, g_disp)
            ys = _experts(blk_e, blk_v, blk_f, blk_n, blk_s, blk_r, xs, w1[l], b1[l], w2[l], b2[l])
            ysum = _sc_gather_gated_sum(ys, dest, gat_t, g_comb)
            x_next = _combine(x1, mod3, ysum, x_next, b0, Bg, B, S, tm_comb)
        x2 = x_next
    return x2.reshape(B, S, D)
```

```python
import functools

import jax
import jax.numpy as jnp
import numpy as np
from jax import lax
from jax.experimental import pallas as pl
from jax.experimental.pallas import tpu as pltpu
from jax.experimental.pallas import tpu_sc as plsc

D_MODEL = 1024
D_LRU = 512
LRU_BLOCKS = 8
LRU_BD = 64
CONV_W = 4
LRU_C = 8.0
N_HEADS = 8
QK_NOPE = 64
QK_ROPE = 32
QK_HEAD = 96
V_HEAD = 64
Q_LORA = 256
KV_LORA = 128
ROPE_THETA = 10000.0
N_EXPERTS = 32
TOP_K = 4
D_FF = 1024
SWIGLU_LIMIT = 7.0
SWIGLU_ALPHA = 1.702
MOE_BLOCK = 1024
EPS = 1e-6

LANES = 128
SUBLANES = 8
HEAD_PAD = 128
ROPE_LO = QK_NOPE
ROPE_HALF = QK_ROPE // 2
TOK_PER_ROW = LANES // ROPE_HALF
D_IN_PAD = 2 * D_LRU + Q_LORA + KV_LORA + 2 * LANES
LOG2_E = 1.4426950408889634
ROW_PARTS = 8

VMEM_LIMIT = 56 * 1024 * 1024

F32 = jnp.float32
BF16 = jnp.bfloat16


def _cparams(sem):
    return pltpu.CompilerParams(dimension_semantics=sem, vmem_limit_bytes=VMEM_LIMIT)


def _dot(a, b):
    return jnp.dot(a, b, preferred_element_type=F32)


def _dot_nt(a, b):
    return lax.dot_general(a, b, (((1,), (1,)), ((), ())), preferred_element_type=F32)


def _split_bf16(a):
    hi = a.astype(BF16)
    lo = (a - hi.astype(F32)).astype(BF16)
    return hi, lo


def _sigmoid(x, scale=1.0):
    return 1.0 / (1.0 + jnp.exp2(x * (-scale * LOG2_E)))


def _pack_halves(x):
    bits = lax.bitcast_convert_type(x.astype(BF16).astype(F32), jnp.uint32)
    half = x.shape[1] // 2
    words = (bits[:, :half] >> 16) | (bits[:, half:] & jnp.uint32(0xFFFF0000))
    return lax.bitcast_convert_type(words, jnp.int32)


def _unpack_halves(words):
    w = lax.bitcast_convert_type(words, jnp.uint32)
    lo = lax.bitcast_convert_type(w << 16, F32)
    hi = lax.bitcast_convert_type(w & jnp.uint32(0xFFFF0000), F32)
    return lo, hi


def _ada_kernel(c_ref, w_ref, b_ref, o_ref):
    c = c_ref[...]
    s = c * _sigmoid(c)
    shi, slo = _split_bf16(s)
    whi, wlo = _split_bf16(w_ref[...])
    o_ref[...] = _dot(shi, whi) + _dot(slo, whi) + _dot(shi, wlo) + b_ref[...]


def _ada(c, w_ada, b_ada):
    B, D = c.shape
    N = w_ada.shape[1]
    tn = 1024
    return pl.pallas_call(
        _ada_kernel,
        grid=(N // tn,),
        in_specs=[
            pl.BlockSpec((B, D), lambda j: (0, 0)),
            pl.BlockSpec((D, tn), lambda j: (0, j)),
            pl.BlockSpec((1, tn), lambda j: (0, j)),
        ],
        out_specs=pl.BlockSpec((B, tn), lambda j: (0, j)),
        out_shape=jax.ShapeDtypeStruct((B, N), F32),
        compiler_params=_cparams(("arbitrary",)),
        name="ada",
    )(c, w_ada, b_ada.reshape(1, N))


def _trig_kernel(pos_ref, freq_ref, rsel_ref, fold_ref, cbase_ref, cos_ref, sin_ref):
    ang = pos_ref[...].astype(F32) * freq_ref[...]
    cs = jnp.concatenate([jnp.cos(ang), jnp.sin(ang)], axis=1)
    tm = cos_ref.shape[0]
    row = lax.broadcasted_iota(jnp.int32, (tm, 2 * LANES), 0)
    lane = lax.broadcasted_iota(jnp.int32, (tm, 2 * LANES), 1)
    own = ((lane % LANES) // ROPE_HALF) == (row % TOK_PER_ROW)
    rsel = rsel_ref[...]
    fold = fold_ref[...]
    by_row = sum(_dot(rsel, part) for part in _split_bf16(cs))
    mine = jnp.where(own, by_row, 0.0)
    out = sum(_dot(part, fold) for part in _split_bf16(mine))
    cos_ref[...] = out[:, :LANES] + cbase_ref[...]
    sin_ref[...] = out[:, LANES:]


def _rope_tables(positions):
    T = positions.size
    rows = T // TOK_PER_ROW
    pos_c = jnp.repeat(positions.reshape(T).astype(jnp.int32), ROPE_HALF).reshape(rows, LANES)
    tm = min(2048, T)
    tr = tm // TOK_PER_ROW
    freqs = np.float32(ROPE_THETA) ** (-np.arange(ROPE_HALF, dtype=np.float32) / np.float32(ROPE_HALF))
    freq_c = np.tile(freqs.astype(np.float32), TOK_PER_ROW).reshape(1, LANES)
    rsel = jnp.asarray(np.arange(tm)[:, None] // TOK_PER_ROW == np.arange(tr)[None, :], BF16)
    src = np.arange(LANES)[:, None] % ROPE_HALF
    dst = np.arange(LANES)[None, :]
    first = dst == ROPE_LO + src
    second = dst == ROPE_LO + ROPE_HALF + src
    fcos = (first | second).astype(np.float32)
    fsin = second.astype(np.float32) - first.astype(np.float32)
    zero = np.zeros((LANES, LANES), np.float32)
    fold = jnp.asarray(np.block([[fcos, zero], [zero, fsin]]), BF16)
    lane = np.arange(LANES)
    cbase = ((lane < ROPE_LO) | (lane >= ROPE_LO + QK_ROPE)).astype(np.float32).reshape(1, LANES)
    full = lambda i: (0, 0)
    return pl.pallas_call(
        _trig_kernel,
        grid=(T // tm,),
        in_specs=[
            pl.BlockSpec((tr, LANES), lambda i: (i, 0)),
            pl.BlockSpec((1, LANES), full),
            pl.BlockSpec((tm, tr), full),
            pl.BlockSpec((2 * LANES, 2 * LANES), full),
            pl.BlockSpec((1, LANES), full),
        ],
        out_specs=[pl.BlockSpec((tm, LANES), lambda i: (i, 0))] * 2,
        out_shape=[jax.ShapeDtypeStruct((T, LANES), F32)] * 2,
        compiler_params=_cparams(("arbitrary",)),
        name="rope_trig",
    )(pos_c, freq_c, rsel, fold, cbase)


def _inproj_kernel(x_ref, cos_ref, sin_ref, shift_ref, scale_ref, gmix_ref, win_ref, gq_ref, wuq_ref,
                   gkv_ref, wukv_ref, gqn_ref, gqr_ref, gkn_ref, gkr_ref,
                   cw_ref, cb_ref, wa_ref, ba_ref, wx_ref, bx_ref, lam_ref,
                   lru_ref, q_ref, k_ref, v_ref, tail_ref, carry_ref):
    HP = N_HEADS * HEAD_PAD

    @pl.when(pl.program_id(1) == 0)
    def _():
        tail_ref[...] = jnp.zeros_like(tail_ref)
        carry_ref[...] = jnp.zeros_like(carry_ref)

    x = x_ref[...]
    ms = jnp.mean(x * x, axis=-1, keepdims=True)
    gain = gmix_ref[...] * (1.0 + scale_ref[0])
    h = x * lax.rsqrt(ms + EPS) * gain + shift_ref[0]
    hb = h.astype(BF16)
    o1 = 2 * D_LRU
    o2 = Q_LORA
    o3 = o2 + KV_LORA
    z_lru = _dot(hb, win_ref[:, :o1])
    z = _dot(hb, win_ref[:, o1:])
    lru_ref[...] = _lru_tile(z_lru[:, :D_LRU], z_lru[:, D_LRU:], cw_ref, cb_ref, wa_ref, ba_ref,
                             wx_ref, bx_ref, lam_ref, tail_ref, carry_ref)
    ql = z[:, :o2]
    kvl = z[:, o2:o3]
    kr = z[:, o3:o3 + LANES]
    kr_rot = z[:, o3 + LANES:]

    qn = ql * lax.rsqrt(jnp.mean(ql * ql, axis=-1, keepdims=True) + EPS) * gq_ref[...]
    qq = _dot(qn.astype(BF16), wuq_ref[...])
    kvn = kvl * lax.rsqrt(jnp.mean(kvl * kvl, axis=-1, keepdims=True) + EPS) * gkv_ref[...]
    kv = _dot(kvn.astype(BF16), wukv_ref[...])

    tm = x.shape[0]
    lane = lax.broadcasted_iota(jnp.int32, (tm, HP), 1)
    pair_lane = lane & (2 * HEAD_PAD - 1)
    ones_cols = (pair_lane >= V_HEAD) & (pair_lane < 2 * HEAD_PAD - V_HEAD)
    v_ref[...] = jnp.where(ones_cols, 1.0, kv[:, HP:]).astype(BF16)

    cos_t = cos_ref[...]
    sin_t = sin_ref[...]
    gqn = gqn_ref[...]
    gkn = gkn_ref[...]
    cq = gqn * cos_t
    sq = gqr_ref[...] * sin_t
    kb = kr * (gkn * cos_t) + kr_rot * (gkr_ref[...] * sin_t)
    inv_w = 1.0 / QK_HEAD
    qscale = QK_HEAD ** -0.5 * LOG2_E
    for hh in range(N_HEADS):
        sl = slice(hh * HEAD_PAD, (hh + 1) * HEAD_PAD)
        qh = qq[:, sl]
        rq = lax.rsqrt(jnp.sum(qh * qh, axis=-1, keepdims=True) * inv_w + EPS) * qscale
        q_ref[:, sl] = ((qh * cq + qq[:, HP + hh * HEAD_PAD:HP + (hh + 1) * HEAD_PAD] * sq) * rq).astype(BF16)
        kraw = kv[:, sl] + kr
        rk = lax.rsqrt(jnp.sum(kraw * kraw, axis=-1, keepdims=True) * inv_w + EPS)
        k_ref[:, sl] = ((kv[:, sl] * gkn + kb) * rk).astype(BF16)


def _inproj(x2, cos_t, sin_t, mod3, g_mix, w_in_p, g_q_lat, w_uq_p, g_kv_lat, w_ukv_p,
            gqn_p, gqr_p, gkn_p, gkr_p, conv_w, conv_b, wa_d, b_a, wx_d, b_x, lam, B, S, tm):
    T, D = x2.shape
    ns = S // tm
    HP = N_HEADS * HEAD_PAD
    C = D_LRU
    row = lambda b, s: (b * ns + s, 0)
    full = lambda b, s: (0, 0)
    return pl.pallas_call(
        _inproj_kernel,
        grid=(B, ns),
        in_specs=[
            pl.BlockSpec((tm, D), row),
            pl.BlockSpec((tm, LANES), row),
            pl.BlockSpec((tm, LANES), row),
            pl.BlockSpec((1, 1, D), lambda b, s: (b * 6 + 0, 0, 0)),
            pl.BlockSpec((1, 1, D), lambda b, s: (b * 6 + 1, 0, 0)),
            pl.BlockSpec((1, D), full),
            pl.BlockSpec((D, D_IN_PAD), full),
            pl.BlockSpec((1, Q_LORA), full),
            pl.BlockSpec((Q_LORA, 2 * HP), full),
            pl.BlockSpec((1, KV_LORA), full),
            pl.BlockSpec((KV_LORA, 2 * HP), full),
            pl.BlockSpec((1, HEAD_PAD), full),
            pl.BlockSpec((1, HEAD_PAD), full),
            pl.BlockSpec((1, HEAD_PAD), full),
            pl.BlockSpec((1, HEAD_PAD), full),
            pl.BlockSpec((CONV_W, C), full),
            pl.BlockSpec((1, C), full),
            pl.BlockSpec((C, C), full),
            pl.BlockSpec((1, C), full),
            pl.BlockSpec((C, C), full),
            pl.BlockSpec((1, C), full),
            pl.BlockSpec((1, C), full),
        ],
        out_specs=[
            pl.BlockSpec((tm, C), row),
            pl.BlockSpec((tm, HP), row),
            pl.BlockSpec((tm, HP), row),
            pl.BlockSpec((tm, HP), row),
        ],
        out_shape=[
            jax.ShapeDtypeStruct((T, C), BF16),
            jax.ShapeDtypeStruct((T, HP), BF16),
            jax.ShapeDtypeStruct((T, HP), BF16),
            jax.ShapeDtypeStruct((T, HP), BF16),
        ],
        scratch_shapes=[pltpu.VMEM((SUBLANES, C), F32), pltpu.VMEM((SUBLANES, C), F32)],
        compiler_params=_cparams(("arbitrary", "arbitrary")),
        name="inproj",
    )(x2, cos_t, sin_t, mod3, mod3, g_mix, w_in_p, g_q_lat, w_uq_p, g_kv_lat, w_ukv_p,
      gqn_p, gqr_p, gkn_p, gkr_p, conv_w, conv_b, wa_d, b_a, wx_d, b_x, lam)


def _gelu_tanh(x):
    c = 0.7978845608028654
    hx = 0.5 * x
    return hx + hx * jnp.tanh(x * (c + (c * 0.044715) * (x * x)))


def _lru_tile(x, y, cw_ref, cb_ref, wa_ref, ba_ref, wx_ref, bx_ref, lam_ref, tail_ref, carry_ref):
    ts = x.shape[0]
    xext = jnp.concatenate([tail_ref[...], x], axis=0)
    cw = cw_ref[...]
    xc = x * cw[CONV_W - 1:CONV_W, :]
    for j in range(CONV_W - 1):
        sh = CONV_W - 1 - j
        xc = xc + xext[8 - sh:8 - sh + ts, :] * cw[j:j + 1, :]
    xc = xc + cb_ref[...]
    tail_ref[...] = x[ts - 8:, :]

    xb = xc.astype(BF16)
    r = _sigmoid(_dot(xb, wa_ref[...]) + ba_ref[...])
    i = _sigmoid(_dot(xb, wx_ref[...]) + bx_ref[...])
    lam = lam_ref[...]
    nl = -lam
    softplus = jnp.maximum(nl, 0.0) + jnp.log(1.0 + jnp.exp(-jnp.abs(nl)))
    log_a = (-LRU_C) * r * softplus
    a = jnp.exp(log_a)
    mult = jnp.sqrt(1.0 - a * a)
    u = mult * (i * xc)

    C = a.shape[1]
    a = a.reshape(ts // SUBLANES, SUBLANES, C)
    u = u.reshape(ts // SUBLANES, SUBLANES, C)
    sub = lax.broadcasted_iota(jnp.int32, (1, SUBLANES, 1), 1)
    sh = 1
    while sh < SUBLANES:
        a_prev = pltpu.roll(a, sh, axis=1)
        u_prev = pltpu.roll(u, sh, axis=1)
        m = sub >= sh
        u = jnp.where(m, a * u_prev + u, u)
        a = jnp.where(m, a * a_prev, a)
        sh *= 2
    a = a.reshape(ts, C)
    u = u.reshape(ts, C)
    h = carry_ref[0:1, :]
    groups = []
    for g0 in range(0, ts, SUBLANES):
        hg = u[g0:g0 + SUBLANES, :] + a[g0:g0 + SUBLANES, :] * h
        groups.append(hg)
        h = hg[SUBLANES - 1:SUBLANES, :]
    carry_ref[...] = jnp.broadcast_to(h, carry_ref.shape)
    hs = jnp.concatenate(groups, axis=0)
    return (_gelu_tanh(y) * hs).astype(BF16)


NEG_INF = -1e30


def _attn_kernel(q_ref, k_ref, v_ref, o_ref, *state, tq, tk):
    m_refs = state[:N_HEADS]
    acc_refs = state[N_HEADS:]
    qi = pl.program_id(1)
    causal = (lax.broadcasted_iota(jnp.int32, (tk, tk), 1)
              <= lax.broadcasted_iota(jnp.int32, (tk, tk), 0))
    lower = slice(tk, tq)

    def head_slice(hh):
        return slice(hh * HEAD_PAD, (hh + 1) * HEAD_PAD)

    def weights(sc, m_b):
        cols = [jnp.exp2(sc[:, c0:c0 + LANES] - m_b) for c0 in range(0, tk, LANES)]
        return jnp.concatenate(cols, axis=1).astype(BF16)

    def scores(hh, r0, rows):
        hs = head_slice(hh)
        return _dot_nt(q_ref[rows, hs], k_ref[pl.ds(r0, tk), hs])

    def update(hh, rows, sc, r0):
        hs = head_slice(hh)
        m_b = m_refs[hh][rows, :]
        m_new = jnp.maximum(m_b, jnp.max(sc, axis=-1, keepdims=True))
        alpha = jnp.exp2(m_b - m_new)
        m_refs[hh][rows, :] = m_new
        acc_refs[hh][rows, :] = (alpha * acc_refs[hh][rows, :]
                                 + _dot(weights(sc, m_new), v_ref[pl.ds(r0, tk), hs]))


    every = slice(0, tq)
    r_d0 = pl.multiple_of(qi * tq, tk)
    sc_next = scores(0, r_d0, every)
    for hh in range(N_HEADS):
        hs = head_slice(hh)
        sc = jnp.concatenate([jnp.where(causal, sc_next[:tk], NEG_INF), sc_next[tk:]], axis=0)
        if hh + 1 < N_HEADS:
            sc_next = scores(hh + 1, r_d0, every)
        m_b = jnp.broadcast_to(jnp.max(sc, axis=-1, keepdims=True), (tq, LANES))
        m_refs[hh][...] = m_b
        acc_refs[hh][...] = _dot(weights(sc, m_b), v_ref[pl.ds(r_d0, tk), hs])

    r_d1 = pl.multiple_of(qi * tq + tk, tk)
    sc_next = scores(0, r_d1, lower)
    for hh in range(N_HEADS):
        sc = jnp.where(causal, sc_next, NEG_INF)
        if hh + 1 < N_HEADS:
            sc_next = scores(hh + 1, r_d1, lower)
        update(hh, lower, sc, r_d1)

    @pl.loop(0, qi * (tq // tk))
    def _(j):
        r0 = pl.multiple_of(j * tk, tk)
        sc_next = scores(0, r0, every)
        for hh in range(N_HEADS):
            sc = sc_next
            if hh + 1 < N_HEADS:
                sc_next = scores(hh + 1, r0, every)
            update(hh, every, sc, r0)

    low = lax.broadcasted_iota(jnp.int32, (tq, HEAD_PAD), 1) < V_HEAD
    for he in range(0, N_HEADS, 2):
        acc_e = acc_refs[he][...]
        acc_o = acc_refs[he + 1][...]
        num = jnp.where(low, acc_e, acc_o)
        den = pltpu.roll(jnp.where(low, acc_o, acc_e), V_HEAD, axis=1)
        o_ref[:, he * V_HEAD:(he + 2) * V_HEAD] = (num / den).astype(BF16)


def _attn(qp, kp, v, B, S, tk):
    T = qp.shape[0]
    tq = 2 * tk
    nq = S // tq
    HP = N_HEADS * HEAD_PAD
    HV = N_HEADS * V_HEAD
    return pl.pallas_call(
        functools.partial(_attn_kernel, tq=tq, tk=tk),
        grid=(B, nq),
        in_specs=[
            pl.BlockSpec((tq, HP), lambda b, i: (b * nq + i, 0)),
            pl.BlockSpec((S, HP), lambda b, i: (b, 0)),
            pl.BlockSpec((S, HP), lambda b, i: (b, 0)),
        ],
        out_specs=pl.BlockSpec((tq, HV), lambda b, i: (b * nq + i, 0)),
        out_shape=jax.ShapeDtypeStruct((T, HV), BF16),
        scratch_shapes=([pltpu.VMEM((tq, LANES), F32)] * N_HEADS
                        + [pltpu.VMEM((tq, HEAD_PAD), F32)] * N_HEADS),
        compiler_params=_cparams(("arbitrary", "arbitrary")),
        name="attn",
    )(qp, kp, v)


def _outproj_kernel(lru_ref, att_ref, x_ref, gate_ref, shift_ref, scale_ref, gffn_ref,
                    wo1_ref, wo2_ref, wr_ref, br_ref, tri_ref,
                    x1_ref, h2p_ref, idx_ref, gat_ref, rank_ref, cnt_ref, run_ref):
    first = (pl.program_id(0) == 0) & (pl.program_id(1) == 0)

    @pl.when(first)
    def _():
        run_ref[...] = jnp.zeros_like(run_ref)

    mix = _dot(lru_ref[...], wo1_ref[...]) + _dot(att_ref[...], wo2_ref[...])
    x1 = x_ref[...] + gate_ref[0] * mix
    x1_ref[...] = x1
    ms = jnp.mean(x1 * x1, axis=-1, keepdims=True)
    gain = gffn_ref[...] * (1.0 + scale_ref[0])
    h2 = x1 * lax.rsqrt(ms + EPS) * gain + shift_ref[0]

    hhi = h2.astype(BF16)
    hlo = (h2 - hhi.astype(F32)).astype(BF16)
    h2p_ref[...] = _pack_halves(h2)

    ne = br_ref.shape[0]
    stacked = _dot_nt(wr_ref[...], hhi)
    logits = stacked[:ne] + stacked[ne:] + _dot_nt(wr_ref[:ne, :], hlo) + br_ref[...]

    tm = logits.shape[1]
    eio = lax.broadcasted_iota(jnp.int32, (ne, tm), 0)
    vals, idxs, sels = [], [], []
    l = logits
    for _ in range(TOP_K):
        m = jnp.max(l, axis=0, keepdims=True)
        idx = jnp.min(jnp.where(l == m, eio, ne), axis=0, keepdims=True)
        sel = eio == idx
        l = jnp.where(sel, -jnp.inf, l)
        vals.append(m)
        idxs.append(idx)
        sels.append(sel)
    es = [jnp.exp(v - vals[0]) for v in vals]
    inv = 1.0 / (es[0] + es[1] + es[2] + es[3])
    sel_any = jnp.where(sels[0] | sels[1] | sels[2] | sels[3], 1.0, 0.0)
    run = run_ref[...]
    excl = _dot(sel_any.astype(BF16), tri_ref[...]) + run
    for kk in range(TOP_K):
        idx_ref[kk:kk + 1, :] = idxs[kk]
        gat_ref[kk:kk + 1, :] = es[kk] * inv
        rk = jnp.sum(jnp.where(sels[kk], excl, 0.0), axis=0, keepdims=True)
        rank_ref[kk:kk + 1, :] = rk.astype(jnp.int32)
    run = run + jnp.sum(sel_any, axis=1, keepdims=True)
    run_ref[...] = run
    cnt_ref[...] = run.astype(jnp.int32)


def _outproj(lru_o, att_o, x2, mod3, g_ffn, wo1, wo2, wr_stack, b_r, tri, b0, B, S, tm):
    D = x2.shape[1]
    T = B * S
    ns = S // tm
    C = lru_o.shape[1]
    row_in = lambda b, s: ((b0 + b) * ns + s, 0)
    row = lambda b, s: (b * ns + s, 0)
    col = lambda b, s: (0, b * ns + s)
    full = lambda b, s: (0, 0)
    return pl.pallas_call(
        _outproj_kernel,
        grid=(B, ns),
        in_specs=[
            pl.BlockSpec((tm, C), row_in),
            pl.BlockSpec((tm, C), row_in),
            pl.BlockSpec((tm, D), row_in),
            pl.BlockSpec((1, 1, D), lambda b, s: ((b0 + b) * 6 + 2, 0, 0)),
            pl.BlockSpec((1, 1, D), lambda b, s: ((b0 + b) * 6 + 3, 0, 0)),
            pl.BlockSpec((1, 1, D), lambda b, s: ((b0 + b) * 6 + 4, 0, 0)),
            pl.BlockSpec((1, D), full),
            pl.BlockSpec((C, D), full),
            pl.BlockSpec((C, D), full),
            pl.BlockSpec((2 * N_EXPERTS, D), full),
            pl.BlockSpec((N_EXPERTS, 1), full),
            pl.BlockSpec(tri.shape, full),
        ],
        out_specs=[
            pl.BlockSpec((tm, D), row),
            pl.BlockSpec((tm, D // 2), row),
            pl.BlockSpec((TOP_K, tm), col),
            pl.BlockSpec((TOP_K, tm), col),
            pl.BlockSpec((TOP_K, tm), col),
            pl.BlockSpec((N_EXPERTS, 1), full),
        ],
        out_shape=[
            jax.ShapeDtypeStruct((T, D), F32),
            jax.ShapeDtypeStruct((T, D // 2), jnp.int32),
            jax.ShapeDtypeStruct((TOP_K, T), jnp.int32),
            jax.ShapeDtypeStruct((TOP_K, T), F32),
            jax.ShapeDtypeStruct((TOP_K, T), jnp.int32),
            jax.ShapeDtypeStruct((N_EXPERTS, 1), jnp.int32),
        ],
        scratch_shapes=[pltpu.VMEM((N_EXPERTS, 1), F32)],
        compiler_params=_cparams(("arbitrary", "arbitrary")),
        name="outproj",
    )(lru_o, att_o, x2, mod3, mod3, mod3, g_ffn, wo1, wo2, wr_stack, b_r, tri)


SC_CORES = 2
SC_SUBCORES = 16
SC_WORKERS = SC_CORES * SC_SUBCORES
SC_LANES = 16


def _sc_mesh():
    return plsc.VectorSubcoreMesh(core_axis_name="c", subcore_axis_name="s",
                                  num_cores=SC_CORES, num_subcores=SC_SUBCORES)


def _sc_worker_id():
    return lax.axis_index("s") * SC_CORES + lax.axis_index("c")


def _sc_scatter_rows(rows, idx, n_out, g):
    T, W = rows.shape
    K = idx.shape[0]
    per_w = T // SC_WORKERS
    nch = per_w // g
    assert per_w * SC_WORKERS == T and nch * g == per_w and nch % 2 == 0
    idx_w = idx.reshape(K, SC_WORKERS, nch, g).transpose(1, 2, 0, 3).reshape(SC_WORKERS, nch * K, g)

    def body(rows_hbm, idx_hbm, out_hbm, idx_v, buf0, buf1, semr0, semr1, semw):
        wid = _sc_worker_id()
        base = wid * per_w
        pltpu.sync_copy(idx_hbm.at[wid], idx_v)

        def read(j, buf, sem):
            return pltpu.make_async_copy(rows_hbm.at[pl.ds(base + j * g, g)], buf, sem)

        def scatter(j, buf):
            copies = [pltpu.async_copy(buf, out_hbm.at[idx_v.at[j * K + kk]], semw)
                      for kk in range(K)]
            for cp in copies:
                cp.wait()

        read(0, buf0, semr0).start()

        @pl.loop(0, nch // 2)
        def _(jj):
            j0 = 2 * jj
            read(j0 + 1, buf1, semr1).start()
            read(j0, buf0, semr0).wait()
            scatter(j0, buf0)

            @pl.when(j0 + 2 < nch)
            def _():
                read(j0 + 2, buf0, semr0).start()

            read(j0 + 1, buf1, semr1).wait()
            scatter(j0 + 1, buf1)

    return pl.kernel(
        body,
        out_type=jax.ShapeDtypeStruct((n_out, W), rows.dtype),
        mesh=_sc_mesh(),
        scratch_types=[
            pltpu.VMEM((nch * K, g), jnp.int32),
            pltpu.VMEM((g, W), rows.dtype),
            pltpu.VMEM((g, W), rows.dtype),
            pltpu.SemaphoreType.DMA,
            pltpu.SemaphoreType.DMA,
            pltpu.SemaphoreType.DMA,
        ],
        name="sc_scatter_rows",
    )(rows, idx_w)


def _sc_gather_gated_sum(table, idx, gates, g):
    W = table.shape[1]
    K, T = idx.shape
    per_w = T // SC_WORKERS
    nch = per_w // g
    assert per_w * SC_WORKERS == T and nch * g == per_w and nch % 2 == 0 and W % SC_LANES == 0
    idx_w = idx.reshape(K, SC_WORKERS, nch, g).transpose(1, 2, 0, 3).reshape(SC_WORKERS, nch * K, g)
    gates_w = gates.reshape(K, SC_WORKERS, nch, g).transpose(1, 2, 0, 3).reshape(SC_WORKERS, nch, K * g)
    def body(table_hbm, idx_hbm, gates_hbm, out_hbm, idx_v, rows_v, gts_v, out_v, sem0, sem1):
        wid = _sc_worker_id()
        base = wid * per_w
        pltpu.sync_copy(idx_hbm.at[wid], idx_v)

        def fetch(j, slot, sem):
            cps = []
            for kk in range(K):
                cps.append(pltpu.make_async_copy(table_hbm.at[idx_v.at[j * K + kk]],
                                                 rows_v.at[slot, kk], sem))
            cps.append(pltpu.make_async_copy(gates_hbm.at[wid, j], gts_v.at[slot], sem))
            return cps

        def start(j, slot, sem):
            for cp in fetch(j, slot, sem):
                cp.start()

        def finish(j, slot, sem):
            for cp in fetch(j, slot, sem):
                cp.wait()

            @pl.loop(0, g)
            def _(t):
                gk = [plsc.load_gather(gts_v.at[slot], [jnp.full((SC_LANES,), kk * g, jnp.int32) + t])
                      for kk in range(K)]

                @plsc.parallel_loop(0, W, SC_LANES, unroll=4)
                def _(off):
                    off = pl.multiple_of(off, SC_LANES)
                    acc_lo = jnp.zeros((SC_LANES,), F32)
                    acc_hi = jnp.zeros((SC_LANES,), F32)
                    for kk in range(K):
                        w = rows_v[slot, kk, t, pl.ds(off, SC_LANES)]
                        lo, hi = plsc.unpack(plsc.bitcast(w, BF16), format=plsc.PackFormat.INTERLEAVED)
                        acc_lo = acc_lo + gk[kk] * lo
                        acc_hi = acc_hi + gk[kk] * hi
                    out_v[slot, t, pl.ds(off, SC_LANES)] = acc_lo
                    out_v[slot, t, pl.ds(W + off, SC_LANES)] = acc_hi

            pltpu.sync_copy(out_v.at[slot], out_hbm.at[pl.ds(base + j * g, g)])

        start(0, 0, sem0)

        @pl.loop(0, nch // 2)
        def _(jj):
            j0 = 2 * jj
            start(j0 + 1, 1, sem1)
            finish(j0, 0, sem0)

            @pl.when(j0 + 2 < nch)
            def _():
                start(j0 + 2, 0, sem0)

            finish(j0 + 1, 1, sem1)

    return pl.kernel(
        body,
        out_type=jax.ShapeDtypeStruct((T, 2 * W), F32),
        mesh=_sc_mesh(),
        scratch_types=[
            pltpu.VMEM((nch * K, g), jnp.int32),
            pltpu.VMEM((2, K, g, W), jnp.int32),
            pltpu.VMEM((2, K * g), F32),
            pltpu.VMEM((2, g, 2 * W), F32),
            pltpu.SemaphoreType.DMA,
            pltpu.SemaphoreType.DMA,
        ],
        compiler_params=pltpu.CompilerParams(needs_layout_passes=False),
        name="sc_gather_gated_sum",
    )(table, idx_w, gates_w)


def _experts_kernel(be_ref, bv_ref, bf_ref, bn_ref, bs_ref, xs_ref, w1_hbm, b1_ref, w2_hbm, b2_ref,
                    ys_ref, w1s_ref, w2s_ref, w1b_ref, w2b_ref, sem):
    i = pl.program_id(0)
    nvalid = bv_ref[i]

    def weight_copies(e, slot):
        return (pltpu.make_async_copy(w1_hbm.at[e], w1s_ref.at[slot], sem.at[0, slot]),
                pltpu.make_async_copy(w2_hbm.at[e], w2s_ref.at[slot], sem.at[1, slot]))

    @pl.when(bf_ref[i] > 0)
    def _():
        slot = bs_ref[i]

        @pl.when(i == 0)
        def _():
            for cp in weight_copies(be_ref[0], slot):
                cp.start()

        for cp in weight_copies(be_ref[i], slot):
            cp.wait()
        w1b_ref[...] = w1s_ref[slot].astype(BF16)
        w2b_ref[...] = w2s_ref[slot].astype(BF16)

        @pl.when(bn_ref[i] >= 0)
        def _():
            for cp in weight_copies(bn_ref[i], 1 - slot):
                cp.start()

    def ffn(rows):
        xw = xs_ref[:rows, :]
        rowi = lax.broadcasted_iota(jnp.int32, (rows, 1), 0)
        lo, hi = _unpack_halves(jnp.where(rowi < nvalid, xw, 0))
        xb = jnp.concatenate([lo.astype(BF16), hi.astype(BF16)], axis=1)
        gu = _dot(xb, w1b_ref[...]) + b1_ref[0]
        glu = jnp.minimum(gu[:, :D_FF], SWIGLU_LIMIT)
        lin = jnp.clip(gu[:, D_FF:], -SWIGLU_LIMIT, SWIGLU_LIMIT)
        act = (lin + 1.0) * (glu * _sigmoid(glu, SWIGLU_ALPHA))
        ys_ref[:rows, :] = _pack_halves(_dot(act.astype(BF16), w2b_ref[...]) + b2_ref[0])

    step = xs_ref.shape[0] // ROW_PARTS
    for nq in range(1, ROW_PARTS + 1):
        pl.when((nvalid > (nq - 1) * step) & (nvalid <= nq * step))(functools.partial(ffn, nq * step))


def _experts(blk_e, blk_v, blk_f, blk_n, blk_s, blk_r, xs, w1, b1, w2, b2):
    P, W = xs.shape
    nb = P // MOE_BLOCK
    E, D, F2 = w1.shape
    grid_spec = pltpu.PrefetchScalarGridSpec(
        num_scalar_prefetch=6,
        grid=(nb,),
        in_specs=[
            pl.BlockSpec((MOE_BLOCK, W), lambda i, be, bv, bf, bn, bs, br: (br[i], 0)),
            pl.BlockSpec(memory_space=pl.ANY),
            pl.BlockSpec((1, 1, F2), lambda i, be, bv, bf, bn, bs, br: (be[i], 0, 0)),
            pl.BlockSpec(memory_space=pl.ANY),
            pl.BlockSpec((1, 1, D), lambda i, be, bv, bf, bn, bs, br: (be[i], 0, 0)),
        ],
        out_specs=pl.BlockSpec((MOE_BLOCK, D // 2), lambda i, be, bv, bf, bn, bs, br: (br[i], 0)),
        scratch_shapes=[
            pltpu.VMEM((2, D, F2), F32),
            pltpu.VMEM((2, D_FF, D), F32),
            pltpu.VMEM((D, F2), BF16),
            pltpu.VMEM((D_FF, D), BF16),
            pltpu.SemaphoreType.DMA((2, 2)),
        ],
    )

    def kern(be_ref, bv_ref, bf_ref, bn_ref, bs_ref, br_ref, *refs):
        del br_ref
        _experts_kernel(be_ref, bv_ref, bf_ref, bn_ref, bs_ref, *refs)

    return pl.pallas_call(
        kern,
        grid_spec=grid_spec,
        out_shape=jax.ShapeDtypeStruct((P, D // 2), jnp.int32),
        compiler_params=_cparams(("arbitrary",)),
        name="experts",
    )(blk_e, blk_v, blk_f, blk_n, blk_s, blk_r, xs, w1, b1.reshape(E, 1, F2), w2,
      b2.reshape(E, 1, D))


def _combine_kernel(x1_ref, gate_ref, ysum_ref, *rest):
    o_ref = rest[-1]
    o_ref[...] = x1_ref[...] + gate_ref[0] * ysum_ref[...]


def _combine(x1, mod3, ysum, out_prev, b0, B, nb_total, S, tm):
    D = x1.shape[1]
    ns = S // tm
    row = lambda b, s: (b * ns + s, 0)
    row_out = lambda b, s: ((b0 + b) * ns + s, 0)
    in_specs = [
        pl.BlockSpec((tm, D), row),
        pl.BlockSpec((1, 1, D), lambda b, s: ((b0 + b) * 6 + 5, 0, 0)),
        pl.BlockSpec((tm, D), row),
    ]
    args = [x1, mod3, ysum]
    aliases = {}
    if out_prev is not None:
        in_specs.append(pl.BlockSpec(memory_space=pl.ANY))
        args.append(out_prev)
        aliases = {len(args) - 1: 0}
    return pl.pallas_call(
        _combine_kernel,
        grid=(B, ns),
        in_specs=in_specs,
        out_specs=pl.BlockSpec((tm, D), row_out),
        out_shape=jax.ShapeDtypeStruct((nb_total * S, D), F32),
        input_output_aliases=aliases,
        compiler_params=_cparams(("arbitrary", "arbitrary")),
        name="combine",
    )(*args)


def _block_diag(w):
    n, c, d = w.shape
    eye = jnp.eye(n, dtype=w.dtype)
    return jnp.einsum("ncd,nm->ncmd", w, eye).reshape(n * c, n * d)


def _pad_heads(w, width):
    k = w.shape[0]
    w = w.reshape(k, N_HEADS, width)
    return jnp.pad(w, ((0, 0), (0, 0), (0, HEAD_PAD - width))).reshape(k, N_HEADS * HEAD_PAD)


def kernel(x, c, positions, w_ada, b_ada, g_mix, w_in, conv_w, conv_b, w_a, b_a, w_x, b_x, lam,
           g_q_lat, w_uq, g_kv_lat, w_ukv, g_qn, g_kn, w_out, g_ffn, w_router, b_router,
           w1, b1, w2, b2):
    B, S, D = x.shape
    T = B * S
    depth = w_ada.shape[0]
    tm_in = min(512, S)
    tk_att = min(512, S // 2)
    tm_out = min(512, S)
    tm_comb = min(1024, S)
    n_groups = 2 if B % 2 == 0 else 1
    Bg = B // n_groups
    Tg = Bg * S
    g_disp = min(64, Tg // SC_WORKERS // 2)
    g_comb = min(16, Tg // SC_WORKERS // 2)

    o1 = 2 * D_LRU
    o2 = o1 + Q_LORA
    o3 = o2 + KV_LORA
    tri = jnp.asarray(np.arange(tm_out)[:, None] < np.arange(tm_out)[None, :], BF16)
    cos_t, sin_t = _rope_tables(positions)

    def rot_cols(w):
        k = w.shape[0]
        w3 = w.reshape(k, -1, HEAD_PAD)
        lo = w3[:, :, ROPE_LO:ROPE_LO + ROPE_HALF]
        hi = w3[:, :, ROPE_LO + ROPE_HALF:ROPE_LO + QK_ROPE]
        zl = jnp.zeros_like(w3[:, :, :ROPE_LO])
        zr = jnp.zeros_like(w3[:, :, ROPE_LO + QK_ROPE:])
        return jnp.concatenate([zl, hi, lo, zr], axis=2).reshape(w.shape)

    x2 = x.reshape(T, D)
    for l in range(depth):
        mod3 = _ada(c, w_ada[l], b_ada[l]).reshape(B * 6, 1, D)

        w_in_l = w_in[l]
        kr_cols = jnp.pad(w_in_l[:, o3:], ((0, 0), (ROPE_LO, LANES - ROPE_LO - QK_ROPE)))
        w_in_p = jnp.concatenate([w_in_l[:, :o3], kr_cols, rot_cols(kr_cols)], axis=1).astype(BF16)
        w_uq_h = _pad_heads(w_uq[l], QK_HEAD)
        w_uq_p = jnp.concatenate([w_uq_h, rot_cols(w_uq_h)], axis=1).astype(BF16)
        w_ukv_l = w_ukv[l].reshape(KV_LORA, N_HEADS, QK_NOPE + V_HEAD)
        w_uk_h = _pad_heads(w_ukv_l[:, :, :QK_NOPE].reshape(KV_LORA, N_HEADS * QK_NOPE), QK_NOPE)
        w_uv_pairs = w_ukv_l[:, :, QK_NOPE:].reshape(KV_LORA, N_HEADS // 2, 2, V_HEAD)
        zero_v = jnp.zeros_like(w_uv_pairs[:, :, 0])
        w_uv_h = jnp.stack([w_uv_pairs[:, :, 0], zero_v, zero_v, w_uv_pairs[:, :, 1]],
                           axis=2).reshape(KV_LORA, N_HEADS * HEAD_PAD)
        w_ukv_p = jnp.concatenate([w_uk_h, w_uv_h], axis=1).astype(BF16)
        gqn_p = jnp.pad(g_qn[l], (0, HEAD_PAD - QK_HEAD)).reshape(1, HEAD_PAD)
        gkn_p = jnp.pad(g_kn[l], (0, HEAD_PAD - QK_HEAD)).reshape(1, HEAD_PAD)

        lru_o, qp, kp, v = _inproj(
            x2, cos_t, sin_t, mod3, g_mix[l].reshape(1, D), w_in_p, g_q_lat[l].reshape(1, Q_LORA),
            w_uq_p, g_kv_lat[l].reshape(1, KV_LORA), w_ukv_p, gqn_p, rot_cols(gqn_p), gkn_p,
            rot_cols(gkn_p), conv_w[l], conv_b[l].reshape(1, D_LRU),
            _block_diag(w_a[l]).astype(BF16), b_a[l].reshape(1, D_LRU),
            _block_diag(w_x[l]).astype(BF16), b_x[l].reshape(1, D_LRU),
            lam[l].reshape(1, D_LRU), B, S, tm_in)

        att_o = _attn(qp, kp, v, B, S, tk_att)

        w_out_b = w_out[l].astype(BF16)
        wr_stack = jnp.concatenate(_split_bf16(w_router[l].T), axis=0)
        g_ffn_l = g_ffn[l].reshape(1, D)
        b_r = b_router[l].reshape(N_EXPERTS, 1)
        eio = np.arange(N_EXPERTS, dtype=np.int32)
        n_blocks = -(-(Tg * TOP_K) // MOE_BLOCK) + N_EXPERTS
        bi = np.arange(n_blocks, dtype=np.int32)

        x_next = None
        for gi in range(n_groups):
            b0 = gi * Bg
            x1, h2p, idx_t, gat_t, rank_t, counts = _outproj(
                lru_o, att_o, x2, mod3, g_ffn_l, w_out_b[:D_LRU], w_out_b[D_LRU:],
                wr_stack, b_r, tri, b0, Bg, S, tm_out)

            counts = counts.reshape(N_EXPERTS)
            nblk_e = (counts + MOE_BLOCK - 1) // MOE_BLOCK
            blk_end = jnp.cumsum(nblk_e)
            pad_start = (blk_end - nblk_e) * MOE_BLOCK
            total = blk_end[-1]
            blk_r = jnp.minimum(bi, total - 1).astype(jnp.int32)
            blk_e = jnp.minimum(jnp.sum(blk_end[None, :] <= blk_r[:, None], axis=1),
                                N_EXPERTS - 1).astype(jnp.int32)
            blk_onehot = blk_e[:, None] == eio[None, :]
            blk_first = jnp.sum(jnp.where(blk_onehot, (blk_end - nblk_e)[None, :], 0), axis=1)
            blk_cnt = jnp.sum(jnp.where(blk_onehot, counts[None, :], 0), axis=1)
            blk_v = jnp.where(bi < total,
                              jnp.clip(blk_cnt - (bi - blk_first) * MOE_BLOCK, 0, MOE_BLOCK),
                              0).astype(jnp.int32)
            blk_f = ((bi == blk_first) & (bi < total)).astype(jnp.int32)
            nxt_first = jnp.sum(jnp.where(blk_onehot, blk_end[None, :], 0), axis=1)
            nxt_e = jnp.minimum(jnp.sum(blk_end[None, :] <= nxt_first[:, None], axis=1),
                                N_EXPERTS - 1)
            blk_n = jnp.where(nxt_first < total, nxt_e, -1).astype(jnp.int32)
            ordinal = jnp.cumsum((nblk_e > 0).astype(jnp.int32)) - 1
            blk_s = (jnp.sum(jnp.where(blk_onehot, ordinal[None, :], 0), axis=1) % 2).astype(jnp.int32)
            slot0 = jnp.sum(jnp.where(idx_t[None] == eio[:, None, None],
                                      pad_start[:, None, None], 0), axis=0)
            dest = slot0.astype(jnp.int32) + rank_t

            xs = _sc_scatter_rows(h2p, dest, n_blocks * MOE_BLOCK, g_disp)
            ys = _experts(blk_e, blk_v, blk_f, blk_n, blk_s, blk_r, xs, w1[l], b1[l], w2[l], b2[l])
            ysum = _sc_gather_gated_sum(ys, dest, gat_t, g_comb)
            x_next = _combine(x1, mod3, ysum, x_next, b0, Bg, B, S, tm_comb)
        x2 = x_next
    return x2.reshape(B, S, D)
```

```python
import functools

import jax
import jax.numpy as jnp
import numpy as np
from jax import lax
from jax.experimental import pallas as pl
from jax.experimental.pallas import tpu as pltpu
from jax.experimental.pallas import tpu_sc as plsc

D_MODEL = 1024
D_LRU = 512
LRU_BLOCKS = 8
LRU_BD = 64
CONV_W = 4
LRU_C = 8.0
N_HEADS = 8
QK_NOPE = 64
QK_ROPE = 32
QK_HEAD = 96
V_HEAD = 64
Q_LORA = 256
KV_LORA = 128
ROPE_THETA = 10000.0
N_EXPERTS = 32
TOP_K = 4
D_FF = 1024
SWIGLU_LIMIT = 7.0
SWIGLU_ALPHA = 1.702
MOE_BLOCK = 1024
EPS = 1e-6

LANES = 128
SUBLANES = 8
HEAD_PAD = 128
ROPE_LO = QK_NOPE
ROPE_HALF = QK_ROPE // 2
TOK_PER_ROW = LANES // ROPE_HALF
D_IN_PAD = 2 * D_LRU + Q_LORA + KV_LORA + 2 * LANES
LOG2_E = 1.4426950408889634
MAX_GROUPS = 3
ROW_PARTS = 8

VMEM_LIMIT = 56 * 1024 * 1024

F32 = jnp.float32
BF16 = jnp.bfloat16


def _cparams(sem):
    return pltpu.CompilerParams(dimension_semantics=sem, vmem_limit_bytes=VMEM_LIMIT)


def _dot(a, b):
    return jnp.dot(a, b, preferred_element_type=F32)


def _dot_nt(a, b):
    return lax.dot_general(a, b, (((1,), (1,)), ((), ())), preferred_element_type=F32)


def _split_bf16(a):
    hi = a.astype(BF16)
    lo = (a - hi.astype(F32)).astype(BF16)
    return hi, lo


def _sigmoid(x, scale=1.0):
    return 1.0 / (1.0 + jnp.exp2(x * (-scale * LOG2_E)))


def _pack_halves(x):
    bits = lax.bitcast_convert_type(x.astype(BF16).astype(F32), jnp.uint32)
    half = x.shape[1] // 2
    words = (bits[:, :half] >> 16) | (bits[:, half:] & jnp.uint32(0xFFFF0000))
    return lax.bitcast_convert_type(words, jnp.int32)


def _unpack_halves(words):
    w = lax.bitcast_convert_type(words, jnp.uint32)
    lo = lax.bitcast_convert_type(w << 16, F32)
    hi = lax.bitcast_convert_type(w & jnp.uint32(0xFFFF0000), F32)
    return lo, hi


def _ada_kernel(c_ref, w_ref, b_ref, o_ref):
    c = c_ref[...]
    s = c * _sigmoid(c)
    shi, slo = _split_bf16(s)
    whi, wlo = _split_bf16(w_ref[...])
    o_ref[...] = _dot(shi, whi) + _dot(slo, whi) + _dot(shi, wlo) + b_ref[...]


def _ada(c, w_ada, b_ada):
    B, D = c.shape
    N = w_ada.shape[1]
    tn = 1024
    return pl.pallas_call(
        _ada_kernel,
        grid=(N // tn,),
        in_specs=[
            pl.BlockSpec((B, D), lambda j: (0, 0)),
            pl.BlockSpec((D, tn), lambda j: (0, j)),
            pl.BlockSpec((1, tn), lambda j: (0, j)),
        ],
        out_specs=pl.BlockSpec((B, tn), lambda j: (0, j)),
        out_shape=jax.ShapeDtypeStruct((B, N), F32),
        compiler_params=_cparams(("arbitrary",)),
        name="ada",
    )(c, w_ada, b_ada.reshape(1, N))


def _trig_kernel(pos_ref, freq_ref, rsel_ref, fold_ref, cbase_ref, cos_ref, sin_ref):
    ang = pos_ref[...].astype(F32) * freq_ref[...]
    cs = jnp.concatenate([jnp.cos(ang), jnp.sin(ang)], axis=1)
    tm = cos_ref.shape[0]
    row = lax.broadcasted_iota(jnp.int32, (tm, 2 * LANES), 0)
    lane = lax.broadcasted_iota(jnp.int32, (tm, 2 * LANES), 1)
    own = ((lane % LANES) // ROPE_HALF) == (row % TOK_PER_ROW)
    rsel = rsel_ref[...]
    fold = fold_ref[...]
    by_row = sum(_dot(rsel, part) for part in _split_bf16(cs))
    mine = jnp.where(own, by_row, 0.0)
    out = sum(_dot(part, fold) for part in _split_bf16(mine))
    cos_ref[...] = out[:, :LANES] + cbase_ref[...]
    sin_ref[...] = out[:, LANES:]


def _rope_tables(positions):
    T = positions.size
    rows = T // TOK_PER_ROW
    pos_c = jnp.repeat(positions.reshape(T).astype(jnp.int32), ROPE_HALF).reshape(rows, LANES)
    tm = min(2048, T)
    tr = tm // TOK_PER_ROW
    freqs = np.float32(ROPE_THETA) ** (-np.arange(ROPE_HALF, dtype=np.float32) / np.float32(ROPE_HALF))
    freq_c = np.tile(freqs.astype(np.float32), TOK_PER_ROW).reshape(1, LANES)
    rsel = jnp.asarray(np.arange(tm)[:, None] // TOK_PER_ROW == np.arange(tr)[None, :], BF16)
    src = np.arange(LANES)[:, None] % ROPE_HALF
    dst = np.arange(LANES)[None, :]
    first = dst == ROPE_LO + src
    second = dst == ROPE_LO + ROPE_HALF + src
    fcos = (first | second).astype(np.float32)
    fsin = second.astype(np.float32) - first.astype(np.float32)
    zero = np.zeros((LANES, LANES), np.float32)
    fold = jnp.asarray(np.block([[fcos, zero], [zero, fsin]]), BF16)
    lane = np.arange(LANES)
    cbase = ((lane < ROPE_LO) | (lane >= ROPE_LO + QK_ROPE)).astype(np.float32).reshape(1, LANES)
    full = lambda i: (0, 0)
    return pl.pallas_call(
        _trig_kernel,
        grid=(T // tm,),
        in_specs=[
            pl.BlockSpec((tr, LANES), lambda i: (i, 0)),
            pl.BlockSpec((1, LANES), full),
            pl.BlockSpec((tm, tr), full),
            pl.BlockSpec((2 * LANES, 2 * LANES), full),
            pl.BlockSpec((1, LANES), full),
        ],
        out_specs=[pl.BlockSpec((tm, LANES), lambda i: (i, 0))] * 2,
        out_shape=[jax.ShapeDtypeStruct((T, LANES), F32)] * 2,
        compiler_params=_cparams(("arbitrary",)),
        name="rope_trig",
    )(pos_c, freq_c, rsel, fold, cbase)


def _inproj_kernel(x_ref, cos_ref, sin_ref, shift_ref, scale_ref, gmix_ref, win_ref, gq_ref, wuq_ref,
                   gkv_ref, wukv_ref, gqn_ref, gqr_ref, gkn_ref, gkr_ref,
                   cw_ref, cb_ref, wa_ref, ba_ref, wx_ref, bx_ref, lam_ref,
                   lru_ref, q_ref, k_ref, v_ref, tail_ref, carry_ref):
    HP = N_HEADS * HEAD_PAD

    @pl.when(pl.program_id(1) == 0)
    def _():
        tail_ref[...] = jnp.zeros_like(tail_ref)
        carry_ref[...] = jnp.zeros_like(carry_ref)

    x = x_ref[...]
    ms = jnp.mean(x * x, axis=-1, keepdims=True)
    gain = gmix_ref[...] * (1.0 + scale_ref[0])
    h = x * lax.rsqrt(ms + EPS) * gain + shift_ref[0]
    hb = h.astype(BF16)
    o1 = 2 * D_LRU
    o2 = Q_LORA
    o3 = o2 + KV_LORA
    z_lru = _dot(hb, win_ref[:, :o1])
    z = _dot(hb, win_ref[:, o1:])
    lru_ref[...] = _lru_tile(z_lru[:, :D_LRU], z_lru[:, D_LRU:], cw_ref, cb_ref, wa_ref, ba_ref,
                             wx_ref, bx_ref, lam_ref, tail_ref, carry_ref)
    ql = z[:, :o2]
    kvl = z[:, o2:o3]
    kr = z[:, o3:o3 + LANES]
    kr_rot = z[:, o3 + LANES:]

    qn = ql * lax.rsqrt(jnp.mean(ql * ql, axis=-1, keepdims=True) + EPS) * gq_ref[...]
    qq = _dot(qn.astype(BF16), wuq_ref[...])
    kvn = kvl * lax.rsqrt(jnp.mean(kvl * kvl, axis=-1, keepdims=True) + EPS) * gkv_ref[...]
    kv = _dot(kvn.astype(BF16), wukv_ref[...])

    tm = x.shape[0]
    lane = lax.broadcasted_iota(jnp.int32, (tm, HP), 1)
    pair_lane = lane & (2 * HEAD_PAD - 1)
    ones_cols = (pair_lane >= V_HEAD) & (pair_lane < 2 * HEAD_PAD - V_HEAD)
    v_ref[...] = jnp.where(ones_cols, 1.0, kv[:, HP:]).astype(BF16)

    cos_t = cos_ref[...]
    sin_t = sin_ref[...]
    gqn = gqn_ref[...]
    gkn = gkn_ref[...]
    cq = gqn * cos_t
    sq = gqr_ref[...] * sin_t
    kb = kr * (gkn * cos_t) + kr_rot * (gkr_ref[...] * sin_t)
    inv_w = 1.0 / QK_HEAD
    qscale = QK_HEAD ** -0.5 * LOG2_E
    for hh in range(N_HEADS):
        sl = slice(hh * HEAD_PAD, (hh + 1) * HEAD_PAD)
        qh = qq[:, sl]
        rq = lax.rsqrt(jnp.sum(qh * qh, axis=-1, keepdims=True) * inv_w + EPS) * qscale
        q_ref[:, sl] = ((qh * cq + qq[:, HP + hh * HEAD_PAD:HP + (hh + 1) * HEAD_PAD] * sq) * rq).astype(BF16)
        kraw = kv[:, sl] + kr
        rk = lax.rsqrt(jnp.sum(kraw * kraw, axis=-1, keepdims=True) * inv_w + EPS)
        k_ref[:, sl] = ((kv[:, sl] * gkn + kb) * rk).astype(BF16)


def _inproj(x2, cos_t, sin_t, mod3, g_mix, w_in_p, g_q_lat, w_uq_p, g_kv_lat, w_ukv_p,
            gqn_p, gqr_p, gkn_p, gkr_p, conv_w, conv_b, wa_d, b_a, wx_d, b_x, lam, B, S, tm):
    T, D = x2.shape
    ns = S // tm
    HP = N_HEADS * HEAD_PAD
    C = D_LRU
    row = lambda b, s: (b * ns + s, 0)
    full = lambda b, s: (0, 0)
    return pl.pallas_call(
        _inproj_kernel,
        grid=(B, ns),
        in_specs=[
            pl.BlockSpec((tm, D), row),
            pl.BlockSpec((tm, LANES), row),
            pl.BlockSpec((tm, LANES), row),
            pl.BlockSpec((1, 1, D), lambda b, s: (b * 6 + 0, 0, 0)),
            pl.BlockSpec((1, 1, D), lambda b, s: (b * 6 + 1, 0, 0)),
            pl.BlockSpec((1, D), full),
            pl.BlockSpec((D, D_IN_PAD), full),
            pl.BlockSpec((1, Q_LORA), full),
            pl.BlockSpec((Q_LORA, 2 * HP), full),
            pl.BlockSpec((1, KV_LORA), full),
            pl.BlockSpec((KV_LORA, 2 * HP), full),
            pl.BlockSpec((1, HEAD_PAD), full),
            pl.BlockSpec((1, HEAD_PAD), full),
            pl.BlockSpec((1, HEAD_PAD), full),
            pl.BlockSpec((1, HEAD_PAD), full),
            pl.BlockSpec((CONV_W, C), full),
            pl.BlockSpec((1, C), full),
            pl.BlockSpec((C, C), full),
            pl.BlockSpec((1, C), full),
            pl.BlockSpec((C, C), full),
            pl.BlockSpec((1, C), full),
            pl.BlockSpec((1, C), full),
        ],
        out_specs=[
            pl.BlockSpec((tm, C), row),
            pl.BlockSpec((tm, HP), row),
            pl.BlockSpec((tm, HP), row),
            pl.BlockSpec((tm, HP), row),
        ],
        out_shape=[
            jax.ShapeDtypeStruct((T, C), BF16),
            jax.ShapeDtypeStruct((T, HP), BF16),
            jax.ShapeDtypeStruct((T, HP), BF16),
            jax.ShapeDtypeStruct((T, HP), BF16),
        ],
        scratch_shapes=[pltpu.VMEM((SUBLANES, C), F32), pltpu.VMEM((SUBLANES, C), F32)],
        compiler_params=_cparams(("arbitrary", "arbitrary")),
        name="inproj",
    )(x2, cos_t, sin_t, mod3, mod3, g_mix, w_in_p, g_q_lat, w_uq_p, g_kv_lat, w_ukv_p,
      gqn_p, gqr_p, gkn_p, gkr_p, conv_w, conv_b, wa_d, b_a, wx_d, b_x, lam)


def _gelu_tanh(x):
    c = 0.7978845608028654
    hx = 0.5 * x
    return hx + hx * jnp.tanh(x * (c + (c * 0.044715) * (x * x)))


def _lru_tile(x, y, cw_ref, cb_ref, wa_ref, ba_ref, wx_ref, bx_ref, lam_ref, tail_ref, carry_ref):
    ts = x.shape[0]
    xext = jnp.concatenate([tail_ref[...], x], axis=0)
    cw = cw_ref[...]
    xc = x * cw[CONV_W - 1:CONV_W, :]
    for j in range(CONV_W - 1):
        sh = CONV_W - 1 - j
        xc = xc + xext[8 - sh:8 - sh + ts, :] * cw[j:j + 1, :]
    xc = xc + cb_ref[...]
    tail_ref[...] = x[ts - 8:, :]

    xb = xc.astype(BF16)
    r = _sigmoid(_dot(xb, wa_ref[...]) + ba_ref[...])
    i = _sigmoid(_dot(xb, wx_ref[...]) + bx_ref[...])
    lam = lam_ref[...]
    nl = -lam
    softplus = jnp.maximum(nl, 0.0) + jnp.log(1.0 + jnp.exp(-jnp.abs(nl)))
    log_a = (-LRU_C) * r * softplus
    a = jnp.exp(log_a)
    mult = jnp.sqrt(1.0 - a * a)
    u = mult * (i * xc)

    C = a.shape[1]
    a = a.reshape(ts // SUBLANES, SUBLANES, C)
    u = u.reshape(ts // SUBLANES, SUBLANES, C)
    sub = lax.broadcasted_iota(jnp.int32, (1, SUBLANES, 1), 1)
    sh = 1
    while sh < SUBLANES:
        a_prev = pltpu.roll(a, sh, axis=1)
        u_prev = pltpu.roll(u, sh, axis=1)
        m = sub >= sh
        u = jnp.where(m, a * u_prev + u, u)
        a = jnp.where(m, a * a_prev, a)
        sh *= 2
    a = a.reshape(ts, C)
    u = u.reshape(ts, C)
    h = carry_ref[0:1, :]
    groups = []
    for g0 in range(0, ts, SUBLANES):
        hg = u[g0:g0 + SUBLANES, :] + a[g0:g0 + SUBLANES, :] * h
        groups.append(hg)
        h = hg[SUBLANES - 1:SUBLANES, :]
    carry_ref[...] = jnp.broadcast_to(h, carry_ref.shape)
    hs = jnp.concatenate(groups, axis=0)
    return (_gelu_tanh(y) * hs).astype(BF16)


NEG_INF = -1e30


def _attn_kernel(q_ref, k_ref, v_ref, o_ref, *state, tq, tk):
    m_refs = state[:N_HEADS]
    acc_refs = state[N_HEADS:]
    qi = pl.program_id(1)
    causal = (lax.broadcasted_iota(jnp.int32, (tk, tk), 1)
              <= lax.broadcasted_iota(jnp.int32, (tk, tk), 0))
    lower = slice(tk, tq)

    def head_slice(hh):
        return slice(hh * HEAD_PAD, (hh + 1) * HEAD_PAD)

    def weights(sc, m_b):
        cols = [jnp.exp2(sc[:, c0:c0 + LANES] - m_b) for c0 in range(0, tk, LANES)]
        return jnp.concatenate(cols, axis=1).astype(BF16)

    def scores(hh, r0, rows):
        hs = head_slice(hh)
        return _dot_nt(q_ref[rows, hs], k_ref[pl.ds(r0, tk), hs])

    def update(hh, rows, sc, r0):
        hs = head_slice(hh)
        m_b = m_refs[hh][rows, :]
        m_new = jnp.maximum(m_b, jnp.max(sc, axis=-1, keepdims=True))
        alpha = jnp.exp2(m_b - m_new)
        m_refs[hh][rows, :] = m_new
        acc_refs[hh][rows, :] = (alpha * acc_refs[hh][rows, :]
                                 + _dot(weights(sc, m_new), v_ref[pl.ds(r0, tk), hs]))


    every = slice(0, tq)
    r_d0 = pl.multiple_of(qi * tq, tk)
    sc_next = scores(0, r_d0, every)
    for hh in range(N_HEADS):
        hs = head_slice(hh)
        sc = jnp.concatenate([jnp.where(causal, sc_next[:tk], NEG_INF), sc_next[tk:]], axis=0)
        if hh + 1 < N_HEADS:
            sc_next = scores(hh + 1, r_d0, every)
        m_b = jnp.broadcast_to(jnp.max(sc, axis=-1, keepdims=True), (tq, LANES))
        m_refs[hh][...] = m_b
        acc_refs[hh][...] = _dot(weights(sc, m_b), v_ref[pl.ds(r_d0, tk), hs])

    r_d1 = pl.multiple_of(qi * tq + tk, tk)
    sc_next = scores(0, r_d1, lower)
    for hh in range(N_HEADS):
        sc = jnp.where(causal, sc_next, NEG_INF)
        if hh + 1 < N_HEADS:
            sc_next = scores(hh + 1, r_d1, lower)
        update(hh, lower, sc, r_d1)

    @pl.loop(0, qi * (tq // tk))
    def _(j):
        r0 = pl.multiple_of(j * tk, tk)
        sc_next = scores(0, r0, every)
        for hh in range(N_HEADS):
            sc = sc_next
            if hh + 1 < N_HEADS:
                sc_next = scores(hh + 1, r0, every)
            update(hh, every, sc, r0)

    low = lax.broadcasted_iota(jnp.int32, (tq, HEAD_PAD), 1) < V_HEAD
    for he in range(0, N_HEADS, 2):
        acc_e = acc_refs[he][...]
        acc_o = acc_refs[he + 1][...]
        num = jnp.where(low, acc_e, acc_o)
        den = pltpu.roll(jnp.where(low, acc_o, acc_e), V_HEAD, axis=1)
        o_ref[:, he * V_HEAD:(he + 2) * V_HEAD] = (num / den).astype(BF16)


def _attn(qp, kp, v, B, S, tk):
    T = qp.shape[0]
    tq = 2 * tk
    nq = S // tq
    HP = N_HEADS * HEAD_PAD
    HV = N_HEADS * V_HEAD
    return pl.pallas_call(
        functools.partial(_attn_kernel, tq=tq, tk=tk),
        grid=(B, nq),
        in_specs=[
            pl.BlockSpec((tq, HP), lambda b, i: (b * nq + i, 0)),
            pl.BlockSpec((S, HP), lambda b, i: (b, 0)),
            pl.BlockSpec((S, HP), lambda b, i: (b, 0)),
        ],
        out_specs=pl.BlockSpec((tq, HV), lambda b, i: (b * nq + i, 0)),
        out_shape=jax.ShapeDtypeStruct((T, HV), BF16),
        scratch_shapes=([pltpu.VMEM((tq, LANES), F32)] * N_HEADS
                        + [pltpu.VMEM((tq, HEAD_PAD), F32)] * N_HEADS),
        compiler_params=_cparams(("arbitrary", "arbitrary")),
        name="attn",
    )(qp, kp, v)


def _outproj_kernel(lru_ref, att_ref, x_ref, gate_ref, shift_ref, scale_ref, gffn_ref,
                    wo1_ref, wo2_ref, wr_ref, br_ref, tri_ref,
                    x1_ref, h2p_ref, idx_ref, gat_ref, rank_ref, cnt_ref, run_ref):
    first = (pl.program_id(0) == 0) & (pl.program_id(1) == 0)

    @pl.when(first)
    def _():
        run_ref[...] = jnp.zeros_like(run_ref)

    mix = _dot(lru_ref[...], wo1_ref[...]) + _dot(att_ref[...], wo2_ref[...])
    x1 = x_ref[...] + gate_ref[0] * mix
    x1_ref[...] = x1
    ms = jnp.mean(x1 * x1, axis=-1, keepdims=True)
    gain = gffn_ref[...] * (1.0 + scale_ref[0])
    h2 = x1 * lax.rsqrt(ms + EPS) * gain + shift_ref[0]

    hhi = h2.astype(BF16)
    hlo = (h2 - hhi.astype(F32)).astype(BF16)
    h2p_ref[...] = _pack_halves(h2)

    ne = br_ref.shape[0]
    stacked = _dot_nt(wr_ref[...], hhi)
    logits = stacked[:ne] + stacked[ne:] + _dot_nt(wr_ref[:ne, :], hlo) + br_ref[...]

    tm = logits.shape[1]
    eio = lax.broadcasted_iota(jnp.int32, (ne, tm), 0)
    vals, idxs, sels = [], [], []
    l = logits
    for _ in range(TOP_K):
        m = jnp.max(l, axis=0, keepdims=True)
        idx = jnp.min(jnp.where(l == m, eio, ne), axis=0, keepdims=True)
        sel = eio == idx
        l = jnp.where(sel, -jnp.inf, l)
        vals.append(m)
        idxs.append(idx)
        sels.append(sel)
    es = [jnp.exp(v - vals[0]) for v in vals]
    inv = 1.0 / (es[0] + es[1] + es[2] + es[3])
    sel_any = jnp.where(sels[0] | sels[1] | sels[2] | sels[3], 1.0, 0.0)
    run = run_ref[...]
    excl = _dot(sel_any.astype(BF16), tri_ref[...]) + run
    for kk in range(TOP_K):
        idx_ref[kk:kk + 1, :] = idxs[kk]
        gat_ref[kk:kk + 1, :] = es[kk] * inv
        rk = jnp.sum(jnp.where(sels[kk], excl, 0.0), axis=0, keepdims=True)
        rank_ref[kk:kk + 1, :] = rk.astype(jnp.int32)
    run = run + jnp.sum(sel_any, axis=1, keepdims=True)
    run_ref[...] = run
    cnt_ref[...] = run.astype(jnp.int32)


def _outproj(lru_o, att_o, x2, mod3, g_ffn, wo1, wo2, wr_stack, b_r, tri, b0, B, S, tm):
    D = x2.shape[1]
    T = B * S
    ns = S // tm
    C = lru_o.shape[1]
    row_in = lambda b, s: ((b0 + b) * ns + s, 0)
    row = lambda b, s: (b * ns + s, 0)
    col = lambda b, s: (0, b * ns + s)
    full = lambda b, s: (0, 0)
    return pl.pallas_call(
        _outproj_kernel,
        grid=(B, ns),
        in_specs=[
            pl.BlockSpec((tm, C), row_in),
            pl.BlockSpec((tm, C), row_in),
            pl.BlockSpec((tm, D), row_in),
            pl.BlockSpec((1, 1, D), lambda b, s: ((b0 + b) * 6 + 2, 0, 0)),
            pl.BlockSpec((1, 1, D), lambda b, s: ((b0 + b) * 6 + 3, 0, 0)),
            pl.BlockSpec((1, 1, D), lambda b, s: ((b0 + b) * 6 + 4, 0, 0)),
            pl.BlockSpec((1, D), full),
            pl.BlockSpec((C, D), full),
            pl.BlockSpec((C, D), full),
            pl.BlockSpec((2 * N_EXPERTS, D), full),
            pl.BlockSpec((N_EXPERTS, 1), full),
            pl.BlockSpec(tri.shape, full),
        ],
        out_specs=[
            pl.BlockSpec((tm, D), row),
            pl.BlockSpec((tm, D // 2), row),
            pl.BlockSpec((TOP_K, tm), col),
            pl.BlockSpec((TOP_K, tm), col),
            pl.BlockSpec((TOP_K, tm), col),
            pl.BlockSpec((N_EXPERTS, 1), full),
        ],
        out_shape=[
            jax.ShapeDtypeStruct((T, D), F32),
            jax.ShapeDtypeStruct((T, D // 2), jnp.int32),
            jax.ShapeDtypeStruct((TOP_K, T), jnp.int32),
            jax.ShapeDtypeStruct((TOP_K, T), F32),
            jax.ShapeDtypeStruct((TOP_K, T), jnp.int32),
            jax.ShapeDtypeStruct((N_EXPERTS, 1), jnp.int32),
        ],
        scratch_shapes=[pltpu.VMEM((N_EXPERTS, 1), F32)],
        compiler_params=_cparams(("arbitrary", "arbitrary")),
        name="outproj",
    )(lru_o, att_o, x2, mod3, mod3, mod3, g_ffn, wo1, wo2, wr_stack, b_r, tri)


SC_CORES = 2
SC_SUBCORES = 16
SC_WORKERS = SC_CORES * SC_SUBCORES
SC_LANES = 16


def _sc_mesh():
    return plsc.VectorSubcoreMesh(core_axis_name="c", subcore_axis_name="s",
                                  num_cores=SC_CORES, num_subcores=SC_SUBCORES)


def _sc_worker_id():
    return lax.axis_index("s") * SC_CORES + lax.axis_index("c")


def _sc_scatter_rows(rows, idx, n_out, g):
    T, W = rows.shape
    K = idx.shape[0]
    per_w = T // SC_WORKERS
    nch = per_w // g
    assert per_w * SC_WORKERS == T and nch * g == per_w and nch % 2 == 0
    idx_w = idx.reshape(K, SC_WORKERS, nch, g).transpose(1, 2, 0, 3).reshape(SC_WORKERS, nch * K, g)

    def body(rows_hbm, idx_hbm, out_hbm, idx_v, buf0, buf1, semr0, semr1, semw):
        wid = _sc_worker_id()
        base = wid * per_w
        pltpu.sync_copy(idx_hbm.at[wid], idx_v)

        def read(j, buf, sem):
            return pltpu.make_async_copy(rows_hbm.at[pl.ds(base + j * g, g)], buf, sem)

        def scatter(j, buf):
            copies = [pltpu.async_copy(buf, out_hbm.at[idx_v.at[j * K + kk]], semw)
                      for kk in range(K)]
            for cp in copies:
                cp.wait()

        read(0, buf0, semr0).start()

        @pl.loop(0, nch // 2)
        def _(jj):
            j0 = 2 * jj
            read(j0 + 1, buf1, semr1).start()
            read(j0, buf0, semr0).wait()
            scatter(j0, buf0)

            @pl.when(j0 + 2 < nch)
            def _():
                read(j0 + 2, buf0, semr0).start()

            read(j0 + 1, buf1, semr1).wait()
            scatter(j0 + 1, buf1)

    return pl.kernel(
        body,
        out_type=jax.ShapeDtypeStruct((n_out, W), rows.dtype),
        mesh=_sc_mesh(),
        scratch_types=[
            pltpu.VMEM((nch * K, g), jnp.int32),
            pltpu.VMEM((g, W), rows.dtype),
            pltpu.VMEM((g, W), rows.dtype),
            pltpu.SemaphoreType.DMA,
            pltpu.SemaphoreType.DMA,
            pltpu.SemaphoreType.DMA,
        ],
        name="sc_scatter_rows",
    )(rows, idx_w)


def _sc_gather_gated_sum(table, idx, gates, g):
    W = table.shape[1]
    K, T = idx.shape
    per_w = T // SC_WORKERS
    nch = per_w // g
    assert per_w * SC_WORKERS == T and nch * g == per_w and nch % 2 == 0 and W % SC_LANES == 0
    idx_w = idx.reshape(K, SC_WORKERS, nch, g).transpose(1, 2, 0, 3).reshape(SC_WORKERS, nch * K, g)
    gates_w = gates.reshape(K, SC_WORKERS, nch, g).transpose(1, 2, 0, 3).reshape(SC_WORKERS, nch, K * g)
    def body(table_hbm, idx_hbm, gates_hbm, out_hbm, idx_v, rows_v, gts_v, out_v, sem0, sem1):
        wid = _sc_worker_id()
        base = wid * per_w
        pltpu.sync_copy(idx_hbm.at[wid], idx_v)

        def fetch(j, slot, sem):
            cps = []
            for kk in range(K):
                cps.append(pltpu.make_async_copy(table_hbm.at[idx_v.at[j * K + kk]],
                                                 rows_v.at[slot, kk], sem))
            cps.append(pltpu.make_async_copy(gates_hbm.at[wid, j], gts_v.at[slot], sem))
            return cps

        def start(j, slot, sem):
            for cp in fetch(j, slot, sem):
                cp.start()

        def finish(j, slot, sem):
            for cp in fetch(j, slot, sem):
                cp.wait()

            @pl.loop(0, g)
            def _(t):
                gk = [plsc.load_gather(gts_v.at[slot], [jnp.full((SC_LANES,), kk * g, jnp.int32) + t])
                      for kk in range(K)]

                @plsc.parallel_loop(0, W, SC_LANES, unroll=4)
                def _(off):
                    off = pl.multiple_of(off, SC_LANES)
                    acc_lo = jnp.zeros((SC_LANES,), F32)
                    acc_hi = jnp.zeros((SC_LANES,), F32)
                    for kk in range(K):
                        w = rows_v[slot, kk, t, pl.ds(off, SC_LANES)]
                        lo, hi = plsc.unpack(plsc.bitcast(w, BF16), format=plsc.PackFormat.INTERLEAVED)
                        acc_lo = acc_lo + gk[kk] * lo
                        acc_hi = acc_hi + gk[kk] * hi
                    out_v[slot, t, pl.ds(off, SC_LANES)] = acc_lo
                    out_v[slot, t, pl.ds(W + off, SC_LANES)] = acc_hi

            pltpu.sync_copy(out_v.at[slot], out_hbm.at[pl.ds(base + j * g, g)])

        start(0, 0, sem0)

        @pl.loop(0, nch // 2)
        def _(jj):
            j0 = 2 * jj
            start(j0 + 1, 1, sem1)
            finish(j0, 0, sem0)

            @pl.when(j0 + 2 < nch)
            def _():
                start(j0 + 2, 0, sem0)

            finish(j0 + 1, 1, sem1)

    return pl.kernel(
        body,
        out_type=jax.ShapeDtypeStruct((T, 2 * W), F32),
        mesh=_sc_mesh(),
        scratch_types=[
            pltpu.VMEM((nch * K, g), jnp.int32),
            pltpu.VMEM((2, K, g, W), jnp.int32),
            pltpu.VMEM((2, K * g), F32),
            pltpu.VMEM((2, g, 2 * W), F32),
            pltpu.SemaphoreType.DMA,
            pltpu.SemaphoreType.DMA,
        ],
        compiler_params=pltpu.CompilerParams(needs_layout_passes=False),
        name="sc_gather_gated_sum",
    )(table, idx_w, gates_w)


def _experts_kernel(be_ref, bv_ref, bf_ref, bn_ref, bs_ref, xs_ref, w1_hbm, b1_ref, w2_hbm, b2_ref,
                    ys_ref, w1s_ref, w2s_ref, w1b_ref, w2b_ref, sem):
    i = pl.program_id(0)
    nvalid = bv_ref[i]

    def weight_copies(e, slot):
        return (pltpu.make_async_copy(w1_hbm.at[e], w1s_ref.at[slot], sem.at[0, slot]),
                pltpu.make_async_copy(w2_hbm.at[e], w2s_ref.at[slot], sem.at[1, slot]))

    @pl.when(bf_ref[i] > 0)
    def _():
        slot = bs_ref[i]

        @pl.when(i == 0)
        def _():
            for cp in weight_copies(be_ref[0], slot):
                cp.start()

        for cp in weight_copies(be_ref[i], slot):
            cp.wait()
        w1b_ref[...] = w1s_ref[slot].astype(BF16)
        w2b_ref[...] = w2s_ref[slot].astype(BF16)

        @pl.when(bn_ref[i] >= 0)
        def _():
            for cp in weight_copies(bn_ref[i], 1 - slot):
                cp.start()

    def ffn(rows):
        xw = xs_ref[:rows, :]
        rowi = lax.broadcasted_iota(jnp.int32, (rows, 1), 0)
        lo, hi = _unpack_halves(jnp.where(rowi < nvalid, xw, 0))
        xb = jnp.concatenate([lo.astype(BF16), hi.astype(BF16)], axis=1)
        gu = _dot(xb, w1b_ref[...]) + b1_ref[0]
        glu = jnp.minimum(gu[:, :D_FF], SWIGLU_LIMIT)
        lin = jnp.clip(gu[:, D_FF:], -SWIGLU_LIMIT, SWIGLU_LIMIT)
        act = (lin + 1.0) * (glu * _sigmoid(glu, SWIGLU_ALPHA))
        ys_ref[:rows, :] = _pack_halves(_dot(act.astype(BF16), w2b_ref[...]) + b2_ref[0])

    step = xs_ref.shape[0] // ROW_PARTS
    for nq in range(1, ROW_PARTS + 1):
        pl.when((nvalid > (nq - 1) * step) & (nvalid <= nq * step))(functools.partial(ffn, nq * step))


def _experts(blk_e, blk_v, blk_f, blk_n, blk_s, blk_r, xs, w1, b1, w2, b2):
    P, W = xs.shape
    nb = P // MOE_BLOCK
    E, D, F2 = w1.shape
    grid_spec = pltpu.PrefetchScalarGridSpec(
        num_scalar_prefetch=6,
        grid=(nb,),
        in_specs=[
            pl.BlockSpec((MOE_BLOCK, W), lambda i, be, bv, bf, bn, bs, br: (br[i], 0)),
            pl.BlockSpec(memory_space=pl.ANY),
            pl.BlockSpec((1, 1, F2), lambda i, be, bv, bf, bn, bs, br: (be[i], 0, 0)),
            pl.BlockSpec(memory_space=pl.ANY),
            pl.BlockSpec((1, 1, D), lambda i, be, bv, bf, bn, bs, br: (be[i], 0, 0)),
        ],
        out_specs=pl.BlockSpec((MOE_BLOCK, D // 2), lambda i, be, bv, bf, bn, bs, br: (br[i], 0)),
        scratch_shapes=[
            pltpu.VMEM((2, D, F2), F32),
            pltpu.VMEM((2, D_FF, D), F32),
            pltpu.VMEM((D, F2), BF16),
            pltpu.VMEM((D_FF, D), BF16),
            pltpu.SemaphoreType.DMA((2, 2)),
        ],
    )

    def kern(be_ref, bv_ref, bf_ref, bn_ref, bs_ref, br_ref, *refs):
        del br_ref
        _experts_kernel(be_ref, bv_ref, bf_ref, bn_ref, bs_ref, *refs)

    return pl.pallas_call(
        kern,
        grid_spec=grid_spec,
        out_shape=jax.ShapeDtypeStruct((P, D // 2), jnp.int32),
        compiler_params=_cparams(("arbitrary",)),
        name="experts",
    )(blk_e, blk_v, blk_f, blk_n, blk_s, blk_r, xs, w1, b1.reshape(E, 1, F2), w2,
      b2.reshape(E, 1, D))


def _combine_kernel(x1_ref, gate_ref, ysum_ref, *rest):
    o_ref = rest[-1]
    o_ref[...] = x1_ref[...] + gate_ref[0] * ysum_ref[...]


def _combine(x1, mod3, ysum, out_prev, b0, B, nb_total, S, tm):
    D = x1.shape[1]
    ns = S // tm
    row = lambda b, s: (b * ns + s, 0)
    row_out = lambda b, s: ((b0 + b) * ns + s, 0)
    in_specs = [
        pl.BlockSpec((tm, D), row),
        pl.BlockSpec((1, 1, D), lambda b, s: ((b0 + b) * 6 + 5, 0, 0)),
        pl.BlockSpec((tm, D), row),
    ]
    args = [x1, mod3, ysum]
    aliases = {}
    if out_prev is not None:
        in_specs.append(pl.BlockSpec(memory_space=pl.ANY))
        args.append(out_prev)
        aliases = {len(args) - 1: 0}
    return pl.pallas_call(
        _combine_kernel,
        grid=(B, ns),
        in_specs=in_specs,
        out_specs=pl.BlockSpec((tm, D), row_out),
        out_shape=jax.ShapeDtypeStruct((nb_total * S, D), F32),
        input_output_aliases=aliases,
        compiler_params=_cparams(("arbitrary", "arbitrary")),
        name="combine",
    )(*args)


def _block_diag(w):
    n, c, d = w.shape
    eye = jnp.eye(n, dtype=w.dtype)
    return jnp.einsum("ncd,nm->ncmd", w, eye).reshape(n * c, n * d)


def _pad_heads(w, width):
    k = w.shape[0]
    w = w.reshape(k, N_HEADS, width)
    return jnp.pad(w, ((0, 0), (0, 0), (0, HEAD_PAD - width))).reshape(k, N_HEADS * HEAD_PAD)


def kernel(x, c, positions, w_ada, b_ada, g_mix, w_in, conv_w, conv_b, w_a, b_a, w_x, b_x, lam,
           g_q_lat, w_uq, g_kv_lat, w_ukv, g_qn, g_kn, w_out, g_ffn, w_router, b_router,
           w1, b1, w2, b2):
    B, S, D = x.shape
    T = B * S
    depth = w_ada.shape[0]
    tm_in = min(512, S)
    tk_att = min(512, S // 2)
    tm_out = min(1024, S)
    tm_comb = min(1024, S)
    n_groups = min(MAX_GROUPS, B)
    group_sizes = [B // n_groups + (1 if gi < B % n_groups else 0) for gi in range(n_groups)]

    o1 = 2 * D_LRU
    o2 = o1 + Q_LORA
    o3 = o2 + KV_LORA
    tri = jnp.asarray(np.arange(tm_out)[:, None] < np.arange(tm_out)[None, :], BF16)
    cos_t, sin_t = _rope_tables(positions)

    def rot_cols(w):
        k = w.shape[0]
        w3 = w.reshape(k, -1, HEAD_PAD)
        lo = w3[:, :, ROPE_LO:ROPE_LO + ROPE_HALF]
        hi = w3[:, :, ROPE_LO + ROPE_HALF:ROPE_LO + QK_ROPE]
        zl = jnp.zeros_like(w3[:, :, :ROPE_LO])
        zr = jnp.zeros_like(w3[:, :, ROPE_LO + QK_ROPE:])
        return jnp.concatenate([zl, hi, lo, zr], axis=2).reshape(w.shape)

    x2 = x.reshape(T, D)
    for l in range(depth):
        mod3 = _ada(c, w_ada[l], b_ada[l]).reshape(B * 6, 1, D)

        w_in_l = w_in[l]
        kr_cols = jnp.pad(w_in_l[:, o3:], ((0, 0), (ROPE_LO, LANES - ROPE_LO - QK_ROPE)))
        w_in_p = jnp.concatenate([w_in_l[:, :o3], kr_cols, rot_cols(kr_cols)], axis=1).astype(BF16)
        w_uq_h = _pad_heads(w_uq[l], QK_HEAD)
        w_uq_p = jnp.concatenate([w_uq_h, rot_cols(w_uq_h)], axis=1).astype(BF16)
        w_ukv_l = w_ukv[l].reshape(KV_LORA, N_HEADS, QK_NOPE + V_HEAD)
        w_uk_h = _pad_heads(w_ukv_l[:, :, :QK_NOPE].reshape(KV_LORA, N_HEADS * QK_NOPE), QK_NOPE)
        w_uv_pairs = w_ukv_l[:, :, QK_NOPE:].reshape(KV_LORA, N_HEADS // 2, 2, V_HEAD)
        zero_v = jnp.zeros_like(w_uv_pairs[:, :, 0])
        w_uv_h = jnp.stack([w_uv_pairs[:, :, 0], zero_v, zero_v, w_uv_pairs[:, :, 1]],
                           axis=2).reshape(KV_LORA, N_HEADS * HEAD_PAD)
        w_ukv_p = jnp.concatenate([w_uk_h, w_uv_h], axis=1).astype(BF16)
        gqn_p = jnp.pad(g_qn[l], (0, HEAD_PAD - QK_HEAD)).reshape(1, HEAD_PAD)
        gkn_p = jnp.pad(g_kn[l], (0, HEAD_PAD - QK_HEAD)).reshape(1, HEAD_PAD)

        lru_o, qp, kp, v = _inproj(
            x2, cos_t, sin_t, mod3, g_mix[l].reshape(1, D), w_in_p, g_q_lat[l].reshape(1, Q_LORA),
            w_uq_p, g_kv_lat[l].reshape(1, KV_LORA), w_ukv_p, gqn_p, rot_cols(gqn_p), gkn_p,
            rot_cols(gkn_p), conv_w[l], conv_b[l].reshape(1, D_LRU),
            _block_diag(w_a[l]).astype(BF16), b_a[l].reshape(1, D_LRU),
            _block_diag(w_x[l]).astype(BF16), b_x[l].reshape(1, D_LRU),
            lam[l].reshape(1, D_LRU), B, S, tm_in)

        att_o = _attn(qp, kp, v, B, S, tk_att)

        w_out_b = w_out[l].astype(BF16)
        wr_stack = jnp.concatenate(_split_bf16(w_router[l].T), axis=0)
        g_ffn_l = g_ffn[l].reshape(1, D)
        b_r = b_router[l].reshape(N_EXPERTS, 1)
        eio = np.arange(N_EXPERTS, dtype=np.int32)

        x_next = None
        b0 = 0
        for Bg in group_sizes:
            Tg = Bg * S
            n_blocks = -(-(Tg * TOP_K) // MOE_BLOCK) + N_EXPERTS
            bi = np.arange(n_blocks, dtype=np.int32)
            g_disp = min(64, Tg // SC_WORKERS // 2)
            g_comb = min(16, Tg // SC_WORKERS // 2)
            x1, h2p, idx_t, gat_t, rank_t, counts = _outproj(
                lru_o, att_o, x2, mod3, g_ffn_l, w_out_b[:D_LRU], w_out_b[D_LRU:],
                wr_stack, b_r, tri, b0, Bg, S, tm_out)

            counts = counts.reshape(N_EXPERTS)
            nblk_e = (counts + MOE_BLOCK - 1) // MOE_BLOCK
            blk_end = jnp.cumsum(nblk_e)
            pad_start = (blk_end - nblk_e) * MOE_BLOCK
            total = blk_end[-1]
            blk_r = jnp.minimum(bi, total - 1).astype(jnp.int32)
            blk_e = jnp.minimum(jnp.sum(blk_end[None, :] <= blk_r[:, None], axis=1),
                                N_EXPERTS - 1).astype(jnp.int32)
            blk_onehot = blk_e[:, None] == eio[None, :]
            blk_first = jnp.sum(jnp.where(blk_onehot, (blk_end - nblk_e)[None, :], 0), axis=1)
            blk_cnt = jnp.sum(jnp.where(blk_onehot, counts[None, :], 0), axis=1)
            blk_v = jnp.where(bi < total,
                              jnp.clip(blk_cnt - (bi - blk_first) * MOE_BLOCK, 0, MOE_BLOCK),
                              0).astype(jnp.int32)
            blk_f = ((bi == blk_first) & (bi < total)).astype(jnp.int32)
            nxt_first = jnp.sum(jnp.where(blk_onehot, blk_end[None, :], 0), axis=1)
            nxt_e = jnp.minimum(jnp.sum(blk_end[None, :] <= nxt_first[:, None], axis=1),
                                N_EXPERTS - 1)
            blk_n = jnp.where(nxt_first < total, nxt_e, -1).astype(jnp.int32)
            ordinal = jnp.cumsum((nblk_e > 0).astype(jnp.int32)) - 1
            blk_s = (jnp.sum(jnp.where(blk_onehot, ordinal[None, :], 0), axis=1) % 2).astype(jnp.int32)
            slot0 = jnp.sum(jnp.where(idx_t[None] == eio[:, None, None],
                                      pad_start[:, None, None], 0), axis=0)
            dest = slot0.astype(jnp.int32) + rank_t

            xs = _sc_scatter_rows(h2p, dest, n_blocks * MOE_BLOCK, g_disp)
            ys = _experts(blk_e, blk_v, blk_f, blk_n, blk_s, blk_r, xs, w1[l], b1[l], w2[l], b2[l])
            ysum = _sc_gather_gated_sum(ys, dest, gat_t, g_comb)
            x_next = _combine(x1, mod3, ysum, x_next, b0, Bg, B, S, tm_comb)
            b0 += Bg
        x2 = x_next
    return x2.reshape(B, S, D)
```

```python
import functools

import jax
import jax.numpy as jnp
import numpy as np
from jax import lax
from jax.experimental import pallas as pl
from jax.experimental.pallas import tpu as pltpu
from jax.experimental.pallas import tpu_sc as plsc

D_MODEL = 1024
D_LRU = 512
LRU_BLOCKS = 8
LRU_BD = 64
CONV_W = 4
LRU_C = 8.0
N_HEADS = 8
QK_NOPE = 64
QK_ROPE = 32
QK_HEAD = 96
V_HEAD = 64
Q_LORA = 256
KV_LORA = 128
ROPE_THETA = 10000.0
N_EXPERTS = 32
TOP_K = 4
D_FF = 1024
SWIGLU_LIMIT = 7.0
SWIGLU_ALPHA = 1.702
MOE_BLOCK = 1024
EPS = 1e-6

LANES = 128
SUBLANES = 8
HEAD_PAD = 128
ROPE_LO = QK_NOPE
ROPE_HALF = QK_ROPE // 2
TOK_PER_ROW = LANES // ROPE_HALF
D_IN_PAD = 2 * D_LRU + Q_LORA + KV_LORA + 2 * LANES
LOG2_E = 1.4426950408889634
MAX_GROUPS = 2
ROW_PARTS = 8

VMEM_LIMIT = 56 * 1024 * 1024

F32 = jnp.float32
BF16 = jnp.bfloat16


def _cparams(sem):
    return pltpu.CompilerParams(dimension_semantics=sem, vmem_limit_bytes=VMEM_LIMIT)


def _dot(a, b):
    return jnp.dot(a, b, preferred_element_type=F32)


def _dot_nt(a, b):
    return lax.dot_general(a, b, (((1,), (1,)), ((), ())), preferred_element_type=F32)


def _split_bf16(a):
    hi = a.astype(BF16)
    lo = (a - hi.astype(F32)).astype(BF16)
    return hi, lo


def _sigmoid(x, scale=1.0):
    return 1.0 / (1.0 + jnp.exp2(x * (-scale * LOG2_E)))


def _pack_halves(x):
    bits = lax.bitcast_convert_type(x.astype(BF16).astype(F32), jnp.uint32)
    half = x.shape[1] // 2
    words = (bits[:, :half] >> 16) | (bits[:, half:] & jnp.uint32(0xFFFF0000))
    return lax.bitcast_convert_type(words, jnp.int32)


def _unpack_halves(words):
    w = lax.bitcast_convert_type(words, jnp.uint32)
    lo = lax.bitcast_convert_type(w << 16, F32)
    hi = lax.bitcast_convert_type(w & jnp.uint32(0xFFFF0000), F32)
    return lo, hi


def _ada_kernel(c_ref, w_ref, b_ref, o_ref):
    c = c_ref[...]
    s = c * _sigmoid(c)
    shi, slo = _split_bf16(s)
    whi, wlo = _split_bf16(w_ref[...])
    o_ref[...] = _dot(shi, whi) + _dot(slo, whi) + _dot(shi, wlo) + b_ref[...]


def _ada(c, w_ada, b_ada):
    B, D = c.shape
    N = w_ada.shape[1]
    tn = 1024
    return pl.pallas_call(
        _ada_kernel,
        grid=(N // tn,),
        in_specs=[
            pl.BlockSpec((B, D), lambda j: (0, 0)),
            pl.BlockSpec((D, tn), lambda j: (0, j)),
            pl.BlockSpec((1, tn), lambda j: (0, j)),
        ],
        out_specs=pl.BlockSpec((B, tn), lambda j: (0, j)),
        out_shape=jax.ShapeDtypeStruct((B, N), F32),
        compiler_params=_cparams(("arbitrary",)),
        name="ada",
    )(c, w_ada, b_ada.reshape(1, N))


def _trig_kernel(pos_ref, freq_ref, rsel_ref, fold_ref, cbase_ref, cos_ref, sin_ref):
    ang = pos_ref[...].astype(F32) * freq_ref[...]
    cs = jnp.concatenate([jnp.cos(ang), jnp.sin(ang)], axis=1)
    tm = cos_ref.shape[0]
    row = lax.broadcasted_iota(jnp.int32, (tm, 2 * LANES), 0)
    lane = lax.broadcasted_iota(jnp.int32, (tm, 2 * LANES), 1)
    own = ((lane % LANES) // ROPE_HALF) == (row % TOK_PER_ROW)
    rsel = rsel_ref[...]
    fold = fold_ref[...]
    by_row = sum(_dot(rsel, part) for part in _split_bf16(cs))
    mine = jnp.where(own, by_row, 0.0)
    out = sum(_dot(part, fold) for part in _split_bf16(mine))
    cos_ref[...] = out[:, :LANES] + cbase_ref[...]
    sin_ref[...] = out[:, LANES:]


def _rope_tables(positions):
    T = positions.size
    rows = T // TOK_PER_ROW
    pos_c = jnp.repeat(positions.reshape(T).astype(jnp.int32), ROPE_HALF).reshape(rows, LANES)
    tm = min(2048, T)
    tr = tm // TOK_PER_ROW
    freqs = np.float32(ROPE_THETA) ** (-np.arange(ROPE_HALF, dtype=np.float32) / np.float32(ROPE_HALF))
    freq_c = np.tile(freqs.astype(np.float32), TOK_PER_ROW).reshape(1, LANES)
    rsel = jnp.asarray(np.arange(tm)[:, None] // TOK_PER_ROW == np.arange(tr)[None, :], BF16)
    src = np.arange(LANES)[:, None] % ROPE_HALF
    dst = np.arange(LANES)[None, :]
    first = dst == ROPE_LO + src
    second = dst == ROPE_LO + ROPE_HALF + src
    fcos = (first | second).astype(np.float32)
    fsin = second.astype(np.float32) - first.astype(np.float32)
    zero = np.zeros((LANES, LANES), np.float32)
    fold = jnp.asarray(np.block([[fcos, zero], [zero, fsin]]), BF16)
    lane = np.arange(LANES)
    cbase = ((lane < ROPE_LO) | (lane >= ROPE_LO + QK_ROPE)).astype(np.float32).reshape(1, LANES)
    full = lambda i: (0, 0)
    return pl.pallas_call(
        _trig_kernel,
        grid=(T // tm,),
        in_specs=[
            pl.BlockSpec((tr, LANES), lambda i: (i, 0)),
            pl.BlockSpec((1, LANES), full),
            pl.BlockSpec((tm, tr), full),
            pl.BlockSpec((2 * LANES, 2 * LANES), full),
            pl.BlockSpec((1, LANES), full),
        ],
        out_specs=[pl.BlockSpec((tm, LANES), lambda i: (i, 0))] * 2,
        out_shape=[jax.ShapeDtypeStruct((T, LANES), F32)] * 2,
        compiler_params=_cparams(("arbitrary",)),
        name="rope_trig",
    )(pos_c, freq_c, rsel, fold, cbase)


def _inproj_kernel(x_ref, cos_ref, sin_ref, shift_ref, scale_ref, gmix_ref, win_ref, gq_ref, wuq_ref,
                   gkv_ref, wukv_ref, gqn_ref, gqr_ref, gkn_ref, gkr_ref,
                   cw_ref, cb_ref, wa_ref, ba_ref, wx_ref, bx_ref, lam_ref,
                   lru_ref, q_ref, k_ref, v_ref, tail_ref, carry_ref):
    HP = N_HEADS * HEAD_PAD

    @pl.when(pl.program_id(1) == 0)
    def _():
        tail_ref[...] = jnp.zeros_like(tail_ref)
        carry_ref[...] = jnp.zeros_like(carry_ref)

    x = x_ref[...]
    ms = jnp.mean(x * x, axis=-1, keepdims=True)
    gain = gmix_ref[...] * (1.0 + scale_ref[0])
    h = x * lax.rsqrt(ms + EPS) * gain + shift_ref[0]
    hb = h.astype(BF16)
    o1 = 2 * D_LRU
    o2 = Q_LORA
    o3 = o2 + KV_LORA
    z_lru = _dot(hb, win_ref[:, :o1])
    z = _dot(hb, win_ref[:, o1:])
    lru_ref[...] = _lru_tile(z_lru[:, :D_LRU], z_lru[:, D_LRU:], cw_ref, cb_ref, wa_ref, ba_ref,
                             wx_ref, bx_ref, lam_ref, tail_ref, carry_ref)
    ql = z[:, :o2]
    kvl = z[:, o2:o3]
    kr = z[:, o3:o3 + LANES]
    kr_rot = z[:, o3 + LANES:]

    qn = ql * lax.rsqrt(jnp.mean(ql * ql, axis=-1, keepdims=True) + EPS) * gq_ref[...]
    qq = _dot(qn.astype(BF16), wuq_ref[...])
    kvn = kvl * lax.rsqrt(jnp.mean(kvl * kvl, axis=-1, keepdims=True) + EPS) * gkv_ref[...]
    kv = _dot(kvn.astype(BF16), wukv_ref[...])

    tm = x.shape[0]
    lane = lax.broadcasted_iota(jnp.int32, (tm, HP), 1)
    pair_lane = lane & (2 * HEAD_PAD - 1)
    ones_cols = (pair_lane >= V_HEAD) & (pair_lane < 2 * HEAD_PAD - V_HEAD)
    v_ref[...] = jnp.where(ones_cols, 1.0, kv[:, HP:]).astype(BF16)

    cos_t = cos_ref[...]
    sin_t = sin_ref[...]
    gqn = gqn_ref[...]
    gkn = gkn_ref[...]
    cq = gqn * cos_t
    sq = gqr_ref[...] * sin_t
    kb = kr * (gkn * cos_t) + kr_rot * (gkr_ref[...] * sin_t)
    inv_w = 1.0 / QK_HEAD
    qscale = QK_HEAD ** -0.5 * LOG2_E
    for hh in range(N_HEADS):
        sl = slice(hh * HEAD_PAD, (hh + 1) * HEAD_PAD)
        qh = qq[:, sl]
        rq = lax.rsqrt(jnp.sum(qh * qh, axis=-1, keepdims=True) * inv_w + EPS) * qscale
        q_ref[:, sl] = ((qh * cq + qq[:, HP + hh * HEAD_PAD:HP + (hh + 1) * HEAD_PAD] * sq) * rq).astype(BF16)
        kraw = kv[:, sl] + kr
        rk = lax.rsqrt(jnp.sum(kraw * kraw, axis=-1, keepdims=True) * inv_w + EPS)
        k_ref[:, sl] = ((kv[:, sl] * gkn + kb) * rk).astype(BF16)


def _inproj(x2, cos_t, sin_t, mod3, g_mix, w_in_p, g_q_lat, w_uq_p, g_kv_lat, w_ukv_p,
            gqn_p, gqr_p, gkn_p, gkr_p, conv_w, conv_b, wa_d, b_a, wx_d, b_x, lam, B, S, tm):
    T, D = x2.shape
    ns = S // tm
    HP = N_HEADS * HEAD_PAD
    C = D_LRU
    row = lambda b, s: (b * ns + s, 0)
    full = lambda b, s: (0, 0)
    return pl.pallas_call(
        _inproj_kernel,
        grid=(B, ns),
        in_specs=[
            pl.BlockSpec((tm, D), row),
            pl.BlockSpec((tm, LANES), row),
            pl.BlockSpec((tm, LANES), row),
            pl.BlockSpec((1, 1, D), lambda b, s: (b * 6 + 0, 0, 0)),
            pl.BlockSpec((1, 1, D), lambda b, s: (b * 6 + 1, 0, 0)),
            pl.BlockSpec((1, D), full),
            pl.BlockSpec((D, D_IN_PAD), full),
            pl.BlockSpec((1, Q_LORA), full),
            pl.BlockSpec((Q_LORA, 2 * HP), full),
            pl.BlockSpec((1, KV_LORA), full),
            pl.BlockSpec((KV_LORA, 2 * HP), full),
            pl.BlockSpec((1, HEAD_PAD), full),
            pl.BlockSpec((1, HEAD_PAD), full),
            pl.BlockSpec((1, HEAD_PAD), full),
            pl.BlockSpec((1, HEAD_PAD), full),
            pl.BlockSpec((CONV_W, C), full),
            pl.BlockSpec((1, C), full),
            pl.BlockSpec((C, C), full),
            pl.BlockSpec((1, C), full),
            pl.BlockSpec((C, C), full),
            pl.BlockSpec((1, C), full),
            pl.BlockSpec((1, C), full),
        ],
        out_specs=[
            pl.BlockSpec((tm, C), row),
            pl.BlockSpec((tm, HP), row),
            pl.BlockSpec((tm, HP), row),
            pl.BlockSpec((tm, HP), row),
        ],
        out_shape=[
            jax.ShapeDtypeStruct((T, C), BF16),
            jax.ShapeDtypeStruct((T, HP), BF16),
            jax.ShapeDtypeStruct((T, HP), BF16),
            jax.ShapeDtypeStruct((T, HP), BF16),
        ],
        scratch_shapes=[pltpu.VMEM((SUBLANES, C), F32), pltpu.VMEM((SUBLANES, C), F32)],
        compiler_params=_cparams(("arbitrary", "arbitrary")),
        name="inproj",
    )(x2, cos_t, sin_t, mod3, mod3, g_mix, w_in_p, g_q_lat, w_uq_p, g_kv_lat, w_ukv_p,
      gqn_p, gqr_p, gkn_p, gkr_p, conv_w, conv_b, wa_d, b_a, wx_d, b_x, lam)


def _gelu_tanh(x):
    c = 0.7978845608028654
    hx = 0.5 * x
    return hx + hx * jnp.tanh(x * (c + (c * 0.044715) * (x * x)))


def _lru_tile(x, y, cw_ref, cb_ref, wa_ref, ba_ref, wx_ref, bx_ref, lam_ref, tail_ref, carry_ref):
    ts = x.shape[0]
    xext = jnp.concatenate([tail_ref[...], x], axis=0)
    cw = cw_ref[...]
    xc = x * cw[CONV_W - 1:CONV_W, :]
    for j in range(CONV_W - 1):
        sh = CONV_W - 1 - j
        xc = xc + xext[8 - sh:8 - sh + ts, :] * cw[j:j + 1, :]
    xc = xc + cb_ref[...]
    tail_ref[...] = x[ts - 8:, :]

    xb = xc.astype(BF16)
    r = _sigmoid(_dot(xb, wa_ref[...]) + ba_ref[...])
    i = _sigmoid(_dot(xb, wx_ref[...]) + bx_ref[...])
    lam = lam_ref[...]
    nl = -lam
    softplus = jnp.maximum(nl, 0.0) + jnp.log(1.0 + jnp.exp(-jnp.abs(nl)))
    log_a = (-LRU_C) * r * softplus
    a = jnp.exp(log_a)
    mult = jnp.sqrt(1.0 - a * a)
    u = mult * (i * xc)

    C = a.shape[1]
    a = a.reshape(ts // SUBLANES, SUBLANES, C)
    u = u.reshape(ts // SUBLANES, SUBLANES, C)
    sub = lax.broadcasted_iota(jnp.int32, (1, SUBLANES, 1), 1)
    sh = 1
    while sh < SUBLANES:
        a_prev = pltpu.roll(a, sh, axis=1)
        u_prev = pltpu.roll(u, sh, axis=1)
        m = sub >= sh
        u = jnp.where(m, a * u_prev + u, u)
        a = jnp.where(m, a * a_prev, a)
        sh *= 2
    a = a.reshape(ts, C)
    u = u.reshape(ts, C)
    h = carry_ref[0:1, :]
    groups = []
    for g0 in range(0, ts, SUBLANES):
        hg = u[g0:g0 + SUBLANES, :] + a[g0:g0 + SUBLANES, :] * h
        groups.append(hg)
        h = hg[SUBLANES - 1:SUBLANES, :]
    carry_ref[...] = jnp.broadcast_to(h, carry_ref.shape)
    hs = jnp.concatenate(groups, axis=0)
    return (_gelu_tanh(y) * hs).astype(BF16)


NEG_INF = -1e30


def _attn_kernel(q_ref, k_ref, v_ref, o_ref, *state, tq, tk):
    m_refs = state[:N_HEADS]
    acc_refs = state[N_HEADS:]
    qi = pl.program_id(1)
    causal = (lax.broadcasted_iota(jnp.int32, (tk, tk), 1)
              <= lax.broadcasted_iota(jnp.int32, (tk, tk), 0))
    lower = slice(tk, tq)

    def head_slice(hh):
        return slice(hh * HEAD_PAD, (hh + 1) * HEAD_PAD)

    def weights(sc, m_b):
        cols = [jnp.exp2(sc[:, c0:c0 + LANES] - m_b) for c0 in range(0, tk, LANES)]
        return jnp.concatenate(cols, axis=1).astype(BF16)

    def scores(hh, r0, rows):
        hs = head_slice(hh)
        return _dot_nt(q_ref[rows, hs], k_ref[pl.ds(r0, tk), hs])

    def update(hh, rows, sc, r0):
        hs = head_slice(hh)
        m_b = m_refs[hh][rows, :]
        m_new = jnp.maximum(m_b, jnp.max(sc, axis=-1, keepdims=True))
        alpha = jnp.exp2(m_b - m_new)
        m_refs[hh][rows, :] = m_new
        acc_refs[hh][rows, :] = (alpha * acc_refs[hh][rows, :]
                                 + _dot(weights(sc, m_new), v_ref[pl.ds(r0, tk), hs]))


    every = slice(0, tq)
    r_d0 = pl.multiple_of(qi * tq, tk)
    sc_next = scores(0, r_d0, every)
    for hh in range(N_HEADS):
        hs = head_slice(hh)
        sc = jnp.concatenate([jnp.where(causal, sc_next[:tk], NEG_INF), sc_next[tk:]], axis=0)
        if hh + 1 < N_HEADS:
            sc_next = scores(hh + 1, r_d0, every)
        m_b = jnp.broadcast_to(jnp.max(sc, axis=-1, keepdims=True), (tq, LANES))
        m_refs[hh][...] = m_b
        acc_refs[hh][...] = _dot(weights(sc, m_b), v_ref[pl.ds(r_d0, tk), hs])

    r_d1 = pl.multiple_of(qi * tq + tk, tk)
    sc_next = scores(0, r_d1, lower)
    for hh in range(N_HEADS):
        sc = jnp.where(causal, sc_next, NEG_INF)
        if hh + 1 < N_HEADS:
            sc_next = scores(hh + 1, r_d1, lower)
        update(hh, lower, sc, r_d1)

    @pl.loop(0, qi * (tq // tk))
    def _(j):
        r0 = pl.multiple_of(j * tk, tk)
        sc_next = scores(0, r0, every)
        for hh in range(N_HEADS):
            sc = sc_next
            if hh + 1 < N_HEADS:
                sc_next = scores(hh + 1, r0, every)
            update(hh, every, sc, r0)

    low = lax.broadcasted_iota(jnp.int32, (tq, HEAD_PAD), 1) < V_HEAD
    for he in range(0, N_HEADS, 2):
        acc_e = acc_refs[he][...]
        acc_o = acc_refs[he + 1][...]
        num = jnp.where(low, acc_e, acc_o)
        den = pltpu.roll(jnp.where(low, acc_o, acc_e), V_HEAD, axis=1)
        o_ref[:, he * V_HEAD:(he + 2) * V_HEAD] = (num / den).astype(BF16)


def _attn(qp, kp, v, B, S, tk):
    T = qp.shape[0]
    tq = 2 * tk
    nq = S // tq
    HP = N_HEADS * HEAD_PAD
    HV = N_HEADS * V_HEAD
    return pl.pallas_call(
        functools.partial(_attn_kernel, tq=tq, tk=tk),
        grid=(B, nq),
        in_specs=[
            pl.BlockSpec((tq, HP), lambda b, i: (b * nq + i, 0)),
            pl.BlockSpec((S, HP), lambda b, i: (b, 0)),
            pl.BlockSpec((S, HP), lambda b, i: (b, 0)),
        ],
        out_specs=pl.BlockSpec((tq, HV), lambda b, i: (b * nq + i, 0)),
        out_shape=jax.ShapeDtypeStruct((T, HV), BF16),
        scratch_shapes=([pltpu.VMEM((tq, LANES), F32)] * N_HEADS
                        + [pltpu.VMEM((tq, HEAD_PAD), F32)] * N_HEADS),
        compiler_params=_cparams(("arbitrary", "arbitrary")),
        name="attn",
    )(qp, kp, v)


def _outproj_kernel(lru_ref, att_ref, x_ref, gate_ref, shift_ref, scale_ref, gffn_ref,
                    wo1_ref, wo2_ref, wr_ref, br_ref, tri_ref,
                    x1_ref, h2p_ref, idx_ref, gat_ref, rank_ref, cnt_ref, run_ref):
    first = (pl.program_id(0) == 0) & (pl.program_id(1) == 0)

    @pl.when(first)
    def _():
        run_ref[...] = jnp.zeros_like(run_ref)

    mix = _dot(lru_ref[...], wo1_ref[...]) + _dot(att_ref[...], wo2_ref[...])
    x1 = x_ref[...] + gate_ref[0] * mix
    x1_ref[...] = x1
    ms = jnp.mean(x1 * x1, axis=-1, keepdims=True)
    gain = gffn_ref[...] * (1.0 + scale_ref[0])
    h2 = x1 * lax.rsqrt(ms + EPS) * gain + shift_ref[0]

    hhi = h2.astype(BF16)
    hlo = (h2 - hhi.astype(F32)).astype(BF16)
    h2p_ref[...] = _pack_halves(h2)

    ne = br_ref.shape[0]
    stacked = _dot_nt(wr_ref[...], hhi)
    logits = stacked[:ne] + stacked[ne:] + _dot_nt(wr_ref[:ne, :], hlo) + br_ref[...]

    tm = logits.shape[1]
    eio = lax.broadcasted_iota(jnp.int32, (ne, tm), 0)
    vals, idxs, sels = [], [], []
    l = logits
    for _ in range(TOP_K):
        m = jnp.max(l, axis=0, keepdims=True)
        idx = jnp.min(jnp.where(l == m, eio, ne), axis=0, keepdims=True)
        sel = eio == idx
        l = jnp.where(sel, -jnp.inf, l)
        vals.append(m)
        idxs.append(idx)
        sels.append(sel)
    es = [jnp.exp(v - vals[0]) for v in vals]
    inv = 1.0 / (es[0] + es[1] + es[2] + es[3])
    sel_any = jnp.where(sels[0] | sels[1] | sels[2] | sels[3], 1.0, 0.0)
    run = run_ref[...]
    excl = _dot(sel_any.astype(BF16), tri_ref[...]) + run
    for kk in range(TOP_K):
        idx_ref[kk:kk + 1, :] = idxs[kk]
        gat_ref[kk:kk + 1, :] = es[kk] * inv
        rk = jnp.sum(jnp.where(sels[kk], excl, 0.0), axis=0, keepdims=True)
        rank_ref[kk:kk + 1, :] = rk.astype(jnp.int32)
    run = run + jnp.sum(sel_any, axis=1, keepdims=True)
    run_ref[...] = run
    cnt_ref[...] = run.astype(jnp.int32)


def _outproj(lru_o, att_o, x2, mod3, g_ffn, wo1, wo2, wr_stack, b_r, tri, b0, B, S, tm):
    D = x2.shape[1]
    T = B * S
    ns = S // tm
    C = lru_o.shape[1]
    row_in = lambda b, s: ((b0 + b) * ns + s, 0)
    row = lambda b, s: (b * ns + s, 0)
    col = lambda b, s: (0, b * ns + s)
    full = lambda b, s: (0, 0)
    return pl.pallas_call(
        _outproj_kernel,
        grid=(B, ns),
        in_specs=[
            pl.BlockSpec((tm, C), row_in),
            pl.BlockSpec((tm, C), row_in),
            pl.BlockSpec((tm, D), row_in),
            pl.BlockSpec((1, 1, D), lambda b, s: ((b0 + b) * 6 + 2, 0, 0)),
            pl.BlockSpec((1, 1, D), lambda b, s: ((b0 + b) * 6 + 3, 0, 0)),
            pl.BlockSpec((1, 1, D), lambda b, s: ((b0 + b) * 6 + 4, 0, 0)),
            pl.BlockSpec((1, D), full),
            pl.BlockSpec((C, D), full),
            pl.BlockSpec((C, D), full),
            pl.BlockSpec((2 * N_EXPERTS, D), full),
            pl.BlockSpec((N_EXPERTS, 1), full),
            pl.BlockSpec(tri.shape, full),
        ],
        out_specs=[
            pl.BlockSpec((tm, D), row),
            pl.BlockSpec((tm, D // 2), row),
            pl.BlockSpec((TOP_K, tm), col),
            pl.BlockSpec((TOP_K, tm), col),
            pl.BlockSpec((TOP_K, tm), col),
            pl.BlockSpec((N_EXPERTS, 1), full),
        ],
        out_shape=[
            jax.ShapeDtypeStruct((T, D), F32),
            jax.ShapeDtypeStruct((T, D // 2), jnp.int32),
            jax.ShapeDtypeStruct((TOP_K, T), jnp.int32),
            jax.ShapeDtypeStruct((TOP_K, T), F32),
            jax.ShapeDtypeStruct((TOP_K, T), jnp.int32),
            jax.ShapeDtypeStruct((N_EXPERTS, 1), jnp.int32),
        ],
        scratch_shapes=[pltpu.VMEM((N_EXPERTS, 1), F32)],
        compiler_params=_cparams(("arbitrary", "arbitrary")),
        name="outproj",
    )(lru_o, att_o, x2, mod3, mod3, mod3, g_ffn, wo1, wo2, wr_stack, b_r, tri)


SC_CORES = 2
SC_SUBCORES = 16
SC_WORKERS = SC_CORES * SC_SUBCORES
SC_LANES = 16


def _sc_mesh():
    return plsc.VectorSubcoreMesh(core_axis_name="c", subcore_axis_name="s",
                                  num_cores=SC_CORES, num_subcores=SC_SUBCORES)


def _sc_worker_id():
    return lax.axis_index("s") * SC_CORES + lax.axis_index("c")


def _sc_scatter_rows(rows, idx, n_out, g):
    T, W = rows.shape
    K = idx.shape[0]
    per_w = T // SC_WORKERS
    nch = per_w // g
    assert per_w * SC_WORKERS == T and nch * g == per_w and nch % 2 == 0
    idx_w = idx.reshape(K, SC_WORKERS, nch, g).transpose(1, 2, 0, 3).reshape(SC_WORKERS, nch * K, g)

    def body(rows_hbm, idx_hbm, out_hbm, idx_v, buf0, buf1, semr0, semr1, semw):
        wid = _sc_worker_id()
        base = wid * per_w
        pltpu.sync_copy(idx_hbm.at[wid], idx_v)

        def read(j, buf, sem):
            return pltpu.make_async_copy(rows_hbm.at[pl.ds(base + j * g, g)], buf, sem)

        def scatter(j, buf):
            copies = [pltpu.async_copy(buf, out_hbm.at[idx_v.at[j * K + kk]], semw)
                      for kk in range(K)]
            for cp in copies:
                cp.wait()

        read(0, buf0, semr0).start()

        @pl.loop(0, nch // 2)
        def _(jj):
            j0 = 2 * jj
            read(j0 + 1, buf1, semr1).start()
            read(j0, buf0, semr0).wait()
            scatter(j0, buf0)

            @pl.when(j0 + 2 < nch)
            def _():
                read(j0 + 2, buf0, semr0).start()

            read(j0 + 1, buf1, semr1).wait()
            scatter(j0 + 1, buf1)

    return pl.kernel(
        body,
        out_type=jax.ShapeDtypeStruct((n_out, W), rows.dtype),
        mesh=_sc_mesh(),
        scratch_types=[
            pltpu.VMEM((nch * K, g), jnp.int32),
            pltpu.VMEM((g, W), rows.dtype),
            pltpu.VMEM((g, W), rows.dtype),
            pltpu.SemaphoreType.DMA,
            pltpu.SemaphoreType.DMA,
            pltpu.SemaphoreType.DMA,
        ],
        name="sc_scatter_rows",
    )(rows, idx_w)


def _sc_gather_gated_sum(table, idx, gates, g):
    W = table.shape[1]
    K, T = idx.shape
    per_w = T // SC_WORKERS
    nch = per_w // g
    assert per_w * SC_WORKERS == T and nch * g == per_w and nch % 2 == 0 and W % SC_LANES == 0
    idx_w = idx.reshape(K, SC_WORKERS, nch, g).transpose(1, 2, 0, 3).reshape(SC_WORKERS, nch * K, g)
    gates_w = gates.reshape(K, SC_WORKERS, nch, g).transpose(1, 2, 0, 3).reshape(SC_WORKERS, nch, K * g)
    def body(table_hbm, idx_hbm, gates_hbm, out_hbm, idx_v, rows_v, gts_v, out_v, sem0, sem1):
        wid = _sc_worker_id()
        base = wid * per_w
        pltpu.sync_copy(idx_hbm.at[wid], idx_v)

        def fetch(j, slot, sem):
            cps = []
            for kk in range(K):
                cps.append(pltpu.make_async_copy(table_hbm.at[idx_v.at[j * K + kk]],
                                                 rows_v.at[slot, kk], sem))
            cps.append(pltpu.make_async_copy(gates_hbm.at[wid, j], gts_v.at[slot], sem))
            return cps

        def start(j, slot, sem):
            for cp in fetch(j, slot, sem):
                cp.start()

        def finish(j, slot, sem):
            for cp in fetch(j, slot, sem):
                cp.wait()

            @pl.loop(0, g)
            def _(t):
                gk = [plsc.load_gather(gts_v.at[slot], [jnp.full((SC_LANES,), kk * g, jnp.int32) + t])
                      for kk in range(K)]

                @plsc.parallel_loop(0, W, SC_LANES, unroll=4)
                def _(off):
                    off = pl.multiple_of(off, SC_LANES)
                    acc_lo = jnp.zeros((SC_LANES,), F32)
                    acc_hi = jnp.zeros((SC_LANES,), F32)
                    for kk in range(K):
                        w = rows_v[slot, kk, t, pl.ds(off, SC_LANES)]
                        lo, hi = plsc.unpack(plsc.bitcast(w, BF16), format=plsc.PackFormat.INTERLEAVED)
                        acc_lo = acc_lo + gk[kk] * lo
                        acc_hi = acc_hi + gk[kk] * hi
                    out_v[slot, t, pl.ds(off, SC_LANES)] = acc_lo
                    out_v[slot, t, pl.ds(W + off, SC_LANES)] = acc_hi

            pltpu.sync_copy(out_v.at[slot], out_hbm.at[pl.ds(base + j * g, g)])

        start(0, 0, sem0)

        @pl.loop(0, nch // 2)
        def _(jj):
            j0 = 2 * jj
            start(j0 + 1, 1, sem1)
            finish(j0, 0, sem0)

            @pl.when(j0 + 2 < nch)
            def _():
                start(j0 + 2, 0, sem0)

            finish(j0 + 1, 1, sem1)

    return pl.kernel(
        body,
        out_type=jax.ShapeDtypeStruct((T, 2 * W), F32),
        mesh=_sc_mesh(),
        scratch_types=[
            pltpu.VMEM((nch * K, g), jnp.int32),
            pltpu.VMEM((2, K, g, W), jnp.int32),
            pltpu.VMEM((2, K * g), F32),
            pltpu.VMEM((2, g, 2 * W), F32),
            pltpu.SemaphoreType.DMA,
            pltpu.SemaphoreType.DMA,
        ],
        compiler_params=pltpu.CompilerParams(needs_layout_passes=False),
        name="sc_gather_gated_sum",
    )(table, idx_w, gates_w)


def _experts_kernel(be_ref, bv_ref, bf_ref, bn_ref, bs_ref, xs_ref, w1_hbm, b1_ref, w2_hbm, b2_ref,
                    ys_ref, w1s_ref, w2s_ref, w1b_ref, w2b_ref, sem):
    i = pl.program_id(0)
    nvalid = bv_ref[i]

    def weight_copies(e, slot):
        return (pltpu.make_async_copy(w1_hbm.at[e], w1s_ref.at[slot], sem.at[0, slot]),
                pltpu.make_async_copy(w2_hbm.at[e], w2s_ref.at[slot], sem.at[1, slot]))

    @pl.when(bf_ref[i] > 0)
    def _():
        slot = bs_ref[i]

        @pl.when(i == 0)
        def _():
            for cp in weight_copies(be_ref[0], slot):
                cp.start()

        for cp in weight_copies(be_ref[i], slot):
            cp.wait()
        w1b_ref[...] = w1s_ref[slot].astype(BF16)
        w2b_ref[...] = w2s_ref[slot].astype(BF16)

        @pl.when(bn_ref[i] >= 0)
        def _():
            for cp in weight_copies(bn_ref[i], 1 - slot):
                cp.start()

    def ffn(rows):
        xw = xs_ref[:rows, :]
        rowi = lax.broadcasted_iota(jnp.int32, (rows, 1), 0)
        lo, hi = _unpack_halves(jnp.where(rowi < nvalid, xw, 0))
        xb = jnp.concatenate([lo.astype(BF16), hi.astype(BF16)], axis=1)
        gu = _dot(xb, w1b_ref[...]) + b1_ref[0]
        glu = jnp.minimum(gu[:, :D_FF], SWIGLU_LIMIT)
        lin = jnp.clip(gu[:, D_FF:], -SWIGLU_LIMIT, SWIGLU_LIMIT)
        act = (lin + 1.0) * (glu * _sigmoid(glu, SWIGLU_ALPHA))
        ys_ref[:rows, :] = _pack_halves(_dot(act.astype(BF16), w2b_ref[...]) + b2_ref[0])

    step = xs_ref.shape[0] // ROW_PARTS
    for nq in range(1, ROW_PARTS + 1):
        pl.when((nvalid > (nq - 1) * step) & (nvalid <= nq * step))(functools.partial(ffn, nq * step))


def _experts(blk_e, blk_v, blk_f, blk_n, blk_s, blk_r, xs, w1, b1, w2, b2):
    P, W = xs.shape
    nb = P // MOE_BLOCK
    E, D, F2 = w1.shape
    grid_spec = pltpu.PrefetchScalarGridSpec(
        num_scalar_prefetch=6,
        grid=(nb,),
        in_specs=[
            pl.BlockSpec((MOE_BLOCK, W), lambda i, be, bv, bf, bn, bs, br: (br[i], 0)),
            pl.BlockSpec(memory_space=pl.ANY),
            pl.BlockSpec((1, 1, F2), lambda i, be, bv, bf, bn, bs, br: (be[i], 0, 0)),
            pl.BlockSpec(memory_space=pl.ANY),
            pl.BlockSpec((1, 1, D), lambda i, be, bv, bf, bn, bs, br: (be[i], 0, 0)),
        ],
        out_specs=pl.BlockSpec((MOE_BLOCK, D // 2), lambda i, be, bv, bf, bn, bs, br: (br[i], 0)),
        scratch_shapes=[
            pltpu.VMEM((2, D, F2), F32),
            pltpu.VMEM((2, D_FF, D), F32),
            pltpu.VMEM((D, F2), BF16),
            pltpu.VMEM((D_FF, D), BF16),
            pltpu.SemaphoreType.DMA((2, 2)),
        ],
    )

    def kern(be_ref, bv_ref, bf_ref, bn_ref, bs_ref, br_ref, *refs):
        del br_ref
        _experts_kernel(be_ref, bv_ref, bf_ref, bn_ref, bs_ref, *refs)

    return pl.pallas_call(
        kern,
        grid_spec=grid_spec,
        out_shape=jax.ShapeDtypeStruct((P, D // 2), jnp.int32),
        compiler_params=_cparams(("arbitrary",)),
        name="experts",
    )(blk_e, blk_v, blk_f, blk_n, blk_s, blk_r, xs, w1, b1.reshape(E, 1, F2), w2,
      b2.reshape(E, 1, D))


def _combine_kernel(x1_ref, gate_ref, ysum_ref, *rest):
    o_ref = rest[-1]
    o_ref[...] = x1_ref[...] + gate_ref[0] * ysum_ref[...]


def _combine(x1, mod3, ysum, out_prev, b0, B, nb_total, S, tm):
    D = x1.shape[1]
    ns = S // tm
    row = lambda b, s: (b * ns + s, 0)
    row_out = lambda b, s: ((b0 + b) * ns + s, 0)
    in_specs = [
        pl.BlockSpec((tm, D), row),
        pl.BlockSpec((1, 1, D), lambda b, s: ((b0 + b) * 6 + 5, 0, 0)),
        pl.BlockSpec((tm, D), row),
    ]
    args = [x1, mod3, ysum]
    aliases = {}
    if out_prev is not None:
        in_specs.append(pl.BlockSpec(memory_space=pl.ANY))
        args.append(out_prev)
        aliases = {len(args) - 1: 0}
    return pl.pallas_call(
        _combine_kernel,
        grid=(B, ns),
        in_specs=in_specs,
        out_specs=pl.BlockSpec((tm, D), row_out),
        out_shape=jax.ShapeDtypeStruct((nb_total * S, D), F32),
        input_output_aliases=aliases,
        compiler_params=_cparams(("arbitrary", "arbitrary")),
        name="combine",
    )(*args)


def _block_diag(w):
    n, c, d = w.shape
    eye = jnp.eye(n, dtype=w.dtype)
    return jnp.einsum("ncd,nm->ncmd", w, eye).reshape(n * c, n * d)


def _pad_heads(w, width):
    k = w.shape[0]
    w = w.reshape(k, N_HEADS, width)
    return jnp.pad(w, ((0, 0), (0, 0), (0, HEAD_PAD - width))).reshape(k, N_HEADS * HEAD_PAD)


def kernel(x, c, positions, w_ada, b_ada, g_mix, w_in, conv_w, conv_b, w_a, b_a, w_x, b_x, lam,
           g_q_lat, w_uq, g_kv_lat, w_ukv, g_qn, g_kn, w_out, g_ffn, w_router, b_router,
           w1, b1, w2, b2):
    B, S, D = x.shape
    T = B * S
    depth = w_ada.shape[0]
    tm_in = min(512, S)
    tk_att = min(512, S // 2)
    tm_out = min(1024, S)
    tm_comb = min(1024, S)
    n_groups = min(MAX_GROUPS, B)
    group_sizes = [B // n_groups + (1 if gi < B % n_groups else 0) for gi in range(n_groups)]

    o1 = 2 * D_LRU
    o2 = o1 + Q_LORA
    o3 = o2 + KV_LORA
    tri = jnp.asarray(np.arange(tm_out)[:, None] < np.arange(tm_out)[None, :], BF16)
    cos_t, sin_t = _rope_tables(positions)

    def rot_cols(w):
        k = w.shape[0]
        w3 = w.reshape(k, -1, HEAD_PAD)
        lo = w3[:, :, ROPE_LO:ROPE_LO + ROPE_HALF]
        hi = w3[:, :, ROPE_LO + ROPE_HALF:ROPE_LO + QK_ROPE]
        zl = jnp.zeros_like(w3[:, :, :ROPE_LO])
        zr = jnp.zeros_like(w3[:, :, ROPE_LO + QK_ROPE:])
        return jnp.concatenate([zl, hi, lo, zr], axis=2).reshape(w.shape)

    x2 = x.reshape(T, D)
    for l in range(depth):
        mod3 = _ada(c, w_ada[l], b_ada[l]).reshape(B * 6, 1, D)

        w_in_l = w_in[l]
        kr_cols = jnp.pad(w_in_l[:, o3:], ((0, 0), (ROPE_LO, LANES - ROPE_LO - QK_ROPE)))
        w_in_p = jnp.concatenate([w_in_l[:, :o3], kr_cols, rot_cols(kr_cols)], axis=1).astype(BF16)
        w_uq_h = _pad_heads(w_uq[l], QK_HEAD)
        w_uq_p = jnp.concatenate([w_uq_h, rot_cols(w_uq_h)], axis=1).astype(BF16)
        w_ukv_l = w_ukv[l].reshape(KV_LORA, N_HEADS, QK_NOPE + V_HEAD)
        w_uk_h = _pad_heads(w_ukv_l[:, :, :QK_NOPE].reshape(KV_LORA, N_HEADS * QK_NOPE), QK_NOPE)
        w_uv_pairs = w_ukv_l[:, :, QK_NOPE:].reshape(KV_LORA, N_HEADS // 2, 2, V_HEAD)
        zero_v = jnp.zeros_like(w_uv_pairs[:, :, 0])
        w_uv_h = jnp.stack([w_uv_pairs[:, :, 0], zero_v, zero_v, w_uv_pairs[:, :, 1]],
                           axis=2).reshape(KV_LORA, N_HEADS * HEAD_PAD)
        w_ukv_p = jnp.concatenate([w_uk_h, w_uv_h], axis=1).astype(BF16)
        gqn_p = jnp.pad(g_qn[l], (0, HEAD_PAD - QK_HEAD)).reshape(1, HEAD_PAD)
        gkn_p = jnp.pad(g_kn[l], (0, HEAD_PAD - QK_HEAD)).reshape(1, HEAD_PAD)

        lru_o, qp, kp, v = _inproj(
            x2, cos_t, sin_t, mod3, g_mix[l].reshape(1, D), w_in_p, g_q_lat[l].reshape(1, Q_LORA),
            w_uq_p, g_kv_lat[l].reshape(1, KV_LORA), w_ukv_p, gqn_p, rot_cols(gqn_p), gkn_p,
            rot_cols(gkn_p), conv_w[l], conv_b[l].reshape(1, D_LRU),
            _block_diag(w_a[l]).astype(BF16), b_a[l].reshape(1, D_LRU),
            _block_diag(w_x[l]).astype(BF16), b_x[l].reshape(1, D_LRU),
            lam[l].reshape(1, D_LRU), B, S, tm_in)

        att_o = _attn(qp, kp, v, B, S, tk_att)

        w_out_b = w_out[l].astype(BF16)
        wr_stack = jnp.concatenate(_split_bf16(w_router[l].T), axis=0)
        g_ffn_l = g_ffn[l].reshape(1, D)
        b_r = b_router[l].reshape(N_EXPERTS, 1)
        eio = np.arange(N_EXPERTS, dtype=np.int32)

        x_next = None
        b0 = 0
        for Bg in group_sizes:
            Tg = Bg * S
            n_blocks = -(-(Tg * TOP_K) // MOE_BLOCK) + N_EXPERTS
            bi = np.arange(n_blocks, dtype=np.int32)
            g_disp = min(64, Tg // SC_WORKERS // 2)
            g_comb = min(16, Tg // SC_WORKERS // 2)
            x1, h2p, idx_t, gat_t, rank_t, counts = _outproj(
                lru_o, att_o, x2, mod3, g_ffn_l, w_out_b[:D_LRU], w_out_b[D_LRU:],
                wr_stack, b_r, tri, b0, Bg, S, tm_out)

            counts = counts.reshape(N_EXPERTS)
            nblk_e = (counts + MOE_BLOCK - 1) // MOE_BLOCK
            blk_end = jnp.cumsum(nblk_e)
            pad_start = (blk_end - nblk_e) * MOE_BLOCK
            total = blk_end[-1]
            blk_r = jnp.minimum(bi, total - 1).astype(jnp.int32)
            blk_e = jnp.minimum(jnp.sum(blk_end[None, :] <= blk_r[:, None], axis=1),
                                N_EXPERTS - 1).astype(jnp.int32)
            blk_onehot = blk_e[:, None] == eio[None, :]
            blk_first = jnp.sum(jnp.where(blk_onehot, (blk_end - nblk_e)[None, :], 0), axis=1)
            blk_cnt = jnp.sum(jnp.where(blk_onehot, counts[None, :], 0), axis=1)
            blk_v = jnp.where(bi < total,
                              jnp.clip(blk_cnt - (bi - blk_first) * MOE_BLOCK, 0, MOE_BLOCK),
                              0).astype(jnp.int32)
            blk_f = ((bi == blk_first) & (bi < total)).astype(jnp.int32)
            nxt_first = jnp.sum(jnp.where(blk_onehot, blk_end[None, :], 0), axis=1)
            nxt_e = jnp.minimum(jnp.sum(blk_end[None, :] <= nxt_first[:, None], axis=1),
                                N_EXPERTS - 1)
            blk_n = jnp.where(nxt_first < total, nxt_e, -1).astype(jnp.int32)
            ordinal = jnp.cumsum((nblk_e > 0).astype(jnp.int32)) - 1
            blk_s = (jnp.sum(jnp.where(blk_onehot, ordinal[None, :], 0), axis=1) % 2).astype(jnp.int32)
            slot0 = jnp.sum(jnp.where(idx_t[None] == eio[:, None, None],
                                      pad_start[:, None, None], 0), axis=0)
            dest = slot0.astype(jnp.int32) + rank_t

            xs = _sc_scatter_rows(h2p, dest, n_blocks * MOE_BLOCK, g_disp)
            ys = _experts(blk_e, blk_v, blk_f, blk_n, blk_s, blk_r, xs, w1[l], b1[l], w2[l], b2[l])
            ysum = _sc_gather_gated_sum(ys, dest, gat_t, g_comb)
            x_next = _combine(x1, mod3, ysum, x_next, b0, Bg, B, S, tm_comb)
            b0 += Bg
        x2 = x_next
    return x2.reshape(B, S, D)
```

```python
import functools

import jax
import jax.numpy as jnp
import numpy as np
from jax import lax
from jax.experimental import pallas as pl
from jax.experimental.pallas import tpu as pltpu
from jax.experimental.pallas import tpu_sc as plsc

D_MODEL = 1024
D_LRU = 512
LRU_BLOCKS = 8
LRU_BD = 64
CONV_W = 4
LRU_C = 8.0
N_HEADS = 8
QK_NOPE = 64
QK_ROPE = 32
QK_HEAD = 96
V_HEAD = 64
Q_LORA = 256
KV_LORA = 128
ROPE_THETA = 10000.0
N_EXPERTS = 32
TOP_K = 4
D_FF = 1024
SWIGLU_LIMIT = 7.0
SWIGLU_ALPHA = 1.702
MOE_BLOCK = 1024
EPS = 1e-6

LANES = 128
SUBLANES = 8
HEAD_PAD = 128
ROPE_LO = QK_NOPE
ROPE_HALF = QK_ROPE // 2
TOK_PER_ROW = LANES // ROPE_HALF
D_IN_PAD = 2 * D_LRU + Q_LORA + KV_LORA + 2 * LANES
LOG2_E = 1.4426950408889634
MAX_GROUPS = 2
ROW_PARTS = 8

VMEM_LIMIT = 56 * 1024 * 1024

F32 = jnp.float32
BF16 = jnp.bfloat16


def _cparams(sem):
    return pltpu.CompilerParams(dimension_semantics=sem, vmem_limit_bytes=VMEM_LIMIT)


def _dot(a, b):
    return jnp.dot(a, b, preferred_element_type=F32)


def _dot_nt(a, b):
    return lax.dot_general(a, b, (((1,), (1,)), ((), ())), preferred_element_type=F32)


def _split_bf16(a):
    hi = a.astype(BF16)
    lo = (a - hi.astype(F32)).astype(BF16)
    return hi, lo


def _sigmoid(x, scale=1.0):
    return 1.0 / (1.0 + jnp.exp2(x * (-scale * LOG2_E)))


def _pack_halves(x):
    bits = lax.bitcast_convert_type(x.astype(BF16).astype(F32), jnp.uint32)
    half = x.shape[1] // 2
    words = (bits[:, :half] >> 16) | (bits[:, half:] & jnp.uint32(0xFFFF0000))
    return lax.bitcast_convert_type(words, jnp.int32)


def _unpack_halves(words):
    w = lax.bitcast_convert_type(words, jnp.uint32)
    lo = lax.bitcast_convert_type(w << 16, F32)
    hi = lax.bitcast_convert_type(w & jnp.uint32(0xFFFF0000), F32)
    return lo, hi


def _ada_kernel(c_ref, w_ref, b_ref, o_ref):
    c = c_ref[...]
    s = c * _sigmoid(c)
    shi, slo = _split_bf16(s)
    whi, wlo = _split_bf16(w_ref[...])
    o_ref[...] = _dot(shi, whi) + _dot(slo, whi) + _dot(shi, wlo) + b_ref[...]


def _ada(c, w_ada, b_ada):
    B, D = c.shape
    N = w_ada.shape[1]
    tn = 1024
    return pl.pallas_call(
        _ada_kernel,
        grid=(N // tn,),
        in_specs=[
            pl.BlockSpec((B, D), lambda j: (0, 0)),
            pl.BlockSpec((D, tn), lambda j: (0, j)),
            pl.BlockSpec((1, tn), lambda j: (0, j)),
        ],
        out_specs=pl.BlockSpec((B, tn), lambda j: (0, j)),
        out_shape=jax.ShapeDtypeStruct((B, N), F32),
        compiler_params=_cparams(("arbitrary",)),
        name="ada",
    )(c, w_ada, b_ada.reshape(1, N))


def _trig_kernel(pos_ref, freq_ref, rsel_ref, fold_ref, cbase_ref, cos_ref, sin_ref):
    ang = pos_ref[...].astype(F32) * freq_ref[...]
    cs = jnp.concatenate([jnp.cos(ang), jnp.sin(ang)], axis=1)
    tm = cos_ref.shape[0]
    row = lax.broadcasted_iota(jnp.int32, (tm, 2 * LANES), 0)
    lane = lax.broadcasted_iota(jnp.int32, (tm, 2 * LANES), 1)
    own = ((lane % LANES) // ROPE_HALF) == (row % TOK_PER_ROW)
    rsel = rsel_ref[...]
    fold = fold_ref[...]
    by_row = sum(_dot(rsel, part) for part in _split_bf16(cs))
    mine = jnp.where(own, by_row, 0.0)
    out = sum(_dot(part, fold) for part in _split_bf16(mine))
    cos_ref[...] = out[:, :LANES] + cbase_ref[...]
    sin_ref[...] = out[:, LANES:]


def _rope_tables(positions):
    T = positions.size
    rows = T // TOK_PER_ROW
    pos_c = jnp.repeat(positions.reshape(T).astype(jnp.int32), ROPE_HALF).reshape(rows, LANES)
    tm = min(2048, T)
    tr = tm // TOK_PER_ROW
    freqs = np.float32(ROPE_THETA) ** (-np.arange(ROPE_HALF, dtype=np.float32) / np.float32(ROPE_HALF))
    freq_c = np.tile(freqs.astype(np.float32), TOK_PER_ROW).reshape(1, LANES)
    rsel = jnp.asarray(np.arange(tm)[:, None] // TOK_PER_ROW == np.arange(tr)[None, :], BF16)
    src = np.arange(LANES)[:, None] % ROPE_HALF
    dst = np.arange(LANES)[None, :]
    first = dst == ROPE_LO + src
    second = dst == ROPE_LO + ROPE_HALF + src
    fcos = (first | second).astype(np.float32)
    fsin = second.astype(np.float32) - first.astype(np.float32)
    zero = np.zeros((LANES, LANES), np.float32)
    fold = jnp.asarray(np.block([[fcos, zero], [zero, fsin]]), BF16)
    lane = np.arange(LANES)
    cbase = ((lane < ROPE_LO) | (lane >= ROPE_LO + QK_ROPE)).astype(np.float32).reshape(1, LANES)
    full = lambda i: (0, 0)
    return pl.pallas_call(
        _trig_kernel,
        grid=(T // tm,),
        in_specs=[
            pl.BlockSpec((tr, LANES), lambda i: (i, 0)),
            pl.BlockSpec((1, LANES), full),
            pl.BlockSpec((tm, tr), full),
            pl.BlockSpec((2 * LANES, 2 * LANES), full),
            pl.BlockSpec((1, LANES), full),
        ],
        out_specs=[pl.BlockSpec((tm, LANES), lambda i: (i, 0))] * 2,
        out_shape=[jax.ShapeDtypeStruct((T, LANES), F32)] * 2,
        compiler_params=_cparams(("arbitrary",)),
        name="rope_trig",
    )(pos_c, freq_c, rsel, fold, cbase)


def _inproj_kernel(x_ref, cos_ref, sin_ref, shift_ref, scale_ref, gmix_ref, win_ref, gq_ref, wuq_ref,
                   gkv_ref, wukv_ref, gqn_ref, gqr_ref, gkn_ref, gkr_ref,
                   cw_ref, cb_ref, wa_ref, ba_ref, wx_ref, bx_ref, lam_ref,
                   lru_ref, q_ref, k_ref, v_ref, tail_ref, carry_ref):
    HP = N_HEADS * HEAD_PAD

    @pl.when(pl.program_id(1) == 0)
    def _():
        tail_ref[...] = jnp.zeros_like(tail_ref)
        carry_ref[...] = jnp.zeros_like(carry_ref)

    x = x_ref[...]
    ms = jnp.mean(x * x, axis=-1, keepdims=True)
    gain = gmix_ref[...] * (1.0 + scale_ref[0])
    h = x * lax.rsqrt(ms + EPS) * gain + shift_ref[0]
    hb = h.astype(BF16)
    o1 = 2 * D_LRU
    o2 = Q_LORA
    o3 = o2 + KV_LORA
    z_lru = _dot(hb, win_ref[:, :o1])
    z = _dot(hb, win_ref[:, o1:])
    lru_ref[...] = _lru_tile(z_lru[:, :D_LRU], z_lru[:, D_LRU:], cw_ref, cb_ref, wa_ref, ba_ref,
                             wx_ref, bx_ref, lam_ref, tail_ref, carry_ref)
    ql = z[:, :o2]
    kvl = z[:, o2:o3]
    kr = z[:, o3:o3 + LANES]
    kr_rot = z[:, o3 + LANES:]

    qn = ql * lax.rsqrt(jnp.mean(ql * ql, axis=-1, keepdims=True) + EPS) * gq_ref[...]
    qq = _dot(qn.astype(BF16), wuq_ref[...])
    kvn = kvl * lax.rsqrt(jnp.mean(kvl * kvl, axis=-1, keepdims=True) + EPS) * gkv_ref[...]
    kv = _dot(kvn.astype(BF16), wukv_ref[...])

    tm = x.shape[0]
    lane = lax.broadcasted_iota(jnp.int32, (tm, HP), 1)
    pair_lane = lane & (2 * HEAD_PAD - 1)
    ones_cols = (pair_lane >= V_HEAD) & (pair_lane < 2 * HEAD_PAD - V_HEAD)
    v_ref[...] = jnp.where(ones_cols, 1.0, kv[:, HP:]).astype(BF16)

    cos_t = cos_ref[...]
    sin_t = sin_ref[...]
    gqn = gqn_ref[...]
    gkn = gkn_ref[...]
    cq = gqn * cos_t
    sq = gqr_ref[...] * sin_t
    kb = kr * (gkn * cos_t) + kr_rot * (gkr_ref[...] * sin_t)
    inv_w = 1.0 / QK_HEAD
    qscale = QK_HEAD ** -0.5 * LOG2_E
    for hh in range(N_HEADS):
        sl = slice(hh * HEAD_PAD, (hh + 1) * HEAD_PAD)
        qh = qq[:, sl]
        rq = lax.rsqrt(jnp.sum(qh * qh, axis=-1, keepdims=True) * inv_w + EPS) * qscale
        q_ref[:, sl] = ((qh * cq + qq[:, HP + hh * HEAD_PAD:HP + (hh + 1) * HEAD_PAD] * sq) * rq).astype(BF16)
        kraw = kv[:, sl] + kr
        rk = lax.rsqrt(jnp.sum(kraw * kraw, axis=-1, keepdims=True) * inv_w + EPS)
        k_ref[:, sl] = ((kv[:, sl] * gkn + kb) * rk).astype(BF16)


def _inproj(x2, cos_t, sin_t, mod3, g_mix, w_in_p, g_q_lat, w_uq_p, g_kv_lat, w_ukv_p,
            gqn_p, gqr_p, gkn_p, gkr_p, conv_w, conv_b, wa_d, b_a, wx_d, b_x, lam, B, S, tm):
    T, D = x2.shape
    ns = S // tm
    HP = N_HEADS * HEAD_PAD
    C = D_LRU
    row = lambda b, s: (b * ns + s, 0)
    full = lambda b, s: (0, 0)
    return pl.pallas_call(
        _inproj_kernel,
        grid=(B, ns),
        in_specs=[
            pl.BlockSpec((tm, D), row),
            pl.BlockSpec((tm, LANES), row),
            pl.BlockSpec((tm, LANES), row),
            pl.BlockSpec((1, 1, D), lambda b, s: (b * 6 + 0, 0, 0)),
            pl.BlockSpec((1, 1, D), lambda b, s: (b * 6 + 1, 0, 0)),
            pl.BlockSpec((1, D), full),
            pl.BlockSpec((D, D_IN_PAD), full),
            pl.BlockSpec((1, Q_LORA), full),
            pl.BlockSpec((Q_LORA, 2 * HP), full),
            pl.BlockSpec((1, KV_LORA), full),
            pl.BlockSpec((KV_LORA, 2 * HP), full),
            pl.BlockSpec((1, HEAD_PAD), full),
            pl.BlockSpec((1, HEAD_PAD), full),
            pl.BlockSpec((1, HEAD_PAD), full),
            pl.BlockSpec((1, HEAD_PAD), full),
            pl.BlockSpec((CONV_W, C), full),
            pl.BlockSpec((1, C), full),
            pl.BlockSpec((C, C), full),
            pl.BlockSpec((1, C), full),
            pl.BlockSpec((C, C), full),
            pl.BlockSpec((1, C), full),
            pl.BlockSpec((1, C), full),
        ],
        out_specs=[
            pl.BlockSpec((tm, C), row),
            pl.BlockSpec((tm, HP), row),
            pl.BlockSpec((tm, HP), row),
            pl.BlockSpec((tm, HP), row),
        ],
        out_shape=[
            jax.ShapeDtypeStruct((T, C), BF16),
            jax.ShapeDtypeStruct((T, HP), BF16),
            jax.ShapeDtypeStruct((T, HP), BF16),
            jax.ShapeDtypeStruct((T, HP), BF16),
        ],
        scratch_shapes=[pltpu.VMEM((SUBLANES, C), F32), pltpu.VMEM((SUBLANES, C), F32)],
        compiler_params=_cparams(("arbitrary", "arbitrary")),
        name="inproj",
    )(x2, cos_t, sin_t, mod3, mod3, g_mix, w_in_p, g_q_lat, w_uq_p, g_kv_lat, w_ukv_p,
      gqn_p, gqr_p, gkn_p, gkr_p, conv_w, conv_b, wa_d, b_a, wx_d, b_x, lam)


def _gelu_tanh(x):
    c = 0.7978845608028654
    hx = 0.5 * x
    return hx + hx * jnp.tanh(x * (c + (c * 0.044715) * (x * x)))


def _lru_tile(x, y, cw_ref, cb_ref, wa_ref, ba_ref, wx_ref, bx_ref, lam_ref, tail_ref, carry_ref):
    ts = x.shape[0]
    xext = jnp.concatenate([tail_ref[...], x], axis=0)
    cw = cw_ref[...]
    xc = x * cw[CONV_W - 1:CONV_W, :]
    for j in range(CONV_W - 1):
        sh = CONV_W - 1 - j
        xc = xc + xext[8 - sh:8 - sh + ts, :] * cw[j:j + 1, :]
    xc = xc + cb_ref[...]
    tail_ref[...] = x[ts - 8:, :]

    xb = xc.astype(BF16)
    r = _sigmoid(_dot(xb, wa_ref[...]) + ba_ref[...])
    i = _sigmoid(_dot(xb, wx_ref[...]) + bx_ref[...])
    lam = lam_ref[...]
    nl = -lam
    softplus = jnp.maximum(nl, 0.0) + jnp.log(1.0 + jnp.exp(-jnp.abs(nl)))
    log_a = (-LRU_C) * r * softplus
    a = jnp.exp(log_a)
    mult = jnp.sqrt(1.0 - a * a)
    u = mult * (i * xc)

    C = a.shape[1]
    a = a.reshape(ts // SUBLANES, SUBLANES, C)
    u = u.reshape(ts // SUBLANES, SUBLANES, C)
    sub = lax.broadcasted_iota(jnp.int32, (1, SUBLANES, 1), 1)
    sh = 1
    while sh < SUBLANES:
        a_prev = pltpu.roll(a, sh, axis=1)
        u_prev = pltpu.roll(u, sh, axis=1)
        m = sub >= sh
        u = jnp.where(m, a * u_prev + u, u)
        a = jnp.where(m, a * a_prev, a)
        sh *= 2
    a = a.reshape(ts, C)
    u = u.reshape(ts, C)
    h = carry_ref[0:1, :]
    groups = []
    for g0 in range(0, ts, SUBLANES):
        hg = u[g0:g0 + SUBLANES, :] + a[g0:g0 + SUBLANES, :] * h
        groups.append(hg)
        h = hg[SUBLANES - 1:SUBLANES, :]
    carry_ref[...] = jnp.broadcast_to(h, carry_ref.shape)
    hs = jnp.concatenate(groups, axis=0)
    return (_gelu_tanh(y) * hs).astype(BF16)


NEG_INF = -1e30


def _attn_kernel(q_ref, k_ref, v_ref, o_ref, *state, tq, tk):
    m_refs = state[:N_HEADS]
    acc_refs = state[N_HEADS:]
    qi = pl.program_id(1)
    causal = (lax.broadcasted_iota(jnp.int32, (tk, tk), 1)
              <= lax.broadcasted_iota(jnp.int32, (tk, tk), 0))
    lower = slice(tk, tq)

    def head_slice(hh):
        return slice(hh * HEAD_PAD, (hh + 1) * HEAD_PAD)

    def weights(sc, m_b):
        cols = [jnp.exp2(sc[:, c0:c0 + LANES] - m_b) for c0 in range(0, tk, LANES)]
        return jnp.concatenate(cols, axis=1).astype(BF16)

    def scores(hh, r0, rows):
        hs = head_slice(hh)
        return _dot_nt(q_ref[rows, hs], k_ref[pl.ds(r0, tk), hs])

    def update(hh, rows, sc, r0):
        hs = head_slice(hh)
        m_b = m_refs[hh][rows, :]
        m_new = jnp.maximum(m_b, jnp.max(sc, axis=-1, keepdims=True))
        alpha = jnp.exp2(m_b - m_new)
        m_refs[hh][rows, :] = m_new
        acc_refs[hh][rows, :] = (alpha * acc_refs[hh][rows, :]
                                 + _dot(weights(sc, m_new), v_ref[pl.ds(r0, tk), hs]))


    every = slice(0, tq)
    r_d0 = pl.multiple_of(qi * tq, tk)
    sc_next = scores(0, r_d0, every)
    for hh in range(N_HEADS):
        hs = head_slice(hh)
        sc = jnp.concatenate([jnp.where(causal, sc_next[:tk], NEG_INF), sc_next[tk:]], axis=0)
        if hh + 1 < N_HEADS:
            sc_next = scores(hh + 1, r_d0, every)
        m_b = jnp.broadcast_to(jnp.max(sc, axis=-1, keepdims=True), (tq, LANES))
        m_refs[hh][...] = m_b
        acc_refs[hh][...] = _dot(weights(sc, m_b), v_ref[pl.ds(r_d0, tk), hs])

    r_d1 = pl.multiple_of(qi * tq + tk, tk)
    sc_next = scores(0, r_d1, lower)
    for hh in range(N_HEADS):
        sc = jnp.where(causal, sc_next, NEG_INF)
        if hh + 1 < N_HEADS:
            sc_next = scores(hh + 1, r_d1, lower)
        update(hh, lower, sc, r_d1)

    @pl.loop(0, qi * (tq // tk))
    def _(j):
        r0 = pl.multiple_of(j * tk, tk)
        sc_next = scores(0, r0, every)
        for hh in range(N_HEADS):
            sc = sc_next
            if hh + 1 < N_HEADS:
                sc_next = scores(hh + 1, r0, every)
            update(hh, every, sc, r0)

    low = lax.broadcasted_iota(jnp.int32, (tq, HEAD_PAD), 1) < V_HEAD
    for he in range(0, N_HEADS, 2):
        acc_e = acc_refs[he][...]
        acc_o = acc_refs[he + 1][...]
        num = jnp.where(low, acc_e, acc_o)
        den = pltpu.roll(jnp.where(low, acc_o, acc_e), V_HEAD, axis=1)
        o_ref[:, he * V_HEAD:(he + 2) * V_HEAD] = (num / den).astype(BF16)


def _attn(qp, kp, v, B, S, tk):
    T = qp.shape[0]
    tq = 2 * tk
    nq = S // tq
    HP = N_HEADS * HEAD_PAD
    HV = N_HEADS * V_HEAD
    return pl.pallas_call(
        functools.partial(_attn_kernel, tq=tq, tk=tk),
        grid=(B, nq),
        in_specs=[
            pl.BlockSpec((tq, HP), lambda b, i: (b * nq + i, 0)),
            pl.BlockSpec((S, HP), lambda b, i: (b, 0)),
            pl.BlockSpec((S, HP), lambda b, i: (b, 0)),
        ],
        out_specs=pl.BlockSpec((tq, HV), lambda b, i: (b * nq + i, 0)),
        out_shape=jax.ShapeDtypeStruct((T, HV), BF16),
        scratch_shapes=([pltpu.VMEM((tq, LANES), F32)] * N_HEADS
                        + [pltpu.VMEM((tq, HEAD_PAD), F32)] * N_HEADS),
        compiler_params=_cparams(("arbitrary", "arbitrary")),
        name="attn",
    )(qp, kp, v)


def _outproj_kernel(lru_ref, att_ref, x_ref, gate_ref, shift_ref, scale_ref, gffn_ref,
                    wo1_ref, wo2_ref, wr_ref, br_ref, tri_ref,
                    x1_ref, h2p_ref, idx_ref, gat_ref, rank_ref, cnt_ref, run_ref):
    first = (pl.program_id(0) == 0) & (pl.program_id(1) == 0)

    @pl.when(first)
    def _():
        run_ref[...] = jnp.zeros_like(run_ref)

    mix = _dot(lru_ref[...], wo1_ref[...]) + _dot(att_ref[...], wo2_ref[...])
    x1 = x_ref[...] + gate_ref[0] * mix
    x1_ref[...] = x1
    ms = jnp.mean(x1 * x1, axis=-1, keepdims=True)
    gain = gffn_ref[...] * (1.0 + scale_ref[0])
    h2 = x1 * lax.rsqrt(ms + EPS) * gain + shift_ref[0]

    hhi = h2.astype(BF16)
    hlo = (h2 - hhi.astype(F32)).astype(BF16)
    h2p_ref[...] = _pack_halves(h2)

    ne = br_ref.shape[0]
    stacked = _dot_nt(wr_ref[...], hhi)
    logits = stacked[:ne] + stacked[ne:] + _dot_nt(wr_ref[:ne, :], hlo) + br_ref[...]

    tm = logits.shape[1]
    eio = lax.broadcasted_iota(jnp.int32, (ne, tm), 0)
    vals, idxs, sels = [], [], []
    l = logits
    for _ in range(TOP_K):
        m = jnp.max(l, axis=0, keepdims=True)
        idx = jnp.min(jnp.where(l == m, eio, ne), axis=0, keepdims=True)
        sel = eio == idx
        l = jnp.where(sel, -jnp.inf, l)
        vals.append(m)
        idxs.append(idx)
        sels.append(sel)
    es = [jnp.exp(v - vals[0]) for v in vals]
    inv = 1.0 / (es[0] + es[1] + es[2] + es[3])
    sel_any = jnp.where(sels[0] | sels[1] | sels[2] | sels[3], 1.0, 0.0)
    run = run_ref[...]
    excl = _dot(sel_any.astype(BF16), tri_ref[...]) + run
    for kk in range(TOP_K):
        idx_ref[kk:kk + 1, :] = idxs[kk]
        gat_ref[kk:kk + 1, :] = es[kk] * inv
        rk = jnp.sum(jnp.where(sels[kk], excl, 0.0), axis=0, keepdims=True)
        rank_ref[kk:kk + 1, :] = rk.astype(jnp.int32)
    run = run + jnp.sum(sel_any, axis=1, keepdims=True)
    run_ref[...] = run
    cnt_ref[...] = run.astype(jnp.int32)


def _outproj(lru_o, att_o, x2, mod3, g_ffn, wo1, wo2, wr_stack, b_r, tri, b0, B, S, tm):
    D = x2.shape[1]
    T = B * S
    ns = S // tm
    C = lru_o.shape[1]
    row_in = lambda b, s: ((b0 + b) * ns + s, 0)
    row = lambda b, s: (b * ns + s, 0)
    col = lambda b, s: (0, b * ns + s)
    full = lambda b, s: (0, 0)
    return pl.pallas_call(
        _outproj_kernel,
        grid=(B, ns),
        in_specs=[
            pl.BlockSpec((tm, C), row_in),
            pl.BlockSpec((tm, C), row_in),
            pl.BlockSpec((tm, D), row_in),
            pl.BlockSpec((1, 1, D), lambda b, s: ((b0 + b) * 6 + 2, 0, 0)),
            pl.BlockSpec((1, 1, D), lambda b, s: ((b0 + b) * 6 + 3, 0, 0)),
            pl.BlockSpec((1, 1, D), lambda b, s: ((b0 + b) * 6 + 4, 0, 0)),
            pl.BlockSpec((1, D), full),
            pl.BlockSpec((C, D), full),
            pl.BlockSpec((C, D), full),
            pl.BlockSpec((2 * N_EXPERTS, D), full),
            pl.BlockSpec((N_EXPERTS, 1), full),
            pl.BlockSpec(tri.shape, full),
        ],
        out_specs=[
            pl.BlockSpec((tm, D), row),
            pl.BlockSpec((tm, D // 2), row),
            pl.BlockSpec((TOP_K, tm), col),
            pl.BlockSpec((TOP_K, tm), col),
            pl.BlockSpec((TOP_K, tm), col),
            pl.BlockSpec((N_EXPERTS, 1), full),
        ],
        out_shape=[
            jax.ShapeDtypeStruct((T, D), F32),
            jax.ShapeDtypeStruct((T, D // 2), jnp.int32),
            jax.ShapeDtypeStruct((TOP_K, T), jnp.int32),
            jax.ShapeDtypeStruct((TOP_K, T), F32),
            jax.ShapeDtypeStruct((TOP_K, T), jnp.int32),
            jax.ShapeDtypeStruct((N_EXPERTS, 1), jnp.int32),
        ],
        scratch_shapes=[pltpu.VMEM((N_EXPERTS, 1), F32)],
        compiler_params=_cparams(("arbitrary", "arbitrary")),
        name="outproj",
    )(lru_o, att_o, x2, mod3, mod3, mod3, g_ffn, wo1, wo2, wr_stack, b_r, tri)


SC_CORES = 2
SC_SUBCORES = 16
SC_WORKERS = SC_CORES * SC_SUBCORES
SC_LANES = 16


def _sc_mesh():
    return plsc.VectorSubcoreMesh(core_axis_name="c", subcore_axis_name="s",
                                  num_cores=SC_CORES, num_subcores=SC_SUBCORES)


def _sc_worker_id():
    return lax.axis_index("s") * SC_CORES + lax.axis_index("c")


def _sc_scatter_rows(rows, idx, n_out, g):
    T, W = rows.shape
    K = idx.shape[0]
    per_w = T // SC_WORKERS
    nch = per_w // g
    assert per_w * SC_WORKERS == T and nch * g == per_w and nch % 2 == 0
    idx_w = idx.reshape(K, SC_WORKERS, nch, g).transpose(1, 2, 0, 3).reshape(SC_WORKERS, nch * K, g)

    def body(rows_hbm, idx_hbm, out_hbm, idx_v, buf0, buf1, semr0, semr1, semw):
        wid = _sc_worker_id()
        base = wid * per_w
        pltpu.sync_copy(idx_hbm.at[wid], idx_v)

        def read(j, buf, sem):
            return pltpu.make_async_copy(rows_hbm.at[pl.ds(base + j * g, g)], buf, sem)

        def scatter(j, buf):
            copies = [pltpu.async_copy(buf, out_hbm.at[idx_v.at[j * K + kk]], semw)
                      for kk in range(K)]
            for cp in copies:
                cp.wait()

        read(0, buf0, semr0).start()

        @pl.loop(0, nch // 2)
        def _(jj):
            j0 = 2 * jj
            read(j0 + 1, buf1, semr1).start()
            read(j0, buf0, semr0).wait()
            scatter(j0, buf0)

            @pl.when(j0 + 2 < nch)
            def _():
                read(j0 + 2, buf0, semr0).start()

            read(j0 + 1, buf1, semr1).wait()
            scatter(j0 + 1, buf1)

    return pl.kernel(
        body,
        out_type=jax.ShapeDtypeStruct((n_out, W), rows.dtype),
        mesh=_sc_mesh(),
        scratch_types=[
            pltpu.VMEM((nch * K, g), jnp.int32),
            pltpu.VMEM((g, W), rows.dtype),
            pltpu.VMEM((g, W), rows.dtype),
            pltpu.SemaphoreType.DMA,
            pltpu.SemaphoreType.DMA,
            pltpu.SemaphoreType.DMA,
        ],
        name="sc_scatter_rows",
    )(rows, idx_w)


def _sc_gather_gated_sum(table, idx, gates, g):
    W = table.shape[1]
    K, T = idx.shape
    per_w = T // SC_WORKERS
    nch = per_w // g
    assert per_w * SC_WORKERS == T and nch * g == per_w and nch % 2 == 0 and W % SC_LANES == 0
    idx_w = idx.reshape(K, SC_WORKERS, nch, g).transpose(1, 2, 0, 3).reshape(SC_WORKERS, nch * K, g)
    gates_w = gates.reshape(K, SC_WORKERS, nch, g).transpose(1, 2, 0, 3).reshape(SC_WORKERS, nch, K * g)
    def body(table_hbm, idx_hbm, gates_hbm, out_hbm, idx_v, rows_v, gts_v, out_v, sem0, sem1):
        wid = _sc_worker_id()
        base = wid * per_w
        pltpu.sync_copy(idx_hbm.at[wid], idx_v)

        def fetch(j, slot, sem):
            cps = []
            for kk in range(K):
                cps.append(pltpu.make_async_copy(table_hbm.at[idx_v.at[j * K + kk]],
                                                 rows_v.at[slot, kk], sem))
            cps.append(pltpu.make_async_copy(gates_hbm.at[wid, j], gts_v.at[slot], sem))
            return cps

        def start(j, slot, sem):
            for cp in fetch(j, slot, sem):
                cp.start()

        def finish(j, slot, sem):
            for cp in fetch(j, slot, sem):
                cp.wait()

            @pl.loop(0, g)
            def _(t):
                gk = [plsc.load_gather(gts_v.at[slot], [jnp.full((SC_LANES,), kk * g, jnp.int32) + t])
                      for kk in range(K)]

                @plsc.parallel_loop(0, W, SC_LANES, unroll=8)
                def _(off):
                    off = pl.multiple_of(off, SC_LANES)
                    acc_lo = jnp.zeros((SC_LANES,), F32)
                    acc_hi = jnp.zeros((SC_LANES,), F32)
                    for kk in range(K):
                        w = rows_v[slot, kk, t, pl.ds(off, SC_LANES)]
                        lo, hi = plsc.unpack(plsc.bitcast(w, BF16), format=plsc.PackFormat.INTERLEAVED)
                        acc_lo = acc_lo + gk[kk] * lo
                        acc_hi = acc_hi + gk[kk] * hi
                    out_v[slot, t, pl.ds(off, SC_LANES)] = acc_lo
                    out_v[slot, t, pl.ds(W + off, SC_LANES)] = acc_hi

            pltpu.sync_copy(out_v.at[slot], out_hbm.at[pl.ds(base + j * g, g)])

        start(0, 0, sem0)

        @pl.loop(0, nch // 2)
        def _(jj):
            j0 = 2 * jj
            start(j0 + 1, 1, sem1)
            finish(j0, 0, sem0)

            @pl.when(j0 + 2 < nch)
            def _():
                start(j0 + 2, 0, sem0)

            finish(j0 + 1, 1, sem1)

    return pl.kernel(
        body,
        out_type=jax.ShapeDtypeStruct((T, 2 * W), F32),
        mesh=_sc_mesh(),
        scratch_types=[
            pltpu.VMEM((nch * K, g), jnp.int32),
            pltpu.VMEM((2, K, g, W), jnp.int32),
            pltpu.VMEM((2, K * g), F32),
            pltpu.VMEM((2, g, 2 * W), F32),
            pltpu.SemaphoreType.DMA,
            pltpu.SemaphoreType.DMA,
        ],
        compiler_params=pltpu.CompilerParams(needs_layout_passes=False),
        name="sc_gather_gated_sum",
    )(table, idx_w, gates_w)


def _experts_kernel(be_ref, bv_ref, bf_ref, bn_ref, bs_ref, xs_ref, w1_hbm, b1_ref, w2_hbm, b2_ref,
                    ys_ref, w1s_ref, w2s_ref, w1b_ref, w2b_ref, sem):
    i = pl.program_id(0)
    nvalid = bv_ref[i]

    def weight_copies(e, slot):
        return (pltpu.make_async_copy(w1_hbm.at[e], w1s_ref.at[slot], sem.at[0, slot]),
                pltpu.make_async_copy(w2_hbm.at[e], w2s_ref.at[slot], sem.at[1, slot]))

    @pl.when(bf_ref[i] > 0)
    def _():
        slot = bs_ref[i]

        @pl.when(i == 0)
        def _():
            for cp in weight_copies(be_ref[0], slot):
                cp.start()

        for cp in weight_copies(be_ref[i], slot):
            cp.wait()
        w1b_ref[...] = w1s_ref[slot].astype(BF16)
        w2b_ref[...] = w2s_ref[slot].astype(BF16)

        @pl.when(bn_ref[i] >= 0)
        def _():
            for cp in weight_copies(bn_ref[i], 1 - slot):
                cp.start()

    def ffn(rows):
        xw = xs_ref[:rows, :]
        rowi = lax.broadcasted_iota(jnp.int32, (rows, 1), 0)
        lo, hi = _unpack_halves(jnp.where(rowi < nvalid, xw, 0))
        xb = jnp.concatenate([lo.astype(BF16), hi.astype(BF16)], axis=1)
        gu = _dot(xb, w1b_ref[...]) + b1_ref[0]
        glu = jnp.minimum(gu[:, :D_FF], SWIGLU_LIMIT)
        lin = jnp.clip(gu[:, D_FF:], -SWIGLU_LIMIT, SWIGLU_LIMIT)
        act = (lin + 1.0) * (glu * _sigmoid(glu, SWIGLU_ALPHA))
        ys_ref[:rows, :] = _pack_halves(_dot(act.astype(BF16), w2b_ref[...]) + b2_ref[0])

    step = xs_ref.shape[0] // ROW_PARTS
    for nq in range(1, ROW_PARTS + 1):
        pl.when((nvalid > (nq - 1) * step) & (nvalid <= nq * step))(functools.partial(ffn, nq * step))


def _experts(blk_e, blk_v, blk_f, blk_n, blk_s, blk_r, xs, w1, b1, w2, b2):
    P, W = xs.shape
    nb = P // MOE_BLOCK
    E, D, F2 = w1.shape
    grid_spec = pltpu.PrefetchScalarGridSpec(
        num_scalar_prefetch=6,
        grid=(nb,),
        in_specs=[
            pl.BlockSpec((MOE_BLOCK, W), lambda i, be, bv, bf, bn, bs, br: (br[i], 0)),
            pl.BlockSpec(memory_space=pl.ANY),
            pl.BlockSpec((1, 1, F2), lambda i, be, bv, bf, bn, bs, br: (be[i], 0, 0)),
            pl.BlockSpec(memory_space=pl.ANY),
            pl.BlockSpec((1, 1, D), lambda i, be, bv, bf, bn, bs, br: (be[i], 0, 0)),
        ],
        out_specs=pl.BlockSpec((MOE_BLOCK, D // 2), lambda i, be, bv, bf, bn, bs, br: (br[i], 0)),
        scratch_shapes=[
            pltpu.VMEM((2, D, F2), F32),
            pltpu.VMEM((2, D_FF, D), F32),
            pltpu.VMEM((D, F2), BF16),
            pltpu.VMEM((D_FF, D), BF16),
            pltpu.SemaphoreType.DMA((2, 2)),
        ],
    )

    def kern(be_ref, bv_ref, bf_ref, bn_ref, bs_ref, br_ref, *refs):
        del br_ref
        _experts_kernel(be_ref, bv_ref, bf_ref, bn_ref, bs_ref, *refs)

    return pl.pallas_call(
        kern,
        grid_spec=grid_spec,
        out_shape=jax.ShapeDtypeStruct((P, D // 2), jnp.int32),
        compiler_params=_cparams(("arbitrary",)),
        name="experts",
    )(blk_e, blk_v, blk_f, blk_n, blk_s, blk_r, xs, w1, b1.reshape(E, 1, F2), w2,
      b2.reshape(E, 1, D))


def _combine_kernel(x1_ref, gate_ref, ysum_ref, *rest):
    o_ref = rest[-1]
    o_ref[...] = x1_ref[...] + gate_ref[0] * ysum_ref[...]


def _combine(x1, mod3, ysum, out_prev, b0, B, nb_total, S, tm):
    D = x1.shape[1]
    ns = S // tm
    row = lambda b, s: (b * ns + s, 0)
    row_out = lambda b, s: ((b0 + b) * ns + s, 0)
    in_specs = [
        pl.BlockSpec((tm, D), row),
        pl.BlockSpec((1, 1, D), lambda b, s: ((b0 + b) * 6 + 5, 0, 0)),
        pl.BlockSpec((tm, D), row),
    ]
    args = [x1, mod3, ysum]
    aliases = {}
    if out_prev is not None:
        in_specs.append(pl.BlockSpec(memory_space=pl.ANY))
        args.append(out_prev)
        aliases = {len(args) - 1: 0}
    return pl.pallas_call(
        _combine_kernel,
        grid=(B, ns),
        in_specs=in_specs,
        out_specs=pl.BlockSpec((tm, D), row_out),
        out_shape=jax.ShapeDtypeStruct((nb_total * S, D), F32),
        input_output_aliases=aliases,
        compiler_params=_cparams(("arbitrary", "arbitrary")),
        name="combine",
    )(*args)


def _block_diag(w):
    n, c, d = w.shape
    eye = jnp.eye(n, dtype=w.dtype)
    return jnp.einsum("ncd,nm->ncmd", w, eye).reshape(n * c, n * d)


def _pad_heads(w, width):
    k = w.shape[0]
    w = w.reshape(k, N_HEADS, width)
    return jnp.pad(w, ((0, 0), (0, 0), (0, HEAD_PAD - width))).reshape(k, N_HEADS * HEAD_PAD)


def kernel(x, c, positions, w_ada, b_ada, g_mix, w_in, conv_w, conv_b, w_a, b_a, w_x, b_x, lam,
           g_q_lat, w_uq, g_kv_lat, w_ukv, g_qn, g_kn, w_out, g_ffn, w_router, b_router,
           w1, b1, w2, b2):
    B, S, D = x.shape
    T = B * S
    depth = w_ada.shape[0]
    tm_in = min(512, S)
    tk_att = min(512, S // 2)
    tm_out = min(1024, S)
    tm_comb = min(1024, S)
    n_groups = min(MAX_GROUPS, B)
    group_sizes = [B // n_groups + (1 if gi < B % n_groups else 0) for gi in range(n_groups)]

    o1 = 2 * D_LRU
    o2 = o1 + Q_LORA
    o3 = o2 + KV_LORA
    tri = jnp.asarray(np.arange(tm_out)[:, None] < np.arange(tm_out)[None, :], BF16)
    cos_t, sin_t = _rope_tables(positions)

    def rot_cols(w):
        k = w.shape[0]
        w3 = w.reshape(k, -1, HEAD_PAD)
        lo = w3[:, :, ROPE_LO:ROPE_LO + ROPE_HALF]
        hi = w3[:, :, ROPE_LO + ROPE_HALF:ROPE_LO + QK_ROPE]
        zl = jnp.zeros_like(w3[:, :, :ROPE_LO])
        zr = jnp.zeros_like(w3[:, :, ROPE_LO + QK_ROPE:])
        return jnp.concatenate([zl, hi, lo, zr], axis=2).reshape(w.shape)

    x2 = x.reshape(T, D)
    for l in range(depth):
        mod3 = _ada(c, w_ada[l], b_ada[l]).reshape(B * 6, 1, D)

        w_in_l = w_in[l]
        kr_cols = jnp.pad(w_in_l[:, o3:], ((0, 0), (ROPE_LO, LANES - ROPE_LO - QK_ROPE)))
        w_in_p = jnp.concatenate([w_in_l[:, :o3], kr_cols, rot_cols(kr_cols)], axis=1).astype(BF16)
        w_uq_h = _pad_heads(w_uq[l], QK_HEAD)
        w_uq_p = jnp.concatenate([w_uq_h, rot_cols(w_uq_h)], axis=1).astype(BF16)
        w_ukv_l = w_ukv[l].reshape(KV_LORA, N_HEADS, QK_NOPE + V_HEAD)
        w_uk_h = _pad_heads(w_ukv_l[:, :, :QK_NOPE].reshape(KV_LORA, N_HEADS * QK_NOPE), QK_NOPE)
        w_uv_pairs = w_ukv_l[:, :, QK_NOPE:].reshape(KV_LORA, N_HEADS // 2, 2, V_HEAD)
        zero_v = jnp.zeros_like(w_uv_pairs[:, :, 0])
        w_uv_h = jnp.stack([w_uv_pairs[:, :, 0], zero_v, zero_v, w_uv_pairs[:, :, 1]],
                           axis=2).reshape(KV_LORA, N_HEADS * HEAD_PAD)
        w_ukv_p = jnp.concatenate([w_uk_h, w_uv_h], axis=1).astype(BF16)
        gqn_p = jnp.pad(g_qn[l], (0, HEAD_PAD - QK_HEAD)).reshape(1, HEAD_PAD)
        gkn_p = jnp.pad(g_kn[l], (0, HEAD_PAD - QK_HEAD)).reshape(1, HEAD_PAD)

        lru_o, qp, kp, v = _inproj(
            x2, cos_t, sin_t, mod3, g_mix[l].reshape(1, D), w_in_p, g_q_lat[l].reshape(1, Q_LORA),
            w_uq_p, g_kv_lat[l].reshape(1, KV_LORA), w_ukv_p, gqn_p, rot_cols(gqn_p), gkn_p,
            rot_cols(gkn_p), conv_w[l], conv_b[l].reshape(1, D_LRU),
            _block_diag(w_a[l]).astype(BF16), b_a[l].reshape(1, D_LRU),
            _block_diag(w_x[l]).astype(BF16), b_x[l].reshape(1, D_LRU),
            lam[l].reshape(1, D_LRU), B, S, tm_in)

        att_o = _attn(qp, kp, v, B, S, tk_att)

        w_out_b = w_out[l].astype(BF16)
        wr_stack = jnp.concatenate(_split_bf16(w_router[l].T), axis=0)
        g_ffn_l = g_ffn[l].reshape(1, D)
        b_r = b_router[l].reshape(N_EXPERTS, 1)
        eio = np.arange(N_EXPERTS, dtype=np.int32)

        x_next = None
        b0 = 0
        for Bg in group_sizes:
            Tg = Bg * S
            n_blocks = -(-(Tg * TOP_K) // MOE_BLOCK) + N_EXPERTS
            bi = np.arange(n_blocks, dtype=np.int32)
            g_disp = min(64, Tg // SC_WORKERS // 2)
            g_comb = min(16, Tg // SC_WORKERS // 2)
            x1, h2p, idx_t, gat_t, rank_t, counts = _outproj(
                lru_o, att_o, x2, mod3, g_ffn_l, w_out_b[:D_LRU], w_out_b[D_LRU:],
                wr_stack, b_r, tri, b0, Bg, S, tm_out)

            counts = counts.reshape(N_EXPERTS)
            nblk_e = (counts + MOE_BLOCK - 1) // MOE_BLOCK
            blk_end = jnp.cumsum(nblk_e)
            pad_start = (blk_end - nblk_e) * MOE_BLOCK
            total = blk_end[-1]
            blk_r = jnp.minimum(bi, total - 1).astype(jnp.int32)
            blk_e = jnp.minimum(jnp.sum(blk_end[None, :] <= blk_r[:, None], axis=1),
                                N_EXPERTS - 1).astype(jnp.int32)
            blk_onehot = blk_e[:, None] == eio[None, :]
            blk_first = jnp.sum(jnp.where(blk_onehot, (blk_end - nblk_e)[None, :], 0), axis=1)
            blk_cnt = jnp.sum(jnp.where(blk_onehot, counts[None, :], 0), axis=1)
            blk_v = jnp.where(bi < total,
                              jnp.clip(blk_cnt - (bi - blk_first) * MOE_BLOCK, 0, MOE_BLOCK),
                              0).astype(jnp.int32)
            blk_f = ((bi == blk_first) & (bi < total)).astype(jnp.int32)
            nxt_first = jnp.sum(jnp.where(blk_onehot, blk_end[None, :], 0), axis=1)
            nxt_e = jnp.minimum(jnp.sum(blk_end[None, :] <= nxt_first[:, None], axis=1),
                                N_EXPERTS - 1)
            blk_n = jnp.where(nxt_first < total, nxt_e, -1).astype(jnp.int32)
            ordinal = jnp.cumsum((nblk_e > 0).astype(jnp.int32)) - 1
            blk_s = (jnp.sum(jnp.where(blk_onehot, ordinal[None, :], 0), axis=1) % 2).astype(jnp.int32)
            slot0 = jnp.sum(jnp.where(idx_t[None] == eio[:, None, None],
                                      pad_start[:, None, None], 0), axis=0)
            dest = slot0.astype(jnp.int32) + rank_t

            xs = _sc_scatter_rows(h2p, dest, n_blocks * MOE_BLOCK, g_disp)
            ys = _experts(blk_e, blk_v, blk_f, blk_n, blk_s, blk_r, xs, w1[l], b1[l], w2[l], b2[l])
            ysum = _sc_gather_gated_sum(ys, dest, gat_t, g_comb)
            x_next = _combine(x1, mod3, ysum, x_next, b0, Bg, B, S, tm_comb)
            b0 += Bg
        x2 = x_next
    return x2.reshape(B, S, D)
```

```python
import functools

import jax
import jax.numpy as jnp
import numpy as np
from jax import lax
from jax.experimental import pallas as pl
from jax.experimental.pallas import tpu as pltpu
from jax.experimental.pallas import tpu_sc as plsc

D_LRU = 512
CONV_W = 4
LRU_C = 8.0
N_HEADS = 8
QK_NOPE = 64
QK_ROPE = 32
QK_HEAD = 96
V_HEAD = 64
Q_LORA = 256
KV_LORA = 128
ROPE_THETA = 10000.0
N_EXPERTS = 32
TOP_K = 4
D_FF = 1024
SWIGLU_LIMIT = 7.0
SWIGLU_ALPHA = 1.702
MOE_BLOCK = 1024
EPS = 1e-6

LANES = 128
SUBLANES = 8
HEAD_PAD = 128
ROPE_LO = QK_NOPE
ROPE_HALF = QK_ROPE // 2
TOK_PER_ROW = LANES // ROPE_HALF
D_IN_PAD = 2 * D_LRU + Q_LORA + KV_LORA + 2 * LANES
LOG2_E = 1.4426950408889634
MAX_GROUPS = 2
ROW_PARTS = 8

VMEM_LIMIT = 56 * 1024 * 1024

F32 = jnp.float32
BF16 = jnp.bfloat16


def _cparams(sem):
    return pltpu.CompilerParams(dimension_semantics=sem, vmem_limit_bytes=VMEM_LIMIT)


def _dot(a, b):
    return jnp.dot(a, b, preferred_element_type=F32)


def _dot_nt(a, b):
    return lax.dot_general(a, b, (((1,), (1,)), ((), ())), preferred_element_type=F32)


def _split_bf16(a):
    hi = a.astype(BF16)
    lo = (a - hi.astype(F32)).astype(BF16)
    return hi, lo


def _sigmoid(x, scale=1.0):
    return 1.0 / (1.0 + jnp.exp2(x * (-scale * LOG2_E)))


def _pack_halves(x):
    bits = lax.bitcast_convert_type(x.astype(BF16).astype(F32), jnp.uint32)
    half = x.shape[1] // 2
    words = (bits[:, :half] >> 16) | (bits[:, half:] & jnp.uint32(0xFFFF0000))
    return lax.bitcast_convert_type(words, jnp.int32)


def _unpack_halves(words):
    w = lax.bitcast_convert_type(words, jnp.uint32)
    lo = lax.bitcast_convert_type(w << 16, F32)
    hi = lax.bitcast_convert_type(w & jnp.uint32(0xFFFF0000), F32)
    return lo, hi


def _ada_kernel(c_ref, w_ref, b_ref, o_ref):
    c = c_ref[...]
    s = c * _sigmoid(c)
    shi, slo = _split_bf16(s)
    whi, wlo = _split_bf16(w_ref[...])
    o_ref[...] = _dot(shi, whi) + _dot(slo, whi) + _dot(shi, wlo) + b_ref[...]


def _ada(c, w_ada, b_ada):
    B, D = c.shape
    N = w_ada.shape[1]
    tn = 1024
    return pl.pallas_call(
        _ada_kernel,
        grid=(N // tn,),
        in_specs=[
            pl.BlockSpec((B, D), lambda j: (0, 0)),
            pl.BlockSpec((D, tn), lambda j: (0, j)),
            pl.BlockSpec((1, tn), lambda j: (0, j)),
        ],
        out_specs=pl.BlockSpec((B, tn), lambda j: (0, j)),
        out_shape=jax.ShapeDtypeStruct((B, N), F32),
        compiler_params=_cparams(("arbitrary",)),
        name="ada",
    )(c, w_ada, b_ada.reshape(1, N))


def _trig_kernel(pos_ref, freq_ref, rsel_ref, fold_ref, cbase_ref, cos_ref, sin_ref):
    ang = pos_ref[...].astype(F32) * freq_ref[...]
    cs = jnp.concatenate([jnp.cos(ang), jnp.sin(ang)], axis=1)
    tm = cos_ref.shape[0]
    row = lax.broadcasted_iota(jnp.int32, (tm, 2 * LANES), 0)
    lane = lax.broadcasted_iota(jnp.int32, (tm, 2 * LANES), 1)
    own = ((lane % LANES) // ROPE_HALF) == (row % TOK_PER_ROW)
    rsel = rsel_ref[...]
    fold = fold_ref[...]
    by_row = sum(_dot(rsel, part) for part in _split_bf16(cs))
    mine = jnp.where(own, by_row, 0.0)
    out = sum(_dot(part, fold) for part in _split_bf16(mine))
    cos_ref[...] = out[:, :LANES] + cbase_ref[...]
    sin_ref[...] = out[:, LANES:]


def _rope_tables(positions):
    T = positions.size
    rows = T // TOK_PER_ROW
    pos_c = jnp.repeat(positions.reshape(T).astype(jnp.int32), ROPE_HALF).reshape(rows, LANES)
    tm = min(2048, T)
    tr = tm // TOK_PER_ROW
    freqs = np.float32(ROPE_THETA) ** (-np.arange(ROPE_HALF, dtype=np.float32) / np.float32(ROPE_HALF))
    freq_c = np.tile(freqs.astype(np.float32), TOK_PER_ROW).reshape(1, LANES)
    rsel = jnp.asarray(np.arange(tm)[:, None] // TOK_PER_ROW == np.arange(tr)[None, :], BF16)
    src = np.arange(LANES)[:, None] % ROPE_HALF
    dst = np.arange(LANES)[None, :]
    first = dst == ROPE_LO + src
    second = dst == ROPE_LO + ROPE_HALF + src
    fcos = (first | second).astype(np.float32)
    fsin = second.astype(np.float32) - first.astype(np.float32)
    zero = np.zeros((LANES, LANES), np.float32)
    fold = jnp.asarray(np.block([[fcos, zero], [zero, fsin]]), BF16)
    lane = np.arange(LANES)
    cbase = ((lane < ROPE_LO) | (lane >= ROPE_LO + QK_ROPE)).astype(np.float32).reshape(1, LANES)
    full = lambda i: (0, 0)
    return pl.pallas_call(
        _trig_kernel,
        grid=(T // tm,),
        in_specs=[
            pl.BlockSpec((tr, LANES), lambda i: (i, 0)),
            pl.BlockSpec((1, LANES), full),
            pl.BlockSpec((tm, tr), full),
            pl.BlockSpec((2 * LANES, 2 * LANES), full),
            pl.BlockSpec((1, LANES), full),
        ],
        out_specs=[pl.BlockSpec((tm, LANES), lambda i: (i, 0))] * 2,
        out_shape=[jax.ShapeDtypeStruct((T, LANES), F32)] * 2,
        compiler_params=_cparams(("arbitrary",)),
        name="rope_trig",
    )(pos_c, freq_c, rsel, fold, cbase)


def _inproj_kernel(x_ref, cos_ref, sin_ref, shift_ref, scale_ref, gmix_ref, win_ref, gq_ref, wuq_ref,
                   gkv_ref, wukv_ref, gqn_ref, gqr_ref, gkn_ref, gkr_ref,
                   cw_ref, cb_ref, wa_ref, ba_ref, wx_ref, bx_ref, lam_ref,
                   lru_ref, q_ref, k_ref, v_ref, tail_ref, carry_ref):
    HP = N_HEADS * HEAD_PAD

    @pl.when(pl.program_id(1) == 0)
    def _():
        tail_ref[...] = jnp.zeros_like(tail_ref)
        carry_ref[...] = jnp.zeros_like(carry_ref)

    x = x_ref[...]
    ms = jnp.mean(x * x, axis=-1, keepdims=True)
    gain = gmix_ref[...] * (1.0 + scale_ref[0])
    h = x * lax.rsqrt(ms + EPS) * gain + shift_ref[0]
    hb = h.astype(BF16)
    o1 = 2 * D_LRU
    o2 = Q_LORA
    o3 = o2 + KV_LORA
    z_lru = _dot(hb, win_ref[:, :o1])
    z = _dot(hb, win_ref[:, o1:])
    lru_ref[...] = _lru_tile(z_lru[:, :D_LRU], z_lru[:, D_LRU:], cw_ref, cb_ref, wa_ref, ba_ref,
                             wx_ref, bx_ref, lam_ref, tail_ref, carry_ref)
    ql = z[:, :o2]
    kvl = z[:, o2:o3]
    kr = z[:, o3:o3 + LANES]
    kr_rot = z[:, o3 + LANES:]

    qn = ql * lax.rsqrt(jnp.mean(ql * ql, axis=-1, keepdims=True) + EPS) * gq_ref[...]
    qq = _dot(qn.astype(BF16), wuq_ref[...])
    kvn = kvl * lax.rsqrt(jnp.mean(kvl * kvl, axis=-1, keepdims=True) + EPS) * gkv_ref[...]
    kv = _dot(kvn.astype(BF16), wukv_ref[...])

    tm = x.shape[0]
    lane = lax.broadcasted_iota(jnp.int32, (tm, HP), 1)
    pair_lane = lane & (2 * HEAD_PAD - 1)
    ones_cols = (pair_lane >= V_HEAD) & (pair_lane < 2 * HEAD_PAD - V_HEAD)
    v_ref[...] = jnp.where(ones_cols, 1.0, kv[:, HP:]).astype(BF16)

    cos_t = cos_ref[...]
    sin_t = sin_ref[...]
    gqn = gqn_ref[...]
    gkn = gkn_ref[...]
    cq = gqn * cos_t
    sq = gqr_ref[...] * sin_t
    kb = kr * (gkn * cos_t) + kr_rot * (gkr_ref[...] * sin_t)
    inv_w = 1.0 / QK_HEAD
    qscale = QK_HEAD ** -0.5 * LOG2_E
    for hh in range(N_HEADS):
        sl = slice(hh * HEAD_PAD, (hh + 1) * HEAD_PAD)
        qh = qq[:, sl]
        rq = lax.rsqrt(jnp.sum(qh * qh, axis=-1, keepdims=True) * inv_w + EPS) * qscale
        q_ref[:, sl] = ((qh * cq + qq[:, HP + hh * HEAD_PAD:HP + (hh + 1) * HEAD_PAD] * sq) * rq).astype(BF16)
        kraw = kv[:, sl] + kr
        rk = lax.rsqrt(jnp.sum(kraw * kraw, axis=-1, keepdims=True) * inv_w + EPS)
        k_ref[:, sl] = ((kv[:, sl] * gkn + kb) * rk).astype(BF16)


def _inproj(x2, cos_t, sin_t, mod3, g_mix, w_in_p, g_q_lat, w_uq_p, g_kv_lat, w_ukv_p,
            gqn_p, gqr_p, gkn_p, gkr_p, conv_w, conv_b, wa_d, b_a, wx_d, b_x, lam, B, S, tm):
    T, D = x2.shape
    ns = S // tm
    HP = N_HEADS * HEAD_PAD
    C = D_LRU
    row = lambda b, s: (b * ns + s, 0)
    full = lambda b, s: (0, 0)
    return pl.pallas_call(
        _inproj_kernel,
        grid=(B, ns),
        in_specs=[
            pl.BlockSpec((tm, D), row),
            pl.BlockSpec((tm, LANES), row),
            pl.BlockSpec((tm, LANES), row),
            pl.BlockSpec((1, 1, D), lambda b, s: (b * 6 + 0, 0, 0)),
            pl.BlockSpec((1, 1, D), lambda b, s: (b * 6 + 1, 0, 0)),
            pl.BlockSpec((1, D), full),
            pl.BlockSpec((D, D_IN_PAD), full),
            pl.BlockSpec((1, Q_LORA), full),
            pl.BlockSpec((Q_LORA, 2 * HP), full),
            pl.BlockSpec((1, KV_LORA), full),
            pl.BlockSpec((KV_LORA, 2 * HP), full),
            pl.BlockSpec((1, HEAD_PAD), full),
            pl.BlockSpec((1, HEAD_PAD), full),
            pl.BlockSpec((1, HEAD_PAD), full),
            pl.BlockSpec((1, HEAD_PAD), full),
            pl.BlockSpec((CONV_W, C), full),
            pl.BlockSpec((1, C), full),
            pl.BlockSpec((C, C), full),
            pl.BlockSpec((1, C), full),
            pl.BlockSpec((C, C), full),
            pl.BlockSpec((1, C), full),
            pl.BlockSpec((1, C), full),
        ],
        out_specs=[
            pl.BlockSpec((tm, C), row),
            pl.BlockSpec((tm, HP), row),
            pl.BlockSpec((tm, HP), row),
            pl.BlockSpec((tm, HP), row),
        ],
        out_shape=[
            jax.ShapeDtypeStruct((T, C), BF16),
            jax.ShapeDtypeStruct((T, HP), BF16),
            jax.ShapeDtypeStruct((T, HP), BF16),
            jax.ShapeDtypeStruct((T, HP), BF16),
        ],
        scratch_shapes=[pltpu.VMEM((SUBLANES, C), F32), pltpu.VMEM((SUBLANES, C), F32)],
        compiler_params=_cparams(("arbitrary", "arbitrary")),
        name="inproj",
    )(x2, cos_t, sin_t, mod3, mod3, g_mix, w_in_p, g_q_lat, w_uq_p, g_kv_lat, w_ukv_p,
      gqn_p, gqr_p, gkn_p, gkr_p, conv_w, conv_b, wa_d, b_a, wx_d, b_x, lam)


def _gelu_tanh(x):
    c = 0.7978845608028654
    hx = 0.5 * x
    return hx + hx * jnp.tanh(x * (c + (c * 0.044715) * (x * x)))


def _lru_tile(x, y, cw_ref, cb_ref, wa_ref, ba_ref, wx_ref, bx_ref, lam_ref, tail_ref, carry_ref):
    ts = x.shape[0]
    xext = jnp.concatenate([tail_ref[...], x], axis=0)
    cw = cw_ref[...]
    xc = x * cw[CONV_W - 1:CONV_W, :]
    for j in range(CONV_W - 1):
        sh = CONV_W - 1 - j
        xc = xc + xext[8 - sh:8 - sh + ts, :] * cw[j:j + 1, :]
    xc = xc + cb_ref[...]
    tail_ref[...] = x[ts - 8:, :]

    xb = xc.astype(BF16)
    r = _sigmoid(_dot(xb, wa_ref[...]) + ba_ref[...])
    i = _sigmoid(_dot(xb, wx_ref[...]) + bx_ref[...])
    lam = lam_ref[...]
    nl = -lam
    softplus = jnp.maximum(nl, 0.0) + jnp.log(1.0 + jnp.exp(-jnp.abs(nl)))
    log_a = (-LRU_C) * r * softplus
    a = jnp.exp(log_a)
    mult = jnp.sqrt(1.0 - a * a)
    u = mult * (i * xc)

    C = a.shape[1]
    a = a.reshape(ts // SUBLANES, SUBLANES, C)
    u = u.reshape(ts // SUBLANES, SUBLANES, C)
    sub = lax.broadcasted_iota(jnp.int32, (1, SUBLANES, 1), 1)
    sh = 1
    while sh < SUBLANES:
        a_prev = pltpu.roll(a, sh, axis=1)
        u_prev = pltpu.roll(u, sh, axis=1)
        m = sub >= sh
        u = jnp.where(m, a * u_prev + u, u)
        a = jnp.where(m, a * a_prev, a)
        sh *= 2
    a = a.reshape(ts, C)
    u = u.reshape(ts, C)
    h = carry_ref[0:1, :]
    groups = []
    for g0 in range(0, ts, SUBLANES):
        hg = u[g0:g0 + SUBLANES, :] + a[g0:g0 + SUBLANES, :] * h
        groups.append(hg)
        h = hg[SUBLANES - 1:SUBLANES, :]
    carry_ref[...] = jnp.broadcast_to(h, carry_ref.shape)
    hs = jnp.concatenate(groups, axis=0)
    return (_gelu_tanh(y) * hs).astype(BF16)


NEG_INF = -1e30


def _attn_kernel(q_ref, k_ref, v_ref, o_ref, *state, tq, tk):
    m_refs = state[:N_HEADS]
    acc_refs = state[N_HEADS:]
    qi = pl.program_id(1)
    causal = (lax.broadcasted_iota(jnp.int32, (tk, tk), 1)
              <= lax.broadcasted_iota(jnp.int32, (tk, tk), 0))
    lower = slice(tk, tq)

    def head_slice(hh):
        return slice(hh * HEAD_PAD, (hh + 1) * HEAD_PAD)

    def weights(sc, m_b):
        cols = [jnp.exp2(sc[:, c0:c0 + LANES] - m_b) for c0 in range(0, tk, LANES)]
        return jnp.concatenate(cols, axis=1).astype(BF16)

    def scores(hh, r0, rows):
        hs = head_slice(hh)
        return _dot_nt(q_ref[rows, hs], k_ref[pl.ds(r0, tk), hs])

    def update(hh, rows, sc, r0):
        hs = head_slice(hh)
        m_b = m_refs[hh][rows, :]
        m_new = jnp.maximum(m_b, jnp.max(sc, axis=-1, keepdims=True))
        alpha = jnp.exp2(m_b - m_new)
        m_refs[hh][rows, :] = m_new
        acc_refs[hh][rows, :] = (alpha * acc_refs[hh][rows, :]
                                 + _dot(weights(sc, m_new), v_ref[pl.ds(r0, tk), hs]))


    every = slice(0, tq)
    r_d0 = pl.multiple_of(qi * tq, tk)
    sc_next = scores(0, r_d0, every)
    for hh in range(N_HEADS):
        hs = head_slice(hh)
        sc = jnp.concatenate([jnp.where(causal, sc_next[:tk], NEG_INF), sc_next[tk:]], axis=0)
        if hh + 1 < N_HEADS:
            sc_next = scores(hh + 1, r_d0, every)
        m_b = jnp.broadcast_to(jnp.max(sc, axis=-1, keepdims=True), (tq, LANES))
        m_refs[hh][...] = m_b
        acc_refs[hh][...] = _dot(weights(sc, m_b), v_ref[pl.ds(r_d0, tk), hs])

    r_d1 = pl.multiple_of(qi * tq + tk, tk)
    sc_next = scores(0, r_d1, lower)
    for hh in range(N_HEADS):
        sc = jnp.where(causal, sc_next, NEG_INF)
        if hh + 1 < N_HEADS:
            sc_next = scores(hh + 1, r_d1, lower)
        update(hh, lower, sc, r_d1)

    @pl.loop(0, qi * (tq // tk))
    def _(j):
        r0 = pl.multiple_of(j * tk, tk)
        sc_next = scores(0, r0, every)
        for hh in range(N_HEADS):
            sc = sc_next
            if hh + 1 < N_HEADS:
                sc_next = scores(hh + 1, r0, every)
            update(hh, every, sc, r0)

    low = lax.broadcasted_iota(jnp.int32, (tq, HEAD_PAD), 1) < V_HEAD
    for he in range(0, N_HEADS, 2):
        acc_e = acc_refs[he][...]
        acc_o = acc_refs[he + 1][...]
        num = jnp.where(low, acc_e, acc_o)
        den = pltpu.roll(jnp.where(low, acc_o, acc_e), V_HEAD, axis=1)
        o_ref[:, he * V_HEAD:(he + 2) * V_HEAD] = (num / den).astype(BF16)


def _attn(qp, kp, v, B, S, tk):
    T = qp.shape[0]
    tq = 2 * tk
    nq = S // tq
    HP = N_HEADS * HEAD_PAD
    HV = N_HEADS * V_HEAD
    return pl.pallas_call(
        functools.partial(_attn_kernel, tq=tq, tk=tk),
        grid=(B, nq),
        in_specs=[
            pl.BlockSpec((tq, HP), lambda b, i: (b * nq + i, 0)),
            pl.BlockSpec((S, HP), lambda b, i: (b, 0)),
            pl.BlockSpec((S, HP), lambda b, i: (b, 0)),
        ],
        out_specs=pl.BlockSpec((tq, HV), lambda b, i: (b * nq + i, 0)),
        out_shape=jax.ShapeDtypeStruct((T, HV), BF16),
        scratch_shapes=([pltpu.VMEM((tq, LANES), F32)] * N_HEADS
                        + [pltpu.VMEM((tq, HEAD_PAD), F32)] * N_HEADS),
        compiler_params=_cparams(("arbitrary", "arbitrary")),
        name="attn",
    )(qp, kp, v)


def _outproj_kernel(lru_ref, att_ref, x_ref, gate_ref, shift_ref, scale_ref, gffn_ref,
                    wo1_ref, wo2_ref, wr_ref, br_ref, tri_ref,
                    x1_ref, h2p_ref, idx_ref, gat_ref, rank_ref, cnt_ref, run_ref):
    first = (pl.program_id(0) == 0) & (pl.program_id(1) == 0)

    @pl.when(first)
    def _():
        run_ref[...] = jnp.zeros_like(run_ref)

    mix = _dot(lru_ref[...], wo1_ref[...]) + _dot(att_ref[...], wo2_ref[...])
    x1 = x_ref[...] + gate_ref[0] * mix
    x1_ref[...] = x1
    ms = jnp.mean(x1 * x1, axis=-1, keepdims=True)
    gain = gffn_ref[...] * (1.0 + scale_ref[0])
    h2 = x1 * lax.rsqrt(ms + EPS) * gain + shift_ref[0]

    hhi = h2.astype(BF16)
    hlo = (h2 - hhi.astype(F32)).astype(BF16)
    h2p_ref[...] = _pack_halves(h2)

    ne = br_ref.shape[0]
    stacked = _dot_nt(wr_ref[...], hhi)
    logits = stacked[:ne] + stacked[ne:] + _dot_nt(wr_ref[:ne, :], hlo) + br_ref[...]

    tm = logits.shape[1]
    eio = lax.broadcasted_iota(jnp.int32, (ne, tm), 0)
    vals, idxs, sels = [], [], []
    l = logits
    for _ in range(TOP_K):
        m = jnp.max(l, axis=0, keepdims=True)
        idx = jnp.min(jnp.where(l == m, eio, ne), axis=0, keepdims=True)
        sel = eio == idx
        l = jnp.where(sel, -jnp.inf, l)
        vals.append(m)
        idxs.append(idx)
        sels.append(sel)
    es = [jnp.exp(v - vals[0]) for v in vals]
    inv = 1.0 / (es[0] + es[1] + es[2] + es[3])
    sel_any = jnp.where(sels[0] | sels[1] | sels[2] | sels[3], 1.0, 0.0)
    run = run_ref[...]
    excl = _dot(sel_any.astype(BF16), tri_ref[...]) + run
    for kk in range(TOP_K):
        idx_ref[kk:kk + 1, :] = idxs[kk]
        gat_ref[kk:kk + 1, :] = es[kk] * inv
        rk = jnp.sum(jnp.where(sels[kk], excl, 0.0), axis=0, keepdims=True)
        rank_ref[kk:kk + 1, :] = rk.astype(jnp.int32)
    run = run + jnp.sum(sel_any, axis=1, keepdims=True)
    run_ref[...] = run
    cnt_ref[...] = run.astype(jnp.int32)


def _outproj(lru_o, att_o, x2, mod3, g_ffn, wo1, wo2, wr_stack, b_r, tri, b0, B, S, tm):
    D = x2.shape[1]
    T = B * S
    ns = S // tm
    C = lru_o.shape[1]
    row_in = lambda b, s: ((b0 + b) * ns + s, 0)
    row = lambda b, s: (b * ns + s, 0)
    col = lambda b, s: (0, b * ns + s)
    full = lambda b, s: (0, 0)
    return pl.pallas_call(
        _outproj_kernel,
        grid=(B, ns),
        in_specs=[
            pl.BlockSpec((tm, C), row_in),
            pl.BlockSpec((tm, C), row_in),
            pl.BlockSpec((tm, D), row_in),
            pl.BlockSpec((1, 1, D), lambda b, s: ((b0 + b) * 6 + 2, 0, 0)),
            pl.BlockSpec((1, 1, D), lambda b, s: ((b0 + b) * 6 + 3, 0, 0)),
            pl.BlockSpec((1, 1, D), lambda b, s: ((b0 + b) * 6 + 4, 0, 0)),
            pl.BlockSpec((1, D), full),
            pl.BlockSpec((C, D), full),
            pl.BlockSpec((C, D), full),
            pl.BlockSpec((2 * N_EXPERTS, D), full),
            pl.BlockSpec((N_EXPERTS, 1), full),
            pl.BlockSpec(tri.shape, full),
        ],
        out_specs=[
            pl.BlockSpec((tm, D), row),
            pl.BlockSpec((tm, D // 2), row),
            pl.BlockSpec((TOP_K, tm), col),
            pl.BlockSpec((TOP_K, tm), col),
            pl.BlockSpec((TOP_K, tm), col),
            pl.BlockSpec((N_EXPERTS, 1), full),
        ],
        out_shape=[
            jax.ShapeDtypeStruct((T, D), F32),
            jax.ShapeDtypeStruct((T, D // 2), jnp.int32),
            jax.ShapeDtypeStruct((TOP_K, T), jnp.int32),
            jax.ShapeDtypeStruct((TOP_K, T), F32),
            jax.ShapeDtypeStruct((TOP_K, T), jnp.int32),
            jax.ShapeDtypeStruct((N_EXPERTS, 1), jnp.int32),
        ],
        scratch_shapes=[pltpu.VMEM((N_EXPERTS, 1), F32)],
        compiler_params=_cparams(("arbitrary", "arbitrary")),
        name="outproj",
    )(lru_o, att_o, x2, mod3, mod3, mod3, g_ffn, wo1, wo2, wr_stack, b_r, tri)


SC_CORES = 2
SC_SUBCORES = 16
SC_WORKERS = SC_CORES * SC_SUBCORES
SC_LANES = 16


def _sc_mesh():
    return plsc.VectorSubcoreMesh(core_axis_name="c", subcore_axis_name="s",
                                  num_cores=SC_CORES, num_subcores=SC_SUBCORES)


def _sc_worker_id():
    return lax.axis_index("s") * SC_CORES + lax.axis_index("c")


def _sc_scatter_rows(rows, idx, n_out, g):
    T, W = rows.shape
    K = idx.shape[0]
    per_w = T // SC_WORKERS
    nch = per_w // g
    assert per_w * SC_WORKERS == T and nch * g == per_w and nch % 2 == 0
    idx_w = idx.reshape(K, SC_WORKERS, nch, g).transpose(1, 2, 0, 3).reshape(SC_WORKERS, nch * K, g)

    def body(rows_hbm, idx_hbm, out_hbm, idx_v, buf0, buf1, semr0, semr1, semw):
        wid = _sc_worker_id()
        base = wid * per_w
        pltpu.sync_copy(idx_hbm.at[wid], idx_v)

        def read(j, buf, sem):
            return pltpu.make_async_copy(rows_hbm.at[pl.ds(base + j * g, g)], buf, sem)

        def scatter(j, buf):
            copies = [pltpu.async_copy(buf, out_hbm.at[idx_v.at[j * K + kk]], semw)
                      for kk in range(K)]
            for cp in copies:
                cp.wait()

        read(0, buf0, semr0).start()

        @pl.loop(0, nch // 2)
        def _(jj):
            j0 = 2 * jj
            read(j0 + 1, buf1, semr1).start()
            read(j0, buf0, semr0).wait()
            scatter(j0, buf0)

            @pl.when(j0 + 2 < nch)
            def _():
                read(j0 + 2, buf0, semr0).start()

            read(j0 + 1, buf1, semr1).wait()
            scatter(j0 + 1, buf1)

    return pl.kernel(
        body,
        out_type=jax.ShapeDtypeStruct((n_out, W), rows.dtype),
        mesh=_sc_mesh(),
        scratch_types=[
            pltpu.VMEM((nch * K, g), jnp.int32),
            pltpu.VMEM((g, W), rows.dtype),
            pltpu.VMEM((g, W), rows.dtype),
            pltpu.SemaphoreType.DMA,
            pltpu.SemaphoreType.DMA,
            pltpu.SemaphoreType.DMA,
        ],
        name="sc_scatter_rows",
    )(rows, idx_w)


def _sc_gather_gated_sum(table, idx, gates, g):
    W = table.shape[1]
    K, T = idx.shape
    per_w = T // SC_WORKERS
    nch = per_w // g
    assert per_w * SC_WORKERS == T and nch * g == per_w and nch % 2 == 0 and W % SC_LANES == 0
    idx_w = idx.reshape(K, SC_WORKERS, nch, g).transpose(1, 2, 0, 3).reshape(SC_WORKERS, nch * K, g)
    gates_w = gates.reshape(K, SC_WORKERS, nch, g).transpose(1, 2, 0, 3).reshape(SC_WORKERS, nch, K * g)
    def body(table_hbm, idx_hbm, gates_hbm, out_hbm, idx_v, rows_v, gts_v, out_v, sem0, sem1):
        wid = _sc_worker_id()
        base = wid * per_w
        pltpu.sync_copy(idx_hbm.at[wid], idx_v)

        def fetch(j, slot, sem):
            cps = []
            for kk in range(K):
                cps.append(pltpu.make_async_copy(table_hbm.at[idx_v.at[j * K + kk]],
                                                 rows_v.at[slot, kk], sem))
            cps.append(pltpu.make_async_copy(gates_hbm.at[wid, j], gts_v.at[slot], sem))
            return cps

        def start(j, slot, sem):
            for cp in fetch(j, slot, sem):
                cp.start()

        def finish(j, slot, sem):
            for cp in fetch(j, slot, sem):
                cp.wait()

            @pl.loop(0, g)
            def _(t):
                gk = [plsc.load_gather(gts_v.at[slot], [jnp.full((SC_LANES,), kk * g, jnp.int32) + t])
                      for kk in range(K)]

                @plsc.parallel_loop(0, W, SC_LANES, unroll=4)
                def _(off):
                    off = pl.multiple_of(off, SC_LANES)
                    acc_lo = jnp.zeros((SC_LANES,), F32)
                    acc_hi = jnp.zeros((SC_LANES,), F32)
                    for kk in range(K):
                        w = rows_v[slot, kk, t, pl.ds(off, SC_LANES)]
                        lo, hi = plsc.unpack(plsc.bitcast(w, BF16), format=plsc.PackFormat.INTERLEAVED)
                        acc_lo = acc_lo + gk[kk] * lo
                        acc_hi = acc_hi + gk[kk] * hi
                    out_v[slot, t, pl.ds(off, SC_LANES)] = acc_lo
                    out_v[slot, t, pl.ds(W + off, SC_LANES)] = acc_hi

            pltpu.sync_copy(out_v.at[slot], out_hbm.at[pl.ds(base + j * g, g)])

        start(0, 0, sem0)

        @pl.loop(0, nch // 2)
        def _(jj):
            j0 = 2 * jj
            start(j0 + 1, 1, sem1)
            finish(j0, 0, sem0)

            @pl.when(j0 + 2 < nch)
            def _():
                start(j0 + 2, 0, sem0)

            finish(j0 + 1, 1, sem1)

    return pl.kernel(
        body,
        out_type=jax.ShapeDtypeStruct((T, 2 * W), F32),
        mesh=_sc_mesh(),
        scratch_types=[
            pltpu.VMEM((nch * K, g), jnp.int32),
            pltpu.VMEM((2, K, g, W), jnp.int32),
            pltpu.VMEM((2, K * g), F32),
            pltpu.VMEM((2, g, 2 * W), F32),
            pltpu.SemaphoreType.DMA,
            pltpu.SemaphoreType.DMA,
        ],
        compiler_params=pltpu.CompilerParams(needs_layout_passes=False),
        name="sc_gather_gated_sum",
    )(table, idx_w, gates_w)


def _experts_kernel(be_ref, bv_ref, bf_ref, bn_ref, bs_ref, xs_ref, w1_hbm, b1_ref, w2_hbm, b2_ref,
                    ys_ref, w1s_ref, w2s_ref, w1b_ref, w2b_ref, sem):
    i = pl.program_id(0)
    nvalid = bv_ref[i]

    def weight_copies(e, slot):
        return (pltpu.make_async_copy(w1_hbm.at[e], w1s_ref.at[slot], sem.at[0, slot]),
                pltpu.make_async_copy(w2_hbm.at[e], w2s_ref.at[slot], sem.at[1, slot]))

    @pl.when(bf_ref[i] > 0)
    def _():
        slot = bs_ref[i]

        @pl.when(i == 0)
        def _():
            for cp in weight_copies(be_ref[0], slot):
                cp.start()

        for cp in weight_copies(be_ref[i], slot):
            cp.wait()
        w1b_ref[...] = w1s_ref[slot].astype(BF16)
        w2b_ref[...] = w2s_ref[slot].astype(BF16)

        @pl.when(bn_ref[i] >= 0)
        def _():
            for cp in weight_copies(bn_ref[i], 1 - slot):
                cp.start()

    def ffn(rows):
        xw = xs_ref[:rows, :]
        rowi = lax.broadcasted_iota(jnp.int32, (rows, 1), 0)
        lo, hi = _unpack_halves(jnp.where(rowi < nvalid, xw, 0))
        xb = jnp.concatenate([lo.astype(BF16), hi.astype(BF16)], axis=1)
        gu = _dot(xb, w1b_ref[...]) + b1_ref[0]
        glu = jnp.minimum(gu[:, :D_FF], SWIGLU_LIMIT)
        lin = jnp.clip(gu[:, D_FF:], -SWIGLU_LIMIT, SWIGLU_LIMIT)
        act = (lin + 1.0) * (glu * _sigmoid(glu, SWIGLU_ALPHA))
        ys_ref[:rows, :] = _pack_halves(_dot(act.astype(BF16), w2b_ref[...]) + b2_ref[0])

    step = xs_ref.shape[0] // ROW_PARTS
    for nq in range(1, ROW_PARTS + 1):
        pl.when((nvalid > (nq - 1) * step) & (nvalid <= nq * step))(functools.partial(ffn, nq * step))


def _experts(blk_e, blk_v, blk_f, blk_n, blk_s, blk_r, xs, w1, b1, w2, b2):
    P, W = xs.shape
    nb = P // MOE_BLOCK
    E, D, F2 = w1.shape
    grid_spec = pltpu.PrefetchScalarGridSpec(
        num_scalar_prefetch=6,
        grid=(nb,),
        in_specs=[
            pl.BlockSpec((MOE_BLOCK, W), lambda i, be, bv, bf, bn, bs, br: (br[i], 0)),
            pl.BlockSpec(memory_space=pl.ANY),
            pl.BlockSpec((1, 1, F2), lambda i, be, bv, bf, bn, bs, br: (be[i], 0, 0)),
            pl.BlockSpec(memory_space=pl.ANY),
            pl.BlockSpec((1, 1, D), lambda i, be, bv, bf, bn, bs, br: (be[i], 0, 0)),
        ],
        out_specs=pl.BlockSpec((MOE_BLOCK, D // 2), lambda i, be, bv, bf, bn, bs, br: (br[i], 0)),
        scratch_shapes=[
            pltpu.VMEM((2, D, F2), F32),
            pltpu.VMEM((2, D_FF, D), F32),
            pltpu.VMEM((D, F2), BF16),
            pltpu.VMEM((D_FF, D), BF16),
            pltpu.SemaphoreType.DMA((2, 2)),
        ],
    )

    def kern(be_ref, bv_ref, bf_ref, bn_ref, bs_ref, br_ref, *refs):
        del br_ref
        _experts_kernel(be_ref, bv_ref, bf_ref, bn_ref, bs_ref, *refs)

    return pl.pallas_call(
        kern,
        grid_spec=grid_spec,
        out_shape=jax.ShapeDtypeStruct((P, D // 2), jnp.int32),
        compiler_params=_cparams(("arbitrary",)),
        name="experts",
    )(blk_e, blk_v, blk_f, blk_n, blk_s, blk_r, xs, w1, b1.reshape(E, 1, F2), w2,
      b2.reshape(E, 1, D))


def _combine_kernel(x1_ref, gate_ref, ysum_ref, *rest):
    o_ref = rest[-1]
    o_ref[...] = x1_ref[...] + gate_ref[0] * ysum_ref[...]


def _combine(x1, mod3, ysum, out_prev, b0, B, nb_total, S, tm):
    D = x1.shape[1]
    ns = S // tm
    row = lambda b, s: (b * ns + s, 0)
    row_out = lambda b, s: ((b0 + b) * ns + s, 0)
    in_specs = [
        pl.BlockSpec((tm, D), row),
        pl.BlockSpec((1, 1, D), lambda b, s: ((b0 + b) * 6 + 5, 0, 0)),
        pl.BlockSpec((tm, D), row),
    ]
    args = [x1, mod3, ysum]
    aliases = {}
    if out_prev is not None:
        in_specs.append(pl.BlockSpec(memory_space=pl.ANY))
        args.append(out_prev)
        aliases = {len(args) - 1: 0}
    return pl.pallas_call(
        _combine_kernel,
        grid=(B, ns),
        in_specs=in_specs,
        out_specs=pl.BlockSpec((tm, D), row_out),
        out_shape=jax.ShapeDtypeStruct((nb_total * S, D), F32),
        input_output_aliases=aliases,
        compiler_params=_cparams(("arbitrary", "arbitrary")),
        name="combine",
    )(*args)


def _block_diag(w):
    n, c, d = w.shape
    eye = jnp.eye(n, dtype=w.dtype)
    return jnp.einsum("ncd,nm->ncmd", w, eye).reshape(n * c, n * d)


def _pad_heads(w, width):
    k = w.shape[0]
    w = w.reshape(k, N_HEADS, width)
    return jnp.pad(w, ((0, 0), (0, 0), (0, HEAD_PAD - width))).reshape(k, N_HEADS * HEAD_PAD)


def kernel(x, c, positions, w_ada, b_ada, g_mix, w_in, conv_w, conv_b, w_a, b_a, w_x, b_x, lam,
           g_q_lat, w_uq, g_kv_lat, w_ukv, g_qn, g_kn, w_out, g_ffn, w_router, b_router,
           w1, b1, w2, b2):
    B, S, D = x.shape
    T = B * S
    depth = w_ada.shape[0]
    tm_in = min(512, S)
    tk_att = min(512, S // 2)
    tm_out = min(1024, S)
    tm_comb = min(1024, S)
    n_groups = min(MAX_GROUPS, B)
    group_sizes = [B // n_groups + (1 if gi < B % n_groups else 0) for gi in range(n_groups)]

    o1 = 2 * D_LRU
    o2 = o1 + Q_LORA
    o3 = o2 + KV_LORA
    tri = jnp.asarray(np.arange(tm_out)[:, None] < np.arange(tm_out)[None, :], BF16)
    cos_t, sin_t = _rope_tables(positions)

    def rot_cols(w):
        k = w.shape[0]
        w3 = w.reshape(k, -1, HEAD_PAD)
        lo = w3[:, :, ROPE_LO:ROPE_LO + ROPE_HALF]
        hi = w3[:, :, ROPE_LO + ROPE_HALF:ROPE_LO + QK_ROPE]
        zl = jnp.zeros_like(w3[:, :, :ROPE_LO])
        zr = jnp.zeros_like(w3[:, :, ROPE_LO + QK_ROPE:])
        return jnp.concatenate([zl, hi, lo, zr], axis=2).reshape(w.shape)

    x2 = x.reshape(T, D)
    for l in range(depth):
        mod3 = _ada(c, w_ada[l], b_ada[l]).reshape(B * 6, 1, D)

        w_in_l = w_in[l]
        kr_cols = jnp.pad(w_in_l[:, o3:], ((0, 0), (ROPE_LO, LANES - ROPE_LO - QK_ROPE)))
        w_in_p = jnp.concatenate([w_in_l[:, :o3], kr_cols, rot_cols(kr_cols)], axis=1).astype(BF16)
        w_uq_h = _pad_heads(w_uq[l], QK_HEAD)
        w_uq_p = jnp.concatenate([w_uq_h, rot_cols(w_uq_h)], axis=1).astype(BF16)
        w_ukv_l = w_ukv[l].reshape(KV_LORA, N_HEADS, QK_NOPE + V_HEAD)
        w_uk_h = _pad_heads(w_ukv_l[:, :, :QK_NOPE].reshape(KV_LORA, N_HEADS * QK_NOPE), QK_NOPE)
        w_uv_pairs = w_ukv_l[:, :, QK_NOPE:].reshape(KV_LORA, N_HEADS // 2, 2, V_HEAD)
        zero_v = jnp.zeros_like(w_uv_pairs[:, :, 0])
        w_uv_h = jnp.stack([w_uv_pairs[:, :, 0], zero_v, zero_v, w_uv_pairs[:, :, 1]],
                           axis=2).reshape(KV_LORA, N_HEADS * HEAD_PAD)
        w_ukv_p = jnp.concatenate([w_uk_h, w_uv_h], axis=1).astype(BF16)
        gqn_p = jnp.pad(g_qn[l], (0, HEAD_PAD - QK_HEAD)).reshape(1, HEAD_PAD)
        gkn_p = jnp.pad(g_kn[l], (0, HEAD_PAD - QK_HEAD)).reshape(1, HEAD_PAD)

        lru_o, qp, kp, v = _inproj(
            x2, cos_t, sin_t, mod3, g_mix[l].reshape(1, D), w_in_p, g_q_lat[l].reshape(1, Q_LORA),
            w_uq_p, g_kv_lat[l].reshape(1, KV_LORA), w_ukv_p, gqn_p, rot_cols(gqn_p), gkn_p,
            rot_cols(gkn_p), conv_w[l], conv_b[l].reshape(1, D_LRU),
            _block_diag(w_a[l]).astype(BF16), b_a[l].reshape(1, D_LRU),
            _block_diag(w_x[l]).astype(BF16), b_x[l].reshape(1, D_LRU),
            lam[l].reshape(1, D_LRU), B, S, tm_in)

        att_o = _attn(qp, kp, v, B, S, tk_att)

        w_out_b = w_out[l].astype(BF16)
        wr_stack = jnp.concatenate(_split_bf16(w_router[l].T), axis=0)
        g_ffn_l = g_ffn[l].reshape(1, D)
        b_r = b_router[l].reshape(N_EXPERTS, 1)
        eio = np.arange(N_EXPERTS, dtype=np.int32)

        x_next = None
        b0 = 0
        for Bg in group_sizes:
            Tg = Bg * S
            n_blocks = -(-(Tg * TOP_K) // MOE_BLOCK) + N_EXPERTS
            bi = np.arange(n_blocks, dtype=np.int32)
            g_disp = min(64, Tg // SC_WORKERS // 2)
            g_comb = min(16, Tg // SC_WORKERS // 2)
            x1, h2p, idx_t, gat_t, rank_t, counts = _outproj(
                lru_o, att_o, x2, mod3, g_ffn_l, w_out_b[:D_LRU], w_out_b[D_LRU:],
                wr_stack, b_r, tri, b0, Bg, S, tm_out)

            counts = counts.reshape(N_EXPERTS)
            nblk_e = (counts + MOE_BLOCK - 1) // MOE_BLOCK
            blk_end = jnp.cumsum(nblk_e)
            pad_start = (blk_end - nblk_e) * MOE_BLOCK
            total = blk_end[-1]
            blk_r = jnp.minimum(bi, total - 1).astype(jnp.int32)
            blk_e = jnp.minimum(jnp.sum(blk_end[None, :] <= blk_r[:, None], axis=1),
                                N_EXPERTS - 1).astype(jnp.int32)
            blk_onehot = blk_e[:, None] == eio[None, :]
            blk_first = jnp.sum(jnp.where(blk_onehot, (blk_end - nblk_e)[None, :], 0), axis=1)
            blk_cnt = jnp.sum(jnp.where(blk_onehot, counts[None, :], 0), axis=1)
            blk_v = jnp.where(bi < total,
                              jnp.clip(blk_cnt - (bi - blk_first) * MOE_BLOCK, 0, MOE_BLOCK),
                              0).astype(jnp.int32)
            blk_f = ((bi == blk_first) & (bi < total)).astype(jnp.int32)
            nxt_first = jnp.sum(jnp.where(blk_onehot, blk_end[None, :], 0), axis=1)
            nxt_e = jnp.minimum(jnp.sum(blk_end[None, :] <= nxt_first[:, None], axis=1),
                                N_EXPERTS - 1)
            blk_n = jnp.where(nxt_first < total, nxt_e, -1).astype(jnp.int32)
            ordinal = jnp.cumsum((nblk_e > 0).astype(jnp.int32)) - 1
            blk_s = (jnp.sum(jnp.where(blk_onehot, ordinal[None, :], 0), axis=1) % 2).astype(jnp.int32)
            slot0 = jnp.sum(jnp.where(idx_t[None] == eio[:, None, None],
                                      pad_start[:, None, None], 0), axis=0)
            dest = slot0.astype(jnp.int32) + rank_t

            xs = _sc_scatter_rows(h2p, dest, n_blocks * MOE_BLOCK, g_disp)
            ys = _experts(blk_e, blk_v, blk_f, blk_n, blk_s, blk_r, xs, w1[l], b1[l], w2[l], b2[l])
            ysum = _sc_gather_gated_sum(ys, dest, gat_t, g_comb)
            x_next = _combine(x1, mod3, ysum, x_next, b0, Bg, B, S, tm_comb)
            b0 += Bg
        x2 = x_next
    return x2.reshape(B, S, D)
```

```python
import functools

import jax
import jax.numpy as jnp
import numpy as np
from jax import lax
from jax.experimental import pallas as pl
from jax.experimental.pallas import tpu as pltpu
from jax.experimental.pallas import tpu_sc as plsc

D_LRU = 512
CONV_W = 4
LRU_C = 8.0
N_HEADS = 8
QK_NOPE = 64
QK_ROPE = 32
QK_HEAD = 96
V_HEAD = 64
Q_LORA = 256
KV_LORA = 128
ROPE_THETA = 10000.0
N_EXPERTS = 32
TOP_K = 4
D_FF = 1024
SWIGLU_LIMIT = 7.0
SWIGLU_ALPHA = 1.702
MOE_BLOCK = 1024
EPS = 1e-6

LANES = 128
SUBLANES = 8
HEAD_PAD = 128
ROPE_LO = QK_NOPE
ROPE_HALF = QK_ROPE // 2
TOK_PER_ROW = LANES // ROPE_HALF
D_IN_PAD = 2 * D_LRU + Q_LORA + KV_LORA + 2 * LANES
LOG2_E = 1.4426950408889634
MAX_GROUPS = 2
ROW_PARTS = 8

VMEM_LIMIT = 56 * 1024 * 1024

F32 = jnp.float32
BF16 = jnp.bfloat16


def _cparams(sem):
    return pltpu.CompilerParams(dimension_semantics=sem, vmem_limit_bytes=VMEM_LIMIT)


def _dot(a, b):
    return jnp.dot(a, b, preferred_element_type=F32)


def _dot_nt(a, b):
    return lax.dot_general(a, b, (((1,), (1,)), ((), ())), preferred_element_type=F32)


def _split_bf16(a):
    hi = a.astype(BF16)
    lo = (a - hi.astype(F32)).astype(BF16)
    return hi, lo


def _sigmoid(x, scale=1.0):
    return 1.0 / (1.0 + jnp.exp2(x * (-scale * LOG2_E)))


def _pack_halves(x):
    bits = lax.bitcast_convert_type(x.astype(BF16).astype(F32), jnp.uint32)
    half = x.shape[1] // 2
    words = (bits[:, :half] >> 16) | (bits[:, half:] & jnp.uint32(0xFFFF0000))
    return lax.bitcast_convert_type(words, jnp.int32)


def _unpack_halves(words):
    w = lax.bitcast_convert_type(words, jnp.uint32)
    lo = lax.bitcast_convert_type(w << 16, F32)
    hi = lax.bitcast_convert_type(w & jnp.uint32(0xFFFF0000), F32)
    return lo, hi


def _ada_kernel(c_ref, w_ref, b_ref, o_ref):
    c = c_ref[...]
    s = c * _sigmoid(c)
    shi, slo = _split_bf16(s)
    whi, wlo = _split_bf16(w_ref[...])
    o_ref[...] = _dot(shi, whi) + _dot(slo, whi) + _dot(shi, wlo) + b_ref[...]


def _ada(c, w_ada, b_ada):
    B, D = c.shape
    N = w_ada.shape[1]
    tn = 1024
    return pl.pallas_call(
        _ada_kernel,
        grid=(N // tn,),
        in_specs=[
            pl.BlockSpec((B, D), lambda j: (0, 0)),
            pl.BlockSpec((D, tn), lambda j: (0, j)),
            pl.BlockSpec((1, tn), lambda j: (0, j)),
        ],
        out_specs=pl.BlockSpec((B, tn), lambda j: (0, j)),
        out_shape=jax.ShapeDtypeStruct((B, N), F32),
        compiler_params=_cparams(("arbitrary",)),
        name="ada",
    )(c, w_ada, b_ada.reshape(1, N))


def _trig_kernel(pos_ref, freq_ref, rsel_ref, fold_ref, cbase_ref, cos_ref, sin_ref):
    ang = pos_ref[...].astype(F32) * freq_ref[...]
    cs = jnp.concatenate([jnp.cos(ang), jnp.sin(ang)], axis=1)
    tm = cos_ref.shape[0]
    row = lax.broadcasted_iota(jnp.int32, (tm, 2 * LANES), 0)
    lane = lax.broadcasted_iota(jnp.int32, (tm, 2 * LANES), 1)
    own = ((lane % LANES) // ROPE_HALF) == (row % TOK_PER_ROW)
    rsel = rsel_ref[...]
    fold = fold_ref[...]
    by_row = sum(_dot(rsel, part) for part in _split_bf16(cs))
    mine = jnp.where(own, by_row, 0.0)
    out = sum(_dot(part, fold) for part in _split_bf16(mine))
    cos_ref[...] = out[:, :LANES] + cbase_ref[...]
    sin_ref[...] = out[:, LANES:]


def _rope_tables(positions):
    T = positions.size
    rows = T // TOK_PER_ROW
    pos_c = jnp.repeat(positions.reshape(T).astype(jnp.int32), ROPE_HALF).reshape(rows, LANES)
    tm = min(2048, T)
    tr = tm // TOK_PER_ROW
    freqs = np.float32(ROPE_THETA) ** (-np.arange(ROPE_HALF, dtype=np.float32) / np.float32(ROPE_HALF))
    freq_c = np.tile(freqs.astype(np.float32), TOK_PER_ROW).reshape(1, LANES)
    rsel = jnp.asarray(np.arange(tm)[:, None] // TOK_PER_ROW == np.arange(tr)[None, :], BF16)
    src = np.arange(LANES)[:, None] % ROPE_HALF
    dst = np.arange(LANES)[None, :]
    first = dst == ROPE_LO + src
    second = dst == ROPE_LO + ROPE_HALF + src
    fcos = (first | second).astype(np.float32)
    fsin = second.astype(np.float32) - first.astype(np.float32)
    zero = np.zeros((LANES, LANES), np.float32)
    fold = jnp.asarray(np.block([[fcos, zero], [zero, fsin]]), BF16)
    lane = np.arange(LANES)
    cbase = ((lane < ROPE_LO) | (lane >= ROPE_LO + QK_ROPE)).astype(np.float32).reshape(1, LANES)
    full = lambda i: (0, 0)
    return pl.pallas_call(
        _trig_kernel,
        grid=(T // tm,),
        in_specs=[
            pl.BlockSpec((tr, LANES), lambda i: (i, 0)),
            pl.BlockSpec((1, LANES), full),
            pl.BlockSpec((tm, tr), full),
            pl.BlockSpec((2 * LANES, 2 * LANES), full),
            pl.BlockSpec((1, LANES), full),
        ],
        out_specs=[pl.BlockSpec((tm, LANES), lambda i: (i, 0))] * 2,
        out_shape=[jax.ShapeDtypeStruct((T, LANES), F32)] * 2,
        compiler_params=_cparams(("arbitrary",)),
        name="rope_trig",
    )(pos_c, freq_c, rsel, fold, cbase)


def _inproj_kernel(x_ref, cos_ref, sin_ref, shift_ref, scale_ref, gmix_ref, win_ref, gq_ref, wuq_ref,
                   gkv_ref, wukv_ref, gqn_ref, gqr_ref, gkn_ref, gkr_ref,
                   cw_ref, cb_ref, wa_ref, ba_ref, wx_ref, bx_ref, lam_ref,
                   lru_ref, q_ref, k_ref, v_ref, tail_ref, carry_ref):
    HP = N_HEADS * HEAD_PAD

    @pl.when(pl.program_id(1) == 0)
    def _():
        tail_ref[...] = jnp.zeros_like(tail_ref)
        carry_ref[...] = jnp.zeros_like(carry_ref)

    x = x_ref[...]
    ms = jnp.mean(x * x, axis=-1, keepdims=True)
    gain = gmix_ref[...] * (1.0 + scale_ref[0])
    h = x * lax.rsqrt(ms + EPS) * gain + shift_ref[0]
    hb = h.astype(BF16)
    o1 = 2 * D_LRU
    o2 = Q_LORA
    o3 = o2 + KV_LORA
    z_lru = _dot(hb, win_ref[:, :o1])
    z = _dot(hb, win_ref[:, o1:])
    lru_ref[...] = _lru_tile(z_lru[:, :D_LRU], z_lru[:, D_LRU:], cw_ref, cb_ref, wa_ref, ba_ref,
                             wx_ref, bx_ref, lam_ref, tail_ref, carry_ref)
    ql = z[:, :o2]
    kvl = z[:, o2:o3]
    kr = z[:, o3:o3 + LANES]
    kr_rot = z[:, o3 + LANES:]

    qn = ql * lax.rsqrt(jnp.mean(ql * ql, axis=-1, keepdims=True) + EPS) * gq_ref[...]
    qq = _dot(qn.astype(BF16), wuq_ref[...])
    kvn = kvl * lax.rsqrt(jnp.mean(kvl * kvl, axis=-1, keepdims=True) + EPS) * gkv_ref[...]
    kv = _dot(kvn.astype(BF16), wukv_ref[...])

    tm = x.shape[0]
    lane = lax.broadcasted_iota(jnp.int32, (tm, HP), 1)
    pair_lane = lane & (2 * HEAD_PAD - 1)
    ones_cols = (pair_lane >= V_HEAD) & (pair_lane < 2 * HEAD_PAD - V_HEAD)
    v_ref[...] = jnp.where(ones_cols, 1.0, kv[:, HP:]).astype(BF16)

    cos_t = cos_ref[...]
    sin_t = sin_ref[...]
    gqn = gqn_ref[...]
    gkn = gkn_ref[...]
    cq = gqn * cos_t
    sq = gqr_ref[...] * sin_t
    kb = kr * (gkn * cos_t) + kr_rot * (gkr_ref[...] * sin_t)
    inv_w = 1.0 / QK_HEAD
    qscale = QK_HEAD ** -0.5 * LOG2_E
    for hh in range(N_HEADS):
        sl = slice(hh * HEAD_PAD, (hh + 1) * HEAD_PAD)
        qh = qq[:, sl]
        rq = lax.rsqrt(jnp.sum(qh * qh, axis=-1, keepdims=True) * inv_w + EPS) * qscale
        q_ref[:, sl] = ((qh * cq + qq[:, HP + hh * HEAD_PAD:HP + (hh + 1) * HEAD_PAD] * sq) * rq).astype(BF16)
        kraw = kv[:, sl] + kr
        rk = lax.rsqrt(jnp.sum(kraw * kraw, axis=-1, keepdims=True) * inv_w + EPS)
        k_ref[:, sl] = ((kv[:, sl] * gkn + kb) * rk).astype(BF16)


def _inproj(x2, cos_t, sin_t, mod3, g_mix, w_in_p, g_q_lat, w_uq_p, g_kv_lat, w_ukv_p,
            gqn_p, gqr_p, gkn_p, gkr_p, conv_w, conv_b, wa_d, b_a, wx_d, b_x, lam, B, S, tm):
    T, D = x2.shape
    ns = S // tm
    HP = N_HEADS * HEAD_PAD
    C = D_LRU
    row = lambda b, s: (b * ns + s, 0)
    full = lambda b, s: (0, 0)
    return pl.pallas_call(
        _inproj_kernel,
        grid=(B, ns),
        in_specs=[
            pl.BlockSpec((tm, D), row),
            pl.BlockSpec((tm, LANES), row),
            pl.BlockSpec((tm, LANES), row),
            pl.BlockSpec((1, 1, D), lambda b, s: (b * 6 + 0, 0, 0)),
            pl.BlockSpec((1, 1, D), lambda b, s: (b * 6 + 1, 0, 0)),
            pl.BlockSpec((1, D), full),
            pl.BlockSpec((D, D_IN_PAD), full),
            pl.BlockSpec((1, Q_LORA), full),
            pl.BlockSpec((Q_LORA, 2 * HP), full),
            pl.BlockSpec((1, KV_LORA), full),
            pl.BlockSpec((KV_LORA, 2 * HP), full),
            pl.BlockSpec((1, HEAD_PAD), full),
            pl.BlockSpec((1, HEAD_PAD), full),
            pl.BlockSpec((1, HEAD_PAD), full),
            pl.BlockSpec((1, HEAD_PAD), full),
            pl.BlockSpec((CONV_W, C), full),
            pl.BlockSpec((1, C), full),
            pl.BlockSpec((C, C), full),
            pl.BlockSpec((1, C), full),
            pl.BlockSpec((C, C), full),
            pl.BlockSpec((1, C), full),
            pl.BlockSpec((1, C), full),
        ],
        out_specs=[
            pl.BlockSpec((tm, C), row),
            pl.BlockSpec((tm, HP), row),
            pl.BlockSpec((tm, HP), row),
            pl.BlockSpec((tm, HP), row),
        ],
        out_shape=[
            jax.ShapeDtypeStruct((T, C), BF16),
            jax.ShapeDtypeStruct((T, HP), BF16),
            jax.ShapeDtypeStruct((T, HP), BF16),
            jax.ShapeDtypeStruct((T, HP), BF16),
        ],
        scratch_shapes=[pltpu.VMEM((SUBLANES, C), F32), pltpu.VMEM((SUBLANES, C), F32)],
        compiler_params=_cparams(("arbitrary", "arbitrary")),
        name="inproj",
    )(x2, cos_t, sin_t, mod3, mod3, g_mix, w_in_p, g_q_lat, w_uq_p, g_kv_lat, w_ukv_p,
      gqn_p, gqr_p, gkn_p, gkr_p, conv_w, conv_b, wa_d, b_a, wx_d, b_x, lam)


def _gelu_tanh(x):
    c = 0.7978845608028654
    hx = 0.5 * x
    return hx + hx * jnp.tanh(x * (c + (c * 0.044715) * (x * x)))


def _lru_tile(x, y, cw_ref, cb_ref, wa_ref, ba_ref, wx_ref, bx_ref, lam_ref, tail_ref, carry_ref):
    ts = x.shape[0]
    xext = jnp.concatenate([tail_ref[...], x], axis=0)
    cw = cw_ref[...]
    xc = x * cw[CONV_W - 1:CONV_W, :]
    for j in range(CONV_W - 1):
        sh = CONV_W - 1 - j
        xc = xc + xext[8 - sh:8 - sh + ts, :] * cw[j:j + 1, :]
    xc = xc + cb_ref[...]
    tail_ref[...] = x[ts - 8:, :]

    xb = xc.astype(BF16)
    r = _sigmoid(_dot(xb, wa_ref[...]) + ba_ref[...])
    i = _sigmoid(_dot(xb, wx_ref[...]) + bx_ref[...])
    lam = lam_ref[...]
    nl = -lam
    softplus = jnp.maximum(nl, 0.0) + jnp.log(1.0 + jnp.exp(-jnp.abs(nl)))
    log_a = (-LRU_C) * r * softplus
    a = jnp.exp(log_a)
    mult = jnp.sqrt(1.0 - a * a)
    u = mult * (i * xc)

    C = a.shape[1]
    a = a.reshape(ts // SUBLANES, SUBLANES, C)
    u = u.reshape(ts // SUBLANES, SUBLANES, C)
    sub = lax.broadcasted_iota(jnp.int32, (1, SUBLANES, 1), 1)
    sh = 1
    while sh < SUBLANES:
        a_prev = pltpu.roll(a, sh, axis=1)
        u_prev = pltpu.roll(u, sh, axis=1)
        m = sub >= sh
        u = jnp.where(m, a * u_prev + u, u)
        a = jnp.where(m, a * a_prev, a)
        sh *= 2
    a = a.reshape(ts, C)
    u = u.reshape(ts, C)
    h = carry_ref[0:1, :]
    groups = []
    for g0 in range(0, ts, SUBLANES):
        hg = u[g0:g0 + SUBLANES, :] + a[g0:g0 + SUBLANES, :] * h
        groups.append(hg)
        h = hg[SUBLANES - 1:SUBLANES, :]
    carry_ref[...] = jnp.broadcast_to(h, carry_ref.shape)
    hs = jnp.concatenate(groups, axis=0)
    return (_gelu_tanh(y) * hs).astype(BF16)


NEG_INF = -1e30


def _attn_kernel(q_ref, k_ref, v_ref, o_ref, *state, tq, tk):
    m_refs = state[:N_HEADS]
    acc_refs = state[N_HEADS:]
    qi = pl.program_id(1)
    causal = (lax.broadcasted_iota(jnp.int32, (tk, tk), 1)
              <= lax.broadcasted_iota(jnp.int32, (tk, tk), 0))
    lower = slice(tk, tq)

    def head_slice(hh):
        return slice(hh * HEAD_PAD, (hh + 1) * HEAD_PAD)

    def weights(sc, m_b):
        cols = [jnp.exp2(sc[:, c0:c0 + LANES] - m_b) for c0 in range(0, tk, LANES)]
        return jnp.concatenate(cols, axis=1).astype(BF16)

    def scores(hh, r0, rows):
        hs = head_slice(hh)
        return _dot_nt(q_ref[rows, hs], k_ref[pl.ds(r0, tk), hs])

    def update(hh, rows, sc, r0):
        hs = head_slice(hh)
        m_b = m_refs[hh][rows, :]
        m_new = jnp.maximum(m_b, jnp.max(sc, axis=-1, keepdims=True))
        alpha = jnp.exp2(m_b - m_new)
        m_refs[hh][rows, :] = m_new
        acc_refs[hh][rows, :] = (alpha * acc_refs[hh][rows, :]
                                 + _dot(weights(sc, m_new), v_ref[pl.ds(r0, tk), hs]))


    every = slice(0, tq)
    r_d0 = pl.multiple_of(qi * tq, tk)
    sc_next = scores(0, r_d0, every)
    for hh in range(N_HEADS):
        hs = head_slice(hh)
        sc = jnp.concatenate([jnp.where(causal, sc_next[:tk], NEG_INF), sc_next[tk:]], axis=0)
        if hh + 1 < N_HEADS:
            sc_next = scores(hh + 1, r_d0, every)
        m_b = jnp.broadcast_to(jnp.max(sc, axis=-1, keepdims=True), (tq, LANES))
        m_refs[hh][...] = m_b
        acc_refs[hh][...] = _dot(weights(sc, m_b), v_ref[pl.ds(r_d0, tk), hs])

    r_d1 = pl.multiple_of(qi * tq + tk, tk)
    sc_next = scores(0, r_d1, lower)
    for hh in range(N_HEADS):
        sc = jnp.where(causal, sc_next, NEG_INF)
        if hh + 1 < N_HEADS:
            sc_next = scores(hh + 1, r_d1, lower)
        update(hh, lower, sc, r_d1)

    @pl.loop(0, qi * (tq // tk))
    def _(j):
        r0 = pl.multiple_of(j * tk, tk)
        sc_next = scores(0, r0, every)
        for hh in range(N_HEADS):
            sc = sc_next
            if hh + 1 < N_HEADS:
                sc_next = scores(hh + 1, r0, every)
            update(hh, every, sc, r0)

    low = lax.broadcasted_iota(jnp.int32, (tq, HEAD_PAD), 1) < V_HEAD
    for he in range(0, N_HEADS, 2):
        acc_e = acc_refs[he][...]
        acc_o = acc_refs[he + 1][...]
        num = jnp.where(low, acc_e, acc_o)
        den = pltpu.roll(jnp.where(low, acc_o, acc_e), V_HEAD, axis=1)
        o_ref[:, he * V_HEAD:(he + 2) * V_HEAD] = (num / den).astype(BF16)


def _attn(qp, kp, v, B, S, tk):
    T = qp.shape[0]
    tq = 2 * tk
    nq = S // tq
    HP = N_HEADS * HEAD_PAD
    HV = N_HEADS * V_HEAD
    return pl.pallas_call(
        functools.partial(_attn_kernel, tq=tq, tk=tk),
        grid=(B, nq),
        in_specs=[
            pl.BlockSpec((tq, HP), lambda b, i: (b * nq + i, 0)),
            pl.BlockSpec((S, HP), lambda b, i: (b, 0)),
            pl.BlockSpec((S, HP), lambda b, i: (b, 0)),
        ],
        out_specs=pl.BlockSpec((tq, HV), lambda b, i: (b * nq + i, 0)),
        out_shape=jax.ShapeDtypeStruct((T, HV), BF16),
        scratch_shapes=([pltpu.VMEM((tq, LANES), F32)] * N_HEADS
                        + [pltpu.VMEM((tq, HEAD_PAD), F32)] * N_HEADS),
        compiler_params=_cparams(("arbitrary", "arbitrary")),
        name="attn",
    )(qp, kp, v)


def _outproj_kernel(lru_ref, att_ref, x_ref, gate_ref, shift_ref, scale_ref, gffn_ref,
                    wo1_ref, wo2_ref, wr_ref, br_ref, tri_ref,
                    x1_ref, h2p_ref, idx_ref, gat_ref, rank_ref, cnt_ref, run_ref):
    first = (pl.program_id(0) == 0) & (pl.program_id(1) == 0)

    @pl.when(first)
    def _():
        run_ref[...] = jnp.zeros_like(run_ref)

    mix = _dot(lru_ref[...], wo1_ref[...]) + _dot(att_ref[...], wo2_ref[...])
    x1 = x_ref[...] + gate_ref[0] * mix
    x1_ref[...] = x1
    ms = jnp.mean(x1 * x1, axis=-1, keepdims=True)
    gain = gffn_ref[...] * (1.0 + scale_ref[0])
    h2 = x1 * lax.rsqrt(ms + EPS) * gain + shift_ref[0]

    hhi = h2.astype(BF16)
    hlo = (h2 - hhi.astype(F32)).astype(BF16)
    h2p_ref[...] = _pack_halves(h2)

    ne = br_ref.shape[0]
    stacked = _dot_nt(wr_ref[...], hhi)
    logits = stacked[:ne] + stacked[ne:] + _dot_nt(wr_ref[:ne, :], hlo) + br_ref[...]

    tm = logits.shape[1]
    eio = lax.broadcasted_iota(jnp.int32, (ne, tm), 0)
    vals, idxs, sels = [], [], []
    l = logits
    for _ in range(TOP_K):
        m = jnp.max(l, axis=0, keepdims=True)
        idx = jnp.min(jnp.where(l == m, eio, ne), axis=0, keepdims=True)
        sel = eio == idx
        l = jnp.where(sel, -jnp.inf, l)
        vals.append(m)
        idxs.append(idx)
        sels.append(sel)
    es = [jnp.exp(v - vals[0]) for v in vals]
    inv = 1.0 / (es[0] + es[1] + es[2] + es[3])
    sel_any = jnp.where(sels[0] | sels[1] | sels[2] | sels[3], 1.0, 0.0)
    run = run_ref[...]
    excl = _dot(sel_any.astype(BF16), tri_ref[...]) + run
    for kk in range(TOP_K):
        idx_ref[kk:kk + 1, :] = idxs[kk]
        gat_ref[kk:kk + 1, :] = es[kk] * inv
        rk = jnp.sum(jnp.where(sels[kk], excl, 0.0), axis=0, keepdims=True)
        rank_ref[kk:kk + 1, :] = rk.astype(jnp.int32)
    run = run + jnp.sum(sel_any, axis=1, keepdims=True)
    run_ref[...] = run
    cnt_ref[...] = run.astype(jnp.int32)


def _outproj(lru_o, att_o, x2, mod3, g_ffn, wo1, wo2, wr_stack, b_r, tri, b0, B, S, tm):
    D = x2.shape[1]
    T = B * S
    ns = S // tm
    C = lru_o.shape[1]
    row_in = lambda b, s: ((b0 + b) * ns + s, 0)
    row = lambda b, s: (b * ns + s, 0)
    col = lambda b, s: (0, b * ns + s)
    full = lambda b, s: (0, 0)
    return pl.pallas_call(
        _outproj_kernel,
        grid=(B, ns),
        in_specs=[
            pl.BlockSpec((tm, C), row_in),
            pl.BlockSpec((tm, C), row_in),
            pl.BlockSpec((tm, D), row_in),
            pl.BlockSpec((1, 1, D), lambda b, s: ((b0 + b) * 6 + 2, 0, 0)),
            pl.BlockSpec((1, 1, D), lambda b, s: ((b0 + b) * 6 + 3, 0, 0)),
            pl.BlockSpec((1, 1, D), lambda b, s: ((b0 + b) * 6 + 4, 0, 0)),
            pl.BlockSpec((1, D), full),
            pl.BlockSpec((C, D), full),
            pl.BlockSpec((C, D), full),
            pl.BlockSpec((2 * N_EXPERTS, D), full),
            pl.BlockSpec((N_EXPERTS, 1), full),
            pl.BlockSpec(tri.shape, full),
        ],
        out_specs=[
            pl.BlockSpec((tm, D), row),
            pl.BlockSpec((tm, D // 2), row),
            pl.BlockSpec((TOP_K, tm), col),
            pl.BlockSpec((TOP_K, tm), col),
            pl.BlockSpec((TOP_K, tm), col),
            pl.BlockSpec((N_EXPERTS, 1), full),
        ],
        out_shape=[
            jax.ShapeDtypeStruct((T, D), F32),
            jax.ShapeDtypeStruct((T, D // 2), jnp.int32),
            jax.ShapeDtypeStruct((TOP_K, T), jnp.int32),
            jax.ShapeDtypeStruct((TOP_K, T), F32),
            jax.ShapeDtypeStruct((TOP_K, T), jnp.int32),
            jax.ShapeDtypeStruct((N_EXPERTS, 1), jnp.int32),
        ],
        scratch_shapes=[pltpu.VMEM((N_EXPERTS, 1), F32)],
        compiler_params=_cparams(("arbitrary", "arbitrary")),
        name="outproj",
    )(lru_o, att_o, x2, mod3, mod3, mod3, g_ffn, wo1, wo2, wr_stack, b_r, tri)


SC_CORES = 2
SC_SUBCORES = 16
SC_WORKERS = SC_CORES * SC_SUBCORES
SC_LANES = 16


def _sc_mesh():
    return plsc.VectorSubcoreMesh(core_axis_name="c", subcore_axis_name="s",
                                  num_cores=SC_CORES, num_subcores=SC_SUBCORES)


def _sc_worker_id():
    return lax.axis_index("s") * SC_CORES + lax.axis_index("c")


def _sc_scatter_rows(rows, idx, n_out, g):
    T, W = rows.shape
    K = idx.shape[0]
    per_w = T // SC_WORKERS
    nch = per_w // g
    assert per_w * SC_WORKERS == T and nch * g == per_w and nch % 2 == 0
    idx_w = idx.reshape(K, SC_WORKERS, nch, g).transpose(1, 2, 0, 3).reshape(SC_WORKERS, nch * K, g)

    def body(rows_hbm, idx_hbm, out_hbm, idx_v, buf0, buf1, semr0, semr1, semw):
        wid = _sc_worker_id()
        base = wid * per_w
        pltpu.sync_copy(idx_hbm.at[wid], idx_v)

        def read(j, buf, sem):
            return pltpu.make_async_copy(rows_hbm.at[pl.ds(base + j * g, g)], buf, sem)

        def scatter(j, buf):
            copies = [pltpu.async_copy(buf, out_hbm.at[idx_v.at[j * K + kk]], semw)
                      for kk in range(K)]
            for cp in copies:
                cp.wait()

        read(0, buf0, semr0).start()

        @pl.loop(0, nch // 2)
        def _(jj):
            j0 = 2 * jj
            read(j0 + 1, buf1, semr1).start()
            read(j0, buf0, semr0).wait()
            scatter(j0, buf0)

            @pl.when(j0 + 2 < nch)
            def _():
                read(j0 + 2, buf0, semr0).start()

            read(j0 + 1, buf1, semr1).wait()
            scatter(j0 + 1, buf1)

    return pl.kernel(
        body,
        out_type=jax.ShapeDtypeStruct((n_out, W), rows.dtype),
        mesh=_sc_mesh(),
        scratch_types=[
            pltpu.VMEM((nch * K, g), jnp.int32),
            pltpu.VMEM((g, W), rows.dtype),
            pltpu.VMEM((g, W), rows.dtype),
            pltpu.SemaphoreType.DMA,
            pltpu.SemaphoreType.DMA,
            pltpu.SemaphoreType.DMA,
        ],
        name="sc_scatter_rows",
    )(rows, idx_w)


def _sc_gather_gated_sum(table, idx, gates, g):
    W = table.shape[1]
    K, T = idx.shape
    per_w = T // SC_WORKERS
    nch = per_w // g
    assert per_w * SC_WORKERS == T and nch * g == per_w and nch % 2 == 0 and W % SC_LANES == 0
    idx_w = idx.reshape(K, SC_WORKERS, nch, g).transpose(1, 2, 0, 3).reshape(SC_WORKERS, nch * K, g)
    gates_w = gates.reshape(K, SC_WORKERS, nch, g).transpose(1, 2, 0, 3).reshape(SC_WORKERS, nch, K * g)
    def body(table_hbm, idx_hbm, gates_hbm, out_hbm, idx_v, rows_v, gts_v, out_v, sem0, sem1):
        wid = _sc_worker_id()
        base = wid * per_w
        pltpu.sync_copy(idx_hbm.at[wid], idx_v)

        def fetch(j, slot, sem):
            cps = []
            for kk in range(K):
                cps.append(pltpu.make_async_copy(table_hbm.at[idx_v.at[j * K + kk]],
                                                 rows_v.at[slot, kk], sem))
            cps.append(pltpu.make_async_copy(gates_hbm.at[wid, j], gts_v.at[slot], sem))
            return cps

        def start(j, slot, sem):
            for cp in fetch(j, slot, sem):
                cp.start()

        def finish(j, slot, sem):
            for cp in fetch(j, slot, sem):
                cp.wait()

            @pl.loop(0, g)
            def _(t):
                gk = [plsc.load_gather(gts_v.at[slot], [jnp.full((SC_LANES,), kk * g, jnp.int32) + t])
                      for kk in range(K)]

                @plsc.parallel_loop(0, W, SC_LANES, unroll=4)
                def _(off):
                    off = pl.multiple_of(off, SC_LANES)
                    acc_lo = jnp.zeros((SC_LANES,), F32)
                    acc_hi = jnp.zeros((SC_LANES,), F32)
                    for kk in range(K):
                        w = rows_v[slot, kk, t, pl.ds(off, SC_LANES)]
                        lo, hi = plsc.unpack(plsc.bitcast(w, BF16), format=plsc.PackFormat.INTERLEAVED)
                        acc_lo = acc_lo + gk[kk] * lo
                        acc_hi = acc_hi + gk[kk] * hi
                    out_v[slot, t, pl.ds(off, SC_LANES)] = acc_lo
                    out_v[slot, t, pl.ds(W + off, SC_LANES)] = acc_hi

            pltpu.sync_copy(out_v.at[slot], out_hbm.at[pl.ds(base + j * g, g)])

        start(0, 0, sem0)

        @pl.loop(0, nch // 2)
        def _(jj):
            j0 = 2 * jj
            start(j0 + 1, 1, sem1)
            finish(j0, 0, sem0)

            @pl.when(j0 + 2 < nch)
            def _():
                start(j0 + 2, 0, sem0)

            finish(j0 + 1, 1, sem1)

    return pl.kernel(
        body,
        out_type=jax.ShapeDtypeStruct((T, 2 * W), F32),
        mesh=_sc_mesh(),
        scratch_types=[
            pltpu.VMEM((nch * K, g), jnp.int32),
            pltpu.VMEM((2, K, g, W), jnp.int32),
            pltpu.VMEM((2, K * g), F32),
            pltpu.VMEM((2, g, 2 * W), F32),
            pltpu.SemaphoreType.DMA,
            pltpu.SemaphoreType.DMA,
        ],
        compiler_params=pltpu.CompilerParams(needs_layout_passes=False),
        name="sc_gather_gated_sum",
    )(table, idx_w, gates_w)


def _experts_kernel(be_ref, bv_ref, bf_ref, bn_ref, bs_ref, xs_ref, w1_hbm, b1_ref, w2_hbm, b2_ref,
                    ys_ref, w1s_ref, w2s_ref, w1b_ref, w2b_ref, sem):
    i = pl.program_id(0)
    nvalid = bv_ref[i]

    def weight_copies(e, slot):
        return (pltpu.make_async_copy(w1_hbm.at[e], w1s_ref.at[slot], sem.at[0, slot]),
                pltpu.make_async_copy(w2_hbm.at[e], w2s_ref.at[slot], sem.at[1, slot]))

    @pl.when(bf_ref[i] > 0)
    def _():
        slot = bs_ref[i]

        @pl.when(i == 0)
        def _():
            for cp in weight_copies(be_ref[0], slot):
                cp.start()

        @pl.when(bn_ref[i] >= 0)
        def _():
            for cp in weight_copies(bn_ref[i], 1 - slot):
                cp.start()

        for cp in weight_copies(be_ref[i], slot):
            cp.wait()
        w1b_ref[...] = w1s_ref[slot].astype(BF16)
        w2b_ref[...] = w2s_ref[slot].astype(BF16)

    def ffn(rows):
        xw = xs_ref[:rows, :]
        rowi = lax.broadcasted_iota(jnp.int32, (rows, 1), 0)
        lo, hi = _unpack_halves(jnp.where(rowi < nvalid, xw, 0))
        xb = jnp.concatenate([lo.astype(BF16), hi.astype(BF16)], axis=1)
        gu = _dot(xb, w1b_ref[...]) + b1_ref[0]
        glu = jnp.minimum(gu[:, :D_FF], SWIGLU_LIMIT)
        lin = jnp.clip(gu[:, D_FF:], -SWIGLU_LIMIT, SWIGLU_LIMIT)
        act = (lin + 1.0) * (glu * _sigmoid(glu, SWIGLU_ALPHA))
        ys_ref[:rows, :] = _pack_halves(_dot(act.astype(BF16), w2b_ref[...]) + b2_ref[0])

    step = xs_ref.shape[0] // ROW_PARTS
    for nq in range(1, ROW_PARTS + 1):
        pl.when((nvalid > (nq - 1) * step) & (nvalid <= nq * step))(functools.partial(ffn, nq * step))


def _experts(blk_e, blk_v, blk_f, blk_n, blk_s, blk_r, xs, w1, b1, w2, b2):
    P, W = xs.shape
    nb = P // MOE_BLOCK
    E, D, F2 = w1.shape
    grid_spec = pltpu.PrefetchScalarGridSpec(
        num_scalar_prefetch=6,
        grid=(nb,),
        in_specs=[
            pl.BlockSpec((MOE_BLOCK, W), lambda i, be, bv, bf, bn, bs, br: (br[i], 0)),
            pl.BlockSpec(memory_space=pl.ANY),
            pl.BlockSpec((1, 1, F2), lambda i, be, bv, bf, bn, bs, br: (be[i], 0, 0)),
            pl.BlockSpec(memory_space=pl.ANY),
            pl.BlockSpec((1, 1, D), lambda i, be, bv, bf, bn, bs, br: (be[i], 0, 0)),
        ],
        out_specs=pl.BlockSpec((MOE_BLOCK, D // 2), lambda i, be, bv, bf, bn, bs, br: (br[i], 0)),
        scratch_shapes=[
            pltpu.VMEM((2, D, F2), F32),
            pltpu.VMEM((2, D_FF, D), F32),
            pltpu.VMEM((D, F2), BF16),
            pltpu.VMEM((D_FF, D), BF16),
            pltpu.SemaphoreType.DMA((2, 2)),
        ],
    )

    def kern(be_ref, bv_ref, bf_ref, bn_ref, bs_ref, br_ref, *refs):
        del br_ref
        _experts_kernel(be_ref, bv_ref, bf_ref, bn_ref, bs_ref, *refs)

    return pl.pallas_call(
        kern,
        grid_spec=grid_spec,
        out_shape=jax.ShapeDtypeStruct((P, D // 2), jnp.int32),
        compiler_params=_cparams(("arbitrary",)),
        name="experts",
    )(blk_e, blk_v, blk_f, blk_n, blk_s, blk_r, xs, w1, b1.reshape(E, 1, F2), w2,
      b2.reshape(E, 1, D))


def _combine_kernel(x1_ref, gate_ref, ysum_ref, *rest):
    o_ref = rest[-1]
    o_ref[...] = x1_ref[...] + gate_ref[0] * ysum_ref[...]


def _combine(x1, mod3, ysum, out_prev, b0, B, nb_total, S, tm):
    D = x1.shape[1]
    ns = S // tm
    row = lambda b, s: (b * ns + s, 0)
    row_out = lambda b, s: ((b0 + b) * ns + s, 0)
    in_specs = [
        pl.BlockSpec((tm, D), row),
        pl.BlockSpec((1, 1, D), lambda b, s: ((b0 + b) * 6 + 5, 0, 0)),
        pl.BlockSpec((tm, D), row),
    ]
    args = [x1, mod3, ysum]
    aliases = {}
    if out_prev is not None:
        in_specs.append(pl.BlockSpec(memory_space=pl.ANY))
        args.append(out_prev)
        aliases = {len(args) - 1: 0}
    return pl.pallas_call(
        _combine_kernel,
        grid=(B, ns),
        in_specs=in_specs,
        out_specs=pl.BlockSpec((tm, D), row_out),
        out_shape=jax.ShapeDtypeStruct((nb_total * S, D), F32),
        input_output_aliases=aliases,
        compiler_params=_cparams(("arbitrary", "arbitrary")),
        name="combine",
    )(*args)


def _block_diag(w):
    n, c, d = w.shape
    eye = jnp.eye(n, dtype=w.dtype)
    return jnp.einsum("ncd,nm->ncmd", w, eye).reshape(n * c, n * d)


def _pad_heads(w, width):
    k = w.shape[0]
    w = w.reshape(k, N_HEADS, width)
    return jnp.pad(w, ((0, 0), (0, 0), (0, HEAD_PAD - width))).reshape(k, N_HEADS * HEAD_PAD)


def kernel(x, c, positions, w_ada, b_ada, g_mix, w_in, conv_w, conv_b, w_a, b_a, w_x, b_x, lam,
           g_q_lat, w_uq, g_kv_lat, w_ukv, g_qn, g_kn, w_out, g_ffn, w_router, b_router,
           w1, b1, w2, b2):
    B, S, D = x.shape
    T = B * S
    depth = w_ada.shape[0]
    tm_in = min(512, S)
    tk_att = min(512, S // 2)
    tm_out = min(1024, S)
    tm_comb = min(1024, S)
    n_groups = min(MAX_GROUPS, B)
    group_sizes = [B // n_groups + (1 if gi < B % n_groups else 0) for gi in range(n_groups)]

    o1 = 2 * D_LRU
    o2 = o1 + Q_LORA
    o3 = o2 + KV_LORA
    tri = jnp.asarray(np.arange(tm_out)[:, None] < np.arange(tm_out)[None, :], BF16)
    cos_t, sin_t = _rope_tables(positions)

    def rot_cols(w):
        k = w.shape[0]
        w3 = w.reshape(k, -1, HEAD_PAD)
        lo = w3[:, :, ROPE_LO:ROPE_LO + ROPE_HALF]
        hi = w3[:, :, ROPE_LO + ROPE_HALF:ROPE_LO + QK_ROPE]
        zl = jnp.zeros_like(w3[:, :, :ROPE_LO])
        zr = jnp.zeros_like(w3[:, :, ROPE_LO + QK_ROPE:])
        return jnp.concatenate([zl, hi, lo, zr], axis=2).reshape(w.shape)

    x2 = x.reshape(T, D)
    for l in range(depth):
        mod3 = _ada(c, w_ada[l], b_ada[l]).reshape(B * 6, 1, D)

        w_in_l = w_in[l]
        kr_cols = jnp.pad(w_in_l[:, o3:], ((0, 0), (ROPE_LO, LANES - ROPE_LO - QK_ROPE)))
        w_in_p = jnp.concatenate([w_in_l[:, :o3], kr_cols, rot_cols(kr_cols)], axis=1).astype(BF16)
        w_uq_h = _pad_heads(w_uq[l], QK_HEAD)
        w_uq_p = jnp.concatenate([w_uq_h, rot_cols(w_uq_h)], axis=1).astype(BF16)
        w_ukv_l = w_ukv[l].reshape(KV_LORA, N_HEADS, QK_NOPE + V_HEAD)
        w_uk_h = _pad_heads(w_ukv_l[:, :, :QK_NOPE].reshape(KV_LORA, N_HEADS * QK_NOPE), QK_NOPE)
        w_uv_pairs = w_ukv_l[:, :, QK_NOPE:].reshape(KV_LORA, N_HEADS // 2, 2, V_HEAD)
        zero_v = jnp.zeros_like(w_uv_pairs[:, :, 0])
        w_uv_h = jnp.stack([w_uv_pairs[:, :, 0], zero_v, zero_v, w_uv_pairs[:, :, 1]],
                           axis=2).reshape(KV_LORA, N_HEADS * HEAD_PAD)
        w_ukv_p = jnp.concatenate([w_uk_h, w_uv_h], axis=1).astype(BF16)
        gqn_p = jnp.pad(g_qn[l], (0, HEAD_PAD - QK_HEAD)).reshape(1, HEAD_PAD)
        gkn_p = jnp.pad(g_kn[l], (0, HEAD_PAD - QK_HEAD)).reshape(1, HEAD_PAD)

        lru_o, qp, kp, v = _inproj(
            x2, cos_t, sin_t, mod3, g_mix[l].reshape(1, D), w_in_p, g_q_lat[l].reshape(1, Q_LORA),
            w_uq_p, g_kv_lat[l].reshape(1, KV_LORA), w_ukv_p, gqn_p, rot_cols(gqn_p), gkn_p,
            rot_cols(gkn_p), conv_w[l], conv_b[l].reshape(1, D_LRU),
            _block_diag(w_a[l]).astype(BF16), b_a[l].reshape(1, D_LRU),
            _block_diag(w_x[l]).astype(BF16), b_x[l].reshape(1, D_LRU),
            lam[l].reshape(1, D_LRU), B, S, tm_in)

        att_o = _attn(qp, kp, v, B, S, tk_att)

        w_out_b = w_out[l].astype(BF16)
        wr_stack = jnp.concatenate(_split_bf16(w_router[l].T), axis=0)
        g_ffn_l = g_ffn[l].reshape(1, D)
        b_r = b_router[l].reshape(N_EXPERTS, 1)
        eio = np.arange(N_EXPERTS, dtype=np.int32)

        x_next = None
        b0 = 0
        for Bg in group_sizes:
            Tg = Bg * S
            n_blocks = -(-(Tg * TOP_K) // MOE_BLOCK) + N_EXPERTS
            bi = np.arange(n_blocks, dtype=np.int32)
            g_disp = min(64, Tg // SC_WORKERS // 2)
            g_comb = min(16, Tg // SC_WORKERS // 2)
            x1, h2p, idx_t, gat_t, rank_t, counts = _outproj(
                lru_o, att_o, x2, mod3, g_ffn_l, w_out_b[:D_LRU], w_out_b[D_LRU:],
                wr_stack, b_r, tri, b0, Bg, S, tm_out)

            counts = counts.reshape(N_EXPERTS)
            nblk_e = (counts + MOE_BLOCK - 1) // MOE_BLOCK
            blk_end = jnp.cumsum(nblk_e)
            pad_start = (blk_end - nblk_e) * MOE_BLOCK
            total = blk_end[-1]
            blk_r = jnp.minimum(bi, total - 1).astype(jnp.int32)
            blk_e = jnp.minimum(jnp.sum(blk_end[None, :] <= blk_r[:, None], axis=1),
                                N_EXPERTS - 1).astype(jnp.int32)
            blk_onehot = blk_e[:, None] == eio[None, :]
            blk_first = jnp.sum(jnp.where(blk_onehot, (blk_end - nblk_e)[None, :], 0), axis=1)
            blk_cnt = jnp.sum(jnp.where(blk_onehot, counts[None, :], 0), axis=1)
            blk_v = jnp.where(bi < total,
                              jnp.clip(blk_cnt - (bi - blk_first) * MOE_BLOCK, 0, MOE_BLOCK),
                              0).astype(jnp.int32)
            blk_f = ((bi == blk_first) & (bi < total)).astype(jnp.int32)
            nxt_first = jnp.sum(jnp.where(blk_onehot, blk_end[None, :], 0), axis=1)
            nxt_e = jnp.minimum(jnp.sum(blk_end[None, :] <= nxt_first[:, None], axis=1),
                                N_EXPERTS - 1)
            blk_n = jnp.where(nxt_first < total, nxt_e, -1).astype(jnp.int32)
            ordinal = jnp.cumsum((nblk_e > 0).astype(jnp.int32)) - 1
            blk_s = (jnp.sum(jnp.where(blk_onehot, ordinal[None, :], 0), axis=1) % 2).astype(jnp.int32)
            slot0 = jnp.sum(jnp.where(idx_t[None] == eio[:, None, None],
                                      pad_start[:, None, None], 0), axis=0)
            dest = slot0.astype(jnp.int32) + rank_t

            xs = _sc_scatter_rows(h2p, dest, n_blocks * MOE_BLOCK, g_disp)
            ys = _experts(blk_e, blk_v, blk_f, blk_n, blk_s, blk_r, xs, w1[l], b1[l], w2[l], b2[l])
            ysum = _sc_gather_gated_sum(ys, dest, gat_t, g_comb)
            x_next = _combine(x1, mod3, ysum, x_next, b0, Bg, B, S, tm_comb)
            b0 += Bg
        x2 = x_next
    return x2.reshape(B, S, D)
```

```python
import functools

import jax
import jax.numpy as jnp
import numpy as np
from jax import lax
from jax.experimental import pallas as pl
from jax.experimental.pallas import tpu as pltpu
from jax.experimental.pallas import tpu_sc as plsc

D_LRU = 512
CONV_W = 4
LRU_C = 8.0
N_HEADS = 8
QK_NOPE = 64
QK_ROPE = 32
QK_HEAD = 96
V_HEAD = 64
Q_LORA = 256
KV_LORA = 128
ROPE_THETA = 10000.0
N_EXPERTS = 32
TOP_K = 4
D_FF = 1024
SWIGLU_LIMIT = 7.0
SWIGLU_ALPHA = 1.702
MOE_BLOCK = 1024
EPS = 1e-6

LANES = 128
SUBLANES = 8
HEAD_PAD = 128
ROPE_LO = QK_NOPE
ROPE_HALF = QK_ROPE // 2
TOK_PER_ROW = LANES // ROPE_HALF
D_IN_PAD = 2 * D_LRU + Q_LORA + KV_LORA + 2 * LANES
LOG2_E = 1.4426950408889634
MAX_GROUPS = 2
ROW_PARTS = 8

VMEM_LIMIT = 56 * 1024 * 1024

F32 = jnp.float32
BF16 = jnp.bfloat16


def _cparams(sem):
    return pltpu.CompilerParams(dimension_semantics=sem, vmem_limit_bytes=VMEM_LIMIT)


def _dot(a, b):
    return jnp.dot(a, b, preferred_element_type=F32)


def _dot_nt(a, b):
    return lax.dot_general(a, b, (((1,), (1,)), ((), ())), preferred_element_type=F32)


def _split_bf16(a):
    hi = a.astype(BF16)
    lo = (a - hi.astype(F32)).astype(BF16)
    return hi, lo


def _sigmoid(x, scale=1.0):
    return 1.0 / (1.0 + jnp.exp2(x * (-scale * LOG2_E)))


def _pack_halves(x):
    bits = lax.bitcast_convert_type(x.astype(BF16).astype(F32), jnp.uint32)
    half = x.shape[1] // 2
    words = (bits[:, :half] >> 16) | (bits[:, half:] & jnp.uint32(0xFFFF0000))
    return lax.bitcast_convert_type(words, jnp.int32)


def _unpack_halves(words):
    w = lax.bitcast_convert_type(words, jnp.uint32)
    lo = lax.bitcast_convert_type(w << 16, F32)
    hi = lax.bitcast_convert_type(w & jnp.uint32(0xFFFF0000), F32)
    return lo, hi


def _ada_kernel(c_ref, w_ref, b_ref, o_ref):
    c = c_ref[...]
    s = c * _sigmoid(c)
    shi, slo = _split_bf16(s)
    whi, wlo = _split_bf16(w_ref[...])
    o_ref[...] = _dot(shi, whi) + _dot(slo, whi) + _dot(shi, wlo) + b_ref[...]


def _ada(c, w_ada, b_ada):
    B, D = c.shape
    N = w_ada.shape[1]
    tn = 1024
    return pl.pallas_call(
        _ada_kernel,
        grid=(N // tn,),
        in_specs=[
            pl.BlockSpec((B, D), lambda j: (0, 0)),
            pl.BlockSpec((D, tn), lambda j: (0, j)),
            pl.BlockSpec((1, tn), lambda j: (0, j)),
        ],
        out_specs=pl.BlockSpec((B, tn), lambda j: (0, j)),
        out_shape=jax.ShapeDtypeStruct((B, N), F32),
        compiler_params=_cparams(("arbitrary",)),
        name="ada",
    )(c, w_ada, b_ada.reshape(1, N))


def _trig_kernel(pos_ref, freq_ref, rsel_ref, fold_ref, cbase_ref, cos_ref, sin_ref):
    ang = pos_ref[...].astype(F32) * freq_ref[...]
    cs = jnp.concatenate([jnp.cos(ang), jnp.sin(ang)], axis=1)
    tm = cos_ref.shape[0]
    row = lax.broadcasted_iota(jnp.int32, (tm, 2 * LANES), 0)
    lane = lax.broadcasted_iota(jnp.int32, (tm, 2 * LANES), 1)
    own = ((lane % LANES) // ROPE_HALF) == (row % TOK_PER_ROW)
    rsel = rsel_ref[...]
    fold = fold_ref[...]
    by_row = sum(_dot(rsel, part) for part in _split_bf16(cs))
    mine = jnp.where(own, by_row, 0.0)
    out = sum(_dot(part, fold) for part in _split_bf16(mine))
    cos_ref[...] = out[:, :LANES] + cbase_ref[...]
    sin_ref[...] = out[:, LANES:]


def _rope_tables(positions):
    T = positions.size
    rows = T // TOK_PER_ROW
    pos_c = jnp.repeat(positions.reshape(T).astype(jnp.int32), ROPE_HALF).reshape(rows, LANES)
    tm = min(2048, T)
    tr = tm // TOK_PER_ROW
    freqs = np.float32(ROPE_THETA) ** (-np.arange(ROPE_HALF, dtype=np.float32) / np.float32(ROPE_HALF))
    freq_c = np.tile(freqs.astype(np.float32), TOK_PER_ROW).reshape(1, LANES)
    rsel = jnp.asarray(np.arange(tm)[:, None] // TOK_PER_ROW == np.arange(tr)[None, :], BF16)
    src = np.arange(LANES)[:, None] % ROPE_HALF
    dst = np.arange(LANES)[None, :]
    first = dst == ROPE_LO + src
    second = dst == ROPE_LO + ROPE_HALF + src
    fcos = (first | second).astype(np.float32)
    fsin = second.astype(np.float32) - first.astype(np.float32)
    zero = np.zeros((LANES, LANES), np.float32)
    fold = jnp.asarray(np.block([[fcos, zero], [zero, fsin]]), BF16)
    lane = np.arange(LANES)
    cbase = ((lane < ROPE_LO) | (lane >= ROPE_LO + QK_ROPE)).astype(np.float32).reshape(1, LANES)
    full = lambda i: (0, 0)
    return pl.pallas_call(
        _trig_kernel,
        grid=(T // tm,),
        in_specs=[
            pl.BlockSpec((tr, LANES), lambda i: (i, 0)),
            pl.BlockSpec((1, LANES), full),
            pl.BlockSpec((tm, tr), full),
            pl.BlockSpec((2 * LANES, 2 * LANES), full),
            pl.BlockSpec((1, LANES), full),
        ],
        out_specs=[pl.BlockSpec((tm, LANES), lambda i: (i, 0))] * 2,
        out_shape=[jax.ShapeDtypeStruct((T, LANES), F32)] * 2,
        compiler_params=_cparams(("arbitrary",)),
        name="rope_trig",
    )(pos_c, freq_c, rsel, fold, cbase)


def _inproj_kernel(x_ref, cos_ref, sin_ref, shift_ref, scale_ref, gmix_ref, win_ref, gq_ref, wuq_ref,
                   gkv_ref, wukv_ref, gqn_ref, gqr_ref, gkn_ref, gkr_ref,
                   cw_ref, cb_ref, wa_ref, ba_ref, wx_ref, bx_ref, lam_ref,
                   lru_ref, q_ref, k_ref, v_ref, tail_ref, carry_ref):
    HP = N_HEADS * HEAD_PAD

    @pl.when(pl.program_id(1) == 0)
    def _():
        tail_ref[...] = jnp.zeros_like(tail_ref)
        carry_ref[...] = jnp.zeros_like(carry_ref)

    x = x_ref[...]
    ms = jnp.mean(x * x, axis=-1, keepdims=True)
    gain = gmix_ref[...] * (1.0 + scale_ref[0])
    h = x * lax.rsqrt(ms + EPS) * gain + shift_ref[0]
    hb = h.astype(BF16)
    o1 = 2 * D_LRU
    o2 = Q_LORA
    o3 = o2 + KV_LORA
    z_lru = _dot(hb, win_ref[:, :o1])
    z = _dot(hb, win_ref[:, o1:])
    lru_ref[...] = _lru_tile(z_lru[:, :D_LRU], z_lru[:, D_LRU:], cw_ref, cb_ref, wa_ref, ba_ref,
                             wx_ref, bx_ref, lam_ref, tail_ref, carry_ref)
    ql = z[:, :o2]
    kvl = z[:, o2:o3]
    kr = z[:, o3:o3 + LANES]
    kr_rot = z[:, o3 + LANES:]

    qn = ql * lax.rsqrt(jnp.mean(ql * ql, axis=-1, keepdims=True) + EPS) * gq_ref[...]
    qq = _dot(qn.astype(BF16), wuq_ref[...])
    kvn = kvl * lax.rsqrt(jnp.mean(kvl * kvl, axis=-1, keepdims=True) + EPS) * gkv_ref[...]
    kv = _dot(kvn.astype(BF16), wukv_ref[...])

    tm = x.shape[0]
    lane = lax.broadcasted_iota(jnp.int32, (tm, HP), 1)
    pair_lane = lane & (2 * HEAD_PAD - 1)
    ones_cols = (pair_lane >= V_HEAD) & (pair_lane < 2 * HEAD_PAD - V_HEAD)
    v_ref[...] = jnp.where(ones_cols, 1.0, kv[:, HP:]).astype(BF16)

    cos_t = cos_ref[...]
    sin_t = sin_ref[...]
    gqn = gqn_ref[...]
    gkn = gkn_ref[...]
    cq = gqn * cos_t
    sq = gqr_ref[...] * sin_t
    kb = kr * (gkn * cos_t) + kr_rot * (gkr_ref[...] * sin_t)
    inv_w = 1.0 / QK_HEAD
    qscale = QK_HEAD ** -0.5 * LOG2_E
    for hh in range(N_HEADS):
        sl = slice(hh * HEAD_PAD, (hh + 1) * HEAD_PAD)
        qh = qq[:, sl]
        rq = lax.rsqrt(jnp.sum(qh * qh, axis=-1, keepdims=True) * inv_w + EPS) * qscale
        q_ref[:, sl] = ((qh * cq + qq[:, HP + hh * HEAD_PAD:HP + (hh + 1) * HEAD_PAD] * sq) * rq).astype(BF16)
        kraw = kv[:, sl] + kr
        rk = lax.rsqrt(jnp.sum(kraw * kraw, axis=-1, keepdims=True) * inv_w + EPS)
        k_ref[:, sl] = ((kv[:, sl] * gkn + kb) * rk).astype(BF16)


def _inproj(x2, cos_t, sin_t, mod3, g_mix, w_in_p, g_q_lat, w_uq_p, g_kv_lat, w_ukv_p,
            gqn_p, gqr_p, gkn_p, gkr_p, conv_w, conv_b, wa_d, b_a, wx_d, b_x, lam, B, S, tm):
    T, D = x2.shape
    ns = S // tm
    HP = N_HEADS * HEAD_PAD
    C = D_LRU
    row = lambda b, s: (b * ns + s, 0)
    full = lambda b, s: (0, 0)
    return pl.pallas_call(
        _inproj_kernel,
        grid=(B, ns),
        in_specs=[
            pl.BlockSpec((tm, D), row),
            pl.BlockSpec((tm, LANES), row),
            pl.BlockSpec((tm, LANES), row),
            pl.BlockSpec((1, 1, D), lambda b, s: (b * 6 + 0, 0, 0)),
            pl.BlockSpec((1, 1, D), lambda b, s: (b * 6 + 1, 0, 0)),
            pl.BlockSpec((1, D), full),
            pl.BlockSpec((D, D_IN_PAD), full),
            pl.BlockSpec((1, Q_LORA), full),
            pl.BlockSpec((Q_LORA, 2 * HP), full),
            pl.BlockSpec((1, KV_LORA), full),
            pl.BlockSpec((KV_LORA, 2 * HP), full),
            pl.BlockSpec((1, HEAD_PAD), full),
            pl.BlockSpec((1, HEAD_PAD), full),
            pl.BlockSpec((1, HEAD_PAD), full),
            pl.BlockSpec((1, HEAD_PAD), full),
            pl.BlockSpec((CONV_W, C), full),
            pl.BlockSpec((1, C), full),
            pl.BlockSpec((C, C), full),
            pl.BlockSpec((1, C), full),
            pl.BlockSpec((C, C), full),
            pl.BlockSpec((1, C), full),
            pl.BlockSpec((1, C), full),
        ],
        out_specs=[
            pl.BlockSpec((tm, C), row),
            pl.BlockSpec((tm, HP), row),
            pl.BlockSpec((tm, HP), row),
            pl.BlockSpec((tm, HP), row),
        ],
        out_shape=[
            jax.ShapeDtypeStruct((T, C), BF16),
            jax.ShapeDtypeStruct((T, HP), BF16),
            jax.ShapeDtypeStruct((T, HP), BF16),
            jax.ShapeDtypeStruct((T, HP), BF16),
        ],
        scratch_shapes=[pltpu.VMEM((SUBLANES, C), F32), pltpu.VMEM((SUBLANES, C), F32)],
        compiler_params=_cparams(("arbitrary", "arbitrary")),
        name="inproj",
    )(x2, cos_t, sin_t, mod3, mod3, g_mix, w_in_p, g_q_lat, w_uq_p, g_kv_lat, w_ukv_p,
      gqn_p, gqr_p, gkn_p, gkr_p, conv_w, conv_b, wa_d, b_a, wx_d, b_x, lam)


def _gelu_tanh(x):
    c = 0.7978845608028654
    hx = 0.5 * x
    return hx + hx * jnp.tanh(x * (c + (c * 0.044715) * (x * x)))


def _lru_tile(x, y, cw_ref, cb_ref, wa_ref, ba_ref, wx_ref, bx_ref, lam_ref, tail_ref, carry_ref):
    ts = x.shape[0]
    xext = jnp.concatenate([tail_ref[...], x], axis=0)
    cw = cw_ref[...]
    xc = x * cw[CONV_W - 1:CONV_W, :]
    for j in range(CONV_W - 1):
        sh = CONV_W - 1 - j
        xc = xc + xext[8 - sh:8 - sh + ts, :] * cw[j:j + 1, :]
    xc = xc + cb_ref[...]
    tail_ref[...] = x[ts - 8:, :]

    xb = xc.astype(BF16)
    r = _sigmoid(_dot(xb, wa_ref[...]) + ba_ref[...])
    i = _sigmoid(_dot(xb, wx_ref[...]) + bx_ref[...])
    lam = lam_ref[...]
    nl = -lam
    softplus = jnp.maximum(nl, 0.0) + jnp.log(1.0 + jnp.exp(-jnp.abs(nl)))
    log_a = (-LRU_C) * r * softplus
    a = jnp.exp(log_a)
    mult = jnp.sqrt(1.0 - a * a)
    u = mult * (i * xc)

    C = a.shape[1]
    a = a.reshape(ts // SUBLANES, SUBLANES, C)
    u = u.reshape(ts // SUBLANES, SUBLANES, C)
    sub = lax.broadcasted_iota(jnp.int32, (1, SUBLANES, 1), 1)
    sh = 1
    while sh < SUBLANES:
        a_prev = pltpu.roll(a, sh, axis=1)
        u_prev = pltpu.roll(u, sh, axis=1)
        m = sub >= sh
        u = jnp.where(m, a * u_prev + u, u)
        a = jnp.where(m, a * a_prev, a)
        sh *= 2
    a = a.reshape(ts, C)
    u = u.reshape(ts, C)
    h = carry_ref[0:1, :]
    groups = []
    for g0 in range(0, ts, SUBLANES):
        hg = u[g0:g0 + SUBLANES, :] + a[g0:g0 + SUBLANES, :] * h
        groups.append(hg)
        h = hg[SUBLANES - 1:SUBLANES, :]
    carry_ref[...] = jnp.broadcast_to(h, carry_ref.shape)
    hs = jnp.concatenate(groups, axis=0)
    return (_gelu_tanh(y) * hs).astype(BF16)


NEG_INF = -1e30


def _attn_kernel(q_ref, k_ref, v_ref, o_ref, *state, tq, tk):
    m_refs = state[:N_HEADS]
    acc_refs = state[N_HEADS:]
    qi = pl.program_id(1)
    causal = (lax.broadcasted_iota(jnp.int32, (tk, tk), 1)
              <= lax.broadcasted_iota(jnp.int32, (tk, tk), 0))
    lower = slice(tk, tq)

    def head_slice(hh):
        return slice(hh * HEAD_PAD, (hh + 1) * HEAD_PAD)

    def weights(sc, m_b):
        cols = [jnp.exp2(sc[:, c0:c0 + LANES] - m_b) for c0 in range(0, tk, LANES)]
        return jnp.concatenate(cols, axis=1).astype(BF16)

    def scores(hh, r0, rows):
        hs = head_slice(hh)
        return _dot_nt(q_ref[rows, hs], k_ref[pl.ds(r0, tk), hs])

    def update(hh, rows, sc, r0):
        hs = head_slice(hh)
        m_b = m_refs[hh][rows, :]
        m_new = jnp.maximum(m_b, jnp.max(sc, axis=-1, keepdims=True))
        alpha = jnp.exp2(m_b - m_new)
        m_refs[hh][rows, :] = m_new
        acc_refs[hh][rows, :] = (alpha * acc_refs[hh][rows, :]
                                 + _dot(weights(sc, m_new), v_ref[pl.ds(r0, tk), hs]))


    every = slice(0, tq)
    r_d0 = pl.multiple_of(qi * tq, tk)
    sc_next = scores(0, r_d0, every)
    for hh in range(N_HEADS):
        hs = head_slice(hh)
        sc = jnp.concatenate([jnp.where(causal, sc_next[:tk], NEG_INF), sc_next[tk:]], axis=0)
        if hh + 1 < N_HEADS:
            sc_next = scores(hh + 1, r_d0, every)
        m_b = jnp.broadcast_to(jnp.max(sc, axis=-1, keepdims=True), (tq, LANES))
        m_refs[hh][...] = m_b
        acc_refs[hh][...] = _dot(weights(sc, m_b), v_ref[pl.ds(r_d0, tk), hs])

    r_d1 = pl.multiple_of(qi * tq + tk, tk)
    sc_next = scores(0, r_d1, lower)
    for hh in range(N_HEADS):
        sc = jnp.where(causal, sc_next, NEG_INF)
        if hh + 1 < N_HEADS:
            sc_next = scores(hh + 1, r_d1, lower)
        update(hh, lower, sc, r_d1)

    @pl.loop(0, qi * (tq // tk))
    def _(j):
        r0 = pl.multiple_of(j * tk, tk)
        sc_next = scores(0, r0, every)
        for hh in range(N_HEADS):
            sc = sc_next
            if hh + 1 < N_HEADS:
                sc_next = scores(hh + 1, r0, every)
            update(hh, every, sc, r0)

    low = lax.broadcasted_iota(jnp.int32, (tq, HEAD_PAD), 1) < V_HEAD
    for he in range(0, N_HEADS, 2):
        acc_e = acc_refs[he][...]
        acc_o = acc_refs[he + 1][...]
        num = jnp.where(low, acc_e, acc_o)
        den = pltpu.roll(jnp.where(low, acc_o, acc_e), V_HEAD, axis=1)
        o_ref[:, he * V_HEAD:(he + 2) * V_HEAD] = (num / den).astype(BF16)


def _attn(qp, kp, v, B, S, tk):
    T = qp.shape[0]
    tq = 2 * tk
    nq = S // tq
    HP = N_HEADS * HEAD_PAD
    HV = N_HEADS * V_HEAD
    return pl.pallas_call(
        functools.partial(_attn_kernel, tq=tq, tk=tk),
        grid=(B, nq),
        in_specs=[
            pl.BlockSpec((tq, HP), lambda b, i: (b * nq + i, 0)),
            pl.BlockSpec((S, HP), lambda b, i: (b, 0)),
            pl.BlockSpec((S, HP), lambda b, i: (b, 0)),
        ],
        out_specs=pl.BlockSpec((tq, HV), lambda b, i: (b * nq + i, 0)),
        out_shape=jax.ShapeDtypeStruct((T, HV), BF16),
        scratch_shapes=([pltpu.VMEM((tq, LANES), F32)] * N_HEADS
                        + [pltpu.VMEM((tq, HEAD_PAD), F32)] * N_HEADS),
        compiler_params=_cparams(("arbitrary", "arbitrary")),
        name="attn",
    )(qp, kp, v)


def _outproj_kernel(lru_ref, att_ref, x_ref, gate_ref, shift_ref, scale_ref, gffn_ref,
                    wo1_ref, wo2_ref, wr_ref, br_ref, tri_ref,
                    x1_ref, h2p_ref, idx_ref, gat_ref, rank_ref, cnt_ref, run_ref):
    first = (pl.program_id(0) == 0) & (pl.program_id(1) == 0)

    @pl.when(first)
    def _():
        run_ref[...] = jnp.zeros_like(run_ref)

    mix = _dot(lru_ref[...], wo1_ref[...]) + _dot(att_ref[...], wo2_ref[...])
    x1 = x_ref[...] + gate_ref[0] * mix
    x1_ref[...] = x1
    ms = jnp.mean(x1 * x1, axis=-1, keepdims=True)
    gain = gffn_ref[...] * (1.0 + scale_ref[0])
    h2 = x1 * lax.rsqrt(ms + EPS) * gain + shift_ref[0]

    hhi = h2.astype(BF16)
    hlo = (h2 - hhi.astype(F32)).astype(BF16)
    h2p_ref[...] = _pack_halves(h2)

    ne = br_ref.shape[0]
    stacked = _dot_nt(wr_ref[...], hhi)
    logits = stacked[:ne] + stacked[ne:] + _dot_nt(wr_ref[:ne, :], hlo) + br_ref[...]

    tm = logits.shape[1]
    eio = lax.broadcasted_iota(jnp.int32, (ne, tm), 0)
    vals, idxs, sels = [], [], []
    l = logits
    for _ in range(TOP_K):
        m = jnp.max(l, axis=0, keepdims=True)
        idx = jnp.min(jnp.where(l == m, eio, ne), axis=0, keepdims=True)
        sel = eio == idx
        l = jnp.where(sel, -jnp.inf, l)
        vals.append(m)
        idxs.append(idx)
        sels.append(sel)
    es = [jnp.exp(v - vals[0]) for v in vals]
    inv = 1.0 / (es[0] + es[1] + es[2] + es[3])
    sel_any = jnp.where(sels[0] | sels[1] | sels[2] | sels[3], 1.0, 0.0)
    run = run_ref[...]
    excl = _dot(sel_any.astype(BF16), tri_ref[...]) + run
    for kk in range(TOP_K):
        idx_ref[kk:kk + 1, :] = idxs[kk]
        gat_ref[kk:kk + 1, :] = es[kk] * inv
        rk = jnp.sum(jnp.where(sels[kk], excl, 0.0), axis=0, keepdims=True)
        rank_ref[kk:kk + 1, :] = rk.astype(jnp.int32)
    run = run + jnp.sum(sel_any, axis=1, keepdims=True)
    run_ref[...] = run
    cnt_ref[...] = run.astype(jnp.int32)


def _outproj(lru_o, att_o, x2, mod3, g_ffn, wo1, wo2, wr_stack, b_r, tri, b0, B, S, tm):
    D = x2.shape[1]
    T = B * S
    ns = S // tm
    C = lru_o.shape[1]
    row_in = lambda b, s: ((b0 + b) * ns + s, 0)
    row = lambda b, s: (b * ns + s, 0)
    col = lambda b, s: (0, b * ns + s)
    full = lambda b, s: (0, 0)
    return pl.pallas_call(
        _outproj_kernel,
        grid=(B, ns),
        in_specs=[
            pl.BlockSpec((tm, C), row_in),
            pl.BlockSpec((tm, C), row_in),
            pl.BlockSpec((tm, D), row_in),
            pl.BlockSpec((1, 1, D), lambda b, s: ((b0 + b) * 6 + 2, 0, 0)),
            pl.BlockSpec((1, 1, D), lambda b, s: ((b0 + b) * 6 + 3, 0, 0)),
            pl.BlockSpec((1, 1, D), lambda b, s: ((b0 + b) * 6 + 4, 0, 0)),
            pl.BlockSpec((1, D), full),
            pl.BlockSpec((C, D), full),
            pl.BlockSpec((C, D), full),
            pl.BlockSpec((2 * N_EXPERTS, D), full),
            pl.BlockSpec((N_EXPERTS, 1), full),
            pl.BlockSpec(tri.shape, full),
        ],
        out_specs=[
            pl.BlockSpec((tm, D), row),
            pl.BlockSpec((tm, D // 2), row),
            pl.BlockSpec((TOP_K, tm), col),
            pl.BlockSpec((TOP_K, tm), col),
            pl.BlockSpec((TOP_K, tm), col),
            pl.BlockSpec((N_EXPERTS, 1), full),
        ],
        out_shape=[
            jax.ShapeDtypeStruct((T, D), F32),
            jax.ShapeDtypeStruct((T, D // 2), jnp.int32),
            jax.ShapeDtypeStruct((TOP_K, T), jnp.int32),
            jax.ShapeDtypeStruct((TOP_K, T), F32),
            jax.ShapeDtypeStruct((TOP_K, T), jnp.int32),
            jax.ShapeDtypeStruct((N_EXPERTS, 1), jnp.int32),
        ],
        scratch_shapes=[pltpu.VMEM((N_EXPERTS, 1), F32)],
        compiler_params=_cparams(("arbitrary", "arbitrary")),
        name="outproj",
    )(lru_o, att_o, x2, mod3, mod3, mod3, g_ffn, wo1, wo2, wr_stack, b_r, tri)


SC_CORES = 2
SC_SUBCORES = 16
SC_WORKERS = SC_CORES * SC_SUBCORES
SC_LANES = 16


def _sc_mesh():
    return plsc.VectorSubcoreMesh(core_axis_name="c", subcore_axis_name="s",
                                  num_cores=SC_CORES, num_subcores=SC_SUBCORES)


def _sc_worker_id():
    return lax.axis_index("s") * SC_CORES + lax.axis_index("c")


def _sc_scatter_rows(rows, idx, n_out, g):
    T, W = rows.shape
    K = idx.shape[0]
    per_w = T // SC_WORKERS
    nch = per_w // g
    assert per_w * SC_WORKERS == T and nch * g == per_w and nch % 2 == 0
    idx_w = idx.reshape(K, SC_WORKERS, nch, g).transpose(1, 2, 0, 3).reshape(SC_WORKERS, nch * K, g)

    def body(rows_hbm, idx_hbm, out_hbm, idx_v, buf0, buf1, semr0, semr1, semw):
        wid = _sc_worker_id()
        base = wid * per_w
        pltpu.sync_copy(idx_hbm.at[wid], idx_v)

        def read(j, buf, sem):
            return pltpu.make_async_copy(rows_hbm.at[pl.ds(base + j * g, g)], buf, sem)

        def scatter(j, buf):
            copies = [pltpu.async_copy(buf, out_hbm.at[idx_v.at[j * K + kk]], semw)
                      for kk in range(K)]
            for cp in copies:
                cp.wait()

        read(0, buf0, semr0).start()

        @pl.loop(0, nch // 2)
        def _(jj):
            j0 = 2 * jj
            read(j0 + 1, buf1, semr1).start()
            read(j0, buf0, semr0).wait()
            scatter(j0, buf0)

            @pl.when(j0 + 2 < nch)
            def _():
                read(j0 + 2, buf0, semr0).start()

            read(j0 + 1, buf1, semr1).wait()
            scatter(j0 + 1, buf1)

    return pl.kernel(
        body,
        out_type=jax.ShapeDtypeStruct((n_out, W), rows.dtype),
        mesh=_sc_mesh(),
        scratch_types=[
            pltpu.VMEM((nch * K, g), jnp.int32),
            pltpu.VMEM((g, W), rows.dtype),
            pltpu.VMEM((g, W), rows.dtype),
            pltpu.SemaphoreType.DMA,
            pltpu.SemaphoreType.DMA,
            pltpu.SemaphoreType.DMA,
        ],
        name="sc_scatter_rows",
    )(rows, idx_w)


def _sc_gather_gated_sum(table, idx, gates, g):
    W = table.shape[1]
    K, T = idx.shape
    per_w = T // SC_WORKERS
    nch = per_w // g
    assert per_w * SC_WORKERS == T and nch * g == per_w and nch % 2 == 0 and W % SC_LANES == 0
    idx_w = idx.reshape(K, SC_WORKERS, nch, g).transpose(1, 2, 0, 3).reshape(SC_WORKERS, nch * K, g)
    gates_w = gates.reshape(K, SC_WORKERS, nch, g).transpose(1, 2, 0, 3).reshape(SC_WORKERS, nch, K * g)
    def body(table_hbm, idx_hbm, gates_hbm, out_hbm, idx_v, rows_v, gts_v, out_v, sem0, sem1):
        wid = _sc_worker_id()
        base = wid * per_w
        pltpu.sync_copy(idx_hbm.at[wid], idx_v)

        def fetch(j, slot, sem):
            cps = []
            for kk in range(K):
                cps.append(pltpu.make_async_copy(table_hbm.at[idx_v.at[j * K + kk]],
                                                 rows_v.at[slot, kk], sem))
            cps.append(pltpu.make_async_copy(gates_hbm.at[wid, j], gts_v.at[slot], sem))
            return cps

        def start(j, slot, sem):
            for cp in fetch(j, slot, sem):
                cp.start()

        def finish(j, slot, sem):
            for cp in fetch(j, slot, sem):
                cp.wait()

            @pl.loop(0, g)
            def _(t):
                gk = [plsc.load_gather(gts_v.at[slot], [jnp.full((SC_LANES,), kk * g, jnp.int32) + t])
                      for kk in range(K)]

                @plsc.parallel_loop(0, W, SC_LANES, unroll=4)
                def _(off):
                    off = pl.multiple_of(off, SC_LANES)
                    acc_lo = jnp.zeros((SC_LANES,), F32)
                    acc_hi = jnp.zeros((SC_LANES,), F32)
                    for kk in range(K):
                        w = rows_v[slot, kk, t, pl.ds(off, SC_LANES)]
                        lo, hi = plsc.unpack(plsc.bitcast(w, BF16), format=plsc.PackFormat.INTERLEAVED)
                        acc_lo = acc_lo + gk[kk] * lo
                        acc_hi = acc_hi + gk[kk] * hi
                    out_v[slot, t, pl.ds(off, SC_LANES)] = acc_lo
                    out_v[slot, t, pl.ds(W + off, SC_LANES)] = acc_hi

            pltpu.sync_copy(out_v.at[slot], out_hbm.at[pl.ds(base + j * g, g)])

        start(0, 0, sem0)

        @pl.loop(0, nch // 2)
        def _(jj):
            j0 = 2 * jj
            start(j0 + 1, 1, sem1)
            finish(j0, 0, sem0)

            @pl.when(j0 + 2 < nch)
            def _():
                start(j0 + 2, 0, sem0)

            finish(j0 + 1, 1, sem1)

    return pl.kernel(
        body,
        out_type=jax.ShapeDtypeStruct((T, 2 * W), F32),
        mesh=_sc_mesh(),
        scratch_types=[
            pltpu.VMEM((nch * K, g), jnp.int32),
            pltpu.VMEM((2, K, g, W), jnp.int32),
            pltpu.VMEM((2, K * g), F32),
            pltpu.VMEM((2, g, 2 * W), F32),
            pltpu.SemaphoreType.DMA,
            pltpu.SemaphoreType.DMA,
        ],
        compiler_params=pltpu.CompilerParams(needs_layout_passes=False),
        name="sc_gather_gated_sum",
    )(table, idx_w, gates_w)


def _experts_kernel(be_ref, bv_ref, bf_ref, bn_ref, bs_ref, xs_ref, w1_hbm, b1_ref, w2_hbm, b2_ref,
                    ys_ref, w1s_ref, w2s_ref, w1b_ref, w2b_ref, sem):
    i = pl.program_id(0)
    nvalid = bv_ref[i]

    def weight_copies(e, slot):
        return (pltpu.make_async_copy(w1_hbm.at[e], w1s_ref.at[slot], sem.at[0, slot]),
                pltpu.make_async_copy(w2_hbm.at[e], w2s_ref.at[slot], sem.at[1, slot]))

    @pl.when(bf_ref[i] > 0)
    def _():
        slot = bs_ref[i]

        @pl.when(i == 0)
        def _():
            for cp in weight_copies(be_ref[0], slot):
                cp.start()

        @pl.when(bn_ref[i] >= 0)
        def _():
            for cp in weight_copies(bn_ref[i], 1 - slot):
                cp.start()

        cp1, cp2 = weight_copies(be_ref[i], slot)
        cp1.wait()
        w1b_ref[...] = w1s_ref[slot].astype(BF16)
        cp2.wait()
        w2b_ref[...] = w2s_ref[slot].astype(BF16)

    def ffn(rows):
        xw = xs_ref[:rows, :]
        rowi = lax.broadcasted_iota(jnp.int32, (rows, 1), 0)
        lo, hi = _unpack_halves(jnp.where(rowi < nvalid, xw, 0))
        xb = jnp.concatenate([lo.astype(BF16), hi.astype(BF16)], axis=1)
        gu = _dot(xb, w1b_ref[...]) + b1_ref[0]
        glu = jnp.minimum(gu[:, :D_FF], SWIGLU_LIMIT)
        lin = jnp.clip(gu[:, D_FF:], -SWIGLU_LIMIT, SWIGLU_LIMIT)
        act = (lin + 1.0) * (glu * _sigmoid(glu, SWIGLU_ALPHA))
        ys_ref[:rows, :] = _pack_halves(_dot(act.astype(BF16), w2b_ref[...]) + b2_ref[0])

    step = xs_ref.shape[0] // ROW_PARTS
    for nq in range(1, ROW_PARTS + 1):
        pl.when((nvalid > (nq - 1) * step) & (nvalid <= nq * step))(functools.partial(ffn, nq * step))


def _experts(blk_e, blk_v, blk_f, blk_n, blk_s, blk_r, xs, w1, b1, w2, b2):
    P, W = xs.shape
    nb = P // MOE_BLOCK
    E, D, F2 = w1.shape
    grid_spec = pltpu.PrefetchScalarGridSpec(
        num_scalar_prefetch=6,
        grid=(nb,),
        in_specs=[
            pl.BlockSpec((MOE_BLOCK, W), lambda i, be, bv, bf, bn, bs, br: (br[i], 0)),
            pl.BlockSpec(memory_space=pl.ANY),
            pl.BlockSpec((1, 1, F2), lambda i, be, bv, bf, bn, bs, br: (be[i], 0, 0)),
            pl.BlockSpec(memory_space=pl.ANY),
            pl.BlockSpec((1, 1, D), lambda i, be, bv, bf, bn, bs, br: (be[i], 0, 0)),
        ],
        out_specs=pl.BlockSpec((MOE_BLOCK, D // 2), lambda i, be, bv, bf, bn, bs, br: (br[i], 0)),
        scratch_shapes=[
            pltpu.VMEM((2, D, F2), F32),
            pltpu.VMEM((2, D_FF, D), F32),
            pltpu.VMEM((D, F2), BF16),
            pltpu.VMEM((D_FF, D), BF16),
            pltpu.SemaphoreType.DMA((2, 2)),
        ],
    )

    def kern(be_ref, bv_ref, bf_ref, bn_ref, bs_ref, br_ref, *refs):
        del br_ref
        _experts_kernel(be_ref, bv_ref, bf_ref, bn_ref, bs_ref, *refs)

    return pl.pallas_call(
        kern,
        grid_spec=grid_spec,
        out_shape=jax.ShapeDtypeStruct((P, D // 2), jnp.int32),
        compiler_params=_cparams(("arbitrary",)),
        name="experts",
    )(blk_e, blk_v, blk_f, blk_n, blk_s, blk_r, xs, w1, b1.reshape(E, 1, F2), w2,
      b2.reshape(E, 1, D))


def _combine_kernel(x1_ref, gate_ref, ysum_ref, *rest):
    o_ref = rest[-1]
    o_ref[...] = x1_ref[...] + gate_ref[0] * ysum_ref[...]


def _combine(x1, mod3, ysum, out_prev, b0, B, nb_total, S, tm):
    D = x1.shape[1]
    ns = S // tm
    row = lambda b, s: (b * ns + s, 0)
    row_out = lambda b, s: ((b0 + b) * ns + s, 0)
    in_specs = [
        pl.BlockSpec((tm, D), row),
        pl.BlockSpec((1, 1, D), lambda b, s: ((b0 + b) * 6 + 5, 0, 0)),
        pl.BlockSpec((tm, D), row),
    ]
    args = [x1, mod3, ysum]
    aliases = {}
    if out_prev is not None:
        in_specs.append(pl.BlockSpec(memory_space=pl.ANY))
        args.append(out_prev)
        aliases = {len(args) - 1: 0}
    return pl.pallas_call(
        _combine_kernel,
        grid=(B, ns),
        in_specs=in_specs,
        out_specs=pl.BlockSpec((tm, D), row_out),
        out_shape=jax.ShapeDtypeStruct((nb_total * S, D), F32),
        input_output_aliases=aliases,
        compiler_params=_cparams(("arbitrary", "arbitrary")),
        name="combine",
    )(*args)


def _block_diag(w):
    n, c, d = w.shape
    eye = jnp.eye(n, dtype=w.dtype)
    return jnp.einsum("ncd,nm->ncmd", w, eye).reshape(n * c, n * d)


def _pad_heads(w, width):
    k = w.shape[0]
    w = w.reshape(k, N_HEADS, width)
    return jnp.pad(w, ((0, 0), (0, 0), (0, HEAD_PAD - width))).reshape(k, N_HEADS * HEAD_PAD)


def kernel(x, c, positions, w_ada, b_ada, g_mix, w_in, conv_w, conv_b, w_a, b_a, w_x, b_x, lam,
           g_q_lat, w_uq, g_kv_lat, w_ukv, g_qn, g_kn, w_out, g_ffn, w_router, b_router,
           w1, b1, w2, b2):
    B, S, D = x.shape
    T = B * S
    depth = w_ada.shape[0]
    tm_in = min(512, S)
    tk_att = min(512, S // 2)
    tm_out = min(1024, S)
    tm_comb = min(1024, S)
    n_groups = min(MAX_GROUPS, B)
    group_sizes = [B // n_groups + (1 if gi < B % n_groups else 0) for gi in range(n_groups)]

    o1 = 2 * D_LRU
    o2 = o1 + Q_LORA
    o3 = o2 + KV_LORA
    tri = jnp.asarray(np.arange(tm_out)[:, None] < np.arange(tm_out)[None, :], BF16)
    cos_t, sin_t = _rope_tables(positions)

    def rot_cols(w):
        k = w.shape[0]
        w3 = w.reshape(k, -1, HEAD_PAD)
        lo = w3[:, :, ROPE_LO:ROPE_LO + ROPE_HALF]
        hi = w3[:, :, ROPE_LO + ROPE_HALF:ROPE_LO + QK_ROPE]
        zl = jnp.zeros_like(w3[:, :, :ROPE_LO])
        zr = jnp.zeros_like(w3[:, :, ROPE_LO + QK_ROPE:])
        return jnp.concatenate([zl, hi, lo, zr], axis=2).reshape(w.shape)

    x2 = x.reshape(T, D)
    for l in range(depth):
        mod3 = _ada(c, w_ada[l], b_ada[l]).reshape(B * 6, 1, D)

        w_in_l = w_in[l]
        kr_cols = jnp.pad(w_in_l[:, o3:], ((0, 0), (ROPE_LO, LANES - ROPE_LO - QK_ROPE)))
        w_in_p = jnp.concatenate([w_in_l[:, :o3], kr_cols, rot_cols(kr_cols)], axis=1).astype(BF16)
        w_uq_h = _pad_heads(w_uq[l], QK_HEAD)
        w_uq_p = jnp.concatenate([w_uq_h, rot_cols(w_uq_h)], axis=1).astype(BF16)
        w_ukv_l = w_ukv[l].reshape(KV_LORA, N_HEADS, QK_NOPE + V_HEAD)
        w_uk_h = _pad_heads(w_ukv_l[:, :, :QK_NOPE].reshape(KV_LORA, N_HEADS * QK_NOPE), QK_NOPE)
        w_uv_pairs = w_ukv_l[:, :, QK_NOPE:].reshape(KV_LORA, N_HEADS // 2, 2, V_HEAD)
        zero_v = jnp.zeros_like(w_uv_pairs[:, :, 0])
        w_uv_h = jnp.stack([w_uv_pairs[:, :, 0], zero_v, zero_v, w_uv_pairs[:, :, 1]],
                           axis=2).reshape(KV_LORA, N_HEADS * HEAD_PAD)
        w_ukv_p = jnp.concatenate([w_uk_h, w_uv_h], axis=1).astype(BF16)
        gqn_p = jnp.pad(g_qn[l], (0, HEAD_PAD - QK_HEAD)).reshape(1, HEAD_PAD)
        gkn_p = jnp.pad(g_kn[l], (0, HEAD_PAD - QK_HEAD)).reshape(1, HEAD_PAD)

        lru_o, qp, kp, v = _inproj(
            x2, cos_t, sin_t, mod3, g_mix[l].reshape(1, D), w_in_p, g_q_lat[l].reshape(1, Q_LORA),
            w_uq_p, g_kv_lat[l].reshape(1, KV_LORA), w_ukv_p, gqn_p, rot_cols(gqn_p), gkn_p,
            rot_cols(gkn_p), conv_w[l], conv_b[l].reshape(1, D_LRU),
            _block_diag(w_a[l]).astype(BF16), b_a[l].reshape(1, D_LRU),
            _block_diag(w_x[l]).astype(BF16), b_x[l].reshape(1, D_LRU),
            lam[l].reshape(1, D_LRU), B, S, tm_in)

        att_o = _attn(qp, kp, v, B, S, tk_att)

        w_out_b = w_out[l].astype(BF16)
        wr_stack = jnp.concatenate(_split_bf16(w_router[l].T), axis=0)
        g_ffn_l = g_ffn[l].reshape(1, D)
        b_r = b_router[l].reshape(N_EXPERTS, 1)
        eio = np.arange(N_EXPERTS, dtype=np.int32)

        x_next = None
        b0 = 0
        for Bg in group_sizes:
            Tg = Bg * S
            n_blocks = -(-(Tg * TOP_K) // MOE_BLOCK) + N_EXPERTS
            bi = np.arange(n_blocks, dtype=np.int32)
            g_disp = min(64, Tg // SC_WORKERS // 2)
            g_comb = min(16, Tg // SC_WORKERS // 2)
            x1, h2p, idx_t, gat_t, rank_t, counts = _outproj(
                lru_o, att_o, x2, mod3, g_ffn_l, w_out_b[:D_LRU], w_out_b[D_LRU:],
                wr_stack, b_r, tri, b0, Bg, S, tm_out)

            counts = counts.reshape(N_EXPERTS)
            nblk_e = (counts + MOE_BLOCK - 1) // MOE_BLOCK
            blk_end = jnp.cumsum(nblk_e)
            pad_start = (blk_end - nblk_e) * MOE_BLOCK
            total = blk_end[-1]
            blk_r = jnp.minimum(bi, total - 1).astype(jnp.int32)
            blk_e = jnp.minimum(jnp.sum(blk_end[None, :] <= blk_r[:, None], axis=1),
                                N_EXPERTS - 1).astype(jnp.int32)
            blk_onehot = blk_e[:, None] == eio[None, :]
            blk_first = jnp.sum(jnp.where(blk_onehot, (blk_end - nblk_e)[None, :], 0), axis=1)
            blk_cnt = jnp.sum(jnp.where(blk_onehot, counts[None, :], 0), axis=1)
            blk_v = jnp.where(bi < total,
                              jnp.clip(blk_cnt - (bi - blk_first) * MOE_BLOCK, 0, MOE_BLOCK),
                              0).astype(jnp.int32)
            blk_f = ((bi == blk_first) & (bi < total)).astype(jnp.int32)
            nxt_first = jnp.sum(jnp.where(blk_onehot, blk_end[None, :], 0), axis=1)
            nxt_e = jnp.minimum(jnp.sum(blk_end[None, :] <= nxt_first[:, None], axis=1),
                                N_EXPERTS - 1)
            blk_n = jnp.where(nxt_first < total, nxt_e, -1).astype(jnp.int32)
            ordinal = jnp.cumsum((nblk_e > 0).astype(jnp.int32)) - 1
            blk_s = (jnp.sum(jnp.where(blk_onehot, ordinal[None, :], 0), axis=1) % 2).astype(jnp.int32)
            slot0 = jnp.sum(jnp.where(idx_t[None] == eio[:, None, None],
                                      pad_start[:, None, None], 0), axis=0)
            dest = slot0.astype(jnp.int32) + rank_t

            xs = _sc_scatter_rows(h2p, dest, n_blocks * MOE_BLOCK, g_disp)
            ys = _experts(blk_e, blk_v, blk_f, blk_n, blk_s, blk_r, xs, w1[l], b1[l], w2[l], b2[l])
            ysum = _sc_gather_gated_sum(ys, dest, gat_t, g_comb)
            x_next = _combine(x1, mod3, ysum, x_next, b0, Bg, B, S, tm_comb)
            b0 += Bg
        x2 = x_next
    return x2.reshape(B, S, D)
```
